```python
import jax, jax.numpy as jnp
from jax import lax
import numpy as np

D_MODEL = 2048
BATCH = 4
SEQ = 2048
DEPTH = 1
DEC_BATCH = 8
DEC_SEQ = 4
PAST_LEN = 16384
PAGE_SIZE = 128

D_MIX = D_MODEL
D_ATTN = D_MIX // 2
D_CONV = D_MIX - D_ATTN
HEAD_DIM = 128
N_HEADS = D_ATTN // HEAD_DIM
N_KV_HEADS = 2
GROUP = N_HEADS // N_KV_HEADS
ROPE_DIM = HEAD_DIM // 4
ROPE_THETA = 500000.0
N_IDX_HEADS = 16
IDX_DIM = 64
IDX_ROPE_DIM = IDX_DIM // 4
TOPK_MAX = 256
CONV_W = 3
N_CONV_GROUPS = 8
Q_BLOCK = 128
EPS = 1e-6

SPLITS = (
    ("q", N_HEADS * HEAD_DIM),
    ("k", N_KV_HEADS * HEAD_DIM),
    ("v", N_KV_HEADS * HEAD_DIM),
    ("z_attn", D_ATTN),
    ("q_idx", N_IDX_HEADS * IDX_DIM),
    ("k_idx", IDX_DIM),
    ("w_idx", N_IDX_HEADS),
    ("h", D_CONV),
    ("b", D_CONV),
    ("c", D_CONV),
    ("z_conv", D_CONV),
)
D_IN = sum(n for _, n in SPLITS)

kernel_name = "hymba_dsa_shortconv_step"


def rmsnorm(x, g):
    xf = x.astype(jnp.float32)
    y = xf * lax.rsqrt(jnp.mean(xf * xf, axis=-1, keepdims=True) + EPS)
    return (y * g.astype(jnp.float32)).astype(x.dtype)


def partial_rope(x, pos, rot_dim):
    half = rot_dim // 2
    inv = ROPE_THETA ** (-jnp.arange(half, dtype=jnp.float32) / half)
    ang = pos.astype(jnp.float32)[:, None] * inv[None, :]
    cos = jnp.cos(ang)[:, None, :]
    sin = jnp.sin(ang)[:, None, :]
    xr = x[..., :rot_dim].astype(jnp.float32)
    x1, x2 = xr[..., :half], xr[..., half:]
    rot = jnp.concatenate([x1 * cos - x2 * sin, x2 * cos + x1 * sin], axis=-1)
    return jnp.concatenate([rot.astype(x.dtype), x[..., rot_dim:]], axis=-1)


def split_proj(p):
    offs = [0] + [int(v) for v in np.cumsum([n for _, n in SPLITS])]
    return {name: p[..., offs[i]:offs[i + 1]] for i, (name, _) in enumerate(SPLITS)}


take_rows = jax.vmap(lambda a, i: a[i])


def index_scores(qi, wi, ki, q_pos, k_pos):
    dots = jnp.einsum("bthd,bld->bthl", qi.astype(jnp.float32), ki.astype(jnp.float32))
    s = jnp.einsum("bth,bthl->btl", wi.astype(jnp.float32), jax.nn.relu(dots))
    return jnp.where(k_pos[None, None, :] <= q_pos[None, :, None], s, -jnp.inf)


def sparse_attend(q, k_sel, v_sel, valid):
    Bn, T = q.shape[:2]
    qg = q.reshape(Bn, T, N_KV_HEADS, GROUP, HEAD_DIM)
    logits = jnp.einsum("btkgd,btnkd->btkgn", qg, k_sel,
                        preferred_element_type=jnp.float32) * (HEAD_DIM ** -0.5)
    logits = jnp.where(valid[:, :, None, None, :], logits, -jnp.inf)
    probs = jax.nn.softmax(logits, axis=-1).astype(v_sel.dtype)
    o = jnp.einsum("btkgn,btnkd->btkgd", probs, v_sel)
    return o.reshape(Bn, T, N_HEADS * HEAD_DIM)


def prompt_sparse_attention(q, k, v, qi, ki, wi):
    Bn, S = q.shape[:2]
    topk = min(TOPK_MAX, S // 4)
    nb = S // Q_BLOCK
    k_pos = jnp.arange(S)

    def blk(args):
        qb, qib, wib, tb = args
        sc = index_scores(qib, wib, ki, tb, k_pos)
        _, idx = lax.top_k(sc, topk)
        valid = idx <= tb[None, :, None]
        return sparse_attend(qb, take_rows(k, idx), take_rows(v, idx), valid)

    def to_blocks(a):
        return a.reshape(Bn, nb, Q_BLOCK, *a.shape[2:]).swapaxes(0, 1)

    out = lax.map(blk, (to_blocks(q), to_blocks(qi), to_blocks(wi), k_pos.reshape(nb, Q_BLOCK)))
    return out.swapaxes(0, 1).reshape(Bn, S, D_ATTN)


def sample_sparse_attention(q, k_new, v_new, qi, ki_new, wi, cache_k, cache_v, cache_kidx, page_table):
    Bd, T = q.shape[:2]
    past = page_table.shape[1] * PAGE_SIZE
    L = past + T
    topk = min(TOPK_MAX, L // 4)
    ki_past = cache_kidx[page_table].reshape(Bd, past, IDX_DIM)
    ki_all = jnp.concatenate([ki_past, ki_new], axis=1)
    q_pos = past + jnp.arange(T)
    sc = index_scores(qi, wi, ki_all, q_pos, jnp.arange(L))
    _, idx = lax.top_k(sc, topk)
    valid = idx <= q_pos[None, :, None]
    in_past = idx < past
    pidx = jnp.minimum(idx, past - 1)
    phys = jax.vmap(lambda pt, i: pt[i])(page_table, pidx // PAGE_SIZE)
    off = pidx % PAGE_SIZE
    nidx = jnp.clip(idx - past, 0, T - 1)

    def sel(pool, new):
        return jnp.where(in_past[..., None, None], pool[phys, off], take_rows(new, nidx))

    return sparse_attend(q, sel(cache_k, k_new), sel(cache_v, v_new), valid)


def short_conv(u, buf, w_conv):
    T = u.shape[1]
    padded = jnp.concatenate([buf, u], axis=1)
    y = padded[:, 0:T] * w_conv[0]
    for j in range(1, CONV_W):
        y = y + padded[:, j:j + T] * w_conv[j]
    return y, padded[:, -(CONV_W - 1):]


def mixer_layer(x, pos, conv_buf, attn_fn, norm_in, w_in, g_q, g_k, g_kidx, w_conv, w_out):
    Bn, T, _ = x.shape
    p = split_proj(rmsnorm(x, norm_in) @ w_in)
    q = partial_rope(rmsnorm(p["q"].reshape(Bn, T, N_HEADS, HEAD_DIM), g_q), pos, ROPE_DIM)
    k = partial_rope(rmsnorm(p["k"].reshape(Bn, T, N_KV_HEADS, HEAD_DIM), g_k), pos, ROPE_DIM)
    v = p["v"].reshape(Bn, T, N_KV_HEADS, HEAD_DIM)
    qi = partial_rope(p["q_idx"].reshape(Bn, T, N_IDX_HEADS, IDX_DIM), pos, IDX_ROPE_DIM)
    ki = partial_rope(rmsnorm(p["k_idx"], g_kidx)[:, :, None, :], pos, IDX_ROPE_DIM)[:, :, 0]
    wi = p["w_idx"] * ((N_IDX_HEADS ** -0.5) * (IDX_DIM ** -0.5))
    attn = attn_fn(q, k, v, qi, ki, wi)
    conv, new_buf = short_conv(p["c"] * p["h"], conv_buf, w_conv)
    mix = jnp.concatenate([attn * jax.nn.silu(p["z_attn"]),
                           p["b"] * conv * jax.nn.silu(p["z_conv"])], axis=-1)
    return x + mix @ w_out, k, v, ki, new_buf


def setup_inputs(seed: int = 0) -> dict:
    key = jax.random.key(seed)
    ks = jax.random.split(key, 16)
    f32 = jnp.float32
    n_pages = PAST_LEN // PAGE_SIZE
    n_used = DEC_BATCH * n_pages
    n_pool = n_used + max(1, n_used // 4)

    def nrm(k, shape, s):
        return jax.random.normal(k, shape, f32) * s

    page_table = jax.random.permutation(ks[6], n_pool)[:n_used].reshape(DEC_BATCH, n_pages).astype(jnp.int32)
    return {
        "x_prompt": nrm(ks[0], (BATCH, SEQ, D_MODEL), 1.0),
        "x_sample": nrm(ks[1], (DEC_BATCH, DEC_SEQ, D_MODEL), 1.0),
        "cache_k": nrm(ks[2], (DEPTH, n_pool, PAGE_SIZE, N_KV_HEADS, HEAD_DIM), 1.0),
        "cache_v": nrm(ks[3], (DEPTH, n_pool, PAGE_SIZE, N_KV_HEADS, HEAD_DIM), 1.0),
        "cache_kidx": nrm(ks[4], (DEPTH, n_pool, PAGE_SIZE, IDX_DIM), 1.0),
        "state_conv": nrm(ks[5], (DEPTH, DEC_BATCH, CONV_W - 1, D_CONV), 1.0),
        "page_table": page_table,
        "norm_in": 1.0 + nrm(ks[7], (DEPTH, D_MODEL), 0.02),
        "w_in": nrm(ks[8], (DEPTH, D_MODEL, D_IN), D_MODEL ** -0.5),
        "g_q": 1.0 + nrm(ks[9], (DEPTH, HEAD_DIM), 0.02),
        "g_k": 1.0 + nrm(ks[10], (DEPTH, HEAD_DIM), 0.02),
        "g_kidx": 1.0 + nrm(ks[11], (DEPTH, IDX_DIM), 0.02),
        "w_conv": nrm(ks[12], (DEPTH, CONV_W, D_CONV), CONV_W ** -0.5),
        "w_out": nrm(ks[13], (DEPTH, D_MIX, D_MODEL), D_MIX ** -0.5),
    }


def reference(x_prompt, x_sample, cache_k, cache_v, cache_kidx, state_conv, page_table,
              norm_in, w_in, g_q, g_k, g_kidx, w_conv, w_out):
    Bp, S, _ = x_prompt.shape
    Bd, T, _ = x_sample.shape
    past = page_table.shape[1] * PAGE_SIZE
    pos_p = jnp.arange(S)
    pos_s = past + jnp.arange(T)
    hp, hs = x_prompt, x_sample
    kp, vp, kip, cp, ks_, vs, kis, cs = [], [], [], [], [], [], [], []
    for l in range(DEPTH):
        lw = (norm_in[l], w_in[l], g_q[l], g_k[l], g_kidx[l], w_conv[l], w_out[l])
        zero_buf = jnp.zeros((Bp, CONV_W - 1, D_CONV), x_prompt.dtype)
        hp, k1, v1, ki1, c1 = mixer_layer(hp, pos_p, zero_buf, prompt_sparse_attention, *lw)
        ck, cv, cki = cache_k[l], cache_v[l], cache_kidx[l]

        def attn_s(q, k, v, qi, ki, wi, ck=ck, cv=cv, cki=cki):
            return sample_sparse_attention(q, k, v, qi, ki, wi, ck, cv, cki, page_table)

        hs, k2, v2, ki2, c2 = mixer_layer(hs, pos_s, state_conv[l], attn_s, *lw)
        kp.append(k1); vp.append(v1); kip.append(ki1); cp.append(c1)
        ks_.append(k2); vs.append(v2); kis.append(ki2); cs.append(c2)
    return (hp, hs,
            jnp.stack(kp), jnp.stack(vp), jnp.stack(kip), jnp.stack(cp),
            jnp.stack(ks_), jnp.stack(vs), jnp.stack(kis), jnp.stack(cs))
```

```python
import functools

import jax
import jax.numpy as jnp
import numpy as np
from jax import lax
from jax.experimental import pallas as pl
from jax.experimental.pallas import tpu as pltpu

F32 = jnp.float32
BF16 = jnp.bfloat16
I32 = jnp.int32

D_MODEL = 2048
HEAD_DIM = 128
N_HEADS = 8
N_KV_HEADS = 2
GROUP = N_HEADS // N_KV_HEADS
D_ATTN = N_HEADS * HEAD_DIM
D_CONV = 1024
ROPE_DIM = HEAD_DIM // 4
ROPE_HALF = ROPE_DIM // 2
ROPE_THETA = 500000.0
N_IDX_HEADS = 16
IDX_DIM = 64
IDX_ROPE_HALF = IDX_DIM // 8
TOPK_MAX = 256
CONV_W = 3
PAGE_SIZE = 128
EPS = 1e-6
W_IDX_SCALE = (N_IDX_HEADS ** -0.5) * (IDX_DIM ** -0.5)
ATTN_SCALE = HEAD_DIM ** -0.5

INT_MIN = -(2 ** 31)
NEG = -1e30

VMEM_LIMIT_BYTES = 56 * 1024 * 1024

PROJ_TN = 512
CONV_CHUNK = 128
N_CONV_CHUNKS = D_CONV // CONV_CHUNK
J_Q = 0
J_QI = 2
J_VW = 4
N_T_STEPS = 5
J_KV = 5
J_Z = 6
J_CONV = 8
N_STEPS = J_CONV + N_CONV_CHUNKS
N_N_TILES = N_STEPS - N_T_STEPS

ATT_TQ = 256
ATT_TK = 256

S_PAGES_PER_STEP = 8


def _cparams(n_axes):
    return pltpu.CompilerParams(
        dimension_semantics=("arbitrary",) * n_axes,
        vmem_limit_bytes=VMEM_LIMIT_BYTES,
    )


def _rmsnorm_body(x_ref, g_ref, o_ref):
    x = x_ref[...]
    ms = jnp.mean(x * x, axis=-1, keepdims=True)
    o_ref[...] = (x * lax.rsqrt(ms + EPS) * g_ref[...]).astype(o_ref.dtype)


def _rmsnorm(x, g, tm):
    m = x.shape[0]
    return pl.pallas_call(
        _rmsnorm_body,
        grid=(m // tm,),
        in_specs=[pl.BlockSpec((tm, D_MODEL), lambda i: (i, 0)),
                  pl.BlockSpec((1, D_MODEL), lambda i: (0, 0))],
        out_specs=pl.BlockSpec((tm, D_MODEL), lambda i: (i, 0)),
        out_shape=jax.ShapeDtypeStruct((m, D_MODEL), BF16),
        compiler_params=_cparams(1),
        name="rmsnorm_in",
    )(x, g.reshape(1, D_MODEL))


def _silu(x):
    return x * jax.nn.sigmoid(x)


def _proj_body(sample, tps, tm, *refs):
    (xn_ref, wt_ref, wn_ref, wki_ref, gq_ref, gk_ref, gki_ref,
     c16_ref, s16_ref, c8_ref, s8_ref,
     kc_ref, ksa_ref, ksb_ref, kic_ref, kisa_ref, kisb_ref, wconv_ref) = refs[:18]
    refs = refs[18:]
    if sample:
        e1_ref, e2_ref = refs[:2]
        refs = refs[2:]
    (qT_ref, qiT_ref, vT_ref, wT_ref, k32_ref, kbf_ref, v32_ref, ki32_ref, kibf_ref,
     gate_ref, mixc_ref, u_ref) = refs[:12]
    refs = refs[12:]
    if not sample:
        (carry_ref,) = refs

    i = pl.program_id(0)
    j = pl.program_id(1)
    nt = (((1,), (1,)), ((), ()))

    if not sample:
        @pl.when((i == 0) & (j == 0))
        def _init():
            carry_ref[...] = jnp.zeros(carry_ref.shape, F32)

    @pl.when(j < J_QI)
    def _q():
        res = lax.dot_general(wt_ref[...], xn_ref[...], nt, preferred_element_type=F32)
        cos = c16_ref[...]
        sin = s16_ref[...]
        for hh in range(PROJ_TN // HEAD_DIM):
            blk = res[hh * HEAD_DIM:(hh + 1) * HEAD_DIM]
            ms = jnp.mean(blk * blk, axis=0, keepdims=True)
            y = blk * lax.rsqrt(ms + EPS) * gq_ref[...]
            x1 = y[0:ROPE_HALF]
            x2 = y[ROPE_HALF:ROPE_DIM]
            base = hh * HEAD_DIM
            qT_ref[base:base + ROPE_HALF, :] = (x1 * cos - x2 * sin).astype(BF16)
            qT_ref[base + ROPE_HALF:base + ROPE_DIM, :] = (x2 * cos + x1 * sin).astype(BF16)
            qT_ref[base + ROPE_DIM:base + HEAD_DIM, :] = y[ROPE_DIM:].astype(BF16)

    @pl.when((j >= J_QI) & (j < J_VW))
    def _qi():
        res = lax.dot_general(wt_ref[...], xn_ref[...], nt, preferred_element_type=F32)
        cos = c8_ref[...]
        sin = s8_ref[...]
        for hh in range(PROJ_TN // IDX_DIM):
            blk = res[hh * IDX_DIM:(hh + 1) * IDX_DIM]
            x1 = blk[0:IDX_ROPE_HALF]
            x2 = blk[IDX_ROPE_HALF:2 * IDX_ROPE_HALF]
            rot = jnp.concatenate([x1 * cos - x2 * sin, x2 * cos + x1 * sin], axis=0)
            base = hh * IDX_DIM
            qiT_ref[base:base + 2 * IDX_ROPE_HALF, :] = rot.astype(BF16)
            qiT_ref[base + 2 * IDX_ROPE_HALF:base + IDX_DIM, :] = blk[2 * IDX_ROPE_HALF:].astype(BF16)

    @pl.when(j == J_VW)
    def _vw():
        res = lax.dot_general(wt_ref[...], xn_ref[...], nt, preferred_element_type=F32)
        nv = N_KV_HEADS * HEAD_DIM
        vT_ref[...] = res[0:nv].astype(BF16)
        wT_ref[...] = res[nv:nv + N_IDX_HEADS] * W_IDX_SCALE

    @pl.when(j == J_KV)
    def _kv():
        xn = xn_ref[...]
        res = jnp.dot(xn, wn_ref[...], preferred_element_type=F32)
        nk = N_KV_HEADS * HEAD_DIM
        heads = []
        for hd in range(N_KV_HEADS):
            blk = res[:, hd * HEAD_DIM:(hd + 1) * HEAD_DIM]
            ms = jnp.mean(blk * blk, axis=-1, keepdims=True)
            heads.append(blk * lax.rsqrt(ms + EPS) * gk_ref[:, hd * HEAD_DIM:(hd + 1) * HEAD_DIM])
        yk = jnp.concatenate(heads, axis=1)
        rot = yk * kc_ref[...] + (pltpu.roll(yk, nk - ROPE_HALF, axis=1) * ksa_ref[...]
                                  + pltpu.roll(yk, ROPE_HALF, axis=1) * ksb_ref[...])
        k32_ref[...] = rot
        kbf_ref[...] = rot.astype(BF16)
        v32_ref[...] = res[:, nk:2 * nk]
        r2 = jnp.dot(xn, wki_ref[...], preferred_element_type=F32)
        ms = jnp.sum(r2 * r2, axis=-1, keepdims=True) * (1.0 / IDX_DIM)
        yi = r2 * lax.rsqrt(ms + EPS) * gki_ref[...]
        roti = yi * kic_ref[...] + (pltpu.roll(yi, 128 - IDX_ROPE_HALF, axis=1) * kisa_ref[...]
                                    + pltpu.roll(yi, IDX_ROPE_HALF, axis=1) * kisb_ref[...])
        ki32_ref[...] = roti[:, 0:IDX_DIM]
        kibf_ref[...] = roti.astype(BF16)

    @pl.when((j >= J_Z) & (j < J_CONV))
    def _z():
        res = jnp.dot(xn_ref[...], wn_ref[...], preferred_element_type=F32)
        gate_ref[...] = _silu(res)

    @pl.when(j >= J_CONV)
    def _conv():
        cc = j - J_CONV
        res = jnp.dot(xn_ref[...], wn_ref[...], preferred_element_type=F32)
        h = res[:, 0:CONV_CHUNK]
        b = res[:, CONV_CHUNK:2 * CONV_CHUNK]
        c = res[:, 2 * CONV_CHUNK:3 * CONV_CHUNK]
        zc = res[:, 3 * CONV_CHUNK:4 * CONV_CHUNK]
        u = c * h
        rowid = lax.broadcasted_iota(I32, (tm, CONV_CHUNK), 0)
        if sample:
            t = rowid & (sample - 1)
            u1 = jnp.where(t >= 1, pltpu.roll(u, 1, axis=0), e1_ref[...])
            u2 = jnp.where(t >= 2, pltpu.roll(u, 2, axis=0), e2_ref[...])
            u_ref[...] = u
        else:
            first = (i % tps) == 0
            prev = carry_ref[cc]
            p0 = jnp.where(first, 0.0, prev[0:1])
            p1 = jnp.where(first, 0.0, prev[1:2])
            u1 = jnp.where(rowid == 0, p1, pltpu.roll(u, 1, axis=0))
            u2 = jnp.where(rowid == 0, p0, jnp.where(rowid == 1, p1, pltpu.roll(u, 2, axis=0)))
            tail = u[tm - 8:tm]
            carry_ref[cc] = jnp.concatenate([tail[6:8], tail[0:6]], axis=0)
            u_ref[0] = tail[6:8]
        w = wconv_ref[...]
        y = u2 * w[0:1] + u1 * w[1:2] + u * w[2:3]
        mixc_ref[...] = (b * y * _silu(zc)).astype(BF16)


def _project(xn, wts, tabs, sample, state_rows=None):
    m = xn.shape[0]
    assert sample & (sample - 1) == 0
    tm = m if sample else 512
    n_i = m // tm
    tps = 1 if sample else (tabs["c16T"].shape[1] // tm)
    wt, wn, wki, gq, gk, gki, wconv = wts

    def tmap(i):
        return i % tps

    in_specs = [
        pl.BlockSpec((tm, D_MODEL), lambda i, j: (i, 0)),
        pl.BlockSpec((PROJ_TN, D_MODEL), lambda i, j: (jnp.minimum(j, N_T_STEPS - 1), 0)),
        pl.BlockSpec((D_MODEL, PROJ_TN), lambda i, j: (0, jnp.clip(j - N_T_STEPS, 0, N_N_TILES - 1))),
        pl.BlockSpec((D_MODEL, 128), lambda i, j: (0, 0)),
        pl.BlockSpec((HEAD_DIM, 1), lambda i, j: (0, 0)),
        pl.BlockSpec((1, 2 * HEAD_DIM), lambda i, j: (0, 0)),
        pl.BlockSpec((1, 128), lambda i, j: (0, 0)),
        pl.BlockSpec((ROPE_HALF, tm), lambda i, j: (0, tmap(i))),
        pl.BlockSpec((ROPE_HALF, tm), lambda i, j: (0, tmap(i))),
        pl.BlockSpec((IDX_ROPE_HALF, tm), lambda i, j: (0, tmap(i))),
        pl.BlockSpec((IDX_ROPE_HALF, tm), lambda i, j: (0, tmap(i))),
        pl.BlockSpec((tm, 256), lambda i, j: (tmap(i), 0)),
        pl.BlockSpec((tm, 256), lambda i, j: (tmap(i), 0)),
        pl.BlockSpec((tm, 256), lambda i, j: (tmap(i), 0)),
        pl.BlockSpec((tm, 128), lambda i, j: (tmap(i), 0)),
        pl.BlockSpec((tm, 128), lambda i, j: (tmap(i), 0)),
        pl.BlockSpec((tm, 128), lambda i, j: (tmap(i), 0)),
        pl.BlockSpec((CONV_W, CONV_CHUNK), lambda i, j: (0, jnp.clip(j - J_CONV, 0, N_CONV_CHUNKS - 1))),
    ]
    args = [xn, wt, wn, wki, gq, gk, gki,
            tabs["c16T"], tabs["s16T"], tabs["c8T"], tabs["s8T"],
            tabs["kC"], tabs["kSa"], tabs["kSb"], tabs["kiC"], tabs["kiSa"], tabs["kiSb"], wconv]
    cmap = lambda i, j: (i, jnp.clip(j - J_CONV, 0, N_CONV_CHUNKS - 1))
    if sample:
        in_specs += [pl.BlockSpec((tm, CONV_CHUNK), cmap), pl.BlockSpec((tm, CONV_CHUNK), cmap)]
        args += list(state_rows)
        u_spec = pl.BlockSpec((tm, CONV_CHUNK), cmap)
        u_shape = jax.ShapeDtypeStruct((m, D_CONV), F32)
        scratch = []
    else:
        u_spec = pl.BlockSpec((1, CONV_W - 1, CONV_CHUNK),
                              lambda i, j: (i, 0, jnp.clip(j - J_CONV, 0, N_CONV_CHUNKS - 1)))
        u_shape = jax.ShapeDtypeStruct((n_i, CONV_W - 1, D_CONV), F32)
        scratch = [pltpu.VMEM((N_CONV_CHUNKS, 8, CONV_CHUNK), F32)]

    nkv = N_KV_HEADS * HEAD_DIM
    out_specs = [
        pl.BlockSpec((PROJ_TN, tm), lambda i, j: (jnp.minimum(j, 1), i)),
        pl.BlockSpec((PROJ_TN, tm), lambda i, j: (jnp.clip(j - J_QI, 0, 1), i)),
        pl.BlockSpec((nkv, tm), lambda i, j: (0, i)),
        pl.BlockSpec((N_IDX_HEADS, tm), lambda i, j: (0, i)),
        pl.BlockSpec((tm, nkv), lambda i, j: (i, 0)),
        pl.BlockSpec((tm, nkv), lambda i, j: (i, 0)),
        pl.BlockSpec((tm, nkv), lambda i, j: (i, 0)),
        pl.BlockSpec((tm, IDX_DIM), lambda i, j: (i, 0)),
        pl.BlockSpec((tm, 128), lambda i, j: (i, 0)),
        pl.BlockSpec((tm, PROJ_TN), lambda i, j: (i, jnp.clip(j - J_Z, 0, 1))),
        pl.BlockSpec((tm, CONV_CHUNK), cmap),
        u_spec,
    ]
    out_shape = [
        jax.ShapeDtypeStruct((D_ATTN, m), BF16),
        jax.ShapeDtypeStruct((N_IDX_HEADS * IDX_DIM, m), BF16),
        jax.ShapeDtypeStruct((nkv, m), BF16),
        jax.ShapeDtypeStruct((N_IDX_HEADS, m), F32),
        jax.ShapeDtypeStruct((m, nkv), F32),
        jax.ShapeDtypeStruct((m, nkv), BF16),
        jax.ShapeDtypeStruct((m, nkv), F32),
        jax.ShapeDtypeStruct((m, IDX_DIM), F32),
        jax.ShapeDtypeStruct((m, 128), BF16),
        jax.ShapeDtypeStruct((m, D_ATTN), F32),
        jax.ShapeDtypeStruct((m, D_CONV), BF16),
        u_shape,
    ]
    return pl.pallas_call(
        functools.partial(_proj_body, sample, tps, tm),
        grid=(n_i, N_STEPS),
        in_specs=in_specs,
        out_specs=out_specs,
        out_shape=out_shape,
        scratch_shapes=scratch,
        compiler_params=_cparams(2),
        name="proj_sample" if sample else "proj_prompt",
    )(*args)


def _sort_key(s):
    bits = pltpu.bitcast(s, I32)
    return bits ^ ((bits >> 31) & 0x7FFFFFFF)


def _bit_value(b):
    return lax.shift_left(jnp.int32(1), jnp.int32(31) - b)


def _attn_prompt_body(qiT_ref, wT_ref, ki_ref, qT_ref, k_ref, vT_ref, gate_ref, o_ref,
                      keys_ref, bias_ref):
    tq, tk = ATT_TQ, ATT_TK
    i = pl.program_id(1)
    nch = i + 1
    w = wT_ref[...]
    row = lax.broadcasted_iota(I32, (tk, tq), 0)
    col = lax.broadcasted_iota(I32, (tk, tq), 1)

    def score_chunk(j, carry):
        off = pl.multiple_of(j * tk, tk)
        kic = ki_ref[pl.ds(off, tk), 0:IDX_DIM]
        acc = jnp.zeros((tk, tq), F32)
        for h in range(N_IDX_HEADS):
            d = jnp.dot(kic, qiT_ref[h * IDX_DIM:(h + 1) * IDX_DIM, :], preferred_element_type=F32)
            acc = acc + w[h:h + 1, :] * jnp.maximum(d, 0.0)
        key = _sort_key(acc)
        future = (row + j * tk) > (col + i * tq)
        keys_ref[pl.ds(off, tk), :] = jnp.where(future, INT_MIN, key)
        return carry

    lax.fori_loop(0, nch, score_chunk, 0)

    def bit_body(b, thr):
        cand = thr + _bit_value(b)

        def cnt_chunk(j, c):
            off = pl.multiple_of(j * tk, tk)
            kk = keys_ref[pl.ds(off, tk), :]
            ge = jnp.where(kk >= cand, 1.0, 0.0)
            return c + ge.reshape(tk // 8, 8, tq).sum(axis=0)

        c = lax.fori_loop(0, nch, cnt_chunk, jnp.zeros((8, tq), F32))
        cnt = jnp.sum(c, axis=0, keepdims=True)
        return jnp.where(cnt >= float(TOPK_MAX), cand, thr)

    thr = lax.fori_loop(0, 32, bit_body, jnp.full((1, tq), INT_MIN, I32))
    thr = jnp.maximum(thr, INT_MIN + 1)

    def bias_chunk(j, carry):
        off = pl.multiple_of(j * tk, tk)
        kk = keys_ref[pl.ds(off, tk), :]
        bias_ref[pl.ds(off, tk), :] = jnp.where(kk >= thr, 0.0, NEG)
        return carry

    lax.fori_loop(0, nch, bias_chunk, 0)

    for h in range(N_HEADS):
        g = h // GROUP
        qh = qT_ref[h * HEAD_DIM:(h + 1) * HEAD_DIM, :]

        def att_chunk(j, carry, g=g, qh=qh):
            m, l, acc = carry
            off = pl.multiple_of(j * tk, tk)
            kc = k_ref[pl.ds(off, tk), g * HEAD_DIM:(g + 1) * HEAD_DIM]
            s = jnp.dot(kc, qh, preferred_element_type=F32) * ATTN_SCALE + bias_ref[pl.ds(off, tk), :]
            m_new = jnp.maximum(m, jnp.max(s, axis=0, keepdims=True))
            alpha = jnp.exp(m - m_new)
            p = jnp.exp(s - m_new)
            l = l * alpha + jnp.sum(p, axis=0, keepdims=True)
            vc = vT_ref[g * HEAD_DIM:(g + 1) * HEAD_DIM, pl.ds(off, tk)]
            acc = acc * alpha + jnp.dot(vc, p.astype(BF16), preferred_element_type=F32)
            return m_new, l, acc

        init = (jnp.full((1, tq), NEG, F32), jnp.zeros((1, tq), F32), jnp.zeros((HEAD_DIM, tq), F32))
        m, l, acc = lax.fori_loop(0, nch, att_chunk, init)
        o = (acc / l).T
        sl = slice(h * HEAD_DIM, (h + 1) * HEAD_DIM)
        o_ref[:, sl] = (o * gate_ref[:, sl]).astype(BF16)


def _attn_prompt(qiT, wT, kibf, qT, kbf, vT, gate, n_batch, seq):
    m = n_batch * seq
    nq = seq // ATT_TQ
    nkv = N_KV_HEADS * HEAD_DIM
    qmap = lambda b, i: (0, b * nq + i)
    return pl.pallas_call(
        _attn_prompt_body,
        grid=(n_batch, nq),
        in_specs=[
            pl.BlockSpec((N_IDX_HEADS * IDX_DIM, ATT_TQ), qmap),
            pl.BlockSpec((N_IDX_HEADS, ATT_TQ), qmap),
            pl.BlockSpec((seq, 128), lambda b, i: (b, 0)),
            pl.BlockSpec((D_ATTN, ATT_TQ), qmap),
            pl.BlockSpec((seq, nkv), lambda b, i: (b, 0)),
            pl.BlockSpec((nkv, seq), lambda b, i: (0, b)),
            pl.BlockSpec((ATT_TQ, D_ATTN), lambda b, i: (b * nq + i, 0)),
        ],
        out_specs=pl.BlockSpec((ATT_TQ, D_ATTN), lambda b, i: (b * nq + i, 0)),
        out_shape=jax.ShapeDtypeStruct((m, D_ATTN), BF16),
        scratch_shapes=[pltpu.VMEM((seq, ATT_TQ), I32), pltpu.VMEM((seq, ATT_TQ), F32)],
        compiler_params=_cparams(2),
        name="attn_prompt",
    )(qiT, wT, kibf, qT, kbf, vT, gate)


def _s_score_body(pt_ref, qi_ref, wcol_ref, kinew_ref, *rest):
    npg = S_PAGES_PER_STEP
    pages = rest[:npg]
    out_ref, outnew_ref = rest[npg:]
    pg = pl.program_id(1)
    nt = (((1,), (1,)), ((), ()))
    qi = qi_ref[0]
    wc = wcol_ref[0]

    def scores(keys_bf):
        d = lax.dot_general(qi, keys_bf, nt, preferred_element_type=F32)
        val = jnp.maximum(d, 0.0) * wc
        return val.reshape(N_IDX_HEADS, 8, val.shape[-1]).sum(axis=0)

    for r in range(npg):
        out_ref[0, :, r * PAGE_SIZE:(r + 1) * PAGE_SIZE] = scores(pages[r][...].astype(BF16))

    @pl.when(pg == 0)
    def _new():
        outnew_ref[0] = scores(kinew_ref[0])


def _s_thresh_body(n_new, sp_ref, sn_ref, bp_ref, bn_ref, keys_ref):
    rows, past = sp_ref.shape
    ch = 2048
    nchunk = past // ch

    def to_keys(c, carry):
        off = pl.multiple_of(c * ch, ch)
        keys_ref[:, pl.ds(off, ch)] = _sort_key(sp_ref[:, pl.ds(off, ch)])
        return carry

    lax.fori_loop(0, nchunk, to_keys, 0)
    t = lax.broadcasted_iota(I32, (rows, 128), 0) & 7
    lane = lax.broadcasted_iota(I32, (rows, 128), 1)
    kn = jnp.where((lane < n_new) & (lane <= t), _sort_key(sn_ref[...]), INT_MIN)

    def fold(x):
        f = x[:, 0:128]
        for q in range(1, x.shape[1] // 128):
            f = f + x[:, q * 128:(q + 1) * 128]
        return f

    def bit_body(b, thr):
        cand = thr + _bit_value(b)

        def cnt_chunk(c, acc):
            off = pl.multiple_of(c * ch, ch)
            kk = keys_ref[:, pl.ds(off, ch)]
            return acc + fold(jnp.where(kk >= cand, 1.0, 0.0))

        acc = lax.fori_loop(0, nchunk, cnt_chunk, jnp.where(kn >= cand, 1.0, 0.0))
        cnt = jnp.sum(acc, axis=1, keepdims=True)
        return jnp.where(cnt >= float(TOPK_MAX), cand, thr)

    thr = lax.fori_loop(0, 32, bit_body, jnp.full((rows, 1), INT_MIN, I32))
    thr = jnp.maximum(thr, INT_MIN + 1)

    def to_bias(c, carry):
        off = pl.multiple_of(c * ch, ch)
        bp_ref[:, pl.ds(off, ch)] = jnp.where(keys_ref[:, pl.ds(off, ch)] >= thr, 0.0, NEG)
        return carry

    lax.fori_loop(0, nchunk, to_bias, 0)
    bn_ref[...] = jnp.where(kn >= thr, 0.0, NEG)


def _s_attn_body(pt_ref, q_ref, bp_ref, bn_ref, knew_ref, vnew_ref, *rest):
    npg = S_PAGES_PER_STEP
    kpages = rest[:npg]
    vpages = rest[npg:2 * npg]
    out_ref, m_ref, l_ref, acc_ref = rest[2 * npg:]
    pg = pl.program_id(1)
    nt = (((1,), (1,)), ((), ()))
    rg = GROUP * 8

    @pl.when(pg == 0)
    def _init():
        m_ref[...] = jnp.full(m_ref.shape, NEG, F32)
        l_ref[...] = jnp.zeros(l_ref.shape, F32)
        acc_ref[...] = jnp.zeros(acc_ref.shape, F32)

    def update(g, s, vg):
        rs = slice(g * rg, (g + 1) * rg)
        m_old = m_ref[rs, :]
        m_new = jnp.maximum(m_old, jnp.max(s, axis=1, keepdims=True))
        alpha = jnp.exp(m_old - m_new)
        p = jnp.exp(s - m_new[:, 0:1])
        f = p[:, 0:128]
        for q in range(1, p.shape[1] // 128):
            f = f + p[:, q * 128:(q + 1) * 128]
        l_ref[rs, :] = l_ref[rs, :] * alpha + f
        acc_ref[rs, :] = acc_ref[rs, :] * alpha + jnp.dot(p.astype(BF16), vg, preferred_element_type=F32)
        m_ref[rs, :] = m_new

    bias8 = bp_ref[0]
    bias = jnp.concatenate([bias8] * GROUP, axis=0)
    for g in range(N_KV_HEADS):
        qg = q_ref[0, g * rg:(g + 1) * rg, :]
        kg = jnp.concatenate([kp[pl.ds(g, PAGE_SIZE, stride=N_KV_HEADS), :] for kp in kpages], axis=0)
        vg = jnp.concatenate([vp[pl.ds(g, PAGE_SIZE, stride=N_KV_HEADS), :] for vp in vpages], axis=0)
        s = lax.dot_general(qg, kg.astype(BF16), nt, preferred_element_type=F32) * ATTN_SCALE + bias
        update(g, s, vg.astype(BF16))

    @pl.when(pg == pl.num_programs(1) - 1)
    def _last():
        biasn = jnp.concatenate([bn_ref[0]] * GROUP, axis=0)
        for g in range(N_KV_HEADS):
            qg = q_ref[0, g * rg:(g + 1) * rg, :]
            kg = knew_ref[0, :, g * HEAD_DIM:(g + 1) * HEAD_DIM]
            vg = vnew_ref[0, :, g * HEAD_DIM:(g + 1) * HEAD_DIM]
            s = lax.dot_general(qg, kg, nt, preferred_element_type=F32) * ATTN_SCALE + biasn
            update(g, s, vg)
        out_ref[0] = acc_ref[...] / jnp.sum(l_ref[...], axis=1, keepdims=True)


def _attn_sample(qT, qiT, wT, kibf, kbf, v32, cache_k, cache_v, cache_kidx, page_table, n_seq, n_tok):
    n_pages = page_table.shape[1]
    past = n_pages * PAGE_SIZE
    npg = S_PAGES_PER_STEP
    n_steps = n_pages // npg
    n_pool = cache_k.shape[0]

    def rows_ht(xT, n_heads, dim):
        x = xT.reshape(n_heads, dim, n_seq, n_tok).transpose(2, 0, 3, 1)
        x = jnp.pad(x, ((0, 0), (0, 0), (0, 8 - n_tok), (0, 0)))
        return x.reshape(n_seq, n_heads * 8, dim)

    qi_rows = rows_ht(qiT, N_IDX_HEADS, IDX_DIM)
    q_rows = rows_ht(qT, N_HEADS, HEAD_DIM)
    wcol = jnp.pad(wT.reshape(N_IDX_HEADS, n_seq, n_tok).transpose(1, 0, 2),
                   ((0, 0), (0, 0), (0, 8 - n_tok))).reshape(n_seq, N_IDX_HEADS * 8, 1)

    def pad_keys(x):
        x = x.reshape(n_seq, n_tok, x.shape[-1])
        return jnp.pad(x, ((0, 0), (0, PAGE_SIZE - n_tok), (0, 0)))

    ki_new = pad_keys(kibf[:, 0:IDX_DIM])
    k_new = pad_keys(kbf)
    v_new = pad_keys(v32.astype(BF16))

    def page_spec(shape, r):
        return pl.BlockSpec((None,) + shape, lambda b, pg, pt, r=r: (pt[b, pg * npg + r], 0, 0))

    sp, sn = pl.pallas_call(
        _s_score_body,
        grid_spec=pltpu.PrefetchScalarGridSpec(
            num_scalar_prefetch=1,
            grid=(n_seq, n_steps),
            in_specs=[
                pl.BlockSpec((1, N_IDX_HEADS * 8, IDX_DIM), lambda b, pg, pt: (b, 0, 0)),
                pl.BlockSpec((1, N_IDX_HEADS * 8, 1), lambda b, pg, pt: (b, 0, 0)),
                pl.BlockSpec((1, PAGE_SIZE, IDX_DIM), lambda b, pg, pt: (b, 0, 0)),
            ] + [page_spec((PAGE_SIZE, IDX_DIM), r) for r in range(npg)],
            out_specs=[
                pl.BlockSpec((1, 8, npg * PAGE_SIZE), lambda b, pg, pt: (b, 0, pg)),
                pl.BlockSpec((1, 8, PAGE_SIZE), lambda b, pg, pt: (b, 0, 0)),
            ],
        ),
        out_shape=[jax.ShapeDtypeStruct((n_seq, 8, past), F32),
                   jax.ShapeDtypeStruct((n_seq, 8, PAGE_SIZE), F32)],
        compiler_params=_cparams(2),
        name="sample_scores",
    )(page_table, qi_rows, wcol, ki_new, *([cache_kidx] * npg))

    rows = n_seq * 8
    bp, bn = pl.pallas_call(
        functools.partial(_s_thresh_body, n_tok),
        grid=(1,),
        in_specs=[pl.BlockSpec((rows, past), lambda i: (0, 0)),
                  pl.BlockSpec((rows, PAGE_SIZE), lambda i: (0, 0))],
        out_specs=[pl.BlockSpec((rows, past), lambda i: (0, 0)),
                   pl.BlockSpec((rows, PAGE_SIZE), lambda i: (0, 0))],
        out_shape=[jax.ShapeDtypeStruct((rows, past), F32),
                   jax.ShapeDtypeStruct((rows, PAGE_SIZE), F32)],
        scratch_shapes=[pltpu.VMEM((rows, past), I32)],
        compiler_params=_cparams(1),
        name="sample_topk_mask",
    )(sp.reshape(rows, past), sn.reshape(rows, PAGE_SIZE))
    bp = bp.reshape(n_seq, 8, past)
    bn = bn.reshape(n_seq, 8, PAGE_SIZE)

    nrow = N_HEADS * 8
    kv_rows = PAGE_SIZE * N_KV_HEADS
    ck = cache_k.reshape(n_pool, kv_rows, HEAD_DIM)
    cv = cache_v.reshape(n_pool, kv_rows, HEAD_DIM)
    out = pl.pallas_call(
        _s_attn_body,
        grid_spec=pltpu.PrefetchScalarGridSpec(
            num_scalar_prefetch=1,
            grid=(n_seq, n_steps),
            in_specs=[
                pl.BlockSpec((1, nrow, HEAD_DIM), lambda b, pg, pt: (b, 0, 0)),
                pl.BlockSpec((1, 8, npg * PAGE_SIZE), lambda b, pg, pt: (b, 0, pg)),
                pl.BlockSpec((1, 8, PAGE_SIZE), lambda b, pg, pt: (b, 0, 0)),
                pl.BlockSpec((1, PAGE_SIZE, N_KV_HEADS * HEAD_DIM), lambda b, pg, pt: (b, 0, 0)),
                pl.BlockSpec((1, PAGE_SIZE, N_KV_HEADS * HEAD_DIM), lambda b, pg, pt: (b, 0, 0)),
            ] + [page_spec((kv_rows, HEAD_DIM), r) for r in range(npg)] * 2,
            out_specs=pl.BlockSpec((1, nrow, HEAD_DIM), lambda b, pg, pt: (b, 0, 0)),
            scratch_shapes=[pltpu.VMEM((nrow, 128), F32), pltpu.VMEM((nrow, 128), F32),
                            pltpu.VMEM((nrow, HEAD_DIM), F32)],
        ),
        out_shape=jax.ShapeDtypeStruct((n_seq, nrow, HEAD_DIM), F32),
        compiler_params=_cparams(2),
        name="sample_attn",
    )(page_table, q_rows, bp, bn, k_new, v_new, *([ck] * npg), *([cv] * npg))
    out = out.reshape(n_seq, N_HEADS, 8, HEAD_DIM)[:, :, 0:n_tok]
    return out.transpose(0, 2, 1, 3).reshape(n_seq * n_tok, D_ATTN)


def _outproj_body(gated, *refs):
    if gated:
        x_ref, a_ref, gate_ref, mc_ref, wo_ref, o_ref = refs
        ma = (a_ref[...] * gate_ref[...]).astype(BF16)
    else:
        x_ref, a_ref, mc_ref, wo_ref, o_ref = refs
        ma = a_ref[...]
    acc = jnp.dot(ma, wo_ref[0:D_ATTN, :], preferred_element_type=F32)
    acc = acc + jnp.dot(mc_ref[...], wo_ref[D_ATTN:D_ATTN + D_CONV, :], preferred_element_type=F32)
    o_ref[...] = x_ref[...] + acc


def _outproj(x, attn, gate, mixc, wo, tm):
    m = x.shape[0]
    gated = gate is not None
    row = lambda i: (i, 0)
    in_specs = [pl.BlockSpec((tm, D_MODEL), row), pl.BlockSpec((tm, D_ATTN), row)]
    args = [x, attn]
    if gated:
        in_specs.append(pl.BlockSpec((tm, D_ATTN), row))
        args.append(gate)
    in_specs += [pl.BlockSpec((tm, D_CONV), row),
                 pl.BlockSpec((D_ATTN + D_CONV, D_MODEL), lambda i: (0, 0))]
    args += [mixc, wo]
    return pl.pallas_call(
        functools.partial(_outproj_body, gated),
        grid=(m // tm,),
        in_specs=in_specs,
        out_specs=pl.BlockSpec((tm, D_MODEL), row),
        out_shape=jax.ShapeDtypeStruct((m, D_MODEL), F32),
        compiler_params=_cparams(1),
        name="outproj_sample" if gated else "outproj_prompt",
    )(*args)


def _rope_tables(pos):
    posf = pos.astype(F32)[:, None]
    n = pos.shape[0]

    def cs(half):
        inv = ROPE_THETA ** (-jnp.arange(half, dtype=F32) / half)
        ang = posf * inv[None, :]
        return jnp.cos(ang), jnp.sin(ang)

    c16, s16 = cs(ROPE_HALF)
    c8, s8 = cs(IDX_ROPE_HALF)
    one = lambda w: jnp.ones((n, w), F32)
    zero = lambda w: jnp.zeros((n, w), F32)
    rest = HEAD_DIM - ROPE_DIM
    k_c = jnp.tile(jnp.concatenate([c16, c16, one(rest)], axis=1), (1, N_KV_HEADS))
    k_sa = jnp.tile(jnp.concatenate([-s16, zero(HEAD_DIM - ROPE_HALF)], axis=1), (1, N_KV_HEADS))
    k_sb = jnp.tile(jnp.concatenate([zero(ROPE_HALF), s16, zero(rest)], axis=1), (1, N_KV_HEADS))
    ki_c = jnp.concatenate([c8, c8, one(128 - 2 * IDX_ROPE_HALF)], axis=1)
    ki_sa = jnp.concatenate([-s8, zero(128 - IDX_ROPE_HALF)], axis=1)
    ki_sb = jnp.concatenate([zero(IDX_ROPE_HALF), s8, zero(128 - 2 * IDX_ROPE_HALF)], axis=1)
    return dict(c16T=c16.T, s16T=s16.T, c8T=c8.T, s8T=s8.T,
                kC=k_c, kSa=k_sa, kSb=k_sb, kiC=ki_c, kiSa=ki_sa, kiSb=ki_sb)


def _prep_weights(w_in, g_q, g_k, g_kidx, w_conv, w_out):
    sizes = [D_ATTN, N_KV_HEADS * HEAD_DIM, N_KV_HEADS * HEAD_DIM, D_ATTN, N_IDX_HEADS * IDX_DIM,
             IDX_DIM, N_IDX_HEADS, D_CONV, D_CONV, D_CONV, D_CONV]
    offs = np.concatenate([[0], np.cumsum(sizes)])
    sec = [w_in[:, int(offs[n]):int(offs[n + 1])].astype(BF16) for n in range(len(sizes))]
    wq, wk, wv, wz, wqi, wki, wwi, wh, wb, wc, wzc = sec
    pad_rows = PROJ_TN - wv.shape[1] - wwi.shape[1]
    wt = jnp.concatenate([wq.T, wqi.T, wv.T, wwi.T, jnp.zeros((pad_rows, D_MODEL), BF16)], axis=0)
    conv = jnp.stack([x.reshape(D_MODEL, N_CONV_CHUNKS, CONV_CHUNK) for x in (wh, wb, wc, wzc)], axis=2)
    wn = jnp.concatenate([wk, wv, wz, conv.reshape(D_MODEL, 4 * D_CONV)], axis=1)
    wki_p = jnp.concatenate([wki, jnp.zeros((D_MODEL, 128 - IDX_DIM), BF16)], axis=1)
    gq = g_q.reshape(HEAD_DIM, 1)
    gk = jnp.tile(g_k.reshape(1, HEAD_DIM), (1, N_KV_HEADS))
    gki = jnp.concatenate([g_kidx.reshape(1, IDX_DIM), jnp.zeros((1, 128 - IDX_DIM), F32)], axis=1)
    return (wt, wn, wki_p, gq, gk, gki, w_conv), w_out.astype(BF16)


def kernel(x_prompt, x_sample, cache_k, cache_v, cache_kidx, state_conv, page_table,
           norm_in, w_in, g_q, g_k, g_kidx, w_conv, w_out):
    n_b, seq, _ = x_prompt.shape
    n_s, n_t, _ = x_sample.shape
    depth = w_in.shape[0]
    past = page_table.shape[1] * PAGE_SIZE
    tabs_p = _rope_tables(jnp.arange(seq))
    tabs_s = _rope_tables(jnp.tile(past + jnp.arange(n_t), n_s))

    hp = x_prompt.reshape(n_b * seq, D_MODEL)
    hs = x_sample.reshape(n_s * n_t, D_MODEL)
    outs = [[] for _ in range(8)]
    for l in range(depth):
        wts, wo = _prep_weights(w_in[l], g_q[l], g_k[l], g_kidx[l], w_conv[l], w_out[l])

        xn = _rmsnorm(hp, norm_in[l], 512)
        (qT, qiT, vT, wT, k32, kbf, v32, ki32, kibf, gate, mixc, utail) = _project(xn, wts, tabs_p, 0)
        mixa = _attn_prompt(qiT, wT, kibf, qT, kbf, vT, gate, n_b, seq)
        hp = _outproj(hp, mixa, None, mixc, wo, 512)
        tps = seq // 512
        outs[0].append(k32.reshape(n_b, seq, N_KV_HEADS, HEAD_DIM))
        outs[1].append(v32.reshape(n_b, seq, N_KV_HEADS, HEAD_DIM))
        outs[2].append(ki32.reshape(n_b, seq, IDX_DIM))
        outs[3].append(utail[tps - 1::tps])

        st = state_conv[l]
        tok = jnp.arange(n_t)
        e1 = st[:, jnp.full((n_t,), CONV_W - 2)].reshape(n_s * n_t, D_CONV)
        e2 = st[:, jnp.minimum(tok, CONV_W - 2)].reshape(n_s * n_t, D_CONV)
        xn = _rmsnorm(hs, norm_in[l], n_s * n_t)
        (qT, qiT, vT, wT, k32, kbf, v32, ki32, kibf, gate, mixc, u) = _project(
            xn, wts, tabs_s, n_t, state_rows=(e1, e2))
        attn = _attn_sample(qT, qiT, wT, kibf, kbf, v32, cache_k[l], cache_v[l], cache_kidx[l],
                            page_table, n_s, n_t)
        hs = _outproj(hs, attn, gate, mixc, wo, n_s * n_t)
        outs[4].append(k32.reshape(n_s, n_t, N_KV_HEADS, HEAD_DIM))
        outs[5].append(v32.reshape(n_s, n_t, N_KV_HEADS, HEAD_DIM))
        outs[6].append(ki32.reshape(n_s, n_t, IDX_DIM))
        outs[7].append(u.reshape(n_s, n_t, D_CONV)[:, n_t - (CONV_W - 1):])

    return (hp.reshape(n_b, seq, D_MODEL), hs.reshape(n_s, n_t, D_MODEL),
            *[jnp.stack(o) for o in outs])
```

```python
import functools

import jax
import jax.numpy as jnp
import numpy as np
from jax import lax
from jax.experimental import pallas as pl
from jax.experimental.pallas import tpu as pltpu

F32 = jnp.float32
BF16 = jnp.bfloat16
I32 = jnp.int32

D_MODEL = 2048
HEAD_DIM = 128
N_HEADS = 8
N_KV_HEADS = 2
GROUP = N_HEADS // N_KV_HEADS
D_ATTN = N_HEADS * HEAD_DIM
D_CONV = 1024
ROPE_DIM = HEAD_DIM // 4
ROPE_HALF = ROPE_DIM // 2
ROPE_THETA = 500000.0
N_IDX_HEADS = 16
IDX_DIM = 64
IDX_ROPE_HALF = IDX_DIM // 8
TOPK_MAX = 256
CONV_W = 3
PAGE_SIZE = 128
EPS = 1e-6
W_IDX_SCALE = (N_IDX_HEADS ** -0.5) * (IDX_DIM ** -0.5)
ATTN_SCALE = HEAD_DIM ** -0.5
Q_PRESCALE = ATTN_SCALE * 1.4426950408889634

INT_MIN = -(2 ** 31)
NEG = -1e30

VMEM_LIMIT_BYTES = 56 * 1024 * 1024

PROJ_TN = 512
CONV_CHUNK = 128
N_CONV_CHUNKS = D_CONV // CONV_CHUNK
J_Q = 0
J_QI = 2
J_VW = 4
N_T_STEPS = 5
J_KV = 5
J_Z = 6
J_CONV = 8
N_STEPS = J_CONV + N_CONV_CHUNKS
N_N_TILES = N_STEPS - N_T_STEPS

ATT_TQ = 256
ATT_TK = 256
ATT_SUB = 64
CNT_ACCS = 4

S_PAGES_PER_STEP = 8


def _cparams(n_axes):
    return pltpu.CompilerParams(
        dimension_semantics=("arbitrary",) * n_axes,
        vmem_limit_bytes=VMEM_LIMIT_BYTES,
    )


def _rmsnorm_body(x_ref, g_ref, o_ref):
    x = x_ref[...]
    ms = jnp.mean(x * x, axis=-1, keepdims=True)
    o_ref[...] = (x * lax.rsqrt(ms + EPS) * g_ref[...]).astype(o_ref.dtype)


def _rmsnorm(x, g, tm):
    m = x.shape[0]
    return pl.pallas_call(
        _rmsnorm_body,
        grid=(m // tm,),
        in_specs=[pl.BlockSpec((tm, D_MODEL), lambda i: (i, 0)),
                  pl.BlockSpec((1, D_MODEL), lambda i: (0, 0))],
        out_specs=pl.BlockSpec((tm, D_MODEL), lambda i: (i, 0)),
        out_shape=jax.ShapeDtypeStruct((m, D_MODEL), BF16),
        compiler_params=_cparams(1),
        name="rmsnorm_in",
    )(x, g.reshape(1, D_MODEL))


def _silu(x):
    return x * jax.nn.sigmoid(x)


def _proj_body(sample, tps, tm, *refs):
    (xn_ref, wt_ref, wn_ref, wki_ref, gq_ref, gk_ref, gki_ref,
     c16_ref, s16_ref, c8_ref, s8_ref,
     kc_ref, ksa_ref, ksb_ref, kic_ref, kisa_ref, kisb_ref, wconv_ref) = refs[:18]
    refs = refs[18:]
    if sample:
        e1_ref, e2_ref = refs[:2]
        refs = refs[2:]
    (qT_ref, qiT_ref, vT_ref, wT_ref, k32_ref, kbf_ref, v32_ref, ki32_ref, kibf_ref,
     gate_ref, mixc_ref, u_ref) = refs[:12]
    refs = refs[12:]
    if not sample:
        (carry_ref,) = refs

    i = pl.program_id(0)
    j = pl.program_id(1)
    nt = (((1,), (1,)), ((), ()))

    if not sample:
        @pl.when((i == 0) & (j == 0))
        def _init():
            carry_ref[...] = jnp.zeros(carry_ref.shape, F32)

    @pl.when(j < J_QI)
    def _q():
        res = lax.dot_general(wt_ref[...], xn_ref[...], nt, preferred_element_type=F32)
        cos = c16_ref[...]
        sin = s16_ref[...]
        for hh in range(PROJ_TN // HEAD_DIM):
            blk = res[hh * HEAD_DIM:(hh + 1) * HEAD_DIM]
            ms = jnp.mean(blk * blk, axis=0, keepdims=True)
            y = blk * lax.rsqrt(ms + EPS) * gq_ref[...]
            x1 = y[0:ROPE_HALF]
            x2 = y[ROPE_HALF:ROPE_DIM]
            base = hh * HEAD_DIM
            qT_ref[base:base + ROPE_HALF, :] = ((x1 * cos - x2 * sin) * Q_PRESCALE).astype(BF16)
            qT_ref[base + ROPE_HALF:base + ROPE_DIM, :] = ((x2 * cos + x1 * sin) * Q_PRESCALE).astype(BF16)
            qT_ref[base + ROPE_DIM:base + HEAD_DIM, :] = (y[ROPE_DIM:] * Q_PRESCALE).astype(BF16)

    @pl.when((j >= J_QI) & (j < J_VW))
    def _qi():
        res = lax.dot_general(wt_ref[...], xn_ref[...], nt, preferred_element_type=F32)
        cos = c8_ref[...]
        sin = s8_ref[...]
        for hh in range(PROJ_TN // IDX_DIM):
            blk = res[hh * IDX_DIM:(hh + 1) * IDX_DIM]
            x1 = blk[0:IDX_ROPE_HALF]
            x2 = blk[IDX_ROPE_HALF:2 * IDX_ROPE_HALF]
            rot = jnp.concatenate([x1 * cos - x2 * sin, x2 * cos + x1 * sin], axis=0)
            base = hh * IDX_DIM
            qiT_ref[base:base + 2 * IDX_ROPE_HALF, :] = rot.astype(BF16)
            qiT_ref[base + 2 * IDX_ROPE_HALF:base + IDX_DIM, :] = blk[2 * IDX_ROPE_HALF:].astype(BF16)

    @pl.when(j == J_VW)
    def _vw():
        res = lax.dot_general(wt_ref[...], xn_ref[...], nt, preferred_element_type=F32)
        nv = N_KV_HEADS * HEAD_DIM
        vT_ref[...] = res[0:nv].astype(BF16)
        wT_ref[...] = res[nv:nv + N_IDX_HEADS] * W_IDX_SCALE

    @pl.when(j == J_KV)
    def _kv():
        xn = xn_ref[...]
        res = jnp.dot(xn, wn_ref[...], preferred_element_type=F32)
        nk = N_KV_HEADS * HEAD_DIM
        heads = []
        for hd in range(N_KV_HEADS):
            blk = res[:, hd * HEAD_DIM:(hd + 1) * HEAD_DIM]
            ms = jnp.mean(blk * blk, axis=-1, keepdims=True)
            heads.append(blk * lax.rsqrt(ms + EPS) * gk_ref[:, hd * HEAD_DIM:(hd + 1) * HEAD_DIM])
        yk = jnp.concatenate(heads, axis=1)
        rot = yk * kc_ref[...] + (pltpu.roll(yk, nk - ROPE_HALF, axis=1) * ksa_ref[...]
                                  + pltpu.roll(yk, ROPE_HALF, axis=1) * ksb_ref[...])
        k32_ref[...] = rot
        kbf_ref[...] = rot.astype(BF16)
        v32_ref[...] = res[:, nk:2 * nk]
        r2 = jnp.dot(xn, wki_ref[...], preferred_element_type=F32)
        ms = jnp.sum(r2 * r2, axis=-1, keepdims=True) * (1.0 / IDX_DIM)
        yi = r2 * lax.rsqrt(ms + EPS) * gki_ref[...]
        roti = yi * kic_ref[...] + (pltpu.roll(yi, 128 - IDX_ROPE_HALF, axis=1) * kisa_ref[...]
                                    + pltpu.roll(yi, IDX_ROPE_HALF, axis=1) * kisb_ref[...])
        ki32_ref[...] = roti[:, 0:IDX_DIM]
        kibf_ref[...] = roti.astype(BF16)

    @pl.when((j >= J_Z) & (j < J_CONV))
    def _z():
        res = jnp.dot(xn_ref[...], wn_ref[...], preferred_element_type=F32)
        gate_ref[...] = _silu(res)

    @pl.when(j >= J_CONV)
    def _conv():
        cc = j - J_CONV
        res = jnp.dot(xn_ref[...], wn_ref[...], preferred_element_type=F32)
        h = res[:, 0:CONV_CHUNK]
        b = res[:, CONV_CHUNK:2 * CONV_CHUNK]
        c = res[:, 2 * CONV_CHUNK:3 * CONV_CHUNK]
        zc = res[:, 3 * CONV_CHUNK:4 * CONV_CHUNK]
        u = c * h
        rowid = lax.broadcasted_iota(I32, (tm, CONV_CHUNK), 0)
        if sample:
            t = rowid & (sample - 1)
            u1 = jnp.where(t >= 1, pltpu.roll(u, 1, axis=0), e1_ref[...])
            u2 = jnp.where(t >= 2, pltpu.roll(u, 2, axis=0), e2_ref[...])
            u_ref[...] = u
        else:
            first = (i % tps) == 0
            prev = carry_ref[cc]
            p0 = jnp.where(first, 0.0, prev[0:1])
            p1 = jnp.where(first, 0.0, prev[1:2])
            u1 = jnp.where(rowid == 0, p1, pltpu.roll(u, 1, axis=0))
            u2 = jnp.where(rowid == 0, p0, jnp.where(rowid == 1, p1, pltpu.roll(u, 2, axis=0)))
            tail = u[tm - 8:tm]
            carry_ref[cc] = jnp.concatenate([tail[6:8], tail[0:6]], axis=0)
            u_ref[0] = tail[6:8]
        w = wconv_ref[...]
        y = u2 * w[0:1] + u1 * w[1:2] + u * w[2:3]
        mixc_ref[...] = (b * y * _silu(zc)).astype(BF16)


def _project(xn, wts, tabs, sample, state_rows=None):
    m = xn.shape[0]
    assert sample & (sample - 1) == 0
    tm = m if sample else 512
    n_i = m // tm
    tps = 1 if sample else (tabs["c16T"].shape[1] // tm)
    wt, wn, wki, gq, gk, gki, wconv = wts

    def tmap(i):
        return i % tps

    in_specs = [
        pl.BlockSpec((tm, D_MODEL), lambda i, j: (i, 0)),
        pl.BlockSpec((PROJ_TN, D_MODEL), lambda i, j: (jnp.minimum(j, N_T_STEPS - 1), 0)),
        pl.BlockSpec((D_MODEL, PROJ_TN), lambda i, j: (0, jnp.clip(j - N_T_STEPS, 0, N_N_TILES - 1))),
        pl.BlockSpec((D_MODEL, 128), lambda i, j: (0, 0)),
        pl.BlockSpec((HEAD_DIM, 1), lambda i, j: (0, 0)),
        pl.BlockSpec((1, 2 * HEAD_DIM), lambda i, j: (0, 0)),
        pl.BlockSpec((1, 128), lambda i, j: (0, 0)),
        pl.BlockSpec((ROPE_HALF, tm), lambda i, j: (0, tmap(i))),
        pl.BlockSpec((ROPE_HALF, tm), lambda i, j: (0, tmap(i))),
        pl.BlockSpec((IDX_ROPE_HALF, tm), lambda i, j: (0, tmap(i))),
        pl.BlockSpec((IDX_ROPE_HALF, tm), lambda i, j: (0, tmap(i))),
        pl.BlockSpec((tm, 256), lambda i, j: (tmap(i), 0)),
        pl.BlockSpec((tm, 256), lambda i, j: (tmap(i), 0)),
        pl.BlockSpec((tm, 256), lambda i, j: (tmap(i), 0)),
        pl.BlockSpec((tm, 128), lambda i, j: (tmap(i), 0)),
        pl.BlockSpec((tm, 128), lambda i, j: (tmap(i), 0)),
        pl.BlockSpec((tm, 128), lambda i, j: (tmap(i), 0)),
        pl.BlockSpec((CONV_W, CONV_CHUNK), lambda i, j: (0, jnp.clip(j - J_CONV, 0, N_CONV_CHUNKS - 1))),
    ]
    args = [xn, wt, wn, wki, gq, gk, gki,
            tabs["c16T"], tabs["s16T"], tabs["c8T"], tabs["s8T"],
            tabs["kC"], tabs["kSa"], tabs["kSb"], tabs["kiC"], tabs["kiSa"], tabs["kiSb"], wconv]
    cmap = lambda i, j: (i, jnp.clip(j - J_CONV, 0, N_CONV_CHUNKS - 1))
    if sample:
        in_specs += [pl.BlockSpec((tm, CONV_CHUNK), cmap), pl.BlockSpec((tm, CONV_CHUNK), cmap)]
        args += list(state_rows)
        u_spec = pl.BlockSpec((tm, CONV_CHUNK), cmap)
        u_shape = jax.ShapeDtypeStruct((m, D_CONV), F32)
        scratch = []
    else:
        u_spec = pl.BlockSpec((1, CONV_W - 1, CONV_CHUNK),
                              lambda i, j: (i, 0, jnp.clip(j - J_CONV, 0, N_CONV_CHUNKS - 1)))
        u_shape = jax.ShapeDtypeStruct((n_i, CONV_W - 1, D_CONV), F32)
        scratch = [pltpu.VMEM((N_CONV_CHUNKS, 8, CONV_CHUNK), F32)]

    nkv = N_KV_HEADS * HEAD_DIM
    out_specs = [
        pl.BlockSpec((PROJ_TN, tm), lambda i, j: (jnp.minimum(j, 1), i)),
        pl.BlockSpec((PROJ_TN, tm), lambda i, j: (jnp.clip(j - J_QI, 0, 1), i)),
        pl.BlockSpec((nkv, tm), lambda i, j: (0, i)),
        pl.BlockSpec((N_IDX_HEADS, tm), lambda i, j: (0, i)),
        pl.BlockSpec((tm, nkv), lambda i, j: (i, 0)),
        pl.BlockSpec((tm, nkv), lambda i, j: (i, 0)),
        pl.BlockSpec((tm, nkv), lambda i, j: (i, 0)),
        pl.BlockSpec((tm, IDX_DIM), lambda i, j: (i, 0)),
        pl.BlockSpec((tm, 128), lambda i, j: (i, 0)),
        pl.BlockSpec((tm, PROJ_TN), lambda i, j: (i, jnp.clip(j - J_Z, 0, 1))),
        pl.BlockSpec((tm, CONV_CHUNK), cmap),
        u_spec,
    ]
    out_shape = [
        jax.ShapeDtypeStruct((D_ATTN, m), BF16),
        jax.ShapeDtypeStruct((N_IDX_HEADS * IDX_DIM, m), BF16),
        jax.ShapeDtypeStruct((nkv, m), BF16),
        jax.ShapeDtypeStruct((N_IDX_HEADS, m), F32),
        jax.ShapeDtypeStruct((m, nkv), F32),
        jax.ShapeDtypeStruct((m, nkv), BF16),
        jax.ShapeDtypeStruct((m, nkv), F32),
        jax.ShapeDtypeStruct((m, IDX_DIM), F32),
        jax.ShapeDtypeStruct((m, 128), BF16),
        jax.ShapeDtypeStruct((m, D_ATTN), F32),
        jax.ShapeDtypeStruct((m, D_CONV), BF16),
        u_shape,
    ]
    return pl.pallas_call(
        functools.partial(_proj_body, sample, tps, tm),
        grid=(n_i, N_STEPS),
        in_specs=in_specs,
        out_specs=out_specs,
        out_shape=out_shape,
        scratch_shapes=scratch,
        compiler_params=_cparams(2),
        name="proj_sample" if sample else "proj_prompt",
    )(*args)


def _sort_key(s):
    bits = pltpu.bitcast(s, I32)
    return bits ^ ((bits >> 31) & 0x7FFFFFFF)


def _bit_value(b):
    return lax.shift_left(jnp.int32(1), jnp.int32(31) - b)


def _attn_prompt_body(qiT_ref, wT_ref, ki_ref, qT_ref, k_ref, vT_ref, gate_ref, o_ref,
                      keys_ref, bias_ref, m_ref, l_ref, acc_ref, s_ref, p_ref):
    tq, tk = ATT_TQ, ATT_TK
    i = pl.program_id(1)
    nch = i + 1
    w = wT_ref[...]
    row = lax.broadcasted_iota(I32, (tk, tq), 0)
    col = lax.broadcasted_iota(I32, (tk, tq), 1)

    def score_chunk(j, carry):
        off = pl.multiple_of(j * tk, tk)
        kic = ki_ref[pl.ds(off, tk), 0:IDX_DIM]
        acc = jnp.zeros((tk, tq), F32)
        for h in range(N_IDX_HEADS):
            d = jnp.dot(kic, qiT_ref[h * IDX_DIM:(h + 1) * IDX_DIM, :], preferred_element_type=F32)
            acc = acc + w[h:h + 1, :] * jnp.maximum(d, 0.0)
        key = _sort_key(acc)
        future = (row + j * tk) > (col + i * tq)
        keys_ref[pl.ds(off, tk), :] = jnp.where(future, INT_MIN, key)
        return carry

    lax.fori_loop(0, nch, score_chunk, 0)

    def bit_body(b, thr):
        cand = thr + _bit_value(b)

        def cnt_chunk(j, cs):
            off = pl.multiple_of(j * tk, tk)
            cs = list(cs)
            kc = keys_ref[pl.ds(off, tk), :]
            for r in range(tk // 8):
                a = cs[r % CNT_ACCS]
                cs[r % CNT_ACCS] = jnp.where(kc[r * 8:(r + 1) * 8] >= cand, a + 1, a)
            return tuple(cs)

        cs = lax.fori_loop(0, nch, cnt_chunk, (jnp.zeros((8, tq), I32),) * CNT_ACCS)
        c = cs[0]
        for a in cs[1:]:
            c = c + a
        cnt = jnp.sum(c.astype(F32), axis=0, keepdims=True)
        return jnp.where(cnt >= float(TOPK_MAX), cand, thr)

    thr = lax.fori_loop(0, 32, bit_body, jnp.full((1, tq), INT_MIN, I32))
    thr = jnp.maximum(thr, INT_MIN + 1)

    def bias_chunk(j, carry):
        off = pl.multiple_of(j * tk, tk)
        kk = keys_ref[pl.ds(off, tk), :]
        bias_ref[pl.ds(off, tk), :] = jnp.where(kk >= thr, 0.0, NEG)
        return carry

    lax.fori_loop(0, nch, bias_chunk, 0)

    m_ref[...] = jnp.full(m_ref.shape, NEG, F32)
    l_ref[...] = jnp.zeros(l_ref.shape, F32)
    acc_ref[...] = jnp.zeros(acc_ref.shape, F32)

    def att_chunk(j, carry):
        off = pl.multiple_of(j * tk, tk)
        bias = bias_ref[pl.ds(off, tk), :]
        for h in range(N_HEADS):
            g = h // GROUP
            kc = k_ref[pl.ds(off, tk), g * HEAD_DIM:(g + 1) * HEAD_DIM]
            s_ref[h * tk:(h + 1) * tk, :] = jnp.dot(
                kc, qT_ref[h * HEAD_DIM:(h + 1) * HEAD_DIM, :], preferred_element_type=F32) + bias
        alphas = []
        for h in range(N_HEADS):
            mloc = None
            for r in range(tk // ATT_SUB):
                blk = s_ref[h * tk + r * ATT_SUB:h * tk + (r + 1) * ATT_SUB, :]
                bm = blk.reshape(ATT_SUB // 8, 8, tq).max(axis=0)
                mloc = bm if mloc is None else jnp.maximum(mloc, bm)
            m_old = m_ref[h:h + 1, :]
            m_new = jnp.maximum(m_old, jnp.max(mloc, axis=0, keepdims=True))
            alpha = jnp.exp2(m_old - m_new)
            lsum = None
            for r in range(tk // ATT_SUB):
                rs = slice(h * tk + r * ATT_SUB, h * tk + (r + 1) * ATT_SUB)
                p = jnp.exp2(s_ref[rs, :] - m_new)
                ps = p.reshape(ATT_SUB // 8, 8, tq).sum(axis=0)
                lsum = ps if lsum is None else lsum + ps
                p_ref[rs, :] = p.astype(BF16)
            l_ref[h:h + 1, :] = l_ref[h:h + 1, :] * alpha + jnp.sum(lsum, axis=0, keepdims=True)
            m_ref[h:h + 1, :] = m_new
            alphas.append(alpha)
        for h in range(N_HEADS):
            g = h // GROUP
            hs = slice(h * HEAD_DIM, (h + 1) * HEAD_DIM)
            vc = vT_ref[g * HEAD_DIM:(g + 1) * HEAD_DIM, pl.ds(off, tk)]
            acc_ref[hs, :] = acc_ref[hs, :] * alphas[h] + jnp.dot(
                vc, p_ref[h * tk:(h + 1) * tk, :], preferred_element_type=F32)
        return carry

    lax.fori_loop(0, nch, att_chunk, 0)

    for h in range(N_HEADS):
        hs = slice(h * HEAD_DIM, (h + 1) * HEAD_DIM)
        o = (acc_ref[hs, :] / l_ref[h:h + 1, :]).T
        o_ref[:, hs] = (o * gate_ref[:, hs]).astype(BF16)


def _attn_prompt(qiT, wT, kibf, qT, kbf, vT, gate, n_batch, seq):
    m = n_batch * seq
    nq = seq // ATT_TQ
    nkv = N_KV_HEADS * HEAD_DIM
    qmap = lambda b, i: (0, b * nq + i)
    return pl.pallas_call(
        _attn_prompt_body,
        grid=(n_batch, nq),
        in_specs=[
            pl.BlockSpec((N_IDX_HEADS * IDX_DIM, ATT_TQ), qmap),
            pl.BlockSpec((N_IDX_HEADS, ATT_TQ), qmap),
            pl.BlockSpec((seq, 128), lambda b, i: (b, 0)),
            pl.BlockSpec((D_ATTN, ATT_TQ), qmap),
            pl.BlockSpec((seq, nkv), lambda b, i: (b, 0)),
            pl.BlockSpec((nkv, seq), lambda b, i: (0, b)),
            pl.BlockSpec((ATT_TQ, D_ATTN), lambda b, i: (b * nq + i, 0)),
        ],
        out_specs=pl.BlockSpec((ATT_TQ, D_ATTN), lambda b, i: (b * nq + i, 0)),
        out_shape=jax.ShapeDtypeStruct((m, D_ATTN), BF16),
        scratch_shapes=[pltpu.VMEM((seq, ATT_TQ), I32), pltpu.VMEM((seq, ATT_TQ), F32),
                        pltpu.VMEM((N_HEADS, ATT_TQ), F32), pltpu.VMEM((N_HEADS, ATT_TQ), F32),
                        pltpu.VMEM((D_ATTN, ATT_TQ), F32),
                        pltpu.VMEM((N_HEADS * ATT_TK, ATT_TQ), F32),
                        pltpu.VMEM((N_HEADS * ATT_TK, ATT_TQ), BF16)],
        compiler_params=_cparams(2),
        name="attn_prompt",
    )(qiT, wT, kibf, qT, kbf, vT, gate)


def _s_score_body(pt_ref, qi_ref, wcol_ref, kinew_ref, *rest):
    npg = S_PAGES_PER_STEP
    pages = rest[:npg]
    out_ref, outnew_ref = rest[npg:]
    pg = pl.program_id(1)
    nt = (((1,), (1,)), ((), ()))
    qi = qi_ref[0]
    wc = wcol_ref[0]

    def scores(keys_bf):
        d = lax.dot_general(qi, keys_bf, nt, preferred_element_type=F32)
        val = jnp.maximum(d, 0.0) * wc
        return val.reshape(N_IDX_HEADS, 8, val.shape[-1]).sum(axis=0)

    for r in range(npg):
        out_ref[0, :, r * PAGE_SIZE:(r + 1) * PAGE_SIZE] = scores(pages[r][...].astype(BF16))

    @pl.when(pg == 0)
    def _new():
        outnew_ref[0] = scores(kinew_ref[0])


def _s_thresh_body(n_new, sp_ref, sn_ref, bp_ref, bn_ref, keys_ref):
    rows, past = sp_ref.shape
    ch = 2048
    nchunk = past // ch

    def to_keys(c, carry):
        off = pl.multiple_of(c * ch, ch)
        keys_ref[:, pl.ds(off, ch)] = _sort_key(sp_ref[:, pl.ds(off, ch)])
        return carry

    lax.fori_loop(0, nchunk, to_keys, 0)
    t = lax.broadcasted_iota(I32, (rows, 128), 0) & 7
    lane = lax.broadcasted_iota(I32, (rows, 128), 1)
    kn = jnp.where((lane < n_new) & (lane <= t), _sort_key(sn_ref[...]), INT_MIN)

    def fold(x):
        f = x[:, 0:128]
        for q in range(1, x.shape[1] // 128):
            f = f + x[:, q * 128:(q + 1) * 128]
        return f

    def bit_body(b, thr):
        cand = thr + _bit_value(b)

        def cnt_chunk(c, acc):
            off = pl.multiple_of(c * ch, ch)
            kk = keys_ref[:, pl.ds(off, ch)]
            return acc + fold(jnp.where(kk >= cand, 1.0, 0.0))

        acc = lax.fori_loop(0, nchunk, cnt_chunk, jnp.where(kn >= cand, 1.0, 0.0))
        cnt = jnp.sum(acc, axis=1, keepdims=True)
        return jnp.where(cnt >= float(TOPK_MAX), cand, thr)

    thr = lax.fori_loop(0, 32, bit_body, jnp.full((rows, 1), INT_MIN, I32))
    thr = jnp.maximum(thr, INT_MIN + 1)

    def to_bias(c, carry):
        off = pl.multiple_of(c * ch, ch)
        bp_ref[:, pl.ds(off, ch)] = jnp.where(keys_ref[:, pl.ds(off, ch)] >= thr, 0.0, NEG)
        return carry

    lax.fori_loop(0, nchunk, to_bias, 0)
    bn_ref[...] = jnp.where(kn >= thr, 0.0, NEG)


def _s_attn_body(pt_ref, q_ref, bp_ref, bn_ref, knew_ref, vnew_ref, *rest):
    npg = S_PAGES_PER_STEP
    kpages = rest[:npg]
    vpages = rest[npg:2 * npg]
    out_ref, m_ref, l_ref, acc_ref = rest[2 * npg:]
    pg = pl.program_id(1)
    nt = (((1,), (1,)), ((), ()))
    rg = GROUP * 8

    @pl.when(pg == 0)
    def _init():
        m_ref[...] = jnp.full(m_ref.shape, NEG, F32)
        l_ref[...] = jnp.zeros(l_ref.shape, F32)
        acc_ref[...] = jnp.zeros(acc_ref.shape, F32)

    def update(g, s, vg):
        rs = slice(g * rg, (g + 1) * rg)
        m_old = m_ref[rs, :]
        m_new = jnp.maximum(m_old, jnp.max(s, axis=1, keepdims=True))
        alpha = jnp.exp2(m_old - m_new)
        p = jnp.exp2(s - m_new[:, 0:1])
        f = p[:, 0:128]
        for q in range(1, p.shape[1] // 128):
            f = f + p[:, q * 128:(q + 1) * 128]
        l_ref[rs, :] = l_ref[rs, :] * alpha + f
        acc_ref[rs, :] = acc_ref[rs, :] * alpha + jnp.dot(p.astype(BF16), vg, preferred_element_type=F32)
        m_ref[rs, :] = m_new

    bias8 = bp_ref[0]
    bias = jnp.concatenate([bias8] * GROUP, axis=0)
    for g in range(N_KV_HEADS):
        qg = q_ref[0, g * rg:(g + 1) * rg, :]
        kg = jnp.concatenate([kp[pl.ds(g, PAGE_SIZE, stride=N_KV_HEADS), :] for kp in kpages], axis=0)
        vg = jnp.concatenate([vp[pl.ds(g, PAGE_SIZE, stride=N_KV_HEADS), :] for vp in vpages], axis=0)
        s = lax.dot_general(qg, kg.astype(BF16), nt, preferred_element_type=F32) + bias
        update(g, s, vg.astype(BF16))

    @pl.when(pg == pl.num_programs(1) - 1)
    def _last():
        biasn = jnp.concatenate([bn_ref[0]] * GROUP, axis=0)
        for g in range(N_KV_HEADS):
            qg = q_ref[0, g * rg:(g + 1) * rg, :]
            kg = knew_ref[0, :, g * HEAD_DIM:(g + 1) * HEAD_DIM]
            vg = vnew_ref[0, :, g * HEAD_DIM:(g + 1) * HEAD_DIM]
            s = lax.dot_general(qg, kg, nt, preferred_element_type=F32) + biasn
            update(g, s, vg)
        out_ref[0] = acc_ref[...] / jnp.sum(l_ref[...], axis=1, keepdims=True)


def _attn_sample(qT, qiT, wT, kibf, kbf, v32, cache_k, cache_v, cache_kidx, page_table, n_seq, n_tok):
    n_pages = page_table.shape[1]
    past = n_pages * PAGE_SIZE
    npg = S_PAGES_PER_STEP
    n_steps = n_pages // npg
    n_pool = cache_k.shape[0]

    def rows_ht(xT, n_heads, dim):
        x = xT.reshape(n_heads, dim, n_seq, n_tok).transpose(2, 0, 3, 1)
        x = jnp.pad(x, ((0, 0), (0, 0), (0, 8 - n_tok), (0, 0)))
        return x.reshape(n_seq, n_heads * 8, dim)

    qi_rows = rows_ht(qiT, N_IDX_HEADS, IDX_DIM)
    q_rows = rows_ht(qT, N_HEADS, HEAD_DIM)
    wcol = jnp.pad(wT.reshape(N_IDX_HEADS, n_seq, n_tok).transpose(1, 0, 2),
                   ((0, 0), (0, 0), (0, 8 - n_tok))).reshape(n_seq, N_IDX_HEADS * 8, 1)

    def pad_keys(x):
        x = x.reshape(n_seq, n_tok, x.shape[-1])
        return jnp.pad(x, ((0, 0), (0, PAGE_SIZE - n_tok), (0, 0)))

    ki_new = pad_keys(kibf[:, 0:IDX_DIM])
    k_new = pad_keys(kbf)
    v_new = pad_keys(v32.astype(BF16))

    def page_spec(shape, r):
        return pl.BlockSpec((None,) + shape, lambda b, pg, pt, r=r: (pt[b, pg * npg + r], 0, 0))

    sp, sn = pl.pallas_call(
        _s_score_body,
        grid_spec=pltpu.PrefetchScalarGridSpec(
            num_scalar_prefetch=1,
            grid=(n_seq, n_steps),
            in_specs=[
                pl.BlockSpec((1, N_IDX_HEADS * 8, IDX_DIM), lambda b, pg, pt: (b, 0, 0)),
                pl.BlockSpec((1, N_IDX_HEADS * 8, 1), lambda b, pg, pt: (b, 0, 0)),
                pl.BlockSpec((1, PAGE_SIZE, IDX_DIM), lambda b, pg, pt: (b, 0, 0)),
            ] + [page_spec((PAGE_SIZE, IDX_DIM), r) for r in range(npg)],
            out_specs=[
                pl.BlockSpec((1, 8, npg * PAGE_SIZE), lambda b, pg, pt: (b, 0, pg)),
                pl.BlockSpec((1, 8, PAGE_SIZE), lambda b, pg, pt: (b, 0, 0)),
            ],
        ),
        out_shape=[jax.ShapeDtypeStruct((n_seq, 8, past), F32),
                   jax.ShapeDtypeStruct((n_seq, 8, PAGE_SIZE), F32)],
        compiler_params=_cparams(2),
        name="sample_scores",
    )(page_table, qi_rows, wcol, ki_new, *([cache_kidx] * npg))

    rows = n_seq * 8
    bp, bn = pl.pallas_call(
        functools.partial(_s_thresh_body, n_tok),
        grid=(1,),
        in_specs=[pl.BlockSpec((rows, past), lambda i: (0, 0)),
                  pl.BlockSpec((rows, PAGE_SIZE), lambda i: (0, 0))],
        out_specs=[pl.BlockSpec((rows, past), lambda i: (0, 0)),
                   pl.BlockSpec((rows, PAGE_SIZE), lambda i: (0, 0))],
        out_shape=[jax.ShapeDtypeStruct((rows, past), F32),
                   jax.ShapeDtypeStruct((rows, PAGE_SIZE), F32)],
        scratch_shapes=[pltpu.VMEM((rows, past), I32)],
        compiler_params=_cparams(1),
        name="sample_topk_mask",
    )(sp.reshape(rows, past), sn.reshape(rows, PAGE_SIZE))
    bp = bp.reshape(n_seq, 8, past)
    bn = bn.reshape(n_seq, 8, PAGE_SIZE)

    nrow = N_HEADS * 8
    kv_rows = PAGE_SIZE * N_KV_HEADS
    ck = cache_k.reshape(n_pool, kv_rows, HEAD_DIM)
    cv = cache_v.reshape(n_pool, kv_rows, HEAD_DIM)
    out = pl.pallas_call(
        _s_attn_body,
        grid_spec=pltpu.PrefetchScalarGridSpec(
            num_scalar_prefetch=1,
            grid=(n_seq, n_steps),
            in_specs=[
                pl.BlockSpec((1, nrow, HEAD_DIM), lambda b, pg, pt: (b, 0, 0)),
                pl.BlockSpec((1, 8, npg * PAGE_SIZE), lambda b, pg, pt: (b, 0, pg)),
                pl.BlockSpec((1, 8, PAGE_SIZE), lambda b, pg, pt: (b, 0, 0)),
                pl.BlockSpec((1, PAGE_SIZE, N_KV_HEADS * HEAD_DIM), lambda b, pg, pt: (b, 0, 0)),
                pl.BlockSpec((1, PAGE_SIZE, N_KV_HEADS * HEAD_DIM), lambda b, pg, pt: (b, 0, 0)),
            ] + [page_spec((kv_rows, HEAD_DIM), r) for r in range(npg)] * 2,
            out_specs=pl.BlockSpec((1, nrow, HEAD_DIM), lambda b, pg, pt: (b, 0, 0)),
            scratch_shapes=[pltpu.VMEM((nrow, 128), F32), pltpu.VMEM((nrow, 128), F32),
                            pltpu.VMEM((nrow, HEAD_DIM), F32)],
        ),
        out_shape=jax.ShapeDtypeStruct((n_seq, nrow, HEAD_DIM), F32),
        compiler_params=_cparams(2),
        name="sample_attn",
    )(page_table, q_rows, bp, bn, k_new, v_new, *([ck] * npg), *([cv] * npg))
    out = out.reshape(n_seq, N_HEADS, 8, HEAD_DIM)[:, :, 0:n_tok]
    return out.transpose(0, 2, 1, 3).reshape(n_seq * n_tok, D_ATTN)


def _outproj_body(gated, *refs):
    if gated:
        x_ref, a_ref, gate_ref, mc_ref, wo_ref, o_ref = refs
        ma = (a_ref[...] * gate_ref[...]).astype(BF16)
    else:
        x_ref, a_ref, mc_ref, wo_ref, o_ref = refs
        ma = a_ref[...]
    acc = jnp.dot(ma, wo_ref[0:D_ATTN, :], preferred_element_type=F32)
    acc = acc + jnp.dot(mc_ref[...], wo_ref[D_ATTN:D_ATTN + D_CONV, :], preferred_element_type=F32)
    o_ref[...] = x_ref[...] + acc


def _outproj(x, attn, gate, mixc, wo, tm):
    m = x.shape[0]
    gated = gate is not None
    row = lambda i: (i, 0)
    in_specs = [pl.BlockSpec((tm, D_MODEL), row), pl.BlockSpec((tm, D_ATTN), row)]
    args = [x, attn]
    if gated:
        in_specs.append(pl.BlockSpec((tm, D_ATTN), row))
        args.append(gate)
    in_specs += [pl.BlockSpec((tm, D_CONV), row),
                 pl.BlockSpec((D_ATTN + D_CONV, D_MODEL), lambda i: (0, 0))]
    args += [mixc, wo]
    return pl.pallas_call(
        functools.partial(_outproj_body, gated),
        grid=(m // tm,),
        in_specs=in_specs,
        out_specs=pl.BlockSpec((tm, D_MODEL), row),
        out_shape=jax.ShapeDtypeStruct((m, D_MODEL), F32),
        compiler_params=_cparams(1),
        name="outproj_sample" if gated else "outproj_prompt",
    )(*args)


def _rope_tables(pos):
    posf = pos.astype(F32)[:, None]
    n = pos.shape[0]

    def cs(half):
        inv = ROPE_THETA ** (-jnp.arange(half, dtype=F32) / half)
        ang = posf * inv[None, :]
        return jnp.cos(ang), jnp.sin(ang)

    c16, s16 = cs(ROPE_HALF)
    c8, s8 = cs(IDX_ROPE_HALF)
    one = lambda w: jnp.ones((n, w), F32)
    zero = lambda w: jnp.zeros((n, w), F32)
    rest = HEAD_DIM - ROPE_DIM
    k_c = jnp.tile(jnp.concatenate([c16, c16, one(rest)], axis=1), (1, N_KV_HEADS))
    k_sa = jnp.tile(jnp.concatenate([-s16, zero(HEAD_DIM - ROPE_HALF)], axis=1), (1, N_KV_HEADS))
    k_sb = jnp.tile(jnp.concatenate([zero(ROPE_HALF), s16, zero(rest)], axis=1), (1, N_KV_HEADS))
    ki_c = jnp.concatenate([c8, c8, one(128 - 2 * IDX_ROPE_HALF)], axis=1)
    ki_sa = jnp.concatenate([-s8, zero(128 - IDX_ROPE_HALF)], axis=1)
    ki_sb = jnp.concatenate([zero(IDX_ROPE_HALF), s8, zero(128 - 2 * IDX_ROPE_HALF)], axis=1)
    return dict(c16T=c16.T, s16T=s16.T, c8T=c8.T, s8T=s8.T,
                kC=k_c, kSa=k_sa, kSb=k_sb, kiC=ki_c, kiSa=ki_sa, kiSb=ki_sb)


def _prep_weights(w_in, g_q, g_k, g_kidx, w_conv, w_out):
    sizes = [D_ATTN, N_KV_HEADS * HEAD_DIM, N_KV_HEADS * HEAD_DIM, D_ATTN, N_IDX_HEADS * IDX_DIM,
             IDX_DIM, N_IDX_HEADS, D_CONV, D_CONV, D_CONV, D_CONV]
    offs = np.concatenate([[0], np.cumsum(sizes)])
    sec = [w_in[:, int(offs[n]):int(offs[n + 1])].astype(BF16) for n in range(len(sizes))]
    wq, wk, wv, wz, wqi, wki, wwi, wh, wb, wc, wzc = sec
    pad_rows = PROJ_TN - wv.shape[1] - wwi.shape[1]
    wt = jnp.concatenate([wq.T, wqi.T, wv.T, wwi.T, jnp.zeros((pad_rows, D_MODEL), BF16)], axis=0)
    conv = jnp.stack([x.reshape(D_MODEL, N_CONV_CHUNKS, CONV_CHUNK) for x in (wh, wb, wc, wzc)], axis=2)
    wn = jnp.concatenate([wk, wv, wz, conv.reshape(D_MODEL, 4 * D_CONV)], axis=1)
    wki_p = jnp.concatenate([wki, jnp.zeros((D_MODEL, 128 - IDX_DIM), BF16)], axis=1)
    gq = g_q.reshape(HEAD_DIM, 1)
    gk = jnp.tile(g_k.reshape(1, HEAD_DIM), (1, N_KV_HEADS))
    gki = jnp.concatenate([g_kidx.reshape(1, IDX_DIM), jnp.zeros((1, 128 - IDX_DIM), F32)], axis=1)
    return (wt, wn, wki_p, gq, gk, gki, w_conv), w_out.astype(BF16)


def kernel(x_prompt, x_sample, cache_k, cache_v, cache_kidx, state_conv, page_table,
           norm_in, w_in, g_q, g_k, g_kidx, w_conv, w_out):
    n_b, seq, _ = x_prompt.shape
    n_s, n_t, _ = x_sample.shape
    depth = w_in.shape[0]
    past = page_table.shape[1] * PAGE_SIZE
    tabs_p = _rope_tables(jnp.arange(seq))
    tabs_s = _rope_tables(jnp.tile(past + jnp.arange(n_t), n_s))

    hp = x_prompt.reshape(n_b * seq, D_MODEL)
    hs = x_sample.reshape(n_s * n_t, D_MODEL)
    outs = [[] for _ in range(8)]
    for l in range(depth):
        wts, wo = _prep_weights(w_in[l], g_q[l], g_k[l], g_kidx[l], w_conv[l], w_out[l])

        xn = _rmsnorm(hp, norm_in[l], 512)
        (qT, qiT, vT, wT, k32, kbf, v32, ki32, kibf, gate, mixc, utail) = _project(xn, wts, tabs_p, 0)
        mixa = _attn_prompt(qiT, wT, kibf, qT, kbf, vT, gate, n_b, seq)
        hp = _outproj(hp, mixa, None, mixc, wo, 512)
        tps = seq // 512
        outs[0].append(k32.reshape(n_b, seq, N_KV_HEADS, HEAD_DIM))
        outs[1].append(v32.reshape(n_b, seq, N_KV_HEADS, HEAD_DIM))
        outs[2].append(ki32.reshape(n_b, seq, IDX_DIM))
        outs[3].append(utail[tps - 1::tps])

        st = state_conv[l]
        tok = jnp.arange(n_t)
        e1 = st[:, jnp.full((n_t,), CONV_W - 2)].reshape(n_s * n_t, D_CONV)
        e2 = st[:, jnp.minimum(tok, CONV_W - 2)].reshape(n_s * n_t, D_CONV)
        xn = _rmsnorm(hs, norm_in[l], n_s * n_t)
        (qT, qiT, vT, wT, k32, kbf, v32, ki32, kibf, gate, mixc, u) = _project(
            xn, wts, tabs_s, n_t, state_rows=(e1, e2))
        attn = _attn_sample(qT, qiT, wT, kibf, kbf, v32, cache_k[l], cache_v[l], cache_kidx[l],
                            page_table, n_s, n_t)
        hs = _outproj(hs, attn, gate, mixc, wo, n_s * n_t)
        outs[4].append(k32.reshape(n_s, n_t, N_KV_HEADS, HEAD_DIM))
        outs[5].append(v32.reshape(n_s, n_t, N_KV_HEADS, HEAD_DIM))
        outs[6].append(ki32.reshape(n_s, n_t, IDX_DIM))
        outs[7].append(u.reshape(n_s, n_t, D_CONV)[:, n_t - (CONV_W - 1):])

    return (hp.reshape(n_b, seq, D_MODEL), hs.reshape(n_s, n_t, D_MODEL),
            *[jnp.stack(o) for o in outs])
```

```python
import functools

import jax
import jax.numpy as jnp
import numpy as np
from jax import lax
from jax.experimental import pallas as pl
from jax.experimental.pallas import tpu as pltpu

F32 = jnp.float32
BF16 = jnp.bfloat16
I32 = jnp.int32

D_MODEL = 2048
HEAD_DIM = 128
N_HEADS = 8
N_KV_HEADS = 2
GROUP = N_HEADS // N_KV_HEADS
D_ATTN = N_HEADS * HEAD_DIM
D_CONV = 1024
ROPE_DIM = HEAD_DIM // 4
ROPE_HALF = ROPE_DIM // 2
ROPE_THETA = 500000.0
N_IDX_HEADS = 16
IDX_DIM = 64
IDX_ROPE_HALF = IDX_DIM // 8
TOPK_MAX = 256
CONV_W = 3
PAGE_SIZE = 128
EPS = 1e-6
W_IDX_SCALE = (N_IDX_HEADS ** -0.5) * (IDX_DIM ** -0.5)
ATTN_SCALE = HEAD_DIM ** -0.5
Q_PRESCALE = ATTN_SCALE * 1.4426950408889634

INT_MIN = -(2 ** 31)
NEG = -1e30

VMEM_LIMIT_BYTES = 56 * 1024 * 1024

PROJ_TN = 512
N_KV = N_KV_HEADS * HEAD_DIM
OFF_Q = 0
OFF_K = OFF_Q + D_ATTN
OFF_V = OFF_K + N_KV
OFF_Z = OFF_V + N_KV
OFF_QI = OFF_Z + D_ATTN
OFF_KI = OFF_QI + N_IDX_HEADS * IDX_DIM
OFF_WI = OFF_KI + IDX_DIM
OFF_H = OFF_WI + N_IDX_HEADS
OFF_B = OFF_H + D_CONV
OFF_C = OFF_B + D_CONV
OFF_ZC = OFF_C + D_CONV
D_IN = OFF_ZC + D_CONV
assert OFF_K == 2 * PROJ_TN and OFF_Z == 3 * PROJ_TN and OFF_QI == 5 * PROJ_TN and OFF_KI == 7 * PROJ_TN
KIW_ROWS = 128
assert OFF_KI % KIW_ROWS == 0 and OFF_WI - OFF_KI == IDX_DIM
CONV_CHUNK = 256
N_CONV_CHUNKS = D_CONV // CONV_CHUNK
J_Q = 0
J_KV = 2
J_Z = 3
J_QI = 5
J_CONV = 7
N_MAIN_BLOCKS = J_CONV
N_STEPS = J_CONV + N_CONV_CHUNKS

ATT_TQ = 256
ATT_TK = 256
ATT_SUB = 64
CNT_ACCS = 4

S_PAGES_PER_STEP = 8


def _cparams(n_axes):
    return pltpu.CompilerParams(
        dimension_semantics=("arbitrary",) * n_axes,
        vmem_limit_bytes=VMEM_LIMIT_BYTES,
    )


def _rmsnorm_body(x_ref, g_ref, o_ref):
    x = x_ref[...]
    ms = jnp.mean(x * x, axis=-1, keepdims=True)
    o_ref[...] = (x * lax.rsqrt(ms + EPS) * g_ref[...]).astype(o_ref.dtype)


def _rmsnorm(x, g, tm):
    m = x.shape[0]
    return pl.pallas_call(
        _rmsnorm_body,
        grid=(m // tm,),
        in_specs=[pl.BlockSpec((tm, D_MODEL), lambda i: (i, 0)),
                  pl.BlockSpec((1, D_MODEL), lambda i: (0, 0))],
        out_specs=pl.BlockSpec((tm, D_MODEL), lambda i: (i, 0)),
        out_shape=jax.ShapeDtypeStruct((m, D_MODEL), BF16),
        compiler_params=_cparams(1),
        name="rmsnorm_in",
    )(x, g.reshape(1, D_MODEL))


def _silu(x):
    return x * jax.nn.sigmoid(x)


def _proj_body(sample, tps, tm, *refs):
    (xn_ref, wa_ref, wkiw_ref, wh_ref, wb_ref, wc_ref, wzc_ref, gq_ref, gk_ref, gki_ref,
     c16_ref, s16_ref, c8_ref, s8_ref,
     kc_ref, ksa_ref, ksb_ref, kic_ref, kisa_ref, kisb_ref, wconv_ref) = refs[:21]
    refs = refs[21:]
    if sample:
        e1_ref, e2_ref = refs[:2]
        refs = refs[2:]
    (qT_ref, qiT_ref, vT_ref, wT_ref, k32_ref, kbf_ref, v32_ref, ki32_ref, kibf_ref,
     gate_ref, mixc_ref, u_ref) = refs[:12]
    refs = refs[12:]
    if not sample:
        (carry_ref,) = refs

    i = pl.program_id(0)
    j = pl.program_id(1)
    nt = (((1,), (1,)), ((), ()))

    def xw(w):
        return lax.dot_general(xn_ref[...], w, nt, preferred_element_type=F32)

    def wx(w):
        return lax.dot_general(w, xn_ref[...], nt, preferred_element_type=F32)

    if not sample:
        @pl.when((i == 0) & (j == 0))
        def _init():
            carry_ref[...] = jnp.zeros(carry_ref.shape, F32)

    @pl.when(j < J_KV)
    def _q():
        res = wx(wa_ref[...])
        cos = c16_ref[...]
        sin = s16_ref[...]
        for hh in range(PROJ_TN // HEAD_DIM):
            blk = res[hh * HEAD_DIM:(hh + 1) * HEAD_DIM]
            ms = jnp.mean(blk * blk, axis=0, keepdims=True)
            y = blk * lax.rsqrt(ms + EPS) * gq_ref[...]
            x1 = y[0:ROPE_HALF]
            x2 = y[ROPE_HALF:ROPE_DIM]
            base = hh * HEAD_DIM
            qT_ref[base:base + ROPE_HALF, :] = ((x1 * cos - x2 * sin) * Q_PRESCALE).astype(BF16)
            qT_ref[base + ROPE_HALF:base + ROPE_DIM, :] = ((x2 * cos + x1 * sin) * Q_PRESCALE).astype(BF16)
            qT_ref[base + ROPE_DIM:base + HEAD_DIM, :] = (y[ROPE_DIM:] * Q_PRESCALE).astype(BF16)

    @pl.when((j >= J_QI) & (j < J_CONV))
    def _qi():
        res = wx(wa_ref[...])
        cos = c8_ref[...]
        sin = s8_ref[...]
        for hh in range(PROJ_TN // IDX_DIM):
            blk = res[hh * IDX_DIM:(hh + 1) * IDX_DIM]
            x1 = blk[0:IDX_ROPE_HALF]
            x2 = blk[IDX_ROPE_HALF:2 * IDX_ROPE_HALF]
            rot = jnp.concatenate([x1 * cos - x2 * sin, x2 * cos + x1 * sin], axis=0)
            base = hh * IDX_DIM
            qiT_ref[base:base + 2 * IDX_ROPE_HALF, :] = rot.astype(BF16)
            qiT_ref[base + 2 * IDX_ROPE_HALF:base + IDX_DIM, :] = blk[2 * IDX_ROPE_HALF:].astype(BF16)

    @pl.when(j == J_KV)
    def _kv():
        res = xw(wa_ref[...])
        heads = []
        for hd in range(N_KV_HEADS):
            blk = res[:, hd * HEAD_DIM:(hd + 1) * HEAD_DIM]
            ms = jnp.mean(blk * blk, axis=-1, keepdims=True)
            heads.append(blk * lax.rsqrt(ms + EPS) * gk_ref[:, hd * HEAD_DIM:(hd + 1) * HEAD_DIM])
        yk = jnp.concatenate(heads, axis=1)
        rot = yk * kc_ref[...] + (pltpu.roll(yk, N_KV - ROPE_HALF, axis=1) * ksa_ref[...]
                                  + pltpu.roll(yk, ROPE_HALF, axis=1) * ksb_ref[...])
        k32_ref[...] = rot
        kbf_ref[...] = rot.astype(BF16)
        v32_ref[...] = res[:, N_KV:2 * N_KV]
        vT_ref[...] = wx(wa_ref[N_KV:2 * N_KV, :]).astype(BF16)
        wT_ref[...] = wx(wkiw_ref[IDX_DIM:IDX_DIM + N_IDX_HEADS, :]) * W_IDX_SCALE
        r2 = xw(wkiw_ref[...])
        lane = lax.broadcasted_iota(I32, r2.shape, 1)
        r2 = jnp.where(lane < IDX_DIM, r2, 0.0)
        ms = jnp.sum(r2 * r2, axis=-1, keepdims=True) * (1.0 / IDX_DIM)
        yi = r2 * lax.rsqrt(ms + EPS) * gki_ref[...]
        roti = yi * kic_ref[...] + (pltpu.roll(yi, KIW_ROWS - IDX_ROPE_HALF, axis=1) * kisa_ref[...]
                                    + pltpu.roll(yi, IDX_ROPE_HALF, axis=1) * kisb_ref[...])
        ki32_ref[...] = roti[:, 0:IDX_DIM]
        kibf_ref[...] = roti.astype(BF16)

    @pl.when((j >= J_Z) & (j < J_QI))
    def _z():
        gate_ref[...] = _silu(xw(wa_ref[...]))

    @pl.when(j >= J_CONV)
    def _conv():
        cc = j - J_CONV
        u = xw(wc_ref[...]) * xw(wh_ref[...])
        rowid = lax.broadcasted_iota(I32, (tm, CONV_CHUNK), 0)
        if sample:
            t = rowid & (sample - 1)
            u1 = jnp.where(t >= 1, pltpu.roll(u, 1, axis=0), e1_ref[...])
            u2 = jnp.where(t >= 2, pltpu.roll(u, 2, axis=0), e2_ref[...])
            u_ref[...] = u
        else:
            first = (i % tps) == 0
            prev = carry_ref[cc]
            p0 = jnp.where(first, 0.0, prev[0:1])
            p1 = jnp.where(first, 0.0, prev[1:2])
            u1 = jnp.where(rowid == 0, p1, pltpu.roll(u, 1, axis=0))
            u2 = jnp.where(rowid == 0, p0, jnp.where(rowid == 1, p1, pltpu.roll(u, 2, axis=0)))
            tail = u[tm - 8:tm]
            carry_ref[cc] = jnp.concatenate([tail[6:8], tail[0:6]], axis=0)
            u_ref[0] = tail[6:8]
        w = wconv_ref[...]
        y = u2 * w[0:1] + u1 * w[1:2] + u * w[2:3]
        mixc_ref[...] = (xw(wb_ref[...]) * y * _silu(xw(wzc_ref[...]))).astype(BF16)


def _project(xn, wT, params, tabs, sample, state_rows=None):
    m = xn.shape[0]
    assert sample & (sample - 1) == 0
    tm = m if sample else 512
    n_i = m // tm
    tps = 1 if sample else (tabs["c16T"].shape[1] // tm)
    gq, gk, gki, wconv = params

    def tmap(i):
        return i % tps

    def cchunk(j):
        return jnp.clip(j - J_CONV, 0, N_CONV_CHUNKS - 1)

    def conv_rows(off):
        return pl.BlockSpec((pl.Element(CONV_CHUNK), pl.Element(D_MODEL)),
                            lambda i, j: (pl.multiple_of(off + CONV_CHUNK * cchunk(j), 16), 0))

    in_specs = [
        pl.BlockSpec((tm, D_MODEL), lambda i, j: (i, 0)),
        pl.BlockSpec((PROJ_TN, D_MODEL), lambda i, j: (jnp.minimum(j, N_MAIN_BLOCKS - 1), 0)),
        pl.BlockSpec((KIW_ROWS, D_MODEL), lambda i, j: (OFF_KI // KIW_ROWS, 0)),
        conv_rows(OFF_H), conv_rows(OFF_B), conv_rows(OFF_C), conv_rows(OFF_ZC),
        pl.BlockSpec((HEAD_DIM, 1), lambda i, j: (0, 0)),
        pl.BlockSpec((1, N_KV), lambda i, j: (0, 0)),
        pl.BlockSpec((1, KIW_ROWS), lambda i, j: (0, 0)),
        pl.BlockSpec((ROPE_HALF, tm), lambda i, j: (0, tmap(i))),
        pl.BlockSpec((ROPE_HALF, tm), lambda i, j: (0, tmap(i))),
        pl.BlockSpec((IDX_ROPE_HALF, tm), lambda i, j: (0, tmap(i))),
        pl.BlockSpec((IDX_ROPE_HALF, tm), lambda i, j: (0, tmap(i))),
        pl.BlockSpec((tm, N_KV), lambda i, j: (tmap(i), 0)),
        pl.BlockSpec((tm, N_KV), lambda i, j: (tmap(i), 0)),
        pl.BlockSpec((tm, N_KV), lambda i, j: (tmap(i), 0)),
        pl.BlockSpec((tm, KIW_ROWS), lambda i, j: (tmap(i), 0)),
        pl.BlockSpec((tm, KIW_ROWS), lambda i, j: (tmap(i), 0)),
        pl.BlockSpec((tm, KIW_ROWS), lambda i, j: (tmap(i), 0)),
        pl.BlockSpec((CONV_W, CONV_CHUNK), lambda i, j: (0, cchunk(j))),
    ]
    args = [xn, wT, wT, wT, wT, wT, wT, gq, gk, gki,
            tabs["c16T"], tabs["s16T"], tabs["c8T"], tabs["s8T"],
            tabs["kC"], tabs["kSa"], tabs["kSb"], tabs["kiC"], tabs["kiSa"], tabs["kiSb"], wconv]
    cmap = lambda i, j: (i, cchunk(j))
    if sample:
        in_specs += [pl.BlockSpec((tm, CONV_CHUNK), cmap), pl.BlockSpec((tm, CONV_CHUNK), cmap)]
        args += list(state_rows)
        u_spec = pl.BlockSpec((tm, CONV_CHUNK), cmap)
        u_shape = jax.ShapeDtypeStruct((m, D_CONV), F32)
        scratch = []
    else:
        u_spec = pl.BlockSpec((1, CONV_W - 1, CONV_CHUNK), lambda i, j: (i, 0, cchunk(j)))
        u_shape = jax.ShapeDtypeStruct((n_i, CONV_W - 1, D_CONV), F32)
        scratch = [pltpu.VMEM((N_CONV_CHUNKS, 8, CONV_CHUNK), F32)]

    out_specs = [
        pl.BlockSpec((PROJ_TN, tm), lambda i, j: (jnp.minimum(j, 1), i)),
        pl.BlockSpec((PROJ_TN, tm), lambda i, j: (jnp.clip(j - J_QI, 0, 1), i)),
        pl.BlockSpec((N_KV, tm), lambda i, j: (0, i)),
        pl.BlockSpec((N_IDX_HEADS, tm), lambda i, j: (0, i)),
        pl.BlockSpec((tm, N_KV), lambda i, j: (i, 0)),
        pl.BlockSpec((tm, N_KV), lambda i, j: (i, 0)),
        pl.BlockSpec((tm, N_KV), lambda i, j: (i, 0)),
        pl.BlockSpec((tm, IDX_DIM), lambda i, j: (i, 0)),
        pl.BlockSpec((tm, KIW_ROWS), lambda i, j: (i, 0)),
        pl.BlockSpec((tm, PROJ_TN), lambda i, j: (i, jnp.clip(j - J_Z, 0, 1))),
        pl.BlockSpec((tm, CONV_CHUNK), cmap),
        u_spec,
    ]
    out_shape = [
        jax.ShapeDtypeStruct((D_ATTN, m), BF16),
        jax.ShapeDtypeStruct((N_IDX_HEADS * IDX_DIM, m), BF16),
        jax.ShapeDtypeStruct((N_KV, m), BF16),
        jax.ShapeDtypeStruct((N_IDX_HEADS, m), F32),
        jax.ShapeDtypeStruct((m, N_KV), F32),
        jax.ShapeDtypeStruct((m, N_KV), BF16),
        jax.ShapeDtypeStruct((m, N_KV), F32),
        jax.ShapeDtypeStruct((m, IDX_DIM), F32),
        jax.ShapeDtypeStruct((m, KIW_ROWS), BF16),
        jax.ShapeDtypeStruct((m, D_ATTN), F32),
        jax.ShapeDtypeStruct((m, D_CONV), BF16),
        u_shape,
    ]
    return pl.pallas_call(
        functools.partial(_proj_body, sample, tps, tm),
        grid=(n_i, N_STEPS),
        in_specs=in_specs,
        out_specs=out_specs,
        out_shape=out_shape,
        scratch_shapes=scratch,
        compiler_params=_cparams(2),
        name="proj_sample" if sample else "proj_prompt",
    )(*args)


def _sort_key(s):
    bits = pltpu.bitcast(s, I32)
    return bits ^ ((bits >> 31) & 0x7FFFFFFF)


def _bit_value(b):
    return lax.shift_left(jnp.int32(1), jnp.int32(31) - b)


def _attn_prompt_body(qiT_ref, wT_ref, ki_ref, qT_ref, k_ref, vT_ref, gate_ref, o_ref,
                      keys_ref, bias_ref, m_ref, l_ref, acc_ref, s_ref, p_ref):
    tq, tk = ATT_TQ, ATT_TK
    i = pl.program_id(1)
    nch = i + 1
    w = wT_ref[...]
    row = lax.broadcasted_iota(I32, (tk, tq), 0)
    col = lax.broadcasted_iota(I32, (tk, tq), 1)

    def score_chunk(j, carry):
        off = pl.multiple_of(j * tk, tk)
        kic = ki_ref[pl.ds(off, tk), 0:IDX_DIM]
        acc = jnp.zeros((tk, tq), F32)
        for h in range(N_IDX_HEADS):
            d = jnp.dot(kic, qiT_ref[h * IDX_DIM:(h + 1) * IDX_DIM, :], preferred_element_type=F32)
            acc = acc + w[h:h + 1, :] * jnp.maximum(d, 0.0)
        key = _sort_key(acc)
        future = (row + j * tk) > (col + i * tq)
        keys_ref[pl.ds(off, tk), :] = jnp.where(future, INT_MIN, key)
        return carry

    lax.fori_loop(0, nch, score_chunk, 0)

    def bit_body(b, thr):
        cand = thr + _bit_value(b)

        def cnt_chunk(j, cs):
            off = pl.multiple_of(j * tk, tk)
            cs = list(cs)
            kc = keys_ref[pl.ds(off, tk), :]
            for r in range(tk // 8):
                a = cs[r % CNT_ACCS]
                cs[r % CNT_ACCS] = jnp.where(kc[r * 8:(r + 1) * 8] >= cand, a + 1, a)
            return tuple(cs)

        cs = lax.fori_loop(0, nch, cnt_chunk, (jnp.zeros((8, tq), I32),) * CNT_ACCS)
        c = cs[0]
        for a in cs[1:]:
            c = c + a
        cnt = jnp.sum(c.astype(F32), axis=0, keepdims=True)
        return jnp.where(cnt >= float(TOPK_MAX), cand, thr)

    thr = lax.fori_loop(0, 32, bit_body, jnp.full((1, tq), INT_MIN, I32))
    thr = jnp.maximum(thr, INT_MIN + 1)

    def bias_chunk(j, carry):
        off = pl.multiple_of(j * tk, tk)
        kk = keys_ref[pl.ds(off, tk), :]
        bias_ref[pl.ds(off, tk), :] = jnp.where(kk >= thr, 0.0, NEG)
        return carry

    lax.fori_loop(0, nch, bias_chunk, 0)

    m_ref[...] = jnp.full(m_ref.shape, NEG, F32)
    l_ref[...] = jnp.zeros(l_ref.shape, F32)
    acc_ref[...] = jnp.zeros(acc_ref.shape, F32)

    def att_chunk(j, carry):
        off = pl.multiple_of(j * tk, tk)
        bias = bias_ref[pl.ds(off, tk), :]
        for h in range(N_HEADS):
            g = h // GROUP
            kc = k_ref[pl.ds(off, tk), g * HEAD_DIM:(g + 1) * HEAD_DIM]
            s_ref[h * tk:(h + 1) * tk, :] = jnp.dot(
                kc, qT_ref[h * HEAD_DIM:(h + 1) * HEAD_DIM, :], preferred_element_type=F32) + bias
        alphas = []
        for h in range(N_HEADS):
            mloc = None
            for r in range(tk // ATT_SUB):
                blk = s_ref[h * tk + r * ATT_SUB:h * tk + (r + 1) * ATT_SUB, :]
                bm = blk.reshape(ATT_SUB // 8, 8, tq).max(axis=0)
                mloc = bm if mloc is None else jnp.maximum(mloc, bm)
            m_old = m_ref[h:h + 1, :]
            m_new = jnp.maximum(m_old, jnp.max(mloc, axis=0, keepdims=True))
            alpha = jnp.exp2(m_old - m_new)
            lsum = None
            for r in range(tk // ATT_SUB):
                rs = slice(h * tk + r * ATT_SUB, h * tk + (r + 1) * ATT_SUB)
                p = jnp.exp2(s_ref[rs, :] - m_new)
                ps = p.reshape(ATT_SUB // 8, 8, tq).sum(axis=0)
                lsum = ps if lsum is None else lsum + ps
                p_ref[rs, :] = p.astype(BF16)
            l_ref[h:h + 1, :] = l_ref[h:h + 1, :] * alpha + jnp.sum(lsum, axis=0, keepdims=True)
            m_ref[h:h + 1, :] = m_new
            alphas.append(alpha)
        for h in range(N_HEADS):
            g = h // GROUP
            hs = slice(h * HEAD_DIM, (h + 1) * HEAD_DIM)
            vc = vT_ref[g * HEAD_DIM:(g + 1) * HEAD_DIM, pl.ds(off, tk)]
            acc_ref[hs, :] = acc_ref[hs, :] * alphas[h] + jnp.dot(
                vc, p_ref[h * tk:(h + 1) * tk, :], preferred_element_type=F32)
        return carry

    lax.fori_loop(0, nch, att_chunk, 0)

    for h in range(N_HEADS):
        hs = slice(h * HEAD_DIM, (h + 1) * HEAD_DIM)
        o = (acc_ref[hs, :] / l_ref[h:h + 1, :]).T
        o_ref[:, hs] = (o * gate_ref[:, hs]).astype(BF16)


def _attn_prompt(qiT, wT, kibf, qT, kbf, vT, gate, n_batch, seq):
    m = n_batch * seq
    nq = seq // ATT_TQ
    nkv = N_KV_HEADS * HEAD_DIM
    qmap = lambda b, i: (0, b * nq + i)
    return pl.pallas_call(
        _attn_prompt_body,
        grid=(n_batch, nq),
        in_specs=[
            pl.BlockSpec((N_IDX_HEADS * IDX_DIM, ATT_TQ), qmap),
            pl.BlockSpec((N_IDX_HEADS, ATT_TQ), qmap),
            pl.BlockSpec((seq, 128), lambda b, i: (b, 0)),
            pl.BlockSpec((D_ATTN, ATT_TQ), qmap),
            pl.BlockSpec((seq, nkv), lambda b, i: (b, 0)),
            pl.BlockSpec((nkv, seq), lambda b, i: (0, b)),
            pl.BlockSpec((ATT_TQ, D_ATTN), lambda b, i: (b * nq + i, 0)),
        ],
        out_specs=pl.BlockSpec((ATT_TQ, D_ATTN), lambda b, i: (b * nq + i, 0)),
        out_shape=jax.ShapeDtypeStruct((m, D_ATTN), BF16),
        scratch_shapes=[pltpu.VMEM((seq, ATT_TQ), I32), pltpu.VMEM((seq, ATT_TQ), F32),
                        pltpu.VMEM((N_HEADS, ATT_TQ), F32), pltpu.VMEM((N_HEADS, ATT_TQ), F32),
                        pltpu.VMEM((D_ATTN, ATT_TQ), F32),
                        pltpu.VMEM((N_HEADS * ATT_TK, ATT_TQ), F32),
                        pltpu.VMEM((N_HEADS * ATT_TK, ATT_TQ), BF16)],
        compiler_params=_cparams(2),
        name="attn_prompt",
    )(qiT, wT, kibf, qT, kbf, vT, gate)


def _s_score_body(pt_ref, qi_ref, wcol_ref, kinew_ref, *rest):
    npg = S_PAGES_PER_STEP
    pages = rest[:npg]
    out_ref, outnew_ref = rest[npg:]
    pg = pl.program_id(1)
    qi = qi_ref[0]
    wc = wcol_ref[0]

    def scores(keys_t):
        d = jnp.dot(qi, keys_t, preferred_element_type=F32)
        val = jnp.maximum(d, 0.0) * wc
        return val.reshape(N_IDX_HEADS, 8, val.shape[-1]).sum(axis=0)

    for r in range(npg):
        out_ref[0, :, r * PAGE_SIZE:(r + 1) * PAGE_SIZE] = scores(pages[r][...].astype(BF16))

    @pl.when(pg == 0)
    def _new():
        outnew_ref[0] = scores(kinew_ref[0])


def _s_thresh_body(n_new, sp_ref, sn_ref, bp_ref, bn_ref, keys_ref):
    rows, past = sp_ref.shape
    ch = 2048
    nchunk = past // ch

    def to_keys(c, carry):
        off = pl.multiple_of(c * ch, ch)
        keys_ref[:, pl.ds(off, ch)] = _sort_key(sp_ref[:, pl.ds(off, ch)])
        return carry

    lax.fori_loop(0, nchunk, to_keys, 0)
    t = lax.broadcasted_iota(I32, (rows, 128), 0) & 7
    lane = lax.broadcasted_iota(I32, (rows, 128), 1)
    kn = jnp.where((lane < n_new) & (lane <= t), _sort_key(sn_ref[...]), INT_MIN)

    def fold(x):
        f = x[:, 0:128]
        for q in range(1, x.shape[1] // 128):
            f = f + x[:, q * 128:(q + 1) * 128]
        return f

    def bit_body(b, thr):
        cand = thr + _bit_value(b)

        def cnt_chunk(c, acc):
            off = pl.multiple_of(c * ch, ch)
            kk = keys_ref[:, pl.ds(off, ch)]
            return acc + fold(jnp.where(kk >= cand, 1.0, 0.0))

        acc = lax.fori_loop(0, nchunk, cnt_chunk, jnp.where(kn >= cand, 1.0, 0.0))
        cnt = jnp.sum(acc, axis=1, keepdims=True)
        return jnp.where(cnt >= float(TOPK_MAX), cand, thr)

    thr = lax.fori_loop(0, 32, bit_body, jnp.full((rows, 1), INT_MIN, I32))
    thr = jnp.maximum(thr, INT_MIN + 1)

    def to_bias(c, carry):
        off = pl.multiple_of(c * ch, ch)
        bp_ref[:, pl.ds(off, ch)] = jnp.where(keys_ref[:, pl.ds(off, ch)] >= thr, 0.0, NEG)
        return carry

    lax.fori_loop(0, nchunk, to_bias, 0)
    bn_ref[...] = jnp.where(kn >= thr, 0.0, NEG)


def _s_attn_body(pt_ref, q_ref, bp_ref, bn_ref, knew_ref, vnew_ref, *rest):
    npg = S_PAGES_PER_STEP
    kpages = rest[:npg]
    vpages = rest[npg:2 * npg]
    out_ref, m_ref, l_ref, acc_ref = rest[2 * npg:]
    pg = pl.program_id(1)
    nt = (((1,), (1,)), ((), ()))
    rg = GROUP * 8

    @pl.when(pg == 0)
    def _init():
        m_ref[...] = jnp.full(m_ref.shape, NEG, F32)
        l_ref[...] = jnp.zeros(l_ref.shape, F32)
        acc_ref[...] = jnp.zeros(acc_ref.shape, F32)

    def update(g, s, vg):
        rs = slice(g * rg, (g + 1) * rg)
        m_old = m_ref[rs, :]
        m_new = jnp.maximum(m_old, jnp.max(s, axis=1, keepdims=True))
        alpha = jnp.exp2(m_old - m_new)
        p = jnp.exp2(s - m_new[:, 0:1])
        f = p[:, 0:128]
        for q in range(1, p.shape[1] // 128):
            f = f + p[:, q * 128:(q + 1) * 128]
        l_ref[rs, :] = l_ref[rs, :] * alpha + f
        acc_ref[rs, :] = acc_ref[rs, :] * alpha + jnp.dot(p.astype(BF16), vg, preferred_element_type=F32)
        m_ref[rs, :] = m_new

    bias8 = bp_ref[0]
    bias = jnp.concatenate([bias8] * GROUP, axis=0)
    for g in range(N_KV_HEADS):
        qg = q_ref[0, g * rg:(g + 1) * rg, :]
        kg = jnp.concatenate([kp[pl.ds(g, PAGE_SIZE, stride=N_KV_HEADS), :] for kp in kpages], axis=0)
        vg = jnp.concatenate([vp[pl.ds(g, PAGE_SIZE, stride=N_KV_HEADS), :] for vp in vpages], axis=0)
        s = lax.dot_general(qg, kg.astype(BF16), nt, preferred_element_type=F32) + bias
        update(g, s, vg.astype(BF16))

    @pl.when(pg == pl.num_programs(1) - 1)
    def _last():
        biasn = jnp.concatenate([bn_ref[0]] * GROUP, axis=0)
        for g in range(N_KV_HEADS):
            qg = q_ref[0, g * rg:(g + 1) * rg, :]
            kg = knew_ref[0, :, g * HEAD_DIM:(g + 1) * HEAD_DIM]
            vg = vnew_ref[0, :, g * HEAD_DIM:(g + 1) * HEAD_DIM]
            s = lax.dot_general(qg, kg, nt, preferred_element_type=F32) + biasn
            update(g, s, vg)
        out_ref[0] = acc_ref[...] / jnp.sum(l_ref[...], axis=1, keepdims=True)


def _attn_sample(qT, qiT, wT, kibf, kbf, v32, cache_k, cache_v, cache_kidx, page_table, n_seq, n_tok):
    n_pages = page_table.shape[1]
    past = n_pages * PAGE_SIZE
    npg = S_PAGES_PER_STEP
    n_steps = n_pages // npg
    n_pool = cache_k.shape[0]

    def rows_ht(xT, n_heads, dim):
        x = xT.reshape(n_heads, dim, n_seq, n_tok).transpose(2, 0, 3, 1)
        x = jnp.pad(x, ((0, 0), (0, 0), (0, 8 - n_tok), (0, 0)))
        return x.reshape(n_seq, n_heads * 8, dim)

    qi_rows = rows_ht(qiT, N_IDX_HEADS, IDX_DIM)
    q_rows = rows_ht(qT, N_HEADS, HEAD_DIM)
    wcol = jnp.pad(wT.reshape(N_IDX_HEADS, n_seq, n_tok).transpose(1, 0, 2),
                   ((0, 0), (0, 0), (0, 8 - n_tok))).reshape(n_seq, N_IDX_HEADS * 8, 1)

    def pad_keys(x):
        x = x.reshape(n_seq, n_tok, x.shape[-1])
        return jnp.pad(x, ((0, 0), (0, PAGE_SIZE - n_tok), (0, 0)))

    ki_new_t = jnp.swapaxes(pad_keys(kibf[:, 0:IDX_DIM]), 1, 2)
    kidx_t = jnp.swapaxes(cache_kidx, 1, 2)
    k_new = pad_keys(kbf)
    v_new = pad_keys(v32.astype(BF16))

    def page_spec(shape, r):
        return pl.BlockSpec((None,) + shape, lambda b, pg, pt, r=r: (pt[b, pg * npg + r], 0, 0))

    sp, sn = pl.pallas_call(
        _s_score_body,
        grid_spec=pltpu.PrefetchScalarGridSpec(
            num_scalar_prefetch=1,
            grid=(n_seq, n_steps),
            in_specs=[
                pl.BlockSpec((1, N_IDX_HEADS * 8, IDX_DIM), lambda b, pg, pt: (b, 0, 0)),
                pl.BlockSpec((1, N_IDX_HEADS * 8, 1), lambda b, pg, pt: (b, 0, 0)),
                pl.BlockSpec((1, IDX_DIM, PAGE_SIZE), lambda b, pg, pt: (b, 0, 0)),
            ] + [page_spec((IDX_DIM, PAGE_SIZE), r) for r in range(npg)],
            out_specs=[
                pl.BlockSpec((1, 8, npg * PAGE_SIZE), lambda b, pg, pt: (b, 0, pg)),
                pl.BlockSpec((1, 8, PAGE_SIZE), lambda b, pg, pt: (b, 0, 0)),
            ],
        ),
        out_shape=[jax.ShapeDtypeStruct((n_seq, 8, past), F32),
                   jax.ShapeDtypeStruct((n_seq, 8, PAGE_SIZE), F32)],
        compiler_params=_cparams(2),
        name="sample_scores",
    )(page_table, qi_rows, wcol, ki_new_t, *([kidx_t] * npg))

    rows = n_seq * 8
    bp, bn = pl.pallas_call(
        functools.partial(_s_thresh_body, n_tok),
        grid=(1,),
        in_specs=[pl.BlockSpec((rows, past), lambda i: (0, 0)),
                  pl.BlockSpec((rows, PAGE_SIZE), lambda i: (0, 0))],
        out_specs=[pl.BlockSpec((rows, past), lambda i: (0, 0)),
                   pl.BlockSpec((rows, PAGE_SIZE), lambda i: (0, 0))],
        out_shape=[jax.ShapeDtypeStruct((rows, past), F32),
                   jax.ShapeDtypeStruct((rows, PAGE_SIZE), F32)],
        scratch_shapes=[pltpu.VMEM((rows, past), I32)],
        compiler_params=_cparams(1),
        name="sample_topk_mask",
    )(sp.reshape(rows, past), sn.reshape(rows, PAGE_SIZE))
    bp = bp.reshape(n_seq, 8, past)
    bn = bn.reshape(n_seq, 8, PAGE_SIZE)

    nrow = N_HEADS * 8
    kv_rows = PAGE_SIZE * N_KV_HEADS
    ck = cache_k.reshape(n_pool, kv_rows, HEAD_DIM)
    cv = cache_v.reshape(n_pool, kv_rows, HEAD_DIM)
    out = pl.pallas_call(
        _s_attn_body,
        grid_spec=pltpu.PrefetchScalarGridSpec(
            num_scalar_prefetch=1,
            grid=(n_seq, n_steps),
            in_specs=[
                pl.BlockSpec((1, nrow, HEAD_DIM), lambda b, pg, pt: (b, 0, 0)),
                pl.BlockSpec((1, 8, npg * PAGE_SIZE), lambda b, pg, pt: (b, 0, pg)),
                pl.BlockSpec((1, 8, PAGE_SIZE), lambda b, pg, pt: (b, 0, 0)),
                pl.BlockSpec((1, PAGE_SIZE, N_KV_HEADS * HEAD_DIM), lambda b, pg, pt: (b, 0, 0)),
                pl.BlockSpec((1, PAGE_SIZE, N_KV_HEADS * HEAD_DIM), lambda b, pg, pt: (b, 0, 0)),
            ] + [page_spec((kv_rows, HEAD_DIM), r) for r in range(npg)] * 2,
            out_specs=pl.BlockSpec((1, nrow, HEAD_DIM), lambda b, pg, pt: (b, 0, 0)),
            scratch_shapes=[pltpu.VMEM((nrow, 128), F32), pltpu.VMEM((nrow, 128), F32),
                            pltpu.VMEM((nrow, HEAD_DIM), F32)],
        ),
        out_shape=jax.ShapeDtypeStruct((n_seq, nrow, HEAD_DIM), F32),
        compiler_params=_cparams(2),
        name="sample_attn",
    )(page_table, q_rows, bp, bn, k_new, v_new, *([ck] * npg), *([cv] * npg))
    out = out.reshape(n_seq, N_HEADS, 8, HEAD_DIM)[:, :, 0:n_tok]
    return out.transpose(0, 2, 1, 3).reshape(n_seq * n_tok, D_ATTN)


def _outproj_body(gated, *refs):
    if gated:
        x_ref, a_ref, gate_ref, mc_ref, wo_ref, o_ref = refs
        ma = (a_ref[...] * gate_ref[...]).astype(BF16)
    else:
        x_ref, a_ref, mc_ref, wo_ref, o_ref = refs
        ma = a_ref[...]
    acc = jnp.dot(ma, wo_ref[0:D_ATTN, :], preferred_element_type=F32)
    acc = acc + jnp.dot(mc_ref[...], wo_ref[D_ATTN:D_ATTN + D_CONV, :], preferred_element_type=F32)
    o_ref[...] = x_ref[...] + acc


def _outproj(x, attn, gate, mixc, wo, tm):
    m = x.shape[0]
    gated = gate is not None
    row = lambda i: (i, 0)
    in_specs = [pl.BlockSpec((tm, D_MODEL), row), pl.BlockSpec((tm, D_ATTN), row)]
    args = [x, attn]
    if gated:
        in_specs.append(pl.BlockSpec((tm, D_ATTN), row))
        args.append(gate)
    in_specs += [pl.BlockSpec((tm, D_CONV), row),
                 pl.BlockSpec((D_ATTN + D_CONV, D_MODEL), lambda i: (0, 0))]
    args += [mixc, wo]
    return pl.pallas_call(
        functools.partial(_outproj_body, gated),
        grid=(m // tm,),
        in_specs=in_specs,
        out_specs=pl.BlockSpec((tm, D_MODEL), row),
        out_shape=jax.ShapeDtypeStruct((m, D_MODEL), F32),
        compiler_params=_cparams(1),
        name="outproj_sample" if gated else "outproj_prompt",
    )(*args)


def _rope_tables(pos):
    posf = pos.astype(F32)[:, None]
    n = pos.shape[0]

    def cs(half):
        inv = ROPE_THETA ** (-jnp.arange(half, dtype=F32) / half)
        ang = posf * inv[None, :]
        return jnp.cos(ang), jnp.sin(ang)

    c16, s16 = cs(ROPE_HALF)
    c8, s8 = cs(IDX_ROPE_HALF)
    one = lambda w: jnp.ones((n, w), F32)
    zero = lambda w: jnp.zeros((n, w), F32)
    rest = HEAD_DIM - ROPE_DIM
    k_c = jnp.tile(jnp.concatenate([c16, c16, one(rest)], axis=1), (1, N_KV_HEADS))
    k_sa = jnp.tile(jnp.concatenate([-s16, zero(HEAD_DIM - ROPE_HALF)], axis=1), (1, N_KV_HEADS))
    k_sb = jnp.tile(jnp.concatenate([zero(ROPE_HALF), s16, zero(rest)], axis=1), (1, N_KV_HEADS))
    ki_c = jnp.concatenate([c8, c8, one(128 - 2 * IDX_ROPE_HALF)], axis=1)
    ki_sa = jnp.concatenate([-s8, zero(128 - IDX_ROPE_HALF)], axis=1)
    ki_sb = jnp.concatenate([zero(IDX_ROPE_HALF), s8, zero(128 - 2 * IDX_ROPE_HALF)], axis=1)
    return dict(c16T=c16.T, s16T=s16.T, c8T=c8.T, s8T=s8.T,
                kC=k_c, kSa=k_sa, kSb=k_sb, kiC=ki_c, kiSa=ki_sa, kiSb=ki_sb)


def _prep_weights(w_in, g_q, g_k, g_kidx, w_conv, w_out):
    assert w_in.shape == (D_MODEL, D_IN)
    wT = w_in.T.astype(BF16)
    gq = g_q.reshape(HEAD_DIM, 1)
    gk = jnp.tile(g_k.reshape(1, HEAD_DIM), (1, N_KV_HEADS))
    gki = jnp.concatenate([g_kidx.reshape(1, IDX_DIM), jnp.zeros((1, KIW_ROWS - IDX_DIM), F32)], axis=1)
    return wT, (gq, gk, gki, w_conv), w_out.astype(BF16)


def kernel(x_prompt, x_sample, cache_k, cache_v, cache_kidx, state_conv, page_table,
           norm_in, w_in, g_q, g_k, g_kidx, w_conv, w_out):
    n_b, seq, _ = x_prompt.shape
    n_s, n_t, _ = x_sample.shape
    depth = w_in.shape[0]
    past = page_table.shape[1] * PAGE_SIZE
    tabs_p = _rope_tables(jnp.arange(seq))
    tabs_s = _rope_tables(jnp.tile(past + jnp.arange(n_t), n_s))

    hp = x_prompt.reshape(n_b * seq, D_MODEL)
    hs = x_sample.reshape(n_s * n_t, D_MODEL)
    outs = [[] for _ in range(8)]
    for l in range(depth):
        w_t, params, wo = _prep_weights(w_in[l], g_q[l], g_k[l], g_kidx[l], w_conv[l], w_out[l])

        xn = _rmsnorm(hp, norm_in[l], 512)
        (qT, qiT, vT, wT, k32, kbf, v32, ki32, kibf, gate, mixc, utail) = _project(
            xn, w_t, params, tabs_p, 0)
        mixa = _attn_prompt(qiT, wT, kibf, qT, kbf, vT, gate, n_b, seq)
        hp = _outproj(hp, mixa, None, mixc, wo, 512)
        tps = seq // 512
        outs[0].append(k32.reshape(n_b, seq, N_KV_HEADS, HEAD_DIM))
        outs[1].append(v32.reshape(n_b, seq, N_KV_HEADS, HEAD_DIM))
        outs[2].append(ki32.reshape(n_b, seq, IDX_DIM))
        outs[3].append(utail[tps - 1::tps])

        st = state_conv[l]
        tok = jnp.arange(n_t)
        e1 = st[:, jnp.full((n_t,), CONV_W - 2)].reshape(n_s * n_t, D_CONV)
        e2 = st[:, jnp.minimum(tok, CONV_W - 2)].reshape(n_s * n_t, D_CONV)
        xn = _rmsnorm(hs, norm_in[l], n_s * n_t)
        (qT, qiT, vT, wT, k32, kbf, v32, ki32, kibf, gate, mixc, u) = _project(
            xn, w_t, params, tabs_s, n_t, state_rows=(e1, e2))
        attn = _attn_sample(qT, qiT, wT, kibf, kbf, v32, cache_k[l], cache_v[l], cache_kidx[l],
                            page_table, n_s, n_t)
        hs = _outproj(hs, attn, gate, mixc, wo, n_s * n_t)
        outs[4].append(k32.reshape(n_s, n_t, N_KV_HEADS, HEAD_DIM))
        outs[5].append(v32.reshape(n_s, n_t, N_KV_HEADS, HEAD_DIM))
        outs[6].append(ki32.reshape(n_s, n_t, IDX_DIM))
        outs[7].append(u.reshape(n_s, n_t, D_CONV)[:, n_t - (CONV_W - 1):])

    return (hp.reshape(n_b, seq, D_MODEL), hs.reshape(n_s, n_t, D_MODEL),
            *[jnp.stack(o) for o in outs])
```

```python
import functools

import jax
import jax.numpy as jnp
import numpy as np
from jax import lax
from jax.experimental import pallas as pl
from jax.experimental.pallas import tpu as pltpu

F32 = jnp.float32
BF16 = jnp.bfloat16
I32 = jnp.int32

D_MODEL = 2048
HEAD_DIM = 128
N_HEADS = 8
N_KV_HEADS = 2
GROUP = N_HEADS // N_KV_HEADS
D_ATTN = N_HEADS * HEAD_DIM
D_CONV = 1024
ROPE_DIM = HEAD_DIM // 4
ROPE_HALF = ROPE_DIM // 2
ROPE_THETA = 500000.0
N_IDX_HEADS = 16
IDX_DIM = 64
IDX_ROPE_HALF = IDX_DIM // 8
TOPK_MAX = 256
CONV_W = 3
PAGE_SIZE = 128
EPS = 1e-6
W_IDX_SCALE = (N_IDX_HEADS ** -0.5) * (IDX_DIM ** -0.5)
ATTN_SCALE = HEAD_DIM ** -0.5
Q_PRESCALE = ATTN_SCALE * 1.4426950408889634

INT_MIN = -(2 ** 31)
NEG = -1e30

VMEM_LIMIT_BYTES = 56 * 1024 * 1024

PROJ_TN = 512
PROJ_TM = 1024
ROW_TM = 512
N_KV = N_KV_HEADS * HEAD_DIM
OFF_Q = 0
OFF_K = OFF_Q + D_ATTN
OFF_V = OFF_K + N_KV
OFF_Z = OFF_V + N_KV
OFF_QI = OFF_Z + D_ATTN
OFF_KI = OFF_QI + N_IDX_HEADS * IDX_DIM
OFF_WI = OFF_KI + IDX_DIM
OFF_H = OFF_WI + N_IDX_HEADS
OFF_B = OFF_H + D_CONV
OFF_C = OFF_B + D_CONV
OFF_ZC = OFF_C + D_CONV
D_IN = OFF_ZC + D_CONV
assert OFF_K == 2 * PROJ_TN and OFF_Z == 3 * PROJ_TN and OFF_QI == 5 * PROJ_TN and OFF_KI == 7 * PROJ_TN
KIW_ROWS = 128
assert OFF_KI % KIW_ROWS == 0 and OFF_WI - OFF_KI == IDX_DIM
CONV_CHUNK = 256
N_CONV_CHUNKS = D_CONV // CONV_CHUNK
J_Q = 0
J_KV = 2
J_Z = 3
J_QI = 5
J_CONV = 7
N_MAIN_BLOCKS = J_CONV
N_STEPS = J_CONV + N_CONV_CHUNKS

ATT_TQ = 256
ATT_TK = 256
ATT_SUB = 64
CNT_ACCS = 4

S_SCORE_PAGES = 32
S_ATTN_PAGES = 16


def _cparams(n_axes):
    return pltpu.CompilerParams(
        dimension_semantics=("arbitrary",) * n_axes,
        vmem_limit_bytes=VMEM_LIMIT_BYTES,
    )


def _rmsnorm_body(x_ref, g_ref, o_ref):
    x = x_ref[...]
    ms = jnp.mean(x * x, axis=-1, keepdims=True)
    o_ref[...] = (x * lax.rsqrt(ms + EPS) * g_ref[...]).astype(o_ref.dtype)


def _rmsnorm(x, g, tm):
    m = x.shape[0]
    return pl.pallas_call(
        _rmsnorm_body,
        grid=(m // tm,),
        in_specs=[pl.BlockSpec((tm, D_MODEL), lambda i: (i, 0)),
                  pl.BlockSpec((1, D_MODEL), lambda i: (0, 0))],
        out_specs=pl.BlockSpec((tm, D_MODEL), lambda i: (i, 0)),
        out_shape=jax.ShapeDtypeStruct((m, D_MODEL), BF16),
        compiler_params=_cparams(1),
        name="rmsnorm_in",
    )(x, g.reshape(1, D_MODEL))


def _silu(x):
    return x * jax.nn.sigmoid(x)


def _proj_body(sample, tps, tm, *refs):
    (xn_ref, wa_ref, wkiw_ref, wh_ref, wb_ref, wc_ref, wzc_ref, gq_ref, gk_ref, gki_ref,
     c16_ref, s16_ref, c8_ref, s8_ref,
     kc_ref, ksa_ref, ksb_ref, kic_ref, kisa_ref, kisb_ref, wconv_ref) = refs[:21]
    refs = refs[21:]
    if sample:
        e1_ref, e2_ref = refs[:2]
        refs = refs[2:]
    (qT_ref, qiT_ref, vT_ref, wT_ref, k32_ref, kbf_ref, v32_ref, ki32_ref, kibf_ref,
     gate_ref, mixc_ref, u_ref) = refs[:12]
    refs = refs[12:]
    if not sample:
        (carry_ref,) = refs

    i = pl.program_id(0)
    j = pl.program_id(1)
    nt = (((1,), (1,)), ((), ()))

    def xw(w):
        return lax.dot_general(xn_ref[...], w, nt, preferred_element_type=F32)

    def wx(w):
        return lax.dot_general(w, xn_ref[...], nt, preferred_element_type=F32)

    if not sample:
        @pl.when((i == 0) & (j == 0))
        def _init():
            carry_ref[...] = jnp.zeros(carry_ref.shape, F32)

    @pl.when(j < J_KV)
    def _q():
        res = wx(wa_ref[...])
        cos = c16_ref[...]
        sin = s16_ref[...]
        for hh in range(PROJ_TN // HEAD_DIM):
            blk = res[hh * HEAD_DIM:(hh + 1) * HEAD_DIM]
            ms = jnp.mean(blk * blk, axis=0, keepdims=True)
            y = blk * lax.rsqrt(ms + EPS) * gq_ref[...]
            x1 = y[0:ROPE_HALF]
            x2 = y[ROPE_HALF:ROPE_DIM]
            base = hh * HEAD_DIM
            qT_ref[base:base + ROPE_HALF, :] = ((x1 * cos - x2 * sin) * Q_PRESCALE).astype(BF16)
            qT_ref[base + ROPE_HALF:base + ROPE_DIM, :] = ((x2 * cos + x1 * sin) * Q_PRESCALE).astype(BF16)
            qT_ref[base + ROPE_DIM:base + HEAD_DIM, :] = (y[ROPE_DIM:] * Q_PRESCALE).astype(BF16)

    @pl.when((j >= J_QI) & (j < J_CONV))
    def _qi():
        res = wx(wa_ref[...])
        cos = c8_ref[...]
        sin = s8_ref[...]
        for hh in range(PROJ_TN // IDX_DIM):
            blk = res[hh * IDX_DIM:(hh + 1) * IDX_DIM]
            x1 = blk[0:IDX_ROPE_HALF]
            x2 = blk[IDX_ROPE_HALF:2 * IDX_ROPE_HALF]
            rot = jnp.concatenate([x1 * cos - x2 * sin, x2 * cos + x1 * sin], axis=0)
            base = hh * IDX_DIM
            qiT_ref[base:base + 2 * IDX_ROPE_HALF, :] = rot.astype(BF16)
            qiT_ref[base + 2 * IDX_ROPE_HALF:base + IDX_DIM, :] = blk[2 * IDX_ROPE_HALF:].astype(BF16)

    @pl.when(j == J_KV)
    def _kv():
        res = xw(wa_ref[...])
        for hd in range(N_KV_HEADS):
            hs = slice(hd * HEAD_DIM, (hd + 1) * HEAD_DIM)
            blk = res[:, hs]
            ms = jnp.mean(blk * blk, axis=-1, keepdims=True)
            yk = blk * lax.rsqrt(ms + EPS) * gk_ref[...]
            rot = yk * kc_ref[...] + (pltpu.roll(yk, HEAD_DIM - ROPE_HALF, axis=1) * ksa_ref[...]
                                      + pltpu.roll(yk, ROPE_HALF, axis=1) * ksb_ref[...])
            k32_ref[:, hs] = rot
            kbf_ref[:, hs] = rot.astype(BF16)
        v = res[:, N_KV:2 * N_KV]
        v32_ref[...] = v
        if sample:
            vT_ref[...] = wx(wa_ref[N_KV:2 * N_KV, :]).astype(BF16)
        else:
            vT_ref[...] = v.T.astype(BF16)
        wT_ref[...] = wx(wkiw_ref[IDX_DIM:IDX_DIM + N_IDX_HEADS, :]) * W_IDX_SCALE
        r2 = xw(wkiw_ref[...])
        lane = lax.broadcasted_iota(I32, r2.shape, 1)
        r2 = jnp.where(lane < IDX_DIM, r2, 0.0)
        ms = jnp.sum(r2 * r2, axis=-1, keepdims=True) * (1.0 / IDX_DIM)
        yi = r2 * lax.rsqrt(ms + EPS) * gki_ref[...]
        roti = yi * kic_ref[...] + (pltpu.roll(yi, KIW_ROWS - IDX_ROPE_HALF, axis=1) * kisa_ref[...]
                                    + pltpu.roll(yi, IDX_ROPE_HALF, axis=1) * kisb_ref[...])
        ki32_ref[...] = roti[:, 0:IDX_DIM]
        kibf_ref[...] = roti.astype(BF16)

    @pl.when((j >= J_Z) & (j < J_QI))
    def _z():
        gate_ref[...] = _silu(xw(wa_ref[...]))

    @pl.when(j >= J_CONV)
    def _conv():
        cc = j - J_CONV
        u = xw(wc_ref[...]) * xw(wh_ref[...])
        rowid = lax.broadcasted_iota(I32, (tm, CONV_CHUNK), 0)
        if sample:
            t = rowid & (sample - 1)
            u1 = jnp.where(t >= 1, pltpu.roll(u, 1, axis=0), e1_ref[...])
            u2 = jnp.where(t >= 2, pltpu.roll(u, 2, axis=0), e2_ref[...])
            u_ref[...] = u
        else:
            first = (i % tps) == 0
            prev = carry_ref[cc]
            p0 = jnp.where(first, 0.0, prev[0:1])
            p1 = jnp.where(first, 0.0, prev[1:2])
            u1 = jnp.where(rowid == 0, p1, pltpu.roll(u, 1, axis=0))
            u2 = jnp.where(rowid == 0, p0, jnp.where(rowid == 1, p1, pltpu.roll(u, 2, axis=0)))
            tail = u[tm - 8:tm]
            carry_ref[cc] = jnp.concatenate([tail[6:8], tail[0:6]], axis=0)
            u_ref[0] = tail[6:8]
        w = wconv_ref[...]
        y = u2 * w[0:1] + u1 * w[1:2] + u * w[2:3]
        mixc_ref[...] = (xw(wb_ref[...]) * y * _silu(xw(wzc_ref[...]))).astype(BF16)


def _project(xn, wT, params, tabs, sample, state_rows=None):
    m = xn.shape[0]
    assert sample & (sample - 1) == 0
    tm = m if sample else PROJ_TM
    n_i = m // tm
    tps = 1 if sample else (tabs["c16T"].shape[1] // tm)
    gq, gk, gki, wconv = params

    def tmap(i):
        return i % tps

    def cchunk(j):
        return jnp.clip(j - J_CONV, 0, N_CONV_CHUNKS - 1)

    def conv_rows(off):
        return pl.BlockSpec((pl.Element(CONV_CHUNK), pl.Element(D_MODEL)),
                            lambda i, j: (pl.multiple_of(off + CONV_CHUNK * cchunk(j), 16), 0))

    in_specs = [
        pl.BlockSpec((tm, D_MODEL), lambda i, j: (i, 0)),
        pl.BlockSpec((PROJ_TN, D_MODEL), lambda i, j: (jnp.minimum(j, N_MAIN_BLOCKS - 1), 0)),
        pl.BlockSpec((KIW_ROWS, D_MODEL), lambda i, j: (OFF_KI // KIW_ROWS, 0)),
        conv_rows(OFF_H), conv_rows(OFF_B), conv_rows(OFF_C), conv_rows(OFF_ZC),
        pl.BlockSpec((HEAD_DIM, 1), lambda i, j: (0, 0)),
        pl.BlockSpec((1, HEAD_DIM), lambda i, j: (0, 0)),
        pl.BlockSpec((1, KIW_ROWS), lambda i, j: (0, 0)),
        pl.BlockSpec((ROPE_HALF, tm), lambda i, j: (0, tmap(i))),
        pl.BlockSpec((ROPE_HALF, tm), lambda i, j: (0, tmap(i))),
        pl.BlockSpec((IDX_ROPE_HALF, tm), lambda i, j: (0, tmap(i))),
        pl.BlockSpec((IDX_ROPE_HALF, tm), lambda i, j: (0, tmap(i))),
        pl.BlockSpec((tm, HEAD_DIM), lambda i, j: (tmap(i), 0)),
        pl.BlockSpec((tm, HEAD_DIM), lambda i, j: (tmap(i), 0)),
        pl.BlockSpec((tm, HEAD_DIM), lambda i, j: (tmap(i), 0)),
        pl.BlockSpec((tm, KIW_ROWS), lambda i, j: (tmap(i), 0)),
        pl.BlockSpec((tm, KIW_ROWS), lambda i, j: (tmap(i), 0)),
        pl.BlockSpec((tm, KIW_ROWS), lambda i, j: (tmap(i), 0)),
        pl.BlockSpec((CONV_W, CONV_CHUNK), lambda i, j: (0, cchunk(j))),
    ]
    args = [xn, wT, wT, wT, wT, wT, wT, gq, gk, gki,
            tabs["c16T"], tabs["s16T"], tabs["c8T"], tabs["s8T"],
            tabs["kC"], tabs["kSa"], tabs["kSb"], tabs["kiC"], tabs["kiSa"], tabs["kiSb"], wconv]
    cmap = lambda i, j: (i, cchunk(j))
    if sample:
        in_specs += [pl.BlockSpec((tm, CONV_CHUNK), cmap), pl.BlockSpec((tm, CONV_CHUNK), cmap)]
        args += list(state_rows)
        u_spec = pl.BlockSpec((tm, CONV_CHUNK), cmap)
        u_shape = jax.ShapeDtypeStruct((m, D_CONV), F32)
        scratch = []
    else:
        u_spec = pl.BlockSpec((1, CONV_W - 1, CONV_CHUNK), lambda i, j: (i, 0, cchunk(j)))
        u_shape = jax.ShapeDtypeStruct((n_i, CONV_W - 1, D_CONV), F32)
        scratch = [pltpu.VMEM((N_CONV_CHUNKS, 8, CONV_CHUNK), F32)]

    out_specs = [
        pl.BlockSpec((PROJ_TN, tm), lambda i, j: (jnp.minimum(j, 1), i)),
        pl.BlockSpec((PROJ_TN, tm), lambda i, j: (jnp.clip(j - J_QI, 0, 1), i)),
        pl.BlockSpec((N_KV, tm), lambda i, j: (0, i)),
        pl.BlockSpec((N_IDX_HEADS, tm), lambda i, j: (0, i)),
        pl.BlockSpec((tm, N_KV), lambda i, j: (i, 0)),
        pl.BlockSpec((tm, N_KV), lambda i, j: (i, 0)),
        pl.BlockSpec((tm, N_KV), lambda i, j: (i, 0)),
        pl.BlockSpec((tm, IDX_DIM), lambda i, j: (i, 0)),
        pl.BlockSpec((tm, KIW_ROWS), lambda i, j: (i, 0)),
        pl.BlockSpec((tm, PROJ_TN), lambda i, j: (i, jnp.clip(j - J_Z, 0, 1))),
        pl.BlockSpec((tm, CONV_CHUNK), cmap),
        u_spec,
    ]
    out_shape = [
        jax.ShapeDtypeStruct((D_ATTN, m), BF16),
        jax.ShapeDtypeStruct((N_IDX_HEADS * IDX_DIM, m), BF16),
        jax.ShapeDtypeStruct((N_KV, m), BF16),
        jax.ShapeDtypeStruct((N_IDX_HEADS, m), F32),
        jax.ShapeDtypeStruct((m, N_KV), F32),
        jax.ShapeDtypeStruct((m, N_KV), BF16),
        jax.ShapeDtypeStruct((m, N_KV), F32),
        jax.ShapeDtypeStruct((m, IDX_DIM), F32),
        jax.ShapeDtypeStruct((m, KIW_ROWS), BF16),
        jax.ShapeDtypeStruct((m, D_ATTN), F32),
        jax.ShapeDtypeStruct((m, D_CONV), BF16),
        u_shape,
    ]
    return pl.pallas_call(
        functools.partial(_proj_body, sample, tps, tm),
        grid=(n_i, N_STEPS),
        in_specs=in_specs,
        out_specs=out_specs,
        out_shape=out_shape,
        scratch_shapes=scratch,
        compiler_params=_cparams(2),
        name="proj_sample" if sample else "proj_prompt",
    )(*args)


def _sort_key(s):
    bits = pltpu.bitcast(s, I32)
    return bits ^ ((bits >> 31) & 0x7FFFFFFF)


def _bit_value(b):
    return lax.shift_left(jnp.int32(1), jnp.int32(31) - b)


def _attn_prompt_body(qiT_ref, wT_ref, ki_ref, qT_ref, k_ref, vT_ref, gate_ref, o_ref,
                      keys_ref, bias_ref, m_ref, l_ref, acc_ref, s_ref, p_ref):
    tq, tk = ATT_TQ, ATT_TK
    i = pl.program_id(1)
    nch = i + 1
    w = wT_ref[...]
    row = lax.broadcasted_iota(I32, (tk, tq), 0)
    col = lax.broadcasted_iota(I32, (tk, tq), 1)

    def score_chunk(j, carry):
        off = pl.multiple_of(j * tk, tk)
        kic = ki_ref[pl.ds(off, tk), 0:IDX_DIM]
        acc = jnp.zeros((tk, tq), F32)
        for h in range(N_IDX_HEADS):
            d = jnp.dot(kic, qiT_ref[h * IDX_DIM:(h + 1) * IDX_DIM, :], preferred_element_type=F32)
            acc = acc + w[h:h + 1, :] * jnp.maximum(d, 0.0)
        key = _sort_key(acc)
        future = (row + j * tk) > (col + i * tq)
        keys_ref[pl.ds(off, tk), :] = jnp.where(future, INT_MIN, key)
        return carry

    lax.fori_loop(0, nch, score_chunk, 0)

    def bit_body(b, thr):
        cand = thr + _bit_value(b)

        def cnt_chunk(j, cs):
            off = pl.multiple_of(j * tk, tk)
            cs = list(cs)
            kc = keys_ref[pl.ds(off, tk), :]
            for r in range(tk // 8):
                a = cs[r % CNT_ACCS]
                cs[r % CNT_ACCS] = jnp.where(kc[r * 8:(r + 1) * 8] >= cand, a + 1, a)
            return tuple(cs)

        cs = lax.fori_loop(0, nch, cnt_chunk, (jnp.zeros((8, tq), I32),) * CNT_ACCS)
        c = cs[0]
        for a in cs[1:]:
            c = c + a
        cnt = jnp.sum(c.astype(F32), axis=0, keepdims=True)
        return jnp.where(cnt >= float(TOPK_MAX), cand, thr)

    thr = lax.fori_loop(0, 32, bit_body, jnp.full((1, tq), INT_MIN, I32))
    thr = jnp.maximum(thr, INT_MIN + 1)

    def bias_chunk(j, carry):
        off = pl.multiple_of(j * tk, tk)
        kk = keys_ref[pl.ds(off, tk), :]
        bias_ref[pl.ds(off, tk), :] = jnp.where(kk >= thr, 0.0, NEG)
        return carry

    lax.fori_loop(0, nch, bias_chunk, 0)

    m_ref[...] = jnp.full(m_ref.shape, NEG, F32)
    l_ref[...] = jnp.zeros(l_ref.shape, F32)
    acc_ref[...] = jnp.zeros(acc_ref.shape, F32)

    def att_chunk(j, carry):
        off = pl.multiple_of(j * tk, tk)
        bias = bias_ref[pl.ds(off, tk), :]
        for h in range(N_HEADS):
            g = h // GROUP
            kc = k_ref[pl.ds(off, tk), g * HEAD_DIM:(g + 1) * HEAD_DIM]
            s_ref[h * tk:(h + 1) * tk, :] = jnp.dot(
                kc, qT_ref[h * HEAD_DIM:(h + 1) * HEAD_DIM, :], preferred_element_type=F32) + bias
        alphas = []
        for h in range(N_HEADS):
            mloc = None
            for r in range(tk // ATT_SUB):
                blk = s_ref[h * tk + r * ATT_SUB:h * tk + (r + 1) * ATT_SUB, :]
                bm = blk.reshape(ATT_SUB // 8, 8, tq).max(axis=0)
                mloc = bm if mloc is None else jnp.maximum(mloc, bm)
            m_old = m_ref[h:h + 1, :]
            m_new = jnp.maximum(m_old, jnp.max(mloc, axis=0, keepdims=True))
            alpha = jnp.exp2(m_old - m_new)
            lsum = None
            for r in range(tk // ATT_SUB):
                rs = slice(h * tk + r * ATT_SUB, h * tk + (r + 1) * ATT_SUB)
                p = jnp.exp2(s_ref[rs, :] - m_new)
                ps = p.reshape(ATT_SUB // 8, 8, tq).sum(axis=0)
                lsum = ps if lsum is None else lsum + ps
                p_ref[rs, :] = p.astype(BF16)
            l_ref[h:h + 1, :] = l_ref[h:h + 1, :] * alpha + jnp.sum(lsum, axis=0, keepdims=True)
            m_ref[h:h + 1, :] = m_new
            alphas.append(alpha)
        for h in range(N_HEADS):
            g = h // GROUP
            hs = slice(h * HEAD_DIM, (h + 1) * HEAD_DIM)
            vc = vT_ref[g * HEAD_DIM:(g + 1) * HEAD_DIM, pl.ds(off, tk)]
            acc_ref[hs, :] = acc_ref[hs, :] * alphas[h] + jnp.dot(
                vc, p_ref[h * tk:(h + 1) * tk, :], preferred_element_type=F32)
        return carry

    lax.fori_loop(0, nch, att_chunk, 0)

    for h in range(N_HEADS):
        hs = slice(h * HEAD_DIM, (h + 1) * HEAD_DIM)
        o = (acc_ref[hs, :] / l_ref[h:h + 1, :]).T
        o_ref[:, hs] = (o * gate_ref[:, hs]).astype(BF16)


def _attn_prompt(qiT, wT, kibf, qT, kbf, vT, gate, n_batch, seq):
    m = n_batch * seq
    nq = seq // ATT_TQ
    nkv = N_KV_HEADS * HEAD_DIM
    qmap = lambda b, i: (0, b * nq + i)
    return pl.pallas_call(
        _attn_prompt_body,
        grid=(n_batch, nq),
        in_specs=[
            pl.BlockSpec((N_IDX_HEADS * IDX_DIM, ATT_TQ), qmap),
            pl.BlockSpec((N_IDX_HEADS, ATT_TQ), qmap),
            pl.BlockSpec((seq, 128), lambda b, i: (b, 0)),
            pl.BlockSpec((D_ATTN, ATT_TQ), qmap),
            pl.BlockSpec((seq, nkv), lambda b, i: (b, 0)),
            pl.BlockSpec((nkv, seq), lambda b, i: (0, b)),
            pl.BlockSpec((ATT_TQ, D_ATTN), lambda b, i: (b * nq + i, 0)),
        ],
        out_specs=pl.BlockSpec((ATT_TQ, D_ATTN), lambda b, i: (b * nq + i, 0)),
        out_shape=jax.ShapeDtypeStruct((m, D_ATTN), BF16),
        scratch_shapes=[pltpu.VMEM((seq, ATT_TQ), I32), pltpu.VMEM((seq, ATT_TQ), F32),
                        pltpu.VMEM((N_HEADS, ATT_TQ), F32), pltpu.VMEM((N_HEADS, ATT_TQ), F32),
                        pltpu.VMEM((D_ATTN, ATT_TQ), F32),
                        pltpu.VMEM((N_HEADS * ATT_TK, ATT_TQ), F32),
                        pltpu.VMEM((N_HEADS * ATT_TK, ATT_TQ), BF16)],
        compiler_params=_cparams(2),
        name="attn_prompt",
    )(qiT, wT, kibf, qT, kbf, vT, gate)


def _s_score_body(pt_ref, qi_ref, wcol_ref, kinew_ref, *rest):
    npg = S_SCORE_PAGES
    pages = rest[:npg]
    out_ref, outnew_ref = rest[npg:]
    pg = pl.program_id(1)
    qi = qi_ref[0]
    wc = wcol_ref[0]

    def scores(keys_t):
        d = jnp.dot(qi, keys_t, preferred_element_type=F32)
        val = jnp.maximum(d, 0.0) * wc
        return val.reshape(N_IDX_HEADS, 8, val.shape[-1]).sum(axis=0)

    for r in range(npg):
        out_ref[0, :, r * PAGE_SIZE:(r + 1) * PAGE_SIZE] = scores(pages[r][...].astype(BF16))

    @pl.when(pg == 0)
    def _new():
        outnew_ref[0] = scores(kinew_ref[0])


def _s_thresh_body(n_new, sp_ref, sn_ref, bp_ref, bn_ref, keys_ref):
    rows, past = sp_ref.shape
    ch = 2048
    nchunk = past // ch

    def to_keys(c, carry):
        off = pl.multiple_of(c * ch, ch)
        keys_ref[:, pl.ds(off, ch)] = _sort_key(sp_ref[:, pl.ds(off, ch)])
        return carry

    lax.fori_loop(0, nchunk, to_keys, 0)
    t = lax.broadcasted_iota(I32, (rows, 128), 0) & 7
    lane = lax.broadcasted_iota(I32, (rows, 128), 1)
    kn = jnp.where((lane < n_new) & (lane <= t), _sort_key(sn_ref[...]), INT_MIN)

    def fold(x):
        f = x[:, 0:128]
        for q in range(1, x.shape[1] // 128):
            f = f + x[:, q * 128:(q + 1) * 128]
        return f

    def bit_body(b, thr):
        cand = thr + _bit_value(b)

        def cnt_chunk(c, acc):
            off = pl.multiple_of(c * ch, ch)
            kk = keys_ref[:, pl.ds(off, ch)]
            return acc + fold(jnp.where(kk >= cand, 1.0, 0.0))

        acc = lax.fori_loop(0, nchunk, cnt_chunk, jnp.where(kn >= cand, 1.0, 0.0))
        cnt = jnp.sum(acc, axis=1, keepdims=True)
        return jnp.where(cnt >= float(TOPK_MAX), cand, thr)

    thr = lax.fori_loop(0, 32, bit_body, jnp.full((rows, 1), INT_MIN, I32))
    thr = jnp.maximum(thr, INT_MIN + 1)

    def to_bias(c, carry):
        off = pl.multiple_of(c * ch, ch)
        bp_ref[:, pl.ds(off, ch)] = jnp.where(keys_ref[:, pl.ds(off, ch)] >= thr, 0.0, NEG)
        return carry

    lax.fori_loop(0, nchunk, to_bias, 0)
    bn_ref[...] = jnp.where(kn >= thr, 0.0, NEG)


def _s_attn_body(n_steps, pt_ref, q_ref, bp_ref, bn_ref, knew_ref, vnew_ref, *rest):
    npg = S_ATTN_PAGES
    kpages = rest[:npg]
    vpages = rest[npg:2 * npg]
    out_ref, s_ref, p_ref, l_ref, acc_ref = rest[2 * npg:]
    t = pl.program_id(1)
    nt = (((1,), (1,)), ((), ()))
    rg = GROUP * 8
    width = npg * PAGE_SIZE
    past = n_steps * width
    sm_chunk = 2048

    def head_rows(pages, g):
        rows = [pg_ref[pl.ds(g, PAGE_SIZE, stride=N_KV_HEADS), :] for pg_ref in pages]
        return jnp.concatenate(rows, axis=0).astype(BF16)

    @pl.when(t < n_steps)
    def _logits():
        off = pl.multiple_of(t * width, width)
        bias = jnp.concatenate([bp_ref[0]] * GROUP, axis=0)
        for g in range(N_KV_HEADS):
            qg = q_ref[0, g * rg:(g + 1) * rg, :]
            s_ref[g * rg:(g + 1) * rg, pl.ds(off, width)] = lax.dot_general(
                qg, head_rows(kpages, g), nt, preferred_element_type=F32) + bias

    @pl.when(t == n_steps - 1)
    def _softmax():
        biasn = jnp.concatenate([bn_ref[0]] * GROUP, axis=0)
        for g in range(N_KV_HEADS):
            qg = q_ref[0, g * rg:(g + 1) * rg, :]
            kg = knew_ref[0, :, g * HEAD_DIM:(g + 1) * HEAD_DIM]
            s_ref[g * rg:(g + 1) * rg, past:past + PAGE_SIZE] = lax.dot_general(
                qg, kg, nt, preferred_element_type=F32) + biasn

        def fold(x, op):
            f = x[:, 0:128]
            for q in range(1, x.shape[1] // 128):
                f = op(f, x[:, q * 128:(q + 1) * 128])
            return f

        def max_chunk(c, m):
            off = pl.multiple_of(c * sm_chunk, sm_chunk)
            return jnp.maximum(m, fold(s_ref[:, pl.ds(off, sm_chunk)], jnp.maximum))

        m = lax.fori_loop(0, past // sm_chunk, max_chunk, s_ref[:, past:past + PAGE_SIZE])
        m = jnp.max(m, axis=1, keepdims=True)

        def exp_chunk(c, l):
            off = pl.multiple_of(c * sm_chunk, sm_chunk)
            p = jnp.exp2(s_ref[:, pl.ds(off, sm_chunk)] - m)
            p_ref[:, pl.ds(off, sm_chunk)] = p.astype(BF16)
            return l + fold(p, jnp.add)

        pn = jnp.exp2(s_ref[:, past:past + PAGE_SIZE] - m)
        p_ref[:, past:past + PAGE_SIZE] = pn.astype(BF16)
        l_ref[...] = lax.fori_loop(0, past // sm_chunk, exp_chunk, pn)
        acc_ref[...] = jnp.zeros(acc_ref.shape, F32)

    @pl.when(t >= n_steps)
    def _values():
        off = pl.multiple_of((t - n_steps) * width, width)
        for g in range(N_KV_HEADS):
            rs = slice(g * rg, (g + 1) * rg)
            acc_ref[rs, :] += jnp.dot(p_ref[rs, pl.ds(off, width)], head_rows(vpages, g),
                                      preferred_element_type=F32)

    @pl.when(t == 2 * n_steps - 1)
    def _finish():
        for g in range(N_KV_HEADS):
            rs = slice(g * rg, (g + 1) * rg)
            vg = vnew_ref[0, :, g * HEAD_DIM:(g + 1) * HEAD_DIM]
            acc_ref[rs, :] += jnp.dot(p_ref[rs, past:past + PAGE_SIZE], vg, preferred_element_type=F32)
        out_ref[0] = acc_ref[...] / jnp.sum(l_ref[...], axis=1, keepdims=True)


def _attn_sample(qT, qiT, wT, kibf, kbf, v32, cache_k, cache_v, cache_kidx, page_table, n_seq, n_tok):
    n_pages = page_table.shape[1]
    past = n_pages * PAGE_SIZE
    npg = S_SCORE_PAGES
    n_steps = n_pages // npg
    n_pool = cache_k.shape[0]

    def rows_ht(xT, n_heads, dim):
        x = xT.reshape(n_heads, dim, n_seq, n_tok).transpose(2, 0, 3, 1)
        x = jnp.pad(x, ((0, 0), (0, 0), (0, 8 - n_tok), (0, 0)))
        return x.reshape(n_seq, n_heads * 8, dim)

    qi_rows = rows_ht(qiT, N_IDX_HEADS, IDX_DIM)
    q_rows = rows_ht(qT, N_HEADS, HEAD_DIM)
    wcol = jnp.pad(wT.reshape(N_IDX_HEADS, n_seq, n_tok).transpose(1, 0, 2),
                   ((0, 0), (0, 0), (0, 8 - n_tok))).reshape(n_seq, N_IDX_HEADS * 8, 1)

    def pad_keys(x):
        x = x.reshape(n_seq, n_tok, x.shape[-1])
        return jnp.pad(x, ((0, 0), (0, PAGE_SIZE - n_tok), (0, 0)))

    ki_new_t = jnp.swapaxes(pad_keys(kibf[:, 0:IDX_DIM]), 1, 2)
    kidx_t = jnp.swapaxes(cache_kidx, 1, 2)
    k_new = pad_keys(kbf)
    v_new = pad_keys(v32.astype(BF16))

    def page_spec(shape, r):
        return pl.BlockSpec((None,) + shape, lambda b, pg, pt, r=r: (pt[b, pg * npg + r], 0, 0))

    def kv_page_spec(shape, r, n, is_v):
        def imap(b, t, pt):
            step = jnp.clip(t - n, 0, n - 1) if is_v else jnp.minimum(t, n - 1)
            return (pt[b, step * S_ATTN_PAGES + r], 0, 0)
        return pl.BlockSpec((None,) + shape, imap)

    sp, sn = pl.pallas_call(
        _s_score_body,
        grid_spec=pltpu.PrefetchScalarGridSpec(
            num_scalar_prefetch=1,
            grid=(n_seq, n_steps),
            in_specs=[
                pl.BlockSpec((1, N_IDX_HEADS * 8, IDX_DIM), lambda b, pg, pt: (b, 0, 0)),
                pl.BlockSpec((1, N_IDX_HEADS * 8, 1), lambda b, pg, pt: (b, 0, 0)),
                pl.BlockSpec((1, IDX_DIM, PAGE_SIZE), lambda b, pg, pt: (b, 0, 0)),
            ] + [page_spec((IDX_DIM, PAGE_SIZE), r) for r in range(npg)],
            out_specs=[
                pl.BlockSpec((1, 8, npg * PAGE_SIZE), lambda b, pg, pt: (b, 0, pg)),
                pl.BlockSpec((1, 8, PAGE_SIZE), lambda b, pg, pt: (b, 0, 0)),
            ],
        ),
        out_shape=[jax.ShapeDtypeStruct((n_seq, 8, past), F32),
                   jax.ShapeDtypeStruct((n_seq, 8, PAGE_SIZE), F32)],
        compiler_params=_cparams(2),
        name="sample_scores",
    )(page_table, qi_rows, wcol, ki_new_t, *([kidx_t] * npg))

    rows = n_seq * 8
    bp, bn = pl.pallas_call(
        functools.partial(_s_thresh_body, n_tok),
        grid=(1,),
        in_specs=[pl.BlockSpec((rows, past), lambda i: (0, 0)),
                  pl.BlockSpec((rows, PAGE_SIZE), lambda i: (0, 0))],
        out_specs=[pl.BlockSpec((rows, past), lambda i: (0, 0)),
                   pl.BlockSpec((rows, PAGE_SIZE), lambda i: (0, 0))],
        out_shape=[jax.ShapeDtypeStruct((rows, past), F32),
                   jax.ShapeDtypeStruct((rows, PAGE_SIZE), F32)],
        scratch_shapes=[pltpu.VMEM((rows, past), I32)],
        compiler_params=_cparams(1),
        name="sample_topk_mask",
    )(sp.reshape(rows, past), sn.reshape(rows, PAGE_SIZE))
    bp = bp.reshape(n_seq, 8, past)
    bn = bn.reshape(n_seq, 8, PAGE_SIZE)

    nrow = N_HEADS * 8
    kv_rows = PAGE_SIZE * N_KV_HEADS
    ck = cache_k.reshape(n_pool, kv_rows, HEAD_DIM)
    cv = cache_v.reshape(n_pool, kv_rows, HEAD_DIM)
    na = n_pages // S_ATTN_PAGES
    width = S_ATTN_PAGES * PAGE_SIZE
    out = pl.pallas_call(
        functools.partial(_s_attn_body, na),
        grid_spec=pltpu.PrefetchScalarGridSpec(
            num_scalar_prefetch=1,
            grid=(n_seq, 2 * na),
            in_specs=[
                pl.BlockSpec((1, nrow, HEAD_DIM), lambda b, t, pt: (b, 0, 0)),
                pl.BlockSpec((1, 8, width), lambda b, t, pt: (b, 0, jnp.minimum(t, na - 1))),
                pl.BlockSpec((1, 8, PAGE_SIZE), lambda b, t, pt: (b, 0, 0)),
                pl.BlockSpec((1, PAGE_SIZE, N_KV), lambda b, t, pt: (b, 0, 0)),
                pl.BlockSpec((1, PAGE_SIZE, N_KV), lambda b, t, pt: (b, 0, 0)),
            ] + [kv_page_spec((kv_rows, HEAD_DIM), r, na, False) for r in range(S_ATTN_PAGES)]
              + [kv_page_spec((kv_rows, HEAD_DIM), r, na, True) for r in range(S_ATTN_PAGES)],
            out_specs=pl.BlockSpec((1, nrow, HEAD_DIM), lambda b, t, pt: (b, 0, 0)),
            scratch_shapes=[pltpu.VMEM((nrow, past + PAGE_SIZE), F32),
                            pltpu.VMEM((nrow, past + PAGE_SIZE), BF16),
                            pltpu.VMEM((nrow, 128), F32), pltpu.VMEM((nrow, HEAD_DIM), F32)],
        ),
        out_shape=jax.ShapeDtypeStruct((n_seq, nrow, HEAD_DIM), F32),
        compiler_params=_cparams(2),
        name="sample_attn",
    )(page_table, q_rows, bp, bn, k_new, v_new, *([ck] * S_ATTN_PAGES), *([cv] * S_ATTN_PAGES))
    out = out.reshape(n_seq, N_HEADS, 8, HEAD_DIM)[:, :, 0:n_tok]
    return out.transpose(0, 2, 1, 3).reshape(n_seq * n_tok, D_ATTN)


def _outproj_body(gated, *refs):
    if gated:
        x_ref, a_ref, gate_ref, mc_ref, wo_ref, o_ref = refs
        ma = (a_ref[...] * gate_ref[...]).astype(BF16)
    else:
        x_ref, a_ref, mc_ref, wo_ref, o_ref = refs
        ma = a_ref[...]
    acc = jnp.dot(ma, wo_ref[0:D_ATTN, :], preferred_element_type=F32)
    acc = acc + jnp.dot(mc_ref[...], wo_ref[D_ATTN:D_ATTN + D_CONV, :], preferred_element_type=F32)
    o_ref[...] = x_ref[...] + acc


def _outproj(x, attn, gate, mixc, wo, tm):
    m = x.shape[0]
    gated = gate is not None
    row = lambda i: (i, 0)
    in_specs = [pl.BlockSpec((tm, D_MODEL), row), pl.BlockSpec((tm, D_ATTN), row)]
    args = [x, attn]
    if gated:
        in_specs.append(pl.BlockSpec((tm, D_ATTN), row))
        args.append(gate)
    in_specs += [pl.BlockSpec((tm, D_CONV), row),
                 pl.BlockSpec((D_ATTN + D_CONV, D_MODEL), lambda i: (0, 0))]
    args += [mixc, wo]
    return pl.pallas_call(
        functools.partial(_outproj_body, gated),
        grid=(m // tm,),
        in_specs=in_specs,
        out_specs=pl.BlockSpec((tm, D_MODEL), row),
        out_shape=jax.ShapeDtypeStruct((m, D_MODEL), F32),
        compiler_params=_cparams(1),
        name="outproj_sample" if gated else "outproj_prompt",
    )(*args)


def _rope_tables(pos):
    posf = pos.astype(F32)[:, None]
    n = pos.shape[0]

    def cs(half):
        inv = ROPE_THETA ** (-jnp.arange(half, dtype=F32) / half)
        ang = posf * inv[None, :]
        return jnp.cos(ang), jnp.sin(ang)

    c16, s16 = cs(ROPE_HALF)
    c8, s8 = cs(IDX_ROPE_HALF)
    one = lambda w: jnp.ones((n, w), F32)
    zero = lambda w: jnp.zeros((n, w), F32)
    rest = HEAD_DIM - ROPE_DIM
    k_c = jnp.concatenate([c16, c16, one(rest)], axis=1)
    k_sa = jnp.concatenate([-s16, zero(HEAD_DIM - ROPE_HALF)], axis=1)
    k_sb = jnp.concatenate([zero(ROPE_HALF), s16, zero(rest)], axis=1)
    ki_c = jnp.concatenate([c8, c8, one(128 - 2 * IDX_ROPE_HALF)], axis=1)
    ki_sa = jnp.concatenate([-s8, zero(128 - IDX_ROPE_HALF)], axis=1)
    ki_sb = jnp.concatenate([zero(IDX_ROPE_HALF), s8, zero(128 - 2 * IDX_ROPE_HALF)], axis=1)
    return dict(c16T=c16.T, s16T=s16.T, c8T=c8.T, s8T=s8.T,
                kC=k_c, kSa=k_sa, kSb=k_sb, kiC=ki_c, kiSa=ki_sa, kiSb=ki_sb)


def _prep_weights(w_in, g_q, g_k, g_kidx, w_conv, w_out):
    assert w_in.shape == (D_MODEL, D_IN)
    wT = w_in.T.astype(BF16)
    gq = g_q.reshape(HEAD_DIM, 1)
    gk = g_k.reshape(1, HEAD_DIM)
    gki = jnp.concatenate([g_kidx.reshape(1, IDX_DIM), jnp.zeros((1, KIW_ROWS - IDX_DIM), F32)], axis=1)
    return wT, (gq, gk, gki, w_conv), w_out.astype(BF16)


def kernel(x_prompt, x_sample, cache_k, cache_v, cache_kidx, state_conv, page_table,
           norm_in, w_in, g_q, g_k, g_kidx, w_conv, w_out):
    n_b, seq, _ = x_prompt.shape
    n_s, n_t, _ = x_sample.shape
    depth = w_in.shape[0]
    past = page_table.shape[1] * PAGE_SIZE
    tabs_p = _rope_tables(jnp.arange(seq))
    tabs_s = _rope_tables(jnp.tile(past + jnp.arange(n_t), n_s))

    hp = x_prompt.reshape(n_b * seq, D_MODEL)
    hs = x_sample.reshape(n_s * n_t, D_MODEL)
    outs = [[] for _ in range(8)]
    for l in range(depth):
        w_t, params, wo = _prep_weights(w_in[l], g_q[l], g_k[l], g_kidx[l], w_conv[l], w_out[l])

        xn = _rmsnorm(hp, norm_in[l], ROW_TM)
        (qT, qiT, vT, wT, k32, kbf, v32, ki32, kibf, gate, mixc, utail) = _project(
            xn, w_t, params, tabs_p, 0)
        mixa = _attn_prompt(qiT, wT, kibf, qT, kbf, vT, gate, n_b, seq)
        hp = _outproj(hp, mixa, None, mixc, wo, ROW_TM)
        tps = seq // PROJ_TM
        outs[0].append(k32.reshape(n_b, seq, N_KV_HEADS, HEAD_DIM))
        outs[1].append(v32.reshape(n_b, seq, N_KV_HEADS, HEAD_DIM))
        outs[2].append(ki32.reshape(n_b, seq, IDX_DIM))
        outs[3].append(utail[tps - 1::tps])

        st = state_conv[l]
        tok = jnp.arange(n_t)
        e1 = st[:, jnp.full((n_t,), CONV_W - 2)].reshape(n_s * n_t, D_CONV)
        e2 = st[:, jnp.minimum(tok, CONV_W - 2)].reshape(n_s * n_t, D_CONV)
        xn = _rmsnorm(hs, norm_in[l], n_s * n_t)
        (qT, qiT, vT, wT, k32, kbf, v32, ki32, kibf, gate, mixc, u) = _project(
            xn, w_t, params, tabs_s, n_t, state_rows=(e1, e2))
        attn = _attn_sample(qT, qiT, wT, kibf, kbf, v32, cache_k[l], cache_v[l], cache_kidx[l],
                            page_table, n_s, n_t)
        hs = _outproj(hs, attn, gate, mixc, wo, n_s * n_t)
        outs[4].append(k32.reshape(n_s, n_t, N_KV_HEADS, HEAD_DIM))
        outs[5].append(v32.reshape(n_s, n_t, N_KV_HEADS, HEAD_DIM))
        outs[6].append(ki32.reshape(n_s, n_t, IDX_DIM))
        outs[7].append(u.reshape(n_s, n_t, D_CONV)[:, n_t - (CONV_W - 1):])

    return (hp.reshape(n_b, seq, D_MODEL), hs.reshape(n_s, n_t, D_MODEL),
            *[jnp.stack(o) for o in outs])
```

```python
import functools

import jax
import jax.numpy as jnp
import numpy as np
from jax import lax
from jax.experimental import pallas as pl
from jax.experimental.pallas import tpu as pltpu

F32 = jnp.float32
BF16 = jnp.bfloat16
I32 = jnp.int32

D_MODEL = 2048
HEAD_DIM = 128
N_HEADS = 8
N_KV_HEADS = 2
GROUP = N_HEADS // N_KV_HEADS
D_ATTN = N_HEADS * HEAD_DIM
D_CONV = 1024
ROPE_DIM = HEAD_DIM // 4
ROPE_HALF = ROPE_DIM // 2
ROPE_THETA = 500000.0
N_IDX_HEADS = 16
IDX_DIM = 64
IDX_ROPE_HALF = IDX_DIM // 8
TOPK_MAX = 256
CONV_W = 3
PAGE_SIZE = 128
EPS = 1e-6
W_IDX_SCALE = (N_IDX_HEADS ** -0.5) * (IDX_DIM ** -0.5)
ATTN_SCALE = HEAD_DIM ** -0.5
Q_PRESCALE = ATTN_SCALE * 1.4426950408889634

INT_MIN = -(2 ** 31)
NEG = -1e30

VMEM_LIMIT_BYTES = 56 * 1024 * 1024

PROJ_TN = 512
PROJ_TM = 1024
ROW_TM = 512
N_KV = N_KV_HEADS * HEAD_DIM
OFF_Q = 0
OFF_K = OFF_Q + D_ATTN
OFF_V = OFF_K + N_KV
OFF_Z = OFF_V + N_KV
OFF_QI = OFF_Z + D_ATTN
OFF_KI = OFF_QI + N_IDX_HEADS * IDX_DIM
OFF_WI = OFF_KI + IDX_DIM
OFF_H = OFF_WI + N_IDX_HEADS
OFF_B = OFF_H + D_CONV
OFF_C = OFF_B + D_CONV
OFF_ZC = OFF_C + D_CONV
D_IN = OFF_ZC + D_CONV
assert OFF_K == 2 * PROJ_TN and OFF_Z == 3 * PROJ_TN and OFF_QI == 5 * PROJ_TN and OFF_KI == 7 * PROJ_TN
KIW_ROWS = 128
assert OFF_KI % KIW_ROWS == 0 and OFF_WI - OFF_KI == IDX_DIM
CONV_CHUNK = 256
N_CONV_CHUNKS = D_CONV // CONV_CHUNK
J_Q = 0
J_KV = 2
J_Z = 3
J_QI = 5
J_CONV = 7
N_MAIN_BLOCKS = J_CONV
N_STEPS = J_CONV + N_CONV_CHUNKS

ATT_TQ = 256
ATT_TK = 256
ATT_SUB = 64
CNT_ACCS = 4

S_SCORE_PAGES = 32
S_ATTN_PAGES = 32


def _cparams(n_axes):
    return pltpu.CompilerParams(
        dimension_semantics=("arbitrary",) * n_axes,
        vmem_limit_bytes=VMEM_LIMIT_BYTES,
    )


def _rmsnorm_body(x_ref, g_ref, o_ref):
    x = x_ref[...]
    ms = jnp.mean(x * x, axis=-1, keepdims=True)
    o_ref[...] = (x * lax.rsqrt(ms + EPS) * g_ref[...]).astype(o_ref.dtype)


def _rmsnorm(x, g, tm):
    m = x.shape[0]
    return pl.pallas_call(
        _rmsnorm_body,
        grid=(m // tm,),
        in_specs=[pl.BlockSpec((tm, D_MODEL), lambda i: (i, 0)),
                  pl.BlockSpec((1, D_MODEL), lambda i: (0, 0))],
        out_specs=pl.BlockSpec((tm, D_MODEL), lambda i: (i, 0)),
        out_shape=jax.ShapeDtypeStruct((m, D_MODEL), BF16),
        compiler_params=_cparams(1),
        name="rmsnorm_in",
    )(x, g.reshape(1, D_MODEL))


def _silu(x):
    return x * jax.nn.sigmoid(x)


def _proj_body(sample, tps, tm, *refs):
    (xn_ref, wa_ref, wkiw_ref, wh_ref, wb_ref, wc_ref, wzc_ref, gq_ref, gk_ref, gki_ref,
     c16_ref, s16_ref, c8_ref, s8_ref,
     kc_ref, ksa_ref, ksb_ref, kic_ref, kisa_ref, kisb_ref, wconv_ref) = refs[:21]
    refs = refs[21:]
    if sample:
        e1_ref, e2_ref = refs[:2]
        refs = refs[2:]
    (qT_ref, qiT_ref, vT_ref, wT_ref, k32_ref, kbf_ref, v32_ref, ki32_ref, kibf_ref,
     gate_ref, mixc_ref, u_ref) = refs[:12]
    refs = refs[12:]
    if not sample:
        (carry_ref,) = refs

    i = pl.program_id(0)
    j = pl.program_id(1)
    nt = (((1,), (1,)), ((), ()))

    def xw(w):
        return lax.dot_general(xn_ref[...], w, nt, preferred_element_type=F32)

    def wx(w):
        return lax.dot_general(w, xn_ref[...], nt, preferred_element_type=F32)

    if not sample:
        @pl.when((i == 0) & (j == 0))
        def _init():
            carry_ref[...] = jnp.zeros(carry_ref.shape, F32)

    @pl.when(j < J_KV)
    def _q():
        res = wx(wa_ref[...])
        cos = c16_ref[...]
        sin = s16_ref[...]
        for hh in range(PROJ_TN // HEAD_DIM):
            blk = res[hh * HEAD_DIM:(hh + 1) * HEAD_DIM]
            ms = jnp.mean(blk * blk, axis=0, keepdims=True)
            y = blk * lax.rsqrt(ms + EPS) * gq_ref[...]
            x1 = y[0:ROPE_HALF]
            x2 = y[ROPE_HALF:ROPE_DIM]
            base = hh * HEAD_DIM
            qT_ref[base:base + ROPE_HALF, :] = ((x1 * cos - x2 * sin) * Q_PRESCALE).astype(BF16)
            qT_ref[base + ROPE_HALF:base + ROPE_DIM, :] = ((x2 * cos + x1 * sin) * Q_PRESCALE).astype(BF16)
            qT_ref[base + ROPE_DIM:base + HEAD_DIM, :] = (y[ROPE_DIM:] * Q_PRESCALE).astype(BF16)

    @pl.when((j >= J_QI) & (j < J_CONV))
    def _qi():
        res = wx(wa_ref[...])
        cos = c8_ref[...]
        sin = s8_ref[...]
        for hh in range(PROJ_TN // IDX_DIM):
            blk = res[hh * IDX_DIM:(hh + 1) * IDX_DIM]
            x1 = blk[0:IDX_ROPE_HALF]
            x2 = blk[IDX_ROPE_HALF:2 * IDX_ROPE_HALF]
            rot = jnp.concatenate([x1 * cos - x2 * sin, x2 * cos + x1 * sin], axis=0)
            base = hh * IDX_DIM
            qiT_ref[base:base + 2 * IDX_ROPE_HALF, :] = rot.astype(BF16)
            qiT_ref[base + 2 * IDX_ROPE_HALF:base + IDX_DIM, :] = blk[2 * IDX_ROPE_HALF:].astype(BF16)

    @pl.when(j == J_KV)
    def _kv():
        res = xw(wa_ref[...])
        for hd in range(N_KV_HEADS):
            hs = slice(hd * HEAD_DIM, (hd + 1) * HEAD_DIM)
            blk = res[:, hs]
            ms = jnp.mean(blk * blk, axis=-1, keepdims=True)
            yk = blk * lax.rsqrt(ms + EPS) * gk_ref[...]
            rot = yk * kc_ref[...] + (pltpu.roll(yk, HEAD_DIM - ROPE_HALF, axis=1) * ksa_ref[...]
                                      + pltpu.roll(yk, ROPE_HALF, axis=1) * ksb_ref[...])
            k32_ref[:, hs] = rot
            kbf_ref[:, hs] = rot.astype(BF16)
        v = res[:, N_KV:2 * N_KV]
        v32_ref[...] = v
        if sample:
            vT_ref[...] = wx(wa_ref[N_KV:2 * N_KV, :]).astype(BF16)
        else:
            vT_ref[...] = v.T.astype(BF16)
        wT_ref[...] = wx(wkiw_ref[IDX_DIM:IDX_DIM + N_IDX_HEADS, :]) * W_IDX_SCALE
        r2 = xw(wkiw_ref[...])
        lane = lax.broadcasted_iota(I32, r2.shape, 1)
        r2 = jnp.where(lane < IDX_DIM, r2, 0.0)
        ms = jnp.sum(r2 * r2, axis=-1, keepdims=True) * (1.0 / IDX_DIM)
        yi = r2 * lax.rsqrt(ms + EPS) * gki_ref[...]
        roti = yi * kic_ref[...] + (pltpu.roll(yi, KIW_ROWS - IDX_ROPE_HALF, axis=1) * kisa_ref[...]
                                    + pltpu.roll(yi, IDX_ROPE_HALF, axis=1) * kisb_ref[...])
        ki32_ref[...] = roti[:, 0:IDX_DIM]
        kibf_ref[...] = roti.astype(BF16)

    @pl.when((j >= J_Z) & (j < J_QI))
    def _z():
        gate_ref[...] = _silu(xw(wa_ref[...]))

    @pl.when(j >= J_CONV)
    def _conv():
        cc = j - J_CONV
        u = xw(wc_ref[...]) * xw(wh_ref[...])
        rowid = lax.broadcasted_iota(I32, (tm, CONV_CHUNK), 0)
        if sample:
            t = rowid & (sample - 1)
            u1 = jnp.where(t >= 1, pltpu.roll(u, 1, axis=0), e1_ref[...])
            u2 = jnp.where(t >= 2, pltpu.roll(u, 2, axis=0), e2_ref[...])
            u_ref[...] = u
        else:
            first = (i % tps) == 0
            prev = carry_ref[cc]
            p0 = jnp.where(first, 0.0, prev[0:1])
            p1 = jnp.where(first, 0.0, prev[1:2])
            u1 = jnp.where(rowid == 0, p1, pltpu.roll(u, 1, axis=0))
            u2 = jnp.where(rowid == 0, p0, jnp.where(rowid == 1, p1, pltpu.roll(u, 2, axis=0)))
            tail = u[tm - 8:tm]
            carry_ref[cc] = jnp.concatenate([tail[6:8], tail[0:6]], axis=0)
            u_ref[0] = tail[6:8]
        w = wconv_ref[...]
        y = u2 * w[0:1] + u1 * w[1:2] + u * w[2:3]
        mixc_ref[...] = (xw(wb_ref[...]) * y * _silu(xw(wzc_ref[...]))).astype(BF16)


def _project(xn, wT, params, tabs, sample, state_rows=None):
    m = xn.shape[0]
    assert sample & (sample - 1) == 0
    tm = m if sample else PROJ_TM
    n_i = m // tm
    tps = 1 if sample else (tabs["c16T"].shape[1] // tm)
    gq, gk, gki, wconv = params

    def tmap(i):
        return i % tps

    def cchunk(j):
        return jnp.clip(j - J_CONV, 0, N_CONV_CHUNKS - 1)

    def conv_rows(off):
        return pl.BlockSpec((pl.Element(CONV_CHUNK), pl.Element(D_MODEL)),
                            lambda i, j: (pl.multiple_of(off + CONV_CHUNK * cchunk(j), 16), 0))

    in_specs = [
        pl.BlockSpec((tm, D_MODEL), lambda i, j: (i, 0)),
        pl.BlockSpec((PROJ_TN, D_MODEL), lambda i, j: (jnp.minimum(j, N_MAIN_BLOCKS - 1), 0)),
        pl.BlockSpec((KIW_ROWS, D_MODEL), lambda i, j: (OFF_KI // KIW_ROWS, 0)),
        conv_rows(OFF_H), conv_rows(OFF_B), conv_rows(OFF_C), conv_rows(OFF_ZC),
        pl.BlockSpec((HEAD_DIM, 1), lambda i, j: (0, 0)),
        pl.BlockSpec((1, HEAD_DIM), lambda i, j: (0, 0)),
        pl.BlockSpec((1, KIW_ROWS), lambda i, j: (0, 0)),
        pl.BlockSpec((ROPE_HALF, tm), lambda i, j: (0, tmap(i))),
        pl.BlockSpec((ROPE_HALF, tm), lambda i, j: (0, tmap(i))),
        pl.BlockSpec((IDX_ROPE_HALF, tm), lambda i, j: (0, tmap(i))),
        pl.BlockSpec((IDX_ROPE_HALF, tm), lambda i, j: (0, tmap(i))),
        pl.BlockSpec((tm, HEAD_DIM), lambda i, j: (tmap(i), 0)),
        pl.BlockSpec((tm, HEAD_DIM), lambda i, j: (tmap(i), 0)),
        pl.BlockSpec((tm, HEAD_DIM), lambda i, j: (tmap(i), 0)),
        pl.BlockSpec((tm, KIW_ROWS), lambda i, j: (tmap(i), 0)),
        pl.BlockSpec((tm, KIW_ROWS), lambda i, j: (tmap(i), 0)),
        pl.BlockSpec((tm, KIW_ROWS), lambda i, j: (tmap(i), 0)),
        pl.BlockSpec((CONV_W, CONV_CHUNK), lambda i, j: (0, cchunk(j))),
    ]
    args = [xn, wT, wT, wT, wT, wT, wT, gq, gk, gki,
            tabs["c16T"], tabs["s16T"], tabs["c8T"], tabs["s8T"],
            tabs["kC"], tabs["kSa"], tabs["kSb"], tabs["kiC"], tabs["kiSa"], tabs["kiSb"], wconv]
    cmap = lambda i, j: (i, cchunk(j))
    if sample:
        in_specs += [pl.BlockSpec((tm, CONV_CHUNK), cmap), pl.BlockSpec((tm, CONV_CHUNK), cmap)]
        args += list(state_rows)
        u_spec = pl.BlockSpec((tm, CONV_CHUNK), cmap)
        u_shape = jax.ShapeDtypeStruct((m, D_CONV), F32)
        scratch = []
    else:
        u_spec = pl.BlockSpec((1, CONV_W - 1, CONV_CHUNK), lambda i, j: (i, 0, cchunk(j)))
        u_shape = jax.ShapeDtypeStruct((n_i, CONV_W - 1, D_CONV), F32)
        scratch = [pltpu.VMEM((N_CONV_CHUNKS, 8, CONV_CHUNK), F32)]

    out_specs = [
        pl.BlockSpec((PROJ_TN, tm), lambda i, j: (jnp.minimum(j, 1), i)),
        pl.BlockSpec((PROJ_TN, tm), lambda i, j: (jnp.clip(j - J_QI, 0, 1), i)),
        pl.BlockSpec((N_KV, tm), lambda i, j: (0, i)),
        pl.BlockSpec((N_IDX_HEADS, tm), lambda i, j: (0, i)),
        pl.BlockSpec((tm, N_KV), lambda i, j: (i, 0)),
        pl.BlockSpec((tm, N_KV), lambda i, j: (i, 0)),
        pl.BlockSpec((tm, N_KV), lambda i, j: (i, 0)),
        pl.BlockSpec((tm, IDX_DIM), lambda i, j: (i, 0)),
        pl.BlockSpec((tm, KIW_ROWS), lambda i, j: (i, 0)),
        pl.BlockSpec((tm, PROJ_TN), lambda i, j: (i, jnp.clip(j - J_Z, 0, 1))),
        pl.BlockSpec((tm, CONV_CHUNK), cmap),
        u_spec,
    ]
    out_shape = [
        jax.ShapeDtypeStruct((D_ATTN, m), BF16),
        jax.ShapeDtypeStruct((N_IDX_HEADS * IDX_DIM, m), BF16),
        jax.ShapeDtypeStruct((N_KV, m), BF16),
        jax.ShapeDtypeStruct((N_IDX_HEADS, m), F32),
        jax.ShapeDtypeStruct((m, N_KV), F32),
        jax.ShapeDtypeStruct((m, N_KV), BF16),
        jax.ShapeDtypeStruct((m, N_KV), F32),
        jax.ShapeDtypeStruct((m, IDX_DIM), F32),
        jax.ShapeDtypeStruct((m, KIW_ROWS), BF16),
        jax.ShapeDtypeStruct((m, D_ATTN), F32),
        jax.ShapeDtypeStruct((m, D_CONV), BF16),
        u_shape,
    ]
    return pl.pallas_call(
        functools.partial(_proj_body, sample, tps, tm),
        grid=(n_i, N_STEPS),
        in_specs=in_specs,
        out_specs=out_specs,
        out_shape=out_shape,
        scratch_shapes=scratch,
        compiler_params=_cparams(2),
        name="proj_sample" if sample else "proj_prompt",
    )(*args)


KEY_LOWEST_FINITE = INT_MIN + 0x00800000


def _key_to_float(key):
    return pltpu.bitcast(key ^ ((key >> 31) & 0x7FFFFFFF), F32)


def _bit_value(b):
    return lax.shift_left(jnp.int32(1), jnp.int32(31) - b)


def _attn_prompt_body(qiT_ref, wT_ref, ki_ref, qT_ref, k_ref, vT_ref, gate_ref, o_ref,
                      sc_ref, bias_ref, m_ref, l_ref, acc_ref, s_ref, p_ref):
    tq, tk = ATT_TQ, ATT_TK
    i = pl.program_id(1)
    nch = i + 1
    w = wT_ref[...]
    row = lax.broadcasted_iota(I32, (tk, tq), 0)
    col = lax.broadcasted_iota(I32, (tk, tq), 1)

    def score_chunk(j, carry):
        off = pl.multiple_of(j * tk, tk)
        kic = ki_ref[pl.ds(off, tk), 0:IDX_DIM]
        acc = jnp.zeros((tk, tq), F32)
        for h in range(N_IDX_HEADS):
            d = jnp.dot(kic, qiT_ref[h * IDX_DIM:(h + 1) * IDX_DIM, :], preferred_element_type=F32)
            acc = acc + w[h:h + 1, :] * jnp.maximum(d, 0.0)
        future = (row + j * tk) > (col + i * tq)
        sc_ref[pl.ds(off, tk), :] = jnp.where(future, -jnp.inf, acc)
        return carry

    lax.fori_loop(0, nch, score_chunk, 0)

    def bit_body(b, thr):
        cand = thr + _bit_value(b)
        cand_f = _key_to_float(cand)

        def cnt_chunk(j, cs):
            off = pl.multiple_of(j * tk, tk)
            cs = list(cs)
            sc = sc_ref[pl.ds(off, tk), :]
            for r in range(tk // 8):
                a = cs[r % CNT_ACCS]
                cs[r % CNT_ACCS] = jnp.where(sc[r * 8:(r + 1) * 8] >= cand_f, a + 1, a)
            return tuple(cs)

        cs = lax.fori_loop(0, nch, cnt_chunk, (jnp.zeros((8, tq), I32),) * CNT_ACCS)
        c = cs[0]
        for a in cs[1:]:
            c = c + a
        cnt = jnp.sum(c.astype(F32), axis=0, keepdims=True)
        return jnp.where(cnt >= float(TOPK_MAX), cand, thr)

    thr = lax.fori_loop(0, 32, bit_body, jnp.full((1, tq), INT_MIN, I32))
    thr_f = _key_to_float(jnp.maximum(thr, KEY_LOWEST_FINITE))

    def bias_chunk(j, carry):
        off = pl.multiple_of(j * tk, tk)
        bias_ref[pl.ds(off, tk), :] = jnp.where(sc_ref[pl.ds(off, tk), :] >= thr_f, 0.0, NEG)
        return carry

    lax.fori_loop(0, nch, bias_chunk, 0)

    m_ref[...] = jnp.full(m_ref.shape, NEG, F32)
    l_ref[...] = jnp.zeros(l_ref.shape, F32)
    acc_ref[...] = jnp.zeros(acc_ref.shape, F32)

    def att_chunk(j, carry):
        off = pl.multiple_of(j * tk, tk)
        bias = bias_ref[pl.ds(off, tk), :]
        for h in range(N_HEADS):
            g = h // GROUP
            kc = k_ref[pl.ds(off, tk), g * HEAD_DIM:(g + 1) * HEAD_DIM]
            s_ref[h * tk:(h + 1) * tk, :] = jnp.dot(
                kc, qT_ref[h * HEAD_DIM:(h + 1) * HEAD_DIM, :], preferred_element_type=F32) + bias
        alphas = []
        for h in range(N_HEADS):
            mloc = None
            for r in range(tk // ATT_SUB):
                blk = s_ref[h * tk + r * ATT_SUB:h * tk + (r + 1) * ATT_SUB, :]
                bm = blk.reshape(ATT_SUB // 8, 8, tq).max(axis=0)
                mloc = bm if mloc is None else jnp.maximum(mloc, bm)
            m_old = m_ref[h:h + 1, :]
            m_new = jnp.maximum(m_old, jnp.max(mloc, axis=0, keepdims=True))
            alpha = jnp.exp2(m_old - m_new)
            lsum = None
            for r in range(tk // ATT_SUB):
                rs = slice(h * tk + r * ATT_SUB, h * tk + (r + 1) * ATT_SUB)
                p = jnp.exp2(s_ref[rs, :] - m_new)
                ps = p.reshape(ATT_SUB // 8, 8, tq).sum(axis=0)
                lsum = ps if lsum is None else lsum + ps
                p_ref[rs, :] = p.astype(BF16)
            l_ref[h:h + 1, :] = l_ref[h:h + 1, :] * alpha + jnp.sum(lsum, axis=0, keepdims=True)
            m_ref[h:h + 1, :] = m_new
            alphas.append(alpha)
        for h in range(N_HEADS):
            g = h // GROUP
            hs = slice(h * HEAD_DIM, (h + 1) * HEAD_DIM)
            vc = vT_ref[g * HEAD_DIM:(g + 1) * HEAD_DIM, pl.ds(off, tk)]
            acc_ref[hs, :] = acc_ref[hs, :] * alphas[h] + jnp.dot(
                vc, p_ref[h * tk:(h + 1) * tk, :], preferred_element_type=F32)
        return carry

    lax.fori_loop(0, nch, att_chunk, 0)

    for h in range(N_HEADS):
        hs = slice(h * HEAD_DIM, (h + 1) * HEAD_DIM)
        o = (acc_ref[hs, :] / l_ref[h:h + 1, :]).T
        o_ref[:, hs] = (o * gate_ref[:, hs]).astype(BF16)


def _attn_prompt(qiT, wT, kibf, qT, kbf, vT, gate, n_batch, seq):
    m = n_batch * seq
    nq = seq // ATT_TQ
    nkv = N_KV_HEADS * HEAD_DIM
    qmap = lambda b, i: (0, b * nq + i)
    return pl.pallas_call(
        _attn_prompt_body,
        grid=(n_batch, nq),
        in_specs=[
            pl.BlockSpec((N_IDX_HEADS * IDX_DIM, ATT_TQ), qmap),
            pl.BlockSpec((N_IDX_HEADS, ATT_TQ), qmap),
            pl.BlockSpec((seq, 128), lambda b, i: (b, 0)),
            pl.BlockSpec((D_ATTN, ATT_TQ), qmap),
            pl.BlockSpec((seq, nkv), lambda b, i: (b, 0)),
            pl.BlockSpec((nkv, seq), lambda b, i: (0, b)),
            pl.BlockSpec((ATT_TQ, D_ATTN), lambda b, i: (b * nq + i, 0)),
        ],
        out_specs=pl.BlockSpec((ATT_TQ, D_ATTN), lambda b, i: (b * nq + i, 0)),
        out_shape=jax.ShapeDtypeStruct((m, D_ATTN), BF16),
        scratch_shapes=[pltpu.VMEM((seq, ATT_TQ), F32), pltpu.VMEM((seq, ATT_TQ), F32),
                        pltpu.VMEM((N_HEADS, ATT_TQ), F32), pltpu.VMEM((N_HEADS, ATT_TQ), F32),
                        pltpu.VMEM((D_ATTN, ATT_TQ), F32),
                        pltpu.VMEM((N_HEADS * ATT_TK, ATT_TQ), F32),
                        pltpu.VMEM((N_HEADS * ATT_TK, ATT_TQ), BF16)],
        compiler_params=_cparams(2),
        name="attn_prompt",
    )(qiT, wT, kibf, qT, kbf, vT, gate)


def _s_score_body(pt_ref, qi_ref, wcol_ref, kinew_ref, *rest):
    npg = S_SCORE_PAGES
    pages = rest[:npg]
    out_ref, outnew_ref = rest[npg:]
    pg = pl.program_id(1)
    qi = qi_ref[0]
    wc = wcol_ref[0]

    def scores(keys_t):
        d = jnp.dot(qi, keys_t, preferred_element_type=F32)
        val = jnp.maximum(d, 0.0) * wc
        return val.reshape(N_IDX_HEADS, 8, val.shape[-1]).sum(axis=0)

    for r in range(npg):
        out_ref[0, :, r * PAGE_SIZE:(r + 1) * PAGE_SIZE] = scores(pages[r][...].astype(BF16))

    @pl.when(pg == 0)
    def _new():
        outnew_ref[0] = scores(kinew_ref[0])


def _s_thresh_body(n_new, sp_ref, sn_ref, bp_ref, bn_ref):
    rows, past = sp_ref.shape
    ch = 2048
    nchunk = past // ch
    t = lax.broadcasted_iota(I32, (rows, 128), 0) & 7
    lane = lax.broadcasted_iota(I32, (rows, 128), 1)
    sn = jnp.where((lane < n_new) & (lane <= t), sn_ref[...], -jnp.inf)

    def fold(x):
        f = x[:, 0:128]
        for q in range(1, x.shape[1] // 128):
            f = f + x[:, q * 128:(q + 1) * 128]
        return f

    def bit_body(b, thr):
        cand = thr + _bit_value(b)
        cand_f = _key_to_float(cand)

        def cnt_chunk(c, acc):
            off = pl.multiple_of(c * ch, ch)
            return acc + fold(jnp.where(sp_ref[:, pl.ds(off, ch)] >= cand_f, 1.0, 0.0))

        acc = lax.fori_loop(0, nchunk, cnt_chunk, jnp.where(sn >= cand_f, 1.0, 0.0))
        cnt = jnp.sum(acc, axis=1, keepdims=True)
        return jnp.where(cnt >= float(TOPK_MAX), cand, thr)

    thr = lax.fori_loop(0, 32, bit_body, jnp.full((rows, 1), INT_MIN, I32))
    thr_f = _key_to_float(jnp.maximum(thr, KEY_LOWEST_FINITE))

    def to_bias(c, carry):
        off = pl.multiple_of(c * ch, ch)
        bp_ref[:, pl.ds(off, ch)] = jnp.where(sp_ref[:, pl.ds(off, ch)] >= thr_f, 0.0, NEG)
        return carry

    lax.fori_loop(0, nchunk, to_bias, 0)
    bn_ref[...] = jnp.where(sn >= thr_f, 0.0, NEG)


def _s_attn_body(n_steps, pt_ref, q_ref, bp_ref, bn_ref, knew_ref, vnew_ref, *rest):
    npg = S_ATTN_PAGES
    kpages = rest[:npg]
    vpages = rest[npg:2 * npg]
    out_ref, s_ref, p_ref, l_ref, acc_ref = rest[2 * npg:]
    t = pl.program_id(1)
    nt = (((1,), (1,)), ((), ()))
    rg = GROUP * 8
    width = npg * PAGE_SIZE
    past = n_steps * width
    sm_chunk = 2048

    def head_rows(pages, g):
        rows = [pg_ref[pl.ds(g, PAGE_SIZE, stride=N_KV_HEADS), :] for pg_ref in pages]
        return jnp.concatenate(rows, axis=0).astype(BF16)

    @pl.when(t < n_steps)
    def _logits():
        off = pl.multiple_of(t * width, width)
        bias = jnp.concatenate([bp_ref[0]] * GROUP, axis=0)
        for g in range(N_KV_HEADS):
            qg = q_ref[0, g * rg:(g + 1) * rg, :]
            s_ref[g * rg:(g + 1) * rg, pl.ds(off, width)] = lax.dot_general(
                qg, head_rows(kpages, g), nt, preferred_element_type=F32) + bias

    @pl.when(t == n_steps - 1)
    def _softmax():
        biasn = jnp.concatenate([bn_ref[0]] * GROUP, axis=0)
        for g in range(N_KV_HEADS):
            qg = q_ref[0, g * rg:(g + 1) * rg, :]
            kg = knew_ref[0, :, g * HEAD_DIM:(g + 1) * HEAD_DIM]
            s_ref[g * rg:(g + 1) * rg, past:past + PAGE_SIZE] = lax.dot_general(
                qg, kg, nt, preferred_element_type=F32) + biasn

        def fold(x, op):
            f = x[:, 0:128]
            for q in range(1, x.shape[1] // 128):
                f = op(f, x[:, q * 128:(q + 1) * 128])
            return f

        def max_chunk(c, m):
            off = pl.multiple_of(c * sm_chunk, sm_chunk)
            return jnp.maximum(m, fold(s_ref[:, pl.ds(off, sm_chunk)], jnp.maximum))

        m = lax.fori_loop(0, past // sm_chunk, max_chunk, s_ref[:, past:past + PAGE_SIZE])
        m = jnp.max(m, axis=1, keepdims=True)

        def exp_chunk(c, l):
            off = pl.multiple_of(c * sm_chunk, sm_chunk)
            p = jnp.exp2(s_ref[:, pl.ds(off, sm_chunk)] - m)
            p_ref[:, pl.ds(off, sm_chunk)] = p.astype(BF16)
            return l + fold(p, jnp.add)

        pn = jnp.exp2(s_ref[:, past:past + PAGE_SIZE] - m)
        p_ref[:, past:past + PAGE_SIZE] = pn.astype(BF16)
        l_ref[...] = lax.fori_loop(0, past // sm_chunk, exp_chunk, pn)
        acc_ref[...] = jnp.zeros(acc_ref.shape, F32)

    @pl.when(t >= n_steps)
    def _values():
        off = pl.multiple_of((t - n_steps) * width, width)
        for g in range(N_KV_HEADS):
            rs = slice(g * rg, (g + 1) * rg)
            acc_ref[rs, :] += jnp.dot(p_ref[rs, pl.ds(off, width)], head_rows(vpages, g),
                                      preferred_element_type=F32)

    @pl.when(t == 2 * n_steps - 1)
    def _finish():
        for g in range(N_KV_HEADS):
            rs = slice(g * rg, (g + 1) * rg)
            vg = vnew_ref[0, :, g * HEAD_DIM:(g + 1) * HEAD_DIM]
            acc_ref[rs, :] += jnp.dot(p_ref[rs, past:past + PAGE_SIZE], vg, preferred_element_type=F32)
        out_ref[0] = acc_ref[...] / jnp.sum(l_ref[...], axis=1, keepdims=True)


def _attn_sample(qT, qiT, wT, kibf, kbf, v32, cache_k, cache_v, cache_kidx, page_table, n_seq, n_tok):
    n_pages = page_table.shape[1]
    past = n_pages * PAGE_SIZE
    npg = S_SCORE_PAGES
    n_steps = n_pages // npg
    n_pool = cache_k.shape[0]

    def rows_ht(xT, n_heads, dim):
        x = xT.reshape(n_heads, dim, n_seq, n_tok).transpose(2, 0, 3, 1)
        x = jnp.pad(x, ((0, 0), (0, 0), (0, 8 - n_tok), (0, 0)))
        return x.reshape(n_seq, n_heads * 8, dim)

    qi_rows = rows_ht(qiT, N_IDX_HEADS, IDX_DIM)
    q_rows = rows_ht(qT, N_HEADS, HEAD_DIM)
    wcol = jnp.pad(wT.reshape(N_IDX_HEADS, n_seq, n_tok).transpose(1, 0, 2),
                   ((0, 0), (0, 0), (0, 8 - n_tok))).reshape(n_seq, N_IDX_HEADS * 8, 1)

    def pad_keys(x):
        x = x.reshape(n_seq, n_tok, x.shape[-1])
        return jnp.pad(x, ((0, 0), (0, PAGE_SIZE - n_tok), (0, 0)))

    ki_new_t = jnp.swapaxes(pad_keys(kibf[:, 0:IDX_DIM]), 1, 2)
    kidx_t = jnp.swapaxes(cache_kidx, 1, 2)
    k_new = pad_keys(kbf)
    v_new = pad_keys(v32.astype(BF16))

    def page_spec(shape, r):
        return pl.BlockSpec((None,) + shape, lambda b, pg, pt, r=r: (pt[b, pg * npg + r], 0, 0))

    def kv_page_spec(shape, r, n, is_v):
        def imap(b, t, pt):
            step = jnp.clip(t - n, 0, n - 1) if is_v else jnp.minimum(t, n - 1)
            return (pt[b, step * S_ATTN_PAGES + r], 0, 0)
        return pl.BlockSpec((None,) + shape, imap)

    sp, sn = pl.pallas_call(
        _s_score_body,
        grid_spec=pltpu.PrefetchScalarGridSpec(
            num_scalar_prefetch=1,
            grid=(n_seq, n_steps),
            in_specs=[
                pl.BlockSpec((1, N_IDX_HEADS * 8, IDX_DIM), lambda b, pg, pt: (b, 0, 0)),
                pl.BlockSpec((1, N_IDX_HEADS * 8, 1), lambda b, pg, pt: (b, 0, 0)),
                pl.BlockSpec((1, IDX_DIM, PAGE_SIZE), lambda b, pg, pt: (b, 0, 0)),
            ] + [page_spec((IDX_DIM, PAGE_SIZE), r) for r in range(npg)],
            out_specs=[
                pl.BlockSpec((1, 8, npg * PAGE_SIZE), lambda b, pg, pt: (b, 0, pg)),
                pl.BlockSpec((1, 8, PAGE_SIZE), lambda b, pg, pt: (b, 0, 0)),
            ],
        ),
        out_shape=[jax.ShapeDtypeStruct((n_seq, 8, past), F32),
                   jax.ShapeDtypeStruct((n_seq, 8, PAGE_SIZE), F32)],
        compiler_params=_cparams(2),
        name="sample_scores",
    )(page_table, qi_rows, wcol, ki_new_t, *([kidx_t] * npg))

    rows = n_seq * 8
    bp, bn = pl.pallas_call(
        functools.partial(_s_thresh_body, n_tok),
        grid=(1,),
        in_specs=[pl.BlockSpec((rows, past), lambda i: (0, 0)),
                  pl.BlockSpec((rows, PAGE_SIZE), lambda i: (0, 0))],
        out_specs=[pl.BlockSpec((rows, past), lambda i: (0, 0)),
                   pl.BlockSpec((rows, PAGE_SIZE), lambda i: (0, 0))],
        out_shape=[jax.ShapeDtypeStruct((rows, past), F32),
                   jax.ShapeDtypeStruct((rows, PAGE_SIZE), F32)],
        compiler_params=_cparams(1),
        name="sample_topk_mask",
    )(sp.reshape(rows, past), sn.reshape(rows, PAGE_SIZE))
    bp = bp.reshape(n_seq, 8, past)
    bn = bn.reshape(n_seq, 8, PAGE_SIZE)

    nrow = N_HEADS * 8
    kv_rows = PAGE_SIZE * N_KV_HEADS
    ck = cache_k.reshape(n_pool, kv_rows, HEAD_DIM)
    cv = cache_v.reshape(n_pool, kv_rows, HEAD_DIM)
    na = n_pages // S_ATTN_PAGES
    width = S_ATTN_PAGES * PAGE_SIZE
    out = pl.pallas_call(
        functools.partial(_s_attn_body, na),
        grid_spec=pltpu.PrefetchScalarGridSpec(
            num_scalar_prefetch=1,
            grid=(n_seq, 2 * na),
            in_specs=[
                pl.BlockSpec((1, nrow, HEAD_DIM), lambda b, t, pt: (b, 0, 0)),
                pl.BlockSpec((1, 8, width), lambda b, t, pt: (b, 0, jnp.minimum(t, na - 1))),
                pl.BlockSpec((1, 8, PAGE_SIZE), lambda b, t, pt: (b, 0, 0)),
                pl.BlockSpec((1, PAGE_SIZE, N_KV), lambda b, t, pt: (b, 0, 0)),
                pl.BlockSpec((1, PAGE_SIZE, N_KV), lambda b, t, pt: (b, 0, 0)),
            ] + [kv_page_spec((kv_rows, HEAD_DIM), r, na, False) for r in range(S_ATTN_PAGES)]
              + [kv_page_spec((kv_rows, HEAD_DIM), r, na, True) for r in range(S_ATTN_PAGES)],
            out_specs=pl.BlockSpec((1, nrow, HEAD_DIM), lambda b, t, pt: (b, 0, 0)),
            scratch_shapes=[pltpu.VMEM((nrow, past + PAGE_SIZE), F32),
                            pltpu.VMEM((nrow, past + PAGE_SIZE), BF16),
                            pltpu.VMEM((nrow, 128), F32), pltpu.VMEM((nrow, HEAD_DIM), F32)],
        ),
        out_shape=jax.ShapeDtypeStruct((n_seq, nrow, HEAD_DIM), F32),
        compiler_params=_cparams(2),
        name="sample_attn",
    )(page_table, q_rows, bp, bn, k_new, v_new, *([ck] * S_ATTN_PAGES), *([cv] * S_ATTN_PAGES))
    out = out.reshape(n_seq, N_HEADS, 8, HEAD_DIM)[:, :, 0:n_tok]
    return out.transpose(0, 2, 1, 3).reshape(n_seq * n_tok, D_ATTN)


def _outproj_body(gated, *refs):
    if gated:
        x_ref, a_ref, gate_ref, mc_ref, wo_ref, o_ref = refs
        ma = (a_ref[...] * gate_ref[...]).astype(BF16)
    else:
        x_ref, a_ref, mc_ref, wo_ref, o_ref = refs
        ma = a_ref[...]
    acc = jnp.dot(ma, wo_ref[0:D_ATTN, :], preferred_element_type=F32)
    acc = acc + jnp.dot(mc_ref[...], wo_ref[D_ATTN:D_ATTN + D_CONV, :], preferred_element_type=F32)
    o_ref[...] = x_ref[...] + acc


def _outproj(x, attn, gate, mixc, wo, tm):
    m = x.shape[0]
    gated = gate is not None
    row = lambda i: (i, 0)
    in_specs = [pl.BlockSpec((tm, D_MODEL), row), pl.BlockSpec((tm, D_ATTN), row)]
    args = [x, attn]
    if gated:
        in_specs.append(pl.BlockSpec((tm, D_ATTN), row))
        args.append(gate)
    in_specs += [pl.BlockSpec((tm, D_CONV), row),
                 pl.BlockSpec((D_ATTN + D_CONV, D_MODEL), lambda i: (0, 0))]
    args += [mixc, wo]
    return pl.pallas_call(
        functools.partial(_outproj_body, gated),
        grid=(m // tm,),
        in_specs=in_specs,
        out_specs=pl.BlockSpec((tm, D_MODEL), row),
        out_shape=jax.ShapeDtypeStruct((m, D_MODEL), F32),
        compiler_params=_cparams(1),
        name="outproj_sample" if gated else "outproj_prompt",
    )(*args)


def _rope_tables(pos):
    posf = pos.astype(F32)[:, None]
    n = pos.shape[0]

    def cs(half):
        inv = ROPE_THETA ** (-jnp.arange(half, dtype=F32) / half)
        ang = posf * inv[None, :]
        return jnp.cos(ang), jnp.sin(ang)

    c16, s16 = cs(ROPE_HALF)
    c8, s8 = cs(IDX_ROPE_HALF)
    one = lambda w: jnp.ones((n, w), F32)
    zero = lambda w: jnp.zeros((n, w), F32)
    rest = HEAD_DIM - ROPE_DIM
    k_c = jnp.concatenate([c16, c16, one(rest)], axis=1)
    k_sa = jnp.concatenate([-s16, zero(HEAD_DIM - ROPE_HALF)], axis=1)
    k_sb = jnp.concatenate([zero(ROPE_HALF), s16, zero(rest)], axis=1)
    ki_c = jnp.concatenate([c8, c8, one(128 - 2 * IDX_ROPE_HALF)], axis=1)
    ki_sa = jnp.concatenate([-s8, zero(128 - IDX_ROPE_HALF)], axis=1)
    ki_sb = jnp.concatenate([zero(IDX_ROPE_HALF), s8, zero(128 - 2 * IDX_ROPE_HALF)], axis=1)
    return dict(c16T=c16.T, s16T=s16.T, c8T=c8.T, s8T=s8.T,
                kC=k_c, kSa=k_sa, kSb=k_sb, kiC=ki_c, kiSa=ki_sa, kiSb=ki_sb)


def _prep_weights(w_in, g_q, g_k, g_kidx, w_conv, w_out):
    assert w_in.shape == (D_MODEL, D_IN)
    wT = w_in.T.astype(BF16)
    gq = g_q.reshape(HEAD_DIM, 1)
    gk = g_k.reshape(1, HEAD_DIM)
    gki = jnp.concatenate([g_kidx.reshape(1, IDX_DIM), jnp.zeros((1, KIW_ROWS - IDX_DIM), F32)], axis=1)
    return wT, (gq, gk, gki, w_conv), w_out.astype(BF16)


def kernel(x_prompt, x_sample, cache_k, cache_v, cache_kidx, state_conv, page_table,
           norm_in, w_in, g_q, g_k, g_kidx, w_conv, w_out):
    n_b, seq, _ = x_prompt.shape
    n_s, n_t, _ = x_sample.shape
    depth = w_in.shape[0]
    past = page_table.shape[1] * PAGE_SIZE
    tabs_p = _rope_tables(jnp.arange(seq))
    tabs_s = _rope_tables(jnp.tile(past + jnp.arange(n_t), n_s))

    hp = x_prompt.reshape(n_b * seq, D_MODEL)
    hs = x_sample.reshape(n_s * n_t, D_MODEL)
    outs = [[] for _ in range(8)]
    for l in range(depth):
        w_t, params, wo = _prep_weights(w_in[l], g_q[l], g_k[l], g_kidx[l], w_conv[l], w_out[l])

        xn = _rmsnorm(hp, norm_in[l], ROW_TM)
        (qT, qiT, vT, wT, k32, kbf, v32, ki32, kibf, gate, mixc, utail) = _project(
            xn, w_t, params, tabs_p, 0)
        mixa = _attn_prompt(qiT, wT, kibf, qT, kbf, vT, gate, n_b, seq)
        hp = _outproj(hp, mixa, None, mixc, wo, ROW_TM)
        tps = seq // PROJ_TM
        outs[0].append(k32.reshape(n_b, seq, N_KV_HEADS, HEAD_DIM))
        outs[1].append(v32.reshape(n_b, seq, N_KV_HEADS, HEAD_DIM))
        outs[2].append(ki32.reshape(n_b, seq, IDX_DIM))
        outs[3].append(utail[tps - 1::tps])

        st = state_conv[l]
        tok = jnp.arange(n_t)
        e1 = st[:, jnp.full((n_t,), CONV_W - 2)].reshape(n_s * n_t, D_CONV)
        e2 = st[:, jnp.minimum(tok, CONV_W - 2)].reshape(n_s * n_t, D_CONV)
        xn = _rmsnorm(hs, norm_in[l], n_s * n_t)
        (qT, qiT, vT, wT, k32, kbf, v32, ki32, kibf, gate, mixc, u) = _project(
            xn, w_t, params, tabs_s, n_t, state_rows=(e1, e2))
        attn = _attn_sample(qT, qiT, wT, kibf, kbf, v32, cache_k[l], cache_v[l], cache_kidx[l],
                            page_table, n_s, n_t)
        hs = _outproj(hs, attn, gate, mixc, wo, n_s * n_t)
        outs[4].append(k32.reshape(n_s, n_t, N_KV_HEADS, HEAD_DIM))
        outs[5].append(v32.reshape(n_s, n_t, N_KV_HEADS, HEAD_DIM))
        outs[6].append(ki32.reshape(n_s, n_t, IDX_DIM))
        outs[7].append(u.reshape(n_s, n_t, D_CONV)[:, n_t - (CONV_W - 1):])

    return (hp.reshape(n_b, seq, D_MODEL), hs.reshape(n_s, n_t, D_MODEL),
            *[jnp.stack(o) for o in outs])
```

```python
import functools

import jax
import jax.numpy as jnp
import numpy as np
from jax import lax
from jax.experimental import pallas as pl
from jax.experimental.pallas import tpu as pltpu

F32 = jnp.float32
BF16 = jnp.bfloat16
I32 = jnp.int32

D_MODEL = 2048
HEAD_DIM = 128
N_HEADS = 8
N_KV_HEADS = 2
GROUP = N_HEADS // N_KV_HEADS
D_ATTN = N_HEADS * HEAD_DIM
D_CONV = 1024
ROPE_DIM = HEAD_DIM // 4
ROPE_HALF = ROPE_DIM // 2
ROPE_THETA = 500000.0
N_IDX_HEADS = 16
IDX_DIM = 64
IDX_ROPE_HALF = IDX_DIM // 8
TOPK_MAX = 256
CONV_W = 3
PAGE_SIZE = 128
EPS = 1e-6
W_IDX_SCALE = (N_IDX_HEADS ** -0.5) * (IDX_DIM ** -0.5)
ATTN_SCALE = HEAD_DIM ** -0.5
Q_PRESCALE = ATTN_SCALE * 1.4426950408889634

INT_MIN = -(2 ** 31)
NEG = -1e30

VMEM_LIMIT_BYTES = 56 * 1024 * 1024

PROJ_TN = 512
PROJ_TM = 1024
ROW_TM = 512
N_KV = N_KV_HEADS * HEAD_DIM
OFF_Q = 0
OFF_K = OFF_Q + D_ATTN
OFF_V = OFF_K + N_KV
OFF_Z = OFF_V + N_KV
OFF_QI = OFF_Z + D_ATTN
OFF_KI = OFF_QI + N_IDX_HEADS * IDX_DIM
OFF_WI = OFF_KI + IDX_DIM
OFF_H = OFF_WI + N_IDX_HEADS
OFF_B = OFF_H + D_CONV
OFF_C = OFF_B + D_CONV
OFF_ZC = OFF_C + D_CONV
D_IN = OFF_ZC + D_CONV
assert OFF_K == 2 * PROJ_TN and OFF_Z == 3 * PROJ_TN and OFF_QI == 5 * PROJ_TN and OFF_KI == 7 * PROJ_TN
KIW_ROWS = 128
assert OFF_KI % KIW_ROWS == 0 and OFF_WI - OFF_KI == IDX_DIM
CONV_CHUNK = 256
N_CONV_CHUNKS = D_CONV // CONV_CHUNK
J_Q = 0
J_KV = 2
J_Z = 3
J_QI = 5
J_CONV = 7
N_MAIN_BLOCKS = J_CONV
N_STEPS = J_CONV + N_CONV_CHUNKS

ATT_TQ = 256
ATT_TK = 256
ATT_SUB = 64
CNT_ACCS = 4

S_SCORE_PAGES = 32
S_ATTN_PAGES = 32


def _cparams(n_axes):
    return pltpu.CompilerParams(
        dimension_semantics=("arbitrary",) * n_axes,
        vmem_limit_bytes=VMEM_LIMIT_BYTES,
    )


def _rmsnorm_body(x_ref, g_ref, o_ref):
    x = x_ref[...]
    ms = jnp.mean(x * x, axis=-1, keepdims=True)
    o_ref[...] = (x * lax.rsqrt(ms + EPS) * g_ref[...]).astype(o_ref.dtype)


def _rmsnorm(x, g, tm):
    m = x.shape[0]
    return pl.pallas_call(
        _rmsnorm_body,
        grid=(m // tm,),
        in_specs=[pl.BlockSpec((tm, D_MODEL), lambda i: (i, 0)),
                  pl.BlockSpec((1, D_MODEL), lambda i: (0, 0))],
        out_specs=pl.BlockSpec((tm, D_MODEL), lambda i: (i, 0)),
        out_shape=jax.ShapeDtypeStruct((m, D_MODEL), BF16),
        compiler_params=_cparams(1),
        name="rmsnorm_in",
    )(x, g.reshape(1, D_MODEL))


def _silu(x):
    return x * jax.nn.sigmoid(x)


def _proj_body(sample, tps, tm, *refs):
    (xn_ref, wa_ref, wkiw_ref, wh_ref, wb_ref, wc_ref, wzc_ref, gq_ref, gk_ref, gki_ref,
     c16_ref, s16_ref, c8_ref, s8_ref,
     kc_ref, ksa_ref, ksb_ref, wconv_ref) = refs[:18]
    refs = refs[18:]
    if sample:
        e1_ref, e2_ref = refs[:2]
        refs = refs[2:]
    (qT_ref, qiT_ref, vT_ref, wT_ref, k32_ref, kbf_ref, v32_ref, ki32_ref, kibf_ref,
     gate_ref, mixc_ref, u_ref) = refs[:12]
    refs = refs[12:]
    if not sample:
        (carry_ref,) = refs

    i = pl.program_id(0)
    j = pl.program_id(1)
    nt = (((1,), (1,)), ((), ()))

    def xw(w):
        return lax.dot_general(xn_ref[...], w, nt, preferred_element_type=F32)

    def wx(w):
        return lax.dot_general(w, xn_ref[...], nt, preferred_element_type=F32)

    if not sample:
        @pl.when((i == 0) & (j == 0))
        def _init():
            carry_ref[...] = jnp.zeros(carry_ref.shape, F32)

    @pl.when(j < J_KV)
    def _q():
        res = wx(wa_ref[...])
        cos = c16_ref[...]
        sin = s16_ref[...]
        for hh in range(PROJ_TN // HEAD_DIM):
            blk = res[hh * HEAD_DIM:(hh + 1) * HEAD_DIM]
            ms = jnp.mean(blk * blk, axis=0, keepdims=True)
            y = blk * lax.rsqrt(ms + EPS) * gq_ref[...]
            x1 = y[0:ROPE_HALF]
            x2 = y[ROPE_HALF:ROPE_DIM]
            base = hh * HEAD_DIM
            qT_ref[base:base + ROPE_HALF, :] = ((x1 * cos - x2 * sin) * Q_PRESCALE).astype(BF16)
            qT_ref[base + ROPE_HALF:base + ROPE_DIM, :] = ((x2 * cos + x1 * sin) * Q_PRESCALE).astype(BF16)
            qT_ref[base + ROPE_DIM:base + HEAD_DIM, :] = (y[ROPE_DIM:] * Q_PRESCALE).astype(BF16)

    @pl.when((j >= J_QI) & (j < J_CONV))
    def _qi():
        res = wx(wa_ref[...])
        cos = c8_ref[...]
        sin = s8_ref[...]
        for hh in range(PROJ_TN // IDX_DIM):
            blk = res[hh * IDX_DIM:(hh + 1) * IDX_DIM]
            x1 = blk[0:IDX_ROPE_HALF]
            x2 = blk[IDX_ROPE_HALF:2 * IDX_ROPE_HALF]
            rot = jnp.concatenate([x1 * cos - x2 * sin, x2 * cos + x1 * sin], axis=0)
            base = hh * IDX_DIM
            qiT_ref[base:base + 2 * IDX_ROPE_HALF, :] = rot.astype(BF16)
            qiT_ref[base + 2 * IDX_ROPE_HALF:base + IDX_DIM, :] = blk[2 * IDX_ROPE_HALF:].astype(BF16)

    @pl.when(j == J_KV)
    def _kv():
        res = xw(wa_ref[...])
        for hd in range(N_KV_HEADS):
            hs = slice(hd * HEAD_DIM, (hd + 1) * HEAD_DIM)
            blk = res[:, hs]
            ms = jnp.mean(blk * blk, axis=-1, keepdims=True)
            yk = blk * lax.rsqrt(ms + EPS) * gk_ref[...]
            rot = yk * kc_ref[...] + (pltpu.roll(yk, HEAD_DIM - ROPE_HALF, axis=1) * ksa_ref[...]
                                      + pltpu.roll(yk, ROPE_HALF, axis=1) * ksb_ref[...])
            k32_ref[pl.ds(hd, tm, stride=N_KV_HEADS), :] = rot
            kbf_ref[:, hs] = rot.astype(BF16)
        v = res[:, N_KV:2 * N_KV]
        for hd in range(N_KV_HEADS):
            v32_ref[pl.ds(hd, tm, stride=N_KV_HEADS), :] = v[:, hd * HEAD_DIM:(hd + 1) * HEAD_DIM]
        if sample:
            vT_ref[...] = wx(wa_ref[N_KV:2 * N_KV, :]).astype(BF16)
        else:
            vT_ref[...] = v.T.astype(BF16)
        r2t = wx(wkiw_ref[...])
        wT_ref[...] = r2t[IDX_DIM:IDX_DIM + N_IDX_HEADS] * W_IDX_SCALE
        kit = r2t[0:IDX_DIM]
        ms = jnp.mean(kit * kit, axis=0, keepdims=True)
        yi = kit * lax.rsqrt(ms + EPS) * gki_ref[...]
        x1 = yi[0:IDX_ROPE_HALF]
        x2 = yi[IDX_ROPE_HALF:2 * IDX_ROPE_HALF]
        cos = c8_ref[...]
        sin = s8_ref[...]
        roti = jnp.concatenate([x1 * cos - x2 * sin, x2 * cos + x1 * sin, yi[2 * IDX_ROPE_HALF:],
                                jnp.zeros((KIW_ROWS - IDX_DIM, tm), F32)], axis=0)
        ki_nat = roti.T
        ki32_ref[...] = ki_nat[:, 0:IDX_DIM]
        kibf_ref[...] = ki_nat.astype(BF16)

    @pl.when((j >= J_Z) & (j < J_QI))
    def _z():
        gate_ref[...] = _silu(xw(wa_ref[...]))

    @pl.when(j >= J_CONV)
    def _conv():
        cc = j - J_CONV
        u = xw(wc_ref[...]) * xw(wh_ref[...])
        rowid = lax.broadcasted_iota(I32, (tm, CONV_CHUNK), 0)
        if sample:
            t = rowid & (sample - 1)
            u1 = jnp.where(t >= 1, pltpu.roll(u, 1, axis=0), e1_ref[...])
            u2 = jnp.where(t >= 2, pltpu.roll(u, 2, axis=0), e2_ref[...])
            u_ref[...] = u
        else:
            first = (i % tps) == 0
            prev = carry_ref[cc]
            p0 = jnp.where(first, 0.0, prev[0:1])
            p1 = jnp.where(first, 0.0, prev[1:2])
            u1 = jnp.where(rowid == 0, p1, pltpu.roll(u, 1, axis=0))
            u2 = jnp.where(rowid == 0, p0, jnp.where(rowid == 1, p1, pltpu.roll(u, 2, axis=0)))
            tail = u[tm - 8:tm]
            carry_ref[cc] = jnp.concatenate([tail[6:8], tail[0:6]], axis=0)
            u_ref[0] = tail[6:8]
        w = wconv_ref[...]
        y = u2 * w[0:1] + u1 * w[1:2] + u * w[2:3]
        mixc_ref[...] = (xw(wb_ref[...]) * y * _silu(xw(wzc_ref[...]))).astype(BF16)


def _project(xn, wT, params, tabs, sample, state_rows=None):
    m = xn.shape[0]
    assert sample & (sample - 1) == 0
    tm = m if sample else PROJ_TM
    n_i = m // tm
    tps = 1 if sample else (tabs["c16T"].shape[1] // tm)
    gq, gk, gki, wconv = params

    def tmap(i):
        return i % tps

    def cchunk(j):
        return jnp.clip(j - J_CONV, 0, N_CONV_CHUNKS - 1)

    def conv_rows(off):
        return pl.BlockSpec((pl.Element(CONV_CHUNK), pl.Element(D_MODEL)),
                            lambda i, j: (pl.multiple_of(off + CONV_CHUNK * cchunk(j), 16), 0))

    in_specs = [
        pl.BlockSpec((tm, D_MODEL), lambda i, j: (i, 0)),
        pl.BlockSpec((PROJ_TN, D_MODEL), lambda i, j: (jnp.minimum(j, N_MAIN_BLOCKS - 1), 0)),
        pl.BlockSpec((KIW_ROWS, D_MODEL), lambda i, j: (OFF_KI // KIW_ROWS, 0)),
        conv_rows(OFF_H), conv_rows(OFF_B), conv_rows(OFF_C), conv_rows(OFF_ZC),
        pl.BlockSpec((HEAD_DIM, 1), lambda i, j: (0, 0)),
        pl.BlockSpec((1, HEAD_DIM), lambda i, j: (0, 0)),
        pl.BlockSpec((IDX_DIM, 1), lambda i, j: (0, 0)),
        pl.BlockSpec((ROPE_HALF, tm), lambda i, j: (0, tmap(i))),
        pl.BlockSpec((ROPE_HALF, tm), lambda i, j: (0, tmap(i))),
        pl.BlockSpec((IDX_ROPE_HALF, tm), lambda i, j: (0, tmap(i))),
        pl.BlockSpec((IDX_ROPE_HALF, tm), lambda i, j: (0, tmap(i))),
        pl.BlockSpec((tm, HEAD_DIM), lambda i, j: (tmap(i), 0)),
        pl.BlockSpec((tm, HEAD_DIM), lambda i, j: (tmap(i), 0)),
        pl.BlockSpec((tm, HEAD_DIM), lambda i, j: (tmap(i), 0)),
        pl.BlockSpec((CONV_W, CONV_CHUNK), lambda i, j: (0, cchunk(j))),
    ]
    args = [xn, wT, wT, wT, wT, wT, wT, gq, gk, gki,
            tabs["c16T"], tabs["s16T"], tabs["c8T"], tabs["s8T"],
            tabs["kC"], tabs["kSa"], tabs["kSb"], wconv]
    cmap = lambda i, j: (i, cchunk(j))
    if sample:
        in_specs += [pl.BlockSpec((tm, CONV_CHUNK), cmap), pl.BlockSpec((tm, CONV_CHUNK), cmap)]
        args += list(state_rows)
        u_spec = pl.BlockSpec((tm, CONV_CHUNK), cmap)
        u_shape = jax.ShapeDtypeStruct((m, D_CONV), F32)
        scratch = []
    else:
        u_spec = pl.BlockSpec((1, CONV_W - 1, CONV_CHUNK), lambda i, j: (i, 0, cchunk(j)))
        u_shape = jax.ShapeDtypeStruct((n_i, CONV_W - 1, D_CONV), F32)
        scratch = [pltpu.VMEM((N_CONV_CHUNKS, 8, CONV_CHUNK), F32)]

    out_specs = [
        pl.BlockSpec((PROJ_TN, tm), lambda i, j: (jnp.minimum(j, 1), i)),
        pl.BlockSpec((PROJ_TN, tm), lambda i, j: (jnp.clip(j - J_QI, 0, 1), i)),
        pl.BlockSpec((N_KV, tm), lambda i, j: (0, i)),
        pl.BlockSpec((N_IDX_HEADS, tm), lambda i, j: (0, i)),
        pl.BlockSpec((N_KV_HEADS * tm, HEAD_DIM), lambda i, j: (i, 0)),
        pl.BlockSpec((tm, N_KV), lambda i, j: (i, 0)),
        pl.BlockSpec((N_KV_HEADS * tm, HEAD_DIM), lambda i, j: (i, 0)),
        pl.BlockSpec((tm, IDX_DIM), lambda i, j: (i, 0)),
        pl.BlockSpec((tm, KIW_ROWS), lambda i, j: (i, 0)),
        pl.BlockSpec((tm, PROJ_TN), lambda i, j: (i, jnp.clip(j - J_Z, 0, 1))),
        pl.BlockSpec((tm, CONV_CHUNK), cmap),
        u_spec,
    ]
    out_shape = [
        jax.ShapeDtypeStruct((D_ATTN, m), BF16),
        jax.ShapeDtypeStruct((N_IDX_HEADS * IDX_DIM, m), BF16),
        jax.ShapeDtypeStruct((N_KV, m), BF16),
        jax.ShapeDtypeStruct((N_IDX_HEADS, m), F32),
        jax.ShapeDtypeStruct((N_KV_HEADS * m, HEAD_DIM), F32),
        jax.ShapeDtypeStruct((m, N_KV), BF16),
        jax.ShapeDtypeStruct((N_KV_HEADS * m, HEAD_DIM), F32),
        jax.ShapeDtypeStruct((m, IDX_DIM), F32),
        jax.ShapeDtypeStruct((m, KIW_ROWS), BF16),
        jax.ShapeDtypeStruct((m, D_ATTN), F32),
        jax.ShapeDtypeStruct((m, D_CONV), BF16),
        u_shape,
    ]
    return pl.pallas_call(
        functools.partial(_proj_body, sample, tps, tm),
        grid=(n_i, N_STEPS),
        in_specs=in_specs,
        out_specs=out_specs,
        out_shape=out_shape,
        scratch_shapes=scratch,
        compiler_params=_cparams(2),
        name="proj_sample" if sample else "proj_prompt",
    )(*args)


KEY_LOWEST_FINITE = INT_MIN + 0x00800000


def _key_to_float(key):
    return pltpu.bitcast(key ^ ((key >> 31) & 0x7FFFFFFF), F32)


def _bit_value(b):
    return lax.shift_left(jnp.int32(1), jnp.int32(31) - b)


def _attn_prompt_body(qiT_ref, wT_ref, ki_ref, qT_ref, k_ref, vT_ref, gate_ref, o_ref,
                      sc_ref, bias_ref, acc_ref, s_ref):
    tq, tk = ATT_TQ, ATT_TK
    i = pl.program_id(1)
    nch = i + 1
    w = wT_ref[...]
    row = lax.broadcasted_iota(I32, (tk, tq), 0)
    col = lax.broadcasted_iota(I32, (tk, tq), 1)

    def score_chunk(j, carry):
        off = pl.multiple_of(j * tk, tk)
        kic = ki_ref[pl.ds(off, tk), 0:IDX_DIM]
        acc = jnp.zeros((tk, tq), F32)
        for h in range(N_IDX_HEADS):
            d = jnp.dot(kic, qiT_ref[h * IDX_DIM:(h + 1) * IDX_DIM, :], preferred_element_type=F32)
            acc = acc + w[h:h + 1, :] * jnp.maximum(d, 0.0)
        future = (row + j * tk) > (col + i * tq)
        sc_ref[pl.ds(off, tk), :] = jnp.where(future, -jnp.inf, acc)
        return carry

    lax.fori_loop(0, nch, score_chunk, 0)

    def bit_body(b, thr):
        cand = thr + _bit_value(b)
        cand_f = _key_to_float(cand)

        def cnt_chunk(j, cs):
            off = pl.multiple_of(j * tk, tk)
            cs = list(cs)
            sc = sc_ref[pl.ds(off, tk), :]
            for r in range(tk // 8):
                a = cs[r % CNT_ACCS]
                cs[r % CNT_ACCS] = jnp.where(sc[r * 8:(r + 1) * 8] >= cand_f, a + 1, a)
            return tuple(cs)

        cs = lax.fori_loop(0, nch, cnt_chunk, (jnp.zeros((8, tq), I32),) * CNT_ACCS)
        c = cs[0]
        for a in cs[1:]:
            c = c + a
        cnt = jnp.sum(c.astype(F32), axis=0, keepdims=True)
        return jnp.where(cnt >= float(TOPK_MAX), cand, thr)

    thr = lax.fori_loop(0, 32, bit_body, jnp.full((1, tq), INT_MIN, I32))
    thr_f = _key_to_float(jnp.maximum(thr, KEY_LOWEST_FINITE))

    def bias_chunk(j, carry):
        off = pl.multiple_of(j * tk, tk)
        bias_ref[pl.ds(off, tk), :] = jnp.where(sc_ref[pl.ds(off, tk), :] >= thr_f, 0.0, NEG)
        return carry

    lax.fori_loop(0, nch, bias_chunk, 0)

    def qk_chunk(j, mrun):
        off = pl.multiple_of(j * tk, tk)
        bias = bias_ref[pl.ds(off, tk), :]
        out = []
        for h in range(N_HEADS):
            g = h // GROUP
            kc = k_ref[pl.ds(off, tk), g * HEAD_DIM:(g + 1) * HEAD_DIM]
            s = jnp.dot(kc, qT_ref[h * HEAD_DIM:(h + 1) * HEAD_DIM, :], preferred_element_type=F32) + bias
            s_ref[h, pl.ds(off, tk), :] = s
            out.append(jnp.maximum(mrun[h], s.reshape(tk // 8, 8, tq).max(axis=0)))
        return tuple(out)

    mrun = lax.fori_loop(0, nch, qk_chunk, (jnp.full((8, tq), NEG, F32),) * N_HEADS)
    ms = [jnp.max(mr, axis=0, keepdims=True) for mr in mrun]
    acc_ref[...] = jnp.zeros(acc_ref.shape, F32)

    def pv_chunk(j, lrun):
        off = pl.multiple_of(j * tk, tk)
        out = []
        for h in range(N_HEADS):
            g = h // GROUP
            hs = slice(h * HEAD_DIM, (h + 1) * HEAD_DIM)
            p = jnp.exp2(s_ref[h, pl.ds(off, tk), :] - ms[h])
            out.append(lrun[h] + p.reshape(tk // 8, 8, tq).sum(axis=0))
            vc = vT_ref[g * HEAD_DIM:(g + 1) * HEAD_DIM, pl.ds(off, tk)]
            acc_ref[hs, :] += jnp.dot(vc, p.astype(BF16), preferred_element_type=F32)
        return tuple(out)

    lrun = lax.fori_loop(0, nch, pv_chunk, (jnp.zeros((8, tq), F32),) * N_HEADS)

    for h in range(N_HEADS):
        hs = slice(h * HEAD_DIM, (h + 1) * HEAD_DIM)
        o = (acc_ref[hs, :] / jnp.sum(lrun[h], axis=0, keepdims=True)).T
        o_ref[:, hs] = (o * gate_ref[:, hs]).astype(BF16)


def _attn_prompt(qiT, wT, kibf, qT, kbf, vT, gate, n_batch, seq):
    m = n_batch * seq
    nq = seq // ATT_TQ
    nkv = N_KV_HEADS * HEAD_DIM
    qmap = lambda b, i: (0, b * nq + i)
    return pl.pallas_call(
        _attn_prompt_body,
        grid=(n_batch, nq),
        in_specs=[
            pl.BlockSpec((N_IDX_HEADS * IDX_DIM, ATT_TQ), qmap),
            pl.BlockSpec((N_IDX_HEADS, ATT_TQ), qmap),
            pl.BlockSpec((seq, 128), lambda b, i: (b, 0)),
            pl.BlockSpec((D_ATTN, ATT_TQ), qmap),
            pl.BlockSpec((seq, nkv), lambda b, i: (b, 0)),
            pl.BlockSpec((nkv, seq), lambda b, i: (0, b)),
            pl.BlockSpec((ATT_TQ, D_ATTN), lambda b, i: (b * nq + i, 0)),
        ],
        out_specs=pl.BlockSpec((ATT_TQ, D_ATTN), lambda b, i: (b * nq + i, 0)),
        out_shape=jax.ShapeDtypeStruct((m, D_ATTN), BF16),
        scratch_shapes=[pltpu.VMEM((seq, ATT_TQ), F32), pltpu.VMEM((seq, ATT_TQ), F32),
                        pltpu.VMEM((D_ATTN, ATT_TQ), F32),
                        pltpu.VMEM((N_HEADS, seq, ATT_TQ), F32)],
        compiler_params=_cparams(2),
        name="attn_prompt",
    )(qiT, wT, kibf, qT, kbf, vT, gate)


def _s_score_body(pt_ref, qi_ref, wcol_ref, kinew_ref, *rest):
    npg = S_SCORE_PAGES
    pages = rest[:npg]
    out_ref, outnew_ref = rest[npg:]
    pg = pl.program_id(1)
    qi = qi_ref[0]
    wc = wcol_ref[0]

    def scores(keys_t):
        d = jnp.dot(qi, keys_t, preferred_element_type=F32)
        val = jnp.maximum(d, 0.0) * wc
        return val.reshape(N_IDX_HEADS, 8, val.shape[-1]).sum(axis=0)

    for r in range(npg):
        out_ref[0, :, r * PAGE_SIZE:(r + 1) * PAGE_SIZE] = scores(pages[r][...].astype(BF16))

    @pl.when(pg == 0)
    def _new():
        outnew_ref[0] = scores(kinew_ref[0])


def _s_thresh_body(n_new, sp_ref, sn_ref, bp_ref, bn_ref):
    rows, past = sp_ref.shape
    ch = 2048
    nchunk = past // ch
    t = lax.broadcasted_iota(I32, (rows, 128), 0) & 7
    lane = lax.broadcasted_iota(I32, (rows, 128), 1)
    sn = jnp.where((lane < n_new) & (lane <= t), sn_ref[...], -jnp.inf)

    def fold(x):
        f = x[:, 0:128]
        for q in range(1, x.shape[1] // 128):
            f = f + x[:, q * 128:(q + 1) * 128]
        return f

    def bit_body(b, thr):
        cand = thr + _bit_value(b)
        cand_f = _key_to_float(cand)

        def cnt_chunk(c, acc):
            off = pl.multiple_of(c * ch, ch)
            return acc + fold(jnp.where(sp_ref[:, pl.ds(off, ch)] >= cand_f, 1.0, 0.0))

        acc = lax.fori_loop(0, nchunk, cnt_chunk, jnp.where(sn >= cand_f, 1.0, 0.0))
        cnt = jnp.sum(acc, axis=1, keepdims=True)
        return jnp.where(cnt >= float(TOPK_MAX), cand, thr)

    thr = lax.fori_loop(0, 32, bit_body, jnp.full((rows, 1), INT_MIN, I32))
    thr_f = _key_to_float(jnp.maximum(thr, KEY_LOWEST_FINITE))

    def to_bias(c, carry):
        off = pl.multiple_of(c * ch, ch)
        bp_ref[:, pl.ds(off, ch)] = jnp.where(sp_ref[:, pl.ds(off, ch)] >= thr_f, 0.0, NEG)
        return carry

    lax.fori_loop(0, nchunk, to_bias, 0)
    bn_ref[...] = jnp.where(sn >= thr_f, 0.0, NEG)


def _s_attn_body(n_steps, pt_ref, q_ref, bp_ref, bn_ref, knew_ref, vnew_ref, *rest):
    npg = S_ATTN_PAGES
    kpages = rest[:npg]
    vpages = rest[npg:2 * npg]
    out_ref, s_ref, p_ref, l_ref, acc_ref = rest[2 * npg:]
    t = pl.program_id(1)
    nt = (((1,), (1,)), ((), ()))
    rg = GROUP * 8
    width = npg * PAGE_SIZE
    past = n_steps * width
    sm_chunk = 2048

    def head_rows(pages, g):
        rows = [pg_ref[pl.ds(g, PAGE_SIZE, stride=N_KV_HEADS), :] for pg_ref in pages]
        return jnp.concatenate(rows, axis=0).astype(BF16)

    @pl.when(t < n_steps)
    def _logits():
        off = pl.multiple_of(t * width, width)
        bias = jnp.concatenate([bp_ref[0]] * GROUP, axis=0)
        for g in range(N_KV_HEADS):
            qg = q_ref[0, g * rg:(g + 1) * rg, :]
            s_ref[g * rg:(g + 1) * rg, pl.ds(off, width)] = lax.dot_general(
                qg, head_rows(kpages, g), nt, preferred_element_type=F32) + bias

    @pl.when(t == n_steps - 1)
    def _softmax():
        biasn = jnp.concatenate([bn_ref[0]] * GROUP, axis=0)
        for g in range(N_KV_HEADS):
            qg = q_ref[0, g * rg:(g + 1) * rg, :]
            kg = knew_ref[0, :, g * HEAD_DIM:(g + 1) * HEAD_DIM]
            s_ref[g * rg:(g + 1) * rg, past:past + PAGE_SIZE] = lax.dot_general(
                qg, kg, nt, preferred_element_type=F32) + biasn

        def fold(x, op):
            f = x[:, 0:128]
            for q in range(1, x.shape[1] // 128):
                f = op(f, x[:, q * 128:(q + 1) * 128])
            return f

        def max_chunk(c, m):
            off = pl.multiple_of(c * sm_chunk, sm_chunk)
            return jnp.maximum(m, fold(s_ref[:, pl.ds(off, sm_chunk)], jnp.maximum))

        m = lax.fori_loop(0, past // sm_chunk, max_chunk, s_ref[:, past:past + PAGE_SIZE])
        m = jnp.max(m, axis=1, keepdims=True)

        def exp_chunk(c, l):
            off = pl.multiple_of(c * sm_chunk, sm_chunk)
            p = jnp.exp2(s_ref[:, pl.ds(off, sm_chunk)] - m)
            p_ref[:, pl.ds(off, sm_chunk)] = p.astype(BF16)
            return l + fold(p, jnp.add)

        pn = jnp.exp2(s_ref[:, past:past + PAGE_SIZE] - m)
        p_ref[:, past:past + PAGE_SIZE] = pn.astype(BF16)
        l_ref[...] = lax.fori_loop(0, past // sm_chunk, exp_chunk, pn)
        acc_ref[...] = jnp.zeros(acc_ref.shape, F32)

    @pl.when(t >= n_steps)
    def _values():
        off = pl.multiple_of((t - n_steps) * width, width)
        for g in range(N_KV_HEADS):
            rs = slice(g * rg, (g + 1) * rg)
            acc_ref[rs, :] += jnp.dot(p_ref[rs, pl.ds(off, width)], head_rows(vpages, g),
                                      preferred_element_type=F32)

    @pl.when(t == 2 * n_steps - 1)
    def _finish():
        for g in range(N_KV_HEADS):
            rs = slice(g * rg, (g + 1) * rg)
            vg = vnew_ref[0, :, g * HEAD_DIM:(g + 1) * HEAD_DIM]
            acc_ref[rs, :] += jnp.dot(p_ref[rs, past:past + PAGE_SIZE], vg, preferred_element_type=F32)
        out_ref[0] = acc_ref[...] / jnp.sum(l_ref[...], axis=1, keepdims=True)


def _attn_sample(qT, qiT, wT, kibf, kbf, v32, cache_k, cache_v, cache_kidx, page_table, n_seq, n_tok):
    n_pages = page_table.shape[1]
    past = n_pages * PAGE_SIZE
    npg = S_SCORE_PAGES
    n_steps = n_pages // npg
    n_pool = cache_k.shape[0]

    def rows_ht(xT, n_heads, dim):
        x = xT.reshape(n_heads, dim, n_seq, n_tok).transpose(2, 0, 3, 1)
        x = jnp.pad(x, ((0, 0), (0, 0), (0, 8 - n_tok), (0, 0)))
        return x.reshape(n_seq, n_heads * 8, dim)

    qi_rows = rows_ht(qiT, N_IDX_HEADS, IDX_DIM)
    q_rows = rows_ht(qT, N_HEADS, HEAD_DIM)
    wcol = jnp.pad(wT.reshape(N_IDX_HEADS, n_seq, n_tok).transpose(1, 0, 2),
                   ((0, 0), (0, 0), (0, 8 - n_tok))).reshape(n_seq, N_IDX_HEADS * 8, 1)

    def pad_keys(x):
        x = x.reshape(n_seq, n_tok, x.shape[-1])
        return jnp.pad(x, ((0, 0), (0, PAGE_SIZE - n_tok), (0, 0)))

    ki_new_t = jnp.swapaxes(pad_keys(kibf[:, 0:IDX_DIM]), 1, 2)
    kidx_t = jnp.swapaxes(cache_kidx, 1, 2)
    k_new = pad_keys(kbf)
    v_new = pad_keys(v32.reshape(n_seq * n_tok, N_KV).astype(BF16))

    def page_spec(shape, r):
        return pl.BlockSpec((None,) + shape, lambda b, pg, pt, r=r: (pt[b, pg * npg + r], 0, 0))

    def kv_page_spec(shape, r, n, is_v):
        def imap(b, t, pt):
            step = jnp.clip(t - n, 0, n - 1) if is_v else jnp.minimum(t, n - 1)
            return (pt[b, step * S_ATTN_PAGES + r], 0, 0)
        return pl.BlockSpec((None,) + shape, imap)

    sp, sn = pl.pallas_call(
        _s_score_body,
        grid_spec=pltpu.PrefetchScalarGridSpec(
            num_scalar_prefetch=1,
            grid=(n_seq, n_steps),
            in_specs=[
                pl.BlockSpec((1, N_IDX_HEADS * 8, IDX_DIM), lambda b, pg, pt: (b, 0, 0)),
                pl.BlockSpec((1, N_IDX_HEADS * 8, 1), lambda b, pg, pt: (b, 0, 0)),
                pl.BlockSpec((1, IDX_DIM, PAGE_SIZE), lambda b, pg, pt: (b, 0, 0)),
            ] + [page_spec((IDX_DIM, PAGE_SIZE), r) for r in range(npg)],
            out_specs=[
                pl.BlockSpec((1, 8, npg * PAGE_SIZE), lambda b, pg, pt: (b, 0, pg)),
                pl.BlockSpec((1, 8, PAGE_SIZE), lambda b, pg, pt: (b, 0, 0)),
            ],
        ),
        out_shape=[jax.ShapeDtypeStruct((n_seq, 8, past), F32),
                   jax.ShapeDtypeStruct((n_seq, 8, PAGE_SIZE), F32)],
        compiler_params=_cparams(2),
        name="sample_scores",
    )(page_table, qi_rows, wcol, ki_new_t, *([kidx_t] * npg))

    rows = n_seq * 8
    bp, bn = pl.pallas_call(
        functools.partial(_s_thresh_body, n_tok),
        grid=(1,),
        in_specs=[pl.BlockSpec((rows, past), lambda i: (0, 0)),
                  pl.BlockSpec((rows, PAGE_SIZE), lambda i: (0, 0))],
        out_specs=[pl.BlockSpec((rows, past), lambda i: (0, 0)),
                   pl.BlockSpec((rows, PAGE_SIZE), lambda i: (0, 0))],
        out_shape=[jax.ShapeDtypeStruct((rows, past), F32),
                   jax.ShapeDtypeStruct((rows, PAGE_SIZE), F32)],
        compiler_params=_cparams(1),
        name="sample_topk_mask",
    )(sp.reshape(rows, past), sn.reshape(rows, PAGE_SIZE))
    bp = bp.reshape(n_seq, 8, past)
    bn = bn.reshape(n_seq, 8, PAGE_SIZE)

    nrow = N_HEADS * 8
    kv_rows = PAGE_SIZE * N_KV_HEADS
    ck = cache_k.reshape(n_pool, kv_rows, HEAD_DIM)
    cv = cache_v.reshape(n_pool, kv_rows, HEAD_DIM)
    na = n_pages // S_ATTN_PAGES
    width = S_ATTN_PAGES * PAGE_SIZE
    out = pl.pallas_call(
        functools.partial(_s_attn_body, na),
        grid_spec=pltpu.PrefetchScalarGridSpec(
            num_scalar_prefetch=1,
            grid=(n_seq, 2 * na),
            in_specs=[
                pl.BlockSpec((1, nrow, HEAD_DIM), lambda b, t, pt: (b, 0, 0)),
                pl.BlockSpec((1, 8, width), lambda b, t, pt: (b, 0, jnp.minimum(t, na - 1))),
                pl.BlockSpec((1, 8, PAGE_SIZE), lambda b, t, pt: (b, 0, 0)),
                pl.BlockSpec((1, PAGE_SIZE, N_KV), lambda b, t, pt: (b, 0, 0)),
                pl.BlockSpec((1, PAGE_SIZE, N_KV), lambda b, t, pt: (b, 0, 0)),
            ] + [kv_page_spec((kv_rows, HEAD_DIM), r, na, False) for r in range(S_ATTN_PAGES)]
              + [kv_page_spec((kv_rows, HEAD_DIM), r, na, True) for r in range(S_ATTN_PAGES)],
            out_specs=pl.BlockSpec((1, nrow, HEAD_DIM), lambda b, t, pt: (b, 0, 0)),
            scratch_shapes=[pltpu.VMEM((nrow, past + PAGE_SIZE), F32),
                            pltpu.VMEM((nrow, past + PAGE_SIZE), BF16),
                            pltpu.VMEM((nrow, 128), F32), pltpu.VMEM((nrow, HEAD_DIM), F32)],
        ),
        out_shape=jax.ShapeDtypeStruct((n_seq, nrow, HEAD_DIM), F32),
        compiler_params=_cparams(2),
        name="sample_attn",
    )(page_table, q_rows, bp, bn, k_new, v_new, *([ck] * S_ATTN_PAGES), *([cv] * S_ATTN_PAGES))
    out = out.reshape(n_seq, N_HEADS, 8, HEAD_DIM)[:, :, 0:n_tok]
    return out.transpose(0, 2, 1, 3).reshape(n_seq * n_tok, D_ATTN)


def _outproj_body(gated, *refs):
    if gated:
        x_ref, a_ref, gate_ref, mc_ref, wo_ref, o_ref = refs
        ma = (a_ref[...] * gate_ref[...]).astype(BF16)
    else:
        x_ref, a_ref, mc_ref, wo_ref, o_ref = refs
        ma = a_ref[...]
    acc = jnp.dot(ma, wo_ref[0:D_ATTN, :], preferred_element_type=F32)
    acc = acc + jnp.dot(mc_ref[...], wo_ref[D_ATTN:D_ATTN + D_CONV, :], preferred_element_type=F32)
    o_ref[...] = x_ref[...] + acc


def _outproj(x, attn, gate, mixc, wo, tm):
    m = x.shape[0]
    gated = gate is not None
    row = lambda i: (i, 0)
    in_specs = [pl.BlockSpec((tm, D_MODEL), row), pl.BlockSpec((tm, D_ATTN), row)]
    args = [x, attn]
    if gated:
        in_specs.append(pl.BlockSpec((tm, D_ATTN), row))
        args.append(gate)
    in_specs += [pl.BlockSpec((tm, D_CONV), row),
                 pl.BlockSpec((D_ATTN + D_CONV, D_MODEL), lambda i: (0, 0))]
    args += [mixc, wo]
    return pl.pallas_call(
        functools.partial(_outproj_body, gated),
        grid=(m // tm,),
        in_specs=in_specs,
        out_specs=pl.BlockSpec((tm, D_MODEL), row),
        out_shape=jax.ShapeDtypeStruct((m, D_MODEL), F32),
        compiler_params=_cparams(1),
        name="outproj_sample" if gated else "outproj_prompt",
    )(*args)


def _rope_tables(pos):
    posf = pos.astype(F32)[:, None]
    n = pos.shape[0]

    def cs(half):
        inv = ROPE_THETA ** (-jnp.arange(half, dtype=F32) / half)
        ang = posf * inv[None, :]
        return jnp.cos(ang), jnp.sin(ang)

    c16, s16 = cs(ROPE_HALF)
    c8, s8 = cs(IDX_ROPE_HALF)
    one = lambda w: jnp.ones((n, w), F32)
    zero = lambda w: jnp.zeros((n, w), F32)
    rest = HEAD_DIM - ROPE_DIM
    k_c = jnp.concatenate([c16, c16, one(rest)], axis=1)
    k_sa = jnp.concatenate([-s16, zero(HEAD_DIM - ROPE_HALF)], axis=1)
    k_sb = jnp.concatenate([zero(ROPE_HALF), s16, zero(rest)], axis=1)
    return dict(c16T=c16.T, s16T=s16.T, c8T=c8.T, s8T=s8.T, kC=k_c, kSa=k_sa, kSb=k_sb)


def _prep_weights(w_in, g_q, g_k, g_kidx, w_conv, w_out):
    assert w_in.shape == (D_MODEL, D_IN)
    wT = w_in.T.astype(BF16)
    gq = g_q.reshape(HEAD_DIM, 1)
    gk = g_k.reshape(1, HEAD_DIM)
    gki = g_kidx.reshape(IDX_DIM, 1)
    return wT, (gq, gk, gki, w_conv), w_out.astype(BF16)


def kernel(x_prompt, x_sample, cache_k, cache_v, cache_kidx, state_conv, page_table,
           norm_in, w_in, g_q, g_k, g_kidx, w_conv, w_out):
    n_b, seq, _ = x_prompt.shape
    n_s, n_t, _ = x_sample.shape
    depth = w_in.shape[0]
    past = page_table.shape[1] * PAGE_SIZE
    tabs_p = _rope_tables(jnp.arange(seq))
    tabs_s = _rope_tables(jnp.tile(past + jnp.arange(n_t), n_s))

    hp = x_prompt.reshape(n_b * seq, D_MODEL)
    hs = x_sample.reshape(n_s * n_t, D_MODEL)
    outs = [[] for _ in range(8)]
    for l in range(depth):
        w_t, params, wo = _prep_weights(w_in[l], g_q[l], g_k[l], g_kidx[l], w_conv[l], w_out[l])

        xn = _rmsnorm(hp, norm_in[l], ROW_TM)
        (qT, qiT, vT, wT, k32, kbf, v32, ki32, kibf, gate, mixc, utail) = _project(
            xn, w_t, params, tabs_p, 0)
        mixa = _attn_prompt(qiT, wT, kibf, qT, kbf, vT, gate, n_b, seq)
        hp = _outproj(hp, mixa, None, mixc, wo, ROW_TM)
        tps = seq // PROJ_TM
        outs[0].append(k32.reshape(n_b, seq, N_KV_HEADS, HEAD_DIM))
        outs[1].append(v32.reshape(n_b, seq, N_KV_HEADS, HEAD_DIM))
        outs[2].append(ki32.reshape(n_b, seq, IDX_DIM))
        outs[3].append(utail[tps - 1::tps])

        st = state_conv[l]
        tok = jnp.arange(n_t)
        e1 = st[:, jnp.full((n_t,), CONV_W - 2)].reshape(n_s * n_t, D_CONV)
        e2 = st[:, jnp.minimum(tok, CONV_W - 2)].reshape(n_s * n_t, D_CONV)
        xn = _rmsnorm(hs, norm_in[l], n_s * n_t)
        (qT, qiT, vT, wT, k32, kbf, v32, ki32, kibf, gate, mixc, u) = _project(
            xn, w_t, params, tabs_s, n_t, state_rows=(e1, e2))
        attn = _attn_sample(qT, qiT, wT, kibf, kbf, v32, cache_k[l], cache_v[l], cache_kidx[l],
                            page_table, n_s, n_t)
        hs = _outproj(hs, attn, gate, mixc, wo, n_s * n_t)
        outs[4].append(k32.reshape(n_s, n_t, N_KV_HEADS, HEAD_DIM))
        outs[5].append(v32.reshape(n_s, n_t, N_KV_HEADS, HEAD_DIM))
        outs[6].append(ki32.reshape(n_s, n_t, IDX_DIM))
        outs[7].append(u.reshape(n_s, n_t, D_CONV)[:, n_t - (CONV_W - 1):])

    return (hp.reshape(n_b, seq, D_MODEL), hs.reshape(n_s, n_t, D_MODEL),
            *[jnp.stack(o) for o in outs])
```

```python
import functools

import jax
import jax.numpy as jnp
import numpy as np
from jax import lax
from jax.experimental import pallas as pl
from jax.experimental.pallas import tpu as pltpu

F32 = jnp.float32
BF16 = jnp.bfloat16
I32 = jnp.int32

D_MODEL = 2048
HEAD_DIM = 128
N_HEADS = 8
N_KV_HEADS = 2
GROUP = N_HEADS // N_KV_HEADS
D_ATTN = N_HEADS * HEAD_DIM
D_CONV = 1024
ROPE_DIM = HEAD_DIM // 4
ROPE_HALF = ROPE_DIM // 2
ROPE_THETA = 500000.0
N_IDX_HEADS = 16
IDX_DIM = 64
IDX_ROPE_HALF = IDX_DIM // 8
TOPK_MAX = 256
CONV_W = 3
PAGE_SIZE = 128
EPS = 1e-6
W_IDX_SCALE = (N_IDX_HEADS ** -0.5) * (IDX_DIM ** -0.5)
ATTN_SCALE = HEAD_DIM ** -0.5
Q_PRESCALE = ATTN_SCALE * 1.4426950408889634

INT_MIN = -(2 ** 31)
NEG = -1e30

VMEM_LIMIT_BYTES = 56 * 1024 * 1024

PROJ_TN = 512
PROJ_TM = 1024
ROW_TM = 512
N_KV = N_KV_HEADS * HEAD_DIM
OFF_Q = 0
OFF_K = OFF_Q + D_ATTN
OFF_V = OFF_K + N_KV
OFF_Z = OFF_V + N_KV
OFF_QI = OFF_Z + D_ATTN
OFF_KI = OFF_QI + N_IDX_HEADS * IDX_DIM
OFF_WI = OFF_KI + IDX_DIM
OFF_H = OFF_WI + N_IDX_HEADS
OFF_B = OFF_H + D_CONV
OFF_C = OFF_B + D_CONV
OFF_ZC = OFF_C + D_CONV
D_IN = OFF_ZC + D_CONV
assert OFF_K == 2 * PROJ_TN and OFF_Z == 3 * PROJ_TN and OFF_QI == 5 * PROJ_TN and OFF_KI == 7 * PROJ_TN
KIW_ROWS = 128
assert OFF_KI % KIW_ROWS == 0 and OFF_WI - OFF_KI == IDX_DIM
CONV_CHUNK = 256
N_CONV_CHUNKS = D_CONV // CONV_CHUNK
J_Q = 0
J_KV = 2
J_Z = 3
J_QI = 5
J_CONV = 7
N_MAIN_BLOCKS = J_CONV
N_STEPS = J_CONV + N_CONV_CHUNKS

ATT_TQ = 256
ATT_TK = 256
ATT_SUB = 64
CNT_ACCS = 4

S_SCORE_PAGES = 64
S_ATTN_PAGES = 64


def _cparams(n_axes):
    return pltpu.CompilerParams(
        dimension_semantics=("arbitrary",) * n_axes,
        vmem_limit_bytes=VMEM_LIMIT_BYTES,
    )


def _rmsnorm_body(x_ref, g_ref, o_ref):
    x = x_ref[...]
    ms = jnp.mean(x * x, axis=-1, keepdims=True)
    o_ref[...] = (x * lax.rsqrt(ms + EPS) * g_ref[...]).astype(o_ref.dtype)


def _rmsnorm(x, g, tm):
    m = x.shape[0]
    return pl.pallas_call(
        _rmsnorm_body,
        grid=(m // tm,),
        in_specs=[pl.BlockSpec((tm, D_MODEL), lambda i: (i, 0)),
                  pl.BlockSpec((1, D_MODEL), lambda i: (0, 0))],
        out_specs=pl.BlockSpec((tm, D_MODEL), lambda i: (i, 0)),
        out_shape=jax.ShapeDtypeStruct((m, D_MODEL), BF16),
        compiler_params=_cparams(1),
        name="rmsnorm_in",
    )(x, g.reshape(1, D_MODEL))


def _silu(x):
    return x * jax.nn.sigmoid(x)


def _proj_body(sample, tps, tm, *refs):
    (xn_ref, wa_ref, wkiw_ref, wh_ref, wb_ref, wc_ref, wzc_ref, gq_ref, gk_ref, gki_ref,
     c16_ref, s16_ref, c8_ref, s8_ref,
     kc_ref, ksa_ref, ksb_ref, wconv_ref) = refs[:18]
    refs = refs[18:]
    if sample:
        e1_ref, e2_ref = refs[:2]
        refs = refs[2:]
    (qT_ref, qiT_ref, vT_ref, wT_ref, k32_ref, kbf_ref, v32_ref, ki32_ref, kibf_ref,
     gate_ref, mixc_ref, u_ref) = refs[:12]
    refs = refs[12:]
    if not sample:
        (carry_ref,) = refs

    i = pl.program_id(0)
    j = pl.program_id(1)
    nt = (((1,), (1,)), ((), ()))

    def xw(w):
        return lax.dot_general(xn_ref[...], w, nt, preferred_element_type=F32)

    def wx(w):
        return lax.dot_general(w, xn_ref[...], nt, preferred_element_type=F32)

    if not sample:
        @pl.when((i == 0) & (j == 0))
        def _init():
            carry_ref[...] = jnp.zeros(carry_ref.shape, F32)

    @pl.when(j < J_KV)
    def _q():
        res = wx(wa_ref[...])
        cos = c16_ref[...]
        sin = s16_ref[...]
        for hh in range(PROJ_TN // HEAD_DIM):
            blk = res[hh * HEAD_DIM:(hh + 1) * HEAD_DIM]
            ms = jnp.mean(blk * blk, axis=0, keepdims=True)
            y = blk * lax.rsqrt(ms + EPS) * gq_ref[...]
            x1 = y[0:ROPE_HALF]
            x2 = y[ROPE_HALF:ROPE_DIM]
            base = hh * HEAD_DIM
            qT_ref[base:base + ROPE_HALF, :] = ((x1 * cos - x2 * sin) * Q_PRESCALE).astype(BF16)
            qT_ref[base + ROPE_HALF:base + ROPE_DIM, :] = ((x2 * cos + x1 * sin) * Q_PRESCALE).astype(BF16)
            qT_ref[base + ROPE_DIM:base + HEAD_DIM, :] = (y[ROPE_DIM:] * Q_PRESCALE).astype(BF16)

    @pl.when((j >= J_QI) & (j < J_CONV))
    def _qi():
        res = wx(wa_ref[...])
        cos = c8_ref[...]
        sin = s8_ref[...]
        for hh in range(PROJ_TN // IDX_DIM):
            blk = res[hh * IDX_DIM:(hh + 1) * IDX_DIM]
            x1 = blk[0:IDX_ROPE_HALF]
            x2 = blk[IDX_ROPE_HALF:2 * IDX_ROPE_HALF]
            rot = jnp.concatenate([x1 * cos - x2 * sin, x2 * cos + x1 * sin], axis=0)
            base = hh * IDX_DIM
            qiT_ref[base:base + 2 * IDX_ROPE_HALF, :] = rot.astype(BF16)
            qiT_ref[base + 2 * IDX_ROPE_HALF:base + IDX_DIM, :] = blk[2 * IDX_ROPE_HALF:].astype(BF16)

    @pl.when(j == J_KV)
    def _kv():
        res = xw(wa_ref[...])
        for hd in range(N_KV_HEADS):
            hs = slice(hd * HEAD_DIM, (hd + 1) * HEAD_DIM)
            blk = res[:, hs]
            ms = jnp.mean(blk * blk, axis=-1, keepdims=True)
            yk = blk * lax.rsqrt(ms + EPS) * gk_ref[...]
            rot = yk * kc_ref[...] + (pltpu.roll(yk, HEAD_DIM - ROPE_HALF, axis=1) * ksa_ref[...]
                                      + pltpu.roll(yk, ROPE_HALF, axis=1) * ksb_ref[...])
            k32_ref[pl.ds(hd, tm, stride=N_KV_HEADS), :] = rot
            kbf_ref[:, hs] = rot.astype(BF16)
        v = res[:, N_KV:2 * N_KV]
        for hd in range(N_KV_HEADS):
            v32_ref[pl.ds(hd, tm, stride=N_KV_HEADS), :] = v[:, hd * HEAD_DIM:(hd + 1) * HEAD_DIM]
        if sample:
            vT_ref[...] = wx(wa_ref[N_KV:2 * N_KV, :]).astype(BF16)
        else:
            vT_ref[...] = v.T.astype(BF16)
        r2t = wx(wkiw_ref[...])
        wT_ref[...] = r2t[IDX_DIM:IDX_DIM + N_IDX_HEADS] * W_IDX_SCALE
        kit = r2t[0:IDX_DIM]
        ms = jnp.mean(kit * kit, axis=0, keepdims=True)
        yi = kit * lax.rsqrt(ms + EPS) * gki_ref[...]
        x1 = yi[0:IDX_ROPE_HALF]
        x2 = yi[IDX_ROPE_HALF:2 * IDX_ROPE_HALF]
        cos = c8_ref[...]
        sin = s8_ref[...]
        roti = jnp.concatenate([x1 * cos - x2 * sin, x2 * cos + x1 * sin, yi[2 * IDX_ROPE_HALF:],
                                jnp.zeros((KIW_ROWS - IDX_DIM, tm), F32)], axis=0)
        ki_nat = roti.T
        ki32_ref[...] = ki_nat[:, 0:IDX_DIM]
        kibf_ref[...] = ki_nat.astype(BF16)

    @pl.when((j >= J_Z) & (j < J_QI))
    def _z():
        gate_ref[...] = _silu(xw(wa_ref[...]))

    @pl.when(j >= J_CONV)
    def _conv():
        cc = j - J_CONV
        u = xw(wc_ref[...]) * xw(wh_ref[...])
        rowid = lax.broadcasted_iota(I32, (tm, CONV_CHUNK), 0)
        if sample:
            t = rowid & (sample - 1)
            u1 = jnp.where(t >= 1, pltpu.roll(u, 1, axis=0), e1_ref[...])
            u2 = jnp.where(t >= 2, pltpu.roll(u, 2, axis=0), e2_ref[...])
            u_ref[...] = u
        else:
            first = (i % tps) == 0
            prev = carry_ref[cc]
            p0 = jnp.where(first, 0.0, prev[0:1])
            p1 = jnp.where(first, 0.0, prev[1:2])
            u1 = jnp.where(rowid == 0, p1, pltpu.roll(u, 1, axis=0))
            u2 = jnp.where(rowid == 0, p0, jnp.where(rowid == 1, p1, pltpu.roll(u, 2, axis=0)))
            tail = u[tm - 8:tm]
            carry_ref[cc] = jnp.concatenate([tail[6:8], tail[0:6]], axis=0)
            u_ref[0] = tail[6:8]
        w = wconv_ref[...]
        y = u2 * w[0:1] + u1 * w[1:2] + u * w[2:3]
        mixc_ref[...] = (xw(wb_ref[...]) * y * _silu(xw(wzc_ref[...]))).astype(BF16)


def _project(xn, wT, params, tabs, sample, state_rows=None):
    m = xn.shape[0]
    assert sample & (sample - 1) == 0
    tm = m if sample else PROJ_TM
    n_i = m // tm
    tps = 1 if sample else (tabs["c16T"].shape[1] // tm)
    gq, gk, gki, wconv = params

    def tmap(i):
        return i % tps

    def cchunk(j):
        return jnp.clip(j - J_CONV, 0, N_CONV_CHUNKS - 1)

    def conv_rows(off):
        return pl.BlockSpec((pl.Element(CONV_CHUNK), pl.Element(D_MODEL)),
                            lambda i, j: (pl.multiple_of(off + CONV_CHUNK * cchunk(j), 16), 0))

    in_specs = [
        pl.BlockSpec((tm, D_MODEL), lambda i, j: (i, 0)),
        pl.BlockSpec((PROJ_TN, D_MODEL), lambda i, j: (jnp.minimum(j, N_MAIN_BLOCKS - 1), 0)),
        pl.BlockSpec((KIW_ROWS, D_MODEL), lambda i, j: (OFF_KI // KIW_ROWS, 0)),
        conv_rows(OFF_H), conv_rows(OFF_B), conv_rows(OFF_C), conv_rows(OFF_ZC),
        pl.BlockSpec((HEAD_DIM, 1), lambda i, j: (0, 0)),
        pl.BlockSpec((1, HEAD_DIM), lambda i, j: (0, 0)),
        pl.BlockSpec((IDX_DIM, 1), lambda i, j: (0, 0)),
        pl.BlockSpec((ROPE_HALF, tm), lambda i, j: (0, tmap(i))),
        pl.BlockSpec((ROPE_HALF, tm), lambda i, j: (0, tmap(i))),
        pl.BlockSpec((IDX_ROPE_HALF, tm), lambda i, j: (0, tmap(i))),
        pl.BlockSpec((IDX_ROPE_HALF, tm), lambda i, j: (0, tmap(i))),
        pl.BlockSpec((tm, HEAD_DIM), lambda i, j: (tmap(i), 0)),
        pl.BlockSpec((tm, HEAD_DIM), lambda i, j: (tmap(i), 0)),
        pl.BlockSpec((tm, HEAD_DIM), lambda i, j: (tmap(i), 0)),
        pl.BlockSpec((CONV_W, CONV_CHUNK), lambda i, j: (0, cchunk(j))),
    ]
    args = [xn, wT, wT, wT, wT, wT, wT, gq, gk, gki,
            tabs["c16T"], tabs["s16T"], tabs["c8T"], tabs["s8T"],
            tabs["kC"], tabs["kSa"], tabs["kSb"], wconv]
    cmap = lambda i, j: (i, cchunk(j))
    if sample:
        in_specs += [pl.BlockSpec((tm, CONV_CHUNK), cmap), pl.BlockSpec((tm, CONV_CHUNK), cmap)]
        args += list(state_rows)
        u_spec = pl.BlockSpec((tm, CONV_CHUNK), cmap)
        u_shape = jax.ShapeDtypeStruct((m, D_CONV), F32)
        scratch = []
    else:
        u_spec = pl.BlockSpec((1, CONV_W - 1, CONV_CHUNK), lambda i, j: (i, 0, cchunk(j)))
        u_shape = jax.ShapeDtypeStruct((n_i, CONV_W - 1, D_CONV), F32)
        scratch = [pltpu.VMEM((N_CONV_CHUNKS, 8, CONV_CHUNK), F32)]

    out_specs = [
        pl.BlockSpec((PROJ_TN, tm), lambda i, j: (jnp.minimum(j, 1), i)),
        pl.BlockSpec((PROJ_TN, tm), lambda i, j: (jnp.clip(j - J_QI, 0, 1), i)),
        pl.BlockSpec((N_KV, tm), lambda i, j: (0, i)),
        pl.BlockSpec((N_IDX_HEADS, tm), lambda i, j: (0, i)),
        pl.BlockSpec((N_KV_HEADS * tm, HEAD_DIM), lambda i, j: (i, 0)),
        pl.BlockSpec((tm, N_KV), lambda i, j: (i, 0)),
        pl.BlockSpec((N_KV_HEADS * tm, HEAD_DIM), lambda i, j: (i, 0)),
        pl.BlockSpec((tm, IDX_DIM), lambda i, j: (i, 0)),
        pl.BlockSpec((tm, KIW_ROWS), lambda i, j: (i, 0)),
        pl.BlockSpec((tm, PROJ_TN), lambda i, j: (i, jnp.clip(j - J_Z, 0, 1))),
        pl.BlockSpec((tm, CONV_CHUNK), cmap),
        u_spec,
    ]
    out_shape = [
        jax.ShapeDtypeStruct((D_ATTN, m), BF16),
        jax.ShapeDtypeStruct((N_IDX_HEADS * IDX_DIM, m), BF16),
        jax.ShapeDtypeStruct((N_KV, m), BF16),
        jax.ShapeDtypeStruct((N_IDX_HEADS, m), F32),
        jax.ShapeDtypeStruct((N_KV_HEADS * m, HEAD_DIM), F32),
        jax.ShapeDtypeStruct((m, N_KV), BF16),
        jax.ShapeDtypeStruct((N_KV_HEADS * m, HEAD_DIM), F32),
        jax.ShapeDtypeStruct((m, IDX_DIM), F32),
        jax.ShapeDtypeStruct((m, KIW_ROWS), BF16),
        jax.ShapeDtypeStruct((m, D_ATTN), F32),
        jax.ShapeDtypeStruct((m, D_CONV), BF16),
        u_shape,
    ]
    return pl.pallas_call(
        functools.partial(_proj_body, sample, tps, tm),
        grid=(n_i, N_STEPS),
        in_specs=in_specs,
        out_specs=out_specs,
        out_shape=out_shape,
        scratch_shapes=scratch,
        compiler_params=_cparams(2),
        name="proj_sample" if sample else "proj_prompt",
    )(*args)


KEY_LOWEST_FINITE = INT_MIN + 0x00800000


def _key_to_float(key):
    return pltpu.bitcast(key ^ ((key >> 31) & 0x7FFFFFFF), F32)


def _bit_value(b):
    return lax.shift_left(jnp.int32(1), jnp.int32(31) - b)


def _attn_prompt_body(qiT_ref, wT_ref, ki_ref, qT_ref, k_ref, vT_ref, gate_ref, o_ref,
                      sc_ref, bias_ref, acc_ref, s_ref):
    tq, tk = ATT_TQ, ATT_TK
    i = pl.program_id(1)
    nch = i + 1
    w = wT_ref[...]
    row = lax.broadcasted_iota(I32, (tk, tq), 0)
    col = lax.broadcasted_iota(I32, (tk, tq), 1)
    row8 = lax.broadcasted_iota(I32, (8, tq), 0)

    def score_chunk(j, carry):
        off = pl.multiple_of(j * tk, tk)
        kic = ki_ref[pl.ds(off, tk), 0:IDX_DIM]
        acc = jnp.zeros((tk, tq), F32)
        for h in range(N_IDX_HEADS):
            d = jnp.dot(kic, qiT_ref[h * IDX_DIM:(h + 1) * IDX_DIM, :], preferred_element_type=F32)
            acc = acc + w[h:h + 1, :] * jnp.maximum(d, 0.0)
        future = (row + j * tk) > (col + i * tq)
        sc_ref[pl.ds(off, tk), :] = jnp.where(future, -jnp.inf, acc)
        return carry

    lax.fori_loop(0, nch, score_chunk, 0)

    def count_keys(pred):
        def cnt_chunk(j, cs):
            off = pl.multiple_of(j * tk, tk)
            cs = list(cs)
            sc = sc_ref[pl.ds(off, tk), :]
            for r in range(tk // 8):
                a = cs[r % CNT_ACCS]
                kpos = row8 + (j * tk + r * 8)
                cs[r % CNT_ACCS] = jnp.where(pred(sc[r * 8:(r + 1) * 8], kpos), a + 1, a)
            return tuple(cs)

        cs = lax.fori_loop(0, nch, cnt_chunk, (jnp.zeros((8, tq), I32),) * CNT_ACCS)
        c = cs[0]
        for a in cs[1:]:
            c = c + a
        return jnp.sum(c.astype(F32), axis=0, keepdims=True)

    def bit_body(b, carry):
        thr, cge = carry
        cand = thr + _bit_value(b)
        cand_f = _key_to_float(cand)
        cnt = count_keys(lambda s, kpos: s >= cand_f)
        ok = cnt >= float(TOPK_MAX)
        return jnp.where(ok, cand, thr), jnp.where(ok, cnt, cge)

    thr, cge = lax.fori_loop(0, 32, bit_body, (jnp.full((1, tq), INT_MIN, I32),
                                               jnp.zeros((1, tq), F32)))
    thr_f = _key_to_float(jnp.maximum(thr, KEY_LOWEST_FINITE))
    has_ties = jnp.max(cge) > float(TOPK_MAX)

    @pl.when(jnp.logical_not(has_ties))
    def _plain_mask():
        def bias_chunk(j, carry):
            off = pl.multiple_of(j * tk, tk)
            bias_ref[pl.ds(off, tk), :] = jnp.where(sc_ref[pl.ds(off, tk), :] >= thr_f, 0.0, NEG)
            return carry

        lax.fori_loop(0, nch, bias_chunk, 0)

    @pl.when(has_ties)
    def _tie_mask():
        need = float(TOPK_MAX) - count_keys(lambda s, kpos: s > thr_f)
        nbits = (sc_ref.shape[0] - 1).bit_length()

        def pos_bit(b, last):
            step = lax.shift_left(jnp.int32(1), jnp.int32(nbits - 1) - b)
            probe = last + (step - 1)
            got = count_keys(lambda s, kpos: (s == thr_f) & (kpos <= probe))
            return jnp.where(got < need, last + step, last)

        last = lax.fori_loop(0, nbits, pos_bit, jnp.zeros((1, tq), I32))

        def bias_chunk(j, carry):
            off = pl.multiple_of(j * tk, tk)
            sc = sc_ref[pl.ds(off, tk), :]
            keep = (sc > thr_f) | ((sc == thr_f) & ((row + j * tk) <= last))
            bias_ref[pl.ds(off, tk), :] = jnp.where(keep, 0.0, NEG)
            return carry

        lax.fori_loop(0, nch, bias_chunk, 0)

    def qk_chunk(j, mrun):
        off = pl.multiple_of(j * tk, tk)
        bias = bias_ref[pl.ds(off, tk), :]
        out = []
        for h in range(N_HEADS):
            g = h // GROUP
            kc = k_ref[pl.ds(off, tk), g * HEAD_DIM:(g + 1) * HEAD_DIM]
            s = jnp.dot(kc, qT_ref[h * HEAD_DIM:(h + 1) * HEAD_DIM, :], preferred_element_type=F32) + bias
            s_ref[h, pl.ds(off, tk), :] = s
            out.append(jnp.maximum(mrun[h], s.reshape(tk // 8, 8, tq).max(axis=0)))
        return tuple(out)

    mrun = lax.fori_loop(0, nch, qk_chunk, (jnp.full((8, tq), NEG, F32),) * N_HEADS)
    ms = [jnp.max(mr, axis=0, keepdims=True) for mr in mrun]
    acc_ref[...] = jnp.zeros(acc_ref.shape, F32)

    def pv_chunk(j, lrun):
        off = pl.multiple_of(j * tk, tk)
        out = []
        for h in range(N_HEADS):
            g = h // GROUP
            hs = slice(h * HEAD_DIM, (h + 1) * HEAD_DIM)
            p = jnp.exp2(s_ref[h, pl.ds(off, tk), :] - ms[h])
            out.append(lrun[h] + p.reshape(tk // 8, 8, tq).sum(axis=0))
            vc = vT_ref[g * HEAD_DIM:(g + 1) * HEAD_DIM, pl.ds(off, tk)]
            acc_ref[hs, :] += jnp.dot(vc, p.astype(BF16), preferred_element_type=F32)
        return tuple(out)

    lrun = lax.fori_loop(0, nch, pv_chunk, (jnp.zeros((8, tq), F32),) * N_HEADS)

    for h in range(N_HEADS):
        hs = slice(h * HEAD_DIM, (h + 1) * HEAD_DIM)
        o = (acc_ref[hs, :] / jnp.sum(lrun[h], axis=0, keepdims=True)).T
        o_ref[:, hs] = (o * gate_ref[:, hs]).astype(BF16)


def _attn_prompt(qiT, wT, kibf, qT, kbf, vT, gate, n_batch, seq):
    m = n_batch * seq
    nq = seq // ATT_TQ
    nkv = N_KV_HEADS * HEAD_DIM
    qmap = lambda b, i: (0, b * nq + i)
    return pl.pallas_call(
        _attn_prompt_body,
        grid=(n_batch, nq),
        in_specs=[
            pl.BlockSpec((N_IDX_HEADS * IDX_DIM, ATT_TQ), qmap),
            pl.BlockSpec((N_IDX_HEADS, ATT_TQ), qmap),
            pl.BlockSpec((seq, 128), lambda b, i: (b, 0)),
            pl.BlockSpec((D_ATTN, ATT_TQ), qmap),
            pl.BlockSpec((seq, nkv), lambda b, i: (b, 0)),
            pl.BlockSpec((nkv, seq), lambda b, i: (0, b)),
            pl.BlockSpec((ATT_TQ, D_ATTN), lambda b, i: (b * nq + i, 0)),
        ],
        out_specs=pl.BlockSpec((ATT_TQ, D_ATTN), lambda b, i: (b * nq + i, 0)),
        out_shape=jax.ShapeDtypeStruct((m, D_ATTN), BF16),
        scratch_shapes=[pltpu.VMEM((seq, ATT_TQ), F32), pltpu.VMEM((seq, ATT_TQ), F32),
                        pltpu.VMEM((D_ATTN, ATT_TQ), F32),
                        pltpu.VMEM((N_HEADS, seq, ATT_TQ), F32)],
        compiler_params=_cparams(2),
        name="attn_prompt",
    )(qiT, wT, kibf, qT, kbf, vT, gate)


def _s_score_body(pt_ref, qi_ref, wcol_ref, kinew_ref, *rest):
    npg = S_SCORE_PAGES
    pages = rest[:npg]
    out_ref, outnew_ref = rest[npg:]
    pg = pl.program_id(1)
    qi = qi_ref[0]
    wc = wcol_ref[0]

    def scores(keys_t):
        d = jnp.dot(qi, keys_t, preferred_element_type=F32)
        val = jnp.maximum(d, 0.0) * wc
        return val.reshape(N_IDX_HEADS, 8, val.shape[-1]).sum(axis=0)

    for r in range(npg):
        out_ref[0, :, r * PAGE_SIZE:(r + 1) * PAGE_SIZE] = scores(pages[r][...].astype(BF16))

    @pl.when(pg == 0)
    def _new():
        outnew_ref[0] = scores(kinew_ref[0])


def _s_thresh_body(n_new, sp_ref, sn_ref, bp_ref, bn_ref):
    rows, past = sp_ref.shape
    ch = 2048
    nchunk = past // ch
    t = lax.broadcasted_iota(I32, (rows, 128), 0) & 7
    lane = lax.broadcasted_iota(I32, (rows, 128), 1)
    sn = jnp.where((lane < n_new) & (lane <= t), sn_ref[...], -jnp.inf)

    def fold(x):
        f = x[:, 0:128]
        for q in range(1, x.shape[1] // 128):
            f = f + x[:, q * 128:(q + 1) * 128]
        return f

    lane_ch = lax.broadcasted_iota(I32, (rows, ch), 1)

    def count_keys(pred):
        def cnt_chunk(c, acc):
            off = pl.multiple_of(c * ch, ch)
            return acc + fold(jnp.where(pred(sp_ref[:, pl.ds(off, ch)], lane_ch + c * ch), 1.0, 0.0))

        acc = lax.fori_loop(0, nchunk, cnt_chunk, jnp.where(pred(sn, lane + past), 1.0, 0.0))
        return jnp.sum(acc, axis=1, keepdims=True)

    def bit_body(b, carry):
        thr, cge = carry
        cand = thr + _bit_value(b)
        cand_f = _key_to_float(cand)
        cnt = count_keys(lambda s, kpos: s >= cand_f)
        ok = cnt >= float(TOPK_MAX)
        return jnp.where(ok, cand, thr), jnp.where(ok, cnt, cge)

    thr, cge = lax.fori_loop(0, 32, bit_body, (jnp.full((rows, 1), INT_MIN, I32),
                                               jnp.zeros((rows, 1), F32)))
    thr_f = _key_to_float(jnp.maximum(thr, KEY_LOWEST_FINITE))
    has_ties = jnp.max(cge) > float(TOPK_MAX)

    @pl.when(jnp.logical_not(has_ties))
    def _plain_mask():
        def to_bias(c, carry):
            off = pl.multiple_of(c * ch, ch)
            bp_ref[:, pl.ds(off, ch)] = jnp.where(sp_ref[:, pl.ds(off, ch)] >= thr_f, 0.0, NEG)
            return carry

        lax.fori_loop(0, nchunk, to_bias, 0)
        bn_ref[...] = jnp.where(sn >= thr_f, 0.0, NEG)

    @pl.when(has_ties)
    def _tie_mask():
        need = float(TOPK_MAX) - count_keys(lambda s, kpos: s > thr_f)
        nbits = (past + 128 - 1).bit_length()

        def pos_bit(b, last):
            step = lax.shift_left(jnp.int32(1), jnp.int32(nbits - 1) - b)
            probe = last + (step - 1)
            got = count_keys(lambda s, kpos: (s == thr_f) & (kpos <= probe))
            return jnp.where(got < need, last + step, last)

        last = lax.fori_loop(0, nbits, pos_bit, jnp.zeros((rows, 1), I32))

        def keep(s, kpos):
            return (s > thr_f) | ((s == thr_f) & (kpos <= last))

        def to_bias(c, carry):
            off = pl.multiple_of(c * ch, ch)
            bp_ref[:, pl.ds(off, ch)] = jnp.where(keep(sp_ref[:, pl.ds(off, ch)], lane_ch + c * ch), 0.0, NEG)
            return carry

        lax.fori_loop(0, nchunk, to_bias, 0)
        bn_ref[...] = jnp.where(keep(sn, lane + past), 0.0, NEG)


def _s_attn_body(n_steps, pt_ref, q_ref, bp_ref, bn_ref, knew_ref, vnew_ref, *rest):
    npg = S_ATTN_PAGES
    kpages = rest[:npg]
    vpages = rest[npg:2 * npg]
    out_ref, s_ref, p_ref, l_ref, acc_ref = rest[2 * npg:]
    t = pl.program_id(1)
    nt = (((1,), (1,)), ((), ()))
    rg = GROUP * 8
    width = npg * PAGE_SIZE
    past = n_steps * width
    sm_chunk = 2048

    def head_rows(pages, g):
        rows = [pg_ref[pl.ds(g, PAGE_SIZE, stride=N_KV_HEADS), :] for pg_ref in pages]
        return jnp.concatenate(rows, axis=0).astype(BF16)

    @pl.when(t < n_steps)
    def _logits():
        off = pl.multiple_of(t * width, width)
        bias = jnp.concatenate([bp_ref[0]] * GROUP, axis=0)
        for g in range(N_KV_HEADS):
            qg = q_ref[0, g * rg:(g + 1) * rg, :]
            s_ref[g * rg:(g + 1) * rg, pl.ds(off, width)] = lax.dot_general(
                qg, head_rows(kpages, g), nt, preferred_element_type=F32) + bias

    @pl.when(t == n_steps - 1)
    def _softmax():
        biasn = jnp.concatenate([bn_ref[0]] * GROUP, axis=0)
        for g in range(N_KV_HEADS):
            qg = q_ref[0, g * rg:(g + 1) * rg, :]
            kg = knew_ref[0, :, g * HEAD_DIM:(g + 1) * HEAD_DIM]
            s_ref[g * rg:(g + 1) * rg, past:past + PAGE_SIZE] = lax.dot_general(
                qg, kg, nt, preferred_element_type=F32) + biasn

        def fold(x, op):
            f = x[:, 0:128]
            for q in range(1, x.shape[1] // 128):
                f = op(f, x[:, q * 128:(q + 1) * 128])
            return f

        def max_chunk(c, m):
            off = pl.multiple_of(c * sm_chunk, sm_chunk)
            return jnp.maximum(m, fold(s_ref[:, pl.ds(off, sm_chunk)], jnp.maximum))

        m = lax.fori_loop(0, past // sm_chunk, max_chunk, s_ref[:, past:past + PAGE_SIZE])
        m = jnp.max(m, axis=1, keepdims=True)

        def exp_chunk(c, l):
            off = pl.multiple_of(c * sm_chunk, sm_chunk)
            p = jnp.exp2(s_ref[:, pl.ds(off, sm_chunk)] - m)
            p_ref[:, pl.ds(off, sm_chunk)] = p.astype(BF16)
            return l + fold(p, jnp.add)

        pn = jnp.exp2(s_ref[:, past:past + PAGE_SIZE] - m)
        p_ref[:, past:past + PAGE_SIZE] = pn.astype(BF16)
        l_ref[...] = lax.fori_loop(0, past // sm_chunk, exp_chunk, pn)
        acc_ref[...] = jnp.zeros(acc_ref.shape, F32)

    @pl.when(t >= n_steps)
    def _values():
        off = pl.multiple_of((t - n_steps) * width, width)
        for g in range(N_KV_HEADS):
            rs = slice(g * rg, (g + 1) * rg)
            acc_ref[rs, :] += jnp.dot(p_ref[rs, pl.ds(off, width)], head_rows(vpages, g),
                                      preferred_element_type=F32)

    @pl.when(t == 2 * n_steps - 1)
    def _finish():
        for g in range(N_KV_HEADS):
            rs = slice(g * rg, (g + 1) * rg)
            vg = vnew_ref[0, :, g * HEAD_DIM:(g + 1) * HEAD_DIM]
            acc_ref[rs, :] += jnp.dot(p_ref[rs, past:past + PAGE_SIZE], vg, preferred_element_type=F32)
        out_ref[0] = acc_ref[...] / jnp.sum(l_ref[...], axis=1, keepdims=True)


def _attn_sample(qT, qiT, wT, kibf, kbf, v32, cache_k, cache_v, cache_kidx, page_table, n_seq, n_tok):
    n_pages = page_table.shape[1]
    past = n_pages * PAGE_SIZE
    npg = S_SCORE_PAGES
    n_steps = n_pages // npg
    n_pool = cache_k.shape[0]

    def rows_ht(xT, n_heads, dim):
        x = xT.reshape(n_heads, dim, n_seq, n_tok).transpose(2, 0, 3, 1)
        x = jnp.pad(x, ((0, 0), (0, 0), (0, 8 - n_tok), (0, 0)))
        return x.reshape(n_seq, n_heads * 8, dim)

    qi_rows = rows_ht(qiT, N_IDX_HEADS, IDX_DIM)
    q_rows = rows_ht(qT, N_HEADS, HEAD_DIM)
    wcol = jnp.pad(wT.reshape(N_IDX_HEADS, n_seq, n_tok).transpose(1, 0, 2),
                   ((0, 0), (0, 0), (0, 8 - n_tok))).reshape(n_seq, N_IDX_HEADS * 8, 1)

    def pad_keys(x):
        x = x.reshape(n_seq, n_tok, x.shape[-1])
        return jnp.pad(x, ((0, 0), (0, PAGE_SIZE - n_tok), (0, 0)))

    ki_new_t = jnp.swapaxes(pad_keys(kibf[:, 0:IDX_DIM]), 1, 2)
    kidx_t = jnp.swapaxes(cache_kidx, 1, 2)
    k_new = pad_keys(kbf)
    v_new = pad_keys(v32.reshape(n_seq * n_tok, N_KV).astype(BF16))

    def page_spec(shape, r):
        return pl.BlockSpec((None,) + shape, lambda b, pg, pt, r=r: (pt[b, pg * npg + r], 0, 0))

    def kv_page_spec(shape, r, n, is_v):
        def imap(b, t, pt):
            step = jnp.clip(t - n, 0, n - 1) if is_v else jnp.minimum(t, n - 1)
            return (pt[b, step * S_ATTN_PAGES + r], 0, 0)
        return pl.BlockSpec((None,) + shape, imap)

    sp, sn = pl.pallas_call(
        _s_score_body,
        grid_spec=pltpu.PrefetchScalarGridSpec(
            num_scalar_prefetch=1,
            grid=(n_seq, n_steps),
            in_specs=[
                pl.BlockSpec((1, N_IDX_HEADS * 8, IDX_DIM), lambda b, pg, pt: (b, 0, 0)),
                pl.BlockSpec((1, N_IDX_HEADS * 8, 1), lambda b, pg, pt: (b, 0, 0)),
                pl.BlockSpec((1, IDX_DIM, PAGE_SIZE), lambda b, pg, pt: (b, 0, 0)),
            ] + [page_spec((IDX_DIM, PAGE_SIZE), r) for r in range(npg)],
            out_specs=[
                pl.BlockSpec((1, 8, npg * PAGE_SIZE), lambda b, pg, pt: (b, 0, pg)),
                pl.BlockSpec((1, 8, PAGE_SIZE), lambda b, pg, pt: (b, 0, 0)),
            ],
        ),
        out_shape=[jax.ShapeDtypeStruct((n_seq, 8, past), F32),
                   jax.ShapeDtypeStruct((n_seq, 8, PAGE_SIZE), F32)],
        compiler_params=_cparams(2),
        name="sample_scores",
    )(page_table, qi_rows, wcol, ki_new_t, *([kidx_t] * npg))

    rows = n_seq * 8
    bp, bn = pl.pallas_call(
        functools.partial(_s_thresh_body, n_tok),
        grid=(1,),
        in_specs=[pl.BlockSpec((rows, past), lambda i: (0, 0)),
                  pl.BlockSpec((rows, PAGE_SIZE), lambda i: (0, 0))],
        out_specs=[pl.BlockSpec((rows, past), lambda i: (0, 0)),
                   pl.BlockSpec((rows, PAGE_SIZE), lambda i: (0, 0))],
        out_shape=[jax.ShapeDtypeStruct((rows, past), F32),
                   jax.ShapeDtypeStruct((rows, PAGE_SIZE), F32)],
        compiler_params=_cparams(1),
        name="sample_topk_mask",
    )(sp.reshape(rows, past), sn.reshape(rows, PAGE_SIZE))
    bp = bp.reshape(n_seq, 8, past)
    bn = bn.reshape(n_seq, 8, PAGE_SIZE)

    nrow = N_HEADS * 8
    kv_rows = PAGE_SIZE * N_KV_HEADS
    ck = cache_k.reshape(n_pool, kv_rows, HEAD_DIM)
    cv = cache_v.reshape(n_pool, kv_rows, HEAD_DIM)
    na = n_pages // S_ATTN_PAGES
    width = S_ATTN_PAGES * PAGE_SIZE
    out = pl.pallas_call(
        functools.partial(_s_attn_body, na),
        grid_spec=pltpu.PrefetchScalarGridSpec(
            num_scalar_prefetch=1,
            grid=(n_seq, 2 * na),
            in_specs=[
                pl.BlockSpec((1, nrow, HEAD_DIM), lambda b, t, pt: (b, 0, 0)),
                pl.BlockSpec((1, 8, width), lambda b, t, pt: (b, 0, jnp.minimum(t, na - 1))),
                pl.BlockSpec((1, 8, PAGE_SIZE), lambda b, t, pt: (b, 0, 0)),
                pl.BlockSpec((1, PAGE_SIZE, N_KV), lambda b, t, pt: (b, 0, 0)),
                pl.BlockSpec((1, PAGE_SIZE, N_KV), lambda b, t, pt: (b, 0, 0)),
            ] + [kv_page_spec((kv_rows, HEAD_DIM), r, na, False) for r in range(S_ATTN_PAGES)]
              + [kv_page_spec((kv_rows, HEAD_DIM), r, na, True) for r in range(S_ATTN_PAGES)],
            out_specs=pl.BlockSpec((1, nrow, HEAD_DIM), lambda b, t, pt: (b, 0, 0)),
            scratch_shapes=[pltpu.VMEM((nrow, past + PAGE_SIZE), F32),
                            pltpu.VMEM((nrow, past + PAGE_SIZE), BF16),
                            pltpu.VMEM((nrow, 128), F32), pltpu.VMEM((nrow, HEAD_DIM), F32)],
        ),
        out_shape=jax.ShapeDtypeStruct((n_seq, nrow, HEAD_DIM), F32),
        compiler_params=_cparams(2),
        name="sample_attn",
    )(page_table, q_rows, bp, bn, k_new, v_new, *([ck] * S_ATTN_PAGES), *([cv] * S_ATTN_PAGES))
    out = out.reshape(n_seq, N_HEADS, 8, HEAD_DIM)[:, :, 0:n_tok]
    return out.transpose(0, 2, 1, 3).reshape(n_seq * n_tok, D_ATTN)


def _outproj_body(gated, *refs):
    if gated:
        x_ref, a_ref, gate_ref, mc_ref, wo_ref, o_ref = refs
        ma = (a_ref[...] * gate_ref[...]).astype(BF16)
    else:
        x_ref, a_ref, mc_ref, wo_ref, o_ref = refs
        ma = a_ref[...]
    acc = jnp.dot(ma, wo_ref[0:D_ATTN, :], preferred_element_type=F32)
    acc = acc + jnp.dot(mc_ref[...], wo_ref[D_ATTN:D_ATTN + D_CONV, :], preferred_element_type=F32)
    o_ref[...] = x_ref[...] + acc


def _outproj(x, attn, gate, mixc, wo, tm):
    m = x.shape[0]
    gated = gate is not None
    row = lambda i: (i, 0)
    in_specs = [pl.BlockSpec((tm, D_MODEL), row), pl.BlockSpec((tm, D_ATTN), row)]
    args = [x, attn]
    if gated:
        in_specs.append(pl.BlockSpec((tm, D_ATTN), row))
        args.append(gate)
    in_specs += [pl.BlockSpec((tm, D_CONV), row),
                 pl.BlockSpec((D_ATTN + D_CONV, D_MODEL), lambda i: (0, 0))]
    args += [mixc, wo]
    return pl.pallas_call(
        functools.partial(_outproj_body, gated),
        grid=(m // tm,),
        in_specs=in_specs,
        out_specs=pl.BlockSpec((tm, D_MODEL), row),
        out_shape=jax.ShapeDtypeStruct((m, D_MODEL), F32),
        compiler_params=_cparams(1),
        name="outproj_sample" if gated else "outproj_prompt",
    )(*args)


def _rope_tables(pos):
    posf = pos.astype(F32)[:, None]
    n = pos.shape[0]

    def cs(half):
        inv = ROPE_THETA ** (-jnp.arange(half, dtype=F32) / half)
        ang = posf * inv[None, :]
        return jnp.cos(ang), jnp.sin(ang)

    c16, s16 = cs(ROPE_HALF)
    c8, s8 = cs(IDX_ROPE_HALF)
    one = lambda w: jnp.ones((n, w), F32)
    zero = lambda w: jnp.zeros((n, w), F32)
    rest = HEAD_DIM - ROPE_DIM
    k_c = jnp.concatenate([c16, c16, one(rest)], axis=1)
    k_sa = jnp.concatenate([-s16, zero(HEAD_DIM - ROPE_HALF)], axis=1)
    k_sb = jnp.concatenate([zero(ROPE_HALF), s16, zero(rest)], axis=1)
    return dict(c16T=c16.T, s16T=s16.T, c8T=c8.T, s8T=s8.T, kC=k_c, kSa=k_sa, kSb=k_sb)


def _prep_weights(w_in, g_q, g_k, g_kidx, w_conv, w_out):
    assert w_in.shape == (D_MODEL, D_IN)
    wT = w_in.T.astype(BF16)
    gq = g_q.reshape(HEAD_DIM, 1)
    gk = g_k.reshape(1, HEAD_DIM)
    gki = g_kidx.reshape(IDX_DIM, 1)
    return wT, (gq, gk, gki, w_conv), w_out.astype(BF16)


def kernel(x_prompt, x_sample, cache_k, cache_v, cache_kidx, state_conv, page_table,
           norm_in, w_in, g_q, g_k, g_kidx, w_conv, w_out):
    n_b, seq, _ = x_prompt.shape
    n_s, n_t, _ = x_sample.shape
    depth = w_in.shape[0]
    past = page_table.shape[1] * PAGE_SIZE
    tabs_p = _rope_tables(jnp.arange(seq))
    tabs_s = _rope_tables(jnp.tile(past + jnp.arange(n_t), n_s))

    hp = x_prompt.reshape(n_b * seq, D_MODEL)
    hs = x_sample.reshape(n_s * n_t, D_MODEL)
    outs = [[] for _ in range(8)]
    for l in range(depth):
        w_t, params, wo = _prep_weights(w_in[l], g_q[l], g_k[l], g_kidx[l], w_conv[l], w_out[l])

        xn = _rmsnorm(hp, norm_in[l], ROW_TM)
        (qT, qiT, vT, wT, k32, kbf, v32, ki32, kibf, gate, mixc, utail) = _project(
            xn, w_t, params, tabs_p, 0)
        mixa = _attn_prompt(qiT, wT, kibf, qT, kbf, vT, gate, n_b, seq)
        hp = _outproj(hp, mixa, None, mixc, wo, ROW_TM)
        tps = seq // PROJ_TM
        outs[0].append(k32.reshape(n_b, seq, N_KV_HEADS, HEAD_DIM))
        outs[1].append(v32.reshape(n_b, seq, N_KV_HEADS, HEAD_DIM))
        outs[2].append(ki32.reshape(n_b, seq, IDX_DIM))
        outs[3].append(utail[tps - 1::tps])

        st = state_conv[l]
        tok = jnp.arange(n_t)
        e1 = st[:, jnp.full((n_t,), CONV_W - 2)].reshape(n_s * n_t, D_CONV)
        e2 = st[:, jnp.minimum(tok, CONV_W - 2)].reshape(n_s * n_t, D_CONV)
        xn = _rmsnorm(hs, norm_in[l], n_s * n_t)
        (qT, qiT, vT, wT, k32, kbf, v32, ki32, kibf, gate, mixc, u) = _project(
            xn, w_t, params, tabs_s, n_t, state_rows=(e1, e2))
        attn = _attn_sample(qT, qiT, wT, kibf, kbf, v32, cache_k[l], cache_v[l], cache_kidx[l],
                            page_table, n_s, n_t)
        hs = _outproj(hs, attn, gate, mixc, wo, n_s * n_t)
        outs[4].append(k32.reshape(n_s, n_t, N_KV_HEADS, HEAD_DIM))
        outs[5].append(v32.reshape(n_s, n_t, N_KV_HEADS, HEAD_DIM))
        outs[6].append(ki32.reshape(n_s, n_t, IDX_DIM))
        outs[7].append(u.reshape(n_s, n_t, D_CONV)[:, n_t - (CONV_W - 1):])

    return (hp.reshape(n_b, seq, D_MODEL), hs.reshape(n_s, n_t, D_MODEL),
            *[jnp.stack(o) for o in outs])
```

```python
import functools

import jax
import jax.numpy as jnp
import numpy as np
from jax import lax
from jax.experimental import pallas as pl
from jax.experimental.pallas import tpu as pltpu

F32 = jnp.float32
BF16 = jnp.bfloat16
I32 = jnp.int32

D_MODEL = 2048
HEAD_DIM = 128
N_HEADS = 8
N_KV_HEADS = 2
GROUP = N_HEADS // N_KV_HEADS
D_ATTN = N_HEADS * HEAD_DIM
D_CONV = 1024
ROPE_DIM = HEAD_DIM // 4
ROPE_HALF = ROPE_DIM // 2
ROPE_THETA = 500000.0
N_IDX_HEADS = 16
IDX_DIM = 64
IDX_ROPE_HALF = IDX_DIM // 8
TOPK_MAX = 256
CONV_W = 3
PAGE_SIZE = 128
EPS = 1e-6
W_IDX_SCALE = (N_IDX_HEADS ** -0.5) * (IDX_DIM ** -0.5)
ATTN_SCALE = HEAD_DIM ** -0.5
Q_PRESCALE = ATTN_SCALE * 1.4426950408889634

INT_MIN = -(2 ** 31)
NEG = -1e30

VMEM_LIMIT_BYTES = 56 * 1024 * 1024

PROJ_TN = 512
PROJ_TM = 1024
PROJ_SUB = 256
ROW_TM = 512
N_KV = N_KV_HEADS * HEAD_DIM
OFF_Q = 0
OFF_K = OFF_Q + D_ATTN
OFF_V = OFF_K + N_KV
OFF_Z = OFF_V + N_KV
OFF_QI = OFF_Z + D_ATTN
OFF_KI = OFF_QI + N_IDX_HEADS * IDX_DIM
OFF_WI = OFF_KI + IDX_DIM
OFF_H = OFF_WI + N_IDX_HEADS
OFF_B = OFF_H + D_CONV
OFF_C = OFF_B + D_CONV
OFF_ZC = OFF_C + D_CONV
D_IN = OFF_ZC + D_CONV
assert OFF_K == 2 * PROJ_TN and OFF_Z == 3 * PROJ_TN and OFF_QI == 5 * PROJ_TN and OFF_KI == 7 * PROJ_TN
KIW_ROWS = 128
assert OFF_KI % KIW_ROWS == 0 and OFF_WI - OFF_KI == IDX_DIM
CONV_CHUNK = 256
N_CONV_CHUNKS = D_CONV // CONV_CHUNK
J_Q = 0
J_KV = 2
J_Z = 3
J_QI = 5
J_CONV = 7
N_MAIN_BLOCKS = J_CONV
N_STEPS = J_CONV + N_CONV_CHUNKS

ATT_TQ = 256
ATT_TK = 256
ATT_SUB = 64
CNT_ACCS = 4

S_SCORE_PAGES = 64
S_ATTN_PAGES = 64


def _cparams(n_axes):
    return pltpu.CompilerParams(
        dimension_semantics=("arbitrary",) * n_axes,
        vmem_limit_bytes=VMEM_LIMIT_BYTES,
    )


def _rmsnorm_body(x_ref, g_ref, o_ref):
    x = x_ref[...]
    ms = jnp.mean(x * x, axis=-1, keepdims=True)
    o_ref[...] = (x * lax.rsqrt(ms + EPS) * g_ref[...]).astype(o_ref.dtype)


def _rmsnorm(x, g, tm):
    m = x.shape[0]
    return pl.pallas_call(
        _rmsnorm_body,
        grid=(m // tm,),
        in_specs=[pl.BlockSpec((tm, D_MODEL), lambda i: (i, 0)),
                  pl.BlockSpec((1, D_MODEL), lambda i: (0, 0))],
        out_specs=pl.BlockSpec((tm, D_MODEL), lambda i: (i, 0)),
        out_shape=jax.ShapeDtypeStruct((m, D_MODEL), BF16),
        compiler_params=_cparams(1),
        name="rmsnorm_in",
    )(x, g.reshape(1, D_MODEL))


def _silu(x):
    return x * jax.nn.sigmoid(x)


def _proj_body(sample, tps, tm, *refs):
    (xn_ref, wa_ref, wkiw_ref, wh_ref, wb_ref, wc_ref, wzc_ref, gq_ref, gk_ref, gki_ref,
     c16_ref, s16_ref, c8_ref, s8_ref,
     kc_ref, ksa_ref, ksb_ref, wconv_ref) = refs[:18]
    refs = refs[18:]
    if sample:
        e1_ref, e2_ref = refs[:2]
        refs = refs[2:]
    (qT_ref, qiT_ref, vT_ref, wT_ref, k32_ref, kbf_ref, v32_ref, ki32_ref, kibf_ref,
     gate_ref, mixc_ref, u_ref) = refs[:12]
    refs = refs[12:]
    if not sample:
        (carry_ref,) = refs

    i = pl.program_id(0)
    j = pl.program_id(1)
    nt = (((1,), (1,)), ((), ()))

    def xw(w, x=None):
        return lax.dot_general(xn_ref[...] if x is None else x, w, nt, preferred_element_type=F32)

    def wx(w, x):
        return lax.dot_general(w, x, nt, preferred_element_type=F32)

    nsub = max(tm // PROJ_SUB, 1)
    sub = tm // nsub
    pieces = [slice(rb * sub, (rb + 1) * sub) for rb in range(nsub)]

    if not sample:
        @pl.when((i == 0) & (j == 0))
        def _init():
            carry_ref[...] = jnp.zeros(carry_ref.shape, F32)

    @pl.when(j < J_KV)
    def _q():
        for cs in pieces:
            res = wx(wa_ref[...], xn_ref[cs, :])
            cos = c16_ref[:, cs]
            sin = s16_ref[:, cs]
            for hh in range(PROJ_TN // HEAD_DIM):
                blk = res[hh * HEAD_DIM:(hh + 1) * HEAD_DIM]
                ms = jnp.mean(blk * blk, axis=0, keepdims=True)
                y = blk * lax.rsqrt(ms + EPS) * gq_ref[...]
                x1 = y[0:ROPE_HALF]
                x2 = y[ROPE_HALF:ROPE_DIM]
                base = hh * HEAD_DIM
                qT_ref[base:base + ROPE_HALF, cs] = ((x1 * cos - x2 * sin) * Q_PRESCALE).astype(BF16)
                qT_ref[base + ROPE_HALF:base + ROPE_DIM, cs] = ((x2 * cos + x1 * sin) * Q_PRESCALE).astype(BF16)
                qT_ref[base + ROPE_DIM:base + HEAD_DIM, cs] = (y[ROPE_DIM:] * Q_PRESCALE).astype(BF16)

    @pl.when((j >= J_QI) & (j < J_CONV))
    def _qi():
        for cs in pieces:
            res = wx(wa_ref[...], xn_ref[cs, :])
            cos = c8_ref[:, cs]
            sin = s8_ref[:, cs]
            for hh in range(PROJ_TN // IDX_DIM):
                blk = res[hh * IDX_DIM:(hh + 1) * IDX_DIM]
                x1 = blk[0:IDX_ROPE_HALF]
                x2 = blk[IDX_ROPE_HALF:2 * IDX_ROPE_HALF]
                rot = jnp.concatenate([x1 * cos - x2 * sin, x2 * cos + x1 * sin], axis=0)
                base = hh * IDX_DIM
                qiT_ref[base:base + 2 * IDX_ROPE_HALF, cs] = rot.astype(BF16)
                qiT_ref[base + 2 * IDX_ROPE_HALF:base + IDX_DIM, cs] = blk[2 * IDX_ROPE_HALF:].astype(BF16)

    @pl.when(j == J_KV)
    def _kv():
        for rb, cs in enumerate(pieces):
            x = xn_ref[cs, :]
            res = xw(wa_ref[...], x)
            for hd in range(N_KV_HEADS):
                hs = slice(hd * HEAD_DIM, (hd + 1) * HEAD_DIM)
                blk = res[:, hs]
                ms = jnp.mean(blk * blk, axis=-1, keepdims=True)
                yk = blk * lax.rsqrt(ms + EPS) * gk_ref[...]
                rot = yk * kc_ref[cs, :] + (pltpu.roll(yk, HEAD_DIM - ROPE_HALF, axis=1) * ksa_ref[cs, :]
                                            + pltpu.roll(yk, ROPE_HALF, axis=1) * ksb_ref[cs, :])
                k32_ref[pl.ds(N_KV_HEADS * rb * sub + hd, sub, stride=N_KV_HEADS), :] = rot
                kbf_ref[cs, hs] = rot.astype(BF16)
            v = res[:, N_KV:2 * N_KV]
            for hd in range(N_KV_HEADS):
                v32_ref[pl.ds(N_KV_HEADS * rb * sub + hd, sub, stride=N_KV_HEADS), :] = (
                    v[:, hd * HEAD_DIM:(hd + 1) * HEAD_DIM])
            if sample:
                vT_ref[:, cs] = wx(wa_ref[N_KV:2 * N_KV, :], x).astype(BF16)
            else:
                vT_ref[:, cs] = v.T.astype(BF16)
            r2t = wx(wkiw_ref[...], x)
            wT_ref[:, cs] = r2t[IDX_DIM:IDX_DIM + N_IDX_HEADS] * W_IDX_SCALE
            kit = r2t[0:IDX_DIM]
            ms = jnp.mean(kit * kit, axis=0, keepdims=True)
            yi = kit * lax.rsqrt(ms + EPS) * gki_ref[...]
            x1 = yi[0:IDX_ROPE_HALF]
            x2 = yi[IDX_ROPE_HALF:2 * IDX_ROPE_HALF]
            cos = c8_ref[:, cs]
            sin = s8_ref[:, cs]
            roti = jnp.concatenate([x1 * cos - x2 * sin, x2 * cos + x1 * sin, yi[2 * IDX_ROPE_HALF:],
                                    jnp.zeros((KIW_ROWS - IDX_DIM, sub), F32)], axis=0)
            ki_nat = roti.T
            ki32_ref[cs, :] = ki_nat[:, 0:IDX_DIM]
            kibf_ref[cs, :] = ki_nat.astype(BF16)

    @pl.when((j >= J_Z) & (j < J_QI))
    def _z():
        for cs in pieces:
            gate_ref[cs, :] = _silu(xw(wa_ref[...], xn_ref[cs, :]))

    @pl.when(j >= J_CONV)
    def _conv():
        cc = j - J_CONV
        u = xw(wc_ref[...]) * xw(wh_ref[...])
        rowid = lax.broadcasted_iota(I32, (tm, CONV_CHUNK), 0)
        if sample:
            t = rowid & (sample - 1)
            u1 = jnp.where(t >= 1, pltpu.roll(u, 1, axis=0), e1_ref[...])
            u2 = jnp.where(t >= 2, pltpu.roll(u, 2, axis=0), e2_ref[...])
            u_ref[...] = u
        else:
            first = (i % tps) == 0
            prev = carry_ref[cc]
            p0 = jnp.where(first, 0.0, prev[0:1])
            p1 = jnp.where(first, 0.0, prev[1:2])
            u1 = jnp.where(rowid == 0, p1, pltpu.roll(u, 1, axis=0))
            u2 = jnp.where(rowid == 0, p0, jnp.where(rowid == 1, p1, pltpu.roll(u, 2, axis=0)))
            tail = u[tm - 8:tm]
            carry_ref[cc] = jnp.concatenate([tail[6:8], tail[0:6]], axis=0)
            u_ref[0] = tail[6:8]
        w = wconv_ref[...]
        y = u2 * w[0:1] + u1 * w[1:2] + u * w[2:3]
        mixc_ref[...] = (xw(wb_ref[...]) * y * _silu(xw(wzc_ref[...]))).astype(BF16)


def _project(xn, wT, params, tabs, sample, state_rows=None):
    m = xn.shape[0]
    assert sample & (sample - 1) == 0
    tm = m if sample else PROJ_TM
    n_i = m // tm
    tps = 1 if sample else (tabs["c16T"].shape[1] // tm)
    gq, gk, gki, wconv = params

    def tmap(i):
        return i % tps

    def cchunk(j):
        return jnp.clip(j - J_CONV, 0, N_CONV_CHUNKS - 1)

    def conv_rows(off):
        return pl.BlockSpec((pl.Element(CONV_CHUNK), pl.Element(D_MODEL)),
                            lambda i, j: (pl.multiple_of(off + CONV_CHUNK * cchunk(j), 16), 0))

    in_specs = [
        pl.BlockSpec((tm, D_MODEL), lambda i, j: (i, 0)),
        pl.BlockSpec((PROJ_TN, D_MODEL), lambda i, j: (jnp.minimum(j, N_MAIN_BLOCKS - 1), 0)),
        pl.BlockSpec((KIW_ROWS, D_MODEL), lambda i, j: (OFF_KI // KIW_ROWS, 0)),
        conv_rows(OFF_H), conv_rows(OFF_B), conv_rows(OFF_C), conv_rows(OFF_ZC),
        pl.BlockSpec((HEAD_DIM, 1), lambda i, j: (0, 0)),
        pl.BlockSpec((1, HEAD_DIM), lambda i, j: (0, 0)),
        pl.BlockSpec((IDX_DIM, 1), lambda i, j: (0, 0)),
        pl.BlockSpec((ROPE_HALF, tm), lambda i, j: (0, tmap(i))),
        pl.BlockSpec((ROPE_HALF, tm), lambda i, j: (0, tmap(i))),
        pl.BlockSpec((IDX_ROPE_HALF, tm), lambda i, j: (0, tmap(i))),
        pl.BlockSpec((IDX_ROPE_HALF, tm), lambda i, j: (0, tmap(i))),
        pl.BlockSpec((tm, HEAD_DIM), lambda i, j: (tmap(i), 0)),
        pl.BlockSpec((tm, HEAD_DIM), lambda i, j: (tmap(i), 0)),
        pl.BlockSpec((tm, HEAD_DIM), lambda i, j: (tmap(i), 0)),
        pl.BlockSpec((CONV_W, CONV_CHUNK), lambda i, j: (0, cchunk(j))),
    ]
    args = [xn, wT, wT, wT, wT, wT, wT, gq, gk, gki,
            tabs["c16T"], tabs["s16T"], tabs["c8T"], tabs["s8T"],
            tabs["kC"], tabs["kSa"], tabs["kSb"], wconv]
    cmap = lambda i, j: (i, cchunk(j))
    if sample:
        in_specs += [pl.BlockSpec((tm, CONV_CHUNK), cmap), pl.BlockSpec((tm, CONV_CHUNK), cmap)]
        args += list(state_rows)
        u_spec = pl.BlockSpec((tm, CONV_CHUNK), cmap)
        u_shape = jax.ShapeDtypeStruct((m, D_CONV), F32)
        scratch = []
    else:
        u_spec = pl.BlockSpec((1, CONV_W - 1, CONV_CHUNK), lambda i, j: (i, 0, cchunk(j)))
        u_shape = jax.ShapeDtypeStruct((n_i, CONV_W - 1, D_CONV), F32)
        scratch = [pltpu.VMEM((N_CONV_CHUNKS, 8, CONV_CHUNK), F32)]

    out_specs = [
        pl.BlockSpec((PROJ_TN, tm), lambda i, j: (jnp.minimum(j, 1), i)),
        pl.BlockSpec((PROJ_TN, tm), lambda i, j: (jnp.clip(j - J_QI, 0, 1), i)),
        pl.BlockSpec((N_KV, tm), lambda i, j: (0, i)),
        pl.BlockSpec((N_IDX_HEADS, tm), lambda i, j: (0, i)),
        pl.BlockSpec((N_KV_HEADS * tm, HEAD_DIM), lambda i, j: (i, 0)),
        pl.BlockSpec((tm, N_KV), lambda i, j: (i, 0)),
        pl.BlockSpec((N_KV_HEADS * tm, HEAD_DIM), lambda i, j: (i, 0)),
        pl.BlockSpec((tm, IDX_DIM), lambda i, j: (i, 0)),
        pl.BlockSpec((tm, KIW_ROWS), lambda i, j: (i, 0)),
        pl.BlockSpec((tm, PROJ_TN), lambda i, j: (i, jnp.clip(j - J_Z, 0, 1))),
        pl.BlockSpec((tm, CONV_CHUNK), cmap),
        u_spec,
    ]
    out_shape = [
        jax.ShapeDtypeStruct((D_ATTN, m), BF16),
        jax.ShapeDtypeStruct((N_IDX_HEADS * IDX_DIM, m), BF16),
        jax.ShapeDtypeStruct((N_KV, m), BF16),
        jax.ShapeDtypeStruct((N_IDX_HEADS, m), F32),
        jax.ShapeDtypeStruct((N_KV_HEADS * m, HEAD_DIM), F32),
        jax.ShapeDtypeStruct((m, N_KV), BF16),
        jax.ShapeDtypeStruct((N_KV_HEADS * m, HEAD_DIM), F32),
        jax.ShapeDtypeStruct((m, IDX_DIM), F32),
        jax.ShapeDtypeStruct((m, KIW_ROWS), BF16),
        jax.ShapeDtypeStruct((m, D_ATTN), F32),
        jax.ShapeDtypeStruct((m, D_CONV), BF16),
        u_shape,
    ]
    return pl.pallas_call(
        functools.partial(_proj_body, sample, tps, tm),
        grid=(n_i, N_STEPS),
        in_specs=in_specs,
        out_specs=out_specs,
        out_shape=out_shape,
        scratch_shapes=scratch,
        compiler_params=_cparams(2),
        name="proj_sample" if sample else "proj_prompt",
    )(*args)


KEY_LOWEST_FINITE = INT_MIN + 0x00800000


def _key_to_float(key):
    return pltpu.bitcast(key ^ ((key >> 31) & 0x7FFFFFFF), F32)


def _bit_value(b):
    return lax.shift_left(jnp.int32(1), jnp.int32(31) - b)


def _attn_prompt_body(qiT_ref, wT_ref, ki_ref, qT_ref, k_ref, vT_ref, gate_ref, o_ref,
                      sc_ref, bias_ref, acc_ref, s_ref):
    tq, tk = ATT_TQ, ATT_TK
    i = pl.program_id(1)
    nch = i + 1
    w = wT_ref[...]
    row = lax.broadcasted_iota(I32, (tk, tq), 0)
    col = lax.broadcasted_iota(I32, (tk, tq), 1)
    row8 = lax.broadcasted_iota(I32, (8, tq), 0)

    def score_chunk(j, carry):
        off = pl.multiple_of(j * tk, tk)
        kic = ki_ref[pl.ds(off, tk), 0:IDX_DIM]
        acc = jnp.zeros((tk, tq), F32)
        for h in range(N_IDX_HEADS):
            d = jnp.dot(kic, qiT_ref[h * IDX_DIM:(h + 1) * IDX_DIM, :], preferred_element_type=F32)
            acc = acc + w[h:h + 1, :] * jnp.maximum(d, 0.0)
        future = (row + j * tk) > (col + i * tq)
        sc_ref[pl.ds(off, tk), :] = jnp.where(future, -jnp.inf, acc)
        return carry

    lax.fori_loop(0, nch, score_chunk, 0)

    def count_keys(pred):
        def cnt_chunk(j, cs):
            off = pl.multiple_of(j * tk, tk)
            cs = list(cs)
            sc = sc_ref[pl.ds(off, tk), :]
            for r in range(tk // 8):
                a = cs[r % CNT_ACCS]
                kpos = row8 + (j * tk + r * 8)
                cs[r % CNT_ACCS] = jnp.where(pred(sc[r * 8:(r + 1) * 8], kpos), a + 1, a)
            return tuple(cs)

        cs = lax.fori_loop(0, nch, cnt_chunk, (jnp.zeros((8, tq), I32),) * CNT_ACCS)
        c = cs[0]
        for a in cs[1:]:
            c = c + a
        return jnp.sum(c.astype(F32), axis=0, keepdims=True)

    def bit_body(b, carry):
        thr, cge = carry
        cand = thr + _bit_value(b)
        cand_f = _key_to_float(cand)
        cnt = count_keys(lambda s, kpos: s >= cand_f)
        ok = cnt >= float(TOPK_MAX)
        return jnp.where(ok, cand, thr), jnp.where(ok, cnt, cge)

    thr, cge = lax.fori_loop(0, 32, bit_body, (jnp.full((1, tq), INT_MIN, I32),
                                               jnp.zeros((1, tq), F32)))
    thr_f = _key_to_float(jnp.maximum(thr, KEY_LOWEST_FINITE))
    has_ties = jnp.max(cge) > float(TOPK_MAX)

    @pl.when(jnp.logical_not(has_ties))
    def _plain_mask():
        def bias_chunk(j, carry):
            off = pl.multiple_of(j * tk, tk)
            bias_ref[pl.ds(off, tk), :] = jnp.where(sc_ref[pl.ds(off, tk), :] >= thr_f, 0.0, NEG)
            return carry

        lax.fori_loop(0, nch, bias_chunk, 0)

    @pl.when(has_ties)
    def _tie_mask():
        need = float(TOPK_MAX) - count_keys(lambda s, kpos: s > thr_f)
        nbits = (sc_ref.shape[0] - 1).bit_length()

        def pos_bit(b, last):
            step = lax.shift_left(jnp.int32(1), jnp.int32(nbits - 1) - b)
            probe = last + (step - 1)
            got = count_keys(lambda s, kpos: (s == thr_f) & (kpos <= probe))
            return jnp.where(got < need, last + step, last)

        last = lax.fori_loop(0, nbits, pos_bit, jnp.zeros((1, tq), I32))

        def bias_chunk(j, carry):
            off = pl.multiple_of(j * tk, tk)
            sc = sc_ref[pl.ds(off, tk), :]
            keep = (sc > thr_f) | ((sc == thr_f) & ((row + j * tk) <= last))
            bias_ref[pl.ds(off, tk), :] = jnp.where(keep, 0.0, NEG)
            return carry

        lax.fori_loop(0, nch, bias_chunk, 0)

    def qk_chunk(j, mrun):
        off = pl.multiple_of(j * tk, tk)
        bias = bias_ref[pl.ds(off, tk), :]
        out = []
        for h in range(N_HEADS):
            g = h // GROUP
            kc = k_ref[pl.ds(off, tk), g * HEAD_DIM:(g + 1) * HEAD_DIM]
            s = jnp.dot(kc, qT_ref[h * HEAD_DIM:(h + 1) * HEAD_DIM, :], preferred_element_type=F32) + bias
            s_ref[h, pl.ds(off, tk), :] = s
            out.append(jnp.maximum(mrun[h], s.reshape(tk // 8, 8, tq).max(axis=0)))
        return tuple(out)

    mrun = lax.fori_loop(0, nch, qk_chunk, (jnp.full((8, tq), NEG, F32),) * N_HEADS)
    ms = [jnp.max(mr, axis=0, keepdims=True) for mr in mrun]
    acc_ref[...] = jnp.zeros(acc_ref.shape, F32)

    def pv_chunk(j, lrun):
        off = pl.multiple_of(j * tk, tk)
        out = []
        for h in range(N_HEADS):
            g = h // GROUP
            hs = slice(h * HEAD_DIM, (h + 1) * HEAD_DIM)
            p = jnp.exp2(s_ref[h, pl.ds(off, tk), :] - ms[h])
            out.append(lrun[h] + p.reshape(tk // 8, 8, tq).sum(axis=0))
            vc = vT_ref[g * HEAD_DIM:(g + 1) * HEAD_DIM, pl.ds(off, tk)]
            acc_ref[hs, :] += jnp.dot(vc, p.astype(BF16), preferred_element_type=F32)
        return tuple(out)

    lrun = lax.fori_loop(0, nch, pv_chunk, (jnp.zeros((8, tq), F32),) * N_HEADS)

    for h in range(N_HEADS):
        hs = slice(h * HEAD_DIM, (h + 1) * HEAD_DIM)
        o = (acc_ref[hs, :] / jnp.sum(lrun[h], axis=0, keepdims=True)).T
        o_ref[:, hs] = (o * gate_ref[:, hs]).astype(BF16)


def _attn_prompt(qiT, wT, kibf, qT, kbf, vT, gate, n_batch, seq):
    m = n_batch * seq
    nq = seq // ATT_TQ
    nkv = N_KV_HEADS * HEAD_DIM
    qmap = lambda b, i: (0, b * nq + i)
    return pl.pallas_call(
        _attn_prompt_body,
        grid=(n_batch, nq),
        in_specs=[
            pl.BlockSpec((N_IDX_HEADS * IDX_DIM, ATT_TQ), qmap),
            pl.BlockSpec((N_IDX_HEADS, ATT_TQ), qmap),
            pl.BlockSpec((seq, 128), lambda b, i: (b, 0)),
            pl.BlockSpec((D_ATTN, ATT_TQ), qmap),
            pl.BlockSpec((seq, nkv), lambda b, i: (b, 0)),
            pl.BlockSpec((nkv, seq), lambda b, i: (0, b)),
            pl.BlockSpec((ATT_TQ, D_ATTN), lambda b, i: (b * nq + i, 0)),
        ],
        out_specs=pl.BlockSpec((ATT_TQ, D_ATTN), lambda b, i: (b * nq + i, 0)),
        out_shape=jax.ShapeDtypeStruct((m, D_ATTN), BF16),
        scratch_shapes=[pltpu.VMEM((seq, ATT_TQ), F32), pltpu.VMEM((seq, ATT_TQ), F32),
                        pltpu.VMEM((D_ATTN, ATT_TQ), F32),
                        pltpu.VMEM((N_HEADS, seq, ATT_TQ), F32)],
        compiler_params=_cparams(2),
        name="attn_prompt",
    )(qiT, wT, kibf, qT, kbf, vT, gate)


def _s_score_body(pt_ref, qi_ref, wcol_ref, kinew_ref, *rest):
    npg = S_SCORE_PAGES
    pages = rest[:npg]
    out_ref, outnew_ref = rest[npg:]
    pg = pl.program_id(1)
    qi = qi_ref[0]
    wc = wcol_ref[0]

    def scores(keys_t):
        d = jnp.dot(qi, keys_t, preferred_element_type=F32)
        val = jnp.maximum(d, 0.0) * wc
        return val.reshape(N_IDX_HEADS, 8, val.shape[-1]).sum(axis=0)

    for r in range(npg):
        out_ref[0, :, r * PAGE_SIZE:(r + 1) * PAGE_SIZE] = scores(pages[r][...].astype(BF16))

    @pl.when(pg == 0)
    def _new():
        outnew_ref[0] = scores(kinew_ref[0])


def _s_thresh_body(n_new, sp_ref, sn_ref, bp_ref, bn_ref):
    rows, past = sp_ref.shape
    ch = 2048
    nchunk = past // ch
    t = lax.broadcasted_iota(I32, (rows, 128), 0) & 7
    lane = lax.broadcasted_iota(I32, (rows, 128), 1)
    sn = jnp.where((lane < n_new) & (lane <= t), sn_ref[...], -jnp.inf)

    def fold(x):
        f = x[:, 0:128]
        for q in range(1, x.shape[1] // 128):
            f = f + x[:, q * 128:(q + 1) * 128]
        return f

    lane_ch = lax.broadcasted_iota(I32, (rows, ch), 1)

    def count_keys(pred):
        def cnt_chunk(c, acc):
            off = pl.multiple_of(c * ch, ch)
            return acc + fold(jnp.where(pred(sp_ref[:, pl.ds(off, ch)], lane_ch + c * ch), 1.0, 0.0))

        acc = lax.fori_loop(0, nchunk, cnt_chunk, jnp.where(pred(sn, lane + past), 1.0, 0.0))
        return jnp.sum(acc, axis=1, keepdims=True)

    def bit_body(b, carry):
        thr, cge = carry
        cand = thr + _bit_value(b)
        cand_f = _key_to_float(cand)
        cnt = count_keys(lambda s, kpos: s >= cand_f)
        ok = cnt >= float(TOPK_MAX)
        return jnp.where(ok, cand, thr), jnp.where(ok, cnt, cge)

    thr, cge = lax.fori_loop(0, 32, bit_body, (jnp.full((rows, 1), INT_MIN, I32),
                                               jnp.zeros((rows, 1), F32)))
    thr_f = _key_to_float(jnp.maximum(thr, KEY_LOWEST_FINITE))
    real = (lax.broadcasted_iota(I32, (rows, 1), 0) & 7) < n_new
    has_ties = jnp.max(jnp.where(real, cge, 0.0)) > float(TOPK_MAX)

    @pl.when(jnp.logical_not(has_ties))
    def _plain_mask():
        def to_bias(c, carry):
            off = pl.multiple_of(c * ch, ch)
            bp_ref[:, pl.ds(off, ch)] = jnp.where(sp_ref[:, pl.ds(off, ch)] >= thr_f, 0.0, NEG)
            return carry

        lax.fori_loop(0, nchunk, to_bias, 0)
        bn_ref[...] = jnp.where(sn >= thr_f, 0.0, NEG)

    @pl.when(has_ties)
    def _tie_mask():
        need = float(TOPK_MAX) - count_keys(lambda s, kpos: s > thr_f)
        nbits = (past + 128 - 1).bit_length()

        def pos_bit(b, last):
            step = lax.shift_left(jnp.int32(1), jnp.int32(nbits - 1) - b)
            probe = last + (step - 1)
            got = count_keys(lambda s, kpos: (s == thr_f) & (kpos <= probe))
            return jnp.where(got < need, last + step, last)

        last = lax.fori_loop(0, nbits, pos_bit, jnp.zeros((rows, 1), I32))

        def keep(s, kpos):
            return (s > thr_f) | ((s == thr_f) & (kpos <= last))

        def to_bias(c, carry):
            off = pl.multiple_of(c * ch, ch)
            bp_ref[:, pl.ds(off, ch)] = jnp.where(keep(sp_ref[:, pl.ds(off, ch)], lane_ch + c * ch), 0.0, NEG)
            return carry

        lax.fori_loop(0, nchunk, to_bias, 0)
        bn_ref[...] = jnp.where(keep(sn, lane + past), 0.0, NEG)


def _s_attn_body(n_steps, pt_ref, q_ref, bp_ref, bn_ref, knew_ref, vnew_ref, *rest):
    npg = S_ATTN_PAGES
    kpages = rest[:npg]
    vpages = rest[npg:2 * npg]
    out_ref, s_ref, p_ref, l_ref, acc_ref = rest[2 * npg:]
    t = pl.program_id(1)
    nt = (((1,), (1,)), ((), ()))
    rg = GROUP * 8
    width = npg * PAGE_SIZE
    past = n_steps * width
    sm_chunk = 2048

    def head_rows(pages, g):
        rows = [pg_ref[pl.ds(g, PAGE_SIZE, stride=N_KV_HEADS), :] for pg_ref in pages]
        return jnp.concatenate(rows, axis=0).astype(BF16)

    @pl.when(t < n_steps)
    def _logits():
        off = pl.multiple_of(t * width, width)
        bias = jnp.concatenate([bp_ref[0]] * GROUP, axis=0)
        for g in range(N_KV_HEADS):
            qg = q_ref[0, g * rg:(g + 1) * rg, :]
            s_ref[g * rg:(g + 1) * rg, pl.ds(off, width)] = lax.dot_general(
                qg, head_rows(kpages, g), nt, preferred_element_type=F32) + bias

    @pl.when(t == n_steps - 1)
    def _softmax():
        biasn = jnp.concatenate([bn_ref[0]] * GROUP, axis=0)
        for g in range(N_KV_HEADS):
            qg = q_ref[0, g * rg:(g + 1) * rg, :]
            kg = knew_ref[0, :, g * HEAD_DIM:(g + 1) * HEAD_DIM]
            s_ref[g * rg:(g + 1) * rg, past:past + PAGE_SIZE] = lax.dot_general(
                qg, kg, nt, preferred_element_type=F32) + biasn

        def fold(x, op):
            f = x[:, 0:128]
            for q in range(1, x.shape[1] // 128):
                f = op(f, x[:, q * 128:(q + 1) * 128])
            return f

        def max_chunk(c, m):
            off = pl.multiple_of(c * sm_chunk, sm_chunk)
            return jnp.maximum(m, fold(s_ref[:, pl.ds(off, sm_chunk)], jnp.maximum))

        m = lax.fori_loop(0, past // sm_chunk, max_chunk, s_ref[:, past:past + PAGE_SIZE])
        m = jnp.max(m, axis=1, keepdims=True)

        def exp_chunk(c, l):
            off = pl.multiple_of(c * sm_chunk, sm_chunk)
            p = jnp.exp2(s_ref[:, pl.ds(off, sm_chunk)] - m)
            p_ref[:, pl.ds(off, sm_chunk)] = p.astype(BF16)
            return l + fold(p, jnp.add)

        pn = jnp.exp2(s_ref[:, past:past + PAGE_SIZE] - m)
        p_ref[:, past:past + PAGE_SIZE] = pn.astype(BF16)
        l_ref[...] = lax.fori_loop(0, past // sm_chunk, exp_chunk, pn)
        acc_ref[...] = jnp.zeros(acc_ref.shape, F32)

    @pl.when(t >= n_steps)
    def _values():
        off = pl.multiple_of((t - n_steps) * width, width)
        for g in range(N_KV_HEADS):
            rs = slice(g * rg, (g + 1) * rg)
            acc_ref[rs, :] += jnp.dot(p_ref[rs, pl.ds(off, width)], head_rows(vpages, g),
                                      preferred_element_type=F32)

    @pl.when(t == 2 * n_steps - 1)
    def _finish():
        for g in range(N_KV_HEADS):
            rs = slice(g * rg, (g + 1) * rg)
            vg = vnew_ref[0, :, g * HEAD_DIM:(g + 1) * HEAD_DIM]
            acc_ref[rs, :] += jnp.dot(p_ref[rs, past:past + PAGE_SIZE], vg, preferred_element_type=F32)
        out_ref[0] = acc_ref[...] / jnp.sum(l_ref[...], axis=1, keepdims=True)


def _attn_sample(qT, qiT, wT, kibf, kbf, v32, cache_k, cache_v, cache_kidx, page_table, n_seq, n_tok):
    n_pages = page_table.shape[1]
    past = n_pages * PAGE_SIZE
    npg = S_SCORE_PAGES
    n_steps = n_pages // npg
    n_pool = cache_k.shape[0]

    def rows_ht(xT, n_heads, dim):
        x = xT.reshape(n_heads, dim, n_seq, n_tok).transpose(2, 0, 3, 1)
        x = jnp.pad(x, ((0, 0), (0, 0), (0, 8 - n_tok), (0, 0)))
        return x.reshape(n_seq, n_heads * 8, dim)

    qi_rows = rows_ht(qiT, N_IDX_HEADS, IDX_DIM)
    q_rows = rows_ht(qT, N_HEADS, HEAD_DIM)
    wcol = jnp.pad(wT.reshape(N_IDX_HEADS, n_seq, n_tok).transpose(1, 0, 2),
                   ((0, 0), (0, 0), (0, 8 - n_tok))).reshape(n_seq, N_IDX_HEADS * 8, 1)

    def pad_keys(x):
        x = x.reshape(n_seq, n_tok, x.shape[-1])
        return jnp.pad(x, ((0, 0), (0, PAGE_SIZE - n_tok), (0, 0)))

    ki_new_t = jnp.swapaxes(pad_keys(kibf[:, 0:IDX_DIM]), 1, 2)
    kidx_t = jnp.swapaxes(cache_kidx, 1, 2)
    k_new = pad_keys(kbf)
    v_new = pad_keys(v32.reshape(n_seq * n_tok, N_KV).astype(BF16))

    def page_spec(shape, r):
        return pl.BlockSpec((None,) + shape, lambda b, pg, pt, r=r: (pt[b, pg * npg + r], 0, 0))

    def kv_page_spec(shape, r, n, is_v):
        def imap(b, t, pt):
            step = jnp.clip(t - n, 0, n - 1) if is_v else jnp.minimum(t, n - 1)
            return (pt[b, step * S_ATTN_PAGES + r], 0, 0)
        return pl.BlockSpec((None,) + shape, imap)

    sp, sn = pl.pallas_call(
        _s_score_body,
        grid_spec=pltpu.PrefetchScalarGridSpec(
            num_scalar_prefetch=1,
            grid=(n_seq, n_steps),
            in_specs=[
                pl.BlockSpec((1, N_IDX_HEADS * 8, IDX_DIM), lambda b, pg, pt: (b, 0, 0)),
                pl.BlockSpec((1, N_IDX_HEADS * 8, 1), lambda b, pg, pt: (b, 0, 0)),
                pl.BlockSpec((1, IDX_DIM, PAGE_SIZE), lambda b, pg, pt: (b, 0, 0)),
            ] + [page_spec((IDX_DIM, PAGE_SIZE), r) for r in range(npg)],
            out_specs=[
                pl.BlockSpec((1, 8, npg * PAGE_SIZE), lambda b, pg, pt: (b, 0, pg)),
                pl.BlockSpec((1, 8, PAGE_SIZE), lambda b, pg, pt: (b, 0, 0)),
            ],
        ),
        out_shape=[jax.ShapeDtypeStruct((n_seq, 8, past), F32),
                   jax.ShapeDtypeStruct((n_seq, 8, PAGE_SIZE), F32)],
        compiler_params=_cparams(2),
        name="sample_scores",
    )(page_table, qi_rows, wcol, ki_new_t, *([kidx_t] * npg))

    rows = n_seq * 8
    bp, bn = pl.pallas_call(
        functools.partial(_s_thresh_body, n_tok),
        grid=(1,),
        in_specs=[pl.BlockSpec((rows, past), lambda i: (0, 0)),
                  pl.BlockSpec((rows, PAGE_SIZE), lambda i: (0, 0))],
        out_specs=[pl.BlockSpec((rows, past), lambda i: (0, 0)),
                   pl.BlockSpec((rows, PAGE_SIZE), lambda i: (0, 0))],
        out_shape=[jax.ShapeDtypeStruct((rows, past), F32),
                   jax.ShapeDtypeStruct((rows, PAGE_SIZE), F32)],
        compiler_params=_cparams(1),
        name="sample_topk_mask",
    )(sp.reshape(rows, past), sn.reshape(rows, PAGE_SIZE))
    bp = bp.reshape(n_seq, 8, past)
    bn = bn.reshape(n_seq, 8, PAGE_SIZE)

    nrow = N_HEADS * 8
    kv_rows = PAGE_SIZE * N_KV_HEADS
    ck = cache_k.reshape(n_pool, kv_rows, HEAD_DIM)
    cv = cache_v.reshape(n_pool, kv_rows, HEAD_DIM)
    na = n_pages // S_ATTN_PAGES
    width = S_ATTN_PAGES * PAGE_SIZE
    out = pl.pallas_call(
        functools.partial(_s_attn_body, na),
        grid_spec=pltpu.PrefetchScalarGridSpec(
            num_scalar_prefetch=1,
            grid=(n_seq, 2 * na),
            in_specs=[
                pl.BlockSpec((1, nrow, HEAD_DIM), lambda b, t, pt: (b, 0, 0)),
                pl.BlockSpec((1, 8, width), lambda b, t, pt: (b, 0, jnp.minimum(t, na - 1))),
                pl.BlockSpec((1, 8, PAGE_SIZE), lambda b, t, pt: (b, 0, 0)),
                pl.BlockSpec((1, PAGE_SIZE, N_KV), lambda b, t, pt: (b, 0, 0)),
                pl.BlockSpec((1, PAGE_SIZE, N_KV), lambda b, t, pt: (b, 0, 0)),
            ] + [kv_page_spec((kv_rows, HEAD_DIM), r, na, False) for r in range(S_ATTN_PAGES)]
              + [kv_page_spec((kv_rows, HEAD_DIM), r, na, True) for r in range(S_ATTN_PAGES)],
            out_specs=pl.BlockSpec((1, nrow, HEAD_DIM), lambda b, t, pt: (b, 0, 0)),
            scratch_shapes=[pltpu.VMEM((nrow, past + PAGE_SIZE), F32),
                            pltpu.VMEM((nrow, past + PAGE_SIZE), BF16),
                            pltpu.VMEM((nrow, 128), F32), pltpu.VMEM((nrow, HEAD_DIM), F32)],
        ),
        out_shape=jax.ShapeDtypeStruct((n_seq, nrow, HEAD_DIM), F32),
        compiler_params=_cparams(2),
        name="sample_attn",
    )(page_table, q_rows, bp, bn, k_new, v_new, *([ck] * S_ATTN_PAGES), *([cv] * S_ATTN_PAGES))
    out = out.reshape(n_seq, N_HEADS, 8, HEAD_DIM)[:, :, 0:n_tok]
    return out.transpose(0, 2, 1, 3).reshape(n_seq * n_tok, D_ATTN)


def _outproj_body(gated, *refs):
    if gated:
        x_ref, a_ref, gate_ref, mc_ref, wo_ref, o_ref = refs
        ma = (a_ref[...] * gate_ref[...]).astype(BF16)
    else:
        x_ref, a_ref, mc_ref, wo_ref, o_ref = refs
        ma = a_ref[...]
    acc = jnp.dot(ma, wo_ref[0:D_ATTN, :], preferred_element_type=F32)
    acc = acc + jnp.dot(mc_ref[...], wo_ref[D_ATTN:D_ATTN + D_CONV, :], preferred_element_type=F32)
    o_ref[...] = x_ref[...] + acc


def _outproj(x, attn, gate, mixc, wo, tm):
    m = x.shape[0]
    gated = gate is not None
    row = lambda i: (i, 0)
    in_specs = [pl.BlockSpec((tm, D_MODEL), row), pl.BlockSpec((tm, D_ATTN), row)]
    args = [x, attn]
    if gated:
        in_specs.append(pl.BlockSpec((tm, D_ATTN), row))
        args.append(gate)
    in_specs += [pl.BlockSpec((tm, D_CONV), row),
                 pl.BlockSpec((D_ATTN + D_CONV, D_MODEL), lambda i: (0, 0))]
    args += [mixc, wo]
    return pl.pallas_call(
        functools.partial(_outproj_body, gated),
        grid=(m // tm,),
        in_specs=in_specs,
        out_specs=pl.BlockSpec((tm, D_MODEL), row),
        out_shape=jax.ShapeDtypeStruct((m, D_MODEL), F32),
        compiler_params=_cparams(1),
        name="outproj_sample" if gated else "outproj_prompt",
    )(*args)


def _rope_tables(pos):
    posf = pos.astype(F32)[:, None]
    n = pos.shape[0]

    def cs(half):
        inv = ROPE_THETA ** (-jnp.arange(half, dtype=F32) / half)
        ang = posf * inv[None, :]
        return jnp.cos(ang), jnp.sin(ang)

    c16, s16 = cs(ROPE_HALF)
    c8, s8 = cs(IDX_ROPE_HALF)
    one = lambda w: jnp.ones((n, w), F32)
    zero = lambda w: jnp.zeros((n, w), F32)
    rest = HEAD_DIM - ROPE_DIM
    k_c = jnp.concatenate([c16, c16, one(rest)], axis=1)
    k_sa = jnp.concatenate([-s16, zero(HEAD_DIM - ROPE_HALF)], axis=1)
    k_sb = jnp.concatenate([zero(ROPE_HALF), s16, zero(rest)], axis=1)
    return dict(c16T=c16.T, s16T=s16.T, c8T=c8.T, s8T=s8.T, kC=k_c, kSa=k_sa, kSb=k_sb)


def _prep_weights(w_in, g_q, g_k, g_kidx, w_conv, w_out):
    assert w_in.shape == (D_MODEL, D_IN)
    wT = w_in.T.astype(BF16)
    gq = g_q.reshape(HEAD_DIM, 1)
    gk = g_k.reshape(1, HEAD_DIM)
    gki = g_kidx.reshape(IDX_DIM, 1)
    return wT, (gq, gk, gki, w_conv), w_out.astype(BF16)


def kernel(x_prompt, x_sample, cache_k, cache_v, cache_kidx, state_conv, page_table,
           norm_in, w_in, g_q, g_k, g_kidx, w_conv, w_out):
    n_b, seq, _ = x_prompt.shape
    n_s, n_t, _ = x_sample.shape
    depth = w_in.shape[0]
    past = page_table.shape[1] * PAGE_SIZE
    tabs_p = _rope_tables(jnp.arange(seq))
    tabs_s = _rope_tables(jnp.tile(past + jnp.arange(n_t), n_s))

    hp = x_prompt.reshape(n_b * seq, D_MODEL)
    hs = x_sample.reshape(n_s * n_t, D_MODEL)
    outs = [[] for _ in range(8)]
    for l in range(depth):
        w_t, params, wo = _prep_weights(w_in[l], g_q[l], g_k[l], g_kidx[l], w_conv[l], w_out[l])

        xn = _rmsnorm(hp, norm_in[l], ROW_TM)
        (qT, qiT, vT, wT, k32, kbf, v32, ki32, kibf, gate, mixc, utail) = _project(
            xn, w_t, params, tabs_p, 0)
        mixa = _attn_prompt(qiT, wT, kibf, qT, kbf, vT, gate, n_b, seq)
        hp = _outproj(hp, mixa, None, mixc, wo, ROW_TM)
        tps = seq // PROJ_TM
        outs[0].append(k32.reshape(n_b, seq, N_KV_HEADS, HEAD_DIM))
        outs[1].append(v32.reshape(n_b, seq, N_KV_HEADS, HEAD_DIM))
        outs[2].append(ki32.reshape(n_b, seq, IDX_DIM))
        outs[3].append(utail[tps - 1::tps])

        st = state_conv[l]
        tok = jnp.arange(n_t)
        e1 = st[:, jnp.full((n_t,), CONV_W - 2)].reshape(n_s * n_t, D_CONV)
        e2 = st[:, jnp.minimum(tok, CONV_W - 2)].reshape(n_s * n_t, D_CONV)
        xn = _rmsnorm(hs, norm_in[l], n_s * n_t)
        (qT, qiT, vT, wT, k32, kbf, v32, ki32, kibf, gate, mixc, u) = _project(
            xn, w_t, params, tabs_s, n_t, state_rows=(e1, e2))
        attn = _attn_sample(qT, qiT, wT, kibf, kbf, v32, cache_k[l], cache_v[l], cache_kidx[l],
                            page_table, n_s, n_t)
        hs = _outproj(hs, attn, gate, mixc, wo, n_s * n_t)
        outs[4].append(k32.reshape(n_s, n_t, N_KV_HEADS, HEAD_DIM))
        outs[5].append(v32.reshape(n_s, n_t, N_KV_HEADS, HEAD_DIM))
        outs[6].append(ki32.reshape(n_s, n_t, IDX_DIM))
        outs[7].append(u.reshape(n_s, n_t, D_CONV)[:, n_t - (CONV_W - 1):])

    return (hp.reshape(n_b, seq, D_MODEL), hs.reshape(n_s, n_t, D_MODEL),
            *[jnp.stack(o) for o in outs])
```

```python
import functools

import jax
import jax.numpy as jnp
import numpy as np
from jax import lax
from jax.experimental import pallas as pl
from jax.experimental.pallas import tpu as pltpu

F32 = jnp.float32
BF16 = jnp.bfloat16
I32 = jnp.int32

D_MODEL = 2048
HEAD_DIM = 128
N_HEADS = 8
N_KV_HEADS = 2
GROUP = N_HEADS // N_KV_HEADS
D_ATTN = N_HEADS * HEAD_DIM
D_CONV = 1024
ROPE_DIM = HEAD_DIM // 4
ROPE_HALF = ROPE_DIM // 2
ROPE_THETA = 500000.0
N_IDX_HEADS = 16
IDX_DIM = 64
IDX_ROPE_HALF = IDX_DIM // 8
TOPK_MAX = 256
CONV_W = 3
PAGE_SIZE = 128
EPS = 1e-6
W_IDX_SCALE = (N_IDX_HEADS ** -0.5) * (IDX_DIM ** -0.5)
ATTN_SCALE = HEAD_DIM ** -0.5
Q_PRESCALE = ATTN_SCALE * 1.4426950408889634

INT_MIN = -(2 ** 31)
NEG = -1e30

VMEM_LIMIT_BYTES = 60 * 1024 * 1024

PROJ_TN = 512
PROJ_TM = 1024
PROJ_SUB = 256
ROW_TM = 512
N_KV = N_KV_HEADS * HEAD_DIM
OFF_Q = 0
OFF_K = OFF_Q + D_ATTN
OFF_V = OFF_K + N_KV
OFF_Z = OFF_V + N_KV
OFF_QI = OFF_Z + D_ATTN
OFF_KI = OFF_QI + N_IDX_HEADS * IDX_DIM
OFF_WI = OFF_KI + IDX_DIM
OFF_H = OFF_WI + N_IDX_HEADS
OFF_B = OFF_H + D_CONV
OFF_C = OFF_B + D_CONV
OFF_ZC = OFF_C + D_CONV
D_IN = OFF_ZC + D_CONV
assert OFF_K == 2 * PROJ_TN and OFF_Z == 3 * PROJ_TN and OFF_QI == 5 * PROJ_TN and OFF_KI == 7 * PROJ_TN
KIW_ROWS = 128
assert OFF_KI % KIW_ROWS == 0 and OFF_WI - OFF_KI == IDX_DIM
CONV_CHUNK = 256
N_CONV_CHUNKS = D_CONV // CONV_CHUNK
J_Q = 0
J_KV = 2
J_Z = 3
J_QI = 5
J_CONV = 7
N_MAIN_BLOCKS = J_CONV
N_STEPS = J_CONV + N_CONV_CHUNKS

ATT_TQ = 256
ATT_TK = 256
ATT_SUB = 64
CNT_ACCS = 4

S_SCORE_PAGES = 64
S_ATTN_PAGES = 64


def _cparams(n_axes):
    return pltpu.CompilerParams(
        dimension_semantics=("arbitrary",) * n_axes,
        vmem_limit_bytes=VMEM_LIMIT_BYTES,
    )


def _silu(x):
    return x * jax.nn.sigmoid(x)


def _proj_body(sample, tps, tm, *refs):
    (x_ref, gin_ref, wa_ref, wkiw_ref, wh_ref, wb_ref, wc_ref, wzc_ref, gq_ref, gk_ref, gki_ref,
     c16_ref, s16_ref, c8_ref, s8_ref,
     kc_ref, ksa_ref, ksb_ref, wconv_ref) = refs[:19]
    refs = refs[19:]
    if sample:
        e1_ref, e2_ref = refs[:2]
        refs = refs[2:]
    (qT_ref, qiT_ref, vT_ref, wT_ref, k32_ref, kbf_ref, v32_ref, ki32_ref, kibf_ref,
     gate_ref, mixc_ref, u_ref) = refs[:12]
    refs = refs[12:]
    xn_ref = refs[0]
    if not sample:
        carry_ref = refs[1]

    i = pl.program_id(0)
    j = pl.program_id(1)
    nt = (((1,), (1,)), ((), ()))

    def xw(w, x=None):
        return lax.dot_general(xn_ref[...] if x is None else x, w, nt, preferred_element_type=F32)

    def wx(w, x):
        return lax.dot_general(w, x, nt, preferred_element_type=F32)

    nsub = max(tm // PROJ_SUB, 1)
    sub = tm // nsub
    pieces = [slice(rb * sub, (rb + 1) * sub) for rb in range(nsub)]

    if not sample:
        @pl.when((i == 0) & (j == 0))
        def _init():
            carry_ref[...] = jnp.zeros(carry_ref.shape, F32)

    def q_step(with_norm):
        for cs in pieces:
            if with_norm:
                x = x_ref[cs, :]
                ms = jnp.mean(x * x, axis=-1, keepdims=True)
                xn_ref[cs, :] = (x * lax.rsqrt(ms + EPS) * gin_ref[...]).astype(BF16)
            res = wx(wa_ref[...], xn_ref[cs, :])
            cos = c16_ref[:, cs]
            sin = s16_ref[:, cs]
            for hh in range(PROJ_TN // HEAD_DIM):
                blk = res[hh * HEAD_DIM:(hh + 1) * HEAD_DIM]
                ms = jnp.mean(blk * blk, axis=0, keepdims=True)
                y = blk * lax.rsqrt(ms + EPS) * gq_ref[...]
                x1 = y[0:ROPE_HALF]
                x2 = y[ROPE_HALF:ROPE_DIM]
                base = hh * HEAD_DIM
                qT_ref[base:base + ROPE_HALF, cs] = ((x1 * cos - x2 * sin) * Q_PRESCALE).astype(BF16)
                qT_ref[base + ROPE_HALF:base + ROPE_DIM, cs] = ((x2 * cos + x1 * sin) * Q_PRESCALE).astype(BF16)
                qT_ref[base + ROPE_DIM:base + HEAD_DIM, cs] = (y[ROPE_DIM:] * Q_PRESCALE).astype(BF16)

    pl.when(j == J_Q)(functools.partial(q_step, True))
    pl.when((j > J_Q) & (j < J_KV))(functools.partial(q_step, False))

    @pl.when((j >= J_QI) & (j < J_CONV))
    def _qi():
        for cs in pieces:
            res = wx(wa_ref[...], xn_ref[cs, :])
            cos = c8_ref[:, cs]
            sin = s8_ref[:, cs]
            for hh in range(PROJ_TN // IDX_DIM):
                blk = res[hh * IDX_DIM:(hh + 1) * IDX_DIM]
                x1 = blk[0:IDX_ROPE_HALF]
                x2 = blk[IDX_ROPE_HALF:2 * IDX_ROPE_HALF]
                rot = jnp.concatenate([x1 * cos - x2 * sin, x2 * cos + x1 * sin], axis=0)
                base = hh * IDX_DIM
                qiT_ref[base:base + 2 * IDX_ROPE_HALF, cs] = rot.astype(BF16)
                qiT_ref[base + 2 * IDX_ROPE_HALF:base + IDX_DIM, cs] = blk[2 * IDX_ROPE_HALF:].astype(BF16)

    @pl.when(j == J_KV)
    def _kv():
        for rb, cs in enumerate(pieces):
            x = xn_ref[cs, :]
            res = xw(wa_ref[...], x)
            for hd in range(N_KV_HEADS):
                hs = slice(hd * HEAD_DIM, (hd + 1) * HEAD_DIM)
                blk = res[:, hs]
                ms = jnp.mean(blk * blk, axis=-1, keepdims=True)
                yk = blk * lax.rsqrt(ms + EPS) * gk_ref[...]
                rot = yk * kc_ref[cs, :] + (pltpu.roll(yk, HEAD_DIM - ROPE_HALF, axis=1) * ksa_ref[cs, :]
                                            + pltpu.roll(yk, ROPE_HALF, axis=1) * ksb_ref[cs, :])
                k32_ref[pl.ds(N_KV_HEADS * rb * sub + hd, sub, stride=N_KV_HEADS), :] = rot
                kbf_ref[cs, hs] = rot.astype(BF16)
            v = res[:, N_KV:2 * N_KV]
            for hd in range(N_KV_HEADS):
                v32_ref[pl.ds(N_KV_HEADS * rb * sub + hd, sub, stride=N_KV_HEADS), :] = (
                    v[:, hd * HEAD_DIM:(hd + 1) * HEAD_DIM])
            if sample:
                vT_ref[:, cs] = wx(wa_ref[N_KV:2 * N_KV, :], x).astype(BF16)
            else:
                vT_ref[:, cs] = v.T.astype(BF16)
            r2t = wx(wkiw_ref[...], x)
            wT_ref[:, cs] = r2t[IDX_DIM:IDX_DIM + N_IDX_HEADS] * W_IDX_SCALE
            kit = r2t[0:IDX_DIM]
            ms = jnp.mean(kit * kit, axis=0, keepdims=True)
            yi = kit * lax.rsqrt(ms + EPS) * gki_ref[...]
            x1 = yi[0:IDX_ROPE_HALF]
            x2 = yi[IDX_ROPE_HALF:2 * IDX_ROPE_HALF]
            cos = c8_ref[:, cs]
            sin = s8_ref[:, cs]
            roti = jnp.concatenate([x1 * cos - x2 * sin, x2 * cos + x1 * sin, yi[2 * IDX_ROPE_HALF:],
                                    jnp.zeros((KIW_ROWS - IDX_DIM, sub), F32)], axis=0)
            ki_nat = roti.T
            ki32_ref[cs, :] = ki_nat[:, 0:IDX_DIM]
            kibf_ref[cs, :] = ki_nat.astype(BF16)

    @pl.when((j >= J_Z) & (j < J_QI))
    def _z():
        for cs in pieces:
            gate_ref[cs, :] = _silu(xw(wa_ref[...], xn_ref[cs, :]))

    @pl.when(j >= J_CONV)
    def _conv():
        cc = j - J_CONV
        u = xw(wc_ref[...]) * xw(wh_ref[...])
        rowid = lax.broadcasted_iota(I32, (tm, CONV_CHUNK), 0)
        if sample:
            t = rowid & (sample - 1)
            u1 = jnp.where(t >= 1, pltpu.roll(u, 1, axis=0), e1_ref[...])
            u2 = jnp.where(t >= 2, pltpu.roll(u, 2, axis=0), e2_ref[...])
            u_ref[...] = u
        else:
            first = (i % tps) == 0
            prev = carry_ref[cc]
            p0 = jnp.where(first, 0.0, prev[0:1])
            p1 = jnp.where(first, 0.0, prev[1:2])
            u1 = jnp.where(rowid == 0, p1, pltpu.roll(u, 1, axis=0))
            u2 = jnp.where(rowid == 0, p0, jnp.where(rowid == 1, p1, pltpu.roll(u, 2, axis=0)))
            tail = u[tm - 8:tm]
            carry_ref[cc] = jnp.concatenate([tail[6:8], tail[0:6]], axis=0)
            u_ref[0] = tail[6:8]
        w = wconv_ref[...]
        y = u2 * w[0:1] + u1 * w[1:2] + u * w[2:3]
        mixc_ref[...] = (xw(wb_ref[...]) * y * _silu(xw(wzc_ref[...]))).astype(BF16)


def _project(x, norm_g, wT, params, tabs, sample, state_rows=None):
    m = x.shape[0]
    assert sample & (sample - 1) == 0
    tm = m if sample else PROJ_TM
    n_i = m // tm
    tps = 1 if sample else (tabs["c16T"].shape[1] // tm)
    gq, gk, gki, wconv = params

    def tmap(i):
        return i % tps

    def cchunk(j):
        return jnp.clip(j - J_CONV, 0, N_CONV_CHUNKS - 1)

    def conv_rows(off):
        return pl.BlockSpec((pl.Element(CONV_CHUNK), pl.Element(D_MODEL)),
                            lambda i, j: (pl.multiple_of(off + CONV_CHUNK * cchunk(j), 16), 0))

    in_specs = [
        pl.BlockSpec((tm, D_MODEL), lambda i, j: (i, 0)),
        pl.BlockSpec((1, D_MODEL), lambda i, j: (0, 0)),
        pl.BlockSpec((PROJ_TN, D_MODEL), lambda i, j: (jnp.minimum(j, N_MAIN_BLOCKS - 1), 0)),
        pl.BlockSpec((KIW_ROWS, D_MODEL), lambda i, j: (OFF_KI // KIW_ROWS, 0)),
        conv_rows(OFF_H), conv_rows(OFF_B), conv_rows(OFF_C), conv_rows(OFF_ZC),
        pl.BlockSpec((HEAD_DIM, 1), lambda i, j: (0, 0)),
        pl.BlockSpec((1, HEAD_DIM), lambda i, j: (0, 0)),
        pl.BlockSpec((IDX_DIM, 1), lambda i, j: (0, 0)),
        pl.BlockSpec((ROPE_HALF, tm), lambda i, j: (0, tmap(i))),
        pl.BlockSpec((ROPE_HALF, tm), lambda i, j: (0, tmap(i))),
        pl.BlockSpec((IDX_ROPE_HALF, tm), lambda i, j: (0, tmap(i))),
        pl.BlockSpec((IDX_ROPE_HALF, tm), lambda i, j: (0, tmap(i))),
        pl.BlockSpec((tm, HEAD_DIM), lambda i, j: (tmap(i), 0)),
        pl.BlockSpec((tm, HEAD_DIM), lambda i, j: (tmap(i), 0)),
        pl.BlockSpec((tm, HEAD_DIM), lambda i, j: (tmap(i), 0)),
        pl.BlockSpec((CONV_W, CONV_CHUNK), lambda i, j: (0, cchunk(j))),
    ]
    args = [x, norm_g.reshape(1, D_MODEL), wT, wT, wT, wT, wT, wT, gq, gk, gki,
            tabs["c16T"], tabs["s16T"], tabs["c8T"], tabs["s8T"],
            tabs["kC"], tabs["kSa"], tabs["kSb"], wconv]
    cmap = lambda i, j: (i, cchunk(j))
    if sample:
        in_specs += [pl.BlockSpec((tm, CONV_CHUNK), cmap), pl.BlockSpec((tm, CONV_CHUNK), cmap)]
        args += list(state_rows)
        u_spec = pl.BlockSpec((tm, CONV_CHUNK), cmap)
        u_shape = jax.ShapeDtypeStruct((m, D_CONV), F32)
        scratch = [pltpu.VMEM((tm, D_MODEL), BF16)]
    else:
        u_spec = pl.BlockSpec((1, CONV_W - 1, CONV_CHUNK), lambda i, j: (i, 0, cchunk(j)))
        u_shape = jax.ShapeDtypeStruct((n_i, CONV_W - 1, D_CONV), F32)
        scratch = [pltpu.VMEM((tm, D_MODEL), BF16), pltpu.VMEM((N_CONV_CHUNKS, 8, CONV_CHUNK), F32)]

    out_specs = [
        pl.BlockSpec((PROJ_TN, tm), lambda i, j: (jnp.minimum(j, 1), i)),
        pl.BlockSpec((PROJ_TN, tm), lambda i, j: (jnp.clip(j - J_QI, 0, 1), i)),
        pl.BlockSpec((N_KV, tm), lambda i, j: (0, i)),
        pl.BlockSpec((N_IDX_HEADS, tm), lambda i, j: (0, i)),
        pl.BlockSpec((N_KV_HEADS * tm, HEAD_DIM), lambda i, j: (i, 0)),
        pl.BlockSpec((tm, N_KV), lambda i, j: (i, 0)),
        pl.BlockSpec((N_KV_HEADS * tm, HEAD_DIM), lambda i, j: (i, 0)),
        pl.BlockSpec((tm, IDX_DIM), lambda i, j: (i, 0)),
        pl.BlockSpec((tm, KIW_ROWS), lambda i, j: (i, 0)),
        pl.BlockSpec((tm, PROJ_TN), lambda i, j: (i, jnp.clip(j - J_Z, 0, 1))),
        pl.BlockSpec((tm, CONV_CHUNK), cmap),
        u_spec,
    ]
    out_shape = [
        jax.ShapeDtypeStruct((D_ATTN, m), BF16),
        jax.ShapeDtypeStruct((N_IDX_HEADS * IDX_DIM, m), BF16),
        jax.ShapeDtypeStruct((N_KV, m), BF16),
        jax.ShapeDtypeStruct((N_IDX_HEADS, m), F32),
        jax.ShapeDtypeStruct((N_KV_HEADS * m, HEAD_DIM), F32),
        jax.ShapeDtypeStruct((m, N_KV), BF16),
        jax.ShapeDtypeStruct((N_KV_HEADS * m, HEAD_DIM), F32),
        jax.ShapeDtypeStruct((m, IDX_DIM), F32),
        jax.ShapeDtypeStruct((m, KIW_ROWS), BF16),
        jax.ShapeDtypeStruct((m, D_ATTN), F32),
        jax.ShapeDtypeStruct((m, D_CONV), BF16),
        u_shape,
    ]
    return pl.pallas_call(
        functools.partial(_proj_body, sample, tps, tm),
        grid=(n_i, N_STEPS),
        in_specs=in_specs,
        out_specs=out_specs,
        out_shape=out_shape,
        scratch_shapes=scratch,
        compiler_params=_cparams(2),
        name="proj_sample" if sample else "proj_prompt",
    )(*args)


KEY_LOWEST_FINITE = INT_MIN + 0x00800000


def _key_to_float(key):
    return pltpu.bitcast(key ^ ((key >> 31) & 0x7FFFFFFF), F32)


def _bit_value(b):
    return lax.shift_left(jnp.int32(1), jnp.int32(31) - b)


def _attn_prompt_body(qiT_ref, wT_ref, ki_ref, qT_ref, k_ref, vT_ref, gate_ref, o_ref,
                      sc_ref, bias_ref, acc_ref, s_ref):
    tq, tk = ATT_TQ, ATT_TK
    i = pl.program_id(1)
    nch = i + 1
    w = wT_ref[...]
    row = lax.broadcasted_iota(I32, (tk, tq), 0)
    col = lax.broadcasted_iota(I32, (tk, tq), 1)
    row8 = lax.broadcasted_iota(I32, (8, tq), 0)

    def score_chunk(j, carry):
        off = pl.multiple_of(j * tk, tk)
        kic = ki_ref[pl.ds(off, tk), 0:IDX_DIM]
        acc = jnp.zeros((tk, tq), F32)
        for h in range(N_IDX_HEADS):
            d = jnp.dot(kic, qiT_ref[h * IDX_DIM:(h + 1) * IDX_DIM, :], preferred_element_type=F32)
            acc = acc + w[h:h + 1, :] * jnp.maximum(d, 0.0)
        future = (row + j * tk) > (col + i * tq)
        sc_ref[pl.ds(off, tk), :] = jnp.where(future, -jnp.inf, acc)
        return carry

    lax.fori_loop(0, nch, score_chunk, 0)

    def count_keys(pred):
        def cnt_chunk(j, cs):
            off = pl.multiple_of(j * tk, tk)
            cs = list(cs)
            sc = sc_ref[pl.ds(off, tk), :]
            for r in range(tk // 8):
                a = cs[r % CNT_ACCS]
                kpos = row8 + (j * tk + r * 8)
                cs[r % CNT_ACCS] = jnp.where(pred(sc[r * 8:(r + 1) * 8], kpos), a + 1, a)
            return tuple(cs)

        cs = lax.fori_loop(0, nch, cnt_chunk, (jnp.zeros((8, tq), I32),) * CNT_ACCS)
        c = cs[0]
        for a in cs[1:]:
            c = c + a
        return jnp.sum(c.astype(F32), axis=0, keepdims=True)

    def bit_body(b, carry):
        thr, cge = carry
        cand = thr + _bit_value(b)
        cand_f = _key_to_float(cand)
        cnt = count_keys(lambda s, kpos: s >= cand_f)
        ok = cnt >= float(TOPK_MAX)
        return jnp.where(ok, cand, thr), jnp.where(ok, cnt, cge)

    thr, cge = lax.fori_loop(0, 32, bit_body, (jnp.full((1, tq), INT_MIN, I32),
                                               jnp.zeros((1, tq), F32)))
    thr_f = _key_to_float(jnp.maximum(thr, KEY_LOWEST_FINITE))
    has_ties = jnp.max(cge) > float(TOPK_MAX)

    @pl.when(jnp.logical_not(has_ties))
    def _plain_mask():
        def bias_chunk(j, carry):
            off = pl.multiple_of(j * tk, tk)
            bias_ref[pl.ds(off, tk), :] = jnp.where(sc_ref[pl.ds(off, tk), :] >= thr_f, 0.0, NEG)
            return carry

        lax.fori_loop(0, nch, bias_chunk, 0)

    @pl.when(has_ties)
    def _tie_mask():
        need = float(TOPK_MAX) - count_keys(lambda s, kpos: s > thr_f)
        nbits = (sc_ref.shape[0] - 1).bit_length()

        def pos_bit(b, last):
            step = lax.shift_left(jnp.int32(1), jnp.int32(nbits - 1) - b)
            probe = last + (step - 1)
            got = count_keys(lambda s, kpos: (s == thr_f) & (kpos <= probe))
            return jnp.where(got < need, last + step, last)

        last = lax.fori_loop(0, nbits, pos_bit, jnp.zeros((1, tq), I32))

        def bias_chunk(j, carry):
            off = pl.multiple_of(j * tk, tk)
            sc = sc_ref[pl.ds(off, tk), :]
            keep = (sc > thr_f) | ((sc == thr_f) & ((row + j * tk) <= last))
            bias_ref[pl.ds(off, tk), :] = jnp.where(keep, 0.0, NEG)
            return carry

        lax.fori_loop(0, nch, bias_chunk, 0)

    def qk_chunk(j, mrun):
        off = pl.multiple_of(j * tk, tk)
        bias = bias_ref[pl.ds(off, tk), :]
        out = []
        for h in range(N_HEADS):
            g = h // GROUP
            kc = k_ref[pl.ds(off, tk), g * HEAD_DIM:(g + 1) * HEAD_DIM]
            s = jnp.dot(kc, qT_ref[h * HEAD_DIM:(h + 1) * HEAD_DIM, :], preferred_element_type=F32) + bias
            s_ref[h, pl.ds(off, tk), :] = s
            out.append(jnp.maximum(mrun[h], s.reshape(tk // 8, 8, tq).max(axis=0)))
        return tuple(out)

    mrun = lax.fori_loop(0, nch, qk_chunk, (jnp.full((8, tq), NEG, F32),) * N_HEADS)
    ms = [jnp.max(mr, axis=0, keepdims=True) for mr in mrun]
    acc_ref[...] = jnp.zeros(acc_ref.shape, F32)

    def pv_chunk(j, lrun):
        off = pl.multiple_of(j * tk, tk)
        out = []
        for h in range(N_HEADS):
            g = h // GROUP
            hs = slice(h * HEAD_DIM, (h + 1) * HEAD_DIM)
            p = jnp.exp2(s_ref[h, pl.ds(off, tk), :] - ms[h])
            out.append(lrun[h] + p.reshape(tk // 8, 8, tq).sum(axis=0))
            vc = vT_ref[g * HEAD_DIM:(g + 1) * HEAD_DIM, pl.ds(off, tk)]
            acc_ref[hs, :] += jnp.dot(vc, p.astype(BF16), preferred_element_type=F32)
        return tuple(out)

    lrun = lax.fori_loop(0, nch, pv_chunk, (jnp.zeros((8, tq), F32),) * N_HEADS)

    for h in range(N_HEADS):
        hs = slice(h * HEAD_DIM, (h + 1) * HEAD_DIM)
        o = (acc_ref[hs, :] / jnp.sum(lrun[h], axis=0, keepdims=True)).T
        o_ref[:, hs] = (o * gate_ref[:, hs]).astype(BF16)


def _attn_prompt(qiT, wT, kibf, qT, kbf, vT, gate, n_batch, seq):
    m = n_batch * seq
    nq = seq // ATT_TQ
    nkv = N_KV_HEADS * HEAD_DIM
    qmap = lambda b, i: (0, b * nq + i)
    return pl.pallas_call(
        _attn_prompt_body,
        grid=(n_batch, nq),
        in_specs=[
            pl.BlockSpec((N_IDX_HEADS * IDX_DIM, ATT_TQ), qmap),
            pl.BlockSpec((N_IDX_HEADS, ATT_TQ), qmap),
            pl.BlockSpec((seq, 128), lambda b, i: (b, 0)),
            pl.BlockSpec((D_ATTN, ATT_TQ), qmap),
            pl.BlockSpec((seq, nkv), lambda b, i: (b, 0)),
            pl.BlockSpec((nkv, seq), lambda b, i: (0, b)),
            pl.BlockSpec((ATT_TQ, D_ATTN), lambda b, i: (b * nq + i, 0)),
        ],
        out_specs=pl.BlockSpec((ATT_TQ, D_ATTN), lambda b, i: (b * nq + i, 0)),
        out_shape=jax.ShapeDtypeStruct((m, D_ATTN), BF16),
        scratch_shapes=[pltpu.VMEM((seq, ATT_TQ), F32), pltpu.VMEM((seq, ATT_TQ), F32),
                        pltpu.VMEM((D_ATTN, ATT_TQ), F32),
                        pltpu.VMEM((N_HEADS, seq, ATT_TQ), F32)],
        compiler_params=_cparams(2),
        name="attn_prompt",
    )(qiT, wT, kibf, qT, kbf, vT, gate)


def _s_score_body(pt_ref, qi_ref, wcol_ref, kinew_ref, *rest):
    npg = S_SCORE_PAGES
    pages = rest[:npg]
    out_ref, outnew_ref = rest[npg:]
    pg = pl.program_id(1)
    qi = qi_ref[0]
    wc = wcol_ref[0]

    def scores(keys_t):
        d = jnp.dot(qi, keys_t, preferred_element_type=F32)
        val = jnp.maximum(d, 0.0) * wc
        return val.reshape(N_IDX_HEADS, 8, val.shape[-1]).sum(axis=0)

    for r in range(npg):
        out_ref[0, :, r * PAGE_SIZE:(r + 1) * PAGE_SIZE] = scores(pages[r][...].astype(BF16))

    @pl.when(pg == 0)
    def _new():
        outnew_ref[0] = scores(kinew_ref[0])


def _s_thresh_body(n_new, sp_ref, sn_ref, bp_ref, bn_ref):
    rows, past = sp_ref.shape
    ch = 2048
    nchunk = past // ch
    t = lax.broadcasted_iota(I32, (rows, 128), 0) & 7
    lane = lax.broadcasted_iota(I32, (rows, 128), 1)
    sn = jnp.where((lane < n_new) & (lane <= t), sn_ref[...], -jnp.inf)

    def fold(x):
        f = x[:, 0:128]
        for q in range(1, x.shape[1] // 128):
            f = f + x[:, q * 128:(q + 1) * 128]
        return f

    lane_ch = lax.broadcasted_iota(I32, (rows, ch), 1)

    def count_keys(pred):
        def cnt_chunk(c, acc):
            off = pl.multiple_of(c * ch, ch)
            return acc + fold(jnp.where(pred(sp_ref[:, pl.ds(off, ch)], lane_ch + c * ch), 1.0, 0.0))

        acc = lax.fori_loop(0, nchunk, cnt_chunk, jnp.where(pred(sn, lane + past), 1.0, 0.0))
        return jnp.sum(acc, axis=1, keepdims=True)

    def bit_body(b, carry):
        thr, cge = carry
        cand = thr + _bit_value(b)
        cand_f = _key_to_float(cand)
        cnt = count_keys(lambda s, kpos: s >= cand_f)
        ok = cnt >= float(TOPK_MAX)
        return jnp.where(ok, cand, thr), jnp.where(ok, cnt, cge)

    thr, cge = lax.fori_loop(0, 32, bit_body, (jnp.full((rows, 1), INT_MIN, I32),
                                               jnp.zeros((rows, 1), F32)))
    thr_f = _key_to_float(jnp.maximum(thr, KEY_LOWEST_FINITE))
    real = (lax.broadcasted_iota(I32, (rows, 1), 0) & 7) < n_new
    has_ties = jnp.max(jnp.where(real, cge, 0.0)) > float(TOPK_MAX)

    @pl.when(jnp.logical_not(has_ties))
    def _plain_mask():
        def to_bias(c, carry):
            off = pl.multiple_of(c * ch, ch)
            bp_ref[:, pl.ds(off, ch)] = jnp.where(sp_ref[:, pl.ds(off, ch)] >= thr_f, 0.0, NEG)
            return carry

        lax.fori_loop(0, nchunk, to_bias, 0)
        bn_ref[...] = jnp.where(sn >= thr_f, 0.0, NEG)

    @pl.when(has_ties)
    def _tie_mask():
        need = float(TOPK_MAX) - count_keys(lambda s, kpos: s > thr_f)
        nbits = (past + 128 - 1).bit_length()

        def pos_bit(b, last):
            step = lax.shift_left(jnp.int32(1), jnp.int32(nbits - 1) - b)
            probe = last + (step - 1)
            got = count_keys(lambda s, kpos: (s == thr_f) & (kpos <= probe))
            return jnp.where(got < need, last + step, last)

        last = lax.fori_loop(0, nbits, pos_bit, jnp.zeros((rows, 1), I32))

        def keep(s, kpos):
            return (s > thr_f) | ((s == thr_f) & (kpos <= last))

        def to_bias(c, carry):
            off = pl.multiple_of(c * ch, ch)
            bp_ref[:, pl.ds(off, ch)] = jnp.where(keep(sp_ref[:, pl.ds(off, ch)], lane_ch + c * ch), 0.0, NEG)
            return carry

        lax.fori_loop(0, nchunk, to_bias, 0)
        bn_ref[...] = jnp.where(keep(sn, lane + past), 0.0, NEG)


def _s_attn_body(n_steps, pt_ref, q_ref, bp_ref, bn_ref, knew_ref, vnew_ref, *rest):
    npg = S_ATTN_PAGES
    kpages = rest[:npg]
    vpages = rest[npg:2 * npg]
    out_ref, s_ref, p_ref, l_ref, acc_ref = rest[2 * npg:]
    t = pl.program_id(1)
    nt = (((1,), (1,)), ((), ()))
    rg = GROUP * 8
    width = npg * PAGE_SIZE
    past = n_steps * width
    sm_chunk = 2048

    def head_rows(pages, g):
        rows = [pg_ref[pl.ds(g, PAGE_SIZE, stride=N_KV_HEADS), :] for pg_ref in pages]
        return jnp.concatenate(rows, axis=0).astype(BF16)

    @pl.when(t < n_steps)
    def _logits():
        off = pl.multiple_of(t * width, width)
        bias = jnp.concatenate([bp_ref[0]] * GROUP, axis=0)
        for g in range(N_KV_HEADS):
            qg = q_ref[0, g * rg:(g + 1) * rg, :]
            s_ref[g * rg:(g + 1) * rg, pl.ds(off, width)] = lax.dot_general(
                qg, head_rows(kpages, g), nt, preferred_element_type=F32) + bias

    @pl.when(t == n_steps - 1)
    def _softmax():
        biasn = jnp.concatenate([bn_ref[0]] * GROUP, axis=0)
        for g in range(N_KV_HEADS):
            qg = q_ref[0, g * rg:(g + 1) * rg, :]
            kg = knew_ref[0, :, g * HEAD_DIM:(g + 1) * HEAD_DIM]
            s_ref[g * rg:(g + 1) * rg, past:past + PAGE_SIZE] = lax.dot_general(
                qg, kg, nt, preferred_element_type=F32) + biasn

        def fold(x, op):
            f = x[:, 0:128]
            for q in range(1, x.shape[1] // 128):
                f = op(f, x[:, q * 128:(q + 1) * 128])
            return f

        def max_chunk(c, m):
            off = pl.multiple_of(c * sm_chunk, sm_chunk)
            return jnp.maximum(m, fold(s_ref[:, pl.ds(off, sm_chunk)], jnp.maximum))

        m = lax.fori_loop(0, past // sm_chunk, max_chunk, s_ref[:, past:past + PAGE_SIZE])
        m = jnp.max(m, axis=1, keepdims=True)

        def exp_chunk(c, l):
            off = pl.multiple_of(c * sm_chunk, sm_chunk)
            p = jnp.exp2(s_ref[:, pl.ds(off, sm_chunk)] - m)
            p_ref[:, pl.ds(off, sm_chunk)] = p.astype(BF16)
            return l + fold(p, jnp.add)

        pn = jnp.exp2(s_ref[:, past:past + PAGE_SIZE] - m)
        p_ref[:, past:past + PAGE_SIZE] = pn.astype(BF16)
        l_ref[...] = lax.fori_loop(0, past // sm_chunk, exp_chunk, pn)
        acc_ref[...] = jnp.zeros(acc_ref.shape, F32)

    @pl.when(t >= n_steps)
    def _values():
        off = pl.multiple_of((t - n_steps) * width, width)
        for g in range(N_KV_HEADS):
            rs = slice(g * rg, (g + 1) * rg)
            acc_ref[rs, :] += jnp.dot(p_ref[rs, pl.ds(off, width)], head_rows(vpages, g),
                                      preferred_element_type=F32)

    @pl.when(t == 2 * n_steps - 1)
    def _finish():
        for g in range(N_KV_HEADS):
            rs = slice(g * rg, (g + 1) * rg)
            vg = vnew_ref[0, :, g * HEAD_DIM:(g + 1) * HEAD_DIM]
            acc_ref[rs, :] += jnp.dot(p_ref[rs, past:past + PAGE_SIZE], vg, preferred_element_type=F32)
        out_ref[0] = acc_ref[...] / jnp.sum(l_ref[...], axis=1, keepdims=True)


def _attn_sample(qT, qiT, wT, kibf, kbf, v32, cache_k, cache_v, cache_kidx, page_table, n_seq, n_tok):
    n_pages = page_table.shape[1]
    past = n_pages * PAGE_SIZE
    npg = S_SCORE_PAGES
    n_steps = n_pages // npg
    n_pool = cache_k.shape[0]

    def rows_ht(xT, n_heads, dim):
        x = xT.reshape(n_heads, dim, n_seq, n_tok).transpose(2, 0, 3, 1)
        x = jnp.pad(x, ((0, 0), (0, 0), (0, 8 - n_tok), (0, 0)))
        return x.reshape(n_seq, n_heads * 8, dim)

    qi_rows = rows_ht(qiT, N_IDX_HEADS, IDX_DIM)
    q_rows = rows_ht(qT, N_HEADS, HEAD_DIM)
    wcol = jnp.pad(wT.reshape(N_IDX_HEADS, n_seq, n_tok).transpose(1, 0, 2),
                   ((0, 0), (0, 0), (0, 8 - n_tok))).reshape(n_seq, N_IDX_HEADS * 8, 1)

    def pad_keys(x):
        x = x.reshape(n_seq, n_tok, x.shape[-1])
        return jnp.pad(x, ((0, 0), (0, PAGE_SIZE - n_tok), (0, 0)))

    ki_new_t = jnp.swapaxes(pad_keys(kibf[:, 0:IDX_DIM]), 1, 2)
    kidx_t = jnp.swapaxes(cache_kidx, 1, 2)
    k_new = pad_keys(kbf)
    v_new = pad_keys(v32.reshape(n_seq * n_tok, N_KV).astype(BF16))

    def page_spec(shape, r):
        return pl.BlockSpec((None,) + shape, lambda b, pg, pt, r=r: (pt[b, pg * npg + r], 0, 0))

    def kv_page_spec(shape, r, n, is_v):
        def imap(b, t, pt):
            step = jnp.clip(t - n, 0, n - 1) if is_v else jnp.minimum(t, n - 1)
            return (pt[b, step * S_ATTN_PAGES + r], 0, 0)
        return pl.BlockSpec((None,) + shape, imap)

    sp, sn = pl.pallas_call(
        _s_score_body,
        grid_spec=pltpu.PrefetchScalarGridSpec(
            num_scalar_prefetch=1,
            grid=(n_seq, n_steps),
            in_specs=[
                pl.BlockSpec((1, N_IDX_HEADS * 8, IDX_DIM), lambda b, pg, pt: (b, 0, 0)),
                pl.BlockSpec((1, N_IDX_HEADS * 8, 1), lambda b, pg, pt: (b, 0, 0)),
                pl.BlockSpec((1, IDX_DIM, PAGE_SIZE), lambda b, pg, pt: (b, 0, 0)),
            ] + [page_spec((IDX_DIM, PAGE_SIZE), r) for r in range(npg)],
            out_specs=[
                pl.BlockSpec((1, 8, npg * PAGE_SIZE), lambda b, pg, pt: (b, 0, pg)),
                pl.BlockSpec((1, 8, PAGE_SIZE), lambda b, pg, pt: (b, 0, 0)),
            ],
        ),
        out_shape=[jax.ShapeDtypeStruct((n_seq, 8, past), F32),
                   jax.ShapeDtypeStruct((n_seq, 8, PAGE_SIZE), F32)],
        compiler_params=_cparams(2),
        name="sample_scores",
    )(page_table, qi_rows, wcol, ki_new_t, *([kidx_t] * npg))

    rows = n_seq * 8
    bp, bn = pl.pallas_call(
        functools.partial(_s_thresh_body, n_tok),
        grid=(1,),
        in_specs=[pl.BlockSpec((rows, past), lambda i: (0, 0)),
                  pl.BlockSpec((rows, PAGE_SIZE), lambda i: (0, 0))],
        out_specs=[pl.BlockSpec((rows, past), lambda i: (0, 0)),
                   pl.BlockSpec((rows, PAGE_SIZE), lambda i: (0, 0))],
        out_shape=[jax.ShapeDtypeStruct((rows, past), F32),
                   jax.ShapeDtypeStruct((rows, PAGE_SIZE), F32)],
        compiler_params=_cparams(1),
        name="sample_topk_mask",
    )(sp.reshape(rows, past), sn.reshape(rows, PAGE_SIZE))
    bp = bp.reshape(n_seq, 8, past)
    bn = bn.reshape(n_seq, 8, PAGE_SIZE)

    nrow = N_HEADS * 8
    kv_rows = PAGE_SIZE * N_KV_HEADS
    ck = cache_k.reshape(n_pool, kv_rows, HEAD_DIM)
    cv = cache_v.reshape(n_pool, kv_rows, HEAD_DIM)
    na = n_pages // S_ATTN_PAGES
    width = S_ATTN_PAGES * PAGE_SIZE
    out = pl.pallas_call(
        functools.partial(_s_attn_body, na),
        grid_spec=pltpu.PrefetchScalarGridSpec(
            num_scalar_prefetch=1,
            grid=(n_seq, 2 * na),
            in_specs=[
                pl.BlockSpec((1, nrow, HEAD_DIM), lambda b, t, pt: (b, 0, 0)),
                pl.BlockSpec((1, 8, width), lambda b, t, pt: (b, 0, jnp.minimum(t, na - 1))),
                pl.BlockSpec((1, 8, PAGE_SIZE), lambda b, t, pt: (b, 0, 0)),
                pl.BlockSpec((1, PAGE_SIZE, N_KV), lambda b, t, pt: (b, 0, 0)),
                pl.BlockSpec((1, PAGE_SIZE, N_KV), lambda b, t, pt: (b, 0, 0)),
            ] + [kv_page_spec((kv_rows, HEAD_DIM), r, na, False) for r in range(S_ATTN_PAGES)]
              + [kv_page_spec((kv_rows, HEAD_DIM), r, na, True) for r in range(S_ATTN_PAGES)],
            out_specs=pl.BlockSpec((1, nrow, HEAD_DIM), lambda b, t, pt: (b, 0, 0)),
            scratch_shapes=[pltpu.VMEM((nrow, past + PAGE_SIZE), F32),
                            pltpu.VMEM((nrow, past + PAGE_SIZE), BF16),
                            pltpu.VMEM((nrow, 128), F32), pltpu.VMEM((nrow, HEAD_DIM), F32)],
        ),
        out_shape=jax.ShapeDtypeStruct((n_seq, nrow, HEAD_DIM), F32),
        compiler_params=_cparams(2),
        name="sample_attn",
    )(page_table, q_rows, bp, bn, k_new, v_new, *([ck] * S_ATTN_PAGES), *([cv] * S_ATTN_PAGES))
    out = out.reshape(n_seq, N_HEADS, 8, HEAD_DIM)[:, :, 0:n_tok]
    return out.transpose(0, 2, 1, 3).reshape(n_seq * n_tok, D_ATTN)


def _outproj_body(gated, *refs):
    if gated:
        x_ref, a_ref, gate_ref, mc_ref, wo_ref, o_ref = refs
        ma = (a_ref[...] * gate_ref[...]).astype(BF16)
    else:
        x_ref, a_ref, mc_ref, wo_ref, o_ref = refs
        ma = a_ref[...]
    acc = jnp.dot(ma, wo_ref[0:D_ATTN, :], preferred_element_type=F32)
    acc = acc + jnp.dot(mc_ref[...], wo_ref[D_ATTN:D_ATTN + D_CONV, :], preferred_element_type=F32)
    o_ref[...] = x_ref[...] + acc


def _outproj(x, attn, gate, mixc, wo, tm):
    m = x.shape[0]
    gated = gate is not None
    row = lambda i: (i, 0)
    in_specs = [pl.BlockSpec((tm, D_MODEL), row), pl.BlockSpec((tm, D_ATTN), row)]
    args = [x, attn]
    if gated:
        in_specs.append(pl.BlockSpec((tm, D_ATTN), row))
        args.append(gate)
    in_specs += [pl.BlockSpec((tm, D_CONV), row),
                 pl.BlockSpec((D_ATTN + D_CONV, D_MODEL), lambda i: (0, 0))]
    args += [mixc, wo]
    return pl.pallas_call(
        functools.partial(_outproj_body, gated),
        grid=(m // tm,),
        in_specs=in_specs,
        out_specs=pl.BlockSpec((tm, D_MODEL), row),
        out_shape=jax.ShapeDtypeStruct((m, D_MODEL), F32),
        compiler_params=_cparams(1),
        name="outproj_sample" if gated else "outproj_prompt",
    )(*args)


def _rope_tables(pos):
    posf = pos.astype(F32)[:, None]
    n = pos.shape[0]

    def cs(half):
        inv = ROPE_THETA ** (-jnp.arange(half, dtype=F32) / half)
        ang = posf * inv[None, :]
        return jnp.cos(ang), jnp.sin(ang)

    c16, s16 = cs(ROPE_HALF)
    c8, s8 = cs(IDX_ROPE_HALF)
    one = lambda w: jnp.ones((n, w), F32)
    zero = lambda w: jnp.zeros((n, w), F32)
    rest = HEAD_DIM - ROPE_DIM
    k_c = jnp.concatenate([c16, c16, one(rest)], axis=1)
    k_sa = jnp.concatenate([-s16, zero(HEAD_DIM - ROPE_HALF)], axis=1)
    k_sb = jnp.concatenate([zero(ROPE_HALF), s16, zero(rest)], axis=1)
    return dict(c16T=c16.T, s16T=s16.T, c8T=c8.T, s8T=s8.T, kC=k_c, kSa=k_sa, kSb=k_sb)


def _prep_weights(w_in, g_q, g_k, g_kidx, w_conv, w_out):
    assert w_in.shape == (D_MODEL, D_IN)
    wT = w_in.T.astype(BF16)
    gq = g_q.reshape(HEAD_DIM, 1)
    gk = g_k.reshape(1, HEAD_DIM)
    gki = g_kidx.reshape(IDX_DIM, 1)
    return wT, (gq, gk, gki, w_conv), w_out.astype(BF16)


def kernel(x_prompt, x_sample, cache_k, cache_v, cache_kidx, state_conv, page_table,
           norm_in, w_in, g_q, g_k, g_kidx, w_conv, w_out):
    n_b, seq, _ = x_prompt.shape
    n_s, n_t, _ = x_sample.shape
    depth = w_in.shape[0]
    past = page_table.shape[1] * PAGE_SIZE
    tabs_p = _rope_tables(jnp.arange(seq))
    tabs_s = _rope_tables(jnp.tile(past + jnp.arange(n_t), n_s))

    hp = x_prompt.reshape(n_b * seq, D_MODEL)
    hs = x_sample.reshape(n_s * n_t, D_MODEL)
    outs = [[] for _ in range(8)]
    for l in range(depth):
        w_t, params, wo = _prep_weights(w_in[l], g_q[l], g_k[l], g_kidx[l], w_conv[l], w_out[l])

        (qT, qiT, vT, wT, k32, kbf, v32, ki32, kibf, gate, mixc, utail) = _project(
            hp, norm_in[l], w_t, params, tabs_p, 0)
        mixa = _attn_prompt(qiT, wT, kibf, qT, kbf, vT, gate, n_b, seq)
        hp = _outproj(hp, mixa, None, mixc, wo, ROW_TM)
        tps = seq // PROJ_TM
        outs[0].append(k32.reshape(n_b, seq, N_KV_HEADS, HEAD_DIM))
        outs[1].append(v32.reshape(n_b, seq, N_KV_HEADS, HEAD_DIM))
        outs[2].append(ki32.reshape(n_b, seq, IDX_DIM))
        outs[3].append(utail[tps - 1::tps])

        st = state_conv[l]
        tok = jnp.arange(n_t)
        e1 = st[:, jnp.full((n_t,), CONV_W - 2)].reshape(n_s * n_t, D_CONV)
        e2 = st[:, jnp.minimum(tok, CONV_W - 2)].reshape(n_s * n_t, D_CONV)
        (qT, qiT, vT, wT, k32, kbf, v32, ki32, kibf, gate, mixc, u) = _project(
            hs, norm_in[l], w_t, params, tabs_s, n_t, state_rows=(e1, e2))
        attn = _attn_sample(qT, qiT, wT, kibf, kbf, v32, cache_k[l], cache_v[l], cache_kidx[l],
                            page_table, n_s, n_t)
        hs = _outproj(hs, attn, gate, mixc, wo, n_s * n_t)
        outs[4].append(k32.reshape(n_s, n_t, N_KV_HEADS, HEAD_DIM))
        outs[5].append(v32.reshape(n_s, n_t, N_KV_HEADS, HEAD_DIM))
        outs[6].append(ki32.reshape(n_s, n_t, IDX_DIM))
        outs[7].append(u.reshape(n_s, n_t, D_CONV)[:, n_t - (CONV_W - 1):])

    return (hp.reshape(n_b, seq, D_MODEL), hs.reshape(n_s, n_t, D_MODEL),
            *[jnp.stack(o) for o in outs])
```

```python
import functools

import jax
import jax.numpy as jnp
import numpy as np
from jax import lax
from jax.experimental import pallas as pl
from jax.experimental.pallas import tpu as pltpu

F32 = jnp.float32
BF16 = jnp.bfloat16
I32 = jnp.int32

D_MODEL = 2048
HEAD_DIM = 128
N_HEADS = 8
N_KV_HEADS = 2
GROUP = N_HEADS // N_KV_HEADS
D_ATTN = N_HEADS * HEAD_DIM
D_CONV = 1024
ROPE_DIM = HEAD_DIM // 4
ROPE_HALF = ROPE_DIM // 2
ROPE_THETA = 500000.0
N_IDX_HEADS = 16
IDX_DIM = 64
IDX_ROPE_HALF = IDX_DIM // 8
TOPK_MAX = 256
CONV_W = 3
PAGE_SIZE = 128
EPS = 1e-6
W_IDX_SCALE = (N_IDX_HEADS ** -0.5) * (IDX_DIM ** -0.5)
ATTN_SCALE = HEAD_DIM ** -0.5
Q_PRESCALE = ATTN_SCALE * 1.4426950408889634

INT_MIN = -(2 ** 31)
NEG = -1e30

VMEM_LIMIT_BYTES = 60 * 1024 * 1024

PROJ_TN = 512
PROJ_TM = 1024
PROJ_SUB = 256
ROW_TM = 512
N_KV = N_KV_HEADS * HEAD_DIM
OFF_Q = 0
OFF_K = OFF_Q + D_ATTN
OFF_V = OFF_K + N_KV
OFF_Z = OFF_V + N_KV
OFF_QI = OFF_Z + D_ATTN
OFF_KI = OFF_QI + N_IDX_HEADS * IDX_DIM
OFF_WI = OFF_KI + IDX_DIM
OFF_H = OFF_WI + N_IDX_HEADS
OFF_B = OFF_H + D_CONV
OFF_C = OFF_B + D_CONV
OFF_ZC = OFF_C + D_CONV
D_IN = OFF_ZC + D_CONV
assert OFF_K == 2 * PROJ_TN and OFF_Z == 3 * PROJ_TN and OFF_QI == 5 * PROJ_TN and OFF_KI == 7 * PROJ_TN
KIW_ROWS = 128
assert OFF_KI % KIW_ROWS == 0 and OFF_WI - OFF_KI == IDX_DIM
CONV_CHUNK = 256
N_CONV_CHUNKS = D_CONV // CONV_CHUNK
J_Q = 0
J_KV = 2
J_Z = 3
J_QI = 5
J_CONV = 7
N_MAIN_BLOCKS = J_CONV
N_STEPS = J_CONV + N_CONV_CHUNKS

ATT_TQ = 256
ATT_TK = 256
ATT_SUB = 64
CNT_ACCS = 4

S_SCORE_PAGES = 64


def _cparams(n_axes):
    return pltpu.CompilerParams(
        dimension_semantics=("arbitrary",) * n_axes,
        vmem_limit_bytes=VMEM_LIMIT_BYTES,
    )


def _silu(x):
    return x * jax.nn.sigmoid(x)


def _proj_body(sample, tps, tm, *refs):
    (x_ref, gin_ref, wa_ref, wkiw_ref, wh_ref, wb_ref, wc_ref, wzc_ref, gq_ref, gk_ref, gki_ref,
     c16_ref, s16_ref, c8_ref, s8_ref,
     kc_ref, ksa_ref, ksb_ref, wconv_ref) = refs[:19]
    refs = refs[19:]
    if sample:
        e1_ref, e2_ref = refs[:2]
        refs = refs[2:]
    (qT_ref, qiT_ref, vT_ref, wT_ref, k32_ref, kbf_ref, v32_ref, ki32_ref, kibf_ref,
     gate_ref, mixc_ref, u_ref) = refs[:12]
    refs = refs[12:]
    xn_ref = refs[0]
    if not sample:
        carry_ref = refs[1]

    i = pl.program_id(0)
    j = pl.program_id(1)
    nt = (((1,), (1,)), ((), ()))

    def xw(w, x=None):
        return lax.dot_general(xn_ref[...] if x is None else x, w, nt, preferred_element_type=F32)

    def wx(w, x):
        return lax.dot_general(w, x, nt, preferred_element_type=F32)

    nsub = max(tm // PROJ_SUB, 1)
    sub = tm // nsub
    pieces = [slice(rb * sub, (rb + 1) * sub) for rb in range(nsub)]

    if not sample:
        @pl.when((i == 0) & (j == 0))
        def _init():
            carry_ref[...] = jnp.zeros(carry_ref.shape, F32)

    def q_step(with_norm):
        for cs in pieces:
            if with_norm:
                x = x_ref[cs, :]
                ms = jnp.mean(x * x, axis=-1, keepdims=True)
                xn_ref[cs, :] = (x * lax.rsqrt(ms + EPS) * gin_ref[...]).astype(BF16)
            res = wx(wa_ref[...], xn_ref[cs, :])
            cos = c16_ref[:, cs]
            sin = s16_ref[:, cs]
            for hh in range(PROJ_TN // HEAD_DIM):
                blk = res[hh * HEAD_DIM:(hh + 1) * HEAD_DIM]
                ms = jnp.mean(blk * blk, axis=0, keepdims=True)
                y = blk * lax.rsqrt(ms + EPS) * gq_ref[...]
                x1 = y[0:ROPE_HALF]
                x2 = y[ROPE_HALF:ROPE_DIM]
                base = hh * HEAD_DIM
                qT_ref[base:base + ROPE_HALF, cs] = ((x1 * cos - x2 * sin) * Q_PRESCALE).astype(BF16)
                qT_ref[base + ROPE_HALF:base + ROPE_DIM, cs] = ((x2 * cos + x1 * sin) * Q_PRESCALE).astype(BF16)
                qT_ref[base + ROPE_DIM:base + HEAD_DIM, cs] = (y[ROPE_DIM:] * Q_PRESCALE).astype(BF16)

    pl.when(j == J_Q)(functools.partial(q_step, True))
    pl.when((j > J_Q) & (j < J_KV))(functools.partial(q_step, False))

    @pl.when((j >= J_QI) & (j < J_CONV))
    def _qi():
        for cs in pieces:
            res = wx(wa_ref[...], xn_ref[cs, :])
            cos = c8_ref[:, cs]
            sin = s8_ref[:, cs]
            for hh in range(PROJ_TN // IDX_DIM):
                blk = res[hh * IDX_DIM:(hh + 1) * IDX_DIM]
                x1 = blk[0:IDX_ROPE_HALF]
                x2 = blk[IDX_ROPE_HALF:2 * IDX_ROPE_HALF]
                rot = jnp.concatenate([x1 * cos - x2 * sin, x2 * cos + x1 * sin], axis=0)
                base = hh * IDX_DIM
                qiT_ref[base:base + 2 * IDX_ROPE_HALF, cs] = rot.astype(BF16)
                qiT_ref[base + 2 * IDX_ROPE_HALF:base + IDX_DIM, cs] = blk[2 * IDX_ROPE_HALF:].astype(BF16)

    @pl.when(j == J_KV)
    def _kv():
        for rb, cs in enumerate(pieces):
            x = xn_ref[cs, :]
            res = xw(wa_ref[...], x)
            for hd in range(N_KV_HEADS):
                hs = slice(hd * HEAD_DIM, (hd + 1) * HEAD_DIM)
                blk = res[:, hs]
                ms = jnp.mean(blk * blk, axis=-1, keepdims=True)
                yk = blk * lax.rsqrt(ms + EPS) * gk_ref[...]
                rot = yk * kc_ref[cs, :] + (pltpu.roll(yk, HEAD_DIM - ROPE_HALF, axis=1) * ksa_ref[cs, :]
                                            + pltpu.roll(yk, ROPE_HALF, axis=1) * ksb_ref[cs, :])
                k32_ref[pl.ds(N_KV_HEADS * rb * sub + hd, sub, stride=N_KV_HEADS), :] = rot
                kbf_ref[cs, hs] = rot.astype(BF16)
            v = res[:, N_KV:2 * N_KV]
            for hd in range(N_KV_HEADS):
                v32_ref[pl.ds(N_KV_HEADS * rb * sub + hd, sub, stride=N_KV_HEADS), :] = (
                    v[:, hd * HEAD_DIM:(hd + 1) * HEAD_DIM])
            if sample:
                vT_ref[:, cs] = wx(wa_ref[N_KV:2 * N_KV, :], x).astype(BF16)
            else:
                vT_ref[:, cs] = v.T.astype(BF16)
            r2t = wx(wkiw_ref[...], x)
            wT_ref[:, cs] = r2t[IDX_DIM:IDX_DIM + N_IDX_HEADS] * W_IDX_SCALE
            kit = r2t[0:IDX_DIM]
            ms = jnp.mean(kit * kit, axis=0, keepdims=True)
            yi = kit * lax.rsqrt(ms + EPS) * gki_ref[...]
            x1 = yi[0:IDX_ROPE_HALF]
            x2 = yi[IDX_ROPE_HALF:2 * IDX_ROPE_HALF]
            cos = c8_ref[:, cs]
            sin = s8_ref[:, cs]
            roti = jnp.concatenate([x1 * cos - x2 * sin, x2 * cos + x1 * sin, yi[2 * IDX_ROPE_HALF:],
                                    jnp.zeros((KIW_ROWS - IDX_DIM, sub), F32)], axis=0)
            ki_nat = roti.T
            ki32_ref[cs, :] = ki_nat[:, 0:IDX_DIM]
            kibf_ref[cs, :] = ki_nat.astype(BF16)

    @pl.when((j >= J_Z) & (j < J_QI))
    def _z():
        for cs in pieces:
            gate_ref[cs, :] = _silu(xw(wa_ref[...], xn_ref[cs, :]))

    @pl.when(j >= J_CONV)
    def _conv():
        cc = j - J_CONV
        u = xw(wc_ref[...]) * xw(wh_ref[...])
        rowid = lax.broadcasted_iota(I32, (tm, CONV_CHUNK), 0)
        if sample:
            t = rowid & (sample - 1)
            u1 = jnp.where(t >= 1, pltpu.roll(u, 1, axis=0), e1_ref[...])
            u2 = jnp.where(t >= 2, pltpu.roll(u, 2, axis=0), e2_ref[...])
            u_ref[...] = u
        else:
            first = (i % tps) == 0
            prev = carry_ref[cc]
            p0 = jnp.where(first, 0.0, prev[0:1])
            p1 = jnp.where(first, 0.0, prev[1:2])
            u1 = jnp.where(rowid == 0, p1, pltpu.roll(u, 1, axis=0))
            u2 = jnp.where(rowid == 0, p0, jnp.where(rowid == 1, p1, pltpu.roll(u, 2, axis=0)))
            tail = u[tm - 8:tm]
            carry_ref[cc] = jnp.concatenate([tail[6:8], tail[0:6]], axis=0)
            u_ref[0] = tail[6:8]
        w = wconv_ref[...]
        y = u2 * w[0:1] + u1 * w[1:2] + u * w[2:3]
        mixc_ref[...] = (xw(wb_ref[...]) * y * _silu(xw(wzc_ref[...]))).astype(BF16)


def _project(x, norm_g, wT, params, tabs, sample, state_rows=None):
    m = x.shape[0]
    assert sample & (sample - 1) == 0
    tm = m if sample else PROJ_TM
    n_i = m // tm
    tps = 1 if sample else (tabs["c16T"].shape[1] // tm)
    gq, gk, gki, wconv = params

    def tmap(i):
        return i % tps

    def cchunk(j):
        return jnp.clip(j - J_CONV, 0, N_CONV_CHUNKS - 1)

    def conv_rows(off):
        return pl.BlockSpec((pl.Element(CONV_CHUNK), pl.Element(D_MODEL)),
                            lambda i, j: (pl.multiple_of(off + CONV_CHUNK * cchunk(j), 16), 0))

    in_specs = [
        pl.BlockSpec((tm, D_MODEL), lambda i, j: (i, 0)),
        pl.BlockSpec((1, D_MODEL), lambda i, j: (0, 0)),
        pl.BlockSpec((PROJ_TN, D_MODEL), lambda i, j: (jnp.minimum(j, N_MAIN_BLOCKS - 1), 0)),
        pl.BlockSpec((KIW_ROWS, D_MODEL), lambda i, j: (OFF_KI // KIW_ROWS, 0)),
        conv_rows(OFF_H), conv_rows(OFF_B), conv_rows(OFF_C), conv_rows(OFF_ZC),
        pl.BlockSpec((HEAD_DIM, 1), lambda i, j: (0, 0)),
        pl.BlockSpec((1, HEAD_DIM), lambda i, j: (0, 0)),
        pl.BlockSpec((IDX_DIM, 1), lambda i, j: (0, 0)),
        pl.BlockSpec((ROPE_HALF, tm), lambda i, j: (0, tmap(i))),
        pl.BlockSpec((ROPE_HALF, tm), lambda i, j: (0, tmap(i))),
        pl.BlockSpec((IDX_ROPE_HALF, tm), lambda i, j: (0, tmap(i))),
        pl.BlockSpec((IDX_ROPE_HALF, tm), lambda i, j: (0, tmap(i))),
        pl.BlockSpec((tm, HEAD_DIM), lambda i, j: (tmap(i), 0)),
        pl.BlockSpec((tm, HEAD_DIM), lambda i, j: (tmap(i), 0)),
        pl.BlockSpec((tm, HEAD_DIM), lambda i, j: (tmap(i), 0)),
        pl.BlockSpec((CONV_W, CONV_CHUNK), lambda i, j: (0, cchunk(j))),
    ]
    args = [x, norm_g.reshape(1, D_MODEL), wT, wT, wT, wT, wT, wT, gq, gk, gki,
            tabs["c16T"], tabs["s16T"], tabs["c8T"], tabs["s8T"],
            tabs["kC"], tabs["kSa"], tabs["kSb"], wconv]
    cmap = lambda i, j: (i, cchunk(j))
    if sample:
        in_specs += [pl.BlockSpec((tm, CONV_CHUNK), cmap), pl.BlockSpec((tm, CONV_CHUNK), cmap)]
        args += list(state_rows)
        u_spec = pl.BlockSpec((tm, CONV_CHUNK), cmap)
        u_shape = jax.ShapeDtypeStruct((m, D_CONV), F32)
        scratch = [pltpu.VMEM((tm, D_MODEL), BF16)]
    else:
        u_spec = pl.BlockSpec((1, CONV_W - 1, CONV_CHUNK), lambda i, j: (i, 0, cchunk(j)))
        u_shape = jax.ShapeDtypeStruct((n_i, CONV_W - 1, D_CONV), F32)
        scratch = [pltpu.VMEM((tm, D_MODEL), BF16), pltpu.VMEM((N_CONV_CHUNKS, 8, CONV_CHUNK), F32)]

    out_specs = [
        pl.BlockSpec((PROJ_TN, tm), lambda i, j: (jnp.minimum(j, 1), i)),
        pl.BlockSpec((PROJ_TN, tm), lambda i, j: (jnp.clip(j - J_QI, 0, 1), i)),
        pl.BlockSpec((N_KV, tm), lambda i, j: (0, i)),
        pl.BlockSpec((N_IDX_HEADS, tm), lambda i, j: (0, i)),
        pl.BlockSpec((N_KV_HEADS * tm, HEAD_DIM), lambda i, j: (i, 0)),
        pl.BlockSpec((tm, N_KV), lambda i, j: (i, 0)),
        pl.BlockSpec((N_KV_HEADS * tm, HEAD_DIM), lambda i, j: (i, 0)),
        pl.BlockSpec((tm, IDX_DIM), lambda i, j: (i, 0)),
        pl.BlockSpec((tm, KIW_ROWS), lambda i, j: (i, 0)),
        pl.BlockSpec((tm, PROJ_TN), lambda i, j: (i, jnp.clip(j - J_Z, 0, 1))),
        pl.BlockSpec((tm, CONV_CHUNK), cmap),
        u_spec,
    ]
    out_shape = [
        jax.ShapeDtypeStruct((D_ATTN, m), BF16),
        jax.ShapeDtypeStruct((N_IDX_HEADS * IDX_DIM, m), BF16),
        jax.ShapeDtypeStruct((N_KV, m), BF16),
        jax.ShapeDtypeStruct((N_IDX_HEADS, m), F32),
        jax.ShapeDtypeStruct((N_KV_HEADS * m, HEAD_DIM), F32),
        jax.ShapeDtypeStruct((m, N_KV), BF16),
        jax.ShapeDtypeStruct((N_KV_HEADS * m, HEAD_DIM), F32),
        jax.ShapeDtypeStruct((m, IDX_DIM), F32),
        jax.ShapeDtypeStruct((m, KIW_ROWS), BF16),
        jax.ShapeDtypeStruct((m, D_ATTN), F32),
        jax.ShapeDtypeStruct((m, D_CONV), BF16),
        u_shape,
    ]
    return pl.pallas_call(
        functools.partial(_proj_body, sample, tps, tm),
        grid=(n_i, N_STEPS),
        in_specs=in_specs,
        out_specs=out_specs,
        out_shape=out_shape,
        scratch_shapes=scratch,
        compiler_params=_cparams(2),
        name="proj_sample" if sample else "proj_prompt",
    )(*args)


KEY_LOWEST_FINITE = INT_MIN + 0x00800000


def _key_to_float(key):
    return pltpu.bitcast(key ^ ((key >> 31) & 0x7FFFFFFF), F32)


def _bit_value(b):
    return lax.shift_left(jnp.int32(1), jnp.int32(31) - b)


def _prompt_attn_step(i, qiT_ref, wT_ref, ki_ref, qT_ref, k_ref, vT_ref, gate_ref, o_ref,
                      sc_ref, bias_ref, acc_ref, s_ref):
    tq, tk = ATT_TQ, ATT_TK
    nch = i + 1
    w = wT_ref[...]
    row = lax.broadcasted_iota(I32, (tk, tq), 0)
    col = lax.broadcasted_iota(I32, (tk, tq), 1)
    row8 = lax.broadcasted_iota(I32, (8, tq), 0)

    def score_chunk(j, carry):
        off = pl.multiple_of(j * tk, tk)
        kic = ki_ref[pl.ds(off, tk), 0:IDX_DIM]
        acc = jnp.zeros((tk, tq), F32)
        for h in range(N_IDX_HEADS):
            d = jnp.dot(kic, qiT_ref[h * IDX_DIM:(h + 1) * IDX_DIM, :], preferred_element_type=F32)
            acc = acc + w[h:h + 1, :] * jnp.maximum(d, 0.0)
        future = (row + j * tk) > (col + i * tq)
        sc_ref[pl.ds(off, tk), :] = jnp.where(future, -jnp.inf, acc)
        return carry

    lax.fori_loop(0, nch, score_chunk, 0)

    def count_keys(pred):
        def cnt_chunk(j, cs):
            off = pl.multiple_of(j * tk, tk)
            cs = list(cs)
            sc = sc_ref[pl.ds(off, tk), :]
            for r in range(tk // 8):
                a = cs[r % CNT_ACCS]
                kpos = row8 + (j * tk + r * 8)
                cs[r % CNT_ACCS] = jnp.where(pred(sc[r * 8:(r + 1) * 8], kpos), a + 1, a)
            return tuple(cs)

        cs = lax.fori_loop(0, nch, cnt_chunk, (jnp.zeros((8, tq), I32),) * CNT_ACCS)
        c = cs[0]
        for a in cs[1:]:
            c = c + a
        return jnp.sum(c.astype(F32), axis=0, keepdims=True)

    def bit_body(b, carry):
        thr, cge = carry
        cand = thr + _bit_value(b)
        cand_f = _key_to_float(cand)
        cnt = count_keys(lambda s, kpos: s >= cand_f)
        ok = cnt >= float(TOPK_MAX)
        return jnp.where(ok, cand, thr), jnp.where(ok, cnt, cge)

    thr, cge = lax.fori_loop(0, 32, bit_body, (jnp.full((1, tq), INT_MIN, I32),
                                               jnp.zeros((1, tq), F32)))
    thr_f = _key_to_float(jnp.maximum(thr, KEY_LOWEST_FINITE))
    has_ties = jnp.max(cge) > float(TOPK_MAX)

    @pl.when(jnp.logical_not(has_ties))
    def _plain_mask():
        def bias_chunk(j, carry):
            off = pl.multiple_of(j * tk, tk)
            bias_ref[pl.ds(off, tk), :] = jnp.where(sc_ref[pl.ds(off, tk), :] >= thr_f, 0.0, NEG)
            return carry

        lax.fori_loop(0, nch, bias_chunk, 0)

    @pl.when(has_ties)
    def _tie_mask():
        need = float(TOPK_MAX) - count_keys(lambda s, kpos: s > thr_f)
        nbits = (sc_ref.shape[0] - 1).bit_length()

        def pos_bit(b, last):
            step = lax.shift_left(jnp.int32(1), jnp.int32(nbits - 1) - b)
            probe = last + (step - 1)
            got = count_keys(lambda s, kpos: (s == thr_f) & (kpos <= probe))
            return jnp.where(got < need, last + step, last)

        last = lax.fori_loop(0, nbits, pos_bit, jnp.zeros((1, tq), I32))

        def bias_chunk(j, carry):
            off = pl.multiple_of(j * tk, tk)
            sc = sc_ref[pl.ds(off, tk), :]
            keep = (sc > thr_f) | ((sc == thr_f) & ((row + j * tk) <= last))
            bias_ref[pl.ds(off, tk), :] = jnp.where(keep, 0.0, NEG)
            return carry

        lax.fori_loop(0, nch, bias_chunk, 0)

    def qk_chunk(j, mrun):
        off = pl.multiple_of(j * tk, tk)
        bias = bias_ref[pl.ds(off, tk), :]
        out = []
        for h in range(N_HEADS):
            g = h // GROUP
            kc = k_ref[pl.ds(off, tk), g * HEAD_DIM:(g + 1) * HEAD_DIM]
            s = jnp.dot(kc, qT_ref[h * HEAD_DIM:(h + 1) * HEAD_DIM, :], preferred_element_type=F32) + bias
            s_ref[h, pl.ds(off, tk), :] = s
            out.append(jnp.maximum(mrun[h], s.reshape(tk // 8, 8, tq).max(axis=0)))
        return tuple(out)

    mrun = lax.fori_loop(0, nch, qk_chunk, (jnp.full((8, tq), NEG, F32),) * N_HEADS)
    ms = [jnp.max(mr, axis=0, keepdims=True) for mr in mrun]
    acc_ref[...] = jnp.zeros(acc_ref.shape, F32)

    def pv_chunk(j, lrun):
        off = pl.multiple_of(j * tk, tk)
        out = []
        for h in range(N_HEADS):
            g = h // GROUP
            hs = slice(h * HEAD_DIM, (h + 1) * HEAD_DIM)
            p = jnp.exp2(s_ref[h, pl.ds(off, tk), :] - ms[h])
            out.append(lrun[h] + p.reshape(tk // 8, 8, tq).sum(axis=0))
            vc = vT_ref[g * HEAD_DIM:(g + 1) * HEAD_DIM, pl.ds(off, tk)]
            acc_ref[hs, :] += jnp.dot(vc, p.astype(BF16), preferred_element_type=F32)
        return tuple(out)

    lrun = lax.fori_loop(0, nch, pv_chunk, (jnp.zeros((8, tq), F32),) * N_HEADS)

    for h in range(N_HEADS):
        hs = slice(h * HEAD_DIM, (h + 1) * HEAD_DIM)
        o = (acc_ref[hs, :] / jnp.sum(lrun[h], axis=0, keepdims=True)).T
        o_ref[:, hs] = (o * gate_ref[:, hs]).astype(BF16)


def _page_copy(cache_hbm, page, pbuf, slot, r, sem):
    return pltpu.make_async_copy(cache_hbm.at[page], pbuf.at[slot, r], sem.at[slot])


def _attn_fused_body(nq, spq, pt_ref, *refs):
    prompt_in = refs[0:7]
    q_ref, bp_ref, bn_ref, knew_ref, vnew_ref, ck_hbm, cv_hbm = refs[7:14]
    o_ref, so_ref = refs[14:16]
    prompt_scratch = refs[16:20]
    pbuf, sem, ss_ref, ps_ref, ls_ref, accs_ref = refs[20:26]

    i = pl.program_id(1)
    step = pl.program_id(0) * nq + i
    n_total = pl.num_programs(0) * nq
    half = spq // 2
    npg = pbuf.shape[1]

    def start_pages(n):
        seq_n = n // spq
        t_n = n % spq
        base = (t_n % half) * npg
        slot = n % 2
        for cache_hbm, pred in ((ck_hbm, t_n < half), (cv_hbm, t_n >= half)):
            @pl.when(pred)
            def _(cache_hbm=cache_hbm):
                for r in range(npg):
                    _page_copy(cache_hbm, pt_ref[seq_n, base + r], pbuf, slot, r, sem).start()

    @pl.when(step == 0)
    def _prologue():
        start_pages(step)

    @pl.when(step + 1 < n_total)
    def _prefetch():
        start_pages(step + 1)

    slot = step % 2
    for r in range(npg):
        _page_copy(ck_hbm, 0, pbuf, slot, r, sem).wait()

    _sample_attn_step(step % spq, half, slot, q_ref, bp_ref, bn_ref, knew_ref, vnew_ref, pbuf,
                      so_ref, ss_ref, ps_ref, ls_ref, accs_ref)
    _prompt_attn_step(i, *prompt_in, o_ref, *prompt_scratch)


def _attention(qiT, wT, kibf, qT, kbf, vT, gate, n_batch, seq,
               q_rows, bp, bn, k_new, v_new, cache_k, cache_v, page_table):
    m = n_batch * seq
    nq = seq // ATT_TQ
    n_seq, n_pages = page_table.shape
    n_steps = n_batch * nq
    spq = n_steps // n_seq
    assert spq * n_seq == n_steps and spq % 2 == 0
    half = spq // 2
    npg = n_pages // half
    assert npg * half == n_pages
    width = npg * PAGE_SIZE
    past = n_pages * PAGE_SIZE
    n_pool = cache_k.shape[0]
    kv_rows = PAGE_SIZE * N_KV_HEADS
    ck = cache_k.reshape(n_pool, kv_rows, HEAD_DIM)
    cv = cache_v.reshape(n_pool, kv_rows, HEAD_DIM)
    nrow = N_HEADS * 8

    qmap = lambda b, i, pt: (0, b * nq + i)
    smap = lambda b, i, pt: ((b * nq + i) // spq, 0, 0)
    return pl.pallas_call(
        functools.partial(_attn_fused_body, nq, spq),
        grid_spec=pltpu.PrefetchScalarGridSpec(
            num_scalar_prefetch=1,
            grid=(n_batch, nq),
            in_specs=[
                pl.BlockSpec((N_IDX_HEADS * IDX_DIM, ATT_TQ), qmap),
                pl.BlockSpec((N_IDX_HEADS, ATT_TQ), qmap),
                pl.BlockSpec((seq, 128), lambda b, i, pt: (b, 0)),
                pl.BlockSpec((D_ATTN, ATT_TQ), qmap),
                pl.BlockSpec((seq, N_KV), lambda b, i, pt: (b, 0)),
                pl.BlockSpec((N_KV, seq), lambda b, i, pt: (0, b)),
                pl.BlockSpec((ATT_TQ, D_ATTN), lambda b, i, pt: (b * nq + i, 0)),
                pl.BlockSpec((1, nrow, HEAD_DIM), smap),
                pl.BlockSpec((1, 8, width),
                             lambda b, i, pt: ((b * nq + i) // spq, 0, jnp.minimum((b * nq + i) % spq, half - 1))),
                pl.BlockSpec((1, 8, PAGE_SIZE), smap),
                pl.BlockSpec((1, PAGE_SIZE, N_KV), smap),
                pl.BlockSpec((1, PAGE_SIZE, N_KV), smap),
                pl.BlockSpec(memory_space=pl.ANY),
                pl.BlockSpec(memory_space=pl.ANY),
            ],
            out_specs=[
                pl.BlockSpec((ATT_TQ, D_ATTN), lambda b, i, pt: (b * nq + i, 0)),
                pl.BlockSpec((1, nrow, HEAD_DIM), smap),
            ],
            scratch_shapes=[
                pltpu.VMEM((seq, ATT_TQ), F32), pltpu.VMEM((seq, ATT_TQ), F32),
                pltpu.VMEM((D_ATTN, ATT_TQ), F32), pltpu.VMEM((N_HEADS, seq, ATT_TQ), F32),
                pltpu.VMEM((2, npg, kv_rows, HEAD_DIM), F32), pltpu.SemaphoreType.DMA((2,)),
                pltpu.VMEM((nrow, past + PAGE_SIZE), F32), pltpu.VMEM((nrow, past + PAGE_SIZE), BF16),
                pltpu.VMEM((nrow, 128), F32), pltpu.VMEM((nrow, HEAD_DIM), F32),
            ],
        ),
        out_shape=[jax.ShapeDtypeStruct((m, D_ATTN), BF16),
                   jax.ShapeDtypeStruct((n_seq, nrow, HEAD_DIM), F32)],
        compiler_params=_cparams(2),
        name="attention",
    )(page_table, qiT, wT, kibf, qT, kbf, vT, gate, q_rows, bp, bn, k_new, v_new, ck, cv)


def _s_score_body(pt_ref, qi_ref, wcol_ref, kinew_ref, *rest):
    npg = S_SCORE_PAGES
    pages = rest[:npg]
    out_ref, outnew_ref = rest[npg:]
    pg = pl.program_id(1)
    qi = qi_ref[0]
    wc = wcol_ref[0]

    def scores(keys_t):
        d = jnp.dot(qi, keys_t, preferred_element_type=F32)
        val = jnp.maximum(d, 0.0) * wc
        return val.reshape(N_IDX_HEADS, 8, val.shape[-1]).sum(axis=0)

    for r in range(npg):
        out_ref[0, :, r * PAGE_SIZE:(r + 1) * PAGE_SIZE] = scores(pages[r][...].astype(BF16))

    @pl.when(pg == 0)
    def _new():
        outnew_ref[0] = scores(kinew_ref[0])


def _s_thresh_body(n_new, sp_ref, sn_ref, bp_ref, bn_ref):
    rows, past = sp_ref.shape
    ch = 2048
    nchunk = past // ch
    t = lax.broadcasted_iota(I32, (rows, 128), 0) & 7
    lane = lax.broadcasted_iota(I32, (rows, 128), 1)
    sn = jnp.where((lane < n_new) & (lane <= t), sn_ref[...], -jnp.inf)

    def fold(x):
        f = x[:, 0:128]
        for q in range(1, x.shape[1] // 128):
            f = f + x[:, q * 128:(q + 1) * 128]
        return f

    lane_ch = lax.broadcasted_iota(I32, (rows, ch), 1)

    def count_keys(pred):
        def cnt_chunk(c, acc):
            off = pl.multiple_of(c * ch, ch)
            return acc + fold(jnp.where(pred(sp_ref[:, pl.ds(off, ch)], lane_ch + c * ch), 1.0, 0.0))

        acc = lax.fori_loop(0, nchunk, cnt_chunk, jnp.where(pred(sn, lane + past), 1.0, 0.0))
        return jnp.sum(acc, axis=1, keepdims=True)

    def bit_body(b, carry):
        thr, cge = carry
        cand = thr + _bit_value(b)
        cand_f = _key_to_float(cand)
        cnt = count_keys(lambda s, kpos: s >= cand_f)
        ok = cnt >= float(TOPK_MAX)
        return jnp.where(ok, cand, thr), jnp.where(ok, cnt, cge)

    thr, cge = lax.fori_loop(0, 32, bit_body, (jnp.full((rows, 1), INT_MIN, I32),
                                               jnp.zeros((rows, 1), F32)))
    thr_f = _key_to_float(jnp.maximum(thr, KEY_LOWEST_FINITE))
    real = (lax.broadcasted_iota(I32, (rows, 1), 0) & 7) < n_new
    has_ties = jnp.max(jnp.where(real, cge, 0.0)) > float(TOPK_MAX)

    @pl.when(jnp.logical_not(has_ties))
    def _plain_mask():
        def to_bias(c, carry):
            off = pl.multiple_of(c * ch, ch)
            bp_ref[:, pl.ds(off, ch)] = jnp.where(sp_ref[:, pl.ds(off, ch)] >= thr_f, 0.0, NEG)
            return carry

        lax.fori_loop(0, nchunk, to_bias, 0)
        bn_ref[...] = jnp.where(sn >= thr_f, 0.0, NEG)

    @pl.when(has_ties)
    def _tie_mask():
        need = float(TOPK_MAX) - count_keys(lambda s, kpos: s > thr_f)
        nbits = (past + 128 - 1).bit_length()

        def pos_bit(b, last):
            step = lax.shift_left(jnp.int32(1), jnp.int32(nbits - 1) - b)
            probe = last + (step - 1)
            got = count_keys(lambda s, kpos: (s == thr_f) & (kpos <= probe))
            return jnp.where(got < need, last + step, last)

        last = lax.fori_loop(0, nbits, pos_bit, jnp.zeros((rows, 1), I32))

        def keep(s, kpos):
            return (s > thr_f) | ((s == thr_f) & (kpos <= last))

        def to_bias(c, carry):
            off = pl.multiple_of(c * ch, ch)
            bp_ref[:, pl.ds(off, ch)] = jnp.where(keep(sp_ref[:, pl.ds(off, ch)], lane_ch + c * ch), 0.0, NEG)
            return carry

        lax.fori_loop(0, nchunk, to_bias, 0)
        bn_ref[...] = jnp.where(keep(sn, lane + past), 0.0, NEG)


def _sample_attn_step(t, n_steps, slot, q_ref, bp_ref, bn_ref, knew_ref, vnew_ref, pbuf,
                      out_ref, s_ref, p_ref, l_ref, acc_ref):
    npg = pbuf.shape[1]
    nt = (((1,), (1,)), ((), ()))
    rg = GROUP * 8
    width = npg * PAGE_SIZE
    past = n_steps * width
    sm_chunk = 2048

    def head_rows(g):
        rows = [pbuf[slot, r, pl.ds(g, PAGE_SIZE, stride=N_KV_HEADS), :] for r in range(npg)]
        return jnp.concatenate(rows, axis=0).astype(BF16)

    @pl.when(t < n_steps)
    def _logits():
        off = pl.multiple_of(t * width, width)
        bias = jnp.concatenate([bp_ref[0]] * GROUP, axis=0)
        for g in range(N_KV_HEADS):
            qg = q_ref[0, g * rg:(g + 1) * rg, :]
            s_ref[g * rg:(g + 1) * rg, pl.ds(off, width)] = lax.dot_general(
                qg, head_rows(g), nt, preferred_element_type=F32) + bias

    @pl.when(t == n_steps - 1)
    def _softmax():
        biasn = jnp.concatenate([bn_ref[0]] * GROUP, axis=0)
        for g in range(N_KV_HEADS):
            qg = q_ref[0, g * rg:(g + 1) * rg, :]
            kg = knew_ref[0, :, g * HEAD_DIM:(g + 1) * HEAD_DIM]
            s_ref[g * rg:(g + 1) * rg, past:past + PAGE_SIZE] = lax.dot_general(
                qg, kg, nt, preferred_element_type=F32) + biasn

        def fold(x, op):
            f = x[:, 0:128]
            for q in range(1, x.shape[1] // 128):
                f = op(f, x[:, q * 128:(q + 1) * 128])
            return f

        def max_chunk(c, m):
            off = pl.multiple_of(c * sm_chunk, sm_chunk)
            return jnp.maximum(m, fold(s_ref[:, pl.ds(off, sm_chunk)], jnp.maximum))

        m = lax.fori_loop(0, past // sm_chunk, max_chunk, s_ref[:, past:past + PAGE_SIZE])
        m = jnp.max(m, axis=1, keepdims=True)

        def exp_chunk(c, l):
            off = pl.multiple_of(c * sm_chunk, sm_chunk)
            p = jnp.exp2(s_ref[:, pl.ds(off, sm_chunk)] - m)
            p_ref[:, pl.ds(off, sm_chunk)] = p.astype(BF16)
            return l + fold(p, jnp.add)

        pn = jnp.exp2(s_ref[:, past:past + PAGE_SIZE] - m)
        p_ref[:, past:past + PAGE_SIZE] = pn.astype(BF16)
        l_ref[...] = lax.fori_loop(0, past // sm_chunk, exp_chunk, pn)
        acc_ref[...] = jnp.zeros(acc_ref.shape, F32)

    @pl.when(t >= n_steps)
    def _values():
        off = pl.multiple_of((t - n_steps) * width, width)
        for g in range(N_KV_HEADS):
            rs = slice(g * rg, (g + 1) * rg)
            acc_ref[rs, :] += jnp.dot(p_ref[rs, pl.ds(off, width)], head_rows(g),
                                      preferred_element_type=F32)

    @pl.when(t == 2 * n_steps - 1)
    def _finish():
        for g in range(N_KV_HEADS):
            rs = slice(g * rg, (g + 1) * rg)
            vg = vnew_ref[0, :, g * HEAD_DIM:(g + 1) * HEAD_DIM]
            acc_ref[rs, :] += jnp.dot(p_ref[rs, past:past + PAGE_SIZE], vg, preferred_element_type=F32)
        out_ref[0] = acc_ref[...] / jnp.sum(l_ref[...], axis=1, keepdims=True)


def _sample_select(qT, qiT, wT, kibf, kbf, v32, cache_kidx, page_table, n_seq, n_tok):
    n_pages = page_table.shape[1]
    past = n_pages * PAGE_SIZE
    npg = S_SCORE_PAGES
    n_steps = n_pages // npg

    def rows_ht(xT, n_heads, dim):
        x = xT.reshape(n_heads, dim, n_seq, n_tok).transpose(2, 0, 3, 1)
        x = jnp.pad(x, ((0, 0), (0, 0), (0, 8 - n_tok), (0, 0)))
        return x.reshape(n_seq, n_heads * 8, dim)

    qi_rows = rows_ht(qiT, N_IDX_HEADS, IDX_DIM)
    q_rows = rows_ht(qT, N_HEADS, HEAD_DIM)
    wcol = jnp.pad(wT.reshape(N_IDX_HEADS, n_seq, n_tok).transpose(1, 0, 2),
                   ((0, 0), (0, 0), (0, 8 - n_tok))).reshape(n_seq, N_IDX_HEADS * 8, 1)

    def pad_keys(x):
        x = x.reshape(n_seq, n_tok, x.shape[-1])
        return jnp.pad(x, ((0, 0), (0, PAGE_SIZE - n_tok), (0, 0)))

    ki_new_t = jnp.swapaxes(pad_keys(kibf[:, 0:IDX_DIM]), 1, 2)
    kidx_t = jnp.swapaxes(cache_kidx, 1, 2)
    k_new = pad_keys(kbf)
    v_new = pad_keys(v32.reshape(n_seq * n_tok, N_KV).astype(BF16))

    def page_spec(shape, r):
        return pl.BlockSpec((None,) + shape, lambda b, pg, pt, r=r: (pt[b, pg * npg + r], 0, 0))

    sp, sn = pl.pallas_call(
        _s_score_body,
        grid_spec=pltpu.PrefetchScalarGridSpec(
            num_scalar_prefetch=1,
            grid=(n_seq, n_steps),
            in_specs=[
                pl.BlockSpec((1, N_IDX_HEADS * 8, IDX_DIM), lambda b, pg, pt: (b, 0, 0)),
                pl.BlockSpec((1, N_IDX_HEADS * 8, 1), lambda b, pg, pt: (b, 0, 0)),
                pl.BlockSpec((1, IDX_DIM, PAGE_SIZE), lambda b, pg, pt: (b, 0, 0)),
            ] + [page_spec((IDX_DIM, PAGE_SIZE), r) for r in range(npg)],
            out_specs=[
                pl.BlockSpec((1, 8, npg * PAGE_SIZE), lambda b, pg, pt: (b, 0, pg)),
                pl.BlockSpec((1, 8, PAGE_SIZE), lambda b, pg, pt: (b, 0, 0)),
            ],
        ),
        out_shape=[jax.ShapeDtypeStruct((n_seq, 8, past), F32),
                   jax.ShapeDtypeStruct((n_seq, 8, PAGE_SIZE), F32)],
        compiler_params=_cparams(2),
        name="sample_scores",
    )(page_table, qi_rows, wcol, ki_new_t, *([kidx_t] * npg))

    rows = n_seq * 8
    bp, bn = pl.pallas_call(
        functools.partial(_s_thresh_body, n_tok),
        grid=(1,),
        in_specs=[pl.BlockSpec((rows, past), lambda i: (0, 0)),
                  pl.BlockSpec((rows, PAGE_SIZE), lambda i: (0, 0))],
        out_specs=[pl.BlockSpec((rows, past), lambda i: (0, 0)),
                   pl.BlockSpec((rows, PAGE_SIZE), lambda i: (0, 0))],
        out_shape=[jax.ShapeDtypeStruct((rows, past), F32),
                   jax.ShapeDtypeStruct((rows, PAGE_SIZE), F32)],
        compiler_params=_cparams(1),
        name="sample_topk_mask",
    )(sp.reshape(rows, past), sn.reshape(rows, PAGE_SIZE))
    return q_rows, bp.reshape(n_seq, 8, past), bn.reshape(n_seq, 8, PAGE_SIZE), k_new, v_new


def _sample_rows_to_tokens(out, n_seq, n_tok):
    out = out.reshape(n_seq, N_HEADS, 8, HEAD_DIM)[:, :, 0:n_tok]
    return out.transpose(0, 2, 1, 3).reshape(n_seq * n_tok, D_ATTN)


def _outproj_body(gated, *refs):
    if gated:
        x_ref, a_ref, gate_ref, mc_ref, wo_ref, o_ref = refs
        ma = (a_ref[...] * gate_ref[...]).astype(BF16)
    else:
        x_ref, a_ref, mc_ref, wo_ref, o_ref = refs
        ma = a_ref[...]
    acc = jnp.dot(ma, wo_ref[0:D_ATTN, :], preferred_element_type=F32)
    acc = acc + jnp.dot(mc_ref[...], wo_ref[D_ATTN:D_ATTN + D_CONV, :], preferred_element_type=F32)
    o_ref[...] = x_ref[...] + acc


def _outproj(x, attn, gate, mixc, wo, tm):
    m = x.shape[0]
    gated = gate is not None
    row = lambda i: (i, 0)
    in_specs = [pl.BlockSpec((tm, D_MODEL), row), pl.BlockSpec((tm, D_ATTN), row)]
    args = [x, attn]
    if gated:
        in_specs.append(pl.BlockSpec((tm, D_ATTN), row))
        args.append(gate)
    in_specs += [pl.BlockSpec((tm, D_CONV), row),
                 pl.BlockSpec((D_ATTN + D_CONV, D_MODEL), lambda i: (0, 0))]
    args += [mixc, wo]
    return pl.pallas_call(
        functools.partial(_outproj_body, gated),
        grid=(m // tm,),
        in_specs=in_specs,
        out_specs=pl.BlockSpec((tm, D_MODEL), row),
        out_shape=jax.ShapeDtypeStruct((m, D_MODEL), F32),
        compiler_params=_cparams(1),
        name="outproj_sample" if gated else "outproj_prompt",
    )(*args)


def _rope_tables(pos):
    posf = pos.astype(F32)[:, None]
    n = pos.shape[0]

    def cs(half):
        inv = ROPE_THETA ** (-jnp.arange(half, dtype=F32) / half)
        ang = posf * inv[None, :]
        return jnp.cos(ang), jnp.sin(ang)

    c16, s16 = cs(ROPE_HALF)
    c8, s8 = cs(IDX_ROPE_HALF)
    one = lambda w: jnp.ones((n, w), F32)
    zero = lambda w: jnp.zeros((n, w), F32)
    rest = HEAD_DIM - ROPE_DIM
    k_c = jnp.concatenate([c16, c16, one(rest)], axis=1)
    k_sa = jnp.concatenate([-s16, zero(HEAD_DIM - ROPE_HALF)], axis=1)
    k_sb = jnp.concatenate([zero(ROPE_HALF), s16, zero(rest)], axis=1)
    return dict(c16T=c16.T, s16T=s16.T, c8T=c8.T, s8T=s8.T, kC=k_c, kSa=k_sa, kSb=k_sb)


def _prep_weights(w_in, g_q, g_k, g_kidx, w_conv, w_out):
    assert w_in.shape == (D_MODEL, D_IN)
    wT = w_in.T.astype(BF16)
    gq = g_q.reshape(HEAD_DIM, 1)
    gk = g_k.reshape(1, HEAD_DIM)
    gki = g_kidx.reshape(IDX_DIM, 1)
    return wT, (gq, gk, gki, w_conv), w_out.astype(BF16)


def kernel(x_prompt, x_sample, cache_k, cache_v, cache_kidx, state_conv, page_table,
           norm_in, w_in, g_q, g_k, g_kidx, w_conv, w_out):
    n_b, seq, _ = x_prompt.shape
    n_s, n_t, _ = x_sample.shape
    depth = w_in.shape[0]
    past = page_table.shape[1] * PAGE_SIZE
    tabs_p = _rope_tables(jnp.arange(seq))
    tabs_s = _rope_tables(jnp.tile(past + jnp.arange(n_t), n_s))

    hp = x_prompt.reshape(n_b * seq, D_MODEL)
    hs = x_sample.reshape(n_s * n_t, D_MODEL)
    outs = [[] for _ in range(8)]
    for l in range(depth):
        w_t, params, wo = _prep_weights(w_in[l], g_q[l], g_k[l], g_kidx[l], w_conv[l], w_out[l])

        (qT, qiT, vT, wT, k32, kbf, v32, ki32, kibf, gate, mixc, utail) = _project(
            hp, norm_in[l], w_t, params, tabs_p, 0)
        st = state_conv[l]
        tok = jnp.arange(n_t)
        e1 = st[:, jnp.full((n_t,), CONV_W - 2)].reshape(n_s * n_t, D_CONV)
        e2 = st[:, jnp.minimum(tok, CONV_W - 2)].reshape(n_s * n_t, D_CONV)
        (qT_s, qiT_s, _, wT_s, k32_s, kbf_s, v32_s, ki32_s, kibf_s, gate_s, mixc_s, u) = _project(
            hs, norm_in[l], w_t, params, tabs_s, n_t, state_rows=(e1, e2))

        sel = _sample_select(qT_s, qiT_s, wT_s, kibf_s, kbf_s, v32_s, cache_kidx[l], page_table, n_s, n_t)
        mixa, attn_rows = _attention(qiT, wT, kibf, qT, kbf, vT, gate, n_b, seq,
                                     *sel, cache_k[l], cache_v[l], page_table)
        attn_s = _sample_rows_to_tokens(attn_rows, n_s, n_t)

        hp = _outproj(hp, mixa, None, mixc, wo, ROW_TM)
        hs = _outproj(hs, attn_s, gate_s, mixc_s, wo, n_s * n_t)
        tps = seq // PROJ_TM
        outs[0].append(k32.reshape(n_b, seq, N_KV_HEADS, HEAD_DIM))
        outs[1].append(v32.reshape(n_b, seq, N_KV_HEADS, HEAD_DIM))
        outs[2].append(ki32.reshape(n_b, seq, IDX_DIM))
        outs[3].append(utail[tps - 1::tps])
        outs[4].append(k32_s.reshape(n_s, n_t, N_KV_HEADS, HEAD_DIM))
        outs[5].append(v32_s.reshape(n_s, n_t, N_KV_HEADS, HEAD_DIM))
        outs[6].append(ki32_s.reshape(n_s, n_t, IDX_DIM))
        outs[7].append(u.reshape(n_s, n_t, D_CONV)[:, n_t - (CONV_W - 1):])

    return (hp.reshape(n_b, seq, D_MODEL), hs.reshape(n_s, n_t, D_MODEL),
            *[jnp.stack(o) for o in outs])
```

```python
import functools

import jax
import jax.numpy as jnp
import numpy as np
from jax import lax
from jax.experimental import pallas as pl
from jax.experimental.pallas import tpu as pltpu

F32 = jnp.float32
BF16 = jnp.bfloat16
I32 = jnp.int32

D_MODEL = 2048
HEAD_DIM = 128
N_HEADS = 8
N_KV_HEADS = 2
GROUP = N_HEADS // N_KV_HEADS
D_ATTN = N_HEADS * HEAD_DIM
D_CONV = 1024
ROPE_DIM = HEAD_DIM // 4
ROPE_HALF = ROPE_DIM // 2
ROPE_THETA = 500000.0
N_IDX_HEADS = 16
IDX_DIM = 64
IDX_ROPE_HALF = IDX_DIM // 8
TOPK_MAX = 256
CONV_W = 3
PAGE_SIZE = 128
EPS = 1e-6
W_IDX_SCALE = (N_IDX_HEADS ** -0.5) * (IDX_DIM ** -0.5)
ATTN_SCALE = HEAD_DIM ** -0.5
Q_PRESCALE = ATTN_SCALE * 1.4426950408889634

INT_MIN = -(2 ** 31)
NEG = -1e30

VMEM_LIMIT_BYTES = 60 * 1024 * 1024

PROJ_TN = 512
PROJ_TM = 1024
PROJ_SUB = 256
ROW_TM = 512
N_KV = N_KV_HEADS * HEAD_DIM
OFF_Q = 0
OFF_K = OFF_Q + D_ATTN
OFF_V = OFF_K + N_KV
OFF_Z = OFF_V + N_KV
OFF_QI = OFF_Z + D_ATTN
OFF_KI = OFF_QI + N_IDX_HEADS * IDX_DIM
OFF_WI = OFF_KI + IDX_DIM
OFF_H = OFF_WI + N_IDX_HEADS
OFF_B = OFF_H + D_CONV
OFF_C = OFF_B + D_CONV
OFF_ZC = OFF_C + D_CONV
D_IN = OFF_ZC + D_CONV
assert OFF_K == 2 * PROJ_TN and OFF_Z == 3 * PROJ_TN and OFF_QI == 5 * PROJ_TN and OFF_KI == 7 * PROJ_TN
KIW_ROWS = 128
assert OFF_KI % KIW_ROWS == 0 and OFF_WI - OFF_KI == IDX_DIM
CONV_CHUNK = 256
N_CONV_CHUNKS = D_CONV // CONV_CHUNK
J_Q = 0
J_KV = 2
J_Z = 3
J_QI = 5
J_CONV = 7
N_MAIN_BLOCKS = J_CONV
N_STEPS = J_CONV + N_CONV_CHUNKS

ATT_TQ = 256
ATT_TK = 256
ATT_SUB = 64
CNT_ACCS = 4

SIDE_PAGES = 16


def _cparams(n_axes):
    return pltpu.CompilerParams(
        dimension_semantics=("arbitrary",) * n_axes,
        vmem_limit_bytes=VMEM_LIMIT_BYTES,
    )


def _silu(x):
    return x * jax.nn.sigmoid(x)


def _indexer_rows_scores(qi, wcol, keys_t):
    d = jnp.dot(qi, keys_t, preferred_element_type=F32)
    val = jnp.maximum(d, 0.0) * wcol
    return val.reshape(N_IDX_HEADS, 8, val.shape[-1]).sum(axis=0)


def _side_scores_step(step, spq, pt_ref, qi_ref, wcol_ref, kinew_ref, kidx_hbm, sp_ref, sn_ref, kbuf, ksem):
    npg = kbuf.shape[1]
    n_side = pt_ref.shape[0] * spq

    def start_pages(n):
        seq_n = n // spq
        base = (n % spq) * npg
        for r in range(npg):
            _page_copy(kidx_hbm, pt_ref[seq_n, base + r], kbuf, n % 2, r, ksem).start()

    @pl.when(step == 0)
    def _prologue():
        start_pages(step)

    @pl.when(step + 1 < n_side)
    def _prefetch():
        start_pages(step + 1)

    @pl.when(step < n_side)
    def _scores():
        slot = step % 2
        for r in range(npg):
            _page_copy(kidx_hbm, 0, kbuf, slot, r, ksem).wait()
        qi = qi_ref[0]
        wcol = wcol_ref[0]
        for r in range(npg):
            sp_ref[0, :, r * PAGE_SIZE:(r + 1) * PAGE_SIZE] = _indexer_rows_scores(
                qi, wcol, kbuf[slot, r].astype(BF16))

        @pl.when(step % spq == 0)
        def _new():
            sn_ref[0] = _indexer_rows_scores(qi, wcol, kinew_ref[0])


def _proj_body(sample, tps, tm, side_spq, *refs):
    if side_spq:
        pt_ref, refs = refs[0], refs[1:]
    (x_ref, gin_ref, wa_ref, wkiw_ref, wh_ref, wb_ref, wc_ref, wzc_ref, gq_ref, gk_ref, gki_ref,
     c16_ref, s16_ref, c8_ref, s8_ref,
     kc_ref, ksa_ref, ksb_ref, wconv_ref) = refs[:19]
    refs = refs[19:]
    if sample:
        e1_ref, e2_ref = refs[:2]
        refs = refs[2:]
    if side_spq:
        side_in, refs = refs[:4], refs[4:]
    (qT_ref, qiT_ref, vT_ref, wT_ref, k32_ref, kbf_ref, v32_ref, ki32_ref, kibf_ref,
     gate_ref, mixc_ref, u_ref) = refs[:12]
    refs = refs[12:]
    if side_spq:
        side_out, refs = refs[:2], refs[2:]
    xn_ref = refs[0]
    if not sample:
        carry_ref = refs[1]

    i = pl.program_id(0)
    j = pl.program_id(1)
    nt = (((1,), (1,)), ((), ()))
    if side_spq:
        _side_scores_step(i * N_STEPS + j, side_spq, pt_ref, *side_in, *side_out, *refs[2:4])

    def xw(w, x=None):
        return lax.dot_general(xn_ref[...] if x is None else x, w, nt, preferred_element_type=F32)

    def wx(w, x):
        return lax.dot_general(w, x, nt, preferred_element_type=F32)

    nsub = max(tm // PROJ_SUB, 1)
    sub = tm // nsub
    pieces = [slice(rb * sub, (rb + 1) * sub) for rb in range(nsub)]

    if not sample:
        @pl.when((i == 0) & (j == 0))
        def _init():
            carry_ref[...] = jnp.zeros(carry_ref.shape, F32)

    def q_step(with_norm):
        for cs in pieces:
            if with_norm:
                x = x_ref[cs, :]
                ms = jnp.mean(x * x, axis=-1, keepdims=True)
                xn_ref[cs, :] = (x * lax.rsqrt(ms + EPS) * gin_ref[...]).astype(BF16)
            res = wx(wa_ref[...], xn_ref[cs, :])
            cos = c16_ref[:, cs]
            sin = s16_ref[:, cs]
            for hh in range(PROJ_TN // HEAD_DIM):
                blk = res[hh * HEAD_DIM:(hh + 1) * HEAD_DIM]
                ms = jnp.mean(blk * blk, axis=0, keepdims=True)
                y = blk * lax.rsqrt(ms + EPS) * gq_ref[...]
                x1 = y[0:ROPE_HALF]
                x2 = y[ROPE_HALF:ROPE_DIM]
                base = hh * HEAD_DIM
                qT_ref[base:base + ROPE_HALF, cs] = ((x1 * cos - x2 * sin) * Q_PRESCALE).astype(BF16)
                qT_ref[base + ROPE_HALF:base + ROPE_DIM, cs] = ((x2 * cos + x1 * sin) * Q_PRESCALE).astype(BF16)
                qT_ref[base + ROPE_DIM:base + HEAD_DIM, cs] = (y[ROPE_DIM:] * Q_PRESCALE).astype(BF16)

    pl.when(j == J_Q)(functools.partial(q_step, True))
    pl.when((j > J_Q) & (j < J_KV))(functools.partial(q_step, False))

    @pl.when((j >= J_QI) & (j < J_CONV))
    def _qi():
        for cs in pieces:
            res = wx(wa_ref[...], xn_ref[cs, :])
            cos = c8_ref[:, cs]
            sin = s8_ref[:, cs]
            for hh in range(PROJ_TN // IDX_DIM):
                blk = res[hh * IDX_DIM:(hh + 1) * IDX_DIM]
                x1 = blk[0:IDX_ROPE_HALF]
                x2 = blk[IDX_ROPE_HALF:2 * IDX_ROPE_HALF]
                rot = jnp.concatenate([x1 * cos - x2 * sin, x2 * cos + x1 * sin], axis=0)
                base = hh * IDX_DIM
                qiT_ref[base:base + 2 * IDX_ROPE_HALF, cs] = rot.astype(BF16)
                qiT_ref[base + 2 * IDX_ROPE_HALF:base + IDX_DIM, cs] = blk[2 * IDX_ROPE_HALF:].astype(BF16)

    @pl.when(j == J_KV)
    def _kv():
        for rb, cs in enumerate(pieces):
            x = xn_ref[cs, :]
            res = xw(wa_ref[...], x)
            for hd in range(N_KV_HEADS):
                hs = slice(hd * HEAD_DIM, (hd + 1) * HEAD_DIM)
                blk = res[:, hs]
                ms = jnp.mean(blk * blk, axis=-1, keepdims=True)
                yk = blk * lax.rsqrt(ms + EPS) * gk_ref[...]
                rot = yk * kc_ref[cs, :] + (pltpu.roll(yk, HEAD_DIM - ROPE_HALF, axis=1) * ksa_ref[cs, :]
                                            + pltpu.roll(yk, ROPE_HALF, axis=1) * ksb_ref[cs, :])
                k32_ref[pl.ds(N_KV_HEADS * rb * sub + hd, sub, stride=N_KV_HEADS), :] = rot
                kbf_ref[cs, hs] = rot.astype(BF16)
            v = res[:, N_KV:2 * N_KV]
            for hd in range(N_KV_HEADS):
                v32_ref[pl.ds(N_KV_HEADS * rb * sub + hd, sub, stride=N_KV_HEADS), :] = (
                    v[:, hd * HEAD_DIM:(hd + 1) * HEAD_DIM])
            if sample:
                vT_ref[:, cs] = wx(wa_ref[N_KV:2 * N_KV, :], x).astype(BF16)
            else:
                vT_ref[:, cs] = v.T.astype(BF16)
            r2t = wx(wkiw_ref[...], x)
            wT_ref[:, cs] = r2t[IDX_DIM:IDX_DIM + N_IDX_HEADS] * W_IDX_SCALE
            kit = r2t[0:IDX_DIM]
            ms = jnp.mean(kit * kit, axis=0, keepdims=True)
            yi = kit * lax.rsqrt(ms + EPS) * gki_ref[...]
            x1 = yi[0:IDX_ROPE_HALF]
            x2 = yi[IDX_ROPE_HALF:2 * IDX_ROPE_HALF]
            cos = c8_ref[:, cs]
            sin = s8_ref[:, cs]
            roti = jnp.concatenate([x1 * cos - x2 * sin, x2 * cos + x1 * sin, yi[2 * IDX_ROPE_HALF:],
                                    jnp.zeros((KIW_ROWS - IDX_DIM, sub), F32)], axis=0)
            ki_nat = roti.T
            ki32_ref[cs, :] = ki_nat[:, 0:IDX_DIM]
            kibf_ref[cs, :] = ki_nat.astype(BF16)

    @pl.when((j >= J_Z) & (j < J_QI))
    def _z():
        for cs in pieces:
            gate_ref[cs, :] = _silu(xw(wa_ref[...], xn_ref[cs, :]))

    @pl.when(j >= J_CONV)
    def _conv():
        cc = j - J_CONV
        u = xw(wc_ref[...]) * xw(wh_ref[...])
        rowid = lax.broadcasted_iota(I32, (tm, CONV_CHUNK), 0)
        if sample:
            t = rowid & (sample - 1)
            u1 = jnp.where(t >= 1, pltpu.roll(u, 1, axis=0), e1_ref[...])
            u2 = jnp.where(t >= 2, pltpu.roll(u, 2, axis=0), e2_ref[...])
            u_ref[...] = u
        else:
            first = (i % tps) == 0
            prev = carry_ref[cc]
            p0 = jnp.where(first, 0.0, prev[0:1])
            p1 = jnp.where(first, 0.0, prev[1:2])
            u1 = jnp.where(rowid == 0, p1, pltpu.roll(u, 1, axis=0))
            u2 = jnp.where(rowid == 0, p0, jnp.where(rowid == 1, p1, pltpu.roll(u, 2, axis=0)))
            tail = u[tm - 8:tm]
            carry_ref[cc] = jnp.concatenate([tail[6:8], tail[0:6]], axis=0)
            u_ref[0] = tail[6:8]
        w = wconv_ref[...]
        y = u2 * w[0:1] + u1 * w[1:2] + u * w[2:3]
        mixc_ref[...] = (xw(wb_ref[...]) * y * _silu(xw(wzc_ref[...]))).astype(BF16)


def _project(x, norm_g, wT, params, tabs, sample, state_rows=None, side=None):
    m = x.shape[0]
    assert sample & (sample - 1) == 0
    tm = m if sample else PROJ_TM
    n_i = m // tm
    tps = 1 if sample else (tabs["c16T"].shape[1] // tm)
    gq, gk, gki, wconv = params

    def tmap(i):
        return i % tps

    def cchunk(j):
        return jnp.clip(j - J_CONV, 0, N_CONV_CHUNKS - 1)

    def conv_rows(off):
        def imap(i, j, *_):
            chunk = jnp.where(j < J_QI, N_CONV_CHUNKS - 1, cchunk(j))
            return (pl.multiple_of(off + CONV_CHUNK * chunk, 16), 0)
        return pl.BlockSpec((pl.Element(CONV_CHUNK), pl.Element(D_MODEL)), imap)

    in_specs = [
        pl.BlockSpec((tm, D_MODEL), lambda i, j, *_: (i, 0)),
        pl.BlockSpec((1, D_MODEL), lambda i, j, *_: (0, 0)),
        pl.BlockSpec((PROJ_TN, D_MODEL), lambda i, j, *_: (jnp.minimum(j, N_MAIN_BLOCKS - 1), 0)),
        pl.BlockSpec((KIW_ROWS, D_MODEL), lambda i, j, *_: (OFF_KI // KIW_ROWS, 0)),
        conv_rows(OFF_H), conv_rows(OFF_B), conv_rows(OFF_C), conv_rows(OFF_ZC),
        pl.BlockSpec((HEAD_DIM, 1), lambda i, j, *_: (0, 0)),
        pl.BlockSpec((1, HEAD_DIM), lambda i, j, *_: (0, 0)),
        pl.BlockSpec((IDX_DIM, 1), lambda i, j, *_: (0, 0)),
        pl.BlockSpec((ROPE_HALF, tm), lambda i, j, *_: (0, tmap(i))),
        pl.BlockSpec((ROPE_HALF, tm), lambda i, j, *_: (0, tmap(i))),
        pl.BlockSpec((IDX_ROPE_HALF, tm), lambda i, j, *_: (0, tmap(i))),
        pl.BlockSpec((IDX_ROPE_HALF, tm), lambda i, j, *_: (0, tmap(i))),
        pl.BlockSpec((tm, HEAD_DIM), lambda i, j, *_: (tmap(i), 0)),
        pl.BlockSpec((tm, HEAD_DIM), lambda i, j, *_: (tmap(i), 0)),
        pl.BlockSpec((tm, HEAD_DIM), lambda i, j, *_: (tmap(i), 0)),
        pl.BlockSpec((CONV_W, CONV_CHUNK), lambda i, j, *_: (0, cchunk(j))),
    ]
    args = [x, norm_g.reshape(1, D_MODEL), wT, wT, wT, wT, wT, wT, gq, gk, gki,
            tabs["c16T"], tabs["s16T"], tabs["c8T"], tabs["s8T"],
            tabs["kC"], tabs["kSa"], tabs["kSb"], wconv]
    cmap = lambda i, j, *_: (i, cchunk(j))
    if sample:
        in_specs += [pl.BlockSpec((tm, CONV_CHUNK), cmap), pl.BlockSpec((tm, CONV_CHUNK), cmap)]
        args += list(state_rows)
        u_spec = pl.BlockSpec((tm, CONV_CHUNK), cmap)
        u_shape = jax.ShapeDtypeStruct((m, D_CONV), F32)
        scratch = [pltpu.VMEM((tm, D_MODEL), BF16)]
    else:
        u_spec = pl.BlockSpec((1, CONV_W - 1, CONV_CHUNK), lambda i, j, *_: (i, 0, cchunk(j)))
        u_shape = jax.ShapeDtypeStruct((n_i, CONV_W - 1, D_CONV), F32)
        scratch = [pltpu.VMEM((tm, D_MODEL), BF16), pltpu.VMEM((N_CONV_CHUNKS, 8, CONV_CHUNK), F32)]

    out_specs = [
        pl.BlockSpec((PROJ_TN, tm), lambda i, j, *_: (jnp.minimum(j, 1), i)),
        pl.BlockSpec((PROJ_TN, tm), lambda i, j, *_: (jnp.clip(j - J_QI, 0, 1), i)),
        pl.BlockSpec((N_KV, tm), lambda i, j, *_: (0, i)),
        pl.BlockSpec((N_IDX_HEADS, tm), lambda i, j, *_: (0, i)),
        pl.BlockSpec((N_KV_HEADS * tm, HEAD_DIM), lambda i, j, *_: (i, 0)),
        pl.BlockSpec((tm, N_KV), lambda i, j, *_: (i, 0)),
        pl.BlockSpec((N_KV_HEADS * tm, HEAD_DIM), lambda i, j, *_: (i, 0)),
        pl.BlockSpec((tm, IDX_DIM), lambda i, j, *_: (i, 0)),
        pl.BlockSpec((tm, KIW_ROWS), lambda i, j, *_: (i, 0)),
        pl.BlockSpec((tm, PROJ_TN), lambda i, j, *_: (i, jnp.clip(j - J_Z, 0, 1))),
        pl.BlockSpec((tm, CONV_CHUNK), cmap),
        u_spec,
    ]
    out_shape = [
        jax.ShapeDtypeStruct((D_ATTN, m), BF16),
        jax.ShapeDtypeStruct((N_IDX_HEADS * IDX_DIM, m), BF16),
        jax.ShapeDtypeStruct((N_KV, m), BF16),
        jax.ShapeDtypeStruct((N_IDX_HEADS, m), F32),
        jax.ShapeDtypeStruct((N_KV_HEADS * m, HEAD_DIM), F32),
        jax.ShapeDtypeStruct((m, N_KV), BF16),
        jax.ShapeDtypeStruct((N_KV_HEADS * m, HEAD_DIM), F32),
        jax.ShapeDtypeStruct((m, IDX_DIM), F32),
        jax.ShapeDtypeStruct((m, KIW_ROWS), BF16),
        jax.ShapeDtypeStruct((m, D_ATTN), F32),
        jax.ShapeDtypeStruct((m, D_CONV), BF16),
        u_shape,
    ]
    if side is None:
        return pl.pallas_call(
            functools.partial(_proj_body, sample, tps, tm, 0),
            grid=(n_i, N_STEPS),
            in_specs=in_specs,
            out_specs=out_specs,
            out_shape=out_shape,
            scratch_shapes=scratch,
            compiler_params=_cparams(2),
            name="proj_sample" if sample else "proj_prompt",
        )(*args)

    page_table, qi_rows, wcol, ki_new_t, kidx_t = side
    n_seq, n_pages = page_table.shape
    spq = n_pages // SIDE_PAGES
    n_side = n_seq * spq
    assert spq * SIDE_PAGES == n_pages and n_side <= n_i * N_STEPS
    width = SIDE_PAGES * PAGE_SIZE

    def side_step(i, j):
        return jnp.minimum(i * N_STEPS + j, n_side - 1)

    seq_map = lambda i, j, *_: (side_step(i, j) // spq, 0, 0)
    in_specs += [
        pl.BlockSpec((1, N_IDX_HEADS * 8, IDX_DIM), seq_map),
        pl.BlockSpec((1, N_IDX_HEADS * 8, 1), seq_map),
        pl.BlockSpec((1, IDX_DIM, PAGE_SIZE), seq_map),
        pl.BlockSpec(memory_space=pl.ANY),
    ]
    out_specs += [
        pl.BlockSpec((1, 8, width), lambda i, j, *_: (side_step(i, j) // spq, 0, side_step(i, j) % spq)),
        pl.BlockSpec((1, 8, PAGE_SIZE), seq_map),
    ]
    out_shape += [jax.ShapeDtypeStruct((n_seq, 8, n_pages * PAGE_SIZE), F32),
                  jax.ShapeDtypeStruct((n_seq, 8, PAGE_SIZE), F32)]
    scratch += [pltpu.VMEM((2, SIDE_PAGES, IDX_DIM, PAGE_SIZE), F32), pltpu.SemaphoreType.DMA((2,))]
    return pl.pallas_call(
        functools.partial(_proj_body, sample, tps, tm, spq),
        grid_spec=pltpu.PrefetchScalarGridSpec(
            num_scalar_prefetch=1,
            grid=(n_i, N_STEPS),
            in_specs=in_specs,
            out_specs=out_specs,
            scratch_shapes=scratch,
        ),
        out_shape=out_shape,
        compiler_params=_cparams(2),
        name="proj_prompt",
    )(page_table, *args, qi_rows, wcol, ki_new_t, kidx_t)


KEY_LOWEST_FINITE = INT_MIN + 0x00800000


def _key_to_float(key):
    return pltpu.bitcast(key ^ ((key >> 31) & 0x7FFFFFFF), F32)


def _bit_value(b):
    return lax.shift_left(jnp.int32(1), jnp.int32(31) - b)


def _prompt_attn_step(i, qiT_ref, wT_ref, ki_ref, qT_ref, k_ref, vT_ref, gate_ref, o_ref,
                      sc_ref, bias_ref, acc_ref, s_ref):
    tq, tk = ATT_TQ, ATT_TK
    nch = i + 1
    w = wT_ref[...]
    row = lax.broadcasted_iota(I32, (tk, tq), 0)
    col = lax.broadcasted_iota(I32, (tk, tq), 1)
    row8 = lax.broadcasted_iota(I32, (8, tq), 0)

    def score_chunk(j, carry):
        off = pl.multiple_of(j * tk, tk)
        kic = ki_ref[pl.ds(off, tk), 0:IDX_DIM]
        acc = jnp.zeros((tk, tq), F32)
        for h in range(N_IDX_HEADS):
            d = jnp.dot(kic, qiT_ref[h * IDX_DIM:(h + 1) * IDX_DIM, :], preferred_element_type=F32)
            acc = acc + w[h:h + 1, :] * jnp.maximum(d, 0.0)
        future = (row + j * tk) > (col + i * tq)
        sc_ref[pl.ds(off, tk), :] = jnp.where(future, -jnp.inf, acc)
        return carry

    lax.fori_loop(0, nch, score_chunk, 0)

    def count_keys(pred):
        def cnt_chunk(j, cs):
            off = pl.multiple_of(j * tk, tk)
            cs = list(cs)
            sc = sc_ref[pl.ds(off, tk), :]
            for r in range(tk // 8):
                a = cs[r % CNT_ACCS]
                kpos = row8 + (j * tk + r * 8)
                cs[r % CNT_ACCS] = jnp.where(pred(sc[r * 8:(r + 1) * 8], kpos), a + 1, a)
            return tuple(cs)

        cs = lax.fori_loop(0, nch, cnt_chunk, (jnp.zeros((8, tq), I32),) * CNT_ACCS)
        c = cs[0]
        for a in cs[1:]:
            c = c + a
        return jnp.sum(c.astype(F32), axis=0, keepdims=True)

    def bit_body(b, carry):
        thr, cge = carry
        cand = thr + _bit_value(b)
        cand_f = _key_to_float(cand)
        cnt = count_keys(lambda s, kpos: s >= cand_f)
        ok = cnt >= float(TOPK_MAX)
        return jnp.where(ok, cand, thr), jnp.where(ok, cnt, cge)

    thr, cge = lax.fori_loop(0, 32, bit_body, (jnp.full((1, tq), INT_MIN, I32),
                                               jnp.zeros((1, tq), F32)))
    thr_f = _key_to_float(jnp.maximum(thr, KEY_LOWEST_FINITE))
    has_ties = jnp.max(cge) > float(TOPK_MAX)

    @pl.when(jnp.logical_not(has_ties))
    def _plain_mask():
        def bias_chunk(j, carry):
            off = pl.multiple_of(j * tk, tk)
            bias_ref[pl.ds(off, tk), :] = jnp.where(sc_ref[pl.ds(off, tk), :] >= thr_f, 0.0, NEG)
            return carry

        lax.fori_loop(0, nch, bias_chunk, 0)

    @pl.when(has_ties)
    def _tie_mask():
        need = float(TOPK_MAX) - count_keys(lambda s, kpos: s > thr_f)
        nbits = (sc_ref.shape[0] - 1).bit_length()

        def pos_bit(b, last):
            step = lax.shift_left(jnp.int32(1), jnp.int32(nbits - 1) - b)
            probe = last + (step - 1)
            got = count_keys(lambda s, kpos: (s == thr_f) & (kpos <= probe))
            return jnp.where(got < need, last + step, last)

        last = lax.fori_loop(0, nbits, pos_bit, jnp.zeros((1, tq), I32))

        def bias_chunk(j, carry):
            off = pl.multiple_of(j * tk, tk)
            sc = sc_ref[pl.ds(off, tk), :]
            keep = (sc > thr_f) | ((sc == thr_f) & ((row + j * tk) <= last))
            bias_ref[pl.ds(off, tk), :] = jnp.where(keep, 0.0, NEG)
            return carry

        lax.fori_loop(0, nch, bias_chunk, 0)

    def qk_chunk(j, mrun):
        off = pl.multiple_of(j * tk, tk)
        bias = bias_ref[pl.ds(off, tk), :]
        out = []
        for h in range(N_HEADS):
            g = h // GROUP
            kc = k_ref[pl.ds(off, tk), g * HEAD_DIM:(g + 1) * HEAD_DIM]
            s = jnp.dot(kc, qT_ref[h * HEAD_DIM:(h + 1) * HEAD_DIM, :], preferred_element_type=F32) + bias
            s_ref[h, pl.ds(off, tk), :] = s
            out.append(jnp.maximum(mrun[h], s.reshape(tk // 8, 8, tq).max(axis=0)))
        return tuple(out)

    mrun = lax.fori_loop(0, nch, qk_chunk, (jnp.full((8, tq), NEG, F32),) * N_HEADS)
    ms = [jnp.max(mr, axis=0, keepdims=True) for mr in mrun]
    acc_ref[...] = jnp.zeros(acc_ref.shape, F32)

    def pv_chunk(j, lrun):
        off = pl.multiple_of(j * tk, tk)
        out = []
        for h in range(N_HEADS):
            g = h // GROUP
            hs = slice(h * HEAD_DIM, (h + 1) * HEAD_DIM)
            p = jnp.exp2(s_ref[h, pl.ds(off, tk), :] - ms[h])
            out.append(lrun[h] + p.reshape(tk // 8, 8, tq).sum(axis=0))
            vc = vT_ref[g * HEAD_DIM:(g + 1) * HEAD_DIM, pl.ds(off, tk)]
            acc_ref[hs, :] += jnp.dot(vc, p.astype(BF16), preferred_element_type=F32)
        return tuple(out)

    lrun = lax.fori_loop(0, nch, pv_chunk, (jnp.zeros((8, tq), F32),) * N_HEADS)

    for h in range(N_HEADS):
        hs = slice(h * HEAD_DIM, (h + 1) * HEAD_DIM)
        o = (acc_ref[hs, :] / jnp.sum(lrun[h], axis=0, keepdims=True)).T
        o_ref[:, hs] = (o * gate_ref[:, hs]).astype(BF16)


def _page_copy(cache_hbm, page, pbuf, slot, r, sem):
    return pltpu.make_async_copy(cache_hbm.at[page], pbuf.at[slot, r], sem.at[slot])


def _attn_fused_body(nq, spq, pt_ref, *refs):
    prompt_in = refs[0:7]
    q_ref, bp_ref, bn_ref, knew_ref, vnew_ref, ck_hbm, cv_hbm = refs[7:14]
    o_ref, so_ref = refs[14:16]
    prompt_scratch = refs[16:20]
    pbuf, sem, ss_ref, ps_ref, ls_ref, accs_ref = refs[20:26]

    i = pl.program_id(1)
    step = pl.program_id(0) * nq + i
    n_total = pl.num_programs(0) * nq
    half = spq // 2
    npg = pbuf.shape[1]

    def start_pages(n):
        seq_n = n // spq
        t_n = n % spq
        base = (t_n % half) * npg
        slot = n % 2
        for cache_hbm, pred in ((ck_hbm, t_n < half), (cv_hbm, t_n >= half)):
            @pl.when(pred)
            def _(cache_hbm=cache_hbm):
                for r in range(npg):
                    _page_copy(cache_hbm, pt_ref[seq_n, base + r], pbuf, slot, r, sem).start()

    @pl.when(step == 0)
    def _prologue():
        start_pages(step)

    @pl.when(step + 1 < n_total)
    def _prefetch():
        start_pages(step + 1)

    slot = step % 2
    for r in range(npg):
        _page_copy(ck_hbm, 0, pbuf, slot, r, sem).wait()

    _sample_attn_step(step % spq, half, slot, q_ref, bp_ref, bn_ref, knew_ref, vnew_ref, pbuf,
                      so_ref, ss_ref, ps_ref, ls_ref, accs_ref)
    _prompt_attn_step(i, *prompt_in, o_ref, *prompt_scratch)


def _attention(qiT, wT, kibf, qT, kbf, vT, gate, n_batch, seq,
               q_rows, bp, bn, k_new, v_new, cache_k, cache_v, page_table):
    m = n_batch * seq
    nq = seq // ATT_TQ
    n_seq, n_pages = page_table.shape
    n_steps = n_batch * nq
    spq = n_steps // n_seq
    assert spq * n_seq == n_steps and spq % 2 == 0
    half = spq // 2
    npg = n_pages // half
    assert npg * half == n_pages
    width = npg * PAGE_SIZE
    past = n_pages * PAGE_SIZE
    n_pool = cache_k.shape[0]
    kv_rows = PAGE_SIZE * N_KV_HEADS
    ck = cache_k.reshape(n_pool, kv_rows, HEAD_DIM)
    cv = cache_v.reshape(n_pool, kv_rows, HEAD_DIM)
    nrow = N_HEADS * 8

    qmap = lambda b, i, pt: (0, b * nq + i)
    smap = lambda b, i, pt: ((b * nq + i) // spq, 0, 0)
    return pl.pallas_call(
        functools.partial(_attn_fused_body, nq, spq),
        grid_spec=pltpu.PrefetchScalarGridSpec(
            num_scalar_prefetch=1,
            grid=(n_batch, nq),
            in_specs=[
                pl.BlockSpec((N_IDX_HEADS * IDX_DIM, ATT_TQ), qmap),
                pl.BlockSpec((N_IDX_HEADS, ATT_TQ), qmap),
                pl.BlockSpec((seq, 128), lambda b, i, pt: (b, 0)),
                pl.BlockSpec((D_ATTN, ATT_TQ), qmap),
                pl.BlockSpec((seq, N_KV), lambda b, i, pt: (b, 0)),
                pl.BlockSpec((N_KV, seq), lambda b, i, pt: (0, b)),
                pl.BlockSpec((ATT_TQ, D_ATTN), lambda b, i, pt: (b * nq + i, 0)),
                pl.BlockSpec((1, nrow, HEAD_DIM), smap),
                pl.BlockSpec((1, 8, width),
                             lambda b, i, pt: ((b * nq + i) // spq, 0, jnp.minimum((b * nq + i) % spq, half - 1))),
                pl.BlockSpec((1, 8, PAGE_SIZE), smap),
                pl.BlockSpec((1, PAGE_SIZE, N_KV), smap),
                pl.BlockSpec((1, PAGE_SIZE, N_KV), smap),
                pl.BlockSpec(memory_space=pl.ANY),
                pl.BlockSpec(memory_space=pl.ANY),
            ],
            out_specs=[
                pl.BlockSpec((ATT_TQ, D_ATTN), lambda b, i, pt: (b * nq + i, 0)),
                pl.BlockSpec((1, nrow, HEAD_DIM), smap),
            ],
            scratch_shapes=[
                pltpu.VMEM((seq, ATT_TQ), F32), pltpu.VMEM((seq, ATT_TQ), F32),
                pltpu.VMEM((D_ATTN, ATT_TQ), F32), pltpu.VMEM((N_HEADS, seq, ATT_TQ), F32),
                pltpu.VMEM((2, npg, kv_rows, HEAD_DIM), F32), pltpu.SemaphoreType.DMA((2,)),
                pltpu.VMEM((nrow, past + PAGE_SIZE), F32), pltpu.VMEM((nrow, past + PAGE_SIZE), BF16),
                pltpu.VMEM((nrow, 128), F32), pltpu.VMEM((nrow, HEAD_DIM), F32),
            ],
        ),
        out_shape=[jax.ShapeDtypeStruct((m, D_ATTN), BF16),
                   jax.ShapeDtypeStruct((n_seq, nrow, HEAD_DIM), F32)],
        compiler_params=_cparams(2),
        name="attention",
    )(page_table, qiT, wT, kibf, qT, kbf, vT, gate, q_rows, bp, bn, k_new, v_new, ck, cv)


def _s_thresh_body(n_new, sp_ref, sn_ref, bp_ref, bn_ref):
    rows, past = sp_ref.shape
    ch = 2048
    nchunk = past // ch
    t = lax.broadcasted_iota(I32, (rows, 128), 0) & 7
    lane = lax.broadcasted_iota(I32, (rows, 128), 1)
    sn = jnp.where((lane < n_new) & (lane <= t), sn_ref[...], -jnp.inf)

    def fold(x):
        f = x[:, 0:128]
        for q in range(1, x.shape[1] // 128):
            f = f + x[:, q * 128:(q + 1) * 128]
        return f

    lane_ch = lax.broadcasted_iota(I32, (rows, ch), 1)

    def count_keys(pred):
        def cnt_chunk(c, acc):
            off = pl.multiple_of(c * ch, ch)
            return acc + fold(jnp.where(pred(sp_ref[:, pl.ds(off, ch)], lane_ch + c * ch), 1.0, 0.0))

        acc = lax.fori_loop(0, nchunk, cnt_chunk, jnp.where(pred(sn, lane + past), 1.0, 0.0))
        return jnp.sum(acc, axis=1, keepdims=True)

    def bit_body(b, carry):
        thr, cge = carry
        cand = thr + _bit_value(b)
        cand_f = _key_to_float(cand)
        cnt = count_keys(lambda s, kpos: s >= cand_f)
        ok = cnt >= float(TOPK_MAX)
        return jnp.where(ok, cand, thr), jnp.where(ok, cnt, cge)

    thr, cge = lax.fori_loop(0, 32, bit_body, (jnp.full((rows, 1), INT_MIN, I32),
                                               jnp.zeros((rows, 1), F32)))
    thr_f = _key_to_float(jnp.maximum(thr, KEY_LOWEST_FINITE))
    real = (lax.broadcasted_iota(I32, (rows, 1), 0) & 7) < n_new
    has_ties = jnp.max(jnp.where(real, cge, 0.0)) > float(TOPK_MAX)

    @pl.when(jnp.logical_not(has_ties))
    def _plain_mask():
        def to_bias(c, carry):
            off = pl.multiple_of(c * ch, ch)
            bp_ref[:, pl.ds(off, ch)] = jnp.where(sp_ref[:, pl.ds(off, ch)] >= thr_f, 0.0, NEG)
            return carry

        lax.fori_loop(0, nchunk, to_bias, 0)
        bn_ref[...] = jnp.where(sn >= thr_f, 0.0, NEG)

    @pl.when(has_ties)
    def _tie_mask():
        need = float(TOPK_MAX) - count_keys(lambda s, kpos: s > thr_f)
        nbits = (past + 128 - 1).bit_length()

        def pos_bit(b, last):
            step = lax.shift_left(jnp.int32(1), jnp.int32(nbits - 1) - b)
            probe = last + (step - 1)
            got = count_keys(lambda s, kpos: (s == thr_f) & (kpos <= probe))
            return jnp.where(got < need, last + step, last)

        last = lax.fori_loop(0, nbits, pos_bit, jnp.zeros((rows, 1), I32))

        def keep(s, kpos):
            return (s > thr_f) | ((s == thr_f) & (kpos <= last))

        def to_bias(c, carry):
            off = pl.multiple_of(c * ch, ch)
            bp_ref[:, pl.ds(off, ch)] = jnp.where(keep(sp_ref[:, pl.ds(off, ch)], lane_ch + c * ch), 0.0, NEG)
            return carry

        lax.fori_loop(0, nchunk, to_bias, 0)
        bn_ref[...] = jnp.where(keep(sn, lane + past), 0.0, NEG)


def _sample_attn_step(t, n_steps, slot, q_ref, bp_ref, bn_ref, knew_ref, vnew_ref, pbuf,
                      out_ref, s_ref, p_ref, l_ref, acc_ref):
    npg = pbuf.shape[1]
    nt = (((1,), (1,)), ((), ()))
    rg = GROUP * 8
    width = npg * PAGE_SIZE
    past = n_steps * width
    sm_chunk = 2048

    def head_rows(g):
        rows = [pbuf[slot, r, pl.ds(g, PAGE_SIZE, stride=N_KV_HEADS), :] for r in range(npg)]
        return jnp.concatenate(rows, axis=0).astype(BF16)

    @pl.when(t < n_steps)
    def _logits():
        off = pl.multiple_of(t * width, width)
        bias = jnp.concatenate([bp_ref[0]] * GROUP, axis=0)
        for g in range(N_KV_HEADS):
            qg = q_ref[0, g * rg:(g + 1) * rg, :]
            s_ref[g * rg:(g + 1) * rg, pl.ds(off, width)] = lax.dot_general(
                qg, head_rows(g), nt, preferred_element_type=F32) + bias

    @pl.when(t == n_steps - 1)
    def _softmax():
        biasn = jnp.concatenate([bn_ref[0]] * GROUP, axis=0)
        for g in range(N_KV_HEADS):
            qg = q_ref[0, g * rg:(g + 1) * rg, :]
            kg = knew_ref[0, :, g * HEAD_DIM:(g + 1) * HEAD_DIM]
            s_ref[g * rg:(g + 1) * rg, past:past + PAGE_SIZE] = lax.dot_general(
                qg, kg, nt, preferred_element_type=F32) + biasn

        def fold(x, op):
            f = x[:, 0:128]
            for q in range(1, x.shape[1] // 128):
                f = op(f, x[:, q * 128:(q + 1) * 128])
            return f

        def max_chunk(c, m):
            off = pl.multiple_of(c * sm_chunk, sm_chunk)
            return jnp.maximum(m, fold(s_ref[:, pl.ds(off, sm_chunk)], jnp.maximum))

        m = lax.fori_loop(0, past // sm_chunk, max_chunk, s_ref[:, past:past + PAGE_SIZE])
        m = jnp.max(m, axis=1, keepdims=True)

        def exp_chunk(c, l):
            off = pl.multiple_of(c * sm_chunk, sm_chunk)
            p = jnp.exp2(s_ref[:, pl.ds(off, sm_chunk)] - m)
            p_ref[:, pl.ds(off, sm_chunk)] = p.astype(BF16)
            return l + fold(p, jnp.add)

        pn = jnp.exp2(s_ref[:, past:past + PAGE_SIZE] - m)
        p_ref[:, past:past + PAGE_SIZE] = pn.astype(BF16)
        l_ref[...] = lax.fori_loop(0, past // sm_chunk, exp_chunk, pn)
        acc_ref[...] = jnp.zeros(acc_ref.shape, F32)

    @pl.when(t >= n_steps)
    def _values():
        off = pl.multiple_of((t - n_steps) * width, width)
        for g in range(N_KV_HEADS):
            rs = slice(g * rg, (g + 1) * rg)
            acc_ref[rs, :] += jnp.dot(p_ref[rs, pl.ds(off, width)], head_rows(g),
                                      preferred_element_type=F32)

    @pl.when(t == 2 * n_steps - 1)
    def _finish():
        for g in range(N_KV_HEADS):
            rs = slice(g * rg, (g + 1) * rg)
            vg = vnew_ref[0, :, g * HEAD_DIM:(g + 1) * HEAD_DIM]
            acc_ref[rs, :] += jnp.dot(p_ref[rs, past:past + PAGE_SIZE], vg, preferred_element_type=F32)
        out_ref[0] = acc_ref[...] / jnp.sum(l_ref[...], axis=1, keepdims=True)


def _sample_operands(qT, qiT, wT, kibf, kbf, v32, cache_kidx, n_seq, n_tok):
    def rows_ht(xT, n_heads, dim):
        x = xT.reshape(n_heads, dim, n_seq, n_tok).transpose(2, 0, 3, 1)
        x = jnp.pad(x, ((0, 0), (0, 0), (0, 8 - n_tok), (0, 0)))
        return x.reshape(n_seq, n_heads * 8, dim)

    qi_rows = rows_ht(qiT, N_IDX_HEADS, IDX_DIM)
    q_rows = rows_ht(qT, N_HEADS, HEAD_DIM)
    wcol = jnp.pad(wT.reshape(N_IDX_HEADS, n_seq, n_tok).transpose(1, 0, 2),
                   ((0, 0), (0, 0), (0, 8 - n_tok))).reshape(n_seq, N_IDX_HEADS * 8, 1)

    def pad_keys(x):
        x = x.reshape(n_seq, n_tok, x.shape[-1])
        return jnp.pad(x, ((0, 0), (0, PAGE_SIZE - n_tok), (0, 0)))

    ki_new_t = jnp.swapaxes(pad_keys(kibf[:, 0:IDX_DIM]), 1, 2)
    kidx_t = jnp.swapaxes(cache_kidx, 1, 2)
    k_new = pad_keys(kbf)
    v_new = pad_keys(v32.reshape(n_seq * n_tok, N_KV).astype(BF16))
    return (qi_rows, wcol, ki_new_t, kidx_t), (q_rows, k_new, v_new)


def _sample_topk(sp, sn, n_tok):
    n_seq, _, past = sp.shape
    rows = n_seq * 8
    bp, bn = pl.pallas_call(
        functools.partial(_s_thresh_body, n_tok),
        grid=(1,),
        in_specs=[pl.BlockSpec((rows, past), lambda i: (0, 0)),
                  pl.BlockSpec((rows, PAGE_SIZE), lambda i: (0, 0))],
        out_specs=[pl.BlockSpec((rows, past), lambda i: (0, 0)),
                   pl.BlockSpec((rows, PAGE_SIZE), lambda i: (0, 0))],
        out_shape=[jax.ShapeDtypeStruct((rows, past), F32),
                   jax.ShapeDtypeStruct((rows, PAGE_SIZE), F32)],
        compiler_params=_cparams(1),
        name="sample_topk_mask",
    )(sp.reshape(rows, past), sn.reshape(rows, PAGE_SIZE))
    return bp.reshape(n_seq, 8, past), bn.reshape(n_seq, 8, PAGE_SIZE)


def _sample_rows_to_tokens(out, n_seq, n_tok):
    out = out.reshape(n_seq, N_HEADS, 8, HEAD_DIM)[:, :, 0:n_tok]
    return out.transpose(0, 2, 1, 3).reshape(n_seq * n_tok, D_ATTN)


def _outproj_body(gated, *refs):
    if gated:
        x_ref, a_ref, gate_ref, mc_ref, wo_ref, o_ref = refs
        ma = (a_ref[...] * gate_ref[...]).astype(BF16)
    else:
        x_ref, a_ref, mc_ref, wo_ref, o_ref = refs
        ma = a_ref[...]
    acc = jnp.dot(ma, wo_ref[0:D_ATTN, :], preferred_element_type=F32)
    acc = acc + jnp.dot(mc_ref[...], wo_ref[D_ATTN:D_ATTN + D_CONV, :], preferred_element_type=F32)
    o_ref[...] = x_ref[...] + acc


def _outproj(x, attn, gate, mixc, wo, tm):
    m = x.shape[0]
    gated = gate is not None
    row = lambda i: (i, 0)
    in_specs = [pl.BlockSpec((tm, D_MODEL), row), pl.BlockSpec((tm, D_ATTN), row)]
    args = [x, attn]
    if gated:
        in_specs.append(pl.BlockSpec((tm, D_ATTN), row))
        args.append(gate)
    in_specs += [pl.BlockSpec((tm, D_CONV), row),
                 pl.BlockSpec((D_ATTN + D_CONV, D_MODEL), lambda i: (0, 0))]
    args += [mixc, wo]
    return pl.pallas_call(
        functools.partial(_outproj_body, gated),
        grid=(m // tm,),
        in_specs=in_specs,
        out_specs=pl.BlockSpec((tm, D_MODEL), row),
        out_shape=jax.ShapeDtypeStruct((m, D_MODEL), F32),
        compiler_params=_cparams(1),
        name="outproj_sample" if gated else "outproj_prompt",
    )(*args)


def _rope_tables(pos):
    posf = pos.astype(F32)[:, None]
    n = pos.shape[0]

    def cs(half):
        inv = ROPE_THETA ** (-jnp.arange(half, dtype=F32) / half)
        ang = posf * inv[None, :]
        return jnp.cos(ang), jnp.sin(ang)

    c16, s16 = cs(ROPE_HALF)
    c8, s8 = cs(IDX_ROPE_HALF)
    one = lambda w: jnp.ones((n, w), F32)
    zero = lambda w: jnp.zeros((n, w), F32)
    rest = HEAD_DIM - ROPE_DIM
    k_c = jnp.concatenate([c16, c16, one(rest)], axis=1)
    k_sa = jnp.concatenate([-s16, zero(HEAD_DIM - ROPE_HALF)], axis=1)
    k_sb = jnp.concatenate([zero(ROPE_HALF), s16, zero(rest)], axis=1)
    return dict(c16T=c16.T, s16T=s16.T, c8T=c8.T, s8T=s8.T, kC=k_c, kSa=k_sa, kSb=k_sb)


def _prep_weights(w_in, g_q, g_k, g_kidx, w_conv, w_out):
    assert w_in.shape == (D_MODEL, D_IN)
    wT = w_in.T.astype(BF16)
    gq = g_q.reshape(HEAD_DIM, 1)
    gk = g_k.reshape(1, HEAD_DIM)
    gki = g_kidx.reshape(IDX_DIM, 1)
    return wT, (gq, gk, gki, w_conv), w_out.astype(BF16)


def kernel(x_prompt, x_sample, cache_k, cache_v, cache_kidx, state_conv, page_table,
           norm_in, w_in, g_q, g_k, g_kidx, w_conv, w_out):
    n_b, seq, _ = x_prompt.shape
    n_s, n_t, _ = x_sample.shape
    depth = w_in.shape[0]
    past = page_table.shape[1] * PAGE_SIZE
    tabs_p = _rope_tables(jnp.arange(seq))
    tabs_s = _rope_tables(jnp.tile(past + jnp.arange(n_t), n_s))

    hp = x_prompt.reshape(n_b * seq, D_MODEL)
    hs = x_sample.reshape(n_s * n_t, D_MODEL)
    outs = [[] for _ in range(8)]
    for l in range(depth):
        w_t, params, wo = _prep_weights(w_in[l], g_q[l], g_k[l], g_kidx[l], w_conv[l], w_out[l])

        st = state_conv[l]
        tok = jnp.arange(n_t)
        e1 = st[:, jnp.full((n_t,), CONV_W - 2)].reshape(n_s * n_t, D_CONV)
        e2 = st[:, jnp.minimum(tok, CONV_W - 2)].reshape(n_s * n_t, D_CONV)
        (qT_s, qiT_s, _, wT_s, k32_s, kbf_s, v32_s, ki32_s, kibf_s, gate_s, mixc_s, u) = _project(
            hs, norm_in[l], w_t, params, tabs_s, n_t, state_rows=(e1, e2))
        idx_ops, (q_rows, k_new, v_new) = _sample_operands(
            qT_s, qiT_s, wT_s, kibf_s, kbf_s, v32_s, cache_kidx[l], n_s, n_t)
        (qT, qiT, vT, wT, k32, kbf, v32, ki32, kibf, gate, mixc, utail, sp, sn) = _project(
            hp, norm_in[l], w_t, params, tabs_p, 0, side=(page_table,) + idx_ops)

        bp, bn = _sample_topk(sp, sn, n_t)
        mixa, attn_rows = _attention(qiT, wT, kibf, qT, kbf, vT, gate, n_b, seq,
                                     q_rows, bp, bn, k_new, v_new, cache_k[l], cache_v[l], page_table)
        attn_s = _sample_rows_to_tokens(attn_rows, n_s, n_t)

        hp = _outproj(hp, mixa, None, mixc, wo, ROW_TM)
        hs = _outproj(hs, attn_s, gate_s, mixc_s, wo, n_s * n_t)
        tps = seq // PROJ_TM
        outs[0].append(k32.reshape(n_b, seq, N_KV_HEADS, HEAD_DIM))
        outs[1].append(v32.reshape(n_b, seq, N_KV_HEADS, HEAD_DIM))
        outs[2].append(ki32.reshape(n_b, seq, IDX_DIM))
        outs[3].append(utail[tps - 1::tps])
        outs[4].append(k32_s.reshape(n_s, n_t, N_KV_HEADS, HEAD_DIM))
        outs[5].append(v32_s.reshape(n_s, n_t, N_KV_HEADS, HEAD_DIM))
        outs[6].append(ki32_s.reshape(n_s, n_t, IDX_DIM))
        outs[7].append(u.reshape(n_s, n_t, D_CONV)[:, n_t - (CONV_W - 1):])

    return (hp.reshape(n_b, seq, D_MODEL), hs.reshape(n_s, n_t, D_MODEL),
            *[jnp.stack(o) for o in outs])
```

```python
import functools

import jax
import jax.numpy as jnp
import numpy as np
from jax import lax
from jax.experimental import pallas as pl
from jax.experimental.pallas import tpu as pltpu

F32 = jnp.float32
BF16 = jnp.bfloat16
I32 = jnp.int32

D_MODEL = 2048
HEAD_DIM = 128
N_HEADS = 8
N_KV_HEADS = 2
GROUP = N_HEADS // N_KV_HEADS
D_ATTN = N_HEADS * HEAD_DIM
D_CONV = 1024
ROPE_DIM = HEAD_DIM // 4
ROPE_HALF = ROPE_DIM // 2
ROPE_THETA = 500000.0
N_IDX_HEADS = 16
IDX_DIM = 64
IDX_ROPE_HALF = IDX_DIM // 8
TOPK_MAX = 256
CONV_W = 3
PAGE_SIZE = 128
EPS = 1e-6
W_IDX_SCALE = (N_IDX_HEADS ** -0.5) * (IDX_DIM ** -0.5)
ATTN_SCALE = HEAD_DIM ** -0.5
Q_PRESCALE = ATTN_SCALE * 1.4426950408889634

INT_MIN = -(2 ** 31)
NEG = -1e30

VMEM_LIMIT_BYTES = 60 * 1024 * 1024

PROJ_TN = 512
PROJ_TM = 1024
PROJ_SUB = 256
ROW_TM = 512
N_KV = N_KV_HEADS * HEAD_DIM
OFF_Q = 0
OFF_K = OFF_Q + D_ATTN
OFF_V = OFF_K + N_KV
OFF_Z = OFF_V + N_KV
OFF_QI = OFF_Z + D_ATTN
OFF_KI = OFF_QI + N_IDX_HEADS * IDX_DIM
OFF_WI = OFF_KI + IDX_DIM
OFF_H = OFF_WI + N_IDX_HEADS
OFF_B = OFF_H + D_CONV
OFF_C = OFF_B + D_CONV
OFF_ZC = OFF_C + D_CONV
D_IN = OFF_ZC + D_CONV
assert OFF_K == 2 * PROJ_TN and OFF_Z == 3 * PROJ_TN and OFF_QI == 5 * PROJ_TN and OFF_KI == 7 * PROJ_TN
KIW_ROWS = 128
assert OFF_KI % KIW_ROWS == 0 and OFF_WI - OFF_KI == IDX_DIM
CONV_CHUNK = 256
N_CONV_CHUNKS = D_CONV // CONV_CHUNK
J_Q = 0
J_KV = 2
J_Z = 3
J_QI = 5
J_CONV = 7
N_MAIN_BLOCKS = J_CONV
N_STEPS = J_CONV + N_CONV_CHUNKS

ATT_TQ = 256
ATT_TK = 256
ATT_SUB = 64
CNT_ACCS = 4

SIDE_PAGES = 16


def _cparams(n_axes):
    return pltpu.CompilerParams(
        dimension_semantics=("arbitrary",) * n_axes,
        vmem_limit_bytes=VMEM_LIMIT_BYTES,
    )


def _silu(x):
    return x * jax.nn.sigmoid(x)


def _indexer_rows_scores(qi, wcol, keys_t):
    d = jnp.dot(qi, keys_t, preferred_element_type=F32)
    val = jnp.maximum(d, 0.0) * wcol
    return val.reshape(N_IDX_HEADS, 8, val.shape[-1]).sum(axis=0)


def _side_scores_step(step, spq, pt_ref, qi_ref, wcol_ref, kinew_ref, kidx_hbm, sp_ref, sn_ref, kbuf, ksem):
    npg = kbuf.shape[1]
    n_side = pt_ref.shape[0] * spq

    def start_pages(n):
        seq_n = n // spq
        base = (n % spq) * npg
        for r in range(npg):
            _page_copy(kidx_hbm, pt_ref[seq_n, base + r], kbuf, n % 2, r, ksem).start()

    @pl.when(step == 0)
    def _prologue():
        start_pages(step)

    @pl.when(step + 1 < n_side)
    def _prefetch():
        start_pages(step + 1)

    @pl.when(step < n_side)
    def _scores():
        slot = step % 2
        for r in range(npg):
            _page_copy(kidx_hbm, 0, kbuf, slot, r, ksem).wait()
        qi = qi_ref[0]
        wcol = wcol_ref[0]
        keys_t = jnp.concatenate([kbuf[slot, r] for r in range(npg)], axis=1).astype(BF16)
        sp_ref[0] = _indexer_rows_scores(qi, wcol, keys_t)

        @pl.when(step % spq == 0)
        def _new():
            sn_ref[0] = _indexer_rows_scores(qi, wcol, kinew_ref[0])


def _proj_body(sample, tps, tm, side_spq, *refs):
    if side_spq:
        pt_ref, refs = refs[0], refs[1:]
    (x_ref, gin_ref, wa_ref, wkiw_ref, wh_ref, wb_ref, wc_ref, wzc_ref, gq_ref, gk_ref, gki_ref,
     c16_ref, s16_ref, c8_ref, s8_ref,
     kc_ref, ksa_ref, ksb_ref, wconv_ref) = refs[:19]
    refs = refs[19:]
    if sample:
        e1_ref, e2_ref = refs[:2]
        refs = refs[2:]
    if side_spq:
        side_in, refs = refs[:4], refs[4:]
    (qT_ref, qiT_ref, vT_ref, wT_ref, k32_ref, kbf_ref, v32_ref, ki32_ref, kibf_ref,
     gate_ref, mixc_ref, u_ref) = refs[:12]
    refs = refs[12:]
    if side_spq:
        side_out, refs = refs[:2], refs[2:]
    xn_ref = refs[0]
    if not sample:
        carry_ref = refs[1]

    i = pl.program_id(0)
    j = pl.program_id(1)
    nt = (((1,), (1,)), ((), ()))
    if side_spq:
        _side_scores_step(i * N_STEPS + j, side_spq, pt_ref, *side_in, *side_out, *refs[2:4])

    def xw(w, x=None):
        return lax.dot_general(xn_ref[...] if x is None else x, w, nt, preferred_element_type=F32)

    def wx(w, x):
        return lax.dot_general(w, x, nt, preferred_element_type=F32)

    nsub = max(tm // PROJ_SUB, 1)
    sub = tm // nsub
    pieces = [slice(rb * sub, (rb + 1) * sub) for rb in range(nsub)]

    if not sample:
        @pl.when((i == 0) & (j == 0))
        def _init():
            carry_ref[...] = jnp.zeros(carry_ref.shape, F32)

    def q_step(with_norm):
        for cs in pieces:
            if with_norm:
                x = x_ref[cs, :]
                ms = jnp.mean(x * x, axis=-1, keepdims=True)
                xn_ref[cs, :] = (x * lax.rsqrt(ms + EPS) * gin_ref[...]).astype(BF16)
            res = wx(wa_ref[...], xn_ref[cs, :])
            cos = c16_ref[:, cs]
            sin = s16_ref[:, cs]
            for hh in range(PROJ_TN // HEAD_DIM):
                blk = res[hh * HEAD_DIM:(hh + 1) * HEAD_DIM]
                ms = jnp.mean(blk * blk, axis=0, keepdims=True)
                y = blk * lax.rsqrt(ms + EPS) * gq_ref[...]
                x1 = y[0:ROPE_HALF]
                x2 = y[ROPE_HALF:ROPE_DIM]
                base = hh * HEAD_DIM
                qT_ref[base:base + ROPE_HALF, cs] = ((x1 * cos - x2 * sin) * Q_PRESCALE).astype(BF16)
                qT_ref[base + ROPE_HALF:base + ROPE_DIM, cs] = ((x2 * cos + x1 * sin) * Q_PRESCALE).astype(BF16)
                qT_ref[base + ROPE_DIM:base + HEAD_DIM, cs] = (y[ROPE_DIM:] * Q_PRESCALE).astype(BF16)

    pl.when(j == J_Q)(functools.partial(q_step, True))
    pl.when((j > J_Q) & (j < J_KV))(functools.partial(q_step, False))

    @pl.when((j >= J_QI) & (j < J_CONV))
    def _qi():
        for cs in pieces:
            res = wx(wa_ref[...], xn_ref[cs, :])
            cos = c8_ref[:, cs]
            sin = s8_ref[:, cs]
            for hh in range(PROJ_TN // IDX_DIM):
                blk = res[hh * IDX_DIM:(hh + 1) * IDX_DIM]
                x1 = blk[0:IDX_ROPE_HALF]
                x2 = blk[IDX_ROPE_HALF:2 * IDX_ROPE_HALF]
                rot = jnp.concatenate([x1 * cos - x2 * sin, x2 * cos + x1 * sin], axis=0)
                base = hh * IDX_DIM
                qiT_ref[base:base + 2 * IDX_ROPE_HALF, cs] = rot.astype(BF16)
                qiT_ref[base + 2 * IDX_ROPE_HALF:base + IDX_DIM, cs] = blk[2 * IDX_ROPE_HALF:].astype(BF16)

    @pl.when(j == J_KV)
    def _kv():
        for rb, cs in enumerate(pieces):
            x = xn_ref[cs, :]
            res = xw(wa_ref[...], x)
            for hd in range(N_KV_HEADS):
                hs = slice(hd * HEAD_DIM, (hd + 1) * HEAD_DIM)
                blk = res[:, hs]
                ms = jnp.mean(blk * blk, axis=-1, keepdims=True)
                yk = blk * lax.rsqrt(ms + EPS) * gk_ref[...]
                rot = yk * kc_ref[cs, :] + (pltpu.roll(yk, HEAD_DIM - ROPE_HALF, axis=1) * ksa_ref[cs, :]
                                            + pltpu.roll(yk, ROPE_HALF, axis=1) * ksb_ref[cs, :])
                k32_ref[pl.ds(N_KV_HEADS * rb * sub + hd, sub, stride=N_KV_HEADS), :] = rot
                kbf_ref[cs, hs] = rot.astype(BF16)
            v = res[:, N_KV:2 * N_KV]
            for hd in range(N_KV_HEADS):
                v32_ref[pl.ds(N_KV_HEADS * rb * sub + hd, sub, stride=N_KV_HEADS), :] = (
                    v[:, hd * HEAD_DIM:(hd + 1) * HEAD_DIM])
            if sample:
                vT_ref[:, cs] = wx(wa_ref[N_KV:2 * N_KV, :], x).astype(BF16)
            else:
                vT_ref[:, cs] = v.T.astype(BF16)
            r2t = wx(wkiw_ref[...], x)
            wT_ref[:, cs] = r2t[IDX_DIM:IDX_DIM + N_IDX_HEADS] * W_IDX_SCALE
            kit = r2t[0:IDX_DIM]
            ms = jnp.mean(kit * kit, axis=0, keepdims=True)
            yi = kit * lax.rsqrt(ms + EPS) * gki_ref[...]
            x1 = yi[0:IDX_ROPE_HALF]
            x2 = yi[IDX_ROPE_HALF:2 * IDX_ROPE_HALF]
            cos = c8_ref[:, cs]
            sin = s8_ref[:, cs]
            roti = jnp.concatenate([x1 * cos - x2 * sin, x2 * cos + x1 * sin, yi[2 * IDX_ROPE_HALF:],
                                    jnp.zeros((KIW_ROWS - IDX_DIM, sub), F32)], axis=0)
            ki_nat = roti.T
            ki32_ref[cs, :] = ki_nat[:, 0:IDX_DIM]
            kibf_ref[cs, :] = ki_nat.astype(BF16)

    @pl.when((j >= J_Z) & (j < J_QI))
    def _z():
        for cs in pieces:
            gate_ref[cs, :] = _silu(xw(wa_ref[...], xn_ref[cs, :]))

    @pl.when(j >= J_CONV)
    def _conv():
        cc = j - J_CONV
        u = xw(wc_ref[...]) * xw(wh_ref[...])
        rowid = lax.broadcasted_iota(I32, (tm, CONV_CHUNK), 0)
        if sample:
            t = rowid & (sample - 1)
            u1 = jnp.where(t >= 1, pltpu.roll(u, 1, axis=0), e1_ref[...])
            u2 = jnp.where(t >= 2, pltpu.roll(u, 2, axis=0), e2_ref[...])
            u_ref[...] = u
        else:
            first = (i % tps) == 0
            prev = carry_ref[cc]
            p0 = jnp.where(first, 0.0, prev[0:1])
            p1 = jnp.where(first, 0.0, prev[1:2])
            u1 = jnp.where(rowid == 0, p1, pltpu.roll(u, 1, axis=0))
            u2 = jnp.where(rowid == 0, p0, jnp.where(rowid == 1, p1, pltpu.roll(u, 2, axis=0)))
            tail = u[tm - 8:tm]
            carry_ref[cc] = jnp.concatenate([tail[6:8], tail[0:6]], axis=0)
            u_ref[0] = tail[6:8]
        w = wconv_ref[...]
        y = u2 * w[0:1] + u1 * w[1:2] + u * w[2:3]
        mixc_ref[...] = (xw(wb_ref[...]) * y * _silu(xw(wzc_ref[...]))).astype(BF16)


def _project(x, norm_g, wT, params, tabs, sample, state_rows=None, side=None):
    m = x.shape[0]
    assert sample & (sample - 1) == 0
    tm = m if sample else PROJ_TM
    n_i = m // tm
    tps = 1 if sample else (tabs["c16T"].shape[1] // tm)
    gq, gk, gki, wconv = params

    def tmap(i):
        return i % tps

    def cchunk(j):
        return jnp.clip(j - J_CONV, 0, N_CONV_CHUNKS - 1)

    def conv_rows(off):
        def imap(i, j, *_):
            chunk = jnp.where(j < J_QI, N_CONV_CHUNKS - 1, cchunk(j))
            return (pl.multiple_of(off + CONV_CHUNK * chunk, 16), 0)
        return pl.BlockSpec((pl.Element(CONV_CHUNK), pl.Element(D_MODEL)), imap)

    in_specs = [
        pl.BlockSpec((tm, D_MODEL), lambda i, j, *_: (i, 0)),
        pl.BlockSpec((1, D_MODEL), lambda i, j, *_: (0, 0)),
        pl.BlockSpec((PROJ_TN, D_MODEL), lambda i, j, *_: (jnp.minimum(j, N_MAIN_BLOCKS - 1), 0)),
        pl.BlockSpec((KIW_ROWS, D_MODEL), lambda i, j, *_: (OFF_KI // KIW_ROWS, 0)),
        conv_rows(OFF_H), conv_rows(OFF_B), conv_rows(OFF_C), conv_rows(OFF_ZC),
        pl.BlockSpec((HEAD_DIM, 1), lambda i, j, *_: (0, 0)),
        pl.BlockSpec((1, HEAD_DIM), lambda i, j, *_: (0, 0)),
        pl.BlockSpec((IDX_DIM, 1), lambda i, j, *_: (0, 0)),
        pl.BlockSpec((ROPE_HALF, tm), lambda i, j, *_: (0, tmap(i))),
        pl.BlockSpec((ROPE_HALF, tm), lambda i, j, *_: (0, tmap(i))),
        pl.BlockSpec((IDX_ROPE_HALF, tm), lambda i, j, *_: (0, tmap(i))),
        pl.BlockSpec((IDX_ROPE_HALF, tm), lambda i, j, *_: (0, tmap(i))),
        pl.BlockSpec((tm, HEAD_DIM), lambda i, j, *_: (tmap(i), 0)),
        pl.BlockSpec((tm, HEAD_DIM), lambda i, j, *_: (tmap(i), 0)),
        pl.BlockSpec((tm, HEAD_DIM), lambda i, j, *_: (tmap(i), 0)),
        pl.BlockSpec((CONV_W, CONV_CHUNK), lambda i, j, *_: (0, cchunk(j))),
    ]
    args = [x, norm_g.reshape(1, D_MODEL), wT, wT, wT, wT, wT, wT, gq, gk, gki,
            tabs["c16T"], tabs["s16T"], tabs["c8T"], tabs["s8T"],
            tabs["kC"], tabs["kSa"], tabs["kSb"], wconv]
    cmap = lambda i, j, *_: (i, cchunk(j))
    if sample:
        in_specs += [pl.BlockSpec((tm, CONV_CHUNK), cmap), pl.BlockSpec((tm, CONV_CHUNK), cmap)]
        args += list(state_rows)
        u_spec = pl.BlockSpec((tm, CONV_CHUNK), cmap)
        u_shape = jax.ShapeDtypeStruct((m, D_CONV), F32)
        scratch = [pltpu.VMEM((tm, D_MODEL), BF16)]
    else:
        u_spec = pl.BlockSpec((1, CONV_W - 1, CONV_CHUNK), lambda i, j, *_: (i, 0, cchunk(j)))
        u_shape = jax.ShapeDtypeStruct((n_i, CONV_W - 1, D_CONV), F32)
        scratch = [pltpu.VMEM((tm, D_MODEL), BF16), pltpu.VMEM((N_CONV_CHUNKS, 8, CONV_CHUNK), F32)]

    out_specs = [
        pl.BlockSpec((PROJ_TN, tm), lambda i, j, *_: (jnp.minimum(j, 1), i)),
        pl.BlockSpec((PROJ_TN, tm), lambda i, j, *_: (jnp.clip(j - J_QI, 0, 1), i)),
        pl.BlockSpec((N_KV, tm), lambda i, j, *_: (0, i)),
        pl.BlockSpec((N_IDX_HEADS, tm), lambda i, j, *_: (0, i)),
        pl.BlockSpec((N_KV_HEADS * tm, HEAD_DIM), lambda i, j, *_: (i, 0)),
        pl.BlockSpec((tm, N_KV), lambda i, j, *_: (i, 0)),
        pl.BlockSpec((N_KV_HEADS * tm, HEAD_DIM), lambda i, j, *_: (i, 0)),
        pl.BlockSpec((tm, IDX_DIM), lambda i, j, *_: (i, 0)),
        pl.BlockSpec((tm, KIW_ROWS), lambda i, j, *_: (i, 0)),
        pl.BlockSpec((tm, PROJ_TN), lambda i, j, *_: (i, jnp.clip(j - J_Z, 0, 1))),
        pl.BlockSpec((tm, CONV_CHUNK), cmap),
        u_spec,
    ]
    out_shape = [
        jax.ShapeDtypeStruct((D_ATTN, m), BF16),
        jax.ShapeDtypeStruct((N_IDX_HEADS * IDX_DIM, m), BF16),
        jax.ShapeDtypeStruct((N_KV, m), BF16),
        jax.ShapeDtypeStruct((N_IDX_HEADS, m), F32),
        jax.ShapeDtypeStruct((N_KV_HEADS * m, HEAD_DIM), F32),
        jax.ShapeDtypeStruct((m, N_KV), BF16),
        jax.ShapeDtypeStruct((N_KV_HEADS * m, HEAD_DIM), F32),
        jax.ShapeDtypeStruct((m, IDX_DIM), F32),
        jax.ShapeDtypeStruct((m, KIW_ROWS), BF16),
        jax.ShapeDtypeStruct((m, D_ATTN), F32),
        jax.ShapeDtypeStruct((m, D_CONV), BF16),
        u_shape,
    ]
    if side is None:
        return pl.pallas_call(
            functools.partial(_proj_body, sample, tps, tm, 0),
            grid=(n_i, N_STEPS),
            in_specs=in_specs,
            out_specs=out_specs,
            out_shape=out_shape,
            scratch_shapes=scratch,
            compiler_params=_cparams(2),
            name="proj_sample" if sample else "proj_prompt",
        )(*args)

    page_table, qi_rows, wcol, ki_new_t, kidx_t = side
    n_seq, n_pages = page_table.shape
    spq = n_pages // SIDE_PAGES
    n_side = n_seq * spq
    assert spq * SIDE_PAGES == n_pages and n_side <= n_i * N_STEPS
    width = SIDE_PAGES * PAGE_SIZE

    def side_step(i, j):
        return jnp.minimum(i * N_STEPS + j, n_side - 1)

    seq_map = lambda i, j, *_: (side_step(i, j) // spq, 0, 0)
    in_specs += [
        pl.BlockSpec((1, N_IDX_HEADS * 8, IDX_DIM), seq_map),
        pl.BlockSpec((1, N_IDX_HEADS * 8, 1), seq_map),
        pl.BlockSpec((1, IDX_DIM, PAGE_SIZE), seq_map),
        pl.BlockSpec(memory_space=pl.ANY),
    ]
    out_specs += [
        pl.BlockSpec((1, 8, width), lambda i, j, *_: (side_step(i, j) // spq, 0, side_step(i, j) % spq)),
        pl.BlockSpec((1, 8, PAGE_SIZE), seq_map),
    ]
    out_shape += [jax.ShapeDtypeStruct((n_seq, 8, n_pages * PAGE_SIZE), F32),
                  jax.ShapeDtypeStruct((n_seq, 8, PAGE_SIZE), F32)]
    scratch += [pltpu.VMEM((2, SIDE_PAGES, IDX_DIM, PAGE_SIZE), F32), pltpu.SemaphoreType.DMA((2,))]
    return pl.pallas_call(
        functools.partial(_proj_body, sample, tps, tm, spq),
        grid_spec=pltpu.PrefetchScalarGridSpec(
            num_scalar_prefetch=1,
            grid=(n_i, N_STEPS),
            in_specs=in_specs,
            out_specs=out_specs,
            scratch_shapes=scratch,
        ),
        out_shape=out_shape,
        compiler_params=_cparams(2),
        name="proj_prompt",
    )(page_table, *args, qi_rows, wcol, ki_new_t, kidx_t)


KEY_LOWEST_FINITE = INT_MIN + 0x00800000


def _key_to_float(key):
    return pltpu.bitcast(key ^ ((key >> 31) & 0x7FFFFFFF), F32)


def _bit_value(b):
    return lax.shift_left(jnp.int32(1), jnp.int32(31) - b)


def _prompt_attn_step(i, qiT_ref, wT_ref, ki_ref, qT_ref, k_ref, vT_ref, gate_ref, o_ref,
                      sc_ref, bias_ref, acc_ref, s_ref):
    tq, tk = ATT_TQ, ATT_TK
    nch = i + 1
    w = wT_ref[...]
    row = lax.broadcasted_iota(I32, (tk, tq), 0)
    col = lax.broadcasted_iota(I32, (tk, tq), 1)
    row8 = lax.broadcasted_iota(I32, (8, tq), 0)

    def score_chunk(j, carry):
        off = pl.multiple_of(j * tk, tk)
        kic = ki_ref[pl.ds(off, tk), 0:IDX_DIM]
        acc = jnp.zeros((tk, tq), F32)
        for h in range(N_IDX_HEADS):
            d = jnp.dot(kic, qiT_ref[h * IDX_DIM:(h + 1) * IDX_DIM, :], preferred_element_type=F32)
            acc = acc + w[h:h + 1, :] * jnp.maximum(d, 0.0)
        future = (row + j * tk) > (col + i * tq)
        sc_ref[pl.ds(off, tk), :] = jnp.where(future, -jnp.inf, acc)
        return carry

    lax.fori_loop(0, nch, score_chunk, 0)

    def count_keys(pred):
        def cnt_chunk(j, cs):
            off = pl.multiple_of(j * tk, tk)
            cs = list(cs)
            sc = sc_ref[pl.ds(off, tk), :]
            for r in range(tk // 8):
                a = cs[r % CNT_ACCS]
                kpos = row8 + (j * tk + r * 8)
                cs[r % CNT_ACCS] = jnp.where(pred(sc[r * 8:(r + 1) * 8], kpos), a + 1, a)
            return tuple(cs)

        cs = lax.fori_loop(0, nch, cnt_chunk, (jnp.zeros((8, tq), I32),) * CNT_ACCS)
        c = cs[0]
        for a in cs[1:]:
            c = c + a
        return jnp.sum(c.astype(F32), axis=0, keepdims=True)

    def bit_body(b, carry):
        thr, cge = carry
        cand = thr + _bit_value(b)
        cand_f = _key_to_float(cand)
        cnt = count_keys(lambda s, kpos: s >= cand_f)
        ok = cnt >= float(TOPK_MAX)
        return jnp.where(ok, cand, thr), jnp.where(ok, cnt, cge)

    thr, cge = lax.fori_loop(0, 32, bit_body, (jnp.full((1, tq), INT_MIN, I32),
                                               jnp.zeros((1, tq), F32)))
    thr_f = _key_to_float(jnp.maximum(thr, KEY_LOWEST_FINITE))
    has_ties = jnp.max(cge) > float(TOPK_MAX)

    @pl.when(jnp.logical_not(has_ties))
    def _plain_mask():
        def bias_chunk(j, carry):
            off = pl.multiple_of(j * tk, tk)
            bias_ref[pl.ds(off, tk), :] = jnp.where(sc_ref[pl.ds(off, tk), :] >= thr_f, 0.0, NEG)
            return carry

        lax.fori_loop(0, nch, bias_chunk, 0)

    @pl.when(has_ties)
    def _tie_mask():
        need = float(TOPK_MAX) - count_keys(lambda s, kpos: s > thr_f)
        nbits = (sc_ref.shape[0] - 1).bit_length()

        def pos_bit(b, last):
            step = lax.shift_left(jnp.int32(1), jnp.int32(nbits - 1) - b)
            probe = last + (step - 1)
            got = count_keys(lambda s, kpos: (s == thr_f) & (kpos <= probe))
            return jnp.where(got < need, last + step, last)

        last = lax.fori_loop(0, nbits, pos_bit, jnp.zeros((1, tq), I32))

        def bias_chunk(j, carry):
            off = pl.multiple_of(j * tk, tk)
            sc = sc_ref[pl.ds(off, tk), :]
            keep = (sc > thr_f) | ((sc == thr_f) & ((row + j * tk) <= last))
            bias_ref[pl.ds(off, tk), :] = jnp.where(keep, 0.0, NEG)
            return carry

        lax.fori_loop(0, nch, bias_chunk, 0)

    def qk_chunk(j, mrun):
        off = pl.multiple_of(j * tk, tk)
        bias = bias_ref[pl.ds(off, tk), :]
        out = []
        for h in range(N_HEADS):
            g = h // GROUP
            kc = k_ref[pl.ds(off, tk), g * HEAD_DIM:(g + 1) * HEAD_DIM]
            s = jnp.dot(kc, qT_ref[h * HEAD_DIM:(h + 1) * HEAD_DIM, :], preferred_element_type=F32) + bias
            s_ref[h, pl.ds(off, tk), :] = s
            out.append(jnp.maximum(mrun[h], s.reshape(tk // 8, 8, tq).max(axis=0)))
        return tuple(out)

    mrun = lax.fori_loop(0, nch, qk_chunk, (jnp.full((8, tq), NEG, F32),) * N_HEADS)
    ms = [jnp.max(mr, axis=0, keepdims=True) for mr in mrun]
    acc_ref[...] = jnp.zeros(acc_ref.shape, F32)

    def pv_chunk(j, lrun):
        off = pl.multiple_of(j * tk, tk)
        out = []
        for h in range(N_HEADS):
            g = h // GROUP
            hs = slice(h * HEAD_DIM, (h + 1) * HEAD_DIM)
            p = jnp.exp2(s_ref[h, pl.ds(off, tk), :] - ms[h])
            out.append(lrun[h] + p.reshape(tk // 8, 8, tq).sum(axis=0))
            vc = vT_ref[g * HEAD_DIM:(g + 1) * HEAD_DIM, pl.ds(off, tk)]
            acc_ref[hs, :] += jnp.dot(vc, p.astype(BF16), preferred_element_type=F32)
        return tuple(out)

    lrun = lax.fori_loop(0, nch, pv_chunk, (jnp.zeros((8, tq), F32),) * N_HEADS)

    for h in range(N_HEADS):
        hs = slice(h * HEAD_DIM, (h + 1) * HEAD_DIM)
        o = (acc_ref[hs, :] / jnp.sum(lrun[h], axis=0, keepdims=True)).T
        o_ref[:, hs] = (o * gate_ref[:, hs]).astype(BF16)


def _page_copy(cache_hbm, page, pbuf, slot, r, sem):
    return pltpu.make_async_copy(cache_hbm.at[page], pbuf.at[slot, r], sem.at[slot])


def _attn_fused_body(nq, spq, pt_ref, *refs):
    prompt_in = refs[0:7]
    q_ref, bp_ref, bn_ref, knew_ref, vnew_ref, ck_hbm, cv_hbm = refs[7:14]
    o_ref, so_ref = refs[14:16]
    prompt_scratch = refs[16:20]
    pbuf, sem, ss_ref, ps_ref, ls_ref, accs_ref = refs[20:26]

    i = pl.program_id(1)
    step = pl.program_id(0) * nq + i
    n_total = pl.num_programs(0) * nq
    half = spq // 2
    npg = pbuf.shape[1]

    def start_pages(n):
        seq_n = n // spq
        t_n = n % spq
        base = (t_n % half) * npg
        slot = n % 2
        for cache_hbm, pred in ((ck_hbm, t_n < half), (cv_hbm, t_n >= half)):
            @pl.when(pred)
            def _(cache_hbm=cache_hbm):
                for r in range(npg):
                    _page_copy(cache_hbm, pt_ref[seq_n, base + r], pbuf, slot, r, sem).start()

    @pl.when(step == 0)
    def _prologue():
        start_pages(step)

    @pl.when(step + 1 < n_total)
    def _prefetch():
        start_pages(step + 1)

    slot = step % 2
    for r in range(npg):
        _page_copy(ck_hbm, 0, pbuf, slot, r, sem).wait()

    _sample_attn_step(step % spq, half, slot, q_ref, bp_ref, bn_ref, knew_ref, vnew_ref, pbuf,
                      so_ref, ss_ref, ps_ref, ls_ref, accs_ref)
    _prompt_attn_step(i, *prompt_in, o_ref, *prompt_scratch)


def _attention(qiT, wT, kibf, qT, kbf, vT, gate, n_batch, seq,
               q_rows, bp, bn, k_new, v_new, cache_k, cache_v, page_table):
    m = n_batch * seq
    nq = seq // ATT_TQ
    n_seq, n_pages = page_table.shape
    n_steps = n_batch * nq
    spq = n_steps // n_seq
    assert spq * n_seq == n_steps and spq % 2 == 0
    half = spq // 2
    npg = n_pages // half
    assert npg * half == n_pages
    width = npg * PAGE_SIZE
    past = n_pages * PAGE_SIZE
    n_pool = cache_k.shape[0]
    kv_rows = PAGE_SIZE * N_KV_HEADS
    ck = cache_k.reshape(n_pool, kv_rows, HEAD_DIM)
    cv = cache_v.reshape(n_pool, kv_rows, HEAD_DIM)
    nrow = N_HEADS * 8

    qmap = lambda b, i, pt: (0, b * nq + i)
    smap = lambda b, i, pt: ((b * nq + i) // spq, 0, 0)
    return pl.pallas_call(
        functools.partial(_attn_fused_body, nq, spq),
        grid_spec=pltpu.PrefetchScalarGridSpec(
            num_scalar_prefetch=1,
            grid=(n_batch, nq),
            in_specs=[
                pl.BlockSpec((N_IDX_HEADS * IDX_DIM, ATT_TQ), qmap),
                pl.BlockSpec((N_IDX_HEADS, ATT_TQ), qmap),
                pl.BlockSpec((seq, 128), lambda b, i, pt: (b, 0)),
                pl.BlockSpec((D_ATTN, ATT_TQ), qmap),
                pl.BlockSpec((seq, N_KV), lambda b, i, pt: (b, 0)),
                pl.BlockSpec((N_KV, seq), lambda b, i, pt: (0, b)),
                pl.BlockSpec((ATT_TQ, D_ATTN), lambda b, i, pt: (b * nq + i, 0)),
                pl.BlockSpec((1, nrow, HEAD_DIM), smap),
                pl.BlockSpec((1, 8, width),
                             lambda b, i, pt: ((b * nq + i) // spq, 0, jnp.minimum((b * nq + i) % spq, half - 1))),
                pl.BlockSpec((1, 8, PAGE_SIZE), smap),
                pl.BlockSpec((1, PAGE_SIZE, N_KV), smap),
                pl.BlockSpec((1, PAGE_SIZE, N_KV), smap),
                pl.BlockSpec(memory_space=pl.ANY),
                pl.BlockSpec(memory_space=pl.ANY),
            ],
            out_specs=[
                pl.BlockSpec((ATT_TQ, D_ATTN), lambda b, i, pt: (b * nq + i, 0)),
                pl.BlockSpec((1, nrow, HEAD_DIM), smap),
            ],
            scratch_shapes=[
                pltpu.VMEM((seq, ATT_TQ), F32), pltpu.VMEM((seq, ATT_TQ), F32),
                pltpu.VMEM((D_ATTN, ATT_TQ), F32), pltpu.VMEM((N_HEADS, seq, ATT_TQ), F32),
                pltpu.VMEM((2, npg, kv_rows, HEAD_DIM), F32), pltpu.SemaphoreType.DMA((2,)),
                pltpu.VMEM((nrow, past + PAGE_SIZE), F32), pltpu.VMEM((nrow, past + PAGE_SIZE), BF16),
                pltpu.VMEM((nrow, 128), F32), pltpu.VMEM((nrow, HEAD_DIM), F32),
            ],
        ),
        out_shape=[jax.ShapeDtypeStruct((m, D_ATTN), BF16),
                   jax.ShapeDtypeStruct((n_seq, nrow, HEAD_DIM), F32)],
        compiler_params=_cparams(2),
        name="attention",
    )(page_table, qiT, wT, kibf, qT, kbf, vT, gate, q_rows, bp, bn, k_new, v_new, ck, cv)


def _s_thresh_body(n_new, sp_ref, sn_ref, bp_ref, bn_ref):
    rows, past = sp_ref.shape
    ch = 2048
    nchunk = past // ch
    t = lax.broadcasted_iota(I32, (rows, 128), 0) & 7
    lane = lax.broadcasted_iota(I32, (rows, 128), 1)
    sn = jnp.where((lane < n_new) & (lane <= t), sn_ref[...], -jnp.inf)

    def fold(x):
        f = x[:, 0:128]
        for q in range(1, x.shape[1] // 128):
            f = f + x[:, q * 128:(q + 1) * 128]
        return f

    lane_ch = lax.broadcasted_iota(I32, (rows, ch), 1)

    def count_keys(pred):
        def cnt_chunk(c, acc):
            off = pl.multiple_of(c * ch, ch)
            return acc + fold(jnp.where(pred(sp_ref[:, pl.ds(off, ch)], lane_ch + c * ch), 1.0, 0.0))

        acc = lax.fori_loop(0, nchunk, cnt_chunk, jnp.where(pred(sn, lane + past), 1.0, 0.0))
        return jnp.sum(acc, axis=1, keepdims=True)

    def bit_body(b, carry):
        thr, cge = carry
        cand = thr + _bit_value(b)
        cand_f = _key_to_float(cand)
        cnt = count_keys(lambda s, kpos: s >= cand_f)
        ok = cnt >= float(TOPK_MAX)
        return jnp.where(ok, cand, thr), jnp.where(ok, cnt, cge)

    thr, cge = lax.fori_loop(0, 32, bit_body, (jnp.full((rows, 1), INT_MIN, I32),
                                               jnp.zeros((rows, 1), F32)))
    thr_f = _key_to_float(jnp.maximum(thr, KEY_LOWEST_FINITE))
    real = (lax.broadcasted_iota(I32, (rows, 1), 0) & 7) < n_new
    has_ties = jnp.max(jnp.where(real, cge, 0.0)) > float(TOPK_MAX)

    @pl.when(jnp.logical_not(has_ties))
    def _plain_mask():
        def to_bias(c, carry):
            off = pl.multiple_of(c * ch, ch)
            bp_ref[:, pl.ds(off, ch)] = jnp.where(sp_ref[:, pl.ds(off, ch)] >= thr_f, 0.0, NEG)
            return carry

        lax.fori_loop(0, nchunk, to_bias, 0)
        bn_ref[...] = jnp.where(sn >= thr_f, 0.0, NEG)

    @pl.when(has_ties)
    def _tie_mask():
        need = float(TOPK_MAX) - count_keys(lambda s, kpos: s > thr_f)
        nbits = (past + 128 - 1).bit_length()

        def pos_bit(b, last):
            step = lax.shift_left(jnp.int32(1), jnp.int32(nbits - 1) - b)
            probe = last + (step - 1)
            got = count_keys(lambda s, kpos: (s == thr_f) & (kpos <= probe))
            return jnp.where(got < need, last + step, last)

        last = lax.fori_loop(0, nbits, pos_bit, jnp.zeros((rows, 1), I32))

        def keep(s, kpos):
            return (s > thr_f) | ((s == thr_f) & (kpos <= last))

        def to_bias(c, carry):
            off = pl.multiple_of(c * ch, ch)
            bp_ref[:, pl.ds(off, ch)] = jnp.where(keep(sp_ref[:, pl.ds(off, ch)], lane_ch + c * ch), 0.0, NEG)
            return carry

        lax.fori_loop(0, nchunk, to_bias, 0)
        bn_ref[...] = jnp.where(keep(sn, lane + past), 0.0, NEG)


def _sample_attn_step(t, n_steps, slot, q_ref, bp_ref, bn_ref, knew_ref, vnew_ref, pbuf,
                      out_ref, s_ref, p_ref, l_ref, acc_ref):
    npg = pbuf.shape[1]
    nt = (((1,), (1,)), ((), ()))
    rg = GROUP * 8
    width = npg * PAGE_SIZE
    past = n_steps * width
    sm_chunk = 2048

    def head_rows(g):
        rows = [pbuf[slot, r, pl.ds(g, PAGE_SIZE, stride=N_KV_HEADS), :] for r in range(npg)]
        return jnp.concatenate(rows, axis=0).astype(BF16)

    @pl.when(t < n_steps)
    def _logits():
        off = pl.multiple_of(t * width, width)
        bias = jnp.concatenate([bp_ref[0]] * GROUP, axis=0)
        for g in range(N_KV_HEADS):
            qg = q_ref[0, g * rg:(g + 1) * rg, :]
            s_ref[g * rg:(g + 1) * rg, pl.ds(off, width)] = lax.dot_general(
                qg, head_rows(g), nt, preferred_element_type=F32) + bias

    @pl.when(t == n_steps - 1)
    def _softmax():
        biasn = jnp.concatenate([bn_ref[0]] * GROUP, axis=0)
        for g in range(N_KV_HEADS):
            qg = q_ref[0, g * rg:(g + 1) * rg, :]
            kg = knew_ref[0, :, g * HEAD_DIM:(g + 1) * HEAD_DIM]
            s_ref[g * rg:(g + 1) * rg, past:past + PAGE_SIZE] = lax.dot_general(
                qg, kg, nt, preferred_element_type=F32) + biasn

        def fold(x, op):
            f = x[:, 0:128]
            for q in range(1, x.shape[1] // 128):
                f = op(f, x[:, q * 128:(q + 1) * 128])
            return f

        def max_chunk(c, m):
            off = pl.multiple_of(c * sm_chunk, sm_chunk)
            return jnp.maximum(m, fold(s_ref[:, pl.ds(off, sm_chunk)], jnp.maximum))

        m = lax.fori_loop(0, past // sm_chunk, max_chunk, s_ref[:, past:past + PAGE_SIZE])
        m = jnp.max(m, axis=1, keepdims=True)

        def exp_chunk(c, l):
            off = pl.multiple_of(c * sm_chunk, sm_chunk)
            p = jnp.exp2(s_ref[:, pl.ds(off, sm_chunk)] - m)
            p_ref[:, pl.ds(off, sm_chunk)] = p.astype(BF16)
            return l + fold(p, jnp.add)

        pn = jnp.exp2(s_ref[:, past:past + PAGE_SIZE] - m)
        p_ref[:, past:past + PAGE_SIZE] = pn.astype(BF16)
        l_ref[...] = lax.fori_loop(0, past // sm_chunk, exp_chunk, pn)
        acc_ref[...] = jnp.zeros(acc_ref.shape, F32)

    @pl.when(t >= n_steps)
    def _values():
        off = pl.multiple_of((t - n_steps) * width, width)
        for g in range(N_KV_HEADS):
            rs = slice(g * rg, (g + 1) * rg)
            acc_ref[rs, :] += jnp.dot(p_ref[rs, pl.ds(off, width)], head_rows(g),
                                      preferred_element_type=F32)

    @pl.when(t == 2 * n_steps - 1)
    def _finish():
        for g in range(N_KV_HEADS):
            rs = slice(g * rg, (g + 1) * rg)
            vg = vnew_ref[0, :, g * HEAD_DIM:(g + 1) * HEAD_DIM]
            acc_ref[rs, :] += jnp.dot(p_ref[rs, past:past + PAGE_SIZE], vg, preferred_element_type=F32)
        out_ref[0] = acc_ref[...] / jnp.sum(l_ref[...], axis=1, keepdims=True)


def _sample_operands(qT, qiT, wT, kibf, kbf, v32, cache_kidx, n_seq, n_tok):
    def rows_ht(xT, n_heads, dim):
        x = xT.reshape(n_heads, dim, n_seq, n_tok).transpose(2, 0, 3, 1)
        x = jnp.pad(x, ((0, 0), (0, 0), (0, 8 - n_tok), (0, 0)))
        return x.reshape(n_seq, n_heads * 8, dim)

    qi_rows = rows_ht(qiT, N_IDX_HEADS, IDX_DIM)
    q_rows = rows_ht(qT, N_HEADS, HEAD_DIM)
    wcol = jnp.pad(wT.reshape(N_IDX_HEADS, n_seq, n_tok).transpose(1, 0, 2),
                   ((0, 0), (0, 0), (0, 8 - n_tok))).reshape(n_seq, N_IDX_HEADS * 8, 1)

    def pad_keys(x):
        x = x.reshape(n_seq, n_tok, x.shape[-1])
        return jnp.pad(x, ((0, 0), (0, PAGE_SIZE - n_tok), (0, 0)))

    ki_new_t = jnp.swapaxes(pad_keys(kibf[:, 0:IDX_DIM]), 1, 2)
    kidx_t = jnp.swapaxes(cache_kidx, 1, 2)
    k_new = pad_keys(kbf)
    v_new = pad_keys(v32.reshape(n_seq * n_tok, N_KV).astype(BF16))
    return (qi_rows, wcol, ki_new_t, kidx_t), (q_rows, k_new, v_new)


def _sample_topk(sp, sn, n_tok):
    n_seq, _, past = sp.shape
    rows = n_seq * 8
    bp, bn = pl.pallas_call(
        functools.partial(_s_thresh_body, n_tok),
        grid=(1,),
        in_specs=[pl.BlockSpec((rows, past), lambda i: (0, 0)),
                  pl.BlockSpec((rows, PAGE_SIZE), lambda i: (0, 0))],
        out_specs=[pl.BlockSpec((rows, past), lambda i: (0, 0)),
                   pl.BlockSpec((rows, PAGE_SIZE), lambda i: (0, 0))],
        out_shape=[jax.ShapeDtypeStruct((rows, past), F32),
                   jax.ShapeDtypeStruct((rows, PAGE_SIZE), F32)],
        compiler_params=_cparams(1),
        name="sample_topk_mask",
    )(sp.reshape(rows, past), sn.reshape(rows, PAGE_SIZE))
    return bp.reshape(n_seq, 8, past), bn.reshape(n_seq, 8, PAGE_SIZE)


def _sample_rows_to_tokens(out, n_seq, n_tok):
    out = out.reshape(n_seq, N_HEADS, 8, HEAD_DIM)[:, :, 0:n_tok]
    return out.transpose(0, 2, 1, 3).reshape(n_seq * n_tok, D_ATTN)


def _outproj_body(gated, *refs):
    if gated:
        x_ref, a_ref, gate_ref, mc_ref, wo_ref, o_ref = refs
        ma = (a_ref[...] * gate_ref[...]).astype(BF16)
    else:
        x_ref, a_ref, mc_ref, wo_ref, o_ref = refs
        ma = a_ref[...]
    acc = jnp.dot(ma, wo_ref[0:D_ATTN, :], preferred_element_type=F32)
    acc = acc + jnp.dot(mc_ref[...], wo_ref[D_ATTN:D_ATTN + D_CONV, :], preferred_element_type=F32)
    o_ref[...] = x_ref[...] + acc


def _outproj(x, attn, gate, mixc, wo, tm):
    m = x.shape[0]
    gated = gate is not None
    row = lambda i: (i, 0)
    in_specs = [pl.BlockSpec((tm, D_MODEL), row), pl.BlockSpec((tm, D_ATTN), row)]
    args = [x, attn]
    if gated:
        in_specs.append(pl.BlockSpec((tm, D_ATTN), row))
        args.append(gate)
    in_specs += [pl.BlockSpec((tm, D_CONV), row),
                 pl.BlockSpec((D_ATTN + D_CONV, D_MODEL), lambda i: (0, 0))]
    args += [mixc, wo]
    return pl.pallas_call(
        functools.partial(_outproj_body, gated),
        grid=(m // tm,),
        in_specs=in_specs,
        out_specs=pl.BlockSpec((tm, D_MODEL), row),
        out_shape=jax.ShapeDtypeStruct((m, D_MODEL), F32),
        compiler_params=_cparams(1),
        name="outproj_sample" if gated else "outproj_prompt",
    )(*args)


def _rope_tables(pos):
    posf = pos.astype(F32)[:, None]
    n = pos.shape[0]

    def cs(half):
        inv = ROPE_THETA ** (-jnp.arange(half, dtype=F32) / half)
        ang = posf * inv[None, :]
        return jnp.cos(ang), jnp.sin(ang)

    c16, s16 = cs(ROPE_HALF)
    c8, s8 = cs(IDX_ROPE_HALF)
    one = lambda w: jnp.ones((n, w), F32)
    zero = lambda w: jnp.zeros((n, w), F32)
    rest = HEAD_DIM - ROPE_DIM
    k_c = jnp.concatenate([c16, c16, one(rest)], axis=1)
    k_sa = jnp.concatenate([-s16, zero(HEAD_DIM - ROPE_HALF)], axis=1)
    k_sb = jnp.concatenate([zero(ROPE_HALF), s16, zero(rest)], axis=1)
    return dict(c16T=c16.T, s16T=s16.T, c8T=c8.T, s8T=s8.T, kC=k_c, kSa=k_sa, kSb=k_sb)


def _prep_weights(w_in, g_q, g_k, g_kidx, w_conv, w_out):
    assert w_in.shape == (D_MODEL, D_IN)
    wT = w_in.T.astype(BF16)
    gq = g_q.reshape(HEAD_DIM, 1)
    gk = g_k.reshape(1, HEAD_DIM)
    gki = g_kidx.reshape(IDX_DIM, 1)
    return wT, (gq, gk, gki, w_conv), w_out.astype(BF16)


def kernel(x_prompt, x_sample, cache_k, cache_v, cache_kidx, state_conv, page_table,
           norm_in, w_in, g_q, g_k, g_kidx, w_conv, w_out):
    n_b, seq, _ = x_prompt.shape
    n_s, n_t, _ = x_sample.shape
    depth = w_in.shape[0]
    past = page_table.shape[1] * PAGE_SIZE
    tabs_p = _rope_tables(jnp.arange(seq))
    tabs_s = _rope_tables(jnp.tile(past + jnp.arange(n_t), n_s))

    hp = x_prompt.reshape(n_b * seq, D_MODEL)
    hs = x_sample.reshape(n_s * n_t, D_MODEL)
    outs = [[] for _ in range(8)]
    for l in range(depth):
        w_t, params, wo = _prep_weights(w_in[l], g_q[l], g_k[l], g_kidx[l], w_conv[l], w_out[l])

        st = state_conv[l]
        tok = jnp.arange(n_t)
        e1 = st[:, jnp.full((n_t,), CONV_W - 2)].reshape(n_s * n_t, D_CONV)
        e2 = st[:, jnp.minimum(tok, CONV_W - 2)].reshape(n_s * n_t, D_CONV)
        (qT_s, qiT_s, _, wT_s, k32_s, kbf_s, v32_s, ki32_s, kibf_s, gate_s, mixc_s, u) = _project(
            hs, norm_in[l], w_t, params, tabs_s, n_t, state_rows=(e1, e2))
        idx_ops, (q_rows, k_new, v_new) = _sample_operands(
            qT_s, qiT_s, wT_s, kibf_s, kbf_s, v32_s, cache_kidx[l], n_s, n_t)
        (qT, qiT, vT, wT, k32, kbf, v32, ki32, kibf, gate, mixc, utail, sp, sn) = _project(
            hp, norm_in[l], w_t, params, tabs_p, 0, side=(page_table,) + idx_ops)

        bp, bn = _sample_topk(sp, sn, n_t)
        mixa, attn_rows = _attention(qiT, wT, kibf, qT, kbf, vT, gate, n_b, seq,
                                     q_rows, bp, bn, k_new, v_new, cache_k[l], cache_v[l], page_table)
        attn_s = _sample_rows_to_tokens(attn_rows, n_s, n_t)

        hp = _outproj(hp, mixa, None, mixc, wo, ROW_TM)
        hs = _outproj(hs, attn_s, gate_s, mixc_s, wo, n_s * n_t)
        tps = seq // PROJ_TM
        outs[0].append(k32.reshape(n_b, seq, N_KV_HEADS, HEAD_DIM))
        outs[1].append(v32.reshape(n_b, seq, N_KV_HEADS, HEAD_DIM))
        outs[2].append(ki32.reshape(n_b, seq, IDX_DIM))
        outs[3].append(utail[tps - 1::tps])
        outs[4].append(k32_s.reshape(n_s, n_t, N_KV_HEADS, HEAD_DIM))
        outs[5].append(v32_s.reshape(n_s, n_t, N_KV_HEADS, HEAD_DIM))
        outs[6].append(ki32_s.reshape(n_s, n_t, IDX_DIM))
        outs[7].append(u.reshape(n_s, n_t, D_CONV)[:, n_t - (CONV_W - 1):])

    return (hp.reshape(n_b, seq, D_MODEL), hs.reshape(n_s, n_t, D_MODEL),
            *[jnp.stack(o) for o in outs])
```

```python
import functools

import jax
import jax.numpy as jnp
from jax import lax
from jax.experimental import pallas as pl
from jax.experimental.pallas import tpu as pltpu

F32 = jnp.float32
BF16 = jnp.bfloat16
I32 = jnp.int32

D_MODEL = 2048
HEAD_DIM = 128
N_HEADS = 8
N_KV_HEADS = 2
GROUP = N_HEADS // N_KV_HEADS
D_ATTN = N_HEADS * HEAD_DIM
D_CONV = 1024
ROPE_DIM = HEAD_DIM // 4
ROPE_HALF = ROPE_DIM // 2
ROPE_THETA = 500000.0
N_IDX_HEADS = 16
IDX_DIM = 64
IDX_ROPE_HALF = IDX_DIM // 8
TOPK_MAX = 256
CONV_W = 3
PAGE_SIZE = 128
EPS = 1e-6
W_IDX_SCALE = (N_IDX_HEADS ** -0.5) * (IDX_DIM ** -0.5)
ATTN_SCALE = HEAD_DIM ** -0.5
Q_PRESCALE = ATTN_SCALE * 1.4426950408889634

INT_MIN = -(2 ** 31)
NEG = -1e30

VMEM_LIMIT_BYTES = 60 * 1024 * 1024

PROJ_TN = 512
PROJ_TM = 1024
PROJ_SUB = 256
ROW_TM = 512
N_KV = N_KV_HEADS * HEAD_DIM
OFF_Q = 0
OFF_K = OFF_Q + D_ATTN
OFF_V = OFF_K + N_KV
OFF_Z = OFF_V + N_KV
OFF_QI = OFF_Z + D_ATTN
OFF_KI = OFF_QI + N_IDX_HEADS * IDX_DIM
OFF_WI = OFF_KI + IDX_DIM
OFF_H = OFF_WI + N_IDX_HEADS
OFF_B = OFF_H + D_CONV
OFF_C = OFF_B + D_CONV
OFF_ZC = OFF_C + D_CONV
D_IN = OFF_ZC + D_CONV
assert OFF_K == 2 * PROJ_TN and OFF_Z == 3 * PROJ_TN and OFF_QI == 5 * PROJ_TN and OFF_KI == 7 * PROJ_TN
KIW_ROWS = 128
assert OFF_KI % KIW_ROWS == 0 and OFF_WI - OFF_KI == IDX_DIM
CONV_CHUNK = 256
N_CONV_CHUNKS = D_CONV // CONV_CHUNK
J_Q = 0
J_KV = 2
J_Z = 3
J_QI = 5
J_CONV = 7
N_MAIN_BLOCKS = J_CONV
N_STEPS = J_CONV + N_CONV_CHUNKS

ATT_TQ = 256
ATT_TK = 256
CNT_ACCS = 4

SIDE_PAGES = 16


def _cparams(n_axes):
    return pltpu.CompilerParams(
        dimension_semantics=("arbitrary",) * n_axes,
        vmem_limit_bytes=VMEM_LIMIT_BYTES,
    )


def _silu(x):
    return x * jax.nn.sigmoid(x)


def _indexer_rows_scores(qi, wcol, keys_t):
    d = jnp.dot(qi, keys_t, preferred_element_type=F32)
    val = jnp.maximum(d, 0.0) * wcol
    return val.reshape(N_IDX_HEADS, 8, val.shape[-1]).sum(axis=0)


class _SideScores:
    def __init__(self, step, n_total, spq, pt_ref, qi_ref, wcol_ref, kinew_ref, kidx_hbm, sp_ref, sn_ref,
                 kbuf, ksem):
        self.step, self.n_total, self.spq, self.pt_ref = step, n_total, spq, pt_ref
        self.qi_ref, self.wcol_ref, self.kinew_ref, self.kidx_hbm = qi_ref, wcol_ref, kinew_ref, kidx_hbm
        self.sp_ref, self.sn_ref, self.kbuf, self.ksem = sp_ref, sn_ref, kbuf, ksem
        self.npg = kbuf.shape[1]
        self.n_side = pt_ref.shape[0] * spq

    def _start(self, n):
        c = jnp.minimum(n, self.n_side - 1)
        seq_n = c // self.spq
        base = (c % self.spq) * self.npg
        for r in range(self.npg):
            _page_copy(self.kidx_hbm, self.pt_ref[seq_n, base + r], self.kbuf, n % 2, r, self.ksem).start()

    def prologue(self):
        pl.when(self.step == 0)(lambda: self._start(self.step))

    def __call__(self):
        pl.when(self.step + 1 < self.n_total)(lambda: self._start(self.step + 1))
        slot = self.step % 2
        for r in range(self.npg):
            _page_copy(self.kidx_hbm, 0, self.kbuf, slot, r, self.ksem).wait()
        qi = self.qi_ref[0]
        wcol = self.wcol_ref[0]
        keys_t = jnp.concatenate([self.kbuf[slot, r] for r in range(self.npg)], axis=1).astype(BF16)
        self.sp_ref[0] = _indexer_rows_scores(qi, wcol, keys_t)
        self.sn_ref[0] = _indexer_rows_scores(qi, wcol, self.kinew_ref[0])


class _Bf16View:
    def __init__(self, ref):
        self.ref = ref

    def __getitem__(self, idx):
        return self.ref[idx].astype(BF16)


def _proj_body(sample, tps, tm, side_spq, convert, *refs):
    if side_spq:
        pt_ref, refs = refs[0], refs[1:]
    (x_ref, gin_ref, wa_ref, wkiw_ref, wh_ref, wb_ref, wc_ref, wzc_ref, gq_ref, gk_ref, gki_ref,
     c16_ref, s16_ref, c8_ref, s8_ref,
     kc_ref, ksa_ref, ksb_ref, wconv_ref) = refs[:19]
    refs = refs[19:]
    if sample:
        e1_ref, e2_ref = refs[:2]
        refs = refs[2:]
    if side_spq:
        side_in, refs = refs[:4], refs[4:]
    (qT_ref, qiT_ref, vT_ref, wT_ref, k32_ref, kbf_ref, v32_ref, ki32_ref, kibf_ref,
     gate_ref, mixc_ref, u_ref) = refs[:12]
    refs = refs[12:]
    if convert:
        w_out, refs = refs[:6], refs[6:]
    if side_spq:
        side_out, refs = refs[:2], refs[2:]
    xn_ref = refs[0]
    if not sample:
        carry_ref = refs[1]

    i = pl.program_id(0)
    j = pl.program_id(1)
    nt = (((1,), (1,)), ((), ()))
    if side_spq:
        side = _SideScores(i * N_STEPS + j, pl.num_programs(0) * N_STEPS, side_spq, pt_ref,
                           *side_in, *side_out, *refs[2:4])
        side.prologue()
    else:
        side = lambda: None

    if convert:
        wa_o, kiw_o, wh_o, wb_o, wc_o, wzc_o = w_out

        @pl.when(j < N_MAIN_BLOCKS)
        def _():
            wa_o[...] = wa_ref[...].astype(BF16)

        @pl.when(j == J_KV)
        def _():
            kiw_o[...] = wkiw_ref[...].astype(BF16)

        @pl.when(j >= J_CONV)
        def _():
            for src, dst in ((wh_ref, wh_o), (wb_ref, wb_o), (wc_ref, wc_o), (wzc_ref, wzc_o)):
                dst[...] = src[...].astype(BF16)

        wa_ref, wkiw_ref, wh_ref, wb_ref, wc_ref, wzc_ref = (
            _Bf16View(r) for r in (wa_ref, wkiw_ref, wh_ref, wb_ref, wc_ref, wzc_ref))

    def xw(w, x=None):
        return lax.dot_general(xn_ref[...] if x is None else x, w, nt, preferred_element_type=F32)

    def wx(w, x):
        return lax.dot_general(w, x, nt, preferred_element_type=F32)

    nsub = max(tm // PROJ_SUB, 1)
    sub = tm // nsub
    pieces = [slice(rb * sub, (rb + 1) * sub) for rb in range(nsub)]

    if not sample:
        @pl.when((i == 0) & (j == 0))
        def _init():
            carry_ref[...] = jnp.zeros(carry_ref.shape, F32)

    def q_step(with_norm):
        side()
        for cs in pieces:
            if with_norm:
                x = x_ref[cs, :]
                ms = jnp.mean(x * x, axis=-1, keepdims=True)
                xn_ref[cs, :] = (x * lax.rsqrt(ms + EPS) * gin_ref[...]).astype(BF16)
            res = wx(wa_ref[...], xn_ref[cs, :])
            cos = c16_ref[:, cs]
            sin = s16_ref[:, cs]
            for hh in range(PROJ_TN // HEAD_DIM):
                blk = res[hh * HEAD_DIM:(hh + 1) * HEAD_DIM]
                ms = jnp.mean(blk * blk, axis=0, keepdims=True)
                y = blk * lax.rsqrt(ms + EPS) * gq_ref[...]
                x1 = y[0:ROPE_HALF]
                x2 = y[ROPE_HALF:ROPE_DIM]
                base = hh * HEAD_DIM
                qT_ref[base:base + ROPE_HALF, cs] = ((x1 * cos - x2 * sin) * Q_PRESCALE).astype(BF16)
                qT_ref[base + ROPE_HALF:base + ROPE_DIM, cs] = ((x2 * cos + x1 * sin) * Q_PRESCALE).astype(BF16)
                qT_ref[base + ROPE_DIM:base + HEAD_DIM, cs] = (y[ROPE_DIM:] * Q_PRESCALE).astype(BF16)

    pl.when(j == J_Q)(functools.partial(q_step, True))
    pl.when((j > J_Q) & (j < J_KV))(functools.partial(q_step, False))

    @pl.when((j >= J_QI) & (j < J_CONV))
    def _qi():
        side()
        for cs in pieces:
            res = wx(wa_ref[...], xn_ref[cs, :])
            cos = c8_ref[:, cs]
            sin = s8_ref[:, cs]
            for hh in range(PROJ_TN // IDX_DIM):
                blk = res[hh * IDX_DIM:(hh + 1) * IDX_DIM]
                x1 = blk[0:IDX_ROPE_HALF]
                x2 = blk[IDX_ROPE_HALF:2 * IDX_ROPE_HALF]
                rot = jnp.concatenate([x1 * cos - x2 * sin, x2 * cos + x1 * sin], axis=0)
                base = hh * IDX_DIM
                qiT_ref[base:base + 2 * IDX_ROPE_HALF, cs] = rot.astype(BF16)
                qiT_ref[base + 2 * IDX_ROPE_HALF:base + IDX_DIM, cs] = blk[2 * IDX_ROPE_HALF:].astype(BF16)

    @pl.when(j == J_KV)
    def _kv():
        side()
        for rb, cs in enumerate(pieces):
            x = xn_ref[cs, :]
            res = xw(wa_ref[...], x)
            for hd in range(N_KV_HEADS):
                hs = slice(hd * HEAD_DIM, (hd + 1) * HEAD_DIM)
                blk = res[:, hs]
                ms = jnp.mean(blk * blk, axis=-1, keepdims=True)
                yk = blk * lax.rsqrt(ms + EPS) * gk_ref[...]
                rot = yk * kc_ref[cs, :] + (pltpu.roll(yk, HEAD_DIM - ROPE_HALF, axis=1) * ksa_ref[cs, :]
                                            + pltpu.roll(yk, ROPE_HALF, axis=1) * ksb_ref[cs, :])
                k32_ref[pl.ds(N_KV_HEADS * rb * sub + hd, sub, stride=N_KV_HEADS), :] = rot
                kbf_ref[cs, hs] = rot.astype(BF16)
            v = res[:, N_KV:2 * N_KV]
            for hd in range(N_KV_HEADS):
                v32_ref[pl.ds(N_KV_HEADS * rb * sub + hd, sub, stride=N_KV_HEADS), :] = (
                    v[:, hd * HEAD_DIM:(hd + 1) * HEAD_DIM])
            if sample:
                vT_ref[:, cs] = wx(wa_ref[N_KV:2 * N_KV, :], x).astype(BF16)
            else:
                vT_ref[:, cs] = v.T.astype(BF16)
            r2t = wx(wkiw_ref[...], x)
            wT_ref[:, cs] = r2t[IDX_DIM:IDX_DIM + N_IDX_HEADS] * W_IDX_SCALE
            kit = r2t[0:IDX_DIM]
            ms = jnp.mean(kit * kit, axis=0, keepdims=True)
            yi = kit * lax.rsqrt(ms + EPS) * gki_ref[...]
            x1 = yi[0:IDX_ROPE_HALF]
            x2 = yi[IDX_ROPE_HALF:2 * IDX_ROPE_HALF]
            cos = c8_ref[:, cs]
            sin = s8_ref[:, cs]
            roti = jnp.concatenate([x1 * cos - x2 * sin, x2 * cos + x1 * sin, yi[2 * IDX_ROPE_HALF:],
                                    jnp.zeros((KIW_ROWS - IDX_DIM, sub), F32)], axis=0)
            ki_nat = roti.T
            ki32_ref[cs, :] = ki_nat[:, 0:IDX_DIM]
            kibf_ref[cs, :] = ki_nat.astype(BF16)

    @pl.when((j >= J_Z) & (j < J_QI))
    def _z():
        side()
        for cs in pieces:
            gate_ref[cs, :] = _silu(xw(wa_ref[...], xn_ref[cs, :]))

    @pl.when(j >= J_CONV)
    def _conv():
        side()
        cc = j - J_CONV
        u = xw(wc_ref[...]) * xw(wh_ref[...])
        rowid = lax.broadcasted_iota(I32, (tm, CONV_CHUNK), 0)
        if sample:
            t = rowid & (sample - 1)
            u1 = jnp.where(t >= 1, pltpu.roll(u, 1, axis=0), e1_ref[...])
            u2 = jnp.where(t >= 2, pltpu.roll(u, 2, axis=0), e2_ref[...])
            u_ref[...] = u
        else:
            first = (i % tps) == 0
            prev = carry_ref[cc]
            p0 = jnp.where(first, 0.0, prev[0:1])
            p1 = jnp.where(first, 0.0, prev[1:2])
            u1 = jnp.where(rowid == 0, p1, pltpu.roll(u, 1, axis=0))
            u2 = jnp.where(rowid == 0, p0, jnp.where(rowid == 1, p1, pltpu.roll(u, 2, axis=0)))
            tail = u[tm - 8:tm]
            carry_ref[cc] = jnp.concatenate([tail[6:8], tail[0:6]], axis=0)
            u_ref[0] = tail[6:8]
        w = wconv_ref[...]
        y = u2 * w[0:1] + u1 * w[1:2] + u * w[2:3]
        mixc_ref[...] = (xw(wb_ref[...]) * y * _silu(xw(wzc_ref[...]))).astype(BF16)


def _project(x, norm_g, wts, params, tabs, sample, state_rows=None, side=None):
    m = x.shape[0]
    convert = not isinstance(wts, tuple)
    assert sample & (sample - 1) == 0
    tm = m if sample else PROJ_TM
    n_i = m // tm
    tps = 1 if sample else (tabs["c16T"].shape[1] // tm)
    gq, gk, gki, wconv = params

    def tmap(i):
        return i % tps

    def cchunk(j):
        return jnp.clip(j - J_CONV, 0, N_CONV_CHUNKS - 1)

    main_spec = pl.BlockSpec((PROJ_TN, D_MODEL), lambda i, j, *_: (jnp.minimum(j, N_MAIN_BLOCKS - 1), 0))
    if convert:
        def conv_rows(off):
            return pl.BlockSpec((pl.Element(CONV_CHUNK), pl.Element(D_MODEL)),
                                lambda i, j, *_: (pl.multiple_of(off + CONV_CHUNK * cchunk(j), 16), 0))

        w_specs = [main_spec, pl.BlockSpec((KIW_ROWS, D_MODEL), lambda i, j, *_: (OFF_KI // KIW_ROWS, 0)),
                   conv_rows(OFF_H), conv_rows(OFF_B), conv_rows(OFF_C), conv_rows(OFF_ZC)]
        w_args = [wts] * 6
    else:
        conv_spec = pl.BlockSpec(
            (CONV_CHUNK, D_MODEL),
            lambda i, j, *_: (jnp.where(j < J_QI, N_CONV_CHUNKS - 1, cchunk(j)), 0))
        w_specs = [main_spec, pl.BlockSpec((KIW_ROWS, D_MODEL), lambda i, j, *_: (0, 0))] + [conv_spec] * 4
        w_args = list(wts)

    in_specs = [
        pl.BlockSpec((tm, D_MODEL), lambda i, j, *_: (i, 0)),
        pl.BlockSpec((1, D_MODEL), lambda i, j, *_: (0, 0)),
        *w_specs,
        pl.BlockSpec((HEAD_DIM, 1), lambda i, j, *_: (0, 0)),
        pl.BlockSpec((1, HEAD_DIM), lambda i, j, *_: (0, 0)),
        pl.BlockSpec((IDX_DIM, 1), lambda i, j, *_: (0, 0)),
        pl.BlockSpec((ROPE_HALF, tm), lambda i, j, *_: (0, tmap(i))),
        pl.BlockSpec((ROPE_HALF, tm), lambda i, j, *_: (0, tmap(i))),
        pl.BlockSpec((IDX_ROPE_HALF, tm), lambda i, j, *_: (0, tmap(i))),
        pl.BlockSpec((IDX_ROPE_HALF, tm), lambda i, j, *_: (0, tmap(i))),
        pl.BlockSpec((tm, HEAD_DIM), lambda i, j, *_: (tmap(i), 0)),
        pl.BlockSpec((tm, HEAD_DIM), lambda i, j, *_: (tmap(i), 0)),
        pl.BlockSpec((tm, HEAD_DIM), lambda i, j, *_: (tmap(i), 0)),
        pl.BlockSpec((CONV_W, CONV_CHUNK), lambda i, j, *_: (0, cchunk(j))),
    ]
    args = [x, norm_g.reshape(1, D_MODEL), *w_args, gq, gk, gki,
            tabs["c16T"], tabs["s16T"], tabs["c8T"], tabs["s8T"],
            tabs["kC"], tabs["kSa"], tabs["kSb"], wconv]
    cmap = lambda i, j, *_: (i, cchunk(j))
    if sample:
        in_specs += [pl.BlockSpec((tm, CONV_CHUNK), cmap), pl.BlockSpec((tm, CONV_CHUNK), cmap)]
        args += list(state_rows)
        u_spec = pl.BlockSpec((tm, CONV_CHUNK), cmap)
        u_shape = jax.ShapeDtypeStruct((m, D_CONV), F32)
        scratch = [pltpu.VMEM((tm, D_MODEL), BF16)]
    else:
        u_spec = pl.BlockSpec((1, CONV_W - 1, CONV_CHUNK), lambda i, j, *_: (i, 0, cchunk(j)))
        u_shape = jax.ShapeDtypeStruct((n_i, CONV_W - 1, D_CONV), F32)
        scratch = [pltpu.VMEM((tm, D_MODEL), BF16), pltpu.VMEM((N_CONV_CHUNKS, 8, CONV_CHUNK), F32)]

    out_specs = [
        pl.BlockSpec((PROJ_TN, tm), lambda i, j, *_: (jnp.minimum(j, 1), i)),
        pl.BlockSpec((PROJ_TN, tm), lambda i, j, *_: (jnp.clip(j - J_QI, 0, 1), i)),
        pl.BlockSpec((N_KV, tm), lambda i, j, *_: (0, i)),
        pl.BlockSpec((N_IDX_HEADS, tm), lambda i, j, *_: (0, i)),
        pl.BlockSpec((N_KV_HEADS * tm, HEAD_DIM), lambda i, j, *_: (i, 0)),
        pl.BlockSpec((tm, N_KV), lambda i, j, *_: (i, 0)),
        pl.BlockSpec((N_KV_HEADS * tm, HEAD_DIM), lambda i, j, *_: (i, 0)),
        pl.BlockSpec((tm, IDX_DIM), lambda i, j, *_: (i, 0)),
        pl.BlockSpec((tm, KIW_ROWS), lambda i, j, *_: (i, 0)),
        pl.BlockSpec((tm, PROJ_TN), lambda i, j, *_: (i, jnp.clip(j - J_Z, 0, 1))),
        pl.BlockSpec((tm, CONV_CHUNK), cmap),
        u_spec,
    ]
    out_shape = [
        jax.ShapeDtypeStruct((D_ATTN, m), BF16),
        jax.ShapeDtypeStruct((N_IDX_HEADS * IDX_DIM, m), BF16),
        jax.ShapeDtypeStruct((N_KV, m), BF16),
        jax.ShapeDtypeStruct((N_IDX_HEADS, m), F32),
        jax.ShapeDtypeStruct((N_KV_HEADS * m, HEAD_DIM), F32),
        jax.ShapeDtypeStruct((m, N_KV), BF16),
        jax.ShapeDtypeStruct((N_KV_HEADS * m, HEAD_DIM), F32),
        jax.ShapeDtypeStruct((m, IDX_DIM), F32),
        jax.ShapeDtypeStruct((m, KIW_ROWS), BF16),
        jax.ShapeDtypeStruct((m, D_ATTN), F32),
        jax.ShapeDtypeStruct((m, D_CONV), BF16),
        u_shape,
    ]
    if convert:
        assert n_i == 1, "every weight block must be visited exactly once to be written back"
        out_specs += [
            main_spec,
            pl.BlockSpec((KIW_ROWS, D_MODEL), lambda i, j, *_: (0, 0)),
        ] + [pl.BlockSpec((CONV_CHUNK, D_MODEL), lambda i, j, *_: (cchunk(j), 0))] * 4
        out_shape += [jax.ShapeDtypeStruct((OFF_KI, D_MODEL), BF16),
                      jax.ShapeDtypeStruct((KIW_ROWS, D_MODEL), BF16)]
        out_shape += [jax.ShapeDtypeStruct((D_CONV, D_MODEL), BF16)] * 4
    if side is None:
        return pl.pallas_call(
            functools.partial(_proj_body, sample, tps, tm, 0, convert),
            grid=(n_i, N_STEPS),
            in_specs=in_specs,
            out_specs=out_specs,
            out_shape=out_shape,
            scratch_shapes=scratch,
            compiler_params=_cparams(2),
            name="proj_sample" if sample else "proj_prompt",
        )(*args)

    page_table, qi_rows, wcol, ki_new_t, kidx_t = side
    n_seq, n_pages = page_table.shape
    spq = n_pages // SIDE_PAGES
    n_side = n_seq * spq
    assert spq * SIDE_PAGES == n_pages and n_side <= n_i * N_STEPS
    width = SIDE_PAGES * PAGE_SIZE

    def side_step(i, j):
        return jnp.minimum(i * N_STEPS + j, n_side - 1)

    seq_map = lambda i, j, *_: (side_step(i, j) // spq, 0, 0)
    in_specs += [
        pl.BlockSpec((1, N_IDX_HEADS * 8, IDX_DIM), seq_map),
        pl.BlockSpec((1, N_IDX_HEADS * 8, 1), seq_map),
        pl.BlockSpec((1, IDX_DIM, PAGE_SIZE), seq_map),
        pl.BlockSpec(memory_space=pl.ANY),
    ]
    out_specs += [
        pl.BlockSpec((1, 8, width), lambda i, j, *_: (side_step(i, j) // spq, 0, side_step(i, j) % spq)),
        pl.BlockSpec((1, 8, PAGE_SIZE), seq_map),
    ]
    out_shape += [jax.ShapeDtypeStruct((n_seq, 8, n_pages * PAGE_SIZE), F32),
                  jax.ShapeDtypeStruct((n_seq, 8, PAGE_SIZE), F32)]
    scratch += [pltpu.VMEM((2, SIDE_PAGES, IDX_DIM, PAGE_SIZE), F32), pltpu.SemaphoreType.DMA((2,))]
    return pl.pallas_call(
        functools.partial(_proj_body, sample, tps, tm, spq, convert),
        grid_spec=pltpu.PrefetchScalarGridSpec(
            num_scalar_prefetch=1,
            grid=(n_i, N_STEPS),
            in_specs=in_specs,
            out_specs=out_specs,
            scratch_shapes=scratch,
        ),
        out_shape=out_shape,
        compiler_params=_cparams(2),
        name="proj_prompt",
    )(page_table, *args, qi_rows, wcol, ki_new_t, kidx_t)


KEY_LOWEST_FINITE = INT_MIN + 0x00800000


def _key_to_float(key):
    return pltpu.bitcast(key ^ ((key >> 31) & 0x7FFFFFFF), F32)


def _bit_value(b):
    return lax.shift_left(jnp.int32(1), jnp.int32(31) - b)


def _prompt_attn_step(i, qiT_ref, wT_ref, ki_ref, qT_ref, k_ref, vT_ref, gate_ref, o_ref,
                      sc_ref, bias_ref, acc_ref, s_ref):
    tq, tk = ATT_TQ, ATT_TK
    nch = i + 1
    w = wT_ref[...]
    row = lax.broadcasted_iota(I32, (tk, tq), 0)
    col = lax.broadcasted_iota(I32, (tk, tq), 1)
    row8 = lax.broadcasted_iota(I32, (8, tq), 0)

    def score_chunk(j, carry):
        off = pl.multiple_of(j * tk, tk)
        kic = ki_ref[pl.ds(off, tk), 0:IDX_DIM]
        acc = jnp.zeros((tk, tq), F32)
        for h in range(N_IDX_HEADS):
            d = jnp.dot(kic, qiT_ref[h * IDX_DIM:(h + 1) * IDX_DIM, :], preferred_element_type=F32)
            acc = acc + w[h:h + 1, :] * jnp.maximum(d, 0.0)
        future = (row + j * tk) > (col + i * tq)
        sc_ref[pl.ds(off, tk), :] = jnp.where(future, -jnp.inf, acc)
        return carry

    lax.fori_loop(0, nch, score_chunk, 0)

    def count_keys(pred):
        def cnt_chunk(j, cs):
            off = pl.multiple_of(j * tk, tk)
            cs = list(cs)
            sc = sc_ref[pl.ds(off, tk), :]
            for r in range(tk // 8):
                a = cs[r % CNT_ACCS]
                kpos = row8 + (j * tk + r * 8)
                cs[r % CNT_ACCS] = jnp.where(pred(sc[r * 8:(r + 1) * 8], kpos), a + 1, a)
            return tuple(cs)

        cs = lax.fori_loop(0, nch, cnt_chunk, (jnp.zeros((8, tq), I32),) * CNT_ACCS)
        c = cs[0]
        for a in cs[1:]:
            c = c + a
        return jnp.sum(c.astype(F32), axis=0, keepdims=True)

    def bit_body(b, carry):
        thr, cge = carry
        cand = thr + _bit_value(b)
        cand_f = _key_to_float(cand)
        cnt = count_keys(lambda s, kpos: s >= cand_f)
        ok = cnt >= float(TOPK_MAX)
        return jnp.where(ok, cand, thr), jnp.where(ok, cnt, cge)

    thr, cge = lax.fori_loop(0, 32, bit_body, (jnp.full((1, tq), INT_MIN, I32),
                                               jnp.zeros((1, tq), F32)))
    thr_f = _key_to_float(jnp.maximum(thr, KEY_LOWEST_FINITE))
    has_ties = jnp.max(cge) > float(TOPK_MAX)

    @pl.when(jnp.logical_not(has_ties))
    def _plain_mask():
        def bias_chunk(j, carry):
            off = pl.multiple_of(j * tk, tk)
            bias_ref[pl.ds(off, tk), :] = jnp.where(sc_ref[pl.ds(off, tk), :] >= thr_f, 0.0, NEG)
            return carry

        lax.fori_loop(0, nch, bias_chunk, 0)

    @pl.when(has_ties)
    def _tie_mask():
        need = float(TOPK_MAX) - count_keys(lambda s, kpos: s > thr_f)
        nbits = (sc_ref.shape[0] - 1).bit_length()

        def pos_bit(b, last):
            step = lax.shift_left(jnp.int32(1), jnp.int32(nbits - 1) - b)
            probe = last + (step - 1)
            got = count_keys(lambda s, kpos: (s == thr_f) & (kpos <= probe))
            return jnp.where(got < need, last + step, last)

        last = lax.fori_loop(0, nbits, pos_bit, jnp.zeros((1, tq), I32))

        def bias_chunk(j, carry):
            off = pl.multiple_of(j * tk, tk)
            sc = sc_ref[pl.ds(off, tk), :]
            keep = (sc > thr_f) | ((sc == thr_f) & ((row + j * tk) <= last))
            bias_ref[pl.ds(off, tk), :] = jnp.where(keep, 0.0, NEG)
            return carry

        lax.fori_loop(0, nch, bias_chunk, 0)

    def qk_chunk(j, mrun):
        off = pl.multiple_of(j * tk, tk)
        bias = bias_ref[pl.ds(off, tk), :]
        out = []
        for h in range(N_HEADS):
            g = h // GROUP
            kc = k_ref[pl.ds(off, tk), g * HEAD_DIM:(g + 1) * HEAD_DIM]
            s = jnp.dot(kc, qT_ref[h * HEAD_DIM:(h + 1) * HEAD_DIM, :], preferred_element_type=F32) + bias
            s_ref[h, pl.ds(off, tk), :] = s
            out.append(jnp.maximum(mrun[h], s.reshape(tk // 8, 8, tq).max(axis=0)))
        return tuple(out)

    mrun = lax.fori_loop(0, nch, qk_chunk, (jnp.full((8, tq), NEG, F32),) * N_HEADS)
    ms = [jnp.max(mr, axis=0, keepdims=True) for mr in mrun]
    acc_ref[...] = jnp.zeros(acc_ref.shape, F32)

    def pv_chunk(j, lrun):
        off = pl.multiple_of(j * tk, tk)
        out = []
        for h in range(N_HEADS):
            g = h // GROUP
            hs = slice(h * HEAD_DIM, (h + 1) * HEAD_DIM)
            p = jnp.exp2(s_ref[h, pl.ds(off, tk), :] - ms[h])
            out.append(lrun[h] + p.reshape(tk // 8, 8, tq).sum(axis=0))
            vc = vT_ref[g * HEAD_DIM:(g + 1) * HEAD_DIM, pl.ds(off, tk)]
            acc_ref[hs, :] += jnp.dot(vc, p.astype(BF16), preferred_element_type=F32)
        return tuple(out)

    lrun = lax.fori_loop(0, nch, pv_chunk, (jnp.zeros((8, tq), F32),) * N_HEADS)

    for h in range(N_HEADS):
        hs = slice(h * HEAD_DIM, (h + 1) * HEAD_DIM)
        o = (acc_ref[hs, :] / jnp.sum(lrun[h], axis=0, keepdims=True)).T
        o_ref[:, hs] = (o * gate_ref[:, hs]).astype(BF16)


def _page_copy(cache_hbm, page, pbuf, slot, r, sem):
    return pltpu.make_async_copy(cache_hbm.at[page], pbuf.at[slot, r], sem.at[slot])


def _attn_fused_body(nq, spq, pt_ref, *refs):
    prompt_in = refs[0:7]
    q_ref, bp_ref, bn_ref, knew_ref, vnew_ref, ck_hbm, cv_hbm = refs[7:14]
    o_ref, so_ref = refs[14:16]
    prompt_scratch = refs[16:20]
    pbuf, sem, ss_ref, ps_ref, ls_ref, accs_ref = refs[20:26]

    i = pl.program_id(1)
    step = pl.program_id(0) * nq + i
    n_total = pl.num_programs(0) * nq
    half = spq // 2
    npg = pbuf.shape[1]

    def start_pages(n):
        seq_n = n // spq
        t_n = n % spq
        base = (t_n % half) * npg
        slot = n % 2
        for cache_hbm, pred in ((ck_hbm, t_n < half), (cv_hbm, t_n >= half)):
            @pl.when(pred)
            def _(cache_hbm=cache_hbm):
                for r in range(npg):
                    _page_copy(cache_hbm, pt_ref[seq_n, base + r], pbuf, slot, r, sem).start()

    @pl.when(step == 0)
    def _prologue():
        start_pages(step)

    @pl.when(step + 1 < n_total)
    def _prefetch():
        start_pages(step + 1)

    slot = step % 2
    for r in range(npg):
        _page_copy(ck_hbm, 0, pbuf, slot, r, sem).wait()

    _sample_attn_step(step % spq, half, slot, q_ref, bp_ref, bn_ref, knew_ref, vnew_ref, pbuf,
                      so_ref, ss_ref, ps_ref, ls_ref, accs_ref)
    _prompt_attn_step(i, *prompt_in, o_ref, *prompt_scratch)


def _attention(qiT, wT, kibf, qT, kbf, vT, gate, n_batch, seq,
               q_rows, bp, bn, k_new, v_new, cache_k, cache_v, page_table):
    m = n_batch * seq
    nq = seq // ATT_TQ
    n_seq, n_pages = page_table.shape
    n_steps = n_batch * nq
    spq = n_steps // n_seq
    assert spq * n_seq == n_steps and spq % 2 == 0
    half = spq // 2
    npg = n_pages // half
    assert npg * half == n_pages
    width = npg * PAGE_SIZE
    past = n_pages * PAGE_SIZE
    n_pool = cache_k.shape[0]
    kv_rows = PAGE_SIZE * N_KV_HEADS
    ck = cache_k.reshape(n_pool, kv_rows, HEAD_DIM)
    cv = cache_v.reshape(n_pool, kv_rows, HEAD_DIM)
    nrow = N_HEADS * 8

    qmap = lambda b, i, pt: (0, b * nq + i)
    smap = lambda b, i, pt: ((b * nq + i) // spq, 0, 0)
    return pl.pallas_call(
        functools.partial(_attn_fused_body, nq, spq),
        grid_spec=pltpu.PrefetchScalarGridSpec(
            num_scalar_prefetch=1,
            grid=(n_batch, nq),
            in_specs=[
                pl.BlockSpec((N_IDX_HEADS * IDX_DIM, ATT_TQ), qmap),
                pl.BlockSpec((N_IDX_HEADS, ATT_TQ), qmap),
                pl.BlockSpec((seq, 128), lambda b, i, pt: (b, 0)),
                pl.BlockSpec((D_ATTN, ATT_TQ), qmap),
                pl.BlockSpec((seq, N_KV), lambda b, i, pt: (b, 0)),
                pl.BlockSpec((N_KV, seq), lambda b, i, pt: (0, b)),
                pl.BlockSpec((ATT_TQ, D_ATTN), lambda b, i, pt: (b * nq + i, 0)),
                pl.BlockSpec((1, nrow, HEAD_DIM), smap),
                pl.BlockSpec((1, 8, width),
                             lambda b, i, pt: ((b * nq + i) // spq, 0, jnp.minimum((b * nq + i) % spq, half - 1))),
                pl.BlockSpec((1, 8, PAGE_SIZE), smap),
                pl.BlockSpec((1, PAGE_SIZE, N_KV), smap),
                pl.BlockSpec((1, PAGE_SIZE, N_KV), smap),
                pl.BlockSpec(memory_space=pl.ANY),
                pl.BlockSpec(memory_space=pl.ANY),
            ],
            out_specs=[
                pl.BlockSpec((ATT_TQ, D_ATTN), lambda b, i, pt: (b * nq + i, 0)),
                pl.BlockSpec((1, nrow, HEAD_DIM), smap),
            ],
            scratch_shapes=[
                pltpu.VMEM((seq, ATT_TQ), F32), pltpu.VMEM((seq, ATT_TQ), F32),
                pltpu.VMEM((D_ATTN, ATT_TQ), F32), pltpu.VMEM((N_HEADS, seq, ATT_TQ), F32),
                pltpu.VMEM((2, npg, kv_rows, HEAD_DIM), F32), pltpu.SemaphoreType.DMA((2,)),
                pltpu.VMEM((nrow, past + PAGE_SIZE), F32), pltpu.VMEM((nrow, past + PAGE_SIZE), BF16),
                pltpu.VMEM((nrow, 128), F32), pltpu.VMEM((nrow, HEAD_DIM), F32),
            ],
        ),
        out_shape=[jax.ShapeDtypeStruct((m, D_ATTN), BF16),
                   jax.ShapeDtypeStruct((n_seq, nrow, HEAD_DIM), F32)],
        compiler_params=_cparams(2),
        name="attention",
    )(page_table, qiT, wT, kibf, qT, kbf, vT, gate, q_rows, bp, bn, k_new, v_new, ck, cv)


def _s_thresh_body(n_new, sp_ref, sn_ref, bp_ref, bn_ref):
    rows, past = sp_ref.shape
    ch = 2048
    nchunk = past // ch
    t = lax.broadcasted_iota(I32, (rows, 128), 0) & 7
    lane = lax.broadcasted_iota(I32, (rows, 128), 1)
    sn = jnp.where((lane < n_new) & (lane <= t), sn_ref[...], -jnp.inf)

    def fold(x):
        f = x[:, 0:128]
        for q in range(1, x.shape[1] // 128):
            f = f + x[:, q * 128:(q + 1) * 128]
        return f

    lane_ch = lax.broadcasted_iota(I32, (rows, ch), 1)

    def count_keys(pred):
        def cnt_chunk(c, acc):
            off = pl.multiple_of(c * ch, ch)
            return acc + fold(jnp.where(pred(sp_ref[:, pl.ds(off, ch)], lane_ch + c * ch), 1.0, 0.0))

        acc = lax.fori_loop(0, nchunk, cnt_chunk, jnp.where(pred(sn, lane + past), 1.0, 0.0))
        return jnp.sum(acc, axis=1, keepdims=True)

    def bit_body(b, carry):
        thr, cge = carry
        cand = thr + _bit_value(b)
        cand_f = _key_to_float(cand)
        cnt = count_keys(lambda s, kpos: s >= cand_f)
        ok = cnt >= float(TOPK_MAX)
        return jnp.where(ok, cand, thr), jnp.where(ok, cnt, cge)

    thr, cge = lax.fori_loop(0, 32, bit_body, (jnp.full((rows, 1), INT_MIN, I32),
                                               jnp.zeros((rows, 1), F32)))
    thr_f = _key_to_float(jnp.maximum(thr, KEY_LOWEST_FINITE))
    real = (lax.broadcasted_iota(I32, (rows, 1), 0) & 7) < n_new
    has_ties = jnp.max(jnp.where(real, cge, 0.0)) > float(TOPK_MAX)

    @pl.when(jnp.logical_not(has_ties))
    def _plain_mask():
        def to_bias(c, carry):
            off = pl.multiple_of(c * ch, ch)
            bp_ref[:, pl.ds(off, ch)] = jnp.where(sp_ref[:, pl.ds(off, ch)] >= thr_f, 0.0, NEG)
            return carry

        lax.fori_loop(0, nchunk, to_bias, 0)
        bn_ref[...] = jnp.where(sn >= thr_f, 0.0, NEG)

    @pl.when(has_ties)
    def _tie_mask():
        need = float(TOPK_MAX) - count_keys(lambda s, kpos: s > thr_f)
        nbits = (past + 128 - 1).bit_length()

        def pos_bit(b, last):
            step = lax.shift_left(jnp.int32(1), jnp.int32(nbits - 1) - b)
            probe = last + (step - 1)
            got = count_keys(lambda s, kpos: (s == thr_f) & (kpos <= probe))
            return jnp.where(got < need, last + step, last)

        last = lax.fori_loop(0, nbits, pos_bit, jnp.zeros((rows, 1), I32))

        def keep(s, kpos):
            return (s > thr_f) | ((s == thr_f) & (kpos <= last))

        def to_bias(c, carry):
            off = pl.multiple_of(c * ch, ch)
            bp_ref[:, pl.ds(off, ch)] = jnp.where(keep(sp_ref[:, pl.ds(off, ch)], lane_ch + c * ch), 0.0, NEG)
            return carry

        lax.fori_loop(0, nchunk, to_bias, 0)
        bn_ref[...] = jnp.where(keep(sn, lane + past), 0.0, NEG)


def _sample_attn_step(t, n_steps, slot, q_ref, bp_ref, bn_ref, knew_ref, vnew_ref, pbuf,
                      out_ref, s_ref, p_ref, l_ref, acc_ref):
    npg = pbuf.shape[1]
    nt = (((1,), (1,)), ((), ()))
    rg = GROUP * 8
    width = npg * PAGE_SIZE
    past = n_steps * width
    sm_chunk = 2048

    def head_rows(g):
        rows = [pbuf[slot, r, pl.ds(g, PAGE_SIZE, stride=N_KV_HEADS), :] for r in range(npg)]
        return jnp.concatenate(rows, axis=0).astype(BF16)

    @pl.when(t < n_steps)
    def _logits():
        off = pl.multiple_of(t * width, width)
        bias = jnp.concatenate([bp_ref[0]] * GROUP, axis=0)
        for g in range(N_KV_HEADS):
            qg = q_ref[0, g * rg:(g + 1) * rg, :]
            s_ref[g * rg:(g + 1) * rg, pl.ds(off, width)] = lax.dot_general(
                qg, head_rows(g), nt, preferred_element_type=F32) + bias

    @pl.when(t == n_steps - 1)
    def _softmax():
        biasn = jnp.concatenate([bn_ref[0]] * GROUP, axis=0)
        for g in range(N_KV_HEADS):
            qg = q_ref[0, g * rg:(g + 1) * rg, :]
            kg = knew_ref[0, :, g * HEAD_DIM:(g + 1) * HEAD_DIM]
            s_ref[g * rg:(g + 1) * rg, past:past + PAGE_SIZE] = lax.dot_general(
                qg, kg, nt, preferred_element_type=F32) + biasn

        def fold(x, op):
            f = x[:, 0:128]
            for q in range(1, x.shape[1] // 128):
                f = op(f, x[:, q * 128:(q + 1) * 128])
            return f

        def max_chunk(c, m):
            off = pl.multiple_of(c * sm_chunk, sm_chunk)
            return jnp.maximum(m, fold(s_ref[:, pl.ds(off, sm_chunk)], jnp.maximum))

        m = lax.fori_loop(0, past // sm_chunk, max_chunk, s_ref[:, past:past + PAGE_SIZE])
        m = jnp.max(m, axis=1, keepdims=True)

        def exp_chunk(c, l):
            off = pl.multiple_of(c * sm_chunk, sm_chunk)
            p = jnp.exp2(s_ref[:, pl.ds(off, sm_chunk)] - m)
            p_ref[:, pl.ds(off, sm_chunk)] = p.astype(BF16)
            return l + fold(p, jnp.add)

        pn = jnp.exp2(s_ref[:, past:past + PAGE_SIZE] - m)
        p_ref[:, past:past + PAGE_SIZE] = pn.astype(BF16)
        l_ref[...] = lax.fori_loop(0, past // sm_chunk, exp_chunk, pn)
        acc_ref[...] = jnp.zeros(acc_ref.shape, F32)

    @pl.when(t >= n_steps)
    def _values():
        off = pl.multiple_of((t - n_steps) * width, width)
        for g in range(N_KV_HEADS):
            rs = slice(g * rg, (g + 1) * rg)
            acc_ref[rs, :] += jnp.dot(p_ref[rs, pl.ds(off, width)], head_rows(g),
                                      preferred_element_type=F32)

    @pl.when(t == 2 * n_steps - 1)
    def _finish():
        for g in range(N_KV_HEADS):
            rs = slice(g * rg, (g + 1) * rg)
            vg = vnew_ref[0, :, g * HEAD_DIM:(g + 1) * HEAD_DIM]
            acc_ref[rs, :] += jnp.dot(p_ref[rs, past:past + PAGE_SIZE], vg, preferred_element_type=F32)
        out_ref[0] = acc_ref[...] / jnp.sum(l_ref[...], axis=1, keepdims=True)


def _sample_operands(qT, qiT, wT, kibf, kbf, v32, cache_kidx, n_seq, n_tok):
    def rows_ht(xT, n_heads, dim):
        x = xT.reshape(n_heads, dim, n_seq, n_tok).transpose(2, 0, 3, 1)
        x = jnp.pad(x, ((0, 0), (0, 0), (0, 8 - n_tok), (0, 0)))
        return x.reshape(n_seq, n_heads * 8, dim)

    qi_rows = rows_ht(qiT, N_IDX_HEADS, IDX_DIM)
    q_rows = rows_ht(qT, N_HEADS, HEAD_DIM)
    wcol = jnp.pad(wT.reshape(N_IDX_HEADS, n_seq, n_tok).transpose(1, 0, 2),
                   ((0, 0), (0, 0), (0, 8 - n_tok))).reshape(n_seq, N_IDX_HEADS * 8, 1)

    def pad_keys(x):
        x = x.reshape(n_seq, n_tok, x.shape[-1])
        return jnp.pad(x, ((0, 0), (0, PAGE_SIZE - n_tok), (0, 0)))

    ki_new_t = jnp.swapaxes(pad_keys(kibf[:, 0:IDX_DIM]), 1, 2)
    kidx_t = jnp.swapaxes(cache_kidx, 1, 2)
    k_new = pad_keys(kbf)
    v_new = pad_keys(v32.reshape(n_seq * n_tok, N_KV).astype(BF16))
    return (qi_rows, wcol, ki_new_t, kidx_t), (q_rows, k_new, v_new)


def _sample_topk(sp, sn, n_tok):
    n_seq, _, past = sp.shape
    rows = n_seq * 8
    bp, bn = pl.pallas_call(
        functools.partial(_s_thresh_body, n_tok),
        grid=(1,),
        in_specs=[pl.BlockSpec((rows, past), lambda i: (0, 0)),
                  pl.BlockSpec((rows, PAGE_SIZE), lambda i: (0, 0))],
        out_specs=[pl.BlockSpec((rows, past), lambda i: (0, 0)),
                   pl.BlockSpec((rows, PAGE_SIZE), lambda i: (0, 0))],
        out_shape=[jax.ShapeDtypeStruct((rows, past), F32),
                   jax.ShapeDtypeStruct((rows, PAGE_SIZE), F32)],
        compiler_params=_cparams(1),
        name="sample_topk_mask",
    )(sp.reshape(rows, past), sn.reshape(rows, PAGE_SIZE))
    return bp.reshape(n_seq, 8, past), bn.reshape(n_seq, 8, PAGE_SIZE)


def _sample_rows_to_tokens(out, n_seq, n_tok):
    out = out.reshape(n_seq, N_HEADS, 8, HEAD_DIM)[:, :, 0:n_tok]
    return out.transpose(0, 2, 1, 3).reshape(n_seq * n_tok, D_ATTN)


def _outproj_body(gated, *refs):
    if gated:
        x_ref, a_ref, gate_ref, mc_ref, wo_ref, o_ref = refs
        ma = (a_ref[...] * gate_ref[...]).astype(BF16)
    else:
        x_ref, a_ref, mc_ref, wo_ref, o_ref = refs
        ma = a_ref[...]
    acc = jnp.dot(ma, wo_ref[0:D_ATTN, :], preferred_element_type=F32)
    acc = acc + jnp.dot(mc_ref[...], wo_ref[D_ATTN:D_ATTN + D_CONV, :], preferred_element_type=F32)
    o_ref[...] = x_ref[...] + acc


def _outproj(x, attn, gate, mixc, wo, tm):
    m = x.shape[0]
    gated = gate is not None
    row = lambda i: (i, 0)
    in_specs = [pl.BlockSpec((tm, D_MODEL), row), pl.BlockSpec((tm, D_ATTN), row)]
    args = [x, attn]
    if gated:
        in_specs.append(pl.BlockSpec((tm, D_ATTN), row))
        args.append(gate)
    in_specs += [pl.BlockSpec((tm, D_CONV), row),
                 pl.BlockSpec((D_ATTN + D_CONV, D_MODEL), lambda i: (0, 0))]
    args += [mixc, wo]
    return pl.pallas_call(
        functools.partial(_outproj_body, gated),
        grid=(m // tm,),
        in_specs=in_specs,
        out_specs=pl.BlockSpec((tm, D_MODEL), row),
        out_shape=jax.ShapeDtypeStruct((m, D_MODEL), F32),
        compiler_params=_cparams(1),
        name="outproj_sample" if gated else "outproj_prompt",
    )(*args)


def _rope_tables(pos):
    posf = pos.astype(F32)[:, None]
    n = pos.shape[0]

    def cs(half):
        inv = ROPE_THETA ** (-jnp.arange(half, dtype=F32) / half)
        ang = posf * inv[None, :]
        return jnp.cos(ang), jnp.sin(ang)

    c16, s16 = cs(ROPE_HALF)
    c8, s8 = cs(IDX_ROPE_HALF)
    one = lambda w: jnp.ones((n, w), F32)
    zero = lambda w: jnp.zeros((n, w), F32)
    rest = HEAD_DIM - ROPE_DIM
    k_c = jnp.concatenate([c16, c16, one(rest)], axis=1)
    k_sa = jnp.concatenate([-s16, zero(HEAD_DIM - ROPE_HALF)], axis=1)
    k_sb = jnp.concatenate([zero(ROPE_HALF), s16, zero(rest)], axis=1)
    return dict(c16T=c16.T, s16T=s16.T, c8T=c8.T, s8T=s8.T, kC=k_c, kSa=k_sa, kSb=k_sb)


def _prep_weights(w_in, g_q, g_k, g_kidx, w_conv, w_out):
    assert w_in.shape == (D_MODEL, D_IN)
    wT = w_in.T
    gq = g_q.reshape(HEAD_DIM, 1)
    gk = g_k.reshape(1, HEAD_DIM)
    gki = g_kidx.reshape(IDX_DIM, 1)
    return wT, (gq, gk, gki, w_conv), w_out.astype(BF16)


def kernel(x_prompt, x_sample, cache_k, cache_v, cache_kidx, state_conv, page_table,
           norm_in, w_in, g_q, g_k, g_kidx, w_conv, w_out):
    n_b, seq, _ = x_prompt.shape
    n_s, n_t, _ = x_sample.shape
    depth = w_in.shape[0]
    past = page_table.shape[1] * PAGE_SIZE
    tabs_p = _rope_tables(jnp.arange(seq))
    tabs_s = _rope_tables(jnp.tile(past + jnp.arange(n_t), n_s))

    hp = x_prompt.reshape(n_b * seq, D_MODEL)
    hs = x_sample.reshape(n_s * n_t, D_MODEL)
    outs = [[] for _ in range(8)]
    for l in range(depth):
        w_t, params, wo = _prep_weights(w_in[l], g_q[l], g_k[l], g_kidx[l], w_conv[l], w_out[l])

        st = state_conv[l]
        tok = jnp.arange(n_t)
        e1 = st[:, jnp.full((n_t,), CONV_W - 2)].reshape(n_s * n_t, D_CONV)
        e2 = st[:, jnp.minimum(tok, CONV_W - 2)].reshape(n_s * n_t, D_CONV)
        (qT_s, qiT_s, _, wT_s, k32_s, kbf_s, v32_s, ki32_s, kibf_s, gate_s, mixc_s, u, *w_bf) = _project(
            hs, norm_in[l], w_t, params, tabs_s, n_t, state_rows=(e1, e2))
        idx_ops, (q_rows, k_new, v_new) = _sample_operands(
            qT_s, qiT_s, wT_s, kibf_s, kbf_s, v32_s, cache_kidx[l], n_s, n_t)
        (qT, qiT, vT, wT, k32, kbf, v32, ki32, kibf, gate, mixc, utail, sp, sn) = _project(
            hp, norm_in[l], tuple(w_bf), params, tabs_p, 0, side=(page_table,) + idx_ops)

        bp, bn = _sample_topk(sp, sn, n_t)
        mixa, attn_rows = _attention(qiT, wT, kibf, qT, kbf, vT, gate, n_b, seq,
                                     q_rows, bp, bn, k_new, v_new, cache_k[l], cache_v[l], page_table)
        attn_s = _sample_rows_to_tokens(attn_rows, n_s, n_t)

        hp = _outproj(hp, mixa, None, mixc, wo, ROW_TM)
        hs = _outproj(hs, attn_s, gate_s, mixc_s, wo, n_s * n_t)
        tps = seq // PROJ_TM
        outs[0].append(k32.reshape(n_b, seq, N_KV_HEADS, HEAD_DIM))
        outs[1].append(v32.reshape(n_b, seq, N_KV_HEADS, HEAD_DIM))
        outs[2].append(ki32.reshape(n_b, seq, IDX_DIM))
        outs[3].append(utail[tps - 1::tps])
        outs[4].append(k32_s.reshape(n_s, n_t, N_KV_HEADS, HEAD_DIM))
        outs[5].append(v32_s.reshape(n_s, n_t, N_KV_HEADS, HEAD_DIM))
        outs[6].append(ki32_s.reshape(n_s, n_t, IDX_DIM))
        outs[7].append(u.reshape(n_s, n_t, D_CONV)[:, n_t - (CONV_W - 1):])

    return (hp.reshape(n_b, seq, D_MODEL), hs.reshape(n_s, n_t, D_MODEL),
            *[jnp.stack(o) for o in outs])
```

```python
import functools

import jax
import jax.numpy as jnp
import numpy as np
from jax import lax
from jax.experimental import pallas as pl
from jax.experimental.pallas import tpu as pltpu

F32 = jnp.float32
BF16 = jnp.bfloat16
I32 = jnp.int32

D_MODEL = 2048
HEAD_DIM = 128
N_HEADS = 8
N_KV_HEADS = 2
GROUP = N_HEADS // N_KV_HEADS
D_ATTN = N_HEADS * HEAD_DIM
D_CONV = 1024
ROPE_DIM = HEAD_DIM // 4
ROPE_HALF = ROPE_DIM // 2
ROPE_THETA = 500000.0
N_IDX_HEADS = 16
IDX_DIM = 64
IDX_ROPE_HALF = IDX_DIM // 8
TOPK_MAX = 256
CONV_W = 3
PAGE_SIZE = 128
EPS = 1e-6
W_IDX_SCALE = (N_IDX_HEADS ** -0.5) * (IDX_DIM ** -0.5)
ATTN_SCALE = HEAD_DIM ** -0.5
Q_PRESCALE = ATTN_SCALE * 1.4426950408889634

INT_MIN = -(2 ** 31)
NEG = -1e30

VMEM_LIMIT_BYTES = 60 * 1024 * 1024

PROJ_TN = 512
PROJ_TM = 1024
PROJ_SUB = 256
ROW_TM = 512
N_KV = N_KV_HEADS * HEAD_DIM
OFF_Q = 0
OFF_K = OFF_Q + D_ATTN
OFF_V = OFF_K + N_KV
OFF_Z = OFF_V + N_KV
OFF_QI = OFF_Z + D_ATTN
OFF_KI = OFF_QI + N_IDX_HEADS * IDX_DIM
OFF_WI = OFF_KI + IDX_DIM
OFF_H = OFF_WI + N_IDX_HEADS
OFF_B = OFF_H + D_CONV
OFF_C = OFF_B + D_CONV
OFF_ZC = OFF_C + D_CONV
D_IN = OFF_ZC + D_CONV
assert OFF_K == 2 * PROJ_TN and OFF_Z == 3 * PROJ_TN and OFF_QI == 5 * PROJ_TN and OFF_KI == 7 * PROJ_TN
KIW_ROWS = 128
assert OFF_KI % KIW_ROWS == 0 and OFF_WI - OFF_KI == IDX_DIM
CONV_CHUNK = 256
N_CONV_CHUNKS = D_CONV // CONV_CHUNK
J_Q = 0
J_KV = 2
J_Z = 3
J_QI = 5
J_CONV = 7
N_MAIN_BLOCKS = J_CONV
N_STEPS = J_CONV + N_CONV_CHUNKS

ATT_TQ = 256
ATT_TK = 256
CNT_ACCS = 4

SIDE_PAGES = 16


def _cparams(n_axes):
    return pltpu.CompilerParams(
        dimension_semantics=("arbitrary",) * n_axes,
        vmem_limit_bytes=VMEM_LIMIT_BYTES,
    )


def _silu(x):
    return x * jax.nn.sigmoid(x)


def _indexer_rows_scores(qi, wcol, keys_t):
    d = jnp.dot(qi, keys_t, preferred_element_type=F32)
    val = jnp.maximum(d, 0.0) * wcol
    return val.reshape(N_IDX_HEADS, 8, val.shape[-1]).sum(axis=0)


class _SideScores:
    def __init__(self, step, n_total, spq, pt_ref, qi_ref, wcol_ref, kinew_ref, kidx_hbm, sp_ref, sn_ref,
                 kbuf, ksem):
        self.step, self.n_total, self.spq, self.pt_ref = step, n_total, spq, pt_ref
        self.qi_ref, self.wcol_ref, self.kinew_ref, self.kidx_hbm = qi_ref, wcol_ref, kinew_ref, kidx_hbm
        self.sp_ref, self.sn_ref, self.kbuf, self.ksem = sp_ref, sn_ref, kbuf, ksem
        self.npg = kbuf.shape[1]
        self.n_side = pt_ref.shape[0] * spq

    def _start(self, n):
        c = jnp.minimum(n, self.n_side - 1)
        seq_n = c // self.spq
        base = (c % self.spq) * self.npg
        for r in range(self.npg):
            _page_copy(self.kidx_hbm, self.pt_ref[seq_n, base + r], self.kbuf, n % 2, r, self.ksem).start()

    def prologue(self):
        pl.when(self.step == 0)(lambda: self._start(self.step))

    def __call__(self):
        pl.when(self.step + 1 < self.n_total)(lambda: self._start(self.step + 1))
        slot = self.step % 2
        for r in range(self.npg):
            _page_copy(self.kidx_hbm, 0, self.kbuf, slot, r, self.ksem).wait()
        qi = self.qi_ref[0]
        wcol = self.wcol_ref[0]
        keys_t = jnp.concatenate([self.kbuf[slot, r] for r in range(self.npg)], axis=1).astype(BF16)
        self.sp_ref[0] = _indexer_rows_scores(qi, wcol, keys_t)
        self.sn_ref[0] = _indexer_rows_scores(qi, wcol, self.kinew_ref[0])


class _Bf16View:
    def __init__(self, ref):
        self.ref = ref

    def __getitem__(self, idx):
        return self.ref[idx].astype(BF16)


def _proj_body(sample, tps, tm, side_spq, convert, *refs):
    if side_spq:
        pt_ref, refs = refs[0], refs[1:]
    (x_ref, gin_ref, wa_ref, wkiw_ref, wh_ref, wb_ref, wc_ref, wzc_ref, gq_ref, gk_ref, gki_ref,
     c16_ref, s16_ref, c8_ref, s8_ref,
     kc_ref, ksa_ref, ksb_ref, wconv_ref) = refs[:19]
    refs = refs[19:]
    if sample:
        e1_ref, e2_ref = refs[:2]
        refs = refs[2:]
    if side_spq:
        side_in, refs = refs[:4], refs[4:]
    (qT_ref, qiT_ref, vT_ref, wT_ref, k32_ref, kbf_ref, v32_ref, ki32_ref, kibf_ref,
     gate_ref, mixc_ref, u_ref) = refs[:12]
    refs = refs[12:]
    if convert:
        w_out, refs = refs[:6], refs[6:]
    if side_spq:
        side_out, refs = refs[:2], refs[2:]
    xn_ref = refs[0]
    if not sample:
        carry_ref = refs[1]

    i = pl.program_id(0)
    j = pl.program_id(1)
    nt = (((1,), (1,)), ((), ()))
    if side_spq:
        side = _SideScores(i * N_STEPS + j, pl.num_programs(0) * N_STEPS, side_spq, pt_ref,
                           *side_in, *side_out, *refs[2:4])
        side.prologue()
    else:
        side = lambda: None

    if convert:
        wa_o, kiw_o, wh_o, wb_o, wc_o, wzc_o = w_out

        @pl.when(j < N_MAIN_BLOCKS)
        def _():
            wa_o[...] = wa_ref[...].astype(BF16)

        @pl.when(j == J_KV)
        def _():
            kiw_o[...] = wkiw_ref[...].astype(BF16)

        @pl.when(j >= J_CONV)
        def _():
            for src, dst in ((wh_ref, wh_o), (wb_ref, wb_o), (wc_ref, wc_o), (wzc_ref, wzc_o)):
                dst[...] = src[...].astype(BF16)

        wa_ref, wkiw_ref, wh_ref, wb_ref, wc_ref, wzc_ref = (
            _Bf16View(r) for r in (wa_ref, wkiw_ref, wh_ref, wb_ref, wc_ref, wzc_ref))

    def xw(w, x=None):
        return lax.dot_general(xn_ref[...] if x is None else x, w, nt, preferred_element_type=F32)

    def wx(w, x):
        return lax.dot_general(w, x, nt, preferred_element_type=F32)

    nsub = max(tm // PROJ_SUB, 1)
    sub = tm // nsub
    pieces = [slice(rb * sub, (rb + 1) * sub) for rb in range(nsub)]

    if not sample:
        @pl.when((i == 0) & (j == 0))
        def _init():
            carry_ref[...] = jnp.zeros(carry_ref.shape, F32)

    def q_step(with_norm):
        side()
        for cs in pieces:
            if with_norm:
                x = x_ref[cs, :]
                ms = jnp.mean(x * x, axis=-1, keepdims=True)
                xn_ref[cs, :] = (x * lax.rsqrt(ms + EPS) * gin_ref[...]).astype(BF16)
            res = wx(wa_ref[...], xn_ref[cs, :])
            cos = c16_ref[:, cs]
            sin = s16_ref[:, cs]
            for hh in range(PROJ_TN // HEAD_DIM):
                blk = res[hh * HEAD_DIM:(hh + 1) * HEAD_DIM]
                ms = jnp.mean(blk * blk, axis=0, keepdims=True)
                y = blk * lax.rsqrt(ms + EPS) * gq_ref[...]
                x1 = y[0:ROPE_HALF]
                x2 = y[ROPE_HALF:ROPE_DIM]
                base = hh * HEAD_DIM
                qT_ref[base:base + ROPE_HALF, cs] = ((x1 * cos - x2 * sin) * Q_PRESCALE).astype(BF16)
                qT_ref[base + ROPE_HALF:base + ROPE_DIM, cs] = ((x2 * cos + x1 * sin) * Q_PRESCALE).astype(BF16)
                qT_ref[base + ROPE_DIM:base + HEAD_DIM, cs] = (y[ROPE_DIM:] * Q_PRESCALE).astype(BF16)

    pl.when(j == J_Q)(functools.partial(q_step, True))
    pl.when((j > J_Q) & (j < J_KV))(functools.partial(q_step, False))

    @pl.when((j >= J_QI) & (j < J_CONV))
    def _qi():
        side()
        for cs in pieces:
            res = wx(wa_ref[...], xn_ref[cs, :])
            cos = c8_ref[:, cs]
            sin = s8_ref[:, cs]
            for hh in range(PROJ_TN // IDX_DIM):
                blk = res[hh * IDX_DIM:(hh + 1) * IDX_DIM]
                x1 = blk[0:IDX_ROPE_HALF]
                x2 = blk[IDX_ROPE_HALF:2 * IDX_ROPE_HALF]
                rot = jnp.concatenate([x1 * cos - x2 * sin, x2 * cos + x1 * sin], axis=0)
                base = hh * IDX_DIM
                qiT_ref[base:base + 2 * IDX_ROPE_HALF, cs] = rot.astype(BF16)
                qiT_ref[base + 2 * IDX_ROPE_HALF:base + IDX_DIM, cs] = blk[2 * IDX_ROPE_HALF:].astype(BF16)

    @pl.when(j == J_KV)
    def _kv():
        side()
        for rb, cs in enumerate(pieces):
            x = xn_ref[cs, :]
            res = xw(wa_ref[...], x)
            for hd in range(N_KV_HEADS):
                hs = slice(hd * HEAD_DIM, (hd + 1) * HEAD_DIM)
                blk = res[:, hs]
                ms = jnp.mean(blk * blk, axis=-1, keepdims=True)
                yk = blk * lax.rsqrt(ms + EPS) * gk_ref[...]
                rot = yk * kc_ref[cs, :] + (pltpu.roll(yk, HEAD_DIM - ROPE_HALF, axis=1) * ksa_ref[cs, :]
                                            + pltpu.roll(yk, ROPE_HALF, axis=1) * ksb_ref[cs, :])
                k32_ref[pl.ds(N_KV_HEADS * rb * sub + hd, sub, stride=N_KV_HEADS), :] = rot
                kbf_ref[cs, hs] = rot.astype(BF16)
            v = res[:, N_KV:2 * N_KV]
            for hd in range(N_KV_HEADS):
                v32_ref[pl.ds(N_KV_HEADS * rb * sub + hd, sub, stride=N_KV_HEADS), :] = (
                    v[:, hd * HEAD_DIM:(hd + 1) * HEAD_DIM])
            if sample:
                vT_ref[:, cs] = wx(wa_ref[N_KV:2 * N_KV, :], x).astype(BF16)
            else:
                vT_ref[:, cs] = v.T.astype(BF16)
            r2t = wx(wkiw_ref[...], x)
            wT_ref[:, cs] = r2t[IDX_DIM:IDX_DIM + N_IDX_HEADS] * W_IDX_SCALE
            kit = r2t[0:IDX_DIM]
            ms = jnp.mean(kit * kit, axis=0, keepdims=True)
            yi = kit * lax.rsqrt(ms + EPS) * gki_ref[...]
            x1 = yi[0:IDX_ROPE_HALF]
            x2 = yi[IDX_ROPE_HALF:2 * IDX_ROPE_HALF]
            cos = c8_ref[:, cs]
            sin = s8_ref[:, cs]
            roti = jnp.concatenate([x1 * cos - x2 * sin, x2 * cos + x1 * sin, yi[2 * IDX_ROPE_HALF:],
                                    jnp.zeros((KIW_ROWS - IDX_DIM, sub), F32)], axis=0)
            ki_nat = roti.T
            ki32_ref[cs, :] = ki_nat[:, 0:IDX_DIM]
            kibf_ref[cs, :] = ki_nat.astype(BF16)

    @pl.when((j >= J_Z) & (j < J_QI))
    def _z():
        side()
        for cs in pieces:
            gate_ref[cs, :] = _silu(xw(wa_ref[...], xn_ref[cs, :]))

    @pl.when(j >= J_CONV)
    def _conv():
        side()
        cc = j - J_CONV
        u = xw(wc_ref[...]) * xw(wh_ref[...])
        rowid = lax.broadcasted_iota(I32, (tm, CONV_CHUNK), 0)
        if sample:
            t = rowid & (sample - 1)
            u1 = jnp.where(t >= 1, pltpu.roll(u, 1, axis=0), e1_ref[...])
            u2 = jnp.where(t >= 2, pltpu.roll(u, 2, axis=0), e2_ref[...])
            u_ref[...] = u
        else:
            first = (i % tps) == 0
            prev = carry_ref[cc]
            p0 = jnp.where(first, 0.0, prev[0:1])
            p1 = jnp.where(first, 0.0, prev[1:2])
            u1 = jnp.where(rowid == 0, p1, pltpu.roll(u, 1, axis=0))
            u2 = jnp.where(rowid == 0, p0, jnp.where(rowid == 1, p1, pltpu.roll(u, 2, axis=0)))
            tail = u[tm - 8:tm]
            carry_ref[cc] = jnp.concatenate([tail[6:8], tail[0:6]], axis=0)
            u_ref[0] = tail[6:8]
        w = wconv_ref[...]
        y = u2 * w[0:1] + u1 * w[1:2] + u * w[2:3]
        mixc_ref[...] = (xw(wb_ref[...]) * y * _silu(xw(wzc_ref[...]))).astype(BF16)


def _project(x, norm_g, wts, params, tabs, sample, state_rows=None, side=None):
    m = x.shape[0]
    convert = not isinstance(wts, tuple)
    assert sample & (sample - 1) == 0
    tm = m if sample else PROJ_TM
    n_i = m // tm
    tps = 1 if sample else (tabs["c16T"].shape[1] // tm)
    gq, gk, gki, wconv = params

    def tmap(i):
        return i % tps

    def cchunk(j):
        return jnp.clip(j - J_CONV, 0, N_CONV_CHUNKS - 1)

    main_spec = pl.BlockSpec((PROJ_TN, D_MODEL), lambda i, j, *_: (jnp.minimum(j, N_MAIN_BLOCKS - 1), 0))
    if convert:
        def conv_rows(off):
            return pl.BlockSpec((pl.Element(CONV_CHUNK), pl.Element(D_MODEL)),
                                lambda i, j, *_: (pl.multiple_of(off + CONV_CHUNK * cchunk(j), 16), 0))

        w_specs = [main_spec, pl.BlockSpec((KIW_ROWS, D_MODEL), lambda i, j, *_: (OFF_KI // KIW_ROWS, 0)),
                   conv_rows(OFF_H), conv_rows(OFF_B), conv_rows(OFF_C), conv_rows(OFF_ZC)]
        w_args = [wts] * 6
    else:
        conv_spec = pl.BlockSpec(
            (CONV_CHUNK, D_MODEL),
            lambda i, j, *_: (jnp.where(j < J_QI, N_CONV_CHUNKS - 1, cchunk(j)), 0))
        w_specs = [main_spec, pl.BlockSpec((KIW_ROWS, D_MODEL), lambda i, j, *_: (0, 0))] + [conv_spec] * 4
        w_args = list(wts)

    in_specs = [
        pl.BlockSpec((tm, D_MODEL), lambda i, j, *_: (i, 0)),
        pl.BlockSpec((1, D_MODEL), lambda i, j, *_: (0, 0)),
        *w_specs,
        pl.BlockSpec((HEAD_DIM, 1), lambda i, j, *_: (0, 0)),
        pl.BlockSpec((1, HEAD_DIM), lambda i, j, *_: (0, 0)),
        pl.BlockSpec((IDX_DIM, 1), lambda i, j, *_: (0, 0)),
        pl.BlockSpec((ROPE_HALF, tm), lambda i, j, *_: (0, tmap(i))),
        pl.BlockSpec((ROPE_HALF, tm), lambda i, j, *_: (0, tmap(i))),
        pl.BlockSpec((IDX_ROPE_HALF, tm), lambda i, j, *_: (0, tmap(i))),
        pl.BlockSpec((IDX_ROPE_HALF, tm), lambda i, j, *_: (0, tmap(i))),
        pl.BlockSpec((tm, HEAD_DIM), lambda i, j, *_: (tmap(i), 0)),
        pl.BlockSpec((tm, HEAD_DIM), lambda i, j, *_: (tmap(i), 0)),
        pl.BlockSpec((tm, HEAD_DIM), lambda i, j, *_: (tmap(i), 0)),
        pl.BlockSpec((CONV_W, CONV_CHUNK), lambda i, j, *_: (0, cchunk(j))),
    ]
    args = [x, norm_g.reshape(1, D_MODEL), *w_args, gq, gk, gki,
            tabs["c16T"], tabs["s16T"], tabs["c8T"], tabs["s8T"],
            tabs["kC"], tabs["kSa"], tabs["kSb"], wconv]
    cmap = lambda i, j, *_: (i, cchunk(j))
    if sample:
        in_specs += [pl.BlockSpec((tm, CONV_CHUNK), cmap), pl.BlockSpec((tm, CONV_CHUNK), cmap)]
        args += list(state_rows)
        u_spec = pl.BlockSpec((tm, CONV_CHUNK), cmap)
        u_shape = jax.ShapeDtypeStruct((m, D_CONV), F32)
        scratch = [pltpu.VMEM((tm, D_MODEL), BF16)]
    else:
        u_spec = pl.BlockSpec((1, CONV_W - 1, CONV_CHUNK), lambda i, j, *_: (i, 0, cchunk(j)))
        u_shape = jax.ShapeDtypeStruct((n_i, CONV_W - 1, D_CONV), F32)
        scratch = [pltpu.VMEM((tm, D_MODEL), BF16), pltpu.VMEM((N_CONV_CHUNKS, 8, CONV_CHUNK), F32)]

    out_specs = [
        pl.BlockSpec((PROJ_TN, tm), lambda i, j, *_: (jnp.minimum(j, 1), i)),
        pl.BlockSpec((PROJ_TN, tm), lambda i, j, *_: (jnp.clip(j - J_QI, 0, 1), i)),
        pl.BlockSpec((N_KV, tm), lambda i, j, *_: (0, i)),
        pl.BlockSpec((N_IDX_HEADS, tm), lambda i, j, *_: (0, i)),
        pl.BlockSpec((N_KV_HEADS * tm, HEAD_DIM), lambda i, j, *_: (i, 0)),
        pl.BlockSpec((tm, N_KV), lambda i, j, *_: (i, 0)),
        pl.BlockSpec((N_KV_HEADS * tm, HEAD_DIM), lambda i, j, *_: (i, 0)),
        pl.BlockSpec((tm, IDX_DIM), lambda i, j, *_: (i, 0)),
        pl.BlockSpec((tm, KIW_ROWS), lambda i, j, *_: (i, 0)),
        pl.BlockSpec((tm, PROJ_TN), lambda i, j, *_: (i, jnp.clip(j - J_Z, 0, 1))),
        pl.BlockSpec((tm, CONV_CHUNK), cmap),
        u_spec,
    ]
    out_shape = [
        jax.ShapeDtypeStruct((D_ATTN, m), BF16),
        jax.ShapeDtypeStruct((N_IDX_HEADS * IDX_DIM, m), BF16),
        jax.ShapeDtypeStruct((N_KV, m), BF16),
        jax.ShapeDtypeStruct((N_IDX_HEADS, m), F32),
        jax.ShapeDtypeStruct((N_KV_HEADS * m, HEAD_DIM), F32),
        jax.ShapeDtypeStruct((m, N_KV), BF16),
        jax.ShapeDtypeStruct((N_KV_HEADS * m, HEAD_DIM), F32),
        jax.ShapeDtypeStruct((m, IDX_DIM), F32),
        jax.ShapeDtypeStruct((m, KIW_ROWS), BF16),
        jax.ShapeDtypeStruct((m, D_ATTN), F32),
        jax.ShapeDtypeStruct((m, D_CONV), BF16),
        u_shape,
    ]
    if convert:
        assert n_i == 1, "every weight block must be visited exactly once to be written back"
        out_specs += [
            main_spec,
            pl.BlockSpec((KIW_ROWS, D_MODEL), lambda i, j, *_: (0, 0)),
        ] + [pl.BlockSpec((CONV_CHUNK, D_MODEL), lambda i, j, *_: (cchunk(j), 0))] * 4
        out_shape += [jax.ShapeDtypeStruct((OFF_KI, D_MODEL), BF16),
                      jax.ShapeDtypeStruct((KIW_ROWS, D_MODEL), BF16)]
        out_shape += [jax.ShapeDtypeStruct((D_CONV, D_MODEL), BF16)] * 4
    if side is None:
        return pl.pallas_call(
            functools.partial(_proj_body, sample, tps, tm, 0, convert),
            grid=(n_i, N_STEPS),
            in_specs=in_specs,
            out_specs=out_specs,
            out_shape=out_shape,
            scratch_shapes=scratch,
            compiler_params=_cparams(2),
            name="proj_sample" if sample else "proj_prompt",
        )(*args)

    page_table, qi_rows, wcol, ki_new_t, kidx_t = side
    n_seq, n_pages = page_table.shape
    spq = n_pages // SIDE_PAGES
    n_side = n_seq * spq
    assert spq * SIDE_PAGES == n_pages and n_side <= n_i * N_STEPS
    width = SIDE_PAGES * PAGE_SIZE

    def side_step(i, j):
        return jnp.minimum(i * N_STEPS + j, n_side - 1)

    seq_map = lambda i, j, *_: (side_step(i, j) // spq, 0, 0)
    in_specs += [
        pl.BlockSpec((1, N_IDX_HEADS * 8, IDX_DIM), seq_map),
        pl.BlockSpec((1, N_IDX_HEADS * 8, 1), seq_map),
        pl.BlockSpec((1, IDX_DIM, PAGE_SIZE), seq_map),
        pl.BlockSpec(memory_space=pl.ANY),
    ]
    out_specs += [
        pl.BlockSpec((1, 8, width), lambda i, j, *_: (side_step(i, j) // spq, 0, side_step(i, j) % spq)),
        pl.BlockSpec((1, 8, PAGE_SIZE), seq_map),
    ]
    out_shape += [jax.ShapeDtypeStruct((n_seq, 8, n_pages * PAGE_SIZE), F32),
                  jax.ShapeDtypeStruct((n_seq, 8, PAGE_SIZE), F32)]
    scratch += [pltpu.VMEM((2, SIDE_PAGES, IDX_DIM, PAGE_SIZE), F32), pltpu.SemaphoreType.DMA((2,))]
    return pl.pallas_call(
        functools.partial(_proj_body, sample, tps, tm, spq, convert),
        grid_spec=pltpu.PrefetchScalarGridSpec(
            num_scalar_prefetch=1,
            grid=(n_i, N_STEPS),
            in_specs=in_specs,
            out_specs=out_specs,
            scratch_shapes=scratch,
        ),
        out_shape=out_shape,
        compiler_params=_cparams(2),
        name="proj_prompt",
    )(page_table, *args, qi_rows, wcol, ki_new_t, kidx_t)


KEY_LOWEST_FINITE = INT_MIN + 0x00800000


def _key_to_float(key):
    return pltpu.bitcast(key ^ ((key >> 31) & 0x7FFFFFFF), F32)


def _bit_value(b):
    return lax.shift_left(jnp.int32(1), jnp.int32(31) - b)


def _prompt_attn_step(i, qiT_ref, wT_ref, ki_ref, qT_ref, k_ref, vT_ref, gate_ref, o_ref,
                      sc_ref, bias_ref, acc_ref, s_ref):
    tq, tk = ATT_TQ, ATT_TK
    nch = i + 1
    w = wT_ref[...]
    row = lax.broadcasted_iota(I32, (tk, tq), 0)
    col = lax.broadcasted_iota(I32, (tk, tq), 1)
    row8 = lax.broadcasted_iota(I32, (8, tq), 0)

    def score_chunk(j, carry):
        off = pl.multiple_of(j * tk, tk)
        kic = ki_ref[pl.ds(off, tk), 0:IDX_DIM]
        acc = jnp.zeros((tk, tq), F32)
        for h in range(N_IDX_HEADS):
            d = jnp.dot(kic, qiT_ref[h * IDX_DIM:(h + 1) * IDX_DIM, :], preferred_element_type=F32)
            acc = acc + w[h:h + 1, :] * jnp.maximum(d, 0.0)
        future = (row + j * tk) > (col + i * tq)
        sc_ref[pl.ds(off, tk), :] = jnp.where(future, -jnp.inf, acc)
        return carry

    lax.fori_loop(0, nch, score_chunk, 0)

    def count_keys(pred):
        def cnt_chunk(j, cs):
            off = pl.multiple_of(j * tk, tk)
            cs = list(cs)
            sc = sc_ref[pl.ds(off, tk), :]
            for r in range(tk // 8):
                a = cs[r % CNT_ACCS]
                kpos = row8 + (j * tk + r * 8)
                cs[r % CNT_ACCS] = jnp.where(pred(sc[r * 8:(r + 1) * 8], kpos), a + 1, a)
            return tuple(cs)

        cs = lax.fori_loop(0, nch, cnt_chunk, (jnp.zeros((8, tq), I32),) * CNT_ACCS)
        c = cs[0]
        for a in cs[1:]:
            c = c + a
        return jnp.sum(c.astype(F32), axis=0, keepdims=True)

    def bit_body(b, carry):
        thr, cge = carry
        cand = thr + _bit_value(b)
        cand_f = _key_to_float(cand)
        cnt = count_keys(lambda s, kpos: s >= cand_f)
        ok = cnt >= float(TOPK_MAX)
        return jnp.where(ok, cand, thr), jnp.where(ok, cnt, cge)

    thr, cge = lax.fori_loop(0, 32, bit_body, (jnp.full((1, tq), INT_MIN, I32),
                                               jnp.zeros((1, tq), F32)))
    thr_f = _key_to_float(jnp.maximum(thr, KEY_LOWEST_FINITE))
    has_ties = jnp.max(cge) > float(TOPK_MAX)

    @pl.when(jnp.logical_not(has_ties))
    def _plain_mask():
        def bias_chunk(j, carry):
            off = pl.multiple_of(j * tk, tk)
            bias_ref[pl.ds(off, tk), :] = jnp.where(sc_ref[pl.ds(off, tk), :] >= thr_f, 0.0, NEG)
            return carry

        lax.fori_loop(0, nch, bias_chunk, 0)

    @pl.when(has_ties)
    def _tie_mask():
        need = float(TOPK_MAX) - count_keys(lambda s, kpos: s > thr_f)
        nbits = (sc_ref.shape[0] - 1).bit_length()

        def pos_bit(b, last):
            step = lax.shift_left(jnp.int32(1), jnp.int32(nbits - 1) - b)
            probe = last + (step - 1)
            got = count_keys(lambda s, kpos: (s == thr_f) & (kpos <= probe))
            return jnp.where(got < need, last + step, last)

        last = lax.fori_loop(0, nbits, pos_bit, jnp.zeros((1, tq), I32))

        def bias_chunk(j, carry):
            off = pl.multiple_of(j * tk, tk)
            sc = sc_ref[pl.ds(off, tk), :]
            keep = (sc > thr_f) | ((sc == thr_f) & ((row + j * tk) <= last))
            bias_ref[pl.ds(off, tk), :] = jnp.where(keep, 0.0, NEG)
            return carry

        lax.fori_loop(0, nch, bias_chunk, 0)

    def qk_chunk(j, mrun):
        off = pl.multiple_of(j * tk, tk)
        bias = bias_ref[pl.ds(off, tk), :]
        out = []
        for h in range(N_HEADS):
            g = h // GROUP
            kc = k_ref[pl.ds(off, tk), g * HEAD_DIM:(g + 1) * HEAD_DIM]
            s = jnp.dot(kc, qT_ref[h * HEAD_DIM:(h + 1) * HEAD_DIM, :], preferred_element_type=F32) + bias
            s_ref[h, pl.ds(off, tk), :] = s
            out.append(jnp.maximum(mrun[h], s.reshape(tk // 8, 8, tq).max(axis=0)))
        return tuple(out)

    mrun = lax.fori_loop(0, nch, qk_chunk, (jnp.full((8, tq), NEG, F32),) * N_HEADS)
    ms = [jnp.max(mr, axis=0, keepdims=True) for mr in mrun]
    acc_ref[...] = jnp.zeros(acc_ref.shape, F32)

    def pv_chunk(j, lrun):
        off = pl.multiple_of(j * tk, tk)
        out = []
        for h in range(N_HEADS):
            g = h // GROUP
            hs = slice(h * HEAD_DIM, (h + 1) * HEAD_DIM)
            p = jnp.exp2(s_ref[h, pl.ds(off, tk), :] - ms[h])
            out.append(lrun[h] + p.reshape(tk // 8, 8, tq).sum(axis=0))
            vc = vT_ref[g * HEAD_DIM:(g + 1) * HEAD_DIM, pl.ds(off, tk)]
            acc_ref[hs, :] += jnp.dot(vc, p.astype(BF16), preferred_element_type=F32)
        return tuple(out)

    lrun = lax.fori_loop(0, nch, pv_chunk, (jnp.zeros((8, tq), F32),) * N_HEADS)

    for h in range(N_HEADS):
        hs = slice(h * HEAD_DIM, (h + 1) * HEAD_DIM)
        o = (acc_ref[hs, :] / jnp.sum(lrun[h], axis=0, keepdims=True)).T
        o_ref[:, hs] = (o * gate_ref[:, hs]).astype(BF16)


def _page_copy(cache_hbm, page, pbuf, slot, r, sem):
    return pltpu.make_async_copy(cache_hbm.at[page], pbuf.at[slot, r], sem.at[slot])


def _attn_fused_body(nq, spq, pt_ref, *refs):
    prompt_in = refs[0:7]
    q_ref, bp_ref, bn_ref, knew_ref, vnew_ref, ck_hbm, cv_hbm = refs[7:14]
    o_ref, so_ref = refs[14:16]
    prompt_scratch = refs[16:20]
    pbuf, sem, ss_ref, ps_ref, ls_ref, accs_ref = refs[20:26]

    i = pl.program_id(1)
    step = pl.program_id(0) * nq + i
    n_total = pl.num_programs(0) * nq
    half = spq // 2
    npg = pbuf.shape[1]

    def start_pages(n):
        seq_n = n // spq
        t_n = n % spq
        base = (t_n % half) * npg
        slot = n % 2
        for cache_hbm, pred in ((ck_hbm, t_n < half), (cv_hbm, t_n >= half)):
            @pl.when(pred)
            def _(cache_hbm=cache_hbm):
                for r in range(npg):
                    _page_copy(cache_hbm, pt_ref[seq_n, base + r], pbuf, slot, r, sem).start()

    @pl.when(step == 0)
    def _prologue():
        start_pages(step)

    @pl.when(step + 1 < n_total)
    def _prefetch():
        start_pages(step + 1)

    slot = step % 2
    for r in range(npg):
        _page_copy(ck_hbm, 0, pbuf, slot, r, sem).wait()

    _sample_attn_step(step % spq, half, slot, q_ref, bp_ref, bn_ref, knew_ref, vnew_ref, pbuf,
                      so_ref, ss_ref, ps_ref, ls_ref, accs_ref)
    _prompt_attn_step(i, *prompt_in, o_ref, *prompt_scratch)


def _attention(qiT, wT, kibf, qT, kbf, vT, gate, n_batch, seq,
               q_rows, bp, bn, k_new, v_new, cache_k, cache_v, page_table):
    m = n_batch * seq
    nq = seq // ATT_TQ
    n_seq, n_pages = page_table.shape
    n_steps = n_batch * nq
    spq = n_steps // n_seq
    assert spq * n_seq == n_steps and spq % 2 == 0
    half = spq // 2
    npg = n_pages // half
    assert npg * half == n_pages
    width = npg * PAGE_SIZE
    past = n_pages * PAGE_SIZE
    n_pool = cache_k.shape[0]
    kv_rows = PAGE_SIZE * N_KV_HEADS
    ck = cache_k.reshape(n_pool, kv_rows, HEAD_DIM)
    cv = cache_v.reshape(n_pool, kv_rows, HEAD_DIM)
    nrow = N_HEADS * 8

    qmap = lambda b, i, pt: (0, b * nq + i)
    smap = lambda b, i, pt: ((b * nq + i) // spq, 0, 0)
    return pl.pallas_call(
        functools.partial(_attn_fused_body, nq, spq),
        grid_spec=pltpu.PrefetchScalarGridSpec(
            num_scalar_prefetch=1,
            grid=(n_batch, nq),
            in_specs=[
                pl.BlockSpec((N_IDX_HEADS * IDX_DIM, ATT_TQ), qmap),
                pl.BlockSpec((N_IDX_HEADS, ATT_TQ), qmap),
                pl.BlockSpec((seq, 128), lambda b, i, pt: (b, 0)),
                pl.BlockSpec((D_ATTN, ATT_TQ), qmap),
                pl.BlockSpec((seq, N_KV), lambda b, i, pt: (b, 0)),
                pl.BlockSpec((N_KV, seq), lambda b, i, pt: (0, b)),
                pl.BlockSpec((ATT_TQ, D_ATTN), lambda b, i, pt: (b * nq + i, 0)),
                pl.BlockSpec((1, nrow, HEAD_DIM), smap),
                pl.BlockSpec((1, 8, width),
                             lambda b, i, pt: ((b * nq + i) // spq, 0, jnp.minimum((b * nq + i) % spq, half - 1))),
                pl.BlockSpec((1, 8, PAGE_SIZE), smap),
                pl.BlockSpec((1, PAGE_SIZE, N_KV), smap),
                pl.BlockSpec((1, PAGE_SIZE, N_KV), smap),
                pl.BlockSpec(memory_space=pl.ANY),
                pl.BlockSpec(memory_space=pl.ANY),
            ],
            out_specs=[
                pl.BlockSpec((ATT_TQ, D_ATTN), lambda b, i, pt: (b * nq + i, 0)),
                pl.BlockSpec((1, nrow, HEAD_DIM), smap),
            ],
            scratch_shapes=[
                pltpu.VMEM((seq, ATT_TQ), F32), pltpu.VMEM((seq, ATT_TQ), F32),
                pltpu.VMEM((D_ATTN, ATT_TQ), F32), pltpu.VMEM((N_HEADS, seq, ATT_TQ), F32),
                pltpu.VMEM((2, npg, kv_rows, HEAD_DIM), F32), pltpu.SemaphoreType.DMA((2,)),
                pltpu.VMEM((nrow, past + PAGE_SIZE), F32), pltpu.VMEM((nrow, past + PAGE_SIZE), BF16),
                pltpu.VMEM((nrow, 128), F32), pltpu.VMEM((nrow, HEAD_DIM), F32),
            ],
        ),
        out_shape=[jax.ShapeDtypeStruct((m, D_ATTN), BF16),
                   jax.ShapeDtypeStruct((n_seq, nrow, HEAD_DIM), F32)],
        compiler_params=_cparams(2),
        name="attention",
    )(page_table, qiT, wT, kibf, qT, kbf, vT, gate, q_rows, bp, bn, k_new, v_new, ck, cv)


def _s_thresh_body(n_new, in_p_ref, in_n_ref, out_p_ref, out_n_ref, sp_ref):
    past = in_p_ref.shape[1]
    rows = sp_ref.shape[0]
    pack = 8 // n_new
    groups = rows // 8
    ch = 2048
    nchunk = past // ch

    def pack_rows(src_ref, cols, ncols):
        which = lax.broadcasted_iota(I32, (8, ncols), 0) // n_new
        out = []
        for p in range(groups):
            dense = src_ref[(p * pack) * 8:(p * pack) * 8 + 8, cols]
            for q in range(1, pack):
                piece = src_ref[(p * pack + q) * 8:(p * pack + q) * 8 + 8, cols]
                dense = jnp.where(which == q, pltpu.roll(piece, q * n_new, axis=0), dense)
            out.append(dense)
        return jnp.concatenate(out, axis=0)

    def unpack_rows(dst_ref, cols, dense):
        for p in range(groups):
            piece = dense[p * 8:(p + 1) * 8]
            for q in range(pack):
                rolled = piece if q == 0 else pltpu.roll(piece, 8 - q * n_new, axis=0)
                dst_ref[(p * pack + q) * 8:(p * pack + q) * 8 + 8, cols] = rolled

    def pack_chunk(c, carry):
        cols = pl.ds(pl.multiple_of(c * ch, ch), ch)
        sp_ref[:, cols] = pack_rows(in_p_ref, cols, ch)
        return carry

    lax.fori_loop(0, nchunk, pack_chunk, 0)
    t = lax.broadcasted_iota(I32, (rows, 128), 0) % n_new
    lane = lax.broadcasted_iota(I32, (rows, 128), 1)
    sn = jnp.where((lane < n_new) & (lane <= t), pack_rows(in_n_ref, slice(0, 128), 128), -jnp.inf)

    def fold(x):
        f = x[:, 0:128]
        for q in range(1, x.shape[1] // 128):
            f = f + x[:, q * 128:(q + 1) * 128]
        return f

    lane_ch = lax.broadcasted_iota(I32, (rows, ch), 1)

    def count_keys(pred):
        def cnt_chunk(c, acc):
            off = pl.multiple_of(c * ch, ch)
            return acc + fold(jnp.where(pred(sp_ref[:, pl.ds(off, ch)], lane_ch + c * ch), 1.0, 0.0))

        acc = lax.fori_loop(0, nchunk, cnt_chunk, jnp.where(pred(sn, lane + past), 1.0, 0.0))
        return jnp.sum(acc, axis=1, keepdims=True)

    def bit_body(b, carry):
        thr, cge = carry
        cand = thr + _bit_value(b)
        cand_f = _key_to_float(cand)
        cnt = count_keys(lambda s, kpos: s >= cand_f)
        ok = cnt >= float(TOPK_MAX)
        return jnp.where(ok, cand, thr), jnp.where(ok, cnt, cge)

    thr, cge = lax.fori_loop(0, 32, bit_body, (jnp.full((rows, 1), INT_MIN, I32),
                                               jnp.zeros((rows, 1), F32)))
    thr_f = _key_to_float(jnp.maximum(thr, KEY_LOWEST_FINITE))
    has_ties = jnp.max(cge) > float(TOPK_MAX)

    @pl.when(jnp.logical_not(has_ties))
    def _plain_mask():
        def to_bias(c, carry):
            cols = pl.ds(pl.multiple_of(c * ch, ch), ch)
            unpack_rows(out_p_ref, cols, jnp.where(sp_ref[:, cols] >= thr_f, 0.0, NEG))
            return carry

        lax.fori_loop(0, nchunk, to_bias, 0)
        unpack_rows(out_n_ref, slice(0, 128), jnp.where(sn >= thr_f, 0.0, NEG))

    @pl.when(has_ties)
    def _tie_mask():
        need = float(TOPK_MAX) - count_keys(lambda s, kpos: s > thr_f)
        nbits = (past + 128 - 1).bit_length()

        def pos_bit(b, last):
            step = lax.shift_left(jnp.int32(1), jnp.int32(nbits - 1) - b)
            probe = last + (step - 1)
            got = count_keys(lambda s, kpos: (s == thr_f) & (kpos <= probe))
            return jnp.where(got < need, last + step, last)

        last = lax.fori_loop(0, nbits, pos_bit, jnp.zeros((rows, 1), I32))

        def keep(s, kpos):
            return (s > thr_f) | ((s == thr_f) & (kpos <= last))

        def to_bias(c, carry):
            cols = pl.ds(pl.multiple_of(c * ch, ch), ch)
            unpack_rows(out_p_ref, cols, jnp.where(keep(sp_ref[:, cols], lane_ch + c * ch), 0.0, NEG))
            return carry

        lax.fori_loop(0, nchunk, to_bias, 0)
        unpack_rows(out_n_ref, slice(0, 128), jnp.where(keep(sn, lane + past), 0.0, NEG))


def _sample_attn_step(t, n_steps, slot, q_ref, bp_ref, bn_ref, knew_ref, vnew_ref, pbuf,
                      out_ref, s_ref, p_ref, l_ref, acc_ref):
    npg = pbuf.shape[1]
    nt = (((1,), (1,)), ((), ()))
    rg = GROUP * 8
    width = npg * PAGE_SIZE
    past = n_steps * width
    sm_chunk = 2048

    def head_rows(g):
        rows = [pbuf[slot, r, pl.ds(g, PAGE_SIZE, stride=N_KV_HEADS), :] for r in range(npg)]
        return jnp.concatenate(rows, axis=0).astype(BF16)

    @pl.when(t < n_steps)
    def _logits():
        off = pl.multiple_of(t * width, width)
        bias = jnp.concatenate([bp_ref[0]] * GROUP, axis=0)
        for g in range(N_KV_HEADS):
            qg = q_ref[0, g * rg:(g + 1) * rg, :]
            s_ref[g * rg:(g + 1) * rg, pl.ds(off, width)] = lax.dot_general(
                qg, head_rows(g), nt, preferred_element_type=F32) + bias

    @pl.when(t == n_steps - 1)
    def _softmax():
        biasn = jnp.concatenate([bn_ref[0]] * GROUP, axis=0)
        for g in range(N_KV_HEADS):
            qg = q_ref[0, g * rg:(g + 1) * rg, :]
            kg = knew_ref[0, :, g * HEAD_DIM:(g + 1) * HEAD_DIM]
            s_ref[g * rg:(g + 1) * rg, past:past + PAGE_SIZE] = lax.dot_general(
                qg, kg, nt, preferred_element_type=F32) + biasn

        def fold(x, op):
            f = x[:, 0:128]
            for q in range(1, x.shape[1] // 128):
                f = op(f, x[:, q * 128:(q + 1) * 128])
            return f

        def max_chunk(c, m):
            off = pl.multiple_of(c * sm_chunk, sm_chunk)
            return jnp.maximum(m, fold(s_ref[:, pl.ds(off, sm_chunk)], jnp.maximum))

        m = lax.fori_loop(0, past // sm_chunk, max_chunk, s_ref[:, past:past + PAGE_SIZE])
        m = jnp.max(m, axis=1, keepdims=True)

        def exp_chunk(c, l):
            off = pl.multiple_of(c * sm_chunk, sm_chunk)
            p = jnp.exp2(s_ref[:, pl.ds(off, sm_chunk)] - m)
            p_ref[:, pl.ds(off, sm_chunk)] = p.astype(BF16)
            return l + fold(p, jnp.add)

        pn = jnp.exp2(s_ref[:, past:past + PAGE_SIZE] - m)
        p_ref[:, past:past + PAGE_SIZE] = pn.astype(BF16)
        l_ref[...] = lax.fori_loop(0, past // sm_chunk, exp_chunk, pn)
        acc_ref[...] = jnp.zeros(acc_ref.shape, F32)

    @pl.when(t >= n_steps)
    def _values():
        off = pl.multiple_of((t - n_steps) * width, width)
        for g in range(N_KV_HEADS):
            rs = slice(g * rg, (g + 1) * rg)
            acc_ref[rs, :] += jnp.dot(p_ref[rs, pl.ds(off, width)], head_rows(g),
                                      preferred_element_type=F32)

    @pl.when(t == 2 * n_steps - 1)
    def _finish():
        for g in range(N_KV_HEADS):
            rs = slice(g * rg, (g + 1) * rg)
            vg = vnew_ref[0, :, g * HEAD_DIM:(g + 1) * HEAD_DIM]
            acc_ref[rs, :] += jnp.dot(p_ref[rs, past:past + PAGE_SIZE], vg, preferred_element_type=F32)
        out_ref[0] = acc_ref[...] / jnp.sum(l_ref[...], axis=1, keepdims=True)


def _sample_operands(qT, qiT, wT, kibf, kbf, v32, cache_kidx, n_seq, n_tok):
    def rows_ht(xT, n_heads, dim):
        x = xT.reshape(n_heads, dim, n_seq, n_tok).transpose(2, 0, 3, 1)
        x = jnp.pad(x, ((0, 0), (0, 0), (0, 8 - n_tok), (0, 0)))
        return x.reshape(n_seq, n_heads * 8, dim)

    qi_rows = rows_ht(qiT, N_IDX_HEADS, IDX_DIM)
    q_rows = rows_ht(qT, N_HEADS, HEAD_DIM)
    wcol = jnp.pad(wT.reshape(N_IDX_HEADS, n_seq, n_tok).transpose(1, 0, 2),
                   ((0, 0), (0, 0), (0, 8 - n_tok))).reshape(n_seq, N_IDX_HEADS * 8, 1)

    def pad_keys(x):
        x = x.reshape(n_seq, n_tok, x.shape[-1])
        return jnp.pad(x, ((0, 0), (0, PAGE_SIZE - n_tok), (0, 0)))

    ki_new_t = jnp.swapaxes(pad_keys(kibf[:, 0:IDX_DIM]), 1, 2)
    kidx_t = jnp.swapaxes(cache_kidx, 1, 2)
    k_new = pad_keys(kbf)
    v_new = pad_keys(v32.reshape(n_seq * n_tok, N_KV).astype(BF16))
    return (qi_rows, wcol, ki_new_t, kidx_t), (q_rows, k_new, v_new)


def _sample_topk(sp, sn, n_tok):
    n_seq, _, past = sp.shape
    rows = n_seq * 8
    assert 8 % n_tok == 0 and n_seq % (8 // n_tok) == 0
    bp, bn = pl.pallas_call(
        functools.partial(_s_thresh_body, n_tok),
        grid=(1,),
        in_specs=[pl.BlockSpec((rows, past), lambda i: (0, 0)),
                  pl.BlockSpec((rows, PAGE_SIZE), lambda i: (0, 0))],
        out_specs=[pl.BlockSpec((rows, past), lambda i: (0, 0)),
                   pl.BlockSpec((rows, PAGE_SIZE), lambda i: (0, 0))],
        out_shape=[jax.ShapeDtypeStruct((rows, past), F32),
                   jax.ShapeDtypeStruct((rows, PAGE_SIZE), F32)],
        scratch_shapes=[pltpu.VMEM((n_seq * n_tok, past), F32)],
        compiler_params=_cparams(1),
        name="sample_topk_mask",
    )(sp.reshape(rows, past), sn.reshape(rows, PAGE_SIZE))
    return bp.reshape(n_seq, 8, past), bn.reshape(n_seq, 8, PAGE_SIZE)


def _sample_rows_to_tokens(out, n_seq, n_tok):
    out = out.reshape(n_seq, N_HEADS, 8, HEAD_DIM)[:, :, 0:n_tok]
    return out.transpose(0, 2, 1, 3).reshape(n_seq * n_tok, D_ATTN)


def _outproj_body(gated, *refs):
    if gated:
        x_ref, a_ref, gate_ref, mc_ref, wo_ref, o_ref, wo_o = refs
        ma = (a_ref[...] * gate_ref[...]).astype(BF16)
        wo_o[...] = wo_ref[...].astype(BF16)
        wo_ref = wo_o
    else:
        x_ref, a_ref, mc_ref, wo_ref, o_ref = refs
        ma = a_ref[...]
    acc = jnp.dot(ma, wo_ref[0:D_ATTN, :], preferred_element_type=F32)
    acc = acc + jnp.dot(mc_ref[...], wo_ref[D_ATTN:D_ATTN + D_CONV, :], preferred_element_type=F32)
    o_ref[...] = x_ref[...] + acc


def _outproj(x, attn, gate, mixc, wo, tm):
    m = x.shape[0]
    gated = gate is not None
    assert not gated or m == tm
    row = lambda i: (i, 0)
    in_specs = [pl.BlockSpec((tm, D_MODEL), row), pl.BlockSpec((tm, D_ATTN), row)]
    args = [x, attn]
    if gated:
        in_specs.append(pl.BlockSpec((tm, D_ATTN), row))
        args.append(gate)
    in_specs += [pl.BlockSpec((tm, D_CONV), row),
                 pl.BlockSpec((D_ATTN + D_CONV, D_MODEL), lambda i: (0, 0))]
    args += [mixc, wo]
    out_specs = [pl.BlockSpec((tm, D_MODEL), row)]
    out_shape = [jax.ShapeDtypeStruct((m, D_MODEL), F32)]
    if gated:
        out_specs.append(pl.BlockSpec((D_ATTN + D_CONV, D_MODEL), lambda i: (0, 0)))
        out_shape.append(jax.ShapeDtypeStruct((D_ATTN + D_CONV, D_MODEL), BF16))
    out = pl.pallas_call(
        functools.partial(_outproj_body, gated),
        grid=(m // tm,),
        in_specs=in_specs,
        out_specs=out_specs,
        out_shape=out_shape,
        compiler_params=_cparams(1),
        name="outproj_sample" if gated else "outproj_prompt",
    )(*args)
    return out if gated else out[0]


def _rope_tables(pos):
    posf = np.asarray(pos, np.float64)[:, None]
    n = posf.shape[0]

    def cs(half):
        inv = ROPE_THETA ** (-np.arange(half, dtype=np.float64) / half)
        ang = posf * inv[None, :]
        return np.cos(ang), np.sin(ang)

    c16, s16 = cs(ROPE_HALF)
    c8, s8 = cs(IDX_ROPE_HALF)
    one = lambda w: np.ones((n, w))
    zero = lambda w: np.zeros((n, w))
    rest = HEAD_DIM - ROPE_DIM
    k_c = np.concatenate([c16, c16, one(rest)], axis=1)
    k_sa = np.concatenate([-s16, zero(HEAD_DIM - ROPE_HALF)], axis=1)
    k_sb = np.concatenate([zero(ROPE_HALF), s16, zero(rest)], axis=1)
    tabs = dict(c16T=c16.T, s16T=s16.T, c8T=c8.T, s8T=s8.T, kC=k_c, kSa=k_sa, kSb=k_sb)
    return {name: jnp.asarray(np.ascontiguousarray(t), F32) for name, t in tabs.items()}


def _prep_weights(w_in, g_q, g_k, g_kidx, w_conv, w_out):
    assert w_in.shape == (D_MODEL, D_IN)
    wT = w_in.T
    gq = g_q.reshape(HEAD_DIM, 1)
    gk = g_k.reshape(1, HEAD_DIM)
    gki = g_kidx.reshape(IDX_DIM, 1)
    return wT, (gq, gk, gki, w_conv), w_out


def kernel(x_prompt, x_sample, cache_k, cache_v, cache_kidx, state_conv, page_table,
           norm_in, w_in, g_q, g_k, g_kidx, w_conv, w_out):
    n_b, seq, _ = x_prompt.shape
    n_s, n_t, _ = x_sample.shape
    depth = w_in.shape[0]
    past = page_table.shape[1] * PAGE_SIZE
    tabs_p = _rope_tables(np.arange(seq))
    tabs_s = _rope_tables(np.tile(past + np.arange(n_t), n_s))

    hp = x_prompt.reshape(n_b * seq, D_MODEL)
    hs = x_sample.reshape(n_s * n_t, D_MODEL)
    outs = [[] for _ in range(8)]
    for l in range(depth):
        w_t, params, wo = _prep_weights(w_in[l], g_q[l], g_k[l], g_kidx[l], w_conv[l], w_out[l])

        st = state_conv[l]
        tok = jnp.arange(n_t)
        e1 = st[:, jnp.full((n_t,), CONV_W - 2)].reshape(n_s * n_t, D_CONV)
        e2 = st[:, jnp.minimum(tok, CONV_W - 2)].reshape(n_s * n_t, D_CONV)
        (qT_s, qiT_s, _, wT_s, k32_s, kbf_s, v32_s, ki32_s, kibf_s, gate_s, mixc_s, u, *w_bf) = _project(
            hs, norm_in[l], w_t, params, tabs_s, n_t, state_rows=(e1, e2))
        idx_ops, (q_rows, k_new, v_new) = _sample_operands(
            qT_s, qiT_s, wT_s, kibf_s, kbf_s, v32_s, cache_kidx[l], n_s, n_t)
        (qT, qiT, vT, wT, k32, kbf, v32, ki32, kibf, gate, mixc, utail, sp, sn) = _project(
            hp, norm_in[l], tuple(w_bf), params, tabs_p, 0, side=(page_table,) + idx_ops)

        bp, bn = _sample_topk(sp, sn, n_t)
        mixa, attn_rows = _attention(qiT, wT, kibf, qT, kbf, vT, gate, n_b, seq,
                                     q_rows, bp, bn, k_new, v_new, cache_k[l], cache_v[l], page_table)
        attn_s = _sample_rows_to_tokens(attn_rows, n_s, n_t)

        hs, wo_bf = _outproj(hs, attn_s, gate_s, mixc_s, wo, n_s * n_t)
        hp = _outproj(hp, mixa, None, mixc, wo_bf, ROW_TM)
        tps = seq // PROJ_TM
        outs[0].append(k32.reshape(n_b, seq, N_KV_HEADS, HEAD_DIM))
        outs[1].append(v32.reshape(n_b, seq, N_KV_HEADS, HEAD_DIM))
        outs[2].append(ki32.reshape(n_b, seq, IDX_DIM))
        outs[3].append(utail[tps - 1::tps])
        outs[4].append(k32_s.reshape(n_s, n_t, N_KV_HEADS, HEAD_DIM))
        outs[5].append(v32_s.reshape(n_s, n_t, N_KV_HEADS, HEAD_DIM))
        outs[6].append(ki32_s.reshape(n_s, n_t, IDX_DIM))
        outs[7].append(u.reshape(n_s, n_t, D_CONV)[:, n_t - (CONV_W - 1):])

    return (hp.reshape(n_b, seq, D_MODEL), hs.reshape(n_s, n_t, D_MODEL),
            *[jnp.stack(o) for o in outs])
```

```python
import functools

import jax
import jax.numpy as jnp
import numpy as np
from jax import lax
from jax.experimental import pallas as pl
from jax.experimental.pallas import tpu as pltpu

F32 = jnp.float32
BF16 = jnp.bfloat16
I32 = jnp.int32

D_MODEL = 2048
HEAD_DIM = 128
N_HEADS = 8
N_KV_HEADS = 2
GROUP = N_HEADS // N_KV_HEADS
D_ATTN = N_HEADS * HEAD_DIM
D_CONV = 1024
ROPE_DIM = HEAD_DIM // 4
ROPE_HALF = ROPE_DIM // 2
ROPE_THETA = 500000.0
N_IDX_HEADS = 16
IDX_DIM = 64
IDX_ROPE_HALF = IDX_DIM // 8
TOPK_MAX = 256
CONV_W = 3
PAGE_SIZE = 128
EPS = 1e-6
W_IDX_SCALE = (N_IDX_HEADS ** -0.5) * (IDX_DIM ** -0.5)
ATTN_SCALE = HEAD_DIM ** -0.5
Q_PRESCALE = ATTN_SCALE * 1.4426950408889634

INT_MIN = -(2 ** 31)
NEG = -1e30

VMEM_LIMIT_BYTES = 60 * 1024 * 1024

PROJ_TN = 512
PROJ_TM = 1024
PROJ_SUB = 256
ROW_TM = 512
N_KV = N_KV_HEADS * HEAD_DIM
OFF_Q = 0
OFF_K = OFF_Q + D_ATTN
OFF_V = OFF_K + N_KV
OFF_Z = OFF_V + N_KV
OFF_QI = OFF_Z + D_ATTN
OFF_KI = OFF_QI + N_IDX_HEADS * IDX_DIM
OFF_WI = OFF_KI + IDX_DIM
OFF_H = OFF_WI + N_IDX_HEADS
OFF_B = OFF_H + D_CONV
OFF_C = OFF_B + D_CONV
OFF_ZC = OFF_C + D_CONV
D_IN = OFF_ZC + D_CONV
assert OFF_K == 2 * PROJ_TN and OFF_Z == 3 * PROJ_TN and OFF_QI == 5 * PROJ_TN and OFF_KI == 7 * PROJ_TN
KIW_ROWS = 128
assert OFF_KI % KIW_ROWS == 0 and OFF_WI - OFF_KI == IDX_DIM
CONV_CHUNK = 256
N_CONV_CHUNKS = D_CONV // CONV_CHUNK
J_Q = 0
J_KV = 2
J_Z = 3
J_QI = 5
J_CONV = 7
N_MAIN_BLOCKS = J_CONV
N_STEPS = J_CONV + N_CONV_CHUNKS

ATT_TQ = 256
ATT_TK = 256
CNT_ACCS = 4

SIDE_PAGES = 16


def _cparams(n_axes):
    return pltpu.CompilerParams(
        dimension_semantics=("arbitrary",) * n_axes,
        vmem_limit_bytes=VMEM_LIMIT_BYTES,
    )


def _silu(x):
    return x * jax.nn.sigmoid(x)


def _tile_rows(x8, period):
    row = lax.broadcasted_iota(I32, x8.shape, 0)
    out = x8
    for q in range(1, 8 // period):
        out = jnp.where(row // period == q, pltpu.roll(x8, q * period, axis=0), out)
    return out


def _indexer_rows_scores(qi, wcol, keys_t):
    n_tok = qi.shape[0] // N_IDX_HEADS
    d = jnp.dot(qi, keys_t, preferred_element_type=F32)
    val = jnp.maximum(d, 0.0) * wcol
    v = val.reshape(val.shape[0] // 8, 8, val.shape[-1]).sum(axis=0)
    out = v
    for q in range(1, 8 // n_tok):
        out = out + pltpu.roll(v, q * n_tok, axis=0)
    return out


class _SideScores:
    def __init__(self, step, n_total, spq, pt_ref, qi_ref, wcol_ref, kinew_ref, kidx_hbm, sp_ref, sn_ref,
                 kbuf, ksem):
        self.step, self.n_total, self.spq, self.pt_ref = step, n_total, spq, pt_ref
        self.qi_ref, self.wcol_ref, self.kinew_ref, self.kidx_hbm = qi_ref, wcol_ref, kinew_ref, kidx_hbm
        self.sp_ref, self.sn_ref, self.kbuf, self.ksem = sp_ref, sn_ref, kbuf, ksem
        self.npg = kbuf.shape[1]
        self.n_side = pt_ref.shape[0] * spq

    def _start(self, n):
        c = jnp.minimum(n, self.n_side - 1)
        seq_n = c // self.spq
        base = (c % self.spq) * self.npg
        for r in range(self.npg):
            _page_copy(self.kidx_hbm, self.pt_ref[seq_n, base + r], self.kbuf, n % 2, r, self.ksem).start()

    def prologue(self):
        pl.when(self.step == 0)(lambda: self._start(self.step))

    def __call__(self):
        pl.when(self.step + 1 < self.n_total)(lambda: self._start(self.step + 1))
        slot = self.step % 2
        for r in range(self.npg):
            _page_copy(self.kidx_hbm, 0, self.kbuf, slot, r, self.ksem).wait()
        qi = self.qi_ref[0]
        wcol = self.wcol_ref[0]
        keys_t = jnp.concatenate([self.kbuf[slot, r] for r in range(self.npg)], axis=1).astype(BF16)
        self.sp_ref[0] = _indexer_rows_scores(qi, wcol, keys_t)
        self.sn_ref[0] = _indexer_rows_scores(qi, wcol, self.kinew_ref[0])


class _Bf16View:
    def __init__(self, ref):
        self.ref = ref

    def __getitem__(self, idx):
        return self.ref[idx].astype(BF16)


def _proj_body(sample, tps, tm, side_spq, convert, *refs):
    if side_spq:
        pt_ref, refs = refs[0], refs[1:]
    (x_ref, gin_ref, wa_ref, wkiw_ref, wh_ref, wb_ref, wc_ref, wzc_ref, gq_ref, gk_ref, gki_ref,
     c16_ref, s16_ref, c8_ref, s8_ref,
     kc_ref, ksa_ref, ksb_ref, wconv_ref) = refs[:19]
    refs = refs[19:]
    if sample:
        e1_ref, e2_ref = refs[:2]
        refs = refs[2:]
    if side_spq:
        side_in, refs = refs[:4], refs[4:]
    (qT_ref, qiT_ref, vT_ref, wT_ref, k32_ref, kbf_ref, v32_ref, ki32_ref, kibf_ref,
     gate_ref, mixc_ref, u_ref) = refs[:12]
    refs = refs[12:]
    if convert:
        w_out, refs = refs[:6], refs[6:]
    if side_spq:
        side_out, refs = refs[:2], refs[2:]
    xn_ref = refs[0]
    if not sample:
        carry_ref = refs[1]

    i = pl.program_id(0)
    j = pl.program_id(1)
    nt = (((1,), (1,)), ((), ()))
    if side_spq:
        side = _SideScores(i * N_STEPS + j, pl.num_programs(0) * N_STEPS, side_spq, pt_ref,
                           *side_in, *side_out, *refs[2:4])
        side.prologue()
    else:
        side = lambda: None

    if convert:
        wa_o, kiw_o, wh_o, wb_o, wc_o, wzc_o = w_out

        @pl.when(j < N_MAIN_BLOCKS)
        def _():
            wa_o[...] = wa_ref[...].astype(BF16)

        @pl.when(j == J_KV)
        def _():
            kiw_o[...] = wkiw_ref[...].astype(BF16)

        @pl.when(j >= J_CONV)
        def _():
            for src, dst in ((wh_ref, wh_o), (wb_ref, wb_o), (wc_ref, wc_o), (wzc_ref, wzc_o)):
                dst[...] = src[...].astype(BF16)

        wa_ref, wkiw_ref, wh_ref, wb_ref, wc_ref, wzc_ref = (
            _Bf16View(r) for r in (wa_ref, wkiw_ref, wh_ref, wb_ref, wc_ref, wzc_ref))

    def xw(w, x=None):
        return lax.dot_general(xn_ref[...] if x is None else x, w, nt, preferred_element_type=F32)

    def wx(w, x):
        return lax.dot_general(w, x, nt, preferred_element_type=F32)

    nsub = max(tm // PROJ_SUB, 1)
    sub = tm // nsub
    pieces = [slice(rb * sub, (rb + 1) * sub) for rb in range(nsub)]

    if not sample:
        @pl.when((i == 0) & (j == 0))
        def _init():
            carry_ref[...] = jnp.zeros(carry_ref.shape, F32)

    def q_step(with_norm):
        side()
        for cs in pieces:
            if with_norm:
                x = x_ref[cs, :]
                ms = jnp.mean(x * x, axis=-1, keepdims=True)
                xn_ref[cs, :] = (x * lax.rsqrt(ms + EPS) * gin_ref[...]).astype(BF16)
            res = wx(wa_ref[...], xn_ref[cs, :])
            cos = c16_ref[:, cs]
            sin = s16_ref[:, cs]
            for hh in range(PROJ_TN // HEAD_DIM):
                blk = res[hh * HEAD_DIM:(hh + 1) * HEAD_DIM]
                ms = jnp.mean(blk * blk, axis=0, keepdims=True)
                y = blk * lax.rsqrt(ms + EPS) * gq_ref[...]
                x1 = y[0:ROPE_HALF]
                x2 = y[ROPE_HALF:ROPE_DIM]
                base = hh * HEAD_DIM
                qT_ref[base:base + ROPE_HALF, cs] = ((x1 * cos - x2 * sin) * Q_PRESCALE).astype(BF16)
                qT_ref[base + ROPE_HALF:base + ROPE_DIM, cs] = ((x2 * cos + x1 * sin) * Q_PRESCALE).astype(BF16)
                qT_ref[base + ROPE_DIM:base + HEAD_DIM, cs] = (y[ROPE_DIM:] * Q_PRESCALE).astype(BF16)

    pl.when(j == J_Q)(functools.partial(q_step, True))
    pl.when((j > J_Q) & (j < J_KV))(functools.partial(q_step, False))

    @pl.when((j >= J_QI) & (j < J_CONV))
    def _qi():
        side()
        for cs in pieces:
            res = wx(wa_ref[...], xn_ref[cs, :])
            cos = c8_ref[:, cs]
            sin = s8_ref[:, cs]
            for hh in range(PROJ_TN // IDX_DIM):
                blk = res[hh * IDX_DIM:(hh + 1) * IDX_DIM]
                x1 = blk[0:IDX_ROPE_HALF]
                x2 = blk[IDX_ROPE_HALF:2 * IDX_ROPE_HALF]
                rot = jnp.concatenate([x1 * cos - x2 * sin, x2 * cos + x1 * sin], axis=0)
                base = hh * IDX_DIM
                qiT_ref[base:base + 2 * IDX_ROPE_HALF, cs] = rot.astype(BF16)
                qiT_ref[base + 2 * IDX_ROPE_HALF:base + IDX_DIM, cs] = blk[2 * IDX_ROPE_HALF:].astype(BF16)

    @pl.when(j == J_KV)
    def _kv():
        side()
        for rb, cs in enumerate(pieces):
            x = xn_ref[cs, :]
            res = xw(wa_ref[...], x)
            for hd in range(N_KV_HEADS):
                hs = slice(hd * HEAD_DIM, (hd + 1) * HEAD_DIM)
                blk = res[:, hs]
                ms = jnp.mean(blk * blk, axis=-1, keepdims=True)
                yk = blk * lax.rsqrt(ms + EPS) * gk_ref[...]
                rot = yk * kc_ref[cs, :] + (pltpu.roll(yk, HEAD_DIM - ROPE_HALF, axis=1) * ksa_ref[cs, :]
                                            + pltpu.roll(yk, ROPE_HALF, axis=1) * ksb_ref[cs, :])
                k32_ref[pl.ds(N_KV_HEADS * rb * sub + hd, sub, stride=N_KV_HEADS), :] = rot
                kbf_ref[cs, hs] = rot.astype(BF16)
            v = res[:, N_KV:2 * N_KV]
            for hd in range(N_KV_HEADS):
                v32_ref[pl.ds(N_KV_HEADS * rb * sub + hd, sub, stride=N_KV_HEADS), :] = (
                    v[:, hd * HEAD_DIM:(hd + 1) * HEAD_DIM])
            if sample:
                vT_ref[:, cs] = wx(wa_ref[N_KV:2 * N_KV, :], x).astype(BF16)
            else:
                vT_ref[:, cs] = v.T.astype(BF16)
            r2t = wx(wkiw_ref[...], x)
            wT_ref[:, cs] = r2t[IDX_DIM:IDX_DIM + N_IDX_HEADS] * W_IDX_SCALE
            kit = r2t[0:IDX_DIM]
            ms = jnp.mean(kit * kit, axis=0, keepdims=True)
            yi = kit * lax.rsqrt(ms + EPS) * gki_ref[...]
            x1 = yi[0:IDX_ROPE_HALF]
            x2 = yi[IDX_ROPE_HALF:2 * IDX_ROPE_HALF]
            cos = c8_ref[:, cs]
            sin = s8_ref[:, cs]
            roti = jnp.concatenate([x1 * cos - x2 * sin, x2 * cos + x1 * sin, yi[2 * IDX_ROPE_HALF:],
                                    jnp.zeros((KIW_ROWS - IDX_DIM, sub), F32)], axis=0)
            ki_nat = roti.T
            ki32_ref[cs, :] = ki_nat[:, 0:IDX_DIM]
            kibf_ref[cs, :] = ki_nat.astype(BF16)

    @pl.when((j >= J_Z) & (j < J_QI))
    def _z():
        side()
        for cs in pieces:
            gate_ref[cs, :] = _silu(xw(wa_ref[...], xn_ref[cs, :]))

    @pl.when(j >= J_CONV)
    def _conv():
        side()
        cc = j - J_CONV
        u = xw(wc_ref[...]) * xw(wh_ref[...])
        rowid = lax.broadcasted_iota(I32, (tm, CONV_CHUNK), 0)
        if sample:
            t = rowid & (sample - 1)
            u1 = jnp.where(t >= 1, pltpu.roll(u, 1, axis=0), e1_ref[...])
            u2 = jnp.where(t >= 2, pltpu.roll(u, 2, axis=0), e2_ref[...])
            u_ref[...] = u
        else:
            first = (i % tps) == 0
            prev = carry_ref[cc]
            p0 = jnp.where(first, 0.0, prev[0:1])
            p1 = jnp.where(first, 0.0, prev[1:2])
            u1 = jnp.where(rowid == 0, p1, pltpu.roll(u, 1, axis=0))
            u2 = jnp.where(rowid == 0, p0, jnp.where(rowid == 1, p1, pltpu.roll(u, 2, axis=0)))
            tail = u[tm - 8:tm]
            carry_ref[cc] = jnp.concatenate([tail[6:8], tail[0:6]], axis=0)
            u_ref[0] = tail[6:8]
        w = wconv_ref[...]
        y = u2 * w[0:1] + u1 * w[1:2] + u * w[2:3]
        mixc_ref[...] = (xw(wb_ref[...]) * y * _silu(xw(wzc_ref[...]))).astype(BF16)


def _project(x, norm_g, wts, params, tabs, sample, state_rows=None, side=None):
    m = x.shape[0]
    convert = not isinstance(wts, tuple)
    assert sample & (sample - 1) == 0
    tm = m if sample else PROJ_TM
    n_i = m // tm
    tps = 1 if sample else (tabs["c16T"].shape[1] // tm)
    gq, gk, gki, wconv = params

    def tmap(i):
        return i % tps

    def cchunk(j):
        return jnp.clip(j - J_CONV, 0, N_CONV_CHUNKS - 1)

    main_spec = pl.BlockSpec((PROJ_TN, D_MODEL), lambda i, j, *_: (jnp.minimum(j, N_MAIN_BLOCKS - 1), 0))
    if convert:
        def conv_rows(off):
            return pl.BlockSpec((pl.Element(CONV_CHUNK), pl.Element(D_MODEL)),
                                lambda i, j, *_: (pl.multiple_of(off + CONV_CHUNK * cchunk(j), 16), 0))

        w_specs = [main_spec, pl.BlockSpec((KIW_ROWS, D_MODEL), lambda i, j, *_: (OFF_KI // KIW_ROWS, 0)),
                   conv_rows(OFF_H), conv_rows(OFF_B), conv_rows(OFF_C), conv_rows(OFF_ZC)]
        w_args = [wts] * 6
    else:
        conv_spec = pl.BlockSpec(
            (CONV_CHUNK, D_MODEL),
            lambda i, j, *_: (jnp.where(j < J_QI, N_CONV_CHUNKS - 1, cchunk(j)), 0))
        w_specs = [main_spec, pl.BlockSpec((KIW_ROWS, D_MODEL), lambda i, j, *_: (0, 0))] + [conv_spec] * 4
        w_args = list(wts)

    in_specs = [
        pl.BlockSpec((tm, D_MODEL), lambda i, j, *_: (i, 0)),
        pl.BlockSpec((1, D_MODEL), lambda i, j, *_: (0, 0)),
        *w_specs,
        pl.BlockSpec((HEAD_DIM, 1), lambda i, j, *_: (0, 0)),
        pl.BlockSpec((1, HEAD_DIM), lambda i, j, *_: (0, 0)),
        pl.BlockSpec((IDX_DIM, 1), lambda i, j, *_: (0, 0)),
        pl.BlockSpec((ROPE_HALF, tm), lambda i, j, *_: (0, tmap(i))),
        pl.BlockSpec((ROPE_HALF, tm), lambda i, j, *_: (0, tmap(i))),
        pl.BlockSpec((IDX_ROPE_HALF, tm), lambda i, j, *_: (0, tmap(i))),
        pl.BlockSpec((IDX_ROPE_HALF, tm), lambda i, j, *_: (0, tmap(i))),
        pl.BlockSpec((tm, HEAD_DIM), lambda i, j, *_: (tmap(i), 0)),
        pl.BlockSpec((tm, HEAD_DIM), lambda i, j, *_: (tmap(i), 0)),
        pl.BlockSpec((tm, HEAD_DIM), lambda i, j, *_: (tmap(i), 0)),
        pl.BlockSpec((CONV_W, CONV_CHUNK), lambda i, j, *_: (0, cchunk(j))),
    ]
    args = [x, norm_g.reshape(1, D_MODEL), *w_args, gq, gk, gki,
            tabs["c16T"], tabs["s16T"], tabs["c8T"], tabs["s8T"],
            tabs["kC"], tabs["kSa"], tabs["kSb"], wconv]
    cmap = lambda i, j, *_: (i, cchunk(j))
    if sample:
        in_specs += [pl.BlockSpec((tm, CONV_CHUNK), cmap), pl.BlockSpec((tm, CONV_CHUNK), cmap)]
        args += list(state_rows)
        u_spec = pl.BlockSpec((tm, CONV_CHUNK), cmap)
        u_shape = jax.ShapeDtypeStruct((m, D_CONV), F32)
        scratch = [pltpu.VMEM((tm, D_MODEL), BF16)]
    else:
        u_spec = pl.BlockSpec((1, CONV_W - 1, CONV_CHUNK), lambda i, j, *_: (i, 0, cchunk(j)))
        u_shape = jax.ShapeDtypeStruct((n_i, CONV_W - 1, D_CONV), F32)
        scratch = [pltpu.VMEM((tm, D_MODEL), BF16), pltpu.VMEM((N_CONV_CHUNKS, 8, CONV_CHUNK), F32)]

    out_specs = [
        pl.BlockSpec((PROJ_TN, tm), lambda i, j, *_: (jnp.minimum(j, 1), i)),
        pl.BlockSpec((PROJ_TN, tm), lambda i, j, *_: (jnp.clip(j - J_QI, 0, 1), i)),
        pl.BlockSpec((N_KV, tm), lambda i, j, *_: (0, i)),
        pl.BlockSpec((N_IDX_HEADS, tm), lambda i, j, *_: (0, i)),
        pl.BlockSpec((N_KV_HEADS * tm, HEAD_DIM), lambda i, j, *_: (i, 0)),
        pl.BlockSpec((tm, N_KV), lambda i, j, *_: (i, 0)),
        pl.BlockSpec((N_KV_HEADS * tm, HEAD_DIM), lambda i, j, *_: (i, 0)),
        pl.BlockSpec((tm, IDX_DIM), lambda i, j, *_: (i, 0)),
        pl.BlockSpec((tm, KIW_ROWS), lambda i, j, *_: (i, 0)),
        pl.BlockSpec((tm, PROJ_TN), lambda i, j, *_: (i, jnp.clip(j - J_Z, 0, 1))),
        pl.BlockSpec((tm, CONV_CHUNK), cmap),
        u_spec,
    ]
    out_shape = [
        jax.ShapeDtypeStruct((D_ATTN, m), BF16),
        jax.ShapeDtypeStruct((N_IDX_HEADS * IDX_DIM, m), BF16),
        jax.ShapeDtypeStruct((N_KV, m), BF16),
        jax.ShapeDtypeStruct((N_IDX_HEADS, m), F32),
        jax.ShapeDtypeStruct((N_KV_HEADS * m, HEAD_DIM), F32),
        jax.ShapeDtypeStruct((m, N_KV), BF16),
        jax.ShapeDtypeStruct((N_KV_HEADS * m, HEAD_DIM), F32),
        jax.ShapeDtypeStruct((m, IDX_DIM), F32),
        jax.ShapeDtypeStruct((m, KIW_ROWS), BF16),
        jax.ShapeDtypeStruct((m, D_ATTN), F32),
        jax.ShapeDtypeStruct((m, D_CONV), BF16),
        u_shape,
    ]
    if convert:
        assert n_i == 1, "every weight block must be visited exactly once to be written back"
        out_specs += [
            main_spec,
            pl.BlockSpec((KIW_ROWS, D_MODEL), lambda i, j, *_: (0, 0)),
        ] + [pl.BlockSpec((CONV_CHUNK, D_MODEL), lambda i, j, *_: (cchunk(j), 0))] * 4
        out_shape += [jax.ShapeDtypeStruct((OFF_KI, D_MODEL), BF16),
                      jax.ShapeDtypeStruct((KIW_ROWS, D_MODEL), BF16)]
        out_shape += [jax.ShapeDtypeStruct((D_CONV, D_MODEL), BF16)] * 4
    if side is None:
        return pl.pallas_call(
            functools.partial(_proj_body, sample, tps, tm, 0, convert),
            grid=(n_i, N_STEPS),
            in_specs=in_specs,
            out_specs=out_specs,
            out_shape=out_shape,
            scratch_shapes=scratch,
            compiler_params=_cparams(2),
            name="proj_sample" if sample else "proj_prompt",
        )(*args)

    page_table, qi_rows, wcol, ki_new_t, kidx_t = side
    n_seq, n_pages = page_table.shape
    spq = n_pages // SIDE_PAGES
    n_side = n_seq * spq
    assert spq * SIDE_PAGES == n_pages and n_side <= n_i * N_STEPS
    width = SIDE_PAGES * PAGE_SIZE

    def side_step(i, j):
        return jnp.minimum(i * N_STEPS + j, n_side - 1)

    seq_map = lambda i, j, *_: (side_step(i, j) // spq, 0, 0)
    in_specs += [
        pl.BlockSpec((1,) + qi_rows.shape[1:], seq_map),
        pl.BlockSpec((1,) + wcol.shape[1:], seq_map),
        pl.BlockSpec((1, IDX_DIM, PAGE_SIZE), seq_map),
        pl.BlockSpec(memory_space=pl.ANY),
    ]
    out_specs += [
        pl.BlockSpec((1, 8, width), lambda i, j, *_: (side_step(i, j) // spq, 0, side_step(i, j) % spq)),
        pl.BlockSpec((1, 8, PAGE_SIZE), seq_map),
    ]
    out_shape += [jax.ShapeDtypeStruct((n_seq, 8, n_pages * PAGE_SIZE), F32),
                  jax.ShapeDtypeStruct((n_seq, 8, PAGE_SIZE), F32)]
    scratch += [pltpu.VMEM((2, SIDE_PAGES, IDX_DIM, PAGE_SIZE), F32), pltpu.SemaphoreType.DMA((2,))]
    return pl.pallas_call(
        functools.partial(_proj_body, sample, tps, tm, spq, convert),
        grid_spec=pltpu.PrefetchScalarGridSpec(
            num_scalar_prefetch=1,
            grid=(n_i, N_STEPS),
            in_specs=in_specs,
            out_specs=out_specs,
            scratch_shapes=scratch,
        ),
        out_shape=out_shape,
        compiler_params=_cparams(2),
        name="proj_prompt",
    )(page_table, *args, qi_rows, wcol, ki_new_t, kidx_t)


KEY_LOWEST_FINITE = INT_MIN + 0x00800000


def _key_to_float(key):
    return pltpu.bitcast(key ^ ((key >> 31) & 0x7FFFFFFF), F32)


def _bit_value(b):
    return lax.shift_left(jnp.int32(1), jnp.int32(31) - b)


def _prompt_attn_step(i, qiT_ref, wT_ref, ki_ref, qT_ref, k_ref, vT_ref, gate_ref, o_ref,
                      sc_ref, bias_ref, acc_ref, s_ref):
    tq, tk = ATT_TQ, ATT_TK
    nch = i + 1
    w = wT_ref[...]
    row = lax.broadcasted_iota(I32, (tk, tq), 0)
    col = lax.broadcasted_iota(I32, (tk, tq), 1)
    row8 = lax.broadcasted_iota(I32, (8, tq), 0)

    def score_chunk(j, carry):
        off = pl.multiple_of(j * tk, tk)
        kic = ki_ref[pl.ds(off, tk), 0:IDX_DIM]
        acc = jnp.zeros((tk, tq), F32)
        for h in range(N_IDX_HEADS):
            d = jnp.dot(kic, qiT_ref[h * IDX_DIM:(h + 1) * IDX_DIM, :], preferred_element_type=F32)
            acc = acc + w[h:h + 1, :] * jnp.maximum(d, 0.0)
        future = (row + j * tk) > (col + i * tq)
        sc_ref[pl.ds(off, tk), :] = jnp.where(future, -jnp.inf, acc)
        return carry

    lax.fori_loop(0, nch, score_chunk, 0)

    def count_keys(pred):
        def cnt_chunk(j, cs):
            off = pl.multiple_of(j * tk, tk)
            cs = list(cs)
            sc = sc_ref[pl.ds(off, tk), :]
            for r in range(tk // 8):
                a = cs[r % CNT_ACCS]
                kpos = row8 + (j * tk + r * 8)
                cs[r % CNT_ACCS] = jnp.where(pred(sc[r * 8:(r + 1) * 8], kpos), a + 1, a)
            return tuple(cs)

        cs = lax.fori_loop(0, nch, cnt_chunk, (jnp.zeros((8, tq), I32),) * CNT_ACCS)
        c = cs[0]
        for a in cs[1:]:
            c = c + a
        return jnp.sum(c.astype(F32), axis=0, keepdims=True)

    def bit_body(b, carry):
        thr, cge = carry
        cand = thr + _bit_value(b)
        cand_f = _key_to_float(cand)
        cnt = count_keys(lambda s, kpos: s >= cand_f)
        ok = cnt >= float(TOPK_MAX)
        return jnp.where(ok, cand, thr), jnp.where(ok, cnt, cge)

    thr, cge = lax.fori_loop(0, 32, bit_body, (jnp.full((1, tq), INT_MIN, I32),
                                               jnp.zeros((1, tq), F32)))
    thr_f = _key_to_float(jnp.maximum(thr, KEY_LOWEST_FINITE))
    has_ties = jnp.max(cge) > float(TOPK_MAX)

    @pl.when(jnp.logical_not(has_ties))
    def _plain_mask():
        def bias_chunk(j, carry):
            off = pl.multiple_of(j * tk, tk)
            bias_ref[pl.ds(off, tk), :] = jnp.where(sc_ref[pl.ds(off, tk), :] >= thr_f, 0.0, NEG)
            return carry

        lax.fori_loop(0, nch, bias_chunk, 0)

    @pl.when(has_ties)
    def _tie_mask():
        need = float(TOPK_MAX) - count_keys(lambda s, kpos: s > thr_f)
        nbits = (sc_ref.shape[0] - 1).bit_length()

        def pos_bit(b, last):
            step = lax.shift_left(jnp.int32(1), jnp.int32(nbits - 1) - b)
            probe = last + (step - 1)
            got = count_keys(lambda s, kpos: (s == thr_f) & (kpos <= probe))
            return jnp.where(got < need, last + step, last)

        last = lax.fori_loop(0, nbits, pos_bit, jnp.zeros((1, tq), I32))

        def bias_chunk(j, carry):
            off = pl.multiple_of(j * tk, tk)
            sc = sc_ref[pl.ds(off, tk), :]
            keep = (sc > thr_f) | ((sc == thr_f) & ((row + j * tk) <= last))
            bias_ref[pl.ds(off, tk), :] = jnp.where(keep, 0.0, NEG)
            return carry

        lax.fori_loop(0, nch, bias_chunk, 0)

    def qk_chunk(j, mrun):
        off = pl.multiple_of(j * tk, tk)
        bias = bias_ref[pl.ds(off, tk), :]
        out = []
        for h in range(N_HEADS):
            g = h // GROUP
            kc = k_ref[pl.ds(off, tk), g * HEAD_DIM:(g + 1) * HEAD_DIM]
            s = jnp.dot(kc, qT_ref[h * HEAD_DIM:(h + 1) * HEAD_DIM, :], preferred_element_type=F32) + bias
            s_ref[h, pl.ds(off, tk), :] = s
            out.append(jnp.maximum(mrun[h], s.reshape(tk // 8, 8, tq).max(axis=0)))
        return tuple(out)

    mrun = lax.fori_loop(0, nch, qk_chunk, (jnp.full((8, tq), NEG, F32),) * N_HEADS)
    ms = [jnp.max(mr, axis=0, keepdims=True) for mr in mrun]
    acc_ref[...] = jnp.zeros(acc_ref.shape, F32)

    def pv_chunk(j, lrun):
        off = pl.multiple_of(j * tk, tk)
        out = []
        for h in range(N_HEADS):
            g = h // GROUP
            hs = slice(h * HEAD_DIM, (h + 1) * HEAD_DIM)
            p = jnp.exp2(s_ref[h, pl.ds(off, tk), :] - ms[h])
            out.append(lrun[h] + p.reshape(tk // 8, 8, tq).sum(axis=0))
            vc = vT_ref[g * HEAD_DIM:(g + 1) * HEAD_DIM, pl.ds(off, tk)]
            acc_ref[hs, :] += jnp.dot(vc, p.astype(BF16), preferred_element_type=F32)
        return tuple(out)

    lrun = lax.fori_loop(0, nch, pv_chunk, (jnp.zeros((8, tq), F32),) * N_HEADS)

    for h in range(N_HEADS):
        hs = slice(h * HEAD_DIM, (h + 1) * HEAD_DIM)
        o = (acc_ref[hs, :] / jnp.sum(lrun[h], axis=0, keepdims=True)).T
        o_ref[:, hs] = (o * gate_ref[:, hs]).astype(BF16)


def _page_copy(cache_hbm, page, pbuf, slot, r, sem):
    return pltpu.make_async_copy(cache_hbm.at[page], pbuf.at[slot, r], sem.at[slot])


def _attn_fused_body(nq, spq, pt_ref, *refs):
    prompt_in = refs[0:7]
    q_ref, bp_ref, bn_ref, knew_ref, vnew_ref, ck_hbm, cv_hbm = refs[7:14]
    o_ref, so_ref = refs[14:16]
    prompt_scratch = refs[16:20]
    pbuf, sem, ss_ref, ps_ref, ls_ref, accs_ref = refs[20:26]

    i = pl.program_id(1)
    step = pl.program_id(0) * nq + i
    n_total = pl.num_programs(0) * nq
    half = spq // 2
    npg = pbuf.shape[1]

    def start_pages(n):
        seq_n = n // spq
        t_n = n % spq
        base = (t_n % half) * npg
        slot = n % 2
        for cache_hbm, pred in ((ck_hbm, t_n < half), (cv_hbm, t_n >= half)):
            @pl.when(pred)
            def _(cache_hbm=cache_hbm):
                for r in range(npg):
                    _page_copy(cache_hbm, pt_ref[seq_n, base + r], pbuf, slot, r, sem).start()

    @pl.when(step == 0)
    def _prologue():
        start_pages(step)

    @pl.when(step + 1 < n_total)
    def _prefetch():
        start_pages(step + 1)

    slot = step % 2
    for r in range(npg):
        _page_copy(ck_hbm, 0, pbuf, slot, r, sem).wait()

    _sample_attn_step(step % spq, half, slot, q_ref, bp_ref, bn_ref, knew_ref, vnew_ref, pbuf,
                      so_ref, ss_ref, ps_ref, ls_ref, accs_ref)
    _prompt_attn_step(i, *prompt_in, o_ref, *prompt_scratch)


def _attention(qiT, wT, kibf, qT, kbf, vT, gate, n_batch, seq,
               q_rows, bp, bn, k_new, v_new, cache_k, cache_v, page_table):
    m = n_batch * seq
    nq = seq // ATT_TQ
    n_seq, n_pages = page_table.shape
    n_steps = n_batch * nq
    spq = n_steps // n_seq
    assert spq * n_seq == n_steps and spq % 2 == 0
    half = spq // 2
    npg = n_pages // half
    assert npg * half == n_pages
    width = npg * PAGE_SIZE
    past = n_pages * PAGE_SIZE
    n_pool = cache_k.shape[0]
    kv_rows = PAGE_SIZE * N_KV_HEADS
    ck = cache_k.reshape(n_pool, kv_rows, HEAD_DIM)
    cv = cache_v.reshape(n_pool, kv_rows, HEAD_DIM)
    nrow = q_rows.shape[1]
    assert (nrow // N_KV_HEADS) % 16 == 0

    qmap = lambda b, i, pt: (0, b * nq + i)
    smap = lambda b, i, pt: ((b * nq + i) // spq, 0, 0)
    return pl.pallas_call(
        functools.partial(_attn_fused_body, nq, spq),
        grid_spec=pltpu.PrefetchScalarGridSpec(
            num_scalar_prefetch=1,
            grid=(n_batch, nq),
            in_specs=[
                pl.BlockSpec((N_IDX_HEADS * IDX_DIM, ATT_TQ), qmap),
                pl.BlockSpec((N_IDX_HEADS, ATT_TQ), qmap),
                pl.BlockSpec((seq, 128), lambda b, i, pt: (b, 0)),
                pl.BlockSpec((D_ATTN, ATT_TQ), qmap),
                pl.BlockSpec((seq, N_KV), lambda b, i, pt: (b, 0)),
                pl.BlockSpec((N_KV, seq), lambda b, i, pt: (0, b)),
                pl.BlockSpec((ATT_TQ, D_ATTN), lambda b, i, pt: (b * nq + i, 0)),
                pl.BlockSpec((1, nrow, HEAD_DIM), smap),
                pl.BlockSpec((1, 8, width),
                             lambda b, i, pt: ((b * nq + i) // spq, 0, jnp.minimum((b * nq + i) % spq, half - 1))),
                pl.BlockSpec((1, 8, PAGE_SIZE), smap),
                pl.BlockSpec((1, PAGE_SIZE, N_KV), smap),
                pl.BlockSpec((1, PAGE_SIZE, N_KV), smap),
                pl.BlockSpec(memory_space=pl.ANY),
                pl.BlockSpec(memory_space=pl.ANY),
            ],
            out_specs=[
                pl.BlockSpec((ATT_TQ, D_ATTN), lambda b, i, pt: (b * nq + i, 0)),
                pl.BlockSpec((1, nrow, HEAD_DIM), smap),
            ],
            scratch_shapes=[
                pltpu.VMEM((seq, ATT_TQ), F32), pltpu.VMEM((seq, ATT_TQ), F32),
                pltpu.VMEM((D_ATTN, ATT_TQ), F32), pltpu.VMEM((N_HEADS, seq, ATT_TQ), F32),
                pltpu.VMEM((2, npg, kv_rows, HEAD_DIM), F32), pltpu.SemaphoreType.DMA((2,)),
                pltpu.VMEM((nrow, past + PAGE_SIZE), F32), pltpu.VMEM((nrow, past + PAGE_SIZE), BF16),
                pltpu.VMEM((nrow, 128), F32), pltpu.VMEM((nrow, HEAD_DIM), F32),
            ],
        ),
        out_shape=[jax.ShapeDtypeStruct((m, D_ATTN), BF16),
                   jax.ShapeDtypeStruct((n_seq, nrow, HEAD_DIM), F32)],
        compiler_params=_cparams(2),
        name="attention",
    )(page_table, qiT, wT, kibf, qT, kbf, vT, gate, q_rows, bp, bn, k_new, v_new, ck, cv)


def _s_thresh_body(n_new, in_p_ref, in_n_ref, out_p_ref, out_n_ref, sp_ref):
    past = in_p_ref.shape[1]
    rows = sp_ref.shape[0]
    pack = 8 // n_new
    groups = rows // 8
    ch = 2048
    nchunk = past // ch

    def pack_rows(src_ref, cols, ncols):
        which = lax.broadcasted_iota(I32, (8, ncols), 0) // n_new
        out = []
        for p in range(groups):
            dense = src_ref[(p * pack) * 8:(p * pack) * 8 + 8, cols]
            for q in range(1, pack):
                piece = src_ref[(p * pack + q) * 8:(p * pack + q) * 8 + 8, cols]
                dense = jnp.where(which == q, pltpu.roll(piece, q * n_new, axis=0), dense)
            out.append(dense)
        return jnp.concatenate(out, axis=0)

    def unpack_rows(dst_ref, cols, dense):
        for p in range(groups):
            piece = dense[p * 8:(p + 1) * 8]
            for q in range(pack):
                rolled = piece if q == 0 else pltpu.roll(piece, 8 - q * n_new, axis=0)
                dst_ref[(p * pack + q) * 8:(p * pack + q) * 8 + 8, cols] = rolled

    def pack_chunk(c, carry):
        cols = pl.ds(pl.multiple_of(c * ch, ch), ch)
        sp_ref[:, cols] = pack_rows(in_p_ref, cols, ch)
        return carry

    lax.fori_loop(0, nchunk, pack_chunk, 0)
    t = lax.broadcasted_iota(I32, (rows, 128), 0) % n_new
    lane = lax.broadcasted_iota(I32, (rows, 128), 1)
    sn = jnp.where((lane < n_new) & (lane <= t), pack_rows(in_n_ref, slice(0, 128), 128), -jnp.inf)

    def fold(x):
        f = x[:, 0:128]
        for q in range(1, x.shape[1] // 128):
            f = f + x[:, q * 128:(q + 1) * 128]
        return f

    lane_ch = lax.broadcasted_iota(I32, (rows, ch), 1)

    def count_keys(pred):
        def cnt_chunk(c, acc):
            off = pl.multiple_of(c * ch, ch)
            return acc + fold(jnp.where(pred(sp_ref[:, pl.ds(off, ch)], lane_ch + c * ch), 1.0, 0.0))

        acc = lax.fori_loop(0, nchunk, cnt_chunk, jnp.where(pred(sn, lane + past), 1.0, 0.0))
        return jnp.sum(acc, axis=1, keepdims=True)

    def bit_body(b, carry):
        thr, cge = carry
        cand = thr + _bit_value(b)
        cand_f = _key_to_float(cand)
        cnt = count_keys(lambda s, kpos: s >= cand_f)
        ok = cnt >= float(TOPK_MAX)
        return jnp.where(ok, cand, thr), jnp.where(ok, cnt, cge)

    thr, cge = lax.fori_loop(0, 32, bit_body, (jnp.full((rows, 1), INT_MIN, I32),
                                               jnp.zeros((rows, 1), F32)))
    thr_f = _key_to_float(jnp.maximum(thr, KEY_LOWEST_FINITE))
    has_ties = jnp.max(cge) > float(TOPK_MAX)

    @pl.when(jnp.logical_not(has_ties))
    def _plain_mask():
        def to_bias(c, carry):
            cols = pl.ds(pl.multiple_of(c * ch, ch), ch)
            unpack_rows(out_p_ref, cols, jnp.where(sp_ref[:, cols] >= thr_f, 0.0, NEG))
            return carry

        lax.fori_loop(0, nchunk, to_bias, 0)
        unpack_rows(out_n_ref, slice(0, 128), jnp.where(sn >= thr_f, 0.0, NEG))

    @pl.when(has_ties)
    def _tie_mask():
        need = float(TOPK_MAX) - count_keys(lambda s, kpos: s > thr_f)
        nbits = (past + 128 - 1).bit_length()

        def pos_bit(b, last):
            step = lax.shift_left(jnp.int32(1), jnp.int32(nbits - 1) - b)
            probe = last + (step - 1)
            got = count_keys(lambda s, kpos: (s == thr_f) & (kpos <= probe))
            return jnp.where(got < need, last + step, last)

        last = lax.fori_loop(0, nbits, pos_bit, jnp.zeros((rows, 1), I32))

        def keep(s, kpos):
            return (s > thr_f) | ((s == thr_f) & (kpos <= last))

        def to_bias(c, carry):
            cols = pl.ds(pl.multiple_of(c * ch, ch), ch)
            unpack_rows(out_p_ref, cols, jnp.where(keep(sp_ref[:, cols], lane_ch + c * ch), 0.0, NEG))
            return carry

        lax.fori_loop(0, nchunk, to_bias, 0)
        unpack_rows(out_n_ref, slice(0, 128), jnp.where(keep(sn, lane + past), 0.0, NEG))


def _sample_attn_step(t, n_steps, slot, q_ref, bp_ref, bn_ref, knew_ref, vnew_ref, pbuf,
                      out_ref, s_ref, p_ref, l_ref, acc_ref):
    npg = pbuf.shape[1]
    nt = (((1,), (1,)), ((), ()))
    rg = q_ref.shape[1] // N_KV_HEADS
    n_tok = rg // GROUP
    width = npg * PAGE_SIZE
    past = n_steps * width
    sm_chunk = 2048

    def head_rows(g):
        rows = [pbuf[slot, r, pl.ds(g, PAGE_SIZE, stride=N_KV_HEADS), :] for r in range(npg)]
        return jnp.concatenate(rows, axis=0).astype(BF16)

    def per_head(mask8):
        return jnp.concatenate([_tile_rows(mask8, n_tok)] * (rg // 8), axis=0)

    @pl.when(t < n_steps)
    def _logits():
        off = pl.multiple_of(t * width, width)
        bias = per_head(bp_ref[0])
        for g in range(N_KV_HEADS):
            qg = q_ref[0, g * rg:(g + 1) * rg, :]
            s_ref[g * rg:(g + 1) * rg, pl.ds(off, width)] = lax.dot_general(
                qg, head_rows(g), nt, preferred_element_type=F32) + bias

    @pl.when(t == n_steps - 1)
    def _softmax():
        biasn = per_head(bn_ref[0])
        for g in range(N_KV_HEADS):
            qg = q_ref[0, g * rg:(g + 1) * rg, :]
            kg = knew_ref[0, :, g * HEAD_DIM:(g + 1) * HEAD_DIM]
            s_ref[g * rg:(g + 1) * rg, past:past + PAGE_SIZE] = lax.dot_general(
                qg, kg, nt, preferred_element_type=F32) + biasn

        def fold(x, op):
            f = x[:, 0:128]
            for q in range(1, x.shape[1] // 128):
                f = op(f, x[:, q * 128:(q + 1) * 128])
            return f

        def max_chunk(c, m):
            off = pl.multiple_of(c * sm_chunk, sm_chunk)
            return jnp.maximum(m, fold(s_ref[:, pl.ds(off, sm_chunk)], jnp.maximum))

        m = lax.fori_loop(0, past // sm_chunk, max_chunk, s_ref[:, past:past + PAGE_SIZE])
        m = jnp.max(m, axis=1, keepdims=True)

        def exp_chunk(c, l):
            off = pl.multiple_of(c * sm_chunk, sm_chunk)
            p = jnp.exp2(s_ref[:, pl.ds(off, sm_chunk)] - m)
            p_ref[:, pl.ds(off, sm_chunk)] = p.astype(BF16)
            return l + fold(p, jnp.add)

        pn = jnp.exp2(s_ref[:, past:past + PAGE_SIZE] - m)
        p_ref[:, past:past + PAGE_SIZE] = pn.astype(BF16)
        l_ref[...] = lax.fori_loop(0, past // sm_chunk, exp_chunk, pn)
        acc_ref[...] = jnp.zeros(acc_ref.shape, F32)

    @pl.when(t >= n_steps)
    def _values():
        off = pl.multiple_of((t - n_steps) * width, width)
        for g in range(N_KV_HEADS):
            rs = slice(g * rg, (g + 1) * rg)
            acc_ref[rs, :] += jnp.dot(p_ref[rs, pl.ds(off, width)], head_rows(g),
                                      preferred_element_type=F32)

    @pl.when(t == 2 * n_steps - 1)
    def _finish():
        for g in range(N_KV_HEADS):
            rs = slice(g * rg, (g + 1) * rg)
            vg = vnew_ref[0, :, g * HEAD_DIM:(g + 1) * HEAD_DIM]
            acc_ref[rs, :] += jnp.dot(p_ref[rs, past:past + PAGE_SIZE], vg, preferred_element_type=F32)
        out_ref[0] = acc_ref[...] / jnp.sum(l_ref[...], axis=1, keepdims=True)


def _sample_operands(qT, qiT, wT, kibf, kbf, v32, cache_kidx, n_seq, n_tok):
    def rows_ht(xT, n_heads, dim):
        x = xT.reshape(n_heads, dim, n_seq, n_tok).transpose(2, 0, 3, 1)
        return x.reshape(n_seq, n_heads * n_tok, dim)

    qi_rows = rows_ht(qiT, N_IDX_HEADS, IDX_DIM)
    q_rows = rows_ht(qT, N_HEADS, HEAD_DIM)
    wcol = wT.reshape(N_IDX_HEADS, n_seq, n_tok).transpose(1, 0, 2).reshape(n_seq, N_IDX_HEADS * n_tok, 1)

    def pad_keys(x):
        x = x.reshape(n_seq, n_tok, x.shape[-1])
        return jnp.pad(x, ((0, 0), (0, PAGE_SIZE - n_tok), (0, 0)))

    ki_new_t = jnp.swapaxes(pad_keys(kibf[:, 0:IDX_DIM]), 1, 2)
    kidx_t = jnp.swapaxes(cache_kidx, 1, 2)
    k_new = pad_keys(kbf)
    v_new = pad_keys(v32.reshape(n_seq * n_tok, N_KV).astype(BF16))
    return (qi_rows, wcol, ki_new_t, kidx_t), (q_rows, k_new, v_new)


def _sample_topk(sp, sn, n_tok):
    n_seq, _, past = sp.shape
    rows = n_seq * 8
    assert 8 % n_tok == 0 and n_seq % (8 // n_tok) == 0
    bp, bn = pl.pallas_call(
        functools.partial(_s_thresh_body, n_tok),
        grid=(1,),
        in_specs=[pl.BlockSpec((rows, past), lambda i: (0, 0)),
                  pl.BlockSpec((rows, PAGE_SIZE), lambda i: (0, 0))],
        out_specs=[pl.BlockSpec((rows, past), lambda i: (0, 0)),
                   pl.BlockSpec((rows, PAGE_SIZE), lambda i: (0, 0))],
        out_shape=[jax.ShapeDtypeStruct((rows, past), F32),
                   jax.ShapeDtypeStruct((rows, PAGE_SIZE), F32)],
        scratch_shapes=[pltpu.VMEM((n_seq * n_tok, past), F32)],
        compiler_params=_cparams(1),
        name="sample_topk_mask",
    )(sp.reshape(rows, past), sn.reshape(rows, PAGE_SIZE))
    return bp.reshape(n_seq, 8, past), bn.reshape(n_seq, 8, PAGE_SIZE)


def _sample_rows_to_tokens(out, n_seq, n_tok):
    out = out.reshape(n_seq, N_HEADS, n_tok, HEAD_DIM)
    return out.transpose(0, 2, 1, 3).reshape(n_seq * n_tok, D_ATTN)


def _outproj_body(gated, *refs):
    if gated:
        x_ref, a_ref, gate_ref, mc_ref, wo_ref, o_ref, wo_o = refs
        ma = (a_ref[...] * gate_ref[...]).astype(BF16)
        wo_o[...] = wo_ref[...].astype(BF16)
        wo_ref = wo_o
    else:
        x_ref, a_ref, mc_ref, wo_ref, o_ref = refs
        ma = a_ref[...]
    acc = jnp.dot(ma, wo_ref[0:D_ATTN, :], preferred_element_type=F32)
    acc = acc + jnp.dot(mc_ref[...], wo_ref[D_ATTN:D_ATTN + D_CONV, :], preferred_element_type=F32)
    o_ref[...] = x_ref[...] + acc


def _outproj(x, attn, gate, mixc, wo, tm):
    m = x.shape[0]
    gated = gate is not None
    assert not gated or m == tm
    row = lambda i: (i, 0)
    in_specs = [pl.BlockSpec((tm, D_MODEL), row), pl.BlockSpec((tm, D_ATTN), row)]
    args = [x, attn]
    if gated:
        in_specs.append(pl.BlockSpec((tm, D_ATTN), row))
        args.append(gate)
    in_specs += [pl.BlockSpec((tm, D_CONV), row),
                 pl.BlockSpec((D_ATTN + D_CONV, D_MODEL), lambda i: (0, 0))]
    args += [mixc, wo]
    out_specs = [pl.BlockSpec((tm, D_MODEL), row)]
    out_shape = [jax.ShapeDtypeStruct((m, D_MODEL), F32)]
    if gated:
        out_specs.append(pl.BlockSpec((D_ATTN + D_CONV, D_MODEL), lambda i: (0, 0)))
        out_shape.append(jax.ShapeDtypeStruct((D_ATTN + D_CONV, D_MODEL), BF16))
    out = pl.pallas_call(
        functools.partial(_outproj_body, gated),
        grid=(m // tm,),
        in_specs=in_specs,
        out_specs=out_specs,
        out_shape=out_shape,
        compiler_params=_cparams(1),
        name="outproj_sample" if gated else "outproj_prompt",
    )(*args)
    return out if gated else out[0]


def _rope_tables(pos):
    posf = np.asarray(pos, np.float64)[:, None]
    n = posf.shape[0]

    def cs(half):
        inv = ROPE_THETA ** (-np.arange(half, dtype=np.float64) / half)
        ang = posf * inv[None, :]
        return np.cos(ang), np.sin(ang)

    c16, s16 = cs(ROPE_HALF)
    c8, s8 = cs(IDX_ROPE_HALF)
    one = lambda w: np.ones((n, w))
    zero = lambda w: np.zeros((n, w))
    rest = HEAD_DIM - ROPE_DIM
    k_c = np.concatenate([c16, c16, one(rest)], axis=1)
    k_sa = np.concatenate([-s16, zero(HEAD_DIM - ROPE_HALF)], axis=1)
    k_sb = np.concatenate([zero(ROPE_HALF), s16, zero(rest)], axis=1)
    tabs = dict(c16T=c16.T, s16T=s16.T, c8T=c8.T, s8T=s8.T, kC=k_c, kSa=k_sa, kSb=k_sb)
    return {name: jnp.asarray(np.ascontiguousarray(t), F32) for name, t in tabs.items()}


def _prep_weights(w_in, g_q, g_k, g_kidx, w_conv, w_out):
    assert w_in.shape == (D_MODEL, D_IN)
    wT = w_in.T
    gq = g_q.reshape(HEAD_DIM, 1)
    gk = g_k.reshape(1, HEAD_DIM)
    gki = g_kidx.reshape(IDX_DIM, 1)
    return wT, (gq, gk, gki, w_conv), w_out


def kernel(x_prompt, x_sample, cache_k, cache_v, cache_kidx, state_conv, page_table,
           norm_in, w_in, g_q, g_k, g_kidx, w_conv, w_out):
    n_b, seq, _ = x_prompt.shape
    n_s, n_t, _ = x_sample.shape
    depth = w_in.shape[0]
    past = page_table.shape[1] * PAGE_SIZE
    tabs_p = _rope_tables(np.arange(seq))
    tabs_s = _rope_tables(np.tile(past + np.arange(n_t), n_s))

    hp = x_prompt.reshape(n_b * seq, D_MODEL)
    hs = x_sample.reshape(n_s * n_t, D_MODEL)
    outs = [[] for _ in range(8)]
    for l in range(depth):
        w_t, params, wo = _prep_weights(w_in[l], g_q[l], g_k[l], g_kidx[l], w_conv[l], w_out[l])

        st = state_conv[l]
        tok = jnp.arange(n_t)
        e1 = st[:, jnp.full((n_t,), CONV_W - 2)].reshape(n_s * n_t, D_CONV)
        e2 = st[:, jnp.minimum(tok, CONV_W - 2)].reshape(n_s * n_t, D_CONV)
        (qT_s, qiT_s, _, wT_s, k32_s, kbf_s, v32_s, ki32_s, kibf_s, gate_s, mixc_s, u, *w_bf) = _project(
            hs, norm_in[l], w_t, params, tabs_s, n_t, state_rows=(e1, e2))
        idx_ops, (q_rows, k_new, v_new) = _sample_operands(
            qT_s, qiT_s, wT_s, kibf_s, kbf_s, v32_s, cache_kidx[l], n_s, n_t)
        (qT, qiT, vT, wT, k32, kbf, v32, ki32, kibf, gate, mixc, utail, sp, sn) = _project(
            hp, norm_in[l], tuple(w_bf), params, tabs_p, 0, side=(page_table,) + idx_ops)

        bp, bn = _sample_topk(sp, sn, n_t)
        mixa, attn_rows = _attention(qiT, wT, kibf, qT, kbf, vT, gate, n_b, seq,
                                     q_rows, bp, bn, k_new, v_new, cache_k[l], cache_v[l], page_table)
        attn_s = _sample_rows_to_tokens(attn_rows, n_s, n_t)

        hs, wo_bf = _outproj(hs, attn_s, gate_s, mixc_s, wo, n_s * n_t)
        hp = _outproj(hp, mixa, None, mixc, wo_bf, ROW_TM)
        tps = seq // PROJ_TM
        outs[0].append(k32.reshape(n_b, seq, N_KV_HEADS, HEAD_DIM))
        outs[1].append(v32.reshape(n_b, seq, N_KV_HEADS, HEAD_DIM))
        outs[2].append(ki32.reshape(n_b, seq, IDX_DIM))
        outs[3].append(utail[tps - 1::tps])
        outs[4].append(k32_s.reshape(n_s, n_t, N_KV_HEADS, HEAD_DIM))
        outs[5].append(v32_s.reshape(n_s, n_t, N_KV_HEADS, HEAD_DIM))
        outs[6].append(ki32_s.reshape(n_s, n_t, IDX_DIM))
        outs[7].append(u.reshape(n_s, n_t, D_CONV)[:, n_t - (CONV_W - 1):])

    return (hp.reshape(n_b, seq, D_MODEL), hs.reshape(n_s, n_t, D_MODEL),
            *[jnp.stack(o) for o in outs])
```

```python
import functools

import jax
import jax.numpy as jnp
import numpy as np
from jax import lax
from jax.experimental import pallas as pl
from jax.experimental.pallas import tpu as pltpu

F32 = jnp.float32
BF16 = jnp.bfloat16
I32 = jnp.int32

D_MODEL = 2048
HEAD_DIM = 128
N_HEADS = 8
N_KV_HEADS = 2
GROUP = N_HEADS // N_KV_HEADS
D_ATTN = N_HEADS * HEAD_DIM
D_CONV = 1024
ROPE_DIM = HEAD_DIM // 4
ROPE_HALF = ROPE_DIM // 2
ROPE_THETA = 500000.0
N_IDX_HEADS = 16
IDX_DIM = 64
IDX_ROPE_HALF = IDX_DIM // 8
TOPK_MAX = 256
CONV_W = 3
PAGE_SIZE = 128
EPS = 1e-6
W_IDX_SCALE = (N_IDX_HEADS ** -0.5) * (IDX_DIM ** -0.5)
ATTN_SCALE = HEAD_DIM ** -0.5
Q_PRESCALE = ATTN_SCALE * 1.4426950408889634

INT_MIN = -(2 ** 31)
NEG = -1e30

VMEM_LIMIT_BYTES = 60 * 1024 * 1024

PROJ_TN = 512
PROJ_TM = 1024
PROJ_SUB = 256
ROW_TM = 512
N_KV = N_KV_HEADS * HEAD_DIM
OFF_Q = 0
OFF_K = OFF_Q + D_ATTN
OFF_V = OFF_K + N_KV
OFF_Z = OFF_V + N_KV
OFF_QI = OFF_Z + D_ATTN
OFF_KI = OFF_QI + N_IDX_HEADS * IDX_DIM
OFF_WI = OFF_KI + IDX_DIM
OFF_H = OFF_WI + N_IDX_HEADS
OFF_B = OFF_H + D_CONV
OFF_C = OFF_B + D_CONV
OFF_ZC = OFF_C + D_CONV
D_IN = OFF_ZC + D_CONV
assert OFF_K == 2 * PROJ_TN and OFF_Z == 3 * PROJ_TN and OFF_QI == 5 * PROJ_TN and OFF_KI == 7 * PROJ_TN
KIW_ROWS = 128
assert OFF_KI % KIW_ROWS == 0 and OFF_WI - OFF_KI == IDX_DIM
CONV_CHUNK = 256
N_CONV_CHUNKS = D_CONV // CONV_CHUNK
J_Q = 0
J_KV = 2
J_Z = 3
J_QI = 5
J_CONV = 7
N_MAIN_BLOCKS = J_CONV
N_STEPS = J_CONV + N_CONV_CHUNKS

ATT_TQ = 256
ATT_TK = 256
CNT_ACCS = 4

SIDE_PAGES = 16


def _cparams(n_axes):
    return pltpu.CompilerParams(
        dimension_semantics=("arbitrary",) * n_axes,
        vmem_limit_bytes=VMEM_LIMIT_BYTES,
    )


def _silu(x):
    return x * jax.nn.sigmoid(x)


def _tile_rows(x8, period):
    row = lax.broadcasted_iota(I32, x8.shape, 0)
    out = x8
    for q in range(1, 8 // period):
        out = jnp.where(row // period == q, pltpu.roll(x8, q * period, axis=0), out)
    return out


def _indexer_rows_scores(qi, wcol, keys_t):
    n_tok = qi.shape[0] // N_IDX_HEADS
    d = jnp.dot(qi, keys_t, preferred_element_type=F32)
    val = jnp.maximum(d, 0.0) * wcol
    v = val.reshape(val.shape[0] // 8, 8, val.shape[-1]).sum(axis=0)
    out = v
    for q in range(1, 8 // n_tok):
        out = out + pltpu.roll(v, q * n_tok, axis=0)
    return out


class _SideScores:
    def __init__(self, step, n_total, spq, pt_ref, qi_ref, wcol_ref, kinew_ref, kidx_hbm, sp_ref, sn_ref,
                 kbuf, ksem):
        self.step, self.n_total, self.spq, self.pt_ref = step, n_total, spq, pt_ref
        self.qi_ref, self.wcol_ref, self.kinew_ref, self.kidx_hbm = qi_ref, wcol_ref, kinew_ref, kidx_hbm
        self.sp_ref, self.sn_ref, self.kbuf, self.ksem = sp_ref, sn_ref, kbuf, ksem
        self.npg = kbuf.shape[1]
        self.n_side = pt_ref.shape[0] * spq

    def _start(self, n):
        c = jnp.minimum(n, self.n_side - 1)
        seq_n = c // self.spq
        base = (c % self.spq) * self.npg
        for r in range(self.npg):
            _page_copy(self.kidx_hbm, self.pt_ref[seq_n, base + r], self.kbuf, n % 2, r, self.ksem).start()

    def prologue(self):
        pl.when(self.step == 0)(lambda: self._start(self.step))

    def __call__(self):
        pl.when(self.step + 1 < self.n_total)(lambda: self._start(self.step + 1))
        slot = self.step % 2
        for r in range(self.npg):
            _page_copy(self.kidx_hbm, 0, self.kbuf, slot, r, self.ksem).wait()
        qi = self.qi_ref[0]
        wcol = self.wcol_ref[0]
        keys_t = jnp.concatenate([self.kbuf[slot, r] for r in range(self.npg)], axis=1).astype(BF16)
        self.sp_ref[0] = _indexer_rows_scores(qi, wcol, keys_t)
        self.sn_ref[0] = _indexer_rows_scores(qi, wcol, self.kinew_ref[0])


class _Bf16View:
    def __init__(self, ref):
        self.ref = ref

    def __getitem__(self, idx):
        return self.ref[idx].astype(BF16)


def _proj_body(sample, tps, tm, side_spq, convert, *refs):
    if side_spq:
        pt_ref, refs = refs[0], refs[1:]
    (x_ref, gin_ref, wa_ref, wkiw_ref, wh_ref, wb_ref, wc_ref, wzc_ref, gq_ref, gk_ref, gki_ref,
     c16_ref, s16_ref, c8_ref, s8_ref,
     kc_ref, ksa_ref, ksb_ref, wconv_ref) = refs[:19]
    refs = refs[19:]
    if sample:
        e1_ref, e2_ref = refs[:2]
        refs = refs[2:]
    if side_spq:
        side_in, refs = refs[:4], refs[4:]
    (qT_ref, qiT_ref, vT_ref, wT_ref, k32_ref, kbf_ref, v32_ref, ki32_ref, kibf_ref,
     gate_ref, mixc_ref, u_ref) = refs[:12]
    refs = refs[12:]
    if convert:
        w_out, refs = refs[:6], refs[6:]
    if side_spq:
        side_out, refs = refs[:2], refs[2:]
    xn_ref = refs[0]
    if not sample:
        carry_ref = refs[1]

    i = pl.program_id(0)
    j = pl.program_id(1)
    nt = (((1,), (1,)), ((), ()))
    if side_spq:
        side = _SideScores(i * N_STEPS + j, pl.num_programs(0) * N_STEPS, side_spq, pt_ref,
                           *side_in, *side_out, *refs[2:4])
        side.prologue()
    else:
        side = lambda: None

    if convert:
        wa_o, kiw_o, wh_o, wb_o, wc_o, wzc_o = w_out

        @pl.when(j < N_MAIN_BLOCKS)
        def _():
            wa_o[...] = wa_ref[...].astype(BF16)

        @pl.when(j == J_KV)
        def _():
            kiw_o[...] = wkiw_ref[...].astype(BF16)

        @pl.when(j >= J_CONV)
        def _():
            for src, dst in ((wh_ref, wh_o), (wb_ref, wb_o), (wc_ref, wc_o), (wzc_ref, wzc_o)):
                dst[...] = src[...].astype(BF16)

        wa_ref, wkiw_ref, wh_ref, wb_ref, wc_ref, wzc_ref = (
            _Bf16View(r) for r in (wa_ref, wkiw_ref, wh_ref, wb_ref, wc_ref, wzc_ref))

    def xw(w, x=None):
        return lax.dot_general(xn_ref[...] if x is None else x, w, nt, preferred_element_type=F32)

    def wx(w, x):
        return lax.dot_general(w, x, nt, preferred_element_type=F32)

    nsub = max(tm // PROJ_SUB, 1)
    sub = tm // nsub
    pieces = [slice(rb * sub, (rb + 1) * sub) for rb in range(nsub)]

    if not sample:
        @pl.when((i == 0) & (j == 0))
        def _init():
            carry_ref[...] = jnp.zeros(carry_ref.shape, F32)

    def q_step(with_norm):
        side()
        for cs in pieces:
            if with_norm:
                x = x_ref[cs, :]
                ms = jnp.mean(x * x, axis=-1, keepdims=True)
                xn_ref[cs, :] = (x * lax.rsqrt(ms + EPS) * gin_ref[...]).astype(BF16)
            res = wx(wa_ref[...], xn_ref[cs, :])
            cos = c16_ref[:, cs]
            sin = s16_ref[:, cs]
            for hh in range(PROJ_TN // HEAD_DIM):
                blk = res[hh * HEAD_DIM:(hh + 1) * HEAD_DIM]
                ms = jnp.mean(blk * blk, axis=0, keepdims=True)
                y = blk * lax.rsqrt(ms + EPS) * gq_ref[...]
                x1 = y[0:ROPE_HALF]
                x2 = y[ROPE_HALF:ROPE_DIM]
                base = hh * HEAD_DIM
                qT_ref[base:base + ROPE_HALF, cs] = ((x1 * cos - x2 * sin) * Q_PRESCALE).astype(BF16)
                qT_ref[base + ROPE_HALF:base + ROPE_DIM, cs] = ((x2 * cos + x1 * sin) * Q_PRESCALE).astype(BF16)
                qT_ref[base + ROPE_DIM:base + HEAD_DIM, cs] = (y[ROPE_DIM:] * Q_PRESCALE).astype(BF16)

    pl.when(j == J_Q)(functools.partial(q_step, True))
    pl.when((j > J_Q) & (j < J_KV))(functools.partial(q_step, False))

    @pl.when((j >= J_QI) & (j < J_CONV))
    def _qi():
        side()
        for cs in pieces:
            res = wx(wa_ref[...], xn_ref[cs, :])
            cos = c8_ref[:, cs]
            sin = s8_ref[:, cs]
            for hh in range(PROJ_TN // IDX_DIM):
                blk = res[hh * IDX_DIM:(hh + 1) * IDX_DIM]
                x1 = blk[0:IDX_ROPE_HALF]
                x2 = blk[IDX_ROPE_HALF:2 * IDX_ROPE_HALF]
                rot = jnp.concatenate([x1 * cos - x2 * sin, x2 * cos + x1 * sin], axis=0)
                base = hh * IDX_DIM
                qiT_ref[base:base + 2 * IDX_ROPE_HALF, cs] = rot.astype(BF16)
                qiT_ref[base + 2 * IDX_ROPE_HALF:base + IDX_DIM, cs] = blk[2 * IDX_ROPE_HALF:].astype(BF16)

    @pl.when(j == J_KV)
    def _kv():
        side()
        for rb, cs in enumerate(pieces):
            x = xn_ref[cs, :]
            res = xw(wa_ref[...], x)
            for hd in range(N_KV_HEADS):
                hs = slice(hd * HEAD_DIM, (hd + 1) * HEAD_DIM)
                blk = res[:, hs]
                ms = jnp.mean(blk * blk, axis=-1, keepdims=True)
                yk = blk * lax.rsqrt(ms + EPS) * gk_ref[...]
                rot = yk * kc_ref[cs, :] + (pltpu.roll(yk, HEAD_DIM - ROPE_HALF, axis=1) * ksa_ref[cs, :]
                                            + pltpu.roll(yk, ROPE_HALF, axis=1) * ksb_ref[cs, :])
                k32_ref[pl.ds(N_KV_HEADS * rb * sub + hd, sub, stride=N_KV_HEADS), :] = rot
                kbf_ref[cs, hs] = rot.astype(BF16)
            v = res[:, N_KV:2 * N_KV]
            for hd in range(N_KV_HEADS):
                v32_ref[pl.ds(N_KV_HEADS * rb * sub + hd, sub, stride=N_KV_HEADS), :] = (
                    v[:, hd * HEAD_DIM:(hd + 1) * HEAD_DIM])
            if sample:
                vT_ref[:, cs] = wx(wa_ref[N_KV:2 * N_KV, :], x).astype(BF16)
            else:
                vT_ref[:, cs] = v.T.astype(BF16)
            r2t = wx(wkiw_ref[...], x)
            wT_ref[:, cs] = r2t[IDX_DIM:IDX_DIM + N_IDX_HEADS] * W_IDX_SCALE
            kit = r2t[0:IDX_DIM]
            ms = jnp.mean(kit * kit, axis=0, keepdims=True)
            yi = kit * lax.rsqrt(ms + EPS) * gki_ref[...]
            x1 = yi[0:IDX_ROPE_HALF]
            x2 = yi[IDX_ROPE_HALF:2 * IDX_ROPE_HALF]
            cos = c8_ref[:, cs]
            sin = s8_ref[:, cs]
            roti = jnp.concatenate([x1 * cos - x2 * sin, x2 * cos + x1 * sin, yi[2 * IDX_ROPE_HALF:],
                                    jnp.zeros((KIW_ROWS - IDX_DIM, sub), F32)], axis=0)
            ki_nat = roti.T
            ki32_ref[cs, :] = ki_nat[:, 0:IDX_DIM]
            kibf_ref[cs, :] = ki_nat.astype(BF16)

    @pl.when((j >= J_Z) & (j < J_QI))
    def _z():
        side()
        for cs in pieces:
            gate_ref[cs, :] = _silu(xw(wa_ref[...], xn_ref[cs, :]))

    @pl.when(j >= J_CONV)
    def _conv():
        side()
        cc = j - J_CONV
        u = xw(wc_ref[...]) * xw(wh_ref[...])
        rowid = lax.broadcasted_iota(I32, (tm, CONV_CHUNK), 0)
        if sample:
            t = rowid & (sample - 1)
            u1 = jnp.where(t >= 1, pltpu.roll(u, 1, axis=0), e1_ref[...])
            u2 = jnp.where(t >= 2, pltpu.roll(u, 2, axis=0), e2_ref[...])
            u_ref[...] = u
        else:
            first = (i % tps) == 0
            prev = carry_ref[cc]
            p0 = jnp.where(first, 0.0, prev[0:1])
            p1 = jnp.where(first, 0.0, prev[1:2])
            u1 = jnp.where(rowid == 0, p1, pltpu.roll(u, 1, axis=0))
            u2 = jnp.where(rowid == 0, p0, jnp.where(rowid == 1, p1, pltpu.roll(u, 2, axis=0)))
            tail = u[tm - 8:tm]
            carry_ref[cc] = jnp.concatenate([tail[6:8], tail[0:6]], axis=0)
            u_ref[0] = tail[6:8]
        w = wconv_ref[...]
        y = u2 * w[0:1] + u1 * w[1:2] + u * w[2:3]
        mixc_ref[...] = (xw(wb_ref[...]) * y * _silu(xw(wzc_ref[...]))).astype(BF16)


def _project(x, norm_g, wts, params, tabs, sample, state_rows=None, side=None):
    m = x.shape[0]
    convert = not isinstance(wts, tuple)
    assert sample & (sample - 1) == 0
    tm = m if sample else PROJ_TM
    n_i = m // tm
    tps = 1 if sample else (tabs["c16T"].shape[1] // tm)
    gq, gk, gki, wconv = params

    def tmap(i):
        return i % tps

    def cchunk(j):
        return jnp.clip(j - J_CONV, 0, N_CONV_CHUNKS - 1)

    main_spec = pl.BlockSpec((PROJ_TN, D_MODEL), lambda i, j, *_: (jnp.minimum(j, N_MAIN_BLOCKS - 1), 0))
    if convert:
        def conv_rows(off):
            return pl.BlockSpec((pl.Element(CONV_CHUNK), pl.Element(D_MODEL)),
                                lambda i, j, *_: (pl.multiple_of(off + CONV_CHUNK * cchunk(j), 16), 0))

        w_specs = [main_spec, pl.BlockSpec((KIW_ROWS, D_MODEL), lambda i, j, *_: (OFF_KI // KIW_ROWS, 0)),
                   conv_rows(OFF_H), conv_rows(OFF_B), conv_rows(OFF_C), conv_rows(OFF_ZC)]
        w_args = [wts] * 6
    else:
        conv_spec = pl.BlockSpec(
            (CONV_CHUNK, D_MODEL),
            lambda i, j, *_: (jnp.where(j < J_QI, N_CONV_CHUNKS - 1, cchunk(j)), 0))
        w_specs = [main_spec, pl.BlockSpec((KIW_ROWS, D_MODEL), lambda i, j, *_: (0, 0))] + [conv_spec] * 4
        w_args = list(wts)

    in_specs = [
        pl.BlockSpec((tm, D_MODEL), lambda i, j, *_: (i, 0)),
        pl.BlockSpec((1, D_MODEL), lambda i, j, *_: (0, 0)),
        *w_specs,
        pl.BlockSpec((HEAD_DIM, 1), lambda i, j, *_: (0, 0)),
        pl.BlockSpec((1, HEAD_DIM), lambda i, j, *_: (0, 0)),
        pl.BlockSpec((IDX_DIM, 1), lambda i, j, *_: (0, 0)),
        pl.BlockSpec((ROPE_HALF, tm), lambda i, j, *_: (0, tmap(i))),
        pl.BlockSpec((ROPE_HALF, tm), lambda i, j, *_: (0, tmap(i))),
        pl.BlockSpec((IDX_ROPE_HALF, tm), lambda i, j, *_: (0, tmap(i))),
        pl.BlockSpec((IDX_ROPE_HALF, tm), lambda i, j, *_: (0, tmap(i))),
        pl.BlockSpec((tm, HEAD_DIM), lambda i, j, *_: (tmap(i), 0)),
        pl.BlockSpec((tm, HEAD_DIM), lambda i, j, *_: (tmap(i), 0)),
        pl.BlockSpec((tm, HEAD_DIM), lambda i, j, *_: (tmap(i), 0)),
        pl.BlockSpec((CONV_W, CONV_CHUNK), lambda i, j, *_: (0, cchunk(j))),
    ]
    args = [x, norm_g.reshape(1, D_MODEL), *w_args, gq, gk, gki,
            tabs["c16T"], tabs["s16T"], tabs["c8T"], tabs["s8T"],
            tabs["kC"], tabs["kSa"], tabs["kSb"], wconv]
    cmap = lambda i, j, *_: (i, cchunk(j))
    if sample:
        in_specs += [pl.BlockSpec((tm, CONV_CHUNK), cmap), pl.BlockSpec((tm, CONV_CHUNK), cmap)]
        args += list(state_rows)
        u_spec = pl.BlockSpec((tm, CONV_CHUNK), cmap)
        u_shape = jax.ShapeDtypeStruct((m, D_CONV), F32)
        scratch = [pltpu.VMEM((tm, D_MODEL), BF16)]
    else:
        u_spec = pl.BlockSpec((1, CONV_W - 1, CONV_CHUNK), lambda i, j, *_: (i, 0, cchunk(j)))
        u_shape = jax.ShapeDtypeStruct((n_i, CONV_W - 1, D_CONV), F32)
        scratch = [pltpu.VMEM((tm, D_MODEL), BF16), pltpu.VMEM((N_CONV_CHUNKS, 8, CONV_CHUNK), F32)]

    out_specs = [
        pl.BlockSpec((PROJ_TN, tm), lambda i, j, *_: (jnp.minimum(j, 1), i)),
        pl.BlockSpec((PROJ_TN, tm), lambda i, j, *_: (jnp.clip(j - J_QI, 0, 1), i)),
        pl.BlockSpec((N_KV, tm), lambda i, j, *_: (0, i)),
        pl.BlockSpec((N_IDX_HEADS, tm), lambda i, j, *_: (0, i)),
        pl.BlockSpec((N_KV_HEADS * tm, HEAD_DIM), lambda i, j, *_: (i, 0)),
        pl.BlockSpec((tm, N_KV), lambda i, j, *_: (i, 0)),
        pl.BlockSpec((N_KV_HEADS * tm, HEAD_DIM), lambda i, j, *_: (i, 0)),
        pl.BlockSpec((tm, IDX_DIM), lambda i, j, *_: (i, 0)),
        pl.BlockSpec((tm, KIW_ROWS), lambda i, j, *_: (i, 0)),
        pl.BlockSpec((tm, PROJ_TN), lambda i, j, *_: (i, jnp.clip(j - J_Z, 0, 1))),
        pl.BlockSpec((tm, CONV_CHUNK), cmap),
        u_spec,
    ]
    out_shape = [
        jax.ShapeDtypeStruct((D_ATTN, m), BF16),
        jax.ShapeDtypeStruct((N_IDX_HEADS * IDX_DIM, m), BF16),
        jax.ShapeDtypeStruct((N_KV, m), BF16),
        jax.ShapeDtypeStruct((N_IDX_HEADS, m), F32),
        jax.ShapeDtypeStruct((N_KV_HEADS * m, HEAD_DIM), F32),
        jax.ShapeDtypeStruct((m, N_KV), BF16),
        jax.ShapeDtypeStruct((N_KV_HEADS * m, HEAD_DIM), F32),
        jax.ShapeDtypeStruct((m, IDX_DIM), F32),
        jax.ShapeDtypeStruct((m, KIW_ROWS), BF16),
        jax.ShapeDtypeStruct((m, D_ATTN), F32),
        jax.ShapeDtypeStruct((m, D_CONV), BF16),
        u_shape,
    ]
    if convert:
        assert n_i == 1, "every weight block must be visited exactly once to be written back"
        out_specs += [
            main_spec,
            pl.BlockSpec((KIW_ROWS, D_MODEL), lambda i, j, *_: (0, 0)),
        ] + [pl.BlockSpec((CONV_CHUNK, D_MODEL), lambda i, j, *_: (cchunk(j), 0))] * 4
        out_shape += [jax.ShapeDtypeStruct((OFF_KI, D_MODEL), BF16),
                      jax.ShapeDtypeStruct((KIW_ROWS, D_MODEL), BF16)]
        out_shape += [jax.ShapeDtypeStruct((D_CONV, D_MODEL), BF16)] * 4
    if side is None:
        return pl.pallas_call(
            functools.partial(_proj_body, sample, tps, tm, 0, convert),
            grid=(n_i, N_STEPS),
            in_specs=in_specs,
            out_specs=out_specs,
            out_shape=out_shape,
            scratch_shapes=scratch,
            compiler_params=_cparams(2),
            name="proj_sample" if sample else "proj_prompt",
        )(*args)

    page_table, qi_rows, wcol, ki_new_t, kidx_t = side
    n_seq, n_pages = page_table.shape
    spq = n_pages // SIDE_PAGES
    n_side = n_seq * spq
    assert spq * SIDE_PAGES == n_pages and n_side <= n_i * N_STEPS
    width = SIDE_PAGES * PAGE_SIZE

    def side_step(i, j):
        return jnp.minimum(i * N_STEPS + j, n_side - 1)

    seq_map = lambda i, j, *_: (side_step(i, j) // spq, 0, 0)
    in_specs += [
        pl.BlockSpec((1,) + qi_rows.shape[1:], seq_map),
        pl.BlockSpec((1,) + wcol.shape[1:], seq_map),
        pl.BlockSpec((1, IDX_DIM, PAGE_SIZE), seq_map),
        pl.BlockSpec(memory_space=pl.ANY),
    ]
    out_specs += [
        pl.BlockSpec((1, 8, width), lambda i, j, *_: (side_step(i, j) // spq, 0, side_step(i, j) % spq)),
        pl.BlockSpec((1, 8, PAGE_SIZE), seq_map),
    ]
    out_shape += [jax.ShapeDtypeStruct((n_seq, 8, n_pages * PAGE_SIZE), F32),
                  jax.ShapeDtypeStruct((n_seq, 8, PAGE_SIZE), F32)]
    scratch += [pltpu.VMEM((2, SIDE_PAGES, IDX_DIM, PAGE_SIZE), F32), pltpu.SemaphoreType.DMA((2,))]
    return pl.pallas_call(
        functools.partial(_proj_body, sample, tps, tm, spq, convert),
        grid_spec=pltpu.PrefetchScalarGridSpec(
            num_scalar_prefetch=1,
            grid=(n_i, N_STEPS),
            in_specs=in_specs,
            out_specs=out_specs,
            scratch_shapes=scratch,
        ),
        out_shape=out_shape,
        compiler_params=_cparams(2),
        name="proj_prompt",
    )(page_table, *args, qi_rows, wcol, ki_new_t, kidx_t)


KEY_LOWEST_FINITE = INT_MIN + 0x00800000


def _key_to_float(key):
    return pltpu.bitcast(key ^ ((key >> 31) & 0x7FFFFFFF), F32)


def _bit_value(b):
    return lax.shift_left(jnp.int32(1), jnp.int32(31) - b)


def _prompt_attn_step(i, qiT_ref, wT_ref, ki_ref, qT_ref, k_ref, vT_ref, gate_ref, o_ref,
                      sc_ref, bias_ref, acc_ref, s_ref):
    tq, tk = ATT_TQ, ATT_TK
    nch = i + 1
    w = wT_ref[...]
    row = lax.broadcasted_iota(I32, (tk, tq), 0)
    col = lax.broadcasted_iota(I32, (tk, tq), 1)
    row8 = lax.broadcasted_iota(I32, (8, tq), 0)

    def score_chunk(j, carry):
        off = pl.multiple_of(j * tk, tk)
        kic = ki_ref[pl.ds(off, tk), 0:IDX_DIM]
        acc = jnp.zeros((tk, tq), F32)
        for h in range(N_IDX_HEADS):
            d = jnp.dot(kic, qiT_ref[h * IDX_DIM:(h + 1) * IDX_DIM, :], preferred_element_type=F32)
            acc = acc + w[h:h + 1, :] * jnp.maximum(d, 0.0)
        future = (row + j * tk) > (col + i * tq)
        sc_ref[pl.ds(off, tk), :] = jnp.where(future, -jnp.inf, acc)
        return carry

    lax.fori_loop(0, nch, score_chunk, 0)

    def count_keys(pred):
        def cnt_chunk(j, cs):
            off = pl.multiple_of(j * tk, tk)
            cs = list(cs)
            sc = sc_ref[pl.ds(off, tk), :]
            for r in range(tk // 8):
                a = cs[r % CNT_ACCS]
                kpos = row8 + (j * tk + r * 8)
                cs[r % CNT_ACCS] = jnp.where(pred(sc[r * 8:(r + 1) * 8], kpos), a + 1, a)
            return tuple(cs)

        cs = lax.fori_loop(0, nch, cnt_chunk, (jnp.zeros((8, tq), I32),) * CNT_ACCS)
        c = cs[0]
        for a in cs[1:]:
            c = c + a
        return jnp.sum(c.astype(F32), axis=0, keepdims=True)

    def bit_body(b, carry):
        thr, cge = carry
        cand = thr + _bit_value(b)
        cand_f = _key_to_float(cand)
        cnt = count_keys(lambda s, kpos: s >= cand_f)
        ok = cnt >= float(TOPK_MAX)
        return jnp.where(ok, cand, thr), jnp.where(ok, cnt, cge)

    n_bits = jnp.where(nch * tk > TOPK_MAX, 32, 0)
    thr, cge = lax.fori_loop(0, n_bits, bit_body, (jnp.full((1, tq), INT_MIN, I32),
                                                   jnp.zeros((1, tq), F32)))
    thr_f = _key_to_float(jnp.maximum(thr, KEY_LOWEST_FINITE))
    has_ties = jnp.max(cge) > float(TOPK_MAX)

    @pl.when(jnp.logical_not(has_ties))
    def _plain_mask():
        def bias_chunk(j, carry):
            off = pl.multiple_of(j * tk, tk)
            bias_ref[pl.ds(off, tk), :] = jnp.where(sc_ref[pl.ds(off, tk), :] >= thr_f, 0.0, NEG)
            return carry

        lax.fori_loop(0, nch, bias_chunk, 0)

    @pl.when(has_ties)
    def _tie_mask():
        need = float(TOPK_MAX) - count_keys(lambda s, kpos: s > thr_f)
        nbits = (sc_ref.shape[0] - 1).bit_length()

        def pos_bit(b, last):
            step = lax.shift_left(jnp.int32(1), jnp.int32(nbits - 1) - b)
            probe = last + (step - 1)
            got = count_keys(lambda s, kpos: (s == thr_f) & (kpos <= probe))
            return jnp.where(got < need, last + step, last)

        last = lax.fori_loop(0, nbits, pos_bit, jnp.zeros((1, tq), I32))

        def bias_chunk(j, carry):
            off = pl.multiple_of(j * tk, tk)
            sc = sc_ref[pl.ds(off, tk), :]
            keep = (sc > thr_f) | ((sc == thr_f) & ((row + j * tk) <= last))
            bias_ref[pl.ds(off, tk), :] = jnp.where(keep, 0.0, NEG)
            return carry

        lax.fori_loop(0, nch, bias_chunk, 0)

    def qk_chunk(j, mrun):
        off = pl.multiple_of(j * tk, tk)
        bias = bias_ref[pl.ds(off, tk), :]
        out = []
        for h in range(N_HEADS):
            g = h // GROUP
            kc = k_ref[pl.ds(off, tk), g * HEAD_DIM:(g + 1) * HEAD_DIM]
            s = jnp.dot(kc, qT_ref[h * HEAD_DIM:(h + 1) * HEAD_DIM, :], preferred_element_type=F32) + bias
            s_ref[h, pl.ds(off, tk), :] = s
            out.append(jnp.maximum(mrun[h], s.reshape(tk // 8, 8, tq).max(axis=0)))
        return tuple(out)

    mrun = lax.fori_loop(0, nch, qk_chunk, (jnp.full((8, tq), NEG, F32),) * N_HEADS)
    ms = [jnp.max(mr, axis=0, keepdims=True) for mr in mrun]
    acc_ref[...] = jnp.zeros(acc_ref.shape, F32)

    def pv_chunk(j, lrun):
        off = pl.multiple_of(j * tk, tk)
        out = []
        for h in range(N_HEADS):
            g = h // GROUP
            hs = slice(h * HEAD_DIM, (h + 1) * HEAD_DIM)
            p = jnp.exp2(s_ref[h, pl.ds(off, tk), :] - ms[h])
            out.append(lrun[h] + p.reshape(tk // 8, 8, tq).sum(axis=0))
            vc = vT_ref[g * HEAD_DIM:(g + 1) * HEAD_DIM, pl.ds(off, tk)]
            acc_ref[hs, :] += jnp.dot(vc, p.astype(BF16), preferred_element_type=F32)
        return tuple(out)

    lrun = lax.fori_loop(0, nch, pv_chunk, (jnp.zeros((8, tq), F32),) * N_HEADS)

    for h in range(N_HEADS):
        hs = slice(h * HEAD_DIM, (h + 1) * HEAD_DIM)
        o = (acc_ref[hs, :] / jnp.sum(lrun[h], axis=0, keepdims=True)).T
        o_ref[:, hs] = (o * gate_ref[:, hs]).astype(BF16)


def _page_copy(cache_hbm, page, pbuf, slot, r, sem):
    return pltpu.make_async_copy(cache_hbm.at[page], pbuf.at[slot, r], sem.at[slot])


def _attn_fused_body(nq, spq, pt_ref, *refs):
    prompt_in = refs[0:7]
    q_ref, bp_ref, bn_ref, knew_ref, vnew_ref, ck_hbm, cv_hbm = refs[7:14]
    o_ref, so_ref = refs[14:16]
    prompt_scratch = refs[16:20]
    pbuf, sem, ss_ref, ps_ref, ls_ref, accs_ref = refs[20:26]

    i = pl.program_id(1)
    step = pl.program_id(0) * nq + i
    n_total = pl.num_programs(0) * nq
    half = spq // 2
    npg = pbuf.shape[1]

    def start_pages(n):
        seq_n = n // spq
        t_n = n % spq
        base = (t_n % half) * npg
        slot = n % 2
        for cache_hbm, pred in ((ck_hbm, t_n < half), (cv_hbm, t_n >= half)):
            @pl.when(pred)
            def _(cache_hbm=cache_hbm):
                for r in range(npg):
                    _page_copy(cache_hbm, pt_ref[seq_n, base + r], pbuf, slot, r, sem).start()

    @pl.when(step == 0)
    def _prologue():
        start_pages(step)

    @pl.when(step + 1 < n_total)
    def _prefetch():
        start_pages(step + 1)

    slot = step % 2
    for r in range(npg):
        _page_copy(ck_hbm, 0, pbuf, slot, r, sem).wait()

    _sample_attn_step(step % spq, half, slot, q_ref, bp_ref, bn_ref, knew_ref, vnew_ref, pbuf,
                      so_ref, ss_ref, ps_ref, ls_ref, accs_ref)
    _prompt_attn_step(i, *prompt_in, o_ref, *prompt_scratch)


def _attention(qiT, wT, kibf, qT, kbf, vT, gate, n_batch, seq,
               q_rows, bp, bn, k_new, v_new, cache_k, cache_v, page_table):
    m = n_batch * seq
    nq = seq // ATT_TQ
    n_seq, n_pages = page_table.shape
    n_steps = n_batch * nq
    spq = n_steps // n_seq
    assert spq * n_seq == n_steps and spq % 2 == 0
    half = spq // 2
    npg = n_pages // half
    assert npg * half == n_pages
    width = npg * PAGE_SIZE
    past = n_pages * PAGE_SIZE
    n_pool = cache_k.shape[0]
    kv_rows = PAGE_SIZE * N_KV_HEADS
    ck = cache_k.reshape(n_pool, kv_rows, HEAD_DIM)
    cv = cache_v.reshape(n_pool, kv_rows, HEAD_DIM)
    nrow = q_rows.shape[1]
    assert (nrow // N_KV_HEADS) % 16 == 0

    qmap = lambda b, i, pt: (0, b * nq + i)
    smap = lambda b, i, pt: ((b * nq + i) // spq, 0, 0)
    return pl.pallas_call(
        functools.partial(_attn_fused_body, nq, spq),
        grid_spec=pltpu.PrefetchScalarGridSpec(
            num_scalar_prefetch=1,
            grid=(n_batch, nq),
            in_specs=[
                pl.BlockSpec((N_IDX_HEADS * IDX_DIM, ATT_TQ), qmap),
                pl.BlockSpec((N_IDX_HEADS, ATT_TQ), qmap),
                pl.BlockSpec((seq, 128), lambda b, i, pt: (b, 0)),
                pl.BlockSpec((D_ATTN, ATT_TQ), qmap),
                pl.BlockSpec((seq, N_KV), lambda b, i, pt: (b, 0)),
                pl.BlockSpec((N_KV, seq), lambda b, i, pt: (0, b)),
                pl.BlockSpec((ATT_TQ, D_ATTN), lambda b, i, pt: (b * nq + i, 0)),
                pl.BlockSpec((1, nrow, HEAD_DIM), smap),
                pl.BlockSpec((1, 8, width),
                             lambda b, i, pt: ((b * nq + i) // spq, 0, jnp.minimum((b * nq + i) % spq, half - 1))),
                pl.BlockSpec((1, 8, PAGE_SIZE), smap),
                pl.BlockSpec((1, PAGE_SIZE, N_KV), smap),
                pl.BlockSpec((1, PAGE_SIZE, N_KV), smap),
                pl.BlockSpec(memory_space=pl.ANY),
                pl.BlockSpec(memory_space=pl.ANY),
            ],
            out_specs=[
                pl.BlockSpec((ATT_TQ, D_ATTN), lambda b, i, pt: (b * nq + i, 0)),
                pl.BlockSpec((1, nrow, HEAD_DIM), smap),
            ],
            scratch_shapes=[
                pltpu.VMEM((seq, ATT_TQ), F32), pltpu.VMEM((seq, ATT_TQ), F32),
                pltpu.VMEM((D_ATTN, ATT_TQ), F32), pltpu.VMEM((N_HEADS, seq, ATT_TQ), F32),
                pltpu.VMEM((2, npg, kv_rows, HEAD_DIM), F32), pltpu.SemaphoreType.DMA((2,)),
                pltpu.VMEM((nrow, past + PAGE_SIZE), F32), pltpu.VMEM((nrow, past + PAGE_SIZE), BF16),
                pltpu.VMEM((nrow, 128), F32), pltpu.VMEM((nrow, HEAD_DIM), F32),
            ],
        ),
        out_shape=[jax.ShapeDtypeStruct((m, D_ATTN), BF16),
                   jax.ShapeDtypeStruct((n_seq, nrow, HEAD_DIM), F32)],
        compiler_params=_cparams(2),
        name="attention",
    )(page_table, qiT, wT, kibf, qT, kbf, vT, gate, q_rows, bp, bn, k_new, v_new, ck, cv)


def _s_thresh_body(n_new, in_p_ref, in_n_ref, out_p_ref, out_n_ref, sp_ref):
    past = in_p_ref.shape[1]
    rows = sp_ref.shape[0]
    pack = 8 // n_new
    groups = rows // 8
    ch = 2048
    nchunk = past // ch

    def pack_rows(src_ref, cols, ncols):
        which = lax.broadcasted_iota(I32, (8, ncols), 0) // n_new
        out = []
        for p in range(groups):
            dense = src_ref[(p * pack) * 8:(p * pack) * 8 + 8, cols]
            for q in range(1, pack):
                piece = src_ref[(p * pack + q) * 8:(p * pack + q) * 8 + 8, cols]
                dense = jnp.where(which == q, pltpu.roll(piece, q * n_new, axis=0), dense)
            out.append(dense)
        return jnp.concatenate(out, axis=0)

    def unpack_rows(dst_ref, cols, dense):
        for p in range(groups):
            piece = dense[p * 8:(p + 1) * 8]
            for q in range(pack):
                rolled = piece if q == 0 else pltpu.roll(piece, 8 - q * n_new, axis=0)
                dst_ref[(p * pack + q) * 8:(p * pack + q) * 8 + 8, cols] = rolled

    def pack_chunk(c, carry):
        cols = pl.ds(pl.multiple_of(c * ch, ch), ch)
        sp_ref[:, cols] = pack_rows(in_p_ref, cols, ch)
        return carry

    lax.fori_loop(0, nchunk, pack_chunk, 0)
    t = lax.broadcasted_iota(I32, (rows, 128), 0) % n_new
    lane = lax.broadcasted_iota(I32, (rows, 128), 1)
    sn = jnp.where((lane < n_new) & (lane <= t), pack_rows(in_n_ref, slice(0, 128), 128), -jnp.inf)

    def fold(x):
        f = x[:, 0:128]
        for q in range(1, x.shape[1] // 128):
            f = f + x[:, q * 128:(q + 1) * 128]
        return f

    lane_ch = lax.broadcasted_iota(I32, (rows, ch), 1)

    def count_keys(pred):
        def cnt_chunk(c, acc):
            off = pl.multiple_of(c * ch, ch)
            return acc + fold(jnp.where(pred(sp_ref[:, pl.ds(off, ch)], lane_ch + c * ch), 1.0, 0.0))

        acc = lax.fori_loop(0, nchunk, cnt_chunk, jnp.where(pred(sn, lane + past), 1.0, 0.0))
        return jnp.sum(acc, axis=1, keepdims=True)

    def bit_body(b, carry):
        thr, cge = carry
        cand = thr + _bit_value(b)
        cand_f = _key_to_float(cand)
        cnt = count_keys(lambda s, kpos: s >= cand_f)
        ok = cnt >= float(TOPK_MAX)
        return jnp.where(ok, cand, thr), jnp.where(ok, cnt, cge)

    thr, cge = lax.fori_loop(0, 32, bit_body, (jnp.full((rows, 1), INT_MIN, I32),
                                               jnp.zeros((rows, 1), F32)))
    thr_f = _key_to_float(jnp.maximum(thr, KEY_LOWEST_FINITE))
    has_ties = jnp.max(cge) > float(TOPK_MAX)

    @pl.when(jnp.logical_not(has_ties))
    def _plain_mask():
        def to_bias(c, carry):
            cols = pl.ds(pl.multiple_of(c * ch, ch), ch)
            unpack_rows(out_p_ref, cols, jnp.where(sp_ref[:, cols] >= thr_f, 0.0, NEG))
            return carry

        lax.fori_loop(0, nchunk, to_bias, 0)
        unpack_rows(out_n_ref, slice(0, 128), jnp.where(sn >= thr_f, 0.0, NEG))

    @pl.when(has_ties)
    def _tie_mask():
        need = float(TOPK_MAX) - count_keys(lambda s, kpos: s > thr_f)
        nbits = (past + 128 - 1).bit_length()

        def pos_bit(b, last):
            step = lax.shift_left(jnp.int32(1), jnp.int32(nbits - 1) - b)
            probe = last + (step - 1)
            got = count_keys(lambda s, kpos: (s == thr_f) & (kpos <= probe))
            return jnp.where(got < need, last + step, last)

        last = lax.fori_loop(0, nbits, pos_bit, jnp.zeros((rows, 1), I32))

        def keep(s, kpos):
            return (s > thr_f) | ((s == thr_f) & (kpos <= last))

        def to_bias(c, carry):
            cols = pl.ds(pl.multiple_of(c * ch, ch), ch)
            unpack_rows(out_p_ref, cols, jnp.where(keep(sp_ref[:, cols], lane_ch + c * ch), 0.0, NEG))
            return carry

        lax.fori_loop(0, nchunk, to_bias, 0)
        unpack_rows(out_n_ref, slice(0, 128), jnp.where(keep(sn, lane + past), 0.0, NEG))


def _sample_attn_step(t, n_steps, slot, q_ref, bp_ref, bn_ref, knew_ref, vnew_ref, pbuf,
                      out_ref, s_ref, p_ref, l_ref, acc_ref):
    npg = pbuf.shape[1]
    nt = (((1,), (1,)), ((), ()))
    rg = q_ref.shape[1] // N_KV_HEADS
    n_tok = rg // GROUP
    width = npg * PAGE_SIZE
    past = n_steps * width
    sm_chunk = 2048

    def head_rows(g):
        rows = [pbuf[slot, r, pl.ds(g, PAGE_SIZE, stride=N_KV_HEADS), :] for r in range(npg)]
        return jnp.concatenate(rows, axis=0).astype(BF16)

    def per_head(mask8):
        return jnp.concatenate([_tile_rows(mask8, n_tok)] * (rg // 8), axis=0)

    @pl.when(t < n_steps)
    def _logits():
        off = pl.multiple_of(t * width, width)
        bias = per_head(bp_ref[0])
        for g in range(N_KV_HEADS):
            qg = q_ref[0, g * rg:(g + 1) * rg, :]
            s_ref[g * rg:(g + 1) * rg, pl.ds(off, width)] = lax.dot_general(
                qg, head_rows(g), nt, preferred_element_type=F32) + bias

    @pl.when(t == n_steps - 1)
    def _softmax():
        biasn = per_head(bn_ref[0])
        for g in range(N_KV_HEADS):
            qg = q_ref[0, g * rg:(g + 1) * rg, :]
            kg = knew_ref[0, :, g * HEAD_DIM:(g + 1) * HEAD_DIM]
            s_ref[g * rg:(g + 1) * rg, past:past + PAGE_SIZE] = lax.dot_general(
                qg, kg, nt, preferred_element_type=F32) + biasn

        def fold(x, op):
            f = x[:, 0:128]
            for q in range(1, x.shape[1] // 128):
                f = op(f, x[:, q * 128:(q + 1) * 128])
            return f

        def max_chunk(c, m):
            off = pl.multiple_of(c * sm_chunk, sm_chunk)
            return jnp.maximum(m, fold(s_ref[:, pl.ds(off, sm_chunk)], jnp.maximum))

        m = lax.fori_loop(0, past // sm_chunk, max_chunk, s_ref[:, past:past + PAGE_SIZE])
        m = jnp.max(m, axis=1, keepdims=True)

        def exp_chunk(c, l):
            off = pl.multiple_of(c * sm_chunk, sm_chunk)
            p = jnp.exp2(s_ref[:, pl.ds(off, sm_chunk)] - m)
            p_ref[:, pl.ds(off, sm_chunk)] = p.astype(BF16)
            return l + fold(p, jnp.add)

        pn = jnp.exp2(s_ref[:, past:past + PAGE_SIZE] - m)
        p_ref[:, past:past + PAGE_SIZE] = pn.astype(BF16)
        l_ref[...] = lax.fori_loop(0, past // sm_chunk, exp_chunk, pn)
        acc_ref[...] = jnp.zeros(acc_ref.shape, F32)

    @pl.when(t >= n_steps)
    def _values():
        off = pl.multiple_of((t - n_steps) * width, width)
        for g in range(N_KV_HEADS):
            rs = slice(g * rg, (g + 1) * rg)
            acc_ref[rs, :] += jnp.dot(p_ref[rs, pl.ds(off, width)], head_rows(g),
                                      preferred_element_type=F32)

    @pl.when(t == 2 * n_steps - 1)
    def _finish():
        for g in range(N_KV_HEADS):
            rs = slice(g * rg, (g + 1) * rg)
            vg = vnew_ref[0, :, g * HEAD_DIM:(g + 1) * HEAD_DIM]
            acc_ref[rs, :] += jnp.dot(p_ref[rs, past:past + PAGE_SIZE], vg, preferred_element_type=F32)
        out_ref[0] = acc_ref[...] / jnp.sum(l_ref[...], axis=1, keepdims=True)


def _sample_operands(qT, qiT, wT, kibf, kbf, v32, cache_kidx, n_seq, n_tok):
    def rows_ht(xT, n_heads, dim):
        x = xT.reshape(n_heads, dim, n_seq, n_tok).transpose(2, 0, 3, 1)
        return x.reshape(n_seq, n_heads * n_tok, dim)

    qi_rows = rows_ht(qiT, N_IDX_HEADS, IDX_DIM)
    q_rows = rows_ht(qT, N_HEADS, HEAD_DIM)
    wcol = wT.reshape(N_IDX_HEADS, n_seq, n_tok).transpose(1, 0, 2).reshape(n_seq, N_IDX_HEADS * n_tok, 1)

    def pad_keys(x):
        x = x.reshape(n_seq, n_tok, x.shape[-1])
        return jnp.pad(x, ((0, 0), (0, PAGE_SIZE - n_tok), (0, 0)))

    ki_new_t = jnp.swapaxes(pad_keys(kibf[:, 0:IDX_DIM]), 1, 2)
    kidx_t = jnp.swapaxes(cache_kidx, 1, 2)
    k_new = pad_keys(kbf)
    v_new = pad_keys(v32.reshape(n_seq * n_tok, N_KV).astype(BF16))
    return (qi_rows, wcol, ki_new_t, kidx_t), (q_rows, k_new, v_new)


def _sample_topk(sp, sn, n_tok):
    n_seq, _, past = sp.shape
    rows = n_seq * 8
    assert 8 % n_tok == 0 and n_seq % (8 // n_tok) == 0
    bp, bn = pl.pallas_call(
        functools.partial(_s_thresh_body, n_tok),
        grid=(1,),
        in_specs=[pl.BlockSpec((rows, past), lambda i: (0, 0)),
                  pl.BlockSpec((rows, PAGE_SIZE), lambda i: (0, 0))],
        out_specs=[pl.BlockSpec((rows, past), lambda i: (0, 0)),
                   pl.BlockSpec((rows, PAGE_SIZE), lambda i: (0, 0))],
        out_shape=[jax.ShapeDtypeStruct((rows, past), F32),
                   jax.ShapeDtypeStruct((rows, PAGE_SIZE), F32)],
        scratch_shapes=[pltpu.VMEM((n_seq * n_tok, past), F32)],
        compiler_params=_cparams(1),
        name="sample_topk_mask",
    )(sp.reshape(rows, past), sn.reshape(rows, PAGE_SIZE))
    return bp.reshape(n_seq, 8, past), bn.reshape(n_seq, 8, PAGE_SIZE)


def _sample_rows_to_tokens(out, n_seq, n_tok):
    out = out.reshape(n_seq, N_HEADS, n_tok, HEAD_DIM)
    return out.transpose(0, 2, 1, 3).reshape(n_seq * n_tok, D_ATTN)


def _outproj_body(gated, *refs):
    if gated:
        x_ref, a_ref, gate_ref, mc_ref, wo_ref, o_ref, wo_o = refs
        ma = (a_ref[...] * gate_ref[...]).astype(BF16)
        wo_o[...] = wo_ref[...].astype(BF16)
        wo_ref = wo_o
    else:
        x_ref, a_ref, mc_ref, wo_ref, o_ref = refs
        ma = a_ref[...]
    acc = jnp.dot(ma, wo_ref[0:D_ATTN, :], preferred_element_type=F32)
    acc = acc + jnp.dot(mc_ref[...], wo_ref[D_ATTN:D_ATTN + D_CONV, :], preferred_element_type=F32)
    o_ref[...] = x_ref[...] + acc


def _outproj(x, attn, gate, mixc, wo, tm):
    m = x.shape[0]
    gated = gate is not None
    assert not gated or m == tm
    row = lambda i: (i, 0)
    in_specs = [pl.BlockSpec((tm, D_MODEL), row), pl.BlockSpec((tm, D_ATTN), row)]
    args = [x, attn]
    if gated:
        in_specs.append(pl.BlockSpec((tm, D_ATTN), row))
        args.append(gate)
    in_specs += [pl.BlockSpec((tm, D_CONV), row),
                 pl.BlockSpec((D_ATTN + D_CONV, D_MODEL), lambda i: (0, 0))]
    args += [mixc, wo]
    out_specs = [pl.BlockSpec((tm, D_MODEL), row)]
    out_shape = [jax.ShapeDtypeStruct((m, D_MODEL), F32)]
    if gated:
        out_specs.append(pl.BlockSpec((D_ATTN + D_CONV, D_MODEL), lambda i: (0, 0)))
        out_shape.append(jax.ShapeDtypeStruct((D_ATTN + D_CONV, D_MODEL), BF16))
    out = pl.pallas_call(
        functools.partial(_outproj_body, gated),
        grid=(m // tm,),
        in_specs=in_specs,
        out_specs=out_specs,
        out_shape=out_shape,
        compiler_params=_cparams(1),
        name="outproj_sample" if gated else "outproj_prompt",
    )(*args)
    return out if gated else out[0]


def _rope_tables(pos):
    posf = np.asarray(pos, np.float64)[:, None]
    n = posf.shape[0]

    def cs(half):
        inv = ROPE_THETA ** (-np.arange(half, dtype=np.float64) / half)
        ang = posf * inv[None, :]
        return np.cos(ang), np.sin(ang)

    c16, s16 = cs(ROPE_HALF)
    c8, s8 = cs(IDX_ROPE_HALF)
    one = lambda w: np.ones((n, w))
    zero = lambda w: np.zeros((n, w))
    rest = HEAD_DIM - ROPE_DIM
    k_c = np.concatenate([c16, c16, one(rest)], axis=1)
    k_sa = np.concatenate([-s16, zero(HEAD_DIM - ROPE_HALF)], axis=1)
    k_sb = np.concatenate([zero(ROPE_HALF), s16, zero(rest)], axis=1)
    tabs = dict(c16T=c16.T, s16T=s16.T, c8T=c8.T, s8T=s8.T, kC=k_c, kSa=k_sa, kSb=k_sb)
    return {name: jnp.asarray(np.ascontiguousarray(t), F32) for name, t in tabs.items()}


def _prep_weights(w_in, g_q, g_k, g_kidx, w_conv, w_out):
    assert w_in.shape == (D_MODEL, D_IN)
    wT = w_in.T
    gq = g_q.reshape(HEAD_DIM, 1)
    gk = g_k.reshape(1, HEAD_DIM)
    gki = g_kidx.reshape(IDX_DIM, 1)
    return wT, (gq, gk, gki, w_conv), w_out


def kernel(x_prompt, x_sample, cache_k, cache_v, cache_kidx, state_conv, page_table,
           norm_in, w_in, g_q, g_k, g_kidx, w_conv, w_out):
    n_b, seq, _ = x_prompt.shape
    n_s, n_t, _ = x_sample.shape
    depth = w_in.shape[0]
    past = page_table.shape[1] * PAGE_SIZE
    tabs_p = _rope_tables(np.arange(seq))
    tabs_s = _rope_tables(np.tile(past + np.arange(n_t), n_s))

    hp = x_prompt.reshape(n_b * seq, D_MODEL)
    hs = x_sample.reshape(n_s * n_t, D_MODEL)
    outs = [[] for _ in range(8)]
    for l in range(depth):
        w_t, params, wo = _prep_weights(w_in[l], g_q[l], g_k[l], g_kidx[l], w_conv[l], w_out[l])

        st = state_conv[l]
        tok = jnp.arange(n_t)
        e1 = st[:, jnp.full((n_t,), CONV_W - 2)].reshape(n_s * n_t, D_CONV)
        e2 = st[:, jnp.minimum(tok, CONV_W - 2)].reshape(n_s * n_t, D_CONV)
        (qT_s, qiT_s, _, wT_s, k32_s, kbf_s, v32_s, ki32_s, kibf_s, gate_s, mixc_s, u, *w_bf) = _project(
            hs, norm_in[l], w_t, params, tabs_s, n_t, state_rows=(e1, e2))
        idx_ops, (q_rows, k_new, v_new) = _sample_operands(
            qT_s, qiT_s, wT_s, kibf_s, kbf_s, v32_s, cache_kidx[l], n_s, n_t)
        (qT, qiT, vT, wT, k32, kbf, v32, ki32, kibf, gate, mixc, utail, sp, sn) = _project(
            hp, norm_in[l], tuple(w_bf), params, tabs_p, 0, side=(page_table,) + idx_ops)

        bp, bn = _sample_topk(sp, sn, n_t)
        mixa, attn_rows = _attention(qiT, wT, kibf, qT, kbf, vT, gate, n_b, seq,
                                     q_rows, bp, bn, k_new, v_new, cache_k[l], cache_v[l], page_table)
        attn_s = _sample_rows_to_tokens(attn_rows, n_s, n_t)

        hs, wo_bf = _outproj(hs, attn_s, gate_s, mixc_s, wo, n_s * n_t)
        hp = _outproj(hp, mixa, None, mixc, wo_bf, ROW_TM)
        tps = seq // PROJ_TM
        outs[0].append(k32.reshape(n_b, seq, N_KV_HEADS, HEAD_DIM))
        outs[1].append(v32.reshape(n_b, seq, N_KV_HEADS, HEAD_DIM))
        outs[2].append(ki32.reshape(n_b, seq, IDX_DIM))
        outs[3].append(utail[tps - 1::tps])
        outs[4].append(k32_s.reshape(n_s, n_t, N_KV_HEADS, HEAD_DIM))
        outs[5].append(v32_s.reshape(n_s, n_t, N_KV_HEADS, HEAD_DIM))
        outs[6].append(ki32_s.reshape(n_s, n_t, IDX_DIM))
        outs[7].append(u.reshape(n_s, n_t, D_CONV)[:, n_t - (CONV_W - 1):])

    return (hp.reshape(n_b, seq, D_MODEL), hs.reshape(n_s, n_t, D_MODEL),
            *[jnp.stack(o) for o in outs])
```

```python
import functools

import jax
import jax.numpy as jnp
import numpy as np
from jax import lax
from jax.experimental import pallas as pl
from jax.experimental.pallas import tpu as pltpu

F32 = jnp.float32
BF16 = jnp.bfloat16
I32 = jnp.int32

D_MODEL = 2048
HEAD_DIM = 128
N_HEADS = 8
N_KV_HEADS = 2
GROUP = N_HEADS // N_KV_HEADS
D_ATTN = N_HEADS * HEAD_DIM
D_CONV = 1024
ROPE_DIM = HEAD_DIM // 4
ROPE_HALF = ROPE_DIM // 2
ROPE_THETA = 500000.0
N_IDX_HEADS = 16
IDX_DIM = 64
IDX_ROPE_HALF = IDX_DIM // 8
TOPK_MAX = 256
CONV_W = 3
PAGE_SIZE = 128
EPS = 1e-6
W_IDX_SCALE = (N_IDX_HEADS ** -0.5) * (IDX_DIM ** -0.5)
ATTN_SCALE = HEAD_DIM ** -0.5
Q_PRESCALE = ATTN_SCALE * 1.4426950408889634

INT_MIN = -(2 ** 31)
NEG = -1e30

VMEM_LIMIT_BYTES = 60 * 1024 * 1024

PROJ_TN = 512
PROJ_TM = 1024
PROJ_SUB = 256
ROW_TM = 512
N_KV = N_KV_HEADS * HEAD_DIM
OFF_Q = 0
OFF_K = OFF_Q + D_ATTN
OFF_V = OFF_K + N_KV
OFF_Z = OFF_V + N_KV
OFF_QI = OFF_Z + D_ATTN
OFF_KI = OFF_QI + N_IDX_HEADS * IDX_DIM
OFF_WI = OFF_KI + IDX_DIM
OFF_H = OFF_WI + N_IDX_HEADS
OFF_B = OFF_H + D_CONV
OFF_C = OFF_B + D_CONV
OFF_ZC = OFF_C + D_CONV
D_IN = OFF_ZC + D_CONV
assert OFF_K == 2 * PROJ_TN and OFF_Z == 3 * PROJ_TN and OFF_QI == 5 * PROJ_TN and OFF_KI == 7 * PROJ_TN
KIW_ROWS = 128
assert OFF_KI % KIW_ROWS == 0 and OFF_WI - OFF_KI == IDX_DIM
CONV_CHUNK = 256
N_CONV_CHUNKS = D_CONV // CONV_CHUNK
J_Q = 0
J_KV = 2
J_Z = 3
J_QI = 5
J_CONV = 7
N_MAIN_BLOCKS = J_CONV
N_STEPS = J_CONV + N_CONV_CHUNKS

ATT_TQ = 256
ATT_TK = 256
CNT_ACCS = 4

SIDE_PAGES = 16


def _cparams(n_axes):
    return pltpu.CompilerParams(
        dimension_semantics=("arbitrary",) * n_axes,
        vmem_limit_bytes=VMEM_LIMIT_BYTES,
    )


def _silu(x):
    return x * jax.nn.sigmoid(x)


def _tile_rows(x8, period):
    row = lax.broadcasted_iota(I32, x8.shape, 0)
    out = x8
    for q in range(1, 8 // period):
        out = jnp.where(row // period == q, pltpu.roll(x8, q * period, axis=0), out)
    return out


def _indexer_rows_scores(qi, wcol, keys_t):
    n_tok = qi.shape[0] // N_IDX_HEADS
    d = jnp.dot(qi, keys_t, preferred_element_type=F32)
    val = jnp.maximum(d, 0.0) * wcol
    v = val.reshape(val.shape[0] // 8, 8, val.shape[-1]).sum(axis=0)
    out = v
    for q in range(1, 8 // n_tok):
        out = out + pltpu.roll(v, q * n_tok, axis=0)
    return out


class _SideScores:
    def __init__(self, step, n_total, spq, pt_ref, qi_ref, wcol_ref, kinew_ref, kidx_hbm, sp_ref, sn_ref,
                 kbuf, ksem):
        self.step, self.n_total, self.spq, self.pt_ref = step, n_total, spq, pt_ref
        self.qi_ref, self.wcol_ref, self.kinew_ref, self.kidx_hbm = qi_ref, wcol_ref, kinew_ref, kidx_hbm
        self.sp_ref, self.sn_ref, self.kbuf, self.ksem = sp_ref, sn_ref, kbuf, ksem
        self.npg = kbuf.shape[1]
        self.n_side = pt_ref.shape[0] * spq

    def _start(self, n):
        c = jnp.minimum(n, self.n_side - 1)
        seq_n = c // self.spq
        base = (c % self.spq) * self.npg
        for r in range(self.npg):
            _page_copy(self.kidx_hbm, self.pt_ref[seq_n, base + r], self.kbuf, n % 2, r, self.ksem).start()

    def prologue(self):
        pl.when(self.step == 0)(lambda: self._start(self.step))

    def __call__(self):
        pl.when(self.step + 1 < self.n_total)(lambda: self._start(self.step + 1))
        slot = self.step % 2
        for r in range(self.npg):
            _page_copy(self.kidx_hbm, 0, self.kbuf, slot, r, self.ksem).wait()
        qi = self.qi_ref[0]
        wcol = self.wcol_ref[0]
        keys_t = jnp.concatenate([self.kbuf[slot, r] for r in range(self.npg)], axis=1).astype(BF16)
        self.sp_ref[0] = _indexer_rows_scores(qi, wcol, keys_t)
        self.sn_ref[0] = _indexer_rows_scores(qi, wcol, self.kinew_ref[0])


class _Bf16View:
    def __init__(self, ref):
        self.ref = ref

    def __getitem__(self, idx):
        return self.ref[idx].astype(BF16)


def _proj_body(sample, tps, tm, side_spq, convert, *refs):
    if side_spq:
        pt_ref, refs = refs[0], refs[1:]
    (x_ref, gin_ref, wa_ref, wkiw_ref, wh_ref, wb_ref, wc_ref, wzc_ref, gq_ref, gk_ref, gki_ref,
     c16_ref, s16_ref, c8_ref, s8_ref,
     kc_ref, ksa_ref, ksb_ref, wconv_ref) = refs[:19]
    refs = refs[19:]
    if sample:
        e1_ref, e2_ref = refs[:2]
        refs = refs[2:]
    if side_spq:
        side_in, refs = refs[:4], refs[4:]
    (qT_ref, qiT_ref, vT_ref, wT_ref, k32_ref, kbf_ref, v32_ref, ki32_ref, kibf_ref,
     gate_ref, mixc_ref, u_ref) = refs[:12]
    refs = refs[12:]
    if convert:
        w_out, refs = refs[:6], refs[6:]
    if side_spq:
        side_out, refs = refs[:2], refs[2:]
    xn_ref = refs[0]
    if not sample:
        carry_ref = refs[1]

    i = pl.program_id(0)
    j = pl.program_id(1)
    nt = (((1,), (1,)), ((), ()))
    if side_spq:
        side = _SideScores(i * N_STEPS + j, pl.num_programs(0) * N_STEPS, side_spq, pt_ref,
                           *side_in, *side_out, *refs[2:4])
        side.prologue()
    else:
        side = lambda: None

    if convert:
        wa_o, kiw_o, wh_o, wb_o, wc_o, wzc_o = w_out

        @pl.when(j < N_MAIN_BLOCKS)
        def _():
            wa_o[...] = wa_ref[...].astype(BF16)

        @pl.when(j == J_KV)
        def _():
            kiw_o[...] = wkiw_ref[...].astype(BF16)

        @pl.when(j >= J_CONV)
        def _():
            for src, dst in ((wh_ref, wh_o), (wb_ref, wb_o), (wc_ref, wc_o), (wzc_ref, wzc_o)):
                dst[...] = src[...].astype(BF16)

        wa_ref, wkiw_ref, wh_ref, wb_ref, wc_ref, wzc_ref = (
            _Bf16View(r) for r in (wa_ref, wkiw_ref, wh_ref, wb_ref, wc_ref, wzc_ref))

    def xw(w, x=None):
        return lax.dot_general(xn_ref[...] if x is None else x, w, nt, preferred_element_type=F32)

    def wx(w, x):
        return lax.dot_general(w, x, nt, preferred_element_type=F32)

    nsub = max(tm // PROJ_SUB, 1)
    sub = tm // nsub
    pieces = [slice(rb * sub, (rb + 1) * sub) for rb in range(nsub)]

    if not sample:
        @pl.when((i == 0) & (j == 0))
        def _init():
            carry_ref[...] = jnp.zeros(carry_ref.shape, F32)

    def q_step(with_norm):
        side()
        for cs in pieces:
            if with_norm:
                x = x_ref[cs, :]
                ms = jnp.mean(x * x, axis=-1, keepdims=True)
                xn_ref[cs, :] = (x * lax.rsqrt(ms + EPS) * gin_ref[...]).astype(BF16)
            res = wx(wa_ref[...], xn_ref[cs, :])
            cos = c16_ref[:, cs]
            sin = s16_ref[:, cs]
            for hh in range(PROJ_TN // HEAD_DIM):
                blk = res[hh * HEAD_DIM:(hh + 1) * HEAD_DIM]
                ms = jnp.mean(blk * blk, axis=0, keepdims=True)
                y = blk * lax.rsqrt(ms + EPS) * gq_ref[...]
                x1 = y[0:ROPE_HALF]
                x2 = y[ROPE_HALF:ROPE_DIM]
                base = hh * HEAD_DIM
                qT_ref[base:base + ROPE_HALF, cs] = ((x1 * cos - x2 * sin) * Q_PRESCALE).astype(BF16)
                qT_ref[base + ROPE_HALF:base + ROPE_DIM, cs] = ((x2 * cos + x1 * sin) * Q_PRESCALE).astype(BF16)
                qT_ref[base + ROPE_DIM:base + HEAD_DIM, cs] = (y[ROPE_DIM:] * Q_PRESCALE).astype(BF16)

    pl.when(j == J_Q)(functools.partial(q_step, True))
    pl.when((j > J_Q) & (j < J_KV))(functools.partial(q_step, False))

    @pl.when((j >= J_QI) & (j < J_CONV))
    def _qi():
        side()
        for cs in pieces:
            res = wx(wa_ref[...], xn_ref[cs, :])
            cos = c8_ref[:, cs]
            sin = s8_ref[:, cs]
            for hh in range(PROJ_TN // IDX_DIM):
                blk = res[hh * IDX_DIM:(hh + 1) * IDX_DIM]
                x1 = blk[0:IDX_ROPE_HALF]
                x2 = blk[IDX_ROPE_HALF:2 * IDX_ROPE_HALF]
                rot = jnp.concatenate([x1 * cos - x2 * sin, x2 * cos + x1 * sin], axis=0)
                base = hh * IDX_DIM
                qiT_ref[base:base + 2 * IDX_ROPE_HALF, cs] = rot.astype(BF16)
                qiT_ref[base + 2 * IDX_ROPE_HALF:base + IDX_DIM, cs] = blk[2 * IDX_ROPE_HALF:].astype(BF16)

    @pl.when(j == J_KV)
    def _kv():
        side()
        for rb, cs in enumerate(pieces):
            x = xn_ref[cs, :]
            res = xw(wa_ref[...], x)
            for hd in range(N_KV_HEADS):
                hs = slice(hd * HEAD_DIM, (hd + 1) * HEAD_DIM)
                blk = res[:, hs]
                ms = jnp.mean(blk * blk, axis=-1, keepdims=True)
                yk = blk * lax.rsqrt(ms + EPS) * gk_ref[...]
                rot = yk * kc_ref[cs, :] + (pltpu.roll(yk, HEAD_DIM - ROPE_HALF, axis=1) * ksa_ref[cs, :]
                                            + pltpu.roll(yk, ROPE_HALF, axis=1) * ksb_ref[cs, :])
                k32_ref[pl.ds(N_KV_HEADS * rb * sub + hd, sub, stride=N_KV_HEADS), :] = rot
                kbf_ref[cs, hs] = rot.astype(BF16)
            v = res[:, N_KV:2 * N_KV]
            for hd in range(N_KV_HEADS):
                v32_ref[pl.ds(N_KV_HEADS * rb * sub + hd, sub, stride=N_KV_HEADS), :] = (
                    v[:, hd * HEAD_DIM:(hd + 1) * HEAD_DIM])
            if sample:
                vT_ref[:, cs] = wx(wa_ref[N_KV:2 * N_KV, :], x).astype(BF16)
            else:
                vT_ref[:, cs] = v.T.astype(BF16)
            r2t = wx(wkiw_ref[...], x)
            wT_ref[:, cs] = r2t[IDX_DIM:IDX_DIM + N_IDX_HEADS] * W_IDX_SCALE
            kit = r2t[0:IDX_DIM]
            ms = jnp.mean(kit * kit, axis=0, keepdims=True)
            yi = kit * lax.rsqrt(ms + EPS) * gki_ref[...]
            x1 = yi[0:IDX_ROPE_HALF]
            x2 = yi[IDX_ROPE_HALF:2 * IDX_ROPE_HALF]
            cos = c8_ref[:, cs]
            sin = s8_ref[:, cs]
            roti = jnp.concatenate([x1 * cos - x2 * sin, x2 * cos + x1 * sin, yi[2 * IDX_ROPE_HALF:],
                                    jnp.zeros((KIW_ROWS - IDX_DIM, sub), F32)], axis=0)
            ki_nat = roti.T
            ki32_ref[cs, :] = ki_nat[:, 0:IDX_DIM]
            kibf_ref[cs, :] = ki_nat.astype(BF16)

    @pl.when((j >= J_Z) & (j < J_QI))
    def _z():
        side()
        for cs in pieces:
            gate_ref[cs, :] = _silu(xw(wa_ref[...], xn_ref[cs, :]))

    @pl.when(j >= J_CONV)
    def _conv():
        side()
        cc = j - J_CONV
        u = xw(wc_ref[...]) * xw(wh_ref[...])
        rowid = lax.broadcasted_iota(I32, (tm, CONV_CHUNK), 0)
        if sample:
            t = rowid & (sample - 1)
            u1 = jnp.where(t >= 1, pltpu.roll(u, 1, axis=0), e1_ref[...])
            u2 = jnp.where(t >= 2, pltpu.roll(u, 2, axis=0), e2_ref[...])
            u_ref[...] = u
        else:
            first = (i % tps) == 0
            prev = carry_ref[cc]
            p0 = jnp.where(first, 0.0, prev[0:1])
            p1 = jnp.where(first, 0.0, prev[1:2])
            u1 = jnp.where(rowid == 0, p1, pltpu.roll(u, 1, axis=0))
            u2 = jnp.where(rowid == 0, p0, jnp.where(rowid == 1, p1, pltpu.roll(u, 2, axis=0)))
            tail = u[tm - 8:tm]
            carry_ref[cc] = jnp.concatenate([tail[6:8], tail[0:6]], axis=0)
            u_ref[0] = tail[6:8]
        w = wconv_ref[...]
        y = u2 * w[0:1] + u1 * w[1:2] + u * w[2:3]
        mixc_ref[...] = (xw(wb_ref[...]) * y * _silu(xw(wzc_ref[...]))).astype(BF16)


def _project(x, norm_g, wts, params, tabs, sample, state_rows=None, side=None):
    m = x.shape[0]
    convert = not isinstance(wts, tuple)
    assert sample & (sample - 1) == 0
    tm = m if sample else PROJ_TM
    n_i = m // tm
    tps = 1 if sample else (tabs["c16T"].shape[1] // tm)
    gq, gk, gki, wconv = params

    def tmap(i):
        return i % tps

    def cchunk(j):
        return jnp.clip(j - J_CONV, 0, N_CONV_CHUNKS - 1)

    main_spec = pl.BlockSpec((PROJ_TN, D_MODEL), lambda i, j, *_: (jnp.minimum(j, N_MAIN_BLOCKS - 1), 0))
    if convert:
        def conv_rows(off):
            return pl.BlockSpec((pl.Element(CONV_CHUNK), pl.Element(D_MODEL)),
                                lambda i, j, *_: (pl.multiple_of(off + CONV_CHUNK * cchunk(j), 16), 0))

        w_specs = [main_spec, pl.BlockSpec((KIW_ROWS, D_MODEL), lambda i, j, *_: (OFF_KI // KIW_ROWS, 0)),
                   conv_rows(OFF_H), conv_rows(OFF_B), conv_rows(OFF_C), conv_rows(OFF_ZC)]
        w_args = [wts] * 6
    else:
        conv_spec = pl.BlockSpec(
            (CONV_CHUNK, D_MODEL),
            lambda i, j, *_: (jnp.where(j < J_QI, N_CONV_CHUNKS - 1, cchunk(j)), 0))
        w_specs = [main_spec, pl.BlockSpec((KIW_ROWS, D_MODEL), lambda i, j, *_: (0, 0))] + [conv_spec] * 4
        w_args = list(wts)

    in_specs = [
        pl.BlockSpec((tm, D_MODEL), lambda i, j, *_: (i, 0)),
        pl.BlockSpec((1, D_MODEL), lambda i, j, *_: (0, 0)),
        *w_specs,
        pl.BlockSpec((HEAD_DIM, 1), lambda i, j, *_: (0, 0)),
        pl.BlockSpec((1, HEAD_DIM), lambda i, j, *_: (0, 0)),
        pl.BlockSpec((IDX_DIM, 1), lambda i, j, *_: (0, 0)),
        pl.BlockSpec((ROPE_HALF, tm), lambda i, j, *_: (0, tmap(i))),
        pl.BlockSpec((ROPE_HALF, tm), lambda i, j, *_: (0, tmap(i))),
        pl.BlockSpec((IDX_ROPE_HALF, tm), lambda i, j, *_: (0, tmap(i))),
        pl.BlockSpec((IDX_ROPE_HALF, tm), lambda i, j, *_: (0, tmap(i))),
        pl.BlockSpec((tm, HEAD_DIM), lambda i, j, *_: (tmap(i), 0)),
        pl.BlockSpec((tm, HEAD_DIM), lambda i, j, *_: (tmap(i), 0)),
        pl.BlockSpec((tm, HEAD_DIM), lambda i, j, *_: (tmap(i), 0)),
        pl.BlockSpec((CONV_W, CONV_CHUNK), lambda i, j, *_: (0, cchunk(j))),
    ]
    args = [x, norm_g.reshape(1, D_MODEL), *w_args, gq, gk, gki,
            tabs["c16T"], tabs["s16T"], tabs["c8T"], tabs["s8T"],
            tabs["kC"], tabs["kSa"], tabs["kSb"], wconv]
    cmap = lambda i, j, *_: (i, cchunk(j))
    if sample:
        in_specs += [pl.BlockSpec((tm, CONV_CHUNK), cmap), pl.BlockSpec((tm, CONV_CHUNK), cmap)]
        args += list(state_rows)
        u_spec = pl.BlockSpec((tm, CONV_CHUNK), cmap)
        u_shape = jax.ShapeDtypeStruct((m, D_CONV), F32)
        scratch = [pltpu.VMEM((tm, D_MODEL), BF16)]
    else:
        u_spec = pl.BlockSpec((1, CONV_W - 1, CONV_CHUNK), lambda i, j, *_: (i, 0, cchunk(j)))
        u_shape = jax.ShapeDtypeStruct((n_i, CONV_W - 1, D_CONV), F32)
        scratch = [pltpu.VMEM((tm, D_MODEL), BF16), pltpu.VMEM((N_CONV_CHUNKS, 8, CONV_CHUNK), F32)]

    out_specs = [
        pl.BlockSpec((PROJ_TN, tm), lambda i, j, *_: (jnp.minimum(j, 1), i)),
        pl.BlockSpec((PROJ_TN, tm), lambda i, j, *_: (jnp.clip(j - J_QI, 0, 1), i)),
        pl.BlockSpec((N_KV, tm), lambda i, j, *_: (0, i)),
        pl.BlockSpec((N_IDX_HEADS, tm), lambda i, j, *_: (0, i)),
        pl.BlockSpec((N_KV_HEADS * tm, HEAD_DIM), lambda i, j, *_: (i, 0)),
        pl.BlockSpec((tm, N_KV), lambda i, j, *_: (i, 0)),
        pl.BlockSpec((N_KV_HEADS * tm, HEAD_DIM), lambda i, j, *_: (i, 0)),
        pl.BlockSpec((tm, IDX_DIM), lambda i, j, *_: (i, 0)),
        pl.BlockSpec((tm, KIW_ROWS), lambda i, j, *_: (i, 0)),
        pl.BlockSpec((tm, PROJ_TN), lambda i, j, *_: (i, jnp.clip(j - J_Z, 0, 1))),
        pl.BlockSpec((tm, CONV_CHUNK), cmap),
        u_spec,
    ]
    out_shape = [
        jax.ShapeDtypeStruct((D_ATTN, m), BF16),
        jax.ShapeDtypeStruct((N_IDX_HEADS * IDX_DIM, m), BF16),
        jax.ShapeDtypeStruct((N_KV, m), BF16),
        jax.ShapeDtypeStruct((N_IDX_HEADS, m), F32),
        jax.ShapeDtypeStruct((N_KV_HEADS * m, HEAD_DIM), F32),
        jax.ShapeDtypeStruct((m, N_KV), BF16),
        jax.ShapeDtypeStruct((N_KV_HEADS * m, HEAD_DIM), F32),
        jax.ShapeDtypeStruct((m, IDX_DIM), F32),
        jax.ShapeDtypeStruct((m, KIW_ROWS), BF16),
        jax.ShapeDtypeStruct((m, D_ATTN), F32),
        jax.ShapeDtypeStruct((m, D_CONV), BF16),
        u_shape,
    ]
    if convert:
        assert n_i == 1, "every weight block must be visited exactly once to be written back"
        out_specs += [
            main_spec,
            pl.BlockSpec((KIW_ROWS, D_MODEL), lambda i, j, *_: (0, 0)),
        ] + [pl.BlockSpec((CONV_CHUNK, D_MODEL), lambda i, j, *_: (cchunk(j), 0))] * 4
        out_shape += [jax.ShapeDtypeStruct((OFF_KI, D_MODEL), BF16),
                      jax.ShapeDtypeStruct((KIW_ROWS, D_MODEL), BF16)]
        out_shape += [jax.ShapeDtypeStruct((D_CONV, D_MODEL), BF16)] * 4
    if side is None:
        return pl.pallas_call(
            functools.partial(_proj_body, sample, tps, tm, 0, convert),
            grid=(n_i, N_STEPS),
            in_specs=in_specs,
            out_specs=out_specs,
            out_shape=out_shape,
            scratch_shapes=scratch,
            compiler_params=_cparams(2),
            name="proj_sample" if sample else "proj_prompt",
        )(*args)

    page_table, qi_rows, wcol, ki_new_t, kidx_t = side
    n_seq, n_pages = page_table.shape
    spq = n_pages // SIDE_PAGES
    n_side = n_seq * spq
    assert spq * SIDE_PAGES == n_pages and n_side <= n_i * N_STEPS
    width = SIDE_PAGES * PAGE_SIZE

    def side_step(i, j):
        return jnp.minimum(i * N_STEPS + j, n_side - 1)

    seq_map = lambda i, j, *_: (side_step(i, j) // spq, 0, 0)
    in_specs += [
        pl.BlockSpec((1,) + qi_rows.shape[1:], seq_map),
        pl.BlockSpec((1,) + wcol.shape[1:], seq_map),
        pl.BlockSpec((1, IDX_DIM, PAGE_SIZE), seq_map),
        pl.BlockSpec(memory_space=pl.ANY),
    ]
    out_specs += [
        pl.BlockSpec((1, 8, width), lambda i, j, *_: (side_step(i, j) // spq, 0, side_step(i, j) % spq)),
        pl.BlockSpec((1, 8, PAGE_SIZE), seq_map),
    ]
    out_shape += [jax.ShapeDtypeStruct((n_seq, 8, n_pages * PAGE_SIZE), F32),
                  jax.ShapeDtypeStruct((n_seq, 8, PAGE_SIZE), F32)]
    scratch += [pltpu.VMEM((2, SIDE_PAGES, IDX_DIM, PAGE_SIZE), F32), pltpu.SemaphoreType.DMA((2,))]
    return pl.pallas_call(
        functools.partial(_proj_body, sample, tps, tm, spq, convert),
        grid_spec=pltpu.PrefetchScalarGridSpec(
            num_scalar_prefetch=1,
            grid=(n_i, N_STEPS),
            in_specs=in_specs,
            out_specs=out_specs,
            scratch_shapes=scratch,
        ),
        out_shape=out_shape,
        compiler_params=_cparams(2),
        name="proj_prompt",
    )(page_table, *args, qi_rows, wcol, ki_new_t, kidx_t)


KEY_LOWEST_FINITE = INT_MIN + 0x00800000


def _key_to_float(key):
    return pltpu.bitcast(key ^ ((key >> 31) & 0x7FFFFFFF), F32)


def _bit_value(b):
    return lax.shift_left(jnp.int32(1), jnp.int32(31) - b)


def _prompt_attn_step(i, qiT_ref, wT_ref, ki_ref, qT_ref, k_ref, vT_ref, gate_ref, o_ref,
                      sc_ref, bias_ref, acc_ref, s_ref):
    tq, tk = ATT_TQ, ATT_TK
    nch = i + 1
    w = wT_ref[...]
    row = lax.broadcasted_iota(I32, (tk, tq), 0)
    col = lax.broadcasted_iota(I32, (tk, tq), 1)
    row8 = lax.broadcasted_iota(I32, (8, tq), 0)

    def score_chunk(j, carry):
        off = pl.multiple_of(j * tk, tk)
        kic = ki_ref[pl.ds(off, tk), 0:IDX_DIM]
        acc = jnp.zeros((tk, tq), F32)
        for h in range(N_IDX_HEADS):
            d = jnp.dot(kic, qiT_ref[h * IDX_DIM:(h + 1) * IDX_DIM, :], preferred_element_type=F32)
            acc = acc + w[h:h + 1, :] * jnp.maximum(d, 0.0)
        future = (row + j * tk) > (col + i * tq)
        sc_ref[pl.ds(off, tk), :] = jnp.where(future, -jnp.inf, acc)
        return carry

    lax.fori_loop(0, nch, score_chunk, 0)

    def count_keys(pred):
        def cnt_chunk(j, cs):
            off = pl.multiple_of(j * tk, tk)
            cs = list(cs)
            sc = sc_ref[pl.ds(off, tk), :]
            for r in range(tk // 8):
                a = cs[r % CNT_ACCS]
                kpos = row8 + (j * tk + r * 8)
                cs[r % CNT_ACCS] = jnp.where(pred(sc[r * 8:(r + 1) * 8], kpos), a + 1, a)
            return tuple(cs)

        cs = lax.fori_loop(0, nch, cnt_chunk, (jnp.zeros((8, tq), I32),) * CNT_ACCS)
        c = cs[0]
        for a in cs[1:]:
            c = c + a
        return jnp.sum(c.astype(F32), axis=0, keepdims=True)

    def bit_body(b, carry):
        thr, cge = carry
        cand = thr + _bit_value(b)
        cand_f = _key_to_float(cand)
        cnt = count_keys(lambda s, kpos: s >= cand_f)
        ok = cnt >= float(TOPK_MAX)
        return jnp.where(ok, cand, thr), jnp.where(ok, cnt, cge)

    n_bits = jnp.where(nch * tk > TOPK_MAX, 32, 0)
    thr, cge = lax.fori_loop(0, n_bits, bit_body, (jnp.full((1, tq), INT_MIN, I32),
                                                   jnp.zeros((1, tq), F32)))
    thr_f = _key_to_float(jnp.maximum(thr, KEY_LOWEST_FINITE))
    has_ties = jnp.max(cge) > float(TOPK_MAX)

    @pl.when(jnp.logical_not(has_ties))
    def _plain_mask():
        def bias_chunk(j, carry):
            off = pl.multiple_of(j * tk, tk)
            bias_ref[pl.ds(off, tk), :] = jnp.where(sc_ref[pl.ds(off, tk), :] >= thr_f, 0.0, NEG)
            return carry

        lax.fori_loop(0, nch, bias_chunk, 0)

    @pl.when(has_ties)
    def _tie_mask():
        need = float(TOPK_MAX) - count_keys(lambda s, kpos: s > thr_f)
        nbits = (sc_ref.shape[0] - 1).bit_length()

        def pos_bit(b, last):
            step = lax.shift_left(jnp.int32(1), jnp.int32(nbits - 1) - b)
            probe = last + (step - 1)
            got = count_keys(lambda s, kpos: (s == thr_f) & (kpos <= probe))
            return jnp.where(got < need, last + step, last)

        last = lax.fori_loop(0, nbits, pos_bit, jnp.zeros((1, tq), I32))

        def bias_chunk(j, carry):
            off = pl.multiple_of(j * tk, tk)
            sc = sc_ref[pl.ds(off, tk), :]
            keep = (sc > thr_f) | ((sc == thr_f) & ((row + j * tk) <= last))
            bias_ref[pl.ds(off, tk), :] = jnp.where(keep, 0.0, NEG)
            return carry

        lax.fori_loop(0, nch, bias_chunk, 0)

    def qk_chunk(j, mrun):
        off = pl.multiple_of(j * tk, tk)
        bias = bias_ref[pl.ds(off, tk), :]
        out = []
        for h in range(N_HEADS):
            g = h // GROUP
            kc = k_ref[pl.ds(off, tk), g * HEAD_DIM:(g + 1) * HEAD_DIM]
            s = jnp.dot(kc, qT_ref[h * HEAD_DIM:(h + 1) * HEAD_DIM, :], preferred_element_type=F32) + bias
            s_ref[h, pl.ds(off, tk), :] = s
            out.append(jnp.maximum(mrun[h], s.reshape(tk // 8, 8, tq).max(axis=0)))
        return tuple(out)

    mrun = lax.fori_loop(0, nch, qk_chunk, (jnp.full((8, tq), NEG, F32),) * N_HEADS)
    ms = [jnp.max(mr, axis=0, keepdims=True) for mr in mrun]
    acc_ref[...] = jnp.zeros(acc_ref.shape, F32)

    def pv_chunk(j, lrun):
        off = pl.multiple_of(j * tk, tk)
        out = []
        for h in range(N_HEADS):
            g = h // GROUP
            hs = slice(h * HEAD_DIM, (h + 1) * HEAD_DIM)
            p = jnp.exp2(s_ref[h, pl.ds(off, tk), :] - ms[h])
            out.append(lrun[h] + p.reshape(tk // 8, 8, tq).sum(axis=0))
            vc = vT_ref[g * HEAD_DIM:(g + 1) * HEAD_DIM, pl.ds(off, tk)]
            acc_ref[hs, :] += jnp.dot(vc, p.astype(BF16), preferred_element_type=F32)
        return tuple(out)

    lrun = lax.fori_loop(0, nch, pv_chunk, (jnp.zeros((8, tq), F32),) * N_HEADS)

    for h in range(N_HEADS):
        hs = slice(h * HEAD_DIM, (h + 1) * HEAD_DIM)
        o = (acc_ref[hs, :] / jnp.sum(lrun[h], axis=0, keepdims=True)).T
        o_ref[:, hs] = (o * gate_ref[:, hs]).astype(BF16)


def _page_copy(cache_hbm, page, pbuf, slot, r, sem):
    return pltpu.make_async_copy(cache_hbm.at[page], pbuf.at[slot, r], sem.at[slot])


def _attn_fused_body(nq, spq, pt_ref, *refs):
    prompt_in = refs[0:7]
    q_ref, bp_ref, bn_ref, knew_ref, vnew_ref, ck_hbm, cv_hbm = refs[7:14]
    o_ref, so_ref = refs[14:16]
    prompt_scratch = refs[16:20]
    pbuf, sem, ss_ref, ps_ref, ls_ref, accs_ref = refs[20:26]

    i = pl.program_id(1)
    step = pl.program_id(0) * nq + i
    n_total = pl.num_programs(0) * nq
    half = spq // 2
    npg = pbuf.shape[1]

    slot = step % 2

    def start_pages(seq_n, t_n, to_slot):
        cache_hbm = ck_hbm if t_n < half else cv_hbm
        base = (t_n % half) * npg
        for r in range(npg):
            _page_copy(cache_hbm, pt_ref[seq_n, base + r], pbuf, to_slot, r, sem).start()

    def start_next(t_next):
        start_pages(step // spq + t_next // spq, t_next % spq, 1 - slot)

    @pl.when(step == 0)
    def _prologue():
        start_pages(0, 0, 0)

    for r in range(npg):
        _page_copy(ck_hbm, 0, pbuf, slot, r, sem).wait()

    _sample_attn_step(step % spq, half, slot, step + 1 < n_total, start_next,
                      q_ref, bp_ref, bn_ref, knew_ref, vnew_ref, pbuf,
                      so_ref, ss_ref, ps_ref, ls_ref, accs_ref)
    _prompt_attn_step(i, *prompt_in, o_ref, *prompt_scratch)


def _attention(qiT, wT, kibf, qT, kbf, vT, gate, n_batch, seq,
               q_rows, bp, bn, k_new, v_new, cache_k, cache_v, page_table):
    m = n_batch * seq
    nq = seq // ATT_TQ
    n_seq, n_pages = page_table.shape
    n_steps = n_batch * nq
    spq = n_steps // n_seq
    assert spq * n_seq == n_steps and spq % 2 == 0
    half = spq // 2
    npg = n_pages // half
    assert npg * half == n_pages
    width = npg * PAGE_SIZE
    past = n_pages * PAGE_SIZE
    n_pool = cache_k.shape[0]
    kv_rows = PAGE_SIZE * N_KV_HEADS
    ck = cache_k.reshape(n_pool, kv_rows, HEAD_DIM)
    cv = cache_v.reshape(n_pool, kv_rows, HEAD_DIM)
    nrow = q_rows.shape[1]
    assert (nrow // N_KV_HEADS) % 16 == 0

    qmap = lambda b, i, pt: (0, b * nq + i)
    smap = lambda b, i, pt: ((b * nq + i) // spq, 0, 0)
    return pl.pallas_call(
        functools.partial(_attn_fused_body, nq, spq),
        grid_spec=pltpu.PrefetchScalarGridSpec(
            num_scalar_prefetch=1,
            grid=(n_batch, nq),
            in_specs=[
                pl.BlockSpec((N_IDX_HEADS * IDX_DIM, ATT_TQ), qmap),
                pl.BlockSpec((N_IDX_HEADS, ATT_TQ), qmap),
                pl.BlockSpec((seq, 128), lambda b, i, pt: (b, 0)),
                pl.BlockSpec((D_ATTN, ATT_TQ), qmap),
                pl.BlockSpec((seq, N_KV), lambda b, i, pt: (b, 0)),
                pl.BlockSpec((N_KV, seq), lambda b, i, pt: (0, b)),
                pl.BlockSpec((ATT_TQ, D_ATTN), lambda b, i, pt: (b * nq + i, 0)),
                pl.BlockSpec((1, nrow, HEAD_DIM), smap),
                pl.BlockSpec((1, 8, width),
                             lambda b, i, pt: ((b * nq + i) // spq, 0, jnp.minimum((b * nq + i) % spq, half - 1))),
                pl.BlockSpec((1, 8, PAGE_SIZE), smap),
                pl.BlockSpec((1, PAGE_SIZE, N_KV), smap),
                pl.BlockSpec((1, PAGE_SIZE, N_KV), smap),
                pl.BlockSpec(memory_space=pl.ANY),
                pl.BlockSpec(memory_space=pl.ANY),
            ],
            out_specs=[
                pl.BlockSpec((ATT_TQ, D_ATTN), lambda b, i, pt: (b * nq + i, 0)),
                pl.BlockSpec((1, nrow, HEAD_DIM), smap),
            ],
            scratch_shapes=[
                pltpu.VMEM((seq, ATT_TQ), F32), pltpu.VMEM((seq, ATT_TQ), F32),
                pltpu.VMEM((D_ATTN, ATT_TQ), F32), pltpu.VMEM((N_HEADS, seq, ATT_TQ), F32),
                pltpu.VMEM((2, npg, kv_rows, HEAD_DIM), F32), pltpu.SemaphoreType.DMA((2,)),
                pltpu.VMEM((nrow, past + PAGE_SIZE), F32), pltpu.VMEM((nrow, past + PAGE_SIZE), BF16),
                pltpu.VMEM((nrow, 128), F32), pltpu.VMEM((nrow, HEAD_DIM), F32),
            ],
        ),
        out_shape=[jax.ShapeDtypeStruct((m, D_ATTN), BF16),
                   jax.ShapeDtypeStruct((n_seq, nrow, HEAD_DIM), F32)],
        compiler_params=_cparams(2),
        name="attention",
    )(page_table, qiT, wT, kibf, qT, kbf, vT, gate, q_rows, bp, bn, k_new, v_new, ck, cv)


def _s_thresh_body(n_new, in_p_ref, in_n_ref, out_p_ref, out_n_ref, sp_ref):
    past = in_p_ref.shape[1]
    rows = sp_ref.shape[0]
    pack = 8 // n_new
    groups = rows // 8
    ch = 2048
    nchunk = past // ch

    def pack_rows(src_ref, cols, ncols):
        which = lax.broadcasted_iota(I32, (8, ncols), 0) // n_new
        out = []
        for p in range(groups):
            dense = src_ref[(p * pack) * 8:(p * pack) * 8 + 8, cols]
            for q in range(1, pack):
                piece = src_ref[(p * pack + q) * 8:(p * pack + q) * 8 + 8, cols]
                dense = jnp.where(which == q, pltpu.roll(piece, q * n_new, axis=0), dense)
            out.append(dense)
        return jnp.concatenate(out, axis=0)

    def unpack_rows(dst_ref, cols, dense):
        for p in range(groups):
            piece = dense[p * 8:(p + 1) * 8]
            for q in range(pack):
                rolled = piece if q == 0 else pltpu.roll(piece, 8 - q * n_new, axis=0)
                dst_ref[(p * pack + q) * 8:(p * pack + q) * 8 + 8, cols] = rolled

    def pack_chunk(c, carry):
        cols = pl.ds(pl.multiple_of(c * ch, ch), ch)
        sp_ref[:, cols] = pack_rows(in_p_ref, cols, ch)
        return carry

    lax.fori_loop(0, nchunk, pack_chunk, 0)
    t = lax.broadcasted_iota(I32, (rows, 128), 0) % n_new
    lane = lax.broadcasted_iota(I32, (rows, 128), 1)
    sn = jnp.where((lane < n_new) & (lane <= t), pack_rows(in_n_ref, slice(0, 128), 128), -jnp.inf)

    def fold(x):
        f = x[:, 0:128]
        for q in range(1, x.shape[1] // 128):
            f = f + x[:, q * 128:(q + 1) * 128]
        return f

    lane_ch = lax.broadcasted_iota(I32, (rows, ch), 1)

    def count_keys(pred):
        def cnt_chunk(c, acc):
            off = pl.multiple_of(c * ch, ch)
            return acc + fold(jnp.where(pred(sp_ref[:, pl.ds(off, ch)], lane_ch + c * ch), 1.0, 0.0))

        acc = lax.fori_loop(0, nchunk, cnt_chunk, jnp.where(pred(sn, lane + past), 1.0, 0.0))
        return jnp.sum(acc, axis=1, keepdims=True)

    def bit_body(b, carry):
        thr, cge = carry
        cand = thr + _bit_value(b)
        cand_f = _key_to_float(cand)
        cnt = count_keys(lambda s, kpos: s >= cand_f)
        ok = cnt >= float(TOPK_MAX)
        return jnp.where(ok, cand, thr), jnp.where(ok, cnt, cge)

    thr, cge = lax.fori_loop(0, 32, bit_body, (jnp.full((rows, 1), INT_MIN, I32),
                                               jnp.zeros((rows, 1), F32)))
    thr_f = _key_to_float(jnp.maximum(thr, KEY_LOWEST_FINITE))
    has_ties = jnp.max(cge) > float(TOPK_MAX)

    @pl.when(jnp.logical_not(has_ties))
    def _plain_mask():
        def to_bias(c, carry):
            cols = pl.ds(pl.multiple_of(c * ch, ch), ch)
            unpack_rows(out_p_ref, cols, jnp.where(sp_ref[:, cols] >= thr_f, 0.0, NEG))
            return carry

        lax.fori_loop(0, nchunk, to_bias, 0)
        unpack_rows(out_n_ref, slice(0, 128), jnp.where(sn >= thr_f, 0.0, NEG))

    @pl.when(has_ties)
    def _tie_mask():
        need = float(TOPK_MAX) - count_keys(lambda s, kpos: s > thr_f)
        nbits = (past + 128 - 1).bit_length()

        def pos_bit(b, last):
            step = lax.shift_left(jnp.int32(1), jnp.int32(nbits - 1) - b)
            probe = last + (step - 1)
            got = count_keys(lambda s, kpos: (s == thr_f) & (kpos <= probe))
            return jnp.where(got < need, last + step, last)

        last = lax.fori_loop(0, nbits, pos_bit, jnp.zeros((rows, 1), I32))

        def keep(s, kpos):
            return (s > thr_f) | ((s == thr_f) & (kpos <= last))

        def to_bias(c, carry):
            cols = pl.ds(pl.multiple_of(c * ch, ch), ch)
            unpack_rows(out_p_ref, cols, jnp.where(keep(sp_ref[:, cols], lane_ch + c * ch), 0.0, NEG))
            return carry

        lax.fori_loop(0, nchunk, to_bias, 0)
        unpack_rows(out_n_ref, slice(0, 128), jnp.where(keep(sn, lane + past), 0.0, NEG))


def _sample_attn_step(t, n_steps, slot, has_next, start_next, q_ref, bp_ref, bn_ref, knew_ref, vnew_ref, pbuf,
                      out_ref, s_ref, p_ref, l_ref, acc_ref):
    npg = pbuf.shape[1]
    nt = (((1,), (1,)), ((), ()))
    rg = q_ref.shape[1] // N_KV_HEADS
    n_tok = rg // GROUP
    width = npg * PAGE_SIZE
    past = n_steps * width
    sm_chunk = 2048

    def head_rows(g):
        rows = [pbuf[slot, r, pl.ds(g, PAGE_SIZE, stride=N_KV_HEADS), :] for r in range(npg)]
        return jnp.concatenate(rows, axis=0).astype(BF16)

    def per_head(mask8):
        return jnp.concatenate([_tile_rows(mask8, n_tok)] * (rg // 8), axis=0)

    def logits(ts):
        off = ts * width
        bias = per_head(bp_ref[0])
        for g in range(N_KV_HEADS):
            qg = q_ref[0, g * rg:(g + 1) * rg, :]
            s_ref[g * rg:(g + 1) * rg, off:off + width] = lax.dot_general(
                qg, head_rows(g), nt, preferred_element_type=F32) + bias

    def softmax():
        biasn = per_head(bn_ref[0])
        for g in range(N_KV_HEADS):
            qg = q_ref[0, g * rg:(g + 1) * rg, :]
            kg = knew_ref[0, :, g * HEAD_DIM:(g + 1) * HEAD_DIM]
            s_ref[g * rg:(g + 1) * rg, past:past + PAGE_SIZE] = lax.dot_general(
                qg, kg, nt, preferred_element_type=F32) + biasn

        def fold(x, op):
            f = x[:, 0:128]
            for q in range(1, x.shape[1] // 128):
                f = op(f, x[:, q * 128:(q + 1) * 128])
            return f

        def max_chunk(c, m):
            off = pl.multiple_of(c * sm_chunk, sm_chunk)
            return jnp.maximum(m, fold(s_ref[:, pl.ds(off, sm_chunk)], jnp.maximum))

        m = lax.fori_loop(0, past // sm_chunk, max_chunk, s_ref[:, past:past + PAGE_SIZE])
        m = jnp.max(m, axis=1, keepdims=True)

        def exp_chunk(c, l):
            off = pl.multiple_of(c * sm_chunk, sm_chunk)
            p = jnp.exp2(s_ref[:, pl.ds(off, sm_chunk)] - m)
            p_ref[:, pl.ds(off, sm_chunk)] = p.astype(BF16)
            return l + fold(p, jnp.add)

        pn = jnp.exp2(s_ref[:, past:past + PAGE_SIZE] - m)
        p_ref[:, past:past + PAGE_SIZE] = pn.astype(BF16)
        l_ref[...] = lax.fori_loop(0, past // sm_chunk, exp_chunk, pn)
        acc_ref[...] = jnp.zeros(acc_ref.shape, F32)

    def values(ts):
        off = (ts - n_steps) * width
        for g in range(N_KV_HEADS):
            rs = slice(g * rg, (g + 1) * rg)
            acc_ref[rs, :] += jnp.dot(p_ref[rs, off:off + width], head_rows(g),
                                      preferred_element_type=F32)

    def finish():
        for g in range(N_KV_HEADS):
            rs = slice(g * rg, (g + 1) * rg)
            vg = vnew_ref[0, :, g * HEAD_DIM:(g + 1) * HEAD_DIM]
            acc_ref[rs, :] += jnp.dot(p_ref[rs, past:past + PAGE_SIZE], vg, preferred_element_type=F32)
        out_ref[0] = acc_ref[...] / jnp.sum(l_ref[...], axis=1, keepdims=True)

    last = 2 * n_steps - 1

    def phase(ts, start):
        if start:
            start_next(ts + 1)
        if ts < n_steps:
            logits(ts)
            if ts == n_steps - 1:
                softmax()
        else:
            values(ts)
            if ts == last:
                finish()

    for ts in range(last):
        pl.when(t == ts)(functools.partial(phase, ts, True))
    pl.when((t == last) & has_next)(functools.partial(phase, last, True))
    pl.when((t == last) & jnp.logical_not(has_next))(functools.partial(phase, last, False))


def _sample_operands(qT, qiT, wT, kibf, kbf, v32, cache_kidx, n_seq, n_tok):
    def rows_ht(xT, n_heads, dim):
        x = xT.reshape(n_heads, dim, n_seq, n_tok).transpose(2, 0, 3, 1)
        return x.reshape(n_seq, n_heads * n_tok, dim)

    qi_rows = rows_ht(qiT, N_IDX_HEADS, IDX_DIM)
    q_rows = rows_ht(qT, N_HEADS, HEAD_DIM)
    wcol = wT.reshape(N_IDX_HEADS, n_seq, n_tok).transpose(1, 0, 2).reshape(n_seq, N_IDX_HEADS * n_tok, 1)

    def pad_keys(x):
        x = x.reshape(n_seq, n_tok, x.shape[-1])
        return jnp.pad(x, ((0, 0), (0, PAGE_SIZE - n_tok), (0, 0)))

    ki_new_t = jnp.swapaxes(pad_keys(kibf[:, 0:IDX_DIM]), 1, 2)
    kidx_t = jnp.swapaxes(cache_kidx, 1, 2)
    k_new = pad_keys(kbf)
    v_new = pad_keys(v32.reshape(n_seq * n_tok, N_KV).astype(BF16))
    return (qi_rows, wcol, ki_new_t, kidx_t), (q_rows, k_new, v_new)


def _sample_topk(sp, sn, n_tok):
    n_seq, _, past = sp.shape
    rows = n_seq * 8
    assert 8 % n_tok == 0 and n_seq % (8 // n_tok) == 0
    bp, bn = pl.pallas_call(
        functools.partial(_s_thresh_body, n_tok),
        grid=(1,),
        in_specs=[pl.BlockSpec((rows, past), lambda i: (0, 0)),
                  pl.BlockSpec((rows, PAGE_SIZE), lambda i: (0, 0))],
        out_specs=[pl.BlockSpec((rows, past), lambda i: (0, 0)),
                   pl.BlockSpec((rows, PAGE_SIZE), lambda i: (0, 0))],
        out_shape=[jax.ShapeDtypeStruct((rows, past), F32),
                   jax.ShapeDtypeStruct((rows, PAGE_SIZE), F32)],
        scratch_shapes=[pltpu.VMEM((n_seq * n_tok, past), F32)],
        compiler_params=_cparams(1),
        name="sample_topk_mask",
    )(sp.reshape(rows, past), sn.reshape(rows, PAGE_SIZE))
    return bp.reshape(n_seq, 8, past), bn.reshape(n_seq, 8, PAGE_SIZE)


def _sample_rows_to_tokens(out, n_seq, n_tok):
    out = out.reshape(n_seq, N_HEADS, n_tok, HEAD_DIM)
    return out.transpose(0, 2, 1, 3).reshape(n_seq * n_tok, D_ATTN)


def _outproj_body(gated, *refs):
    if gated:
        x_ref, a_ref, gate_ref, mc_ref, wo_ref, o_ref, wo_o = refs
        ma = (a_ref[...] * gate_ref[...]).astype(BF16)
        wo_o[...] = wo_ref[...].astype(BF16)
        wo_ref = wo_o
    else:
        x_ref, a_ref, mc_ref, wo_ref, o_ref = refs
        ma = a_ref[...]
    acc = jnp.dot(ma, wo_ref[0:D_ATTN, :], preferred_element_type=F32)
    acc = acc + jnp.dot(mc_ref[...], wo_ref[D_ATTN:D_ATTN + D_CONV, :], preferred_element_type=F32)
    o_ref[...] = x_ref[...] + acc


def _outproj(x, attn, gate, mixc, wo, tm):
    m = x.shape[0]
    gated = gate is not None
    assert not gated or m == tm
    row = lambda i: (i, 0)
    in_specs = [pl.BlockSpec((tm, D_MODEL), row), pl.BlockSpec((tm, D_ATTN), row)]
    args = [x, attn]
    if gated:
        in_specs.append(pl.BlockSpec((tm, D_ATTN), row))
        args.append(gate)
    in_specs += [pl.BlockSpec((tm, D_CONV), row),
                 pl.BlockSpec((D_ATTN + D_CONV, D_MODEL), lambda i: (0, 0))]
    args += [mixc, wo]
    out_specs = [pl.BlockSpec((tm, D_MODEL), row)]
    out_shape = [jax.ShapeDtypeStruct((m, D_MODEL), F32)]
    if gated:
        out_specs.append(pl.BlockSpec((D_ATTN + D_CONV, D_MODEL), lambda i: (0, 0)))
        out_shape.append(jax.ShapeDtypeStruct((D_ATTN + D_CONV, D_MODEL), BF16))
    out = pl.pallas_call(
        functools.partial(_outproj_body, gated),
        grid=(m // tm,),
        in_specs=in_specs,
        out_specs=out_specs,
        out_shape=out_shape,
        compiler_params=_cparams(1),
        name="outproj_sample" if gated else "outproj_prompt",
    )(*args)
    return out if gated else out[0]


def _rope_tables(pos):
    posf = np.asarray(pos, np.float64)[:, None]
    n = posf.shape[0]

    def cs(half):
        inv = ROPE_THETA ** (-np.arange(half, dtype=np.float64) / half)
        ang = posf * inv[None, :]
        return np.cos(ang), np.sin(ang)

    c16, s16 = cs(ROPE_HALF)
    c8, s8 = cs(IDX_ROPE_HALF)
    one = lambda w: np.ones((n, w))
    zero = lambda w: np.zeros((n, w))
    rest = HEAD_DIM - ROPE_DIM
    k_c = np.concatenate([c16, c16, one(rest)], axis=1)
    k_sa = np.concatenate([-s16, zero(HEAD_DIM - ROPE_HALF)], axis=1)
    k_sb = np.concatenate([zero(ROPE_HALF), s16, zero(rest)], axis=1)
    tabs = dict(c16T=c16.T, s16T=s16.T, c8T=c8.T, s8T=s8.T, kC=k_c, kSa=k_sa, kSb=k_sb)
    return {name: jnp.asarray(np.ascontiguousarray(t), F32) for name, t in tabs.items()}


def _prep_weights(w_in, g_q, g_k, g_kidx, w_conv, w_out):
    assert w_in.shape == (D_MODEL, D_IN)
    wT = w_in.T
    gq = g_q.reshape(HEAD_DIM, 1)
    gk = g_k.reshape(1, HEAD_DIM)
    gki = g_kidx.reshape(IDX_DIM, 1)
    return wT, (gq, gk, gki, w_conv), w_out


def kernel(x_prompt, x_sample, cache_k, cache_v, cache_kidx, state_conv, page_table,
           norm_in, w_in, g_q, g_k, g_kidx, w_conv, w_out):
    n_b, seq, _ = x_prompt.shape
    n_s, n_t, _ = x_sample.shape
    depth = w_in.shape[0]
    past = page_table.shape[1] * PAGE_SIZE
    tabs_p = _rope_tables(np.arange(seq))
    tabs_s = _rope_tables(np.tile(past + np.arange(n_t), n_s))

    hp = x_prompt.reshape(n_b * seq, D_MODEL)
    hs = x_sample.reshape(n_s * n_t, D_MODEL)
    outs = [[] for _ in range(8)]
    for l in range(depth):
        w_t, params, wo = _prep_weights(w_in[l], g_q[l], g_k[l], g_kidx[l], w_conv[l], w_out[l])

        st = state_conv[l]
        tok = jnp.arange(n_t)
        e1 = st[:, jnp.full((n_t,), CONV_W - 2)].reshape(n_s * n_t, D_CONV)
        e2 = st[:, jnp.minimum(tok, CONV_W - 2)].reshape(n_s * n_t, D_CONV)
        (qT_s, qiT_s, _, wT_s, k32_s, kbf_s, v32_s, ki32_s, kibf_s, gate_s, mixc_s, u, *w_bf) = _project(
            hs, norm_in[l], w_t, params, tabs_s, n_t, state_rows=(e1, e2))
        idx_ops, (q_rows, k_new, v_new) = _sample_operands(
            qT_s, qiT_s, wT_s, kibf_s, kbf_s, v32_s, cache_kidx[l], n_s, n_t)
        (qT, qiT, vT, wT, k32, kbf, v32, ki32, kibf, gate, mixc, utail, sp, sn) = _project(
            hp, norm_in[l], tuple(w_bf), params, tabs_p, 0, side=(page_table,) + idx_ops)

        bp, bn = _sample_topk(sp, sn, n_t)
        mixa, attn_rows = _attention(qiT, wT, kibf, qT, kbf, vT, gate, n_b, seq,
                                     q_rows, bp, bn, k_new, v_new, cache_k[l], cache_v[l], page_table)
        attn_s = _sample_rows_to_tokens(attn_rows, n_s, n_t)

        hs, wo_bf = _outproj(hs, attn_s, gate_s, mixc_s, wo, n_s * n_t)
        hp = _outproj(hp, mixa, None, mixc, wo_bf, ROW_TM)
        tps = seq // PROJ_TM
        outs[0].append(k32.reshape(n_b, seq, N_KV_HEADS, HEAD_DIM))
        outs[1].append(v32.reshape(n_b, seq, N_KV_HEADS, HEAD_DIM))
        outs[2].append(ki32.reshape(n_b, seq, IDX_DIM))
        outs[3].append(utail[tps - 1::tps])
        outs[4].append(k32_s.reshape(n_s, n_t, N_KV_HEADS, HEAD_DIM))
        outs[5].append(v32_s.reshape(n_s, n_t, N_KV_HEADS, HEAD_DIM))
        outs[6].append(ki32_s.reshape(n_s, n_t, IDX_DIM))
        outs[7].append(u.reshape(n_s, n_t, D_CONV)[:, n_t - (CONV_W - 1):])

    return (hp.reshape(n_b, seq, D_MODEL), hs.reshape(n_s, n_t, D_MODEL),
            *[jnp.stack(o) for o in outs])
```

```python
import functools

import jax
import jax.numpy as jnp
import numpy as np
from jax import lax
from jax.experimental import pallas as pl
from jax.experimental.pallas import tpu as pltpu

F32 = jnp.float32
BF16 = jnp.bfloat16
I32 = jnp.int32

D_MODEL = 2048
HEAD_DIM = 128
N_HEADS = 8
N_KV_HEADS = 2
GROUP = N_HEADS // N_KV_HEADS
D_ATTN = N_HEADS * HEAD_DIM
D_CONV = 1024
ROPE_DIM = HEAD_DIM // 4
ROPE_HALF = ROPE_DIM // 2
ROPE_THETA = 500000.0
N_IDX_HEADS = 16
IDX_DIM = 64
IDX_ROPE_HALF = IDX_DIM // 8
TOPK_MAX = 256
CONV_W = 3
PAGE_SIZE = 128
EPS = 1e-6
W_IDX_SCALE = (N_IDX_HEADS ** -0.5) * (IDX_DIM ** -0.5)
ATTN_SCALE = HEAD_DIM ** -0.5
Q_PRESCALE = ATTN_SCALE * 1.4426950408889634

INT_MIN = -(2 ** 31)
NEG = -1e30

VMEM_LIMIT_BYTES = 60 * 1024 * 1024

PROJ_TN = 512
PROJ_TM = 1024
PROJ_SUB = 256
ROW_TM = 512
N_KV = N_KV_HEADS * HEAD_DIM
OFF_Q = 0
OFF_K = OFF_Q + D_ATTN
OFF_V = OFF_K + N_KV
OFF_Z = OFF_V + N_KV
OFF_QI = OFF_Z + D_ATTN
OFF_KI = OFF_QI + N_IDX_HEADS * IDX_DIM
OFF_WI = OFF_KI + IDX_DIM
OFF_H = OFF_WI + N_IDX_HEADS
OFF_B = OFF_H + D_CONV
OFF_C = OFF_B + D_CONV
OFF_ZC = OFF_C + D_CONV
D_IN = OFF_ZC + D_CONV
assert OFF_K == 2 * PROJ_TN and OFF_Z == 3 * PROJ_TN and OFF_QI == 5 * PROJ_TN and OFF_KI == 7 * PROJ_TN
KIW_ROWS = 128
assert OFF_KI % KIW_ROWS == 0 and OFF_WI - OFF_KI == IDX_DIM
CONV_CHUNK = 256
N_CONV_CHUNKS = D_CONV // CONV_CHUNK
J_Q = 0
J_KV = 2
J_Z = 3
J_QI = 5
J_CONV = 7
N_MAIN_BLOCKS = J_CONV
N_STEPS = J_CONV + N_CONV_CHUNKS

ATT_TQ = 256
ATT_TK = 256
CNT_ACCS = 4

SIDE_PAGES = 16


def _cparams(n_axes):
    return pltpu.CompilerParams(
        dimension_semantics=("arbitrary",) * n_axes,
        vmem_limit_bytes=VMEM_LIMIT_BYTES,
    )


def _silu(x):
    return x * jax.nn.sigmoid(x)


def _tile_rows(x8, period):
    row = lax.broadcasted_iota(I32, x8.shape, 0)
    out = x8
    for q in range(1, 8 // period):
        out = jnp.where(row // period == q, pltpu.roll(x8, q * period, axis=0), out)
    return out


def _indexer_rows_scores(qi, wcol, keys_t):
    n_tok = qi.shape[0] // N_IDX_HEADS
    d = jnp.dot(qi, keys_t, preferred_element_type=F32)
    val = jnp.maximum(d, 0.0) * wcol
    v = val.reshape(val.shape[0] // 8, 8, val.shape[-1]).sum(axis=0)
    out = v
    for q in range(1, 8 // n_tok):
        out = out + pltpu.roll(v, q * n_tok, axis=0)
    return out


class _SideScores:
    def __init__(self, step, n_total, spq, pt_ref, qi_ref, wcol_ref, kinew_ref, kidx_hbm, sp_ref, sn_ref,
                 kbuf, ksem):
        self.step, self.n_total, self.spq, self.pt_ref = step, n_total, spq, pt_ref
        self.qi_ref, self.wcol_ref, self.kinew_ref, self.kidx_hbm = qi_ref, wcol_ref, kinew_ref, kidx_hbm
        self.sp_ref, self.sn_ref, self.kbuf, self.ksem = sp_ref, sn_ref, kbuf, ksem
        self.npg = kbuf.shape[1]
        self.n_side = pt_ref.shape[0] * spq

    def _start(self, n):
        c = jnp.minimum(n, self.n_side - 1)
        seq_n = c // self.spq
        base = (c % self.spq) * self.npg
        for r in range(self.npg):
            _page_copy(self.kidx_hbm, self.pt_ref[seq_n, base + r], self.kbuf, n % 2, r, self.ksem).start()

    def prologue(self):
        pl.when(self.step == 0)(lambda: self._start(self.step))

    def __call__(self):
        pl.when(self.step + 1 < self.n_total)(lambda: self._start(self.step + 1))
        slot = self.step % 2
        for r in range(self.npg):
            _page_copy(self.kidx_hbm, 0, self.kbuf, slot, r, self.ksem).wait()
        qi = self.qi_ref[0]
        wcol = self.wcol_ref[0]
        keys_t = jnp.concatenate([self.kbuf[slot, r] for r in range(self.npg)], axis=1).astype(BF16)
        self.sp_ref[0] = _indexer_rows_scores(qi, wcol, keys_t)
        self.sn_ref[0] = _indexer_rows_scores(qi, wcol, self.kinew_ref[0])


class _Bf16View:
    def __init__(self, ref):
        self.ref = ref

    def __getitem__(self, idx):
        return self.ref[idx].astype(BF16)


def _proj_body(sample, tps, tm, side_spq, convert, *refs):
    if side_spq:
        pt_ref, refs = refs[0], refs[1:]
    (x_ref, gin_ref, wa_ref, wkiw_ref, wh_ref, wb_ref, wc_ref, wzc_ref, gq_ref, gk_ref, gki_ref,
     c16_ref, s16_ref, c8_ref, s8_ref,
     kc_ref, ksa_ref, ksb_ref, wconv_ref) = refs[:19]
    refs = refs[19:]
    if sample:
        e1_ref, e2_ref = refs[:2]
        refs = refs[2:]
    if side_spq:
        side_in, refs = refs[:4], refs[4:]
    (qT_ref, qiT_ref, vT_ref, wT_ref, k32_ref, kbf_ref, v32_ref, ki32_ref, kibf_ref,
     gate_ref, mixc_ref, u_ref) = refs[:12]
    refs = refs[12:]
    if convert:
        w_out, refs = refs[:6], refs[6:]
    if side_spq:
        side_out, refs = refs[:2], refs[2:]
    xn_ref = refs[0]
    if not sample:
        carry_ref = refs[1]

    i = pl.program_id(0)
    j = pl.program_id(1)
    nt = (((1,), (1,)), ((), ()))
    if side_spq:
        side = _SideScores(i * N_STEPS + j, pl.num_programs(0) * N_STEPS, side_spq, pt_ref,
                           *side_in, *side_out, *refs[2:4])
        side.prologue()
    else:
        side = lambda: None

    if convert:
        wa_o, kiw_o, wh_o, wb_o, wc_o, wzc_o = w_out

        @pl.when(j < N_MAIN_BLOCKS)
        def _():
            wa_o[...] = wa_ref[...].astype(BF16)

        @pl.when(j == J_KV)
        def _():
            kiw_o[...] = wkiw_ref[...].astype(BF16)

        @pl.when(j >= J_CONV)
        def _():
            for src, dst in ((wh_ref, wh_o), (wb_ref, wb_o), (wc_ref, wc_o), (wzc_ref, wzc_o)):
                dst[...] = src[...].astype(BF16)

        wa_ref, wkiw_ref, wh_ref, wb_ref, wc_ref, wzc_ref = (
            _Bf16View(r) for r in (wa_ref, wkiw_ref, wh_ref, wb_ref, wc_ref, wzc_ref))

    def xw(w, x=None):
        return lax.dot_general(xn_ref[...] if x is None else x, w, nt, preferred_element_type=F32)

    def wx(w, x):
        return lax.dot_general(w, x, nt, preferred_element_type=F32)

    nsub = max(tm // PROJ_SUB, 1)
    sub = tm // nsub
    pieces = [slice(rb * sub, (rb + 1) * sub) for rb in range(nsub)]

    if not sample:
        @pl.when((i == 0) & (j == 0))
        def _init():
            carry_ref[...] = jnp.zeros(carry_ref.shape, F32)

    def q_step(with_norm):
        side()
        for cs in pieces:
            if with_norm:
                x = x_ref[cs, :]
                ms = jnp.mean(x * x, axis=-1, keepdims=True)
                xn_ref[cs, :] = (x * lax.rsqrt(ms + EPS) * gin_ref[...]).astype(BF16)
            res = wx(wa_ref[...], xn_ref[cs, :])
            cos = c16_ref[:, cs]
            sin = s16_ref[:, cs]
            for hh in range(PROJ_TN // HEAD_DIM):
                blk = res[hh * HEAD_DIM:(hh + 1) * HEAD_DIM]
                ms = jnp.mean(blk * blk, axis=0, keepdims=True)
                y = blk * lax.rsqrt(ms + EPS) * gq_ref[...]
                x1 = y[0:ROPE_HALF]
                x2 = y[ROPE_HALF:ROPE_DIM]
                base = hh * HEAD_DIM
                qT_ref[base:base + ROPE_HALF, cs] = ((x1 * cos - x2 * sin) * Q_PRESCALE).astype(BF16)
                qT_ref[base + ROPE_HALF:base + ROPE_DIM, cs] = ((x2 * cos + x1 * sin) * Q_PRESCALE).astype(BF16)
                qT_ref[base + ROPE_DIM:base + HEAD_DIM, cs] = (y[ROPE_DIM:] * Q_PRESCALE).astype(BF16)

    pl.when(j == J_Q)(functools.partial(q_step, True))
    pl.when((j > J_Q) & (j < J_KV))(functools.partial(q_step, False))

    @pl.when((j >= J_QI) & (j < J_CONV))
    def _qi():
        side()
        for cs in pieces:
            res = wx(wa_ref[...], xn_ref[cs, :])
            cos = c8_ref[:, cs]
            sin = s8_ref[:, cs]
            for hh in range(PROJ_TN // IDX_DIM):
                blk = res[hh * IDX_DIM:(hh + 1) * IDX_DIM]
                x1 = blk[0:IDX_ROPE_HALF]
                x2 = blk[IDX_ROPE_HALF:2 * IDX_ROPE_HALF]
                rot = jnp.concatenate([x1 * cos - x2 * sin, x2 * cos + x1 * sin], axis=0)
                base = hh * IDX_DIM
                qiT_ref[base:base + 2 * IDX_ROPE_HALF, cs] = rot.astype(BF16)
                qiT_ref[base + 2 * IDX_ROPE_HALF:base + IDX_DIM, cs] = blk[2 * IDX_ROPE_HALF:].astype(BF16)

    @pl.when(j == J_KV)
    def _kv():
        side()
        for rb, cs in enumerate(pieces):
            x = xn_ref[cs, :]
            res = xw(wa_ref[...], x)
            for hd in range(N_KV_HEADS):
                hs = slice(hd * HEAD_DIM, (hd + 1) * HEAD_DIM)
                blk = res[:, hs]
                ms = jnp.mean(blk * blk, axis=-1, keepdims=True)
                yk = blk * lax.rsqrt(ms + EPS) * gk_ref[...]
                rot = yk * kc_ref[cs, :] + (pltpu.roll(yk, HEAD_DIM - ROPE_HALF, axis=1) * ksa_ref[cs, :]
                                            + pltpu.roll(yk, ROPE_HALF, axis=1) * ksb_ref[cs, :])
                k32_ref[pl.ds(N_KV_HEADS * rb * sub + hd, sub, stride=N_KV_HEADS), :] = rot
                kbf_ref[cs, hs] = rot.astype(BF16)
            v = res[:, N_KV:2 * N_KV]
            for hd in range(N_KV_HEADS):
                v32_ref[pl.ds(N_KV_HEADS * rb * sub + hd, sub, stride=N_KV_HEADS), :] = (
                    v[:, hd * HEAD_DIM:(hd + 1) * HEAD_DIM])
            if sample:
                vT_ref[:, cs] = wx(wa_ref[N_KV:2 * N_KV, :], x).astype(BF16)
            else:
                vT_ref[:, cs] = v.T.astype(BF16)
            r2t = wx(wkiw_ref[...], x)
            wT_ref[:, cs] = r2t[IDX_DIM:IDX_DIM + N_IDX_HEADS] * W_IDX_SCALE
            kit = r2t[0:IDX_DIM]
            ms = jnp.mean(kit * kit, axis=0, keepdims=True)
            yi = kit * lax.rsqrt(ms + EPS) * gki_ref[...]
            x1 = yi[0:IDX_ROPE_HALF]
            x2 = yi[IDX_ROPE_HALF:2 * IDX_ROPE_HALF]
            cos = c8_ref[:, cs]
            sin = s8_ref[:, cs]
            roti = jnp.concatenate([x1 * cos - x2 * sin, x2 * cos + x1 * sin, yi[2 * IDX_ROPE_HALF:],
                                    jnp.zeros((KIW_ROWS - IDX_DIM, sub), F32)], axis=0)
            ki_nat = roti.T
            ki32_ref[cs, :] = ki_nat[:, 0:IDX_DIM]
            kibf_ref[cs, :] = ki_nat.astype(BF16)

    @pl.when((j >= J_Z) & (j < J_QI))
    def _z():
        side()
        for cs in pieces:
            gate_ref[cs, :] = _silu(xw(wa_ref[...], xn_ref[cs, :]))

    @pl.when(j >= J_CONV)
    def _conv():
        side()
        cc = j - J_CONV
        u = xw(wc_ref[...]) * xw(wh_ref[...])
        rowid = lax.broadcasted_iota(I32, (tm, CONV_CHUNK), 0)
        if sample:
            t = rowid & (sample - 1)
            u1 = jnp.where(t >= 1, pltpu.roll(u, 1, axis=0), e1_ref[...])
            u2 = jnp.where(t >= 2, pltpu.roll(u, 2, axis=0), e2_ref[...])
            u_ref[...] = u
        else:
            first = (i % tps) == 0
            prev = carry_ref[cc]
            p0 = jnp.where(first, 0.0, prev[0:1])
            p1 = jnp.where(first, 0.0, prev[1:2])
            u1 = jnp.where(rowid == 0, p1, pltpu.roll(u, 1, axis=0))
            u2 = jnp.where(rowid == 0, p0, jnp.where(rowid == 1, p1, pltpu.roll(u, 2, axis=0)))
            tail = u[tm - 8:tm]
            carry_ref[cc] = jnp.concatenate([tail[6:8], tail[0:6]], axis=0)
            u_ref[0] = tail[6:8]
        w = wconv_ref[...]
        y = u2 * w[0:1] + u1 * w[1:2] + u * w[2:3]
        mixc_ref[...] = (xw(wb_ref[...]) * y * _silu(xw(wzc_ref[...]))).astype(BF16)


def _project(x, norm_g, wts, params, tabs, sample, state_rows=None, side=None):
    m = x.shape[0]
    convert = not isinstance(wts, tuple)
    assert sample & (sample - 1) == 0
    tm = m if sample else PROJ_TM
    n_i = m // tm
    tps = 1 if sample else (tabs["c16T"].shape[1] // tm)
    gq, gk, gki, wconv = params

    def tmap(i):
        return i % tps

    def cchunk(j):
        return jnp.clip(j - J_CONV, 0, N_CONV_CHUNKS - 1)

    main_spec = pl.BlockSpec((PROJ_TN, D_MODEL), lambda i, j, *_: (jnp.minimum(j, N_MAIN_BLOCKS - 1), 0))
    if convert:
        def conv_rows(off):
            return pl.BlockSpec((pl.Element(CONV_CHUNK), pl.Element(D_MODEL)),
                                lambda i, j, *_: (pl.multiple_of(off + CONV_CHUNK * cchunk(j), 16), 0))

        w_specs = [main_spec, pl.BlockSpec((KIW_ROWS, D_MODEL), lambda i, j, *_: (OFF_KI // KIW_ROWS, 0)),
                   conv_rows(OFF_H), conv_rows(OFF_B), conv_rows(OFF_C), conv_rows(OFF_ZC)]
        w_args = [wts] * 6
    else:
        conv_spec = pl.BlockSpec(
            (CONV_CHUNK, D_MODEL),
            lambda i, j, *_: (jnp.where(j < J_QI, N_CONV_CHUNKS - 1, cchunk(j)), 0))
        w_specs = [main_spec, pl.BlockSpec((KIW_ROWS, D_MODEL), lambda i, j, *_: (0, 0))] + [conv_spec] * 4
        w_args = list(wts)

    in_specs = [
        pl.BlockSpec((tm, D_MODEL), lambda i, j, *_: (i, 0)),
        pl.BlockSpec((1, D_MODEL), lambda i, j, *_: (0, 0)),
        *w_specs,
        pl.BlockSpec((HEAD_DIM, 1), lambda i, j, *_: (0, 0)),
        pl.BlockSpec((1, HEAD_DIM), lambda i, j, *_: (0, 0)),
        pl.BlockSpec((IDX_DIM, 1), lambda i, j, *_: (0, 0)),
        pl.BlockSpec((ROPE_HALF, tm), lambda i, j, *_: (0, tmap(i))),
        pl.BlockSpec((ROPE_HALF, tm), lambda i, j, *_: (0, tmap(i))),
        pl.BlockSpec((IDX_ROPE_HALF, tm), lambda i, j, *_: (0, tmap(i))),
        pl.BlockSpec((IDX_ROPE_HALF, tm), lambda i, j, *_: (0, tmap(i))),
        pl.BlockSpec((tm, HEAD_DIM), lambda i, j, *_: (tmap(i), 0)),
        pl.BlockSpec((tm, HEAD_DIM), lambda i, j, *_: (tmap(i), 0)),
        pl.BlockSpec((tm, HEAD_DIM), lambda i, j, *_: (tmap(i), 0)),
        pl.BlockSpec((CONV_W, CONV_CHUNK), lambda i, j, *_: (0, cchunk(j))),
    ]
    args = [x, norm_g.reshape(1, D_MODEL), *w_args, gq, gk, gki,
            tabs["c16T"], tabs["s16T"], tabs["c8T"], tabs["s8T"],
            tabs["kC"], tabs["kSa"], tabs["kSb"], wconv]
    cmap = lambda i, j, *_: (i, cchunk(j))
    if sample:
        in_specs += [pl.BlockSpec((tm, CONV_CHUNK), cmap), pl.BlockSpec((tm, CONV_CHUNK), cmap)]
        args += list(state_rows)
        u_spec = pl.BlockSpec((tm, CONV_CHUNK), cmap)
        u_shape = jax.ShapeDtypeStruct((m, D_CONV), F32)
        scratch = [pltpu.VMEM((tm, D_MODEL), BF16)]
    else:
        u_spec = pl.BlockSpec((1, CONV_W - 1, CONV_CHUNK), lambda i, j, *_: (i, 0, cchunk(j)))
        u_shape = jax.ShapeDtypeStruct((n_i, CONV_W - 1, D_CONV), F32)
        scratch = [pltpu.VMEM((tm, D_MODEL), BF16), pltpu.VMEM((N_CONV_CHUNKS, 8, CONV_CHUNK), F32)]

    out_specs = [
        pl.BlockSpec((PROJ_TN, tm), lambda i, j, *_: (jnp.minimum(j, 1), i)),
        pl.BlockSpec((PROJ_TN, tm), lambda i, j, *_: (jnp.clip(j - J_QI, 0, 1), i)),
        pl.BlockSpec((N_KV, tm), lambda i, j, *_: (0, i)),
        pl.BlockSpec((N_IDX_HEADS, tm), lambda i, j, *_: (0, i)),
        pl.BlockSpec((N_KV_HEADS * tm, HEAD_DIM), lambda i, j, *_: (i, 0)),
        pl.BlockSpec((tm, N_KV), lambda i, j, *_: (i, 0)),
        pl.BlockSpec((N_KV_HEADS * tm, HEAD_DIM), lambda i, j, *_: (i, 0)),
        pl.BlockSpec((tm, IDX_DIM), lambda i, j, *_: (i, 0)),
        pl.BlockSpec((tm, KIW_ROWS), lambda i, j, *_: (i, 0)),
        pl.BlockSpec((tm, PROJ_TN), lambda i, j, *_: (i, jnp.clip(j - J_Z, 0, 1))),
        pl.BlockSpec((tm, CONV_CHUNK), cmap),
        u_spec,
    ]
    out_shape = [
        jax.ShapeDtypeStruct((D_ATTN, m), BF16),
        jax.ShapeDtypeStruct((N_IDX_HEADS * IDX_DIM, m), BF16),
        jax.ShapeDtypeStruct((N_KV, m), BF16),
        jax.ShapeDtypeStruct((N_IDX_HEADS, m), F32),
        jax.ShapeDtypeStruct((N_KV_HEADS * m, HEAD_DIM), F32),
        jax.ShapeDtypeStruct((m, N_KV), BF16),
        jax.ShapeDtypeStruct((N_KV_HEADS * m, HEAD_DIM), F32),
        jax.ShapeDtypeStruct((m, IDX_DIM), F32),
        jax.ShapeDtypeStruct((m, KIW_ROWS), BF16),
        jax.ShapeDtypeStruct((m, D_ATTN), F32),
        jax.ShapeDtypeStruct((m, D_CONV), BF16),
        u_shape,
    ]
    if convert:
        assert n_i == 1, "every weight block must be visited exactly once to be written back"
        out_specs += [
            main_spec,
            pl.BlockSpec((KIW_ROWS, D_MODEL), lambda i, j, *_: (0, 0)),
        ] + [pl.BlockSpec((CONV_CHUNK, D_MODEL), lambda i, j, *_: (cchunk(j), 0))] * 4
        out_shape += [jax.ShapeDtypeStruct((OFF_KI, D_MODEL), BF16),
                      jax.ShapeDtypeStruct((KIW_ROWS, D_MODEL), BF16)]
        out_shape += [jax.ShapeDtypeStruct((D_CONV, D_MODEL), BF16)] * 4
    if side is None:
        return pl.pallas_call(
            functools.partial(_proj_body, sample, tps, tm, 0, convert),
            grid=(n_i, N_STEPS),
            in_specs=in_specs,
            out_specs=out_specs,
            out_shape=out_shape,
            scratch_shapes=scratch,
            compiler_params=_cparams(2),
            name="proj_sample" if sample else "proj_prompt",
        )(*args)

    page_table, qi_rows, wcol, ki_new_t, kidx_t = side
    n_seq, n_pages = page_table.shape
    spq = n_pages // SIDE_PAGES
    n_side = n_seq * spq
    assert spq * SIDE_PAGES == n_pages and n_side <= n_i * N_STEPS
    width = SIDE_PAGES * PAGE_SIZE

    def side_step(i, j):
        return jnp.minimum(i * N_STEPS + j, n_side - 1)

    seq_map = lambda i, j, *_: (side_step(i, j) // spq, 0, 0)
    in_specs += [
        pl.BlockSpec((1,) + qi_rows.shape[1:], seq_map),
        pl.BlockSpec((1,) + wcol.shape[1:], seq_map),
        pl.BlockSpec((1, IDX_DIM, PAGE_SIZE), seq_map),
        pl.BlockSpec(memory_space=pl.ANY),
    ]
    out_specs += [
        pl.BlockSpec((1, 8, width), lambda i, j, *_: (side_step(i, j) // spq, 0, side_step(i, j) % spq)),
        pl.BlockSpec((1, 8, PAGE_SIZE), seq_map),
    ]
    out_shape += [jax.ShapeDtypeStruct((n_seq, 8, n_pages * PAGE_SIZE), F32),
                  jax.ShapeDtypeStruct((n_seq, 8, PAGE_SIZE), F32)]
    scratch += [pltpu.VMEM((2, SIDE_PAGES, IDX_DIM, PAGE_SIZE), F32), pltpu.SemaphoreType.DMA((2,))]
    return pl.pallas_call(
        functools.partial(_proj_body, sample, tps, tm, spq, convert),
        grid_spec=pltpu.PrefetchScalarGridSpec(
            num_scalar_prefetch=1,
            grid=(n_i, N_STEPS),
            in_specs=in_specs,
            out_specs=out_specs,
            scratch_shapes=scratch,
        ),
        out_shape=out_shape,
        compiler_params=_cparams(2),
        name="proj_prompt",
    )(page_table, *args, qi_rows, wcol, ki_new_t, kidx_t)


KEY_LOWEST_FINITE = INT_MIN + 0x00800000


def _key_to_float(key):
    return pltpu.bitcast(key ^ ((key >> 31) & 0x7FFFFFFF), F32)


def _bit_value(b):
    return lax.shift_left(jnp.int32(1), jnp.int32(31) - b)


def _prompt_attn_step(i, qiT_ref, wT_ref, ki_ref, qT_ref, k_ref, vT_ref, gate_ref, o_ref,
                      sc_ref, bias_ref, acc_ref, s_ref):
    tq, tk = ATT_TQ, ATT_TK
    nch = i + 1
    w = wT_ref[...]
    row = lax.broadcasted_iota(I32, (tk, tq), 0)
    col = lax.broadcasted_iota(I32, (tk, tq), 1)
    row8 = lax.broadcasted_iota(I32, (8, tq), 0)

    def score_chunk(j, carry):
        off = pl.multiple_of(j * tk, tk)
        kic = ki_ref[pl.ds(off, tk), 0:IDX_DIM]
        acc = jnp.zeros((tk, tq), F32)
        for h in range(N_IDX_HEADS):
            d = jnp.dot(kic, qiT_ref[h * IDX_DIM:(h + 1) * IDX_DIM, :], preferred_element_type=F32)
            acc = acc + w[h:h + 1, :] * jnp.maximum(d, 0.0)
        future = (row + j * tk) > (col + i * tq)
        sc_ref[pl.ds(off, tk), :] = jnp.where(future, -jnp.inf, acc)
        return carry

    lax.fori_loop(0, nch, score_chunk, 0)

    def count_keys(pred):
        def cnt_chunk(j, cs):
            off = pl.multiple_of(j * tk, tk)
            cs = list(cs)
            sc = sc_ref[pl.ds(off, tk), :]
            for r in range(tk // 8):
                a = cs[r % CNT_ACCS]
                kpos = row8 + (j * tk + r * 8)
                cs[r % CNT_ACCS] = jnp.where(pred(sc[r * 8:(r + 1) * 8], kpos), a + 1, a)
            return tuple(cs)

        cs = lax.fori_loop(0, nch, cnt_chunk, (jnp.zeros((8, tq), I32),) * CNT_ACCS)
        c = cs[0]
        for a in cs[1:]:
            c = c + a
        return jnp.sum(c.astype(F32), axis=0, keepdims=True)

    def bit_body(b, carry):
        thr, cge = carry
        cand = thr + _bit_value(b)
        cand_f = _key_to_float(cand)
        cnt = count_keys(lambda s, kpos: s >= cand_f)
        ok = cnt >= float(TOPK_MAX)
        return jnp.where(ok, cand, thr), jnp.where(ok, cnt, cge)

    n_bits = jnp.where(nch * tk > TOPK_MAX, 32, 0)
    thr, cge = lax.fori_loop(0, n_bits, bit_body, (jnp.full((1, tq), INT_MIN, I32),
                                                   jnp.zeros((1, tq), F32)))
    thr_f = _key_to_float(jnp.maximum(thr, KEY_LOWEST_FINITE))
    has_ties = jnp.max(cge) > float(TOPK_MAX)

    @pl.when(jnp.logical_not(has_ties))
    def _plain_mask():
        def bias_chunk(j, carry):
            off = pl.multiple_of(j * tk, tk)
            bias_ref[pl.ds(off, tk), :] = jnp.where(sc_ref[pl.ds(off, tk), :] >= thr_f, 0.0, NEG)
            return carry

        lax.fori_loop(0, nch, bias_chunk, 0)

    @pl.when(has_ties)
    def _tie_mask():
        need = float(TOPK_MAX) - count_keys(lambda s, kpos: s > thr_f)
        nbits = (sc_ref.shape[0] - 1).bit_length()

        def pos_bit(b, last):
            step = lax.shift_left(jnp.int32(1), jnp.int32(nbits - 1) - b)
            probe = last + (step - 1)
            got = count_keys(lambda s, kpos: (s == thr_f) & (kpos <= probe))
            return jnp.where(got < need, last + step, last)

        last = lax.fori_loop(0, nbits, pos_bit, jnp.zeros((1, tq), I32))

        def bias_chunk(j, carry):
            off = pl.multiple_of(j * tk, tk)
            sc = sc_ref[pl.ds(off, tk), :]
            keep = (sc > thr_f) | ((sc == thr_f) & ((row + j * tk) <= last))
            bias_ref[pl.ds(off, tk), :] = jnp.where(keep, 0.0, NEG)
            return carry

        lax.fori_loop(0, nch, bias_chunk, 0)

    def qk_chunk(j, mrun):
        off = pl.multiple_of(j * tk, tk)
        bias = bias_ref[pl.ds(off, tk), :]
        out = []
        for h in range(N_HEADS):
            g = h // GROUP
            kc = k_ref[pl.ds(off, tk), g * HEAD_DIM:(g + 1) * HEAD_DIM]
            s = jnp.dot(kc, qT_ref[h * HEAD_DIM:(h + 1) * HEAD_DIM, :], preferred_element_type=F32) + bias
            s_ref[h, pl.ds(off, tk), :] = s
            out.append(jnp.maximum(mrun[h], s.reshape(tk // 8, 8, tq).max(axis=0)))
        return tuple(out)

    mrun = lax.fori_loop(0, nch, qk_chunk, (jnp.full((8, tq), NEG, F32),) * N_HEADS)
    ms = [jnp.max(mr, axis=0, keepdims=True) for mr in mrun]
    acc_ref[...] = jnp.zeros(acc_ref.shape, F32)

    def pv_chunk(j, lrun):
        off = pl.multiple_of(j * tk, tk)
        out = []
        for h in range(N_HEADS):
            g = h // GROUP
            hs = slice(h * HEAD_DIM, (h + 1) * HEAD_DIM)
            p = jnp.exp2(s_ref[h, pl.ds(off, tk), :] - ms[h])
            out.append(lrun[h] + p.reshape(tk // 8, 8, tq).sum(axis=0))
            vc = vT_ref[g * HEAD_DIM:(g + 1) * HEAD_DIM, pl.ds(off, tk)]
            acc_ref[hs, :] += jnp.dot(vc, p.astype(BF16), preferred_element_type=F32)
        return tuple(out)

    lrun = lax.fori_loop(0, nch, pv_chunk, (jnp.zeros((8, tq), F32),) * N_HEADS)

    for h in range(N_HEADS):
        hs = slice(h * HEAD_DIM, (h + 1) * HEAD_DIM)
        o = (acc_ref[hs, :] / jnp.sum(lrun[h], axis=0, keepdims=True)).T
        o_ref[:, hs] = (o * gate_ref[:, hs]).astype(BF16)


def _page_copy(cache_hbm, page, pbuf, slot, r, sem):
    return pltpu.make_async_copy(cache_hbm.at[page], pbuf.at[slot, r], sem.at[slot])


def _attn_fused_body(nq, spq, pt_ref, *refs):
    prompt_in = refs[0:7]
    q_ref, bp_ref, bn_ref, knew_ref, vnew_ref, ck_hbm, cv_hbm = refs[7:14]
    o_ref, so_ref = refs[14:16]
    prompt_scratch = refs[16:20]
    pbuf, sem, ss_ref, ps_ref, ls_ref, accs_ref = refs[20:26]

    i = pl.program_id(1)
    step = pl.program_id(0) * nq + i
    n_total = pl.num_programs(0) * nq
    half = spq // 2
    npg = pbuf.shape[1]

    slot = step % 2

    def start_pages(seq_n, t_n, to_slot):
        cache_hbm = ck_hbm if t_n < half else cv_hbm
        base = (t_n % half) * npg
        for r in range(npg):
            _page_copy(cache_hbm, pt_ref[seq_n, base + r], pbuf, to_slot, r, sem).start()

    def start_next(t_next):
        start_pages(step // spq + t_next // spq, t_next % spq, 1 - slot)

    @pl.when(step == 0)
    def _prologue():
        start_pages(0, 0, 0)

    pltpu.make_async_copy(ck_hbm.at[pl.ds(0, npg)], pbuf.at[slot], sem.at[slot]).wait()

    _sample_attn_step(step % spq, half, slot, step + 1 < n_total, start_next,
                      q_ref, bp_ref, bn_ref, knew_ref, vnew_ref, pbuf,
                      so_ref, ss_ref, ps_ref, ls_ref, accs_ref)
    _prompt_attn_step(i, *prompt_in, o_ref, *prompt_scratch)


def _attention(qiT, wT, kibf, qT, kbf, vT, gate, n_batch, seq,
               q_rows, bp, bn, k_new, v_new, cache_k, cache_v, page_table):
    m = n_batch * seq
    nq = seq // ATT_TQ
    n_seq, n_pages = page_table.shape
    n_steps = n_batch * nq
    spq = n_steps // n_seq
    assert spq * n_seq == n_steps and spq % 2 == 0
    half = spq // 2
    npg = n_pages // half
    assert npg * half == n_pages
    width = npg * PAGE_SIZE
    past = n_pages * PAGE_SIZE
    n_pool = cache_k.shape[0]
    kv_rows = PAGE_SIZE * N_KV_HEADS
    ck = cache_k.reshape(n_pool, kv_rows, HEAD_DIM)
    cv = cache_v.reshape(n_pool, kv_rows, HEAD_DIM)
    nrow = q_rows.shape[1]
    assert (nrow // N_KV_HEADS) % 16 == 0

    qmap = lambda b, i, pt: (0, b * nq + i)
    smap = lambda b, i, pt: ((b * nq + i) // spq, 0, 0)
    return pl.pallas_call(
        functools.partial(_attn_fused_body, nq, spq),
        grid_spec=pltpu.PrefetchScalarGridSpec(
            num_scalar_prefetch=1,
            grid=(n_batch, nq),
            in_specs=[
                pl.BlockSpec((N_IDX_HEADS * IDX_DIM, ATT_TQ), qmap),
                pl.BlockSpec((N_IDX_HEADS, ATT_TQ), qmap),
                pl.BlockSpec((seq, 128), lambda b, i, pt: (b, 0)),
                pl.BlockSpec((D_ATTN, ATT_TQ), qmap),
                pl.BlockSpec((seq, N_KV), lambda b, i, pt: (b, 0)),
                pl.BlockSpec((N_KV, seq), lambda b, i, pt: (0, b)),
                pl.BlockSpec((ATT_TQ, D_ATTN), lambda b, i, pt: (b * nq + i, 0)),
                pl.BlockSpec((1, nrow, HEAD_DIM), smap),
                pl.BlockSpec((1, 8, width),
                             lambda b, i, pt: ((b * nq + i) // spq, 0, jnp.minimum((b * nq + i) % spq, half - 1))),
                pl.BlockSpec((1, 8, PAGE_SIZE), smap),
                pl.BlockSpec((1, PAGE_SIZE, N_KV), smap),
                pl.BlockSpec((1, PAGE_SIZE, N_KV), smap),
                pl.BlockSpec(memory_space=pl.ANY),
                pl.BlockSpec(memory_space=pl.ANY),
            ],
            out_specs=[
                pl.BlockSpec((ATT_TQ, D_ATTN), lambda b, i, pt: (b * nq + i, 0)),
                pl.BlockSpec((1, nrow, HEAD_DIM), smap),
            ],
            scratch_shapes=[
                pltpu.VMEM((seq, ATT_TQ), F32), pltpu.VMEM((seq, ATT_TQ), F32),
                pltpu.VMEM((D_ATTN, ATT_TQ), F32), pltpu.VMEM((N_HEADS, seq, ATT_TQ), F32),
                pltpu.VMEM((2, npg, kv_rows, HEAD_DIM), F32), pltpu.SemaphoreType.DMA((2,)),
                pltpu.VMEM((nrow, past + PAGE_SIZE), F32), pltpu.VMEM((nrow, past + PAGE_SIZE), BF16),
                pltpu.VMEM((nrow, 128), F32), pltpu.VMEM((nrow, HEAD_DIM), F32),
            ],
        ),
        out_shape=[jax.ShapeDtypeStruct((m, D_ATTN), BF16),
                   jax.ShapeDtypeStruct((n_seq, nrow, HEAD_DIM), F32)],
        compiler_params=_cparams(2),
        name="attention",
    )(page_table, qiT, wT, kibf, qT, kbf, vT, gate, q_rows, bp, bn, k_new, v_new, ck, cv)


def _s_thresh_body(n_new, in_p_ref, in_n_ref, out_p_ref, out_n_ref, sp_ref):
    past = in_p_ref.shape[1]
    rows = sp_ref.shape[0]
    pack = 8 // n_new
    groups = rows // 8
    ch = 2048
    nchunk = past // ch

    def pack_rows(src_ref, cols, ncols):
        which = lax.broadcasted_iota(I32, (8, ncols), 0) // n_new
        out = []
        for p in range(groups):
            dense = src_ref[(p * pack) * 8:(p * pack) * 8 + 8, cols]
            for q in range(1, pack):
                piece = src_ref[(p * pack + q) * 8:(p * pack + q) * 8 + 8, cols]
                dense = jnp.where(which == q, pltpu.roll(piece, q * n_new, axis=0), dense)
            out.append(dense)
        return jnp.concatenate(out, axis=0)

    def unpack_rows(dst_ref, cols, dense):
        for p in range(groups):
            piece = dense[p * 8:(p + 1) * 8]
            for q in range(pack):
                rolled = piece if q == 0 else pltpu.roll(piece, 8 - q * n_new, axis=0)
                dst_ref[(p * pack + q) * 8:(p * pack + q) * 8 + 8, cols] = rolled

    def pack_chunk(c, carry):
        cols = pl.ds(pl.multiple_of(c * ch, ch), ch)
        sp_ref[:, cols] = pack_rows(in_p_ref, cols, ch)
        return carry

    lax.fori_loop(0, nchunk, pack_chunk, 0)
    t = lax.broadcasted_iota(I32, (rows, 128), 0) % n_new
    lane = lax.broadcasted_iota(I32, (rows, 128), 1)
    sn = jnp.where((lane < n_new) & (lane <= t), pack_rows(in_n_ref, slice(0, 128), 128), -jnp.inf)

    def fold(x):
        f = x[:, 0:128]
        for q in range(1, x.shape[1] // 128):
            f = f + x[:, q * 128:(q + 1) * 128]
        return f

    lane_ch = lax.broadcasted_iota(I32, (rows, ch), 1)

    def count_keys(pred):
        def cnt_chunk(c, acc):
            off = pl.multiple_of(c * ch, ch)
            return acc + fold(jnp.where(pred(sp_ref[:, pl.ds(off, ch)], lane_ch + c * ch), 1.0, 0.0))

        acc = lax.fori_loop(0, nchunk, cnt_chunk, jnp.where(pred(sn, lane + past), 1.0, 0.0))
        return jnp.sum(acc, axis=1, keepdims=True)

    def bit_body(b, carry):
        thr, cge = carry
        cand = thr + _bit_value(b)
        cand_f = _key_to_float(cand)
        cnt = count_keys(lambda s, kpos: s >= cand_f)
        ok = cnt >= float(TOPK_MAX)
        return jnp.where(ok, cand, thr), jnp.where(ok, cnt, cge)

    thr, cge = lax.fori_loop(0, 32, bit_body, (jnp.full((rows, 1), INT_MIN, I32),
                                               jnp.zeros((rows, 1), F32)))
    thr_f = _key_to_float(jnp.maximum(thr, KEY_LOWEST_FINITE))
    has_ties = jnp.max(cge) > float(TOPK_MAX)

    @pl.when(jnp.logical_not(has_ties))
    def _plain_mask():
        def to_bias(c, carry):
            cols = pl.ds(pl.multiple_of(c * ch, ch), ch)
            unpack_rows(out_p_ref, cols, jnp.where(sp_ref[:, cols] >= thr_f, 0.0, NEG))
            return carry

        lax.fori_loop(0, nchunk, to_bias, 0)
        unpack_rows(out_n_ref, slice(0, 128), jnp.where(sn >= thr_f, 0.0, NEG))

    @pl.when(has_ties)
    def _tie_mask():
        need = float(TOPK_MAX) - count_keys(lambda s, kpos: s > thr_f)
        nbits = (past + 128 - 1).bit_length()

        def pos_bit(b, last):
            step = lax.shift_left(jnp.int32(1), jnp.int32(nbits - 1) - b)
            probe = last + (step - 1)
            got = count_keys(lambda s, kpos: (s == thr_f) & (kpos <= probe))
            return jnp.where(got < need, last + step, last)

        last = lax.fori_loop(0, nbits, pos_bit, jnp.zeros((rows, 1), I32))

        def keep(s, kpos):
            return (s > thr_f) | ((s == thr_f) & (kpos <= last))

        def to_bias(c, carry):
            cols = pl.ds(pl.multiple_of(c * ch, ch), ch)
            unpack_rows(out_p_ref, cols, jnp.where(keep(sp_ref[:, cols], lane_ch + c * ch), 0.0, NEG))
            return carry

        lax.fori_loop(0, nchunk, to_bias, 0)
        unpack_rows(out_n_ref, slice(0, 128), jnp.where(keep(sn, lane + past), 0.0, NEG))


def _sample_attn_step(t, n_steps, slot, has_next, start_next, q_ref, bp_ref, bn_ref, knew_ref, vnew_ref, pbuf,
                      out_ref, s_ref, p_ref, l_ref, acc_ref):
    npg = pbuf.shape[1]
    nt = (((1,), (1,)), ((), ()))
    rg = q_ref.shape[1] // N_KV_HEADS
    n_tok = rg // GROUP
    width = npg * PAGE_SIZE
    past = n_steps * width
    sm_chunk = 2048

    def head_rows(g):
        rows = [pbuf[slot, r, pl.ds(g, PAGE_SIZE, stride=N_KV_HEADS), :] for r in range(npg)]
        return jnp.concatenate(rows, axis=0).astype(BF16)

    def per_head(mask8):
        return jnp.concatenate([_tile_rows(mask8, n_tok)] * (rg // 8), axis=0)

    def logits(ts):
        off = ts * width
        bias = per_head(bp_ref[0])
        for g in range(N_KV_HEADS):
            qg = q_ref[0, g * rg:(g + 1) * rg, :]
            s_ref[g * rg:(g + 1) * rg, off:off + width] = lax.dot_general(
                qg, head_rows(g), nt, preferred_element_type=F32) + bias

    def softmax():
        biasn = per_head(bn_ref[0])
        for g in range(N_KV_HEADS):
            qg = q_ref[0, g * rg:(g + 1) * rg, :]
            kg = knew_ref[0, :, g * HEAD_DIM:(g + 1) * HEAD_DIM]
            s_ref[g * rg:(g + 1) * rg, past:past + PAGE_SIZE] = lax.dot_general(
                qg, kg, nt, preferred_element_type=F32) + biasn

        def fold(x, op):
            f = x[:, 0:128]
            for q in range(1, x.shape[1] // 128):
                f = op(f, x[:, q * 128:(q + 1) * 128])
            return f

        def max_chunk(c, m):
            off = pl.multiple_of(c * sm_chunk, sm_chunk)
            return jnp.maximum(m, fold(s_ref[:, pl.ds(off, sm_chunk)], jnp.maximum))

        m = lax.fori_loop(0, past // sm_chunk, max_chunk, s_ref[:, past:past + PAGE_SIZE])
        m = jnp.max(m, axis=1, keepdims=True)

        def exp_chunk(c, l):
            off = pl.multiple_of(c * sm_chunk, sm_chunk)
            p = jnp.exp2(s_ref[:, pl.ds(off, sm_chunk)] - m)
            p_ref[:, pl.ds(off, sm_chunk)] = p.astype(BF16)
            return l + fold(p, jnp.add)

        pn = jnp.exp2(s_ref[:, past:past + PAGE_SIZE] - m)
        p_ref[:, past:past + PAGE_SIZE] = pn.astype(BF16)
        l_ref[...] = lax.fori_loop(0, past // sm_chunk, exp_chunk, pn)
        acc_ref[...] = jnp.zeros(acc_ref.shape, F32)

    def values(ts):
        off = (ts - n_steps) * width
        for g in range(N_KV_HEADS):
            rs = slice(g * rg, (g + 1) * rg)
            acc_ref[rs, :] += jnp.dot(p_ref[rs, off:off + width], head_rows(g),
                                      preferred_element_type=F32)

    def finish():
        for g in range(N_KV_HEADS):
            rs = slice(g * rg, (g + 1) * rg)
            vg = vnew_ref[0, :, g * HEAD_DIM:(g + 1) * HEAD_DIM]
            acc_ref[rs, :] += jnp.dot(p_ref[rs, past:past + PAGE_SIZE], vg, preferred_element_type=F32)
        out_ref[0] = acc_ref[...] / jnp.sum(l_ref[...], axis=1, keepdims=True)

    last = 2 * n_steps - 1

    def phase(ts, start):
        if start:
            start_next(ts + 1)
        if ts < n_steps:
            logits(ts)
            if ts == n_steps - 1:
                softmax()
        else:
            values(ts)
            if ts == last:
                finish()

    for ts in range(last):
        pl.when(t == ts)(functools.partial(phase, ts, True))
    pl.when((t == last) & has_next)(functools.partial(phase, last, True))
    pl.when((t == last) & jnp.logical_not(has_next))(functools.partial(phase, last, False))


def _sample_operands(qT, qiT, wT, kibf, kbf, v32, cache_kidx, n_seq, n_tok):
    def rows_ht(xT, n_heads, dim):
        x = xT.reshape(n_heads, dim, n_seq, n_tok).transpose(2, 0, 3, 1)
        return x.reshape(n_seq, n_heads * n_tok, dim)

    qi_rows = rows_ht(qiT, N_IDX_HEADS, IDX_DIM)
    q_rows = rows_ht(qT, N_HEADS, HEAD_DIM)
    wcol = wT.reshape(N_IDX_HEADS, n_seq, n_tok).transpose(1, 0, 2).reshape(n_seq, N_IDX_HEADS * n_tok, 1)

    def pad_keys(x):
        x = x.reshape(n_seq, n_tok, x.shape[-1])
        return jnp.pad(x, ((0, 0), (0, PAGE_SIZE - n_tok), (0, 0)))

    ki_new_t = jnp.swapaxes(pad_keys(kibf[:, 0:IDX_DIM]), 1, 2)
    kidx_t = jnp.swapaxes(cache_kidx, 1, 2)
    k_new = pad_keys(kbf)
    v_new = pad_keys(v32.reshape(n_seq * n_tok, N_KV).astype(BF16))
    return (qi_rows, wcol, ki_new_t, kidx_t), (q_rows, k_new, v_new)


def _sample_topk(sp, sn, n_tok):
    n_seq, _, past = sp.shape
    rows = n_seq * 8
    assert 8 % n_tok == 0 and n_seq % (8 // n_tok) == 0
    bp, bn = pl.pallas_call(
        functools.partial(_s_thresh_body, n_tok),
        grid=(1,),
        in_specs=[pl.BlockSpec((rows, past), lambda i: (0, 0)),
                  pl.BlockSpec((rows, PAGE_SIZE), lambda i: (0, 0))],
        out_specs=[pl.BlockSpec((rows, past), lambda i: (0, 0)),
                   pl.BlockSpec((rows, PAGE_SIZE), lambda i: (0, 0))],
        out_shape=[jax.ShapeDtypeStruct((rows, past), F32),
                   jax.ShapeDtypeStruct((rows, PAGE_SIZE), F32)],
        scratch_shapes=[pltpu.VMEM((n_seq * n_tok, past), F32)],
        compiler_params=_cparams(1),
        name="sample_topk_mask",
    )(sp.reshape(rows, past), sn.reshape(rows, PAGE_SIZE))
    return bp.reshape(n_seq, 8, past), bn.reshape(n_seq, 8, PAGE_SIZE)


def _sample_rows_to_tokens(out, n_seq, n_tok):
    out = out.reshape(n_seq, N_HEADS, n_tok, HEAD_DIM)
    return out.transpose(0, 2, 1, 3).reshape(n_seq * n_tok, D_ATTN)


def _outproj_body(gated, *refs):
    if gated:
        x_ref, a_ref, gate_ref, mc_ref, wo_ref, o_ref, wo_o = refs
        ma = (a_ref[...] * gate_ref[...]).astype(BF16)
        wo_o[...] = wo_ref[...].astype(BF16)
        wo_ref = wo_o
    else:
        x_ref, a_ref, mc_ref, wo_ref, o_ref = refs
        ma = a_ref[...]
    acc = jnp.dot(ma, wo_ref[0:D_ATTN, :], preferred_element_type=F32)
    acc = acc + jnp.dot(mc_ref[...], wo_ref[D_ATTN:D_ATTN + D_CONV, :], preferred_element_type=F32)
    o_ref[...] = x_ref[...] + acc


def _outproj(x, attn, gate, mixc, wo, tm):
    m = x.shape[0]
    gated = gate is not None
    assert not gated or m == tm
    row = lambda i: (i, 0)
    in_specs = [pl.BlockSpec((tm, D_MODEL), row), pl.BlockSpec((tm, D_ATTN), row)]
    args = [x, attn]
    if gated:
        in_specs.append(pl.BlockSpec((tm, D_ATTN), row))
        args.append(gate)
    in_specs += [pl.BlockSpec((tm, D_CONV), row),
                 pl.BlockSpec((D_ATTN + D_CONV, D_MODEL), lambda i: (0, 0))]
    args += [mixc, wo]
    out_specs = [pl.BlockSpec((tm, D_MODEL), row)]
    out_shape = [jax.ShapeDtypeStruct((m, D_MODEL), F32)]
    if gated:
        out_specs.append(pl.BlockSpec((D_ATTN + D_CONV, D_MODEL), lambda i: (0, 0)))
        out_shape.append(jax.ShapeDtypeStruct((D_ATTN + D_CONV, D_MODEL), BF16))
    out = pl.pallas_call(
        functools.partial(_outproj_body, gated),
        grid=(m // tm,),
        in_specs=in_specs,
        out_specs=out_specs,
        out_shape=out_shape,
        compiler_params=_cparams(1),
        name="outproj_sample" if gated else "outproj_prompt",
    )(*args)
    return out if gated else out[0]


def _rope_tables(pos):
    posf = np.asarray(pos, np.float64)[:, None]
    n = posf.shape[0]

    def cs(half):
        inv = ROPE_THETA ** (-np.arange(half, dtype=np.float64) / half)
        ang = posf * inv[None, :]
        return np.cos(ang), np.sin(ang)

    c16, s16 = cs(ROPE_HALF)
    c8, s8 = cs(IDX_ROPE_HALF)
    one = lambda w: np.ones((n, w))
    zero = lambda w: np.zeros((n, w))
    rest = HEAD_DIM - ROPE_DIM
    k_c = np.concatenate([c16, c16, one(rest)], axis=1)
    k_sa = np.concatenate([-s16, zero(HEAD_DIM - ROPE_HALF)], axis=1)
    k_sb = np.concatenate([zero(ROPE_HALF), s16, zero(rest)], axis=1)
    tabs = dict(c16T=c16.T, s16T=s16.T, c8T=c8.T, s8T=s8.T, kC=k_c, kSa=k_sa, kSb=k_sb)
    return {name: jnp.asarray(np.ascontiguousarray(t), F32) for name, t in tabs.items()}


def _prep_weights(w_in, g_q, g_k, g_kidx, w_conv, w_out):
    assert w_in.shape == (D_MODEL, D_IN)
    wT = w_in.T
    gq = g_q.reshape(HEAD_DIM, 1)
    gk = g_k.reshape(1, HEAD_DIM)
    gki = g_kidx.reshape(IDX_DIM, 1)
    return wT, (gq, gk, gki, w_conv), w_out


def kernel(x_prompt, x_sample, cache_k, cache_v, cache_kidx, state_conv, page_table,
           norm_in, w_in, g_q, g_k, g_kidx, w_conv, w_out):
    n_b, seq, _ = x_prompt.shape
    n_s, n_t, _ = x_sample.shape
    depth = w_in.shape[0]
    past = page_table.shape[1] * PAGE_SIZE
    tabs_p = _rope_tables(np.arange(seq))
    tabs_s = _rope_tables(np.tile(past + np.arange(n_t), n_s))

    hp = x_prompt.reshape(n_b * seq, D_MODEL)
    hs = x_sample.reshape(n_s * n_t, D_MODEL)
    outs = [[] for _ in range(8)]
    for l in range(depth):
        w_t, params, wo = _prep_weights(w_in[l], g_q[l], g_k[l], g_kidx[l], w_conv[l], w_out[l])

        st = state_conv[l]
        tok = jnp.arange(n_t)
        e1 = st[:, jnp.full((n_t,), CONV_W - 2)].reshape(n_s * n_t, D_CONV)
        e2 = st[:, jnp.minimum(tok, CONV_W - 2)].reshape(n_s * n_t, D_CONV)
        (qT_s, qiT_s, _, wT_s, k32_s, kbf_s, v32_s, ki32_s, kibf_s, gate_s, mixc_s, u, *w_bf) = _project(
            hs, norm_in[l], w_t, params, tabs_s, n_t, state_rows=(e1, e2))
        idx_ops, (q_rows, k_new, v_new) = _sample_operands(
            qT_s, qiT_s, wT_s, kibf_s, kbf_s, v32_s, cache_kidx[l], n_s, n_t)
        (qT, qiT, vT, wT, k32, kbf, v32, ki32, kibf, gate, mixc, utail, sp, sn) = _project(
            hp, norm_in[l], tuple(w_bf), params, tabs_p, 0, side=(page_table,) + idx_ops)

        bp, bn = _sample_topk(sp, sn, n_t)
        mixa, attn_rows = _attention(qiT, wT, kibf, qT, kbf, vT, gate, n_b, seq,
                                     q_rows, bp, bn, k_new, v_new, cache_k[l], cache_v[l], page_table)
        attn_s = _sample_rows_to_tokens(attn_rows, n_s, n_t)

        hs, wo_bf = _outproj(hs, attn_s, gate_s, mixc_s, wo, n_s * n_t)
        hp = _outproj(hp, mixa, None, mixc, wo_bf, ROW_TM)
        tps = seq // PROJ_TM
        outs[0].append(k32.reshape(n_b, seq, N_KV_HEADS, HEAD_DIM))
        outs[1].append(v32.reshape(n_b, seq, N_KV_HEADS, HEAD_DIM))
        outs[2].append(ki32.reshape(n_b, seq, IDX_DIM))
        outs[3].append(utail[tps - 1::tps])
        outs[4].append(k32_s.reshape(n_s, n_t, N_KV_HEADS, HEAD_DIM))
        outs[5].append(v32_s.reshape(n_s, n_t, N_KV_HEADS, HEAD_DIM))
        outs[6].append(ki32_s.reshape(n_s, n_t, IDX_DIM))
        outs[7].append(u.reshape(n_s, n_t, D_CONV)[:, n_t - (CONV_W - 1):])

    return (hp.reshape(n_b, seq, D_MODEL), hs.reshape(n_s, n_t, D_MODEL),
            *[jnp.stack(o) for o in outs])
```

```python
import functools

import jax
import jax.numpy as jnp
import numpy as np
from jax import lax
from jax.experimental import pallas as pl
from jax.experimental.pallas import tpu as pltpu

F32 = jnp.float32
BF16 = jnp.bfloat16
I32 = jnp.int32

D_MODEL = 2048
HEAD_DIM = 128
N_HEADS = 8
N_KV_HEADS = 2
GROUP = N_HEADS // N_KV_HEADS
D_ATTN = N_HEADS * HEAD_DIM
D_CONV = 1024
ROPE_DIM = HEAD_DIM // 4
ROPE_HALF = ROPE_DIM // 2
ROPE_THETA = 500000.0
N_IDX_HEADS = 16
IDX_DIM = 64
IDX_ROPE_HALF = IDX_DIM // 8
TOPK_MAX = 256
CONV_W = 3
PAGE_SIZE = 128
EPS = 1e-6
W_IDX_SCALE = (N_IDX_HEADS ** -0.5) * (IDX_DIM ** -0.5)
ATTN_SCALE = HEAD_DIM ** -0.5
Q_PRESCALE = ATTN_SCALE * 1.4426950408889634

INT_MIN = -(2 ** 31)
NEG = -1e30

VMEM_LIMIT_BYTES = 60 * 1024 * 1024

PROJ_TN = 512
PROJ_TM = 1024
PROJ_SUB = 256
ROW_TM = 512
N_KV = N_KV_HEADS * HEAD_DIM
OFF_Q = 0
OFF_K = OFF_Q + D_ATTN
OFF_V = OFF_K + N_KV
OFF_Z = OFF_V + N_KV
OFF_QI = OFF_Z + D_ATTN
OFF_KI = OFF_QI + N_IDX_HEADS * IDX_DIM
OFF_WI = OFF_KI + IDX_DIM
OFF_H = OFF_WI + N_IDX_HEADS
OFF_B = OFF_H + D_CONV
OFF_C = OFF_B + D_CONV
OFF_ZC = OFF_C + D_CONV
D_IN = OFF_ZC + D_CONV
assert OFF_K == 2 * PROJ_TN and OFF_Z == 3 * PROJ_TN and OFF_QI == 5 * PROJ_TN and OFF_KI == 7 * PROJ_TN
KIW_ROWS = 128
assert OFF_KI % KIW_ROWS == 0 and OFF_WI - OFF_KI == IDX_DIM
CONV_CHUNK = 256
N_CONV_CHUNKS = D_CONV // CONV_CHUNK
J_Q = 0
J_KV = 2
J_Z = 3
J_QI = 5
J_CONV = 7
N_MAIN_BLOCKS = J_CONV
N_STEPS = J_CONV + N_CONV_CHUNKS

ATT_TQ = 256
ATT_TK = 256
CNT_ACCS = 4

SIDE_PAGES = 16


def _cparams(n_axes):
    return pltpu.CompilerParams(
        dimension_semantics=("arbitrary",) * n_axes,
        vmem_limit_bytes=VMEM_LIMIT_BYTES,
    )


def _silu(x):
    return x * jax.nn.sigmoid(x)


def _tile_rows(x8, period):
    row = lax.broadcasted_iota(I32, x8.shape, 0)
    out = x8
    for q in range(1, 8 // period):
        out = jnp.where(row // period == q, pltpu.roll(x8, q * period, axis=0), out)
    return out


def _indexer_rows_scores(qi, wcol, keys_t):
    n_tok = qi.shape[0] // N_IDX_HEADS
    d = jnp.dot(qi, keys_t, preferred_element_type=F32)
    val = jnp.maximum(d, 0.0) * wcol
    v = val.reshape(val.shape[0] // 8, 8, val.shape[-1]).sum(axis=0)
    out = v
    for q in range(1, 8 // n_tok):
        out = out + pltpu.roll(v, q * n_tok, axis=0)
    return out


class _SideScores:
    def __init__(self, step, n_total, spq, pt_ref, qi_ref, wcol_ref, kinew_ref, kidx_hbm, sp_ref, sn_ref,
                 kbuf, ksem):
        self.step, self.n_total, self.spq, self.pt_ref = step, n_total, spq, pt_ref
        self.qi_ref, self.wcol_ref, self.kinew_ref, self.kidx_hbm = qi_ref, wcol_ref, kinew_ref, kidx_hbm
        self.sp_ref, self.sn_ref, self.kbuf, self.ksem = sp_ref, sn_ref, kbuf, ksem
        self.npg = kbuf.shape[1]
        self.n_side = pt_ref.shape[0] * spq

    def _start(self, n):
        c = jnp.minimum(n, self.n_side - 1)
        seq_n = c // self.spq
        base = (c % self.spq) * self.npg
        for r in range(self.npg):
            _page_copy(self.kidx_hbm, self.pt_ref[seq_n, base + r], self.kbuf, n % 2, r, self.ksem).start()

    def prologue(self):
        pl.when(self.step == 0)(lambda: self._start(self.step))

    def __call__(self):
        pl.when(self.step + 1 < self.n_total)(lambda: self._start(self.step + 1))
        slot = self.step % 2
        pltpu.make_async_copy(self.kidx_hbm.at[pl.ds(0, self.npg)], self.kbuf.at[slot], self.ksem.at[slot]).wait()
        qi = self.qi_ref[0]
        wcol = self.wcol_ref[0]
        keys_t = jnp.concatenate([self.kbuf[slot, r] for r in range(self.npg)], axis=1).astype(BF16)
        self.sp_ref[0] = _indexer_rows_scores(qi, wcol, keys_t)
        self.sn_ref[0] = _indexer_rows_scores(qi, wcol, self.kinew_ref[0])


class _Bf16View:
    def __init__(self, ref):
        self.ref = ref

    def __getitem__(self, idx):
        return self.ref[idx].astype(BF16)


def _proj_body(sample, tps, tm, side_spq, convert, *refs):
    if side_spq:
        pt_ref, refs = refs[0], refs[1:]
    (x_ref, gin_ref, wa_ref, wkiw_ref, wh_ref, wb_ref, wc_ref, wzc_ref, gq_ref, gk_ref, gki_ref,
     c16_ref, s16_ref, c8_ref, s8_ref,
     kc_ref, ksa_ref, ksb_ref, wconv_ref) = refs[:19]
    refs = refs[19:]
    if sample:
        e1_ref, e2_ref = refs[:2]
        refs = refs[2:]
    if side_spq:
        side_in, refs = refs[:4], refs[4:]
    (qT_ref, qiT_ref, vT_ref, wT_ref, k32_ref, kbf_ref, v32_ref, ki32_ref, kibf_ref,
     gate_ref, mixc_ref, u_ref) = refs[:12]
    refs = refs[12:]
    if convert:
        w_out, refs = refs[:6], refs[6:]
    if side_spq:
        side_out, refs = refs[:2], refs[2:]
    xn_ref = refs[0]
    if not sample:
        carry_ref = refs[1]

    i = pl.program_id(0)
    j = pl.program_id(1)
    nt = (((1,), (1,)), ((), ()))
    if side_spq:
        side = _SideScores(i * N_STEPS + j, pl.num_programs(0) * N_STEPS, side_spq, pt_ref,
                           *side_in, *side_out, *refs[2:4])
        side.prologue()
    else:
        side = lambda: None

    if convert:
        wa_o, kiw_o, wh_o, wb_o, wc_o, wzc_o = w_out

        @pl.when(j < N_MAIN_BLOCKS)
        def _():
            wa_o[...] = wa_ref[...].astype(BF16)

        @pl.when(j == J_KV)
        def _():
            kiw_o[...] = wkiw_ref[...].astype(BF16)

        @pl.when(j >= J_CONV)
        def _():
            for src, dst in ((wh_ref, wh_o), (wb_ref, wb_o), (wc_ref, wc_o), (wzc_ref, wzc_o)):
                dst[...] = src[...].astype(BF16)

        wa_ref, wkiw_ref, wh_ref, wb_ref, wc_ref, wzc_ref = (
            _Bf16View(r) for r in (wa_ref, wkiw_ref, wh_ref, wb_ref, wc_ref, wzc_ref))

    def xw(w, x=None):
        return lax.dot_general(xn_ref[...] if x is None else x, w, nt, preferred_element_type=F32)

    def wx(w, x):
        return lax.dot_general(w, x, nt, preferred_element_type=F32)

    nsub = max(tm // PROJ_SUB, 1)
    sub = tm // nsub
    pieces = [slice(rb * sub, (rb + 1) * sub) for rb in range(nsub)]

    if not sample:
        @pl.when((i == 0) & (j == 0))
        def _init():
            carry_ref[...] = jnp.zeros(carry_ref.shape, F32)

    def q_step(with_norm):
        side()
        for cs in pieces:
            if with_norm:
                x = x_ref[cs, :]
                ms = jnp.mean(x * x, axis=-1, keepdims=True)
                xn_ref[cs, :] = (x * lax.rsqrt(ms + EPS) * gin_ref[...]).astype(BF16)
            res = wx(wa_ref[...], xn_ref[cs, :])
            cos = c16_ref[:, cs]
            sin = s16_ref[:, cs]
            for hh in range(PROJ_TN // HEAD_DIM):
                blk = res[hh * HEAD_DIM:(hh + 1) * HEAD_DIM]
                ms = jnp.mean(blk * blk, axis=0, keepdims=True)
                y = blk * lax.rsqrt(ms + EPS) * gq_ref[...]
                x1 = y[0:ROPE_HALF]
                x2 = y[ROPE_HALF:ROPE_DIM]
                base = hh * HEAD_DIM
                qT_ref[base:base + ROPE_HALF, cs] = ((x1 * cos - x2 * sin) * Q_PRESCALE).astype(BF16)
                qT_ref[base + ROPE_HALF:base + ROPE_DIM, cs] = ((x2 * cos + x1 * sin) * Q_PRESCALE).astype(BF16)
                qT_ref[base + ROPE_DIM:base + HEAD_DIM, cs] = (y[ROPE_DIM:] * Q_PRESCALE).astype(BF16)

    pl.when(j == J_Q)(functools.partial(q_step, True))
    pl.when((j > J_Q) & (j < J_KV))(functools.partial(q_step, False))

    @pl.when((j >= J_QI) & (j < J_CONV))
    def _qi():
        side()
        for cs in pieces:
            res = wx(wa_ref[...], xn_ref[cs, :])
            cos = c8_ref[:, cs]
            sin = s8_ref[:, cs]
            for hh in range(PROJ_TN // IDX_DIM):
                blk = res[hh * IDX_DIM:(hh + 1) * IDX_DIM]
                x1 = blk[0:IDX_ROPE_HALF]
                x2 = blk[IDX_ROPE_HALF:2 * IDX_ROPE_HALF]
                rot = jnp.concatenate([x1 * cos - x2 * sin, x2 * cos + x1 * sin], axis=0)
                base = hh * IDX_DIM
                qiT_ref[base:base + 2 * IDX_ROPE_HALF, cs] = rot.astype(BF16)
                qiT_ref[base + 2 * IDX_ROPE_HALF:base + IDX_DIM, cs] = blk[2 * IDX_ROPE_HALF:].astype(BF16)

    @pl.when(j == J_KV)
    def _kv():
        side()
        for rb, cs in enumerate(pieces):
            x = xn_ref[cs, :]
            res = xw(wa_ref[...], x)
            for hd in range(N_KV_HEADS):
                hs = slice(hd * HEAD_DIM, (hd + 1) * HEAD_DIM)
                blk = res[:, hs]
                ms = jnp.mean(blk * blk, axis=-1, keepdims=True)
                yk = blk * lax.rsqrt(ms + EPS) * gk_ref[...]
                rot = yk * kc_ref[cs, :] + (pltpu.roll(yk, HEAD_DIM - ROPE_HALF, axis=1) * ksa_ref[cs, :]
                                            + pltpu.roll(yk, ROPE_HALF, axis=1) * ksb_ref[cs, :])
                k32_ref[pl.ds(N_KV_HEADS * rb * sub + hd, sub, stride=N_KV_HEADS), :] = rot
                kbf_ref[cs, hs] = rot.astype(BF16)
            v = res[:, N_KV:2 * N_KV]
            for hd in range(N_KV_HEADS):
                v32_ref[pl.ds(N_KV_HEADS * rb * sub + hd, sub, stride=N_KV_HEADS), :] = (
                    v[:, hd * HEAD_DIM:(hd + 1) * HEAD_DIM])
            if sample:
                vT_ref[:, cs] = wx(wa_ref[N_KV:2 * N_KV, :], x).astype(BF16)
            else:
                vT_ref[:, cs] = v.T.astype(BF16)
            r2t = wx(wkiw_ref[...], x)
            wT_ref[:, cs] = r2t[IDX_DIM:IDX_DIM + N_IDX_HEADS] * W_IDX_SCALE
            kit = r2t[0:IDX_DIM]
            ms = jnp.mean(kit * kit, axis=0, keepdims=True)
            yi = kit * lax.rsqrt(ms + EPS) * gki_ref[...]
            x1 = yi[0:IDX_ROPE_HALF]
            x2 = yi[IDX_ROPE_HALF:2 * IDX_ROPE_HALF]
            cos = c8_ref[:, cs]
            sin = s8_ref[:, cs]
            roti = jnp.concatenate([x1 * cos - x2 * sin, x2 * cos + x1 * sin, yi[2 * IDX_ROPE_HALF:],
                                    jnp.zeros((KIW_ROWS - IDX_DIM, sub), F32)], axis=0)
            ki_nat = roti.T
            ki32_ref[cs, :] = ki_nat[:, 0:IDX_DIM]
            kibf_ref[cs, :] = ki_nat.astype(BF16)

    @pl.when((j >= J_Z) & (j < J_QI))
    def _z():
        side()
        for cs in pieces:
            gate_ref[cs, :] = _silu(xw(wa_ref[...], xn_ref[cs, :]))

    @pl.when(j >= J_CONV)
    def _conv():
        side()
        cc = j - J_CONV
        u = xw(wc_ref[...]) * xw(wh_ref[...])
        rowid = lax.broadcasted_iota(I32, (tm, CONV_CHUNK), 0)
        if sample:
            t = rowid & (sample - 1)
            u1 = jnp.where(t >= 1, pltpu.roll(u, 1, axis=0), e1_ref[...])
            u2 = jnp.where(t >= 2, pltpu.roll(u, 2, axis=0), e2_ref[...])
            u_ref[...] = u
        else:
            first = (i % tps) == 0
            prev = carry_ref[cc]
            p0 = jnp.where(first, 0.0, prev[0:1])
            p1 = jnp.where(first, 0.0, prev[1:2])
            u1 = jnp.where(rowid == 0, p1, pltpu.roll(u, 1, axis=0))
            u2 = jnp.where(rowid == 0, p0, jnp.where(rowid == 1, p1, pltpu.roll(u, 2, axis=0)))
            tail = u[tm - 8:tm]
            carry_ref[cc] = jnp.concatenate([tail[6:8], tail[0:6]], axis=0)
            u_ref[0] = tail[6:8]
        w = wconv_ref[...]
        y = u2 * w[0:1] + u1 * w[1:2] + u * w[2:3]
        mixc_ref[...] = (xw(wb_ref[...]) * y * _silu(xw(wzc_ref[...]))).astype(BF16)


def _project(x, norm_g, wts, params, tabs, sample, state_rows=None, side=None):
    m = x.shape[0]
    convert = not isinstance(wts, tuple)
    assert sample & (sample - 1) == 0
    tm = m if sample else PROJ_TM
    n_i = m // tm
    tps = 1 if sample else (tabs["c16T"].shape[1] // tm)
    gq, gk, gki, wconv = params

    def tmap(i):
        return i % tps

    def cchunk(j):
        return jnp.clip(j - J_CONV, 0, N_CONV_CHUNKS - 1)

    main_spec = pl.BlockSpec((PROJ_TN, D_MODEL), lambda i, j, *_: (jnp.minimum(j, N_MAIN_BLOCKS - 1), 0))
    if convert:
        def conv_rows(off):
            return pl.BlockSpec((pl.Element(CONV_CHUNK), pl.Element(D_MODEL)),
                                lambda i, j, *_: (pl.multiple_of(off + CONV_CHUNK * cchunk(j), 16), 0))

        w_specs = [main_spec, pl.BlockSpec((KIW_ROWS, D_MODEL), lambda i, j, *_: (OFF_KI // KIW_ROWS, 0)),
                   conv_rows(OFF_H), conv_rows(OFF_B), conv_rows(OFF_C), conv_rows(OFF_ZC)]
        w_args = [wts] * 6
    else:
        conv_spec = pl.BlockSpec(
            (CONV_CHUNK, D_MODEL),
            lambda i, j, *_: (jnp.where(j < J_QI, N_CONV_CHUNKS - 1, cchunk(j)), 0))
        w_specs = [main_spec, pl.BlockSpec((KIW_ROWS, D_MODEL), lambda i, j, *_: (0, 0))] + [conv_spec] * 4
        w_args = list(wts)

    in_specs = [
        pl.BlockSpec((tm, D_MODEL), lambda i, j, *_: (i, 0)),
        pl.BlockSpec((1, D_MODEL), lambda i, j, *_: (0, 0)),
        *w_specs,
        pl.BlockSpec((HEAD_DIM, 1), lambda i, j, *_: (0, 0)),
        pl.BlockSpec((1, HEAD_DIM), lambda i, j, *_: (0, 0)),
        pl.BlockSpec((IDX_DIM, 1), lambda i, j, *_: (0, 0)),
        pl.BlockSpec((ROPE_HALF, tm), lambda i, j, *_: (0, tmap(i))),
        pl.BlockSpec((ROPE_HALF, tm), lambda i, j, *_: (0, tmap(i))),
        pl.BlockSpec((IDX_ROPE_HALF, tm), lambda i, j, *_: (0, tmap(i))),
        pl.BlockSpec((IDX_ROPE_HALF, tm), lambda i, j, *_: (0, tmap(i))),
        pl.BlockSpec((tm, HEAD_DIM), lambda i, j, *_: (tmap(i), 0)),
        pl.BlockSpec((tm, HEAD_DIM), lambda i, j, *_: (tmap(i), 0)),
        pl.BlockSpec((tm, HEAD_DIM), lambda i, j, *_: (tmap(i), 0)),
        pl.BlockSpec((CONV_W, CONV_CHUNK), lambda i, j, *_: (0, cchunk(j))),
    ]
    args = [x, norm_g.reshape(1, D_MODEL), *w_args, gq, gk, gki,
            tabs["c16T"], tabs["s16T"], tabs["c8T"], tabs["s8T"],
            tabs["kC"], tabs["kSa"], tabs["kSb"], wconv]
    cmap = lambda i, j, *_: (i, cchunk(j))
    if sample:
        in_specs += [pl.BlockSpec((tm, CONV_CHUNK), cmap), pl.BlockSpec((tm, CONV_CHUNK), cmap)]
        args += list(state_rows)
        u_spec = pl.BlockSpec((tm, CONV_CHUNK), cmap)
        u_shape = jax.ShapeDtypeStruct((m, D_CONV), F32)
        scratch = [pltpu.VMEM((tm, D_MODEL), BF16)]
    else:
        u_spec = pl.BlockSpec((1, CONV_W - 1, CONV_CHUNK), lambda i, j, *_: (i, 0, cchunk(j)))
        u_shape = jax.ShapeDtypeStruct((n_i, CONV_W - 1, D_CONV), F32)
        scratch = [pltpu.VMEM((tm, D_MODEL), BF16), pltpu.VMEM((N_CONV_CHUNKS, 8, CONV_CHUNK), F32)]

    out_specs = [
        pl.BlockSpec((PROJ_TN, tm), lambda i, j, *_: (jnp.minimum(j, 1), i)),
        pl.BlockSpec((PROJ_TN, tm), lambda i, j, *_: (jnp.clip(j - J_QI, 0, 1), i)),
        pl.BlockSpec((N_KV, tm), lambda i, j, *_: (0, i)),
        pl.BlockSpec((N_IDX_HEADS, tm), lambda i, j, *_: (0, i)),
        pl.BlockSpec((N_KV_HEADS * tm, HEAD_DIM), lambda i, j, *_: (i, 0)),
        pl.BlockSpec((tm, N_KV), lambda i, j, *_: (i, 0)),
        pl.BlockSpec((N_KV_HEADS * tm, HEAD_DIM), lambda i, j, *_: (i, 0)),
        pl.BlockSpec((tm, IDX_DIM), lambda i, j, *_: (i, 0)),
        pl.BlockSpec((tm, KIW_ROWS), lambda i, j, *_: (i, 0)),
        pl.BlockSpec((tm, PROJ_TN), lambda i, j, *_: (i, jnp.clip(j - J_Z, 0, 1))),
        pl.BlockSpec((tm, CONV_CHUNK), cmap),
        u_spec,
    ]
    out_shape = [
        jax.ShapeDtypeStruct((D_ATTN, m), BF16),
        jax.ShapeDtypeStruct((N_IDX_HEADS * IDX_DIM, m), BF16),
        jax.ShapeDtypeStruct((N_KV, m), BF16),
        jax.ShapeDtypeStruct((N_IDX_HEADS, m), F32),
        jax.ShapeDtypeStruct((N_KV_HEADS * m, HEAD_DIM), F32),
        jax.ShapeDtypeStruct((m, N_KV), BF16),
        jax.ShapeDtypeStruct((N_KV_HEADS * m, HEAD_DIM), F32),
        jax.ShapeDtypeStruct((m, IDX_DIM), F32),
        jax.ShapeDtypeStruct((m, KIW_ROWS), BF16),
        jax.ShapeDtypeStruct((m, D_ATTN), F32),
        jax.ShapeDtypeStruct((m, D_CONV), BF16),
        u_shape,
    ]
    if convert:
        assert n_i == 1, "every weight block must be visited exactly once to be written back"
        out_specs += [
            main_spec,
            pl.BlockSpec((KIW_ROWS, D_MODEL), lambda i, j, *_: (0, 0)),
        ] + [pl.BlockSpec((CONV_CHUNK, D_MODEL), lambda i, j, *_: (cchunk(j), 0))] * 4
        out_shape += [jax.ShapeDtypeStruct((OFF_KI, D_MODEL), BF16),
                      jax.ShapeDtypeStruct((KIW_ROWS, D_MODEL), BF16)]
        out_shape += [jax.ShapeDtypeStruct((D_CONV, D_MODEL), BF16)] * 4
    if side is None:
        return pl.pallas_call(
            functools.partial(_proj_body, sample, tps, tm, 0, convert),
            grid=(n_i, N_STEPS),
            in_specs=in_specs,
            out_specs=out_specs,
            out_shape=out_shape,
            scratch_shapes=scratch,
            compiler_params=_cparams(2),
            name="proj_sample" if sample else "proj_prompt",
        )(*args)

    page_table, qi_rows, wcol, ki_new_t, kidx_t = side
    n_seq, n_pages = page_table.shape
    spq = n_pages // SIDE_PAGES
    n_side = n_seq * spq
    assert spq * SIDE_PAGES == n_pages and n_side <= n_i * N_STEPS
    width = SIDE_PAGES * PAGE_SIZE

    def side_step(i, j):
        return jnp.minimum(i * N_STEPS + j, n_side - 1)

    seq_map = lambda i, j, *_: (side_step(i, j) // spq, 0, 0)
    in_specs += [
        pl.BlockSpec((1,) + qi_rows.shape[1:], seq_map),
        pl.BlockSpec((1,) + wcol.shape[1:], seq_map),
        pl.BlockSpec((1, IDX_DIM, PAGE_SIZE), seq_map),
        pl.BlockSpec(memory_space=pl.ANY),
    ]
    out_specs += [
        pl.BlockSpec((1, 8, width), lambda i, j, *_: (side_step(i, j) // spq, 0, side_step(i, j) % spq)),
        pl.BlockSpec((1, 8, PAGE_SIZE), seq_map),
    ]
    out_shape += [jax.ShapeDtypeStruct((n_seq, 8, n_pages * PAGE_SIZE), F32),
                  jax.ShapeDtypeStruct((n_seq, 8, PAGE_SIZE), F32)]
    scratch += [pltpu.VMEM((2, SIDE_PAGES, IDX_DIM, PAGE_SIZE), F32), pltpu.SemaphoreType.DMA((2,))]
    return pl.pallas_call(
        functools.partial(_proj_body, sample, tps, tm, spq, convert),
        grid_spec=pltpu.PrefetchScalarGridSpec(
            num_scalar_prefetch=1,
            grid=(n_i, N_STEPS),
            in_specs=in_specs,
            out_specs=out_specs,
            scratch_shapes=scratch,
        ),
        out_shape=out_shape,
        compiler_params=_cparams(2),
        name="proj_prompt",
    )(page_table, *args, qi_rows, wcol, ki_new_t, kidx_t)


KEY_LOWEST_FINITE = INT_MIN + 0x00800000


def _key_to_float(key):
    return pltpu.bitcast(key ^ ((key >> 31) & 0x7FFFFFFF), F32)


def _bit_value(b):
    return lax.shift_left(jnp.int32(1), jnp.int32(31) - b)


def _prompt_attn_step(i, qiT_ref, wT_ref, ki_ref, qT_ref, k_ref, vT_ref, gate_ref, o_ref,
                      sc_ref, bias_ref, acc_ref, s_ref):
    tq, tk = ATT_TQ, ATT_TK
    nch = i + 1
    w = wT_ref[...]
    row = lax.broadcasted_iota(I32, (tk, tq), 0)
    col = lax.broadcasted_iota(I32, (tk, tq), 1)
    row8 = lax.broadcasted_iota(I32, (8, tq), 0)

    def score_chunk(j, carry):
        off = pl.multiple_of(j * tk, tk)
        kic = ki_ref[pl.ds(off, tk), 0:IDX_DIM]
        acc = jnp.zeros((tk, tq), F32)
        for h in range(N_IDX_HEADS):
            d = jnp.dot(kic, qiT_ref[h * IDX_DIM:(h + 1) * IDX_DIM, :], preferred_element_type=F32)
            acc = acc + w[h:h + 1, :] * jnp.maximum(d, 0.0)
        future = (row + j * tk) > (col + i * tq)
        sc_ref[pl.ds(off, tk), :] = jnp.where(future, -jnp.inf, acc)
        return carry

    lax.fori_loop(0, nch, score_chunk, 0)

    def count_keys(pred):
        def cnt_chunk(j, cs):
            off = pl.multiple_of(j * tk, tk)
            cs = list(cs)
            sc = sc_ref[pl.ds(off, tk), :]
            for r in range(tk // 8):
                a = cs[r % CNT_ACCS]
                kpos = row8 + (j * tk + r * 8)
                cs[r % CNT_ACCS] = jnp.where(pred(sc[r * 8:(r + 1) * 8], kpos), a + 1, a)
            return tuple(cs)

        cs = lax.fori_loop(0, nch, cnt_chunk, (jnp.zeros((8, tq), I32),) * CNT_ACCS)
        c = cs[0]
        for a in cs[1:]:
            c = c + a
        return jnp.sum(c.astype(F32), axis=0, keepdims=True)

    def bit_body(b, carry):
        thr, cge = carry
        cand = thr + _bit_value(b)
        cand_f = _key_to_float(cand)
        cnt = count_keys(lambda s, kpos: s >= cand_f)
        ok = cnt >= float(TOPK_MAX)
        return jnp.where(ok, cand, thr), jnp.where(ok, cnt, cge)

    n_bits = jnp.where(nch * tk > TOPK_MAX, 32, 0)
    thr, cge = lax.fori_loop(0, n_bits, bit_body, (jnp.full((1, tq), INT_MIN, I32),
                                                   jnp.zeros((1, tq), F32)))
    thr_f = _key_to_float(jnp.maximum(thr, KEY_LOWEST_FINITE))
    has_ties = jnp.max(cge) > float(TOPK_MAX)

    @pl.when(jnp.logical_not(has_ties))
    def _plain_mask():
        def bias_chunk(j, carry):
            off = pl.multiple_of(j * tk, tk)
            bias_ref[pl.ds(off, tk), :] = jnp.where(sc_ref[pl.ds(off, tk), :] >= thr_f, 0.0, NEG)
            return carry

        lax.fori_loop(0, nch, bias_chunk, 0)

    @pl.when(has_ties)
    def _tie_mask():
        need = float(TOPK_MAX) - count_keys(lambda s, kpos: s > thr_f)
        nbits = (sc_ref.shape[0] - 1).bit_length()

        def pos_bit(b, last):
            step = lax.shift_left(jnp.int32(1), jnp.int32(nbits - 1) - b)
            probe = last + (step - 1)
            got = count_keys(lambda s, kpos: (s == thr_f) & (kpos <= probe))
            return jnp.where(got < need, last + step, last)

        last = lax.fori_loop(0, nbits, pos_bit, jnp.zeros((1, tq), I32))

        def bias_chunk(j, carry):
            off = pl.multiple_of(j * tk, tk)
            sc = sc_ref[pl.ds(off, tk), :]
            keep = (sc > thr_f) | ((sc == thr_f) & ((row + j * tk) <= last))
            bias_ref[pl.ds(off, tk), :] = jnp.where(keep, 0.0, NEG)
            return carry

        lax.fori_loop(0, nch, bias_chunk, 0)

    def qk_chunk(j, mrun):
        off = pl.multiple_of(j * tk, tk)
        bias = bias_ref[pl.ds(off, tk), :]
        out = []
        for h in range(N_HEADS):
            g = h // GROUP
            kc = k_ref[pl.ds(off, tk), g * HEAD_DIM:(g + 1) * HEAD_DIM]
            s = jnp.dot(kc, qT_ref[h * HEAD_DIM:(h + 1) * HEAD_DIM, :], preferred_element_type=F32) + bias
            s_ref[h, pl.ds(off, tk), :] = s
            out.append(jnp.maximum(mrun[h], s.reshape(tk // 8, 8, tq).max(axis=0)))
        return tuple(out)

    mrun = lax.fori_loop(0, nch, qk_chunk, (jnp.full((8, tq), NEG, F32),) * N_HEADS)
    ms = [jnp.max(mr, axis=0, keepdims=True) for mr in mrun]
    acc_ref[...] = jnp.zeros(acc_ref.shape, F32)

    def pv_chunk(j, lrun):
        off = pl.multiple_of(j * tk, tk)
        out = []
        for h in range(N_HEADS):
            g = h // GROUP
            hs = slice(h * HEAD_DIM, (h + 1) * HEAD_DIM)
            p = jnp.exp2(s_ref[h, pl.ds(off, tk), :] - ms[h])
            out.append(lrun[h] + p.reshape(tk // 8, 8, tq).sum(axis=0))
            vc = vT_ref[g * HEAD_DIM:(g + 1) * HEAD_DIM, pl.ds(off, tk)]
            acc_ref[hs, :] += jnp.dot(vc, p.astype(BF16), preferred_element_type=F32)
        return tuple(out)

    lrun = lax.fori_loop(0, nch, pv_chunk, (jnp.zeros((8, tq), F32),) * N_HEADS)

    for h in range(N_HEADS):
        hs = slice(h * HEAD_DIM, (h + 1) * HEAD_DIM)
        o = (acc_ref[hs, :] / jnp.sum(lrun[h], axis=0, keepdims=True)).T
        o_ref[:, hs] = (o * gate_ref[:, hs]).astype(BF16)


def _page_copy(cache_hbm, page, pbuf, slot, r, sem):
    return pltpu.make_async_copy(cache_hbm.at[page], pbuf.at[slot, r], sem.at[slot])


def _attn_fused_body(nq, spq, pt_ref, *refs):
    prompt_in = refs[0:7]
    q_ref, bp_ref, bn_ref, knew_ref, vnew_ref, ck_hbm, cv_hbm = refs[7:14]
    o_ref, so_ref = refs[14:16]
    prompt_scratch = refs[16:20]
    pbuf, sem, ss_ref, ps_ref, ls_ref, accs_ref = refs[20:26]

    i = pl.program_id(1)
    step = pl.program_id(0) * nq + i
    n_total = pl.num_programs(0) * nq
    half = spq // 2
    npg = pbuf.shape[1]

    slot = step % 2

    def start_pages(seq_n, t_n, to_slot):
        cache_hbm = ck_hbm if t_n < half else cv_hbm
        base = (t_n % half) * npg
        for r in range(npg):
            _page_copy(cache_hbm, pt_ref[seq_n, base + r], pbuf, to_slot, r, sem).start()

    def start_next(t_next):
        start_pages(step // spq + t_next // spq, t_next % spq, 1 - slot)

    @pl.when(step == 0)
    def _prologue():
        start_pages(0, 0, 0)

    pltpu.make_async_copy(ck_hbm.at[pl.ds(0, npg)], pbuf.at[slot], sem.at[slot]).wait()

    _sample_attn_step(step % spq, half, slot, step + 1 < n_total, start_next,
                      q_ref, bp_ref, bn_ref, knew_ref, vnew_ref, pbuf,
                      so_ref, ss_ref, ps_ref, ls_ref, accs_ref)
    _prompt_attn_step(i, *prompt_in, o_ref, *prompt_scratch)


def _attention(qiT, wT, kibf, qT, kbf, vT, gate, n_batch, seq,
               q_rows, bp, bn, k_new, v_new, cache_k, cache_v, page_table):
    m = n_batch * seq
    nq = seq // ATT_TQ
    n_seq, n_pages = page_table.shape
    n_steps = n_batch * nq
    spq = n_steps // n_seq
    assert spq * n_seq == n_steps and spq % 2 == 0
    half = spq // 2
    npg = n_pages // half
    assert npg * half == n_pages
    width = npg * PAGE_SIZE
    past = n_pages * PAGE_SIZE
    n_pool = cache_k.shape[0]
    kv_rows = PAGE_SIZE * N_KV_HEADS
    ck = cache_k.reshape(n_pool, kv_rows, HEAD_DIM)
    cv = cache_v.reshape(n_pool, kv_rows, HEAD_DIM)
    nrow = q_rows.shape[1]
    assert (nrow // N_KV_HEADS) % 16 == 0

    qmap = lambda b, i, pt: (0, b * nq + i)
    smap = lambda b, i, pt: ((b * nq + i) // spq, 0, 0)
    return pl.pallas_call(
        functools.partial(_attn_fused_body, nq, spq),
        grid_spec=pltpu.PrefetchScalarGridSpec(
            num_scalar_prefetch=1,
            grid=(n_batch, nq),
            in_specs=[
                pl.BlockSpec((N_IDX_HEADS * IDX_DIM, ATT_TQ), qmap),
                pl.BlockSpec((N_IDX_HEADS, ATT_TQ), qmap),
                pl.BlockSpec((seq, 128), lambda b, i, pt: (b, 0)),
                pl.BlockSpec((D_ATTN, ATT_TQ), qmap),
                pl.BlockSpec((seq, N_KV), lambda b, i, pt: (b, 0)),
                pl.BlockSpec((N_KV, seq), lambda b, i, pt: (0, b)),
                pl.BlockSpec((ATT_TQ, D_ATTN), lambda b, i, pt: (b * nq + i, 0)),
                pl.BlockSpec((1, nrow, HEAD_DIM), smap),
                pl.BlockSpec((1, 8, width),
                             lambda b, i, pt: ((b * nq + i) // spq, 0, jnp.minimum((b * nq + i) % spq, half - 1))),
                pl.BlockSpec((1, 8, PAGE_SIZE), smap),
                pl.BlockSpec((1, PAGE_SIZE, N_KV), smap),
                pl.BlockSpec((1, PAGE_SIZE, N_KV), smap),
                pl.BlockSpec(memory_space=pl.ANY),
                pl.BlockSpec(memory_space=pl.ANY),
            ],
            out_specs=[
                pl.BlockSpec((ATT_TQ, D_ATTN), lambda b, i, pt: (b * nq + i, 0)),
                pl.BlockSpec((1, nrow, HEAD_DIM), smap),
            ],
            scratch_shapes=[
                pltpu.VMEM((seq, ATT_TQ), F32), pltpu.VMEM((seq, ATT_TQ), F32),
                pltpu.VMEM((D_ATTN, ATT_TQ), F32), pltpu.VMEM((N_HEADS, seq, ATT_TQ), F32),
                pltpu.VMEM((2, npg, kv_rows, HEAD_DIM), F32), pltpu.SemaphoreType.DMA((2,)),
                pltpu.VMEM((nrow, past + PAGE_SIZE), F32), pltpu.VMEM((nrow, past + PAGE_SIZE), BF16),
                pltpu.VMEM((nrow, 128), F32), pltpu.VMEM((nrow, HEAD_DIM), F32),
            ],
        ),
        out_shape=[jax.ShapeDtypeStruct((m, D_ATTN), BF16),
                   jax.ShapeDtypeStruct((n_seq, nrow, HEAD_DIM), F32)],
        compiler_params=_cparams(2),
        name="attention",
    )(page_table, qiT, wT, kibf, qT, kbf, vT, gate, q_rows, bp, bn, k_new, v_new, ck, cv)


def _s_thresh_body(n_new, in_p_ref, in_n_ref, out_p_ref, out_n_ref, sp_ref):
    past = in_p_ref.shape[1]
    rows = sp_ref.shape[0]
    pack = 8 // n_new
    groups = rows // 8
    ch = 2048
    nchunk = past // ch

    def pack_rows(src_ref, cols, ncols):
        which = lax.broadcasted_iota(I32, (8, ncols), 0) // n_new
        out = []
        for p in range(groups):
            dense = src_ref[(p * pack) * 8:(p * pack) * 8 + 8, cols]
            for q in range(1, pack):
                piece = src_ref[(p * pack + q) * 8:(p * pack + q) * 8 + 8, cols]
                dense = jnp.where(which == q, pltpu.roll(piece, q * n_new, axis=0), dense)
            out.append(dense)
        return jnp.concatenate(out, axis=0)

    def unpack_rows(dst_ref, cols, dense):
        for p in range(groups):
            piece = dense[p * 8:(p + 1) * 8]
            for q in range(pack):
                rolled = piece if q == 0 else pltpu.roll(piece, 8 - q * n_new, axis=0)
                dst_ref[(p * pack + q) * 8:(p * pack + q) * 8 + 8, cols] = rolled

    def pack_chunk(c, carry):
        cols = pl.ds(pl.multiple_of(c * ch, ch), ch)
        sp_ref[:, cols] = pack_rows(in_p_ref, cols, ch)
        return carry

    lax.fori_loop(0, nchunk, pack_chunk, 0)
    t = lax.broadcasted_iota(I32, (rows, 128), 0) % n_new
    lane = lax.broadcasted_iota(I32, (rows, 128), 1)
    sn = jnp.where((lane < n_new) & (lane <= t), pack_rows(in_n_ref, slice(0, 128), 128), -jnp.inf)

    def fold(x):
        f = x[:, 0:128]
        for q in range(1, x.shape[1] // 128):
            f = f + x[:, q * 128:(q + 1) * 128]
        return f

    lane_ch = lax.broadcasted_iota(I32, (rows, ch), 1)

    def count_keys(pred):
        def cnt_chunk(c, acc):
            off = pl.multiple_of(c * ch, ch)
            return acc + fold(jnp.where(pred(sp_ref[:, pl.ds(off, ch)], lane_ch + c * ch), 1.0, 0.0))

        acc = lax.fori_loop(0, nchunk, cnt_chunk, jnp.where(pred(sn, lane + past), 1.0, 0.0))
        return jnp.sum(acc, axis=1, keepdims=True)

    def bit_body(b, carry):
        thr, cge = carry
        cand = thr + _bit_value(b)
        cand_f = _key_to_float(cand)
        cnt = count_keys(lambda s, kpos: s >= cand_f)
        ok = cnt >= float(TOPK_MAX)
        return jnp.where(ok, cand, thr), jnp.where(ok, cnt, cge)

    thr, cge = lax.fori_loop(0, 32, bit_body, (jnp.full((rows, 1), INT_MIN, I32),
                                               jnp.zeros((rows, 1), F32)))
    thr_f = _key_to_float(jnp.maximum(thr, KEY_LOWEST_FINITE))
    has_ties = jnp.max(cge) > float(TOPK_MAX)

    @pl.when(jnp.logical_not(has_ties))
    def _plain_mask():
        def to_bias(c, carry):
            cols = pl.ds(pl.multiple_of(c * ch, ch), ch)
            unpack_rows(out_p_ref, cols, jnp.where(sp_ref[:, cols] >= thr_f, 0.0, NEG))
            return carry

        lax.fori_loop(0, nchunk, to_bias, 0)
        unpack_rows(out_n_ref, slice(0, 128), jnp.where(sn >= thr_f, 0.0, NEG))

    @pl.when(has_ties)
    def _tie_mask():
        need = float(TOPK_MAX) - count_keys(lambda s, kpos: s > thr_f)
        nbits = (past + 128 - 1).bit_length()

        def pos_bit(b, last):
            step = lax.shift_left(jnp.int32(1), jnp.int32(nbits - 1) - b)
            probe = last + (step - 1)
            got = count_keys(lambda s, kpos: (s == thr_f) & (kpos <= probe))
            return jnp.where(got < need, last + step, last)

        last = lax.fori_loop(0, nbits, pos_bit, jnp.zeros((rows, 1), I32))

        def keep(s, kpos):
            return (s > thr_f) | ((s == thr_f) & (kpos <= last))

        def to_bias(c, carry):
            cols = pl.ds(pl.multiple_of(c * ch, ch), ch)
            unpack_rows(out_p_ref, cols, jnp.where(keep(sp_ref[:, cols], lane_ch + c * ch), 0.0, NEG))
            return carry

        lax.fori_loop(0, nchunk, to_bias, 0)
        unpack_rows(out_n_ref, slice(0, 128), jnp.where(keep(sn, lane + past), 0.0, NEG))


def _sample_attn_step(t, n_steps, slot, has_next, start_next, q_ref, bp_ref, bn_ref, knew_ref, vnew_ref, pbuf,
                      out_ref, s_ref, p_ref, l_ref, acc_ref):
    npg = pbuf.shape[1]
    nt = (((1,), (1,)), ((), ()))
    rg = q_ref.shape[1] // N_KV_HEADS
    n_tok = rg // GROUP
    width = npg * PAGE_SIZE
    past = n_steps * width
    sm_chunk = 2048

    def head_rows(g):
        rows = [pbuf[slot, r, pl.ds(g, PAGE_SIZE, stride=N_KV_HEADS), :] for r in range(npg)]
        return jnp.concatenate(rows, axis=0).astype(BF16)

    def per_head(mask8):
        return jnp.concatenate([_tile_rows(mask8, n_tok)] * (rg // 8), axis=0)

    def logits(ts):
        off = ts * width
        bias = per_head(bp_ref[0])
        for g in range(N_KV_HEADS):
            qg = q_ref[0, g * rg:(g + 1) * rg, :]
            s_ref[g * rg:(g + 1) * rg, off:off + width] = lax.dot_general(
                qg, head_rows(g), nt, preferred_element_type=F32) + bias

    def softmax():
        biasn = per_head(bn_ref[0])
        for g in range(N_KV_HEADS):
            qg = q_ref[0, g * rg:(g + 1) * rg, :]
            kg = knew_ref[0, :, g * HEAD_DIM:(g + 1) * HEAD_DIM]
            s_ref[g * rg:(g + 1) * rg, past:past + PAGE_SIZE] = lax.dot_general(
                qg, kg, nt, preferred_element_type=F32) + biasn

        def fold(x, op):
            f = x[:, 0:128]
            for q in range(1, x.shape[1] // 128):
                f = op(f, x[:, q * 128:(q + 1) * 128])
            return f

        def max_chunk(c, m):
            off = pl.multiple_of(c * sm_chunk, sm_chunk)
            return jnp.maximum(m, fold(s_ref[:, pl.ds(off, sm_chunk)], jnp.maximum))

        m = lax.fori_loop(0, past // sm_chunk, max_chunk, s_ref[:, past:past + PAGE_SIZE])
        m = jnp.max(m, axis=1, keepdims=True)

        def exp_chunk(c, l):
            off = pl.multiple_of(c * sm_chunk, sm_chunk)
            p = jnp.exp2(s_ref[:, pl.ds(off, sm_chunk)] - m)
            p_ref[:, pl.ds(off, sm_chunk)] = p.astype(BF16)
            return l + fold(p, jnp.add)

        pn = jnp.exp2(s_ref[:, past:past + PAGE_SIZE] - m)
        p_ref[:, past:past + PAGE_SIZE] = pn.astype(BF16)
        l_ref[...] = lax.fori_loop(0, past // sm_chunk, exp_chunk, pn)
        acc_ref[...] = jnp.zeros(acc_ref.shape, F32)

    def values(ts):
        off = (ts - n_steps) * width
        for g in range(N_KV_HEADS):
            rs = slice(g * rg, (g + 1) * rg)
            acc_ref[rs, :] += jnp.dot(p_ref[rs, off:off + width], head_rows(g),
                                      preferred_element_type=F32)

    def finish():
        for g in range(N_KV_HEADS):
            rs = slice(g * rg, (g + 1) * rg)
            vg = vnew_ref[0, :, g * HEAD_DIM:(g + 1) * HEAD_DIM]
            acc_ref[rs, :] += jnp.dot(p_ref[rs, past:past + PAGE_SIZE], vg, preferred_element_type=F32)
        out_ref[0] = acc_ref[...] / jnp.sum(l_ref[...], axis=1, keepdims=True)

    last = 2 * n_steps - 1

    def phase(ts, start):
        if start:
            start_next(ts + 1)
        if ts < n_steps:
            logits(ts)
            if ts == n_steps - 1:
                softmax()
        else:
            values(ts)
            if ts == last:
                finish()

    for ts in range(last):
        pl.when(t == ts)(functools.partial(phase, ts, True))
    pl.when((t == last) & has_next)(functools.partial(phase, last, True))
    pl.when((t == last) & jnp.logical_not(has_next))(functools.partial(phase, last, False))


def _sample_operands(qT, qiT, wT, kibf, kbf, v32, cache_kidx, n_seq, n_tok):
    def rows_ht(xT, n_heads, dim):
        x = xT.reshape(n_heads, dim, n_seq, n_tok).transpose(2, 0, 3, 1)
        return x.reshape(n_seq, n_heads * n_tok, dim)

    qi_rows = rows_ht(qiT, N_IDX_HEADS, IDX_DIM)
    q_rows = rows_ht(qT, N_HEADS, HEAD_DIM)
    wcol = wT.reshape(N_IDX_HEADS, n_seq, n_tok).transpose(1, 0, 2).reshape(n_seq, N_IDX_HEADS * n_tok, 1)

    def pad_keys(x):
        x = x.reshape(n_seq, n_tok, x.shape[-1])
        return jnp.pad(x, ((0, 0), (0, PAGE_SIZE - n_tok), (0, 0)))

    ki_new_t = jnp.swapaxes(pad_keys(kibf[:, 0:IDX_DIM]), 1, 2)
    kidx_t = jnp.swapaxes(cache_kidx, 1, 2)
    k_new = pad_keys(kbf)
    v_new = pad_keys(v32.reshape(n_seq * n_tok, N_KV).astype(BF16))
    return (qi_rows, wcol, ki_new_t, kidx_t), (q_rows, k_new, v_new)


def _sample_topk(sp, sn, n_tok):
    n_seq, _, past = sp.shape
    rows = n_seq * 8
    assert 8 % n_tok == 0 and n_seq % (8 // n_tok) == 0
    bp, bn = pl.pallas_call(
        functools.partial(_s_thresh_body, n_tok),
        grid=(1,),
        in_specs=[pl.BlockSpec((rows, past), lambda i: (0, 0)),
                  pl.BlockSpec((rows, PAGE_SIZE), lambda i: (0, 0))],
        out_specs=[pl.BlockSpec((rows, past), lambda i: (0, 0)),
                   pl.BlockSpec((rows, PAGE_SIZE), lambda i: (0, 0))],
        out_shape=[jax.ShapeDtypeStruct((rows, past), F32),
                   jax.ShapeDtypeStruct((rows, PAGE_SIZE), F32)],
        scratch_shapes=[pltpu.VMEM((n_seq * n_tok, past), F32)],
        compiler_params=_cparams(1),
        name="sample_topk_mask",
    )(sp.reshape(rows, past), sn.reshape(rows, PAGE_SIZE))
    return bp.reshape(n_seq, 8, past), bn.reshape(n_seq, 8, PAGE_SIZE)


def _sample_rows_to_tokens(out, n_seq, n_tok):
    out = out.reshape(n_seq, N_HEADS, n_tok, HEAD_DIM)
    return out.transpose(0, 2, 1, 3).reshape(n_seq * n_tok, D_ATTN)


def _outproj_body(gated, *refs):
    if gated:
        x_ref, a_ref, gate_ref, mc_ref, wo_ref, o_ref, wo_o = refs
        ma = (a_ref[...] * gate_ref[...]).astype(BF16)
        wo_o[...] = wo_ref[...].astype(BF16)
        wo_ref = wo_o
    else:
        x_ref, a_ref, mc_ref, wo_ref, o_ref = refs
        ma = a_ref[...]
    acc = jnp.dot(ma, wo_ref[0:D_ATTN, :], preferred_element_type=F32)
    acc = acc + jnp.dot(mc_ref[...], wo_ref[D_ATTN:D_ATTN + D_CONV, :], preferred_element_type=F32)
    o_ref[...] = x_ref[...] + acc


def _outproj(x, attn, gate, mixc, wo, tm):
    m = x.shape[0]
    gated = gate is not None
    assert not gated or m == tm
    row = lambda i: (i, 0)
    in_specs = [pl.BlockSpec((tm, D_MODEL), row), pl.BlockSpec((tm, D_ATTN), row)]
    args = [x, attn]
    if gated:
        in_specs.append(pl.BlockSpec((tm, D_ATTN), row))
        args.append(gate)
    in_specs += [pl.BlockSpec((tm, D_CONV), row),
                 pl.BlockSpec((D_ATTN + D_CONV, D_MODEL), lambda i: (0, 0))]
    args += [mixc, wo]
    out_specs = [pl.BlockSpec((tm, D_MODEL), row)]
    out_shape = [jax.ShapeDtypeStruct((m, D_MODEL), F32)]
    if gated:
        out_specs.append(pl.BlockSpec((D_ATTN + D_CONV, D_MODEL), lambda i: (0, 0)))
        out_shape.append(jax.ShapeDtypeStruct((D_ATTN + D_CONV, D_MODEL), BF16))
    out = pl.pallas_call(
        functools.partial(_outproj_body, gated),
        grid=(m // tm,),
        in_specs=in_specs,
        out_specs=out_specs,
        out_shape=out_shape,
        compiler_params=_cparams(1),
        name="outproj_sample" if gated else "outproj_prompt",
    )(*args)
    return out if gated else out[0]


def _rope_tables(pos):
    posf = np.asarray(pos, np.float64)[:, None]
    n = posf.shape[0]

    def cs(half):
        inv = ROPE_THETA ** (-np.arange(half, dtype=np.float64) / half)
        ang = posf * inv[None, :]
        return np.cos(ang), np.sin(ang)

    c16, s16 = cs(ROPE_HALF)
    c8, s8 = cs(IDX_ROPE_HALF)
    one = lambda w: np.ones((n, w))
    zero = lambda w: np.zeros((n, w))
    rest = HEAD_DIM - ROPE_DIM
    k_c = np.concatenate([c16, c16, one(rest)], axis=1)
    k_sa = np.concatenate([-s16, zero(HEAD_DIM - ROPE_HALF)], axis=1)
    k_sb = np.concatenate([zero(ROPE_HALF), s16, zero(rest)], axis=1)
    tabs = dict(c16T=c16.T, s16T=s16.T, c8T=c8.T, s8T=s8.T, kC=k_c, kSa=k_sa, kSb=k_sb)
    return {name: jnp.asarray(np.ascontiguousarray(t), F32) for name, t in tabs.items()}


def _prep_weights(w_in, g_q, g_k, g_kidx, w_conv, w_out):
    assert w_in.shape == (D_MODEL, D_IN)
    wT = w_in.T
    gq = g_q.reshape(HEAD_DIM, 1)
    gk = g_k.reshape(1, HEAD_DIM)
    gki = g_kidx.reshape(IDX_DIM, 1)
    return wT, (gq, gk, gki, w_conv), w_out


def kernel(x_prompt, x_sample, cache_k, cache_v, cache_kidx, state_conv, page_table,
           norm_in, w_in, g_q, g_k, g_kidx, w_conv, w_out):
    n_b, seq, _ = x_prompt.shape
    n_s, n_t, _ = x_sample.shape
    depth = w_in.shape[0]
    past = page_table.shape[1] * PAGE_SIZE
    tabs_p = _rope_tables(np.arange(seq))
    tabs_s = _rope_tables(np.tile(past + np.arange(n_t), n_s))

    hp = x_prompt.reshape(n_b * seq, D_MODEL)
    hs = x_sample.reshape(n_s * n_t, D_MODEL)
    outs = [[] for _ in range(8)]
    for l in range(depth):
        w_t, params, wo = _prep_weights(w_in[l], g_q[l], g_k[l], g_kidx[l], w_conv[l], w_out[l])

        st = state_conv[l]
        tok = jnp.arange(n_t)
        e1 = st[:, jnp.full((n_t,), CONV_W - 2)].reshape(n_s * n_t, D_CONV)
        e2 = st[:, jnp.minimum(tok, CONV_W - 2)].reshape(n_s * n_t, D_CONV)
        (qT_s, qiT_s, _, wT_s, k32_s, kbf_s, v32_s, ki32_s, kibf_s, gate_s, mixc_s, u, *w_bf) = _project(
            hs, norm_in[l], w_t, params, tabs_s, n_t, state_rows=(e1, e2))
        idx_ops, (q_rows, k_new, v_new) = _sample_operands(
            qT_s, qiT_s, wT_s, kibf_s, kbf_s, v32_s, cache_kidx[l], n_s, n_t)
        (qT, qiT, vT, wT, k32, kbf, v32, ki32, kibf, gate, mixc, utail, sp, sn) = _project(
            hp, norm_in[l], tuple(w_bf), params, tabs_p, 0, side=(page_table,) + idx_ops)

        bp, bn = _sample_topk(sp, sn, n_t)
        mixa, attn_rows = _attention(qiT, wT, kibf, qT, kbf, vT, gate, n_b, seq,
                                     q_rows, bp, bn, k_new, v_new, cache_k[l], cache_v[l], page_table)
        attn_s = _sample_rows_to_tokens(attn_rows, n_s, n_t)

        hs, wo_bf = _outproj(hs, attn_s, gate_s, mixc_s, wo, n_s * n_t)
        hp = _outproj(hp, mixa, None, mixc, wo_bf, ROW_TM)
        tps = seq // PROJ_TM
        outs[0].append(k32.reshape(n_b, seq, N_KV_HEADS, HEAD_DIM))
        outs[1].append(v32.reshape(n_b, seq, N_KV_HEADS, HEAD_DIM))
        outs[2].append(ki32.reshape(n_b, seq, IDX_DIM))
        outs[3].append(utail[tps - 1::tps])
        outs[4].append(k32_s.reshape(n_s, n_t, N_KV_HEADS, HEAD_DIM))
        outs[5].append(v32_s.reshape(n_s, n_t, N_KV_HEADS, HEAD_DIM))
        outs[6].append(ki32_s.reshape(n_s, n_t, IDX_DIM))
        outs[7].append(u.reshape(n_s, n_t, D_CONV)[:, n_t - (CONV_W - 1):])

    return (hp.reshape(n_b, seq, D_MODEL), hs.reshape(n_s, n_t, D_MODEL),
            *[jnp.stack(o) for o in outs])
```
